```python
import math
import jax, jax.numpy as jnp
from jax import lax
import numpy as np


D_MODEL = 1024
BATCH = 16
SEQ = 4096
DEPTH = 1

SSD_EXPAND = 2
D_INNER = SSD_EXPAND * D_MODEL
SSD_HEAD_DIM = 64
SSD_HEADS = D_INNER // SSD_HEAD_DIM
SSD_GROUPS = 8
D_STATE = 128
SSD_CONV = 4
SSD_CHUNK = 128
SSD_CONV_DIM = D_INNER + 2 * SSD_GROUPS * D_STATE
SSD_NORM_GROUP = D_INNER // SSD_GROUPS
ATTN_WINDOWS = (128, 512, 2048)
ATTN_DILATIONS = (1, 4, 16)
ATTN_N_GROUPS = 3
ATTN_HEADS_PER_GROUP = 8
ATTN_HEAD_DIM = 64
ATTN_BLOCK = 128
ATTN_WIDTH = ATTN_N_GROUPS * ATTN_HEADS_PER_GROUP * ATTN_HEAD_DIM
ATTN_OUT = ATTN_HEADS_PER_GROUP * ATTN_HEAD_DIM
D_FF = 2816
FFN_CONV = 3
EPS = 1e-6
IN_WIDTHS = (D_INNER, SSD_CONV_DIM, SSD_HEADS, ATTN_WIDTH, ATTN_WIDTH, ATTN_WIDTH, D_MODEL, D_MODEL)
IN_SPLITS = tuple(sum(IN_WIDTHS[:i + 1]) for i in range(len(IN_WIDTHS) - 1))
D_IN_PROJ = sum(IN_WIDTHS)

kernel_name = 'hybrid_ssd_dilated_attn_block'


def rms_norm(x, g):
    xf = x.astype(jnp.float32)
    y = xf * lax.rsqrt(jnp.mean(xf * xf, axis=-1, keepdims=True) + EPS)
    return (y * g.astype(jnp.float32)).astype(x.dtype)


def causal_dwconv(x, w, bias):
    K = w.shape[0]
    s = x.shape[1]
    xp = jnp.pad(x, ((0, 0), (K - 1, 0), (0, 0)))
    out = bias
    for i in range(K):
        out = out + xp[:, i:i + s] * w[i]
    return out


def ssd_chunked(xs, dt, A, Bm, Cm):
    b, s, H, P = xs.shape
    G, N = Bm.shape[2], Bm.shape[3]
    K = H // G
    Q = SSD_CHUNK
    c = s // Q
    X = (xs * dt[..., None]).reshape(b, c, Q, G, K, P)
    a = (dt * A).reshape(b, c, Q, G, K).transpose(0, 1, 3, 4, 2)
    a_cs = jnp.cumsum(a, axis=-1)
    Bc = Bm.reshape(b, c, Q, G, N)
    Cc = Cm.reshape(b, c, Q, G, N)
    causal = jnp.tril(jnp.ones((Q, Q), dtype=bool))
    seg = a_cs[..., :, None] - a_cs[..., None, :]
    Ldec = jnp.exp(jnp.where(causal, seg, -jnp.inf))
    CB = jnp.einsum('bclgn,bcsgn->bcgls', Cc, Bc)
    y_diag = jnp.einsum('bcgkls,bcsgkp->bclgkp', CB[:, :, :, None] * Ldec, X)
    decay_states = jnp.exp(a_cs[..., -1:] - a_cs)
    states = jnp.einsum('bclgn,bcgkl,bclgkp->bcgkpn', Bc, decay_states, X)
    chunk_decay = jnp.exp(a_cs[..., -1])

    def step(carry, inp):
        st, dec = inp
        return carry * dec[..., None, None] + st, carry

    init = jnp.zeros((b, G, K, P, N), jnp.float32)
    _, prev = lax.scan(step, init, (jnp.moveaxis(states, 1, 0), jnp.moveaxis(chunk_decay, 1, 0)))
    prev = jnp.moveaxis(prev, 0, 1)
    y_off = jnp.einsum('bclgn,bcgkpn,bcgkl->bclgkp', Cc, prev, jnp.exp(a_cs))
    return (y_diag + y_off).reshape(b, s, H, P)


def ssd_branch(z, xbc, dt_raw, conv_w, conv_b, dt_bias, a_log, d_skip, norm_g):
    b, s, _ = z.shape
    f32 = jnp.float32
    xbc = jax.nn.silu(causal_dwconv(xbc, conv_w, conv_b))
    xs, Bm, Cm = jnp.split(xbc, (D_INNER, D_INNER + SSD_GROUPS * D_STATE), axis=-1)
    xs = xs.reshape(b, s, SSD_HEADS, SSD_HEAD_DIM).astype(f32)
    Bm = Bm.reshape(b, s, SSD_GROUPS, D_STATE).astype(f32)
    Cm = Cm.reshape(b, s, SSD_GROUPS, D_STATE).astype(f32)
    dt = jax.nn.softplus(dt_raw.astype(f32) + dt_bias.astype(f32))
    A = -jnp.exp(a_log.astype(f32))
    y = ssd_chunked(xs, dt, A, Bm, Cm) + d_skip.astype(f32)[:, None] * xs
    y = y.reshape(b, s, D_INNER) * jax.nn.silu(z.astype(f32))
    y = rms_norm(y.reshape(b, s, SSD_GROUPS, SSD_NORM_GROUP), norm_g.reshape(SSD_GROUPS, SSD_NORM_GROUP))
    return y.reshape(b, s, D_INNER).astype(z.dtype)


def dilated_window_attention(q, k, v, dil, n_back):
    b, s, h, hd = q.shape
    L = s // dil
    nb = -(-L // ATTN_BLOCK)
    Lp = nb * ATTN_BLOCK

    def to_sub(t):
        t = t.reshape(b, L, dil, h, hd).transpose(0, 2, 1, 3, 4).reshape(b * dil, L, h, hd)
        t = jnp.pad(t, ((0, 0), (0, Lp - L), (0, 0), (0, 0)))
        return t.reshape(b * dil, nb, ATTN_BLOCK, h, hd)

    def with_prev(t):
        prev = jnp.pad(t, ((0, 0), (1, 0), (0, 0), (0, 0), (0, 0)))[:, :-1]
        return jnp.concatenate([prev, t], axis=2)

    qb = to_sub(q)
    kc = with_prev(to_sub(k))
    vc = with_prev(to_sub(v))
    scores = jnp.einsum('znqhd,znkhd->znhqk', qb, kc) * (hd ** -0.5)
    qi = jnp.arange(ATTN_BLOCK)[:, None]
    ki = jnp.arange(2 * ATTN_BLOCK)[None, :]
    dist = ATTN_BLOCK + qi - ki
    band = (dist >= 0) & (dist <= n_back)
    key_pos = (jnp.arange(nb)[:, None, None] - 1) * ATTN_BLOCK + ki[None]
    mask = band[None] & (key_pos >= 0)
    scores = jnp.where(mask[None, :, None], scores, -jnp.inf)
    m = jnp.max(scores, axis=-1, keepdims=True)
    p = jnp.exp(scores - m)
    den = jnp.sum(p, axis=-1, keepdims=True)
    o = jnp.einsum('znhqk,znkhd->znqhd', p / den, vc)
    lse = (m + jnp.log(den))[..., 0]
    o = o.reshape(b, dil, Lp, h, hd)[:, :, :L].transpose(0, 2, 1, 3, 4).reshape(b, s, h, hd)
    lse = lse.transpose(0, 1, 3, 2).reshape(b, dil, Lp, h)[:, :, :L].transpose(0, 2, 1, 3).reshape(b, s, h)
    return o, lse


def attn_branch(q, k, v, q_norm_g, k_norm_g):
    b, s, _ = q.shape
    f32 = jnp.float32
    shp = (b, s, ATTN_N_GROUPS, ATTN_HEADS_PER_GROUP, ATTN_HEAD_DIM)
    qn = rms_norm(q.reshape(shp), q_norm_g).astype(f32)
    kn = rms_norm(k.reshape(shp), k_norm_g).astype(f32)
    vv = v.reshape(shp).astype(f32)
    outs, lses = [], []
    for gi in range(ATTN_N_GROUPS):
        o, lse = dilated_window_attention(qn[:, :, gi], kn[:, :, gi], vv[:, :, gi],
                                          ATTN_DILATIONS[gi], ATTN_WINDOWS[gi] // ATTN_DILATIONS[gi])
        outs.append(o)
        lses.append(lse)
    wts = jax.nn.softmax(jnp.stack(lses, axis=0), axis=0)
    o = jnp.einsum('gbsh,gbshd->bshd', wts, jnp.stack(outs, axis=0))
    return o.reshape(b, s, ATTN_OUT).astype(q.dtype)


def _fwd_setup_inputs(seed: int = 0) -> dict:
    key = jax.random.key(seed)
    ks = jax.random.split(key, 20)
    f32 = jnp.float32

    def dense(k, shape, fan_in):
        return jax.random.normal(k, shape, f32) * fan_in ** -0.5

    def gain(k, shape):
        return 1.0 + 0.05 * jax.random.normal(k, shape, f32)

    x = jax.random.normal(ks[0], (BATCH, SEQ, D_MODEL), f32)
    norm1_g = gain(ks[1], (DEPTH, D_MODEL))
    w_in = dense(ks[2], (DEPTH, D_MODEL, D_IN_PROJ), D_MODEL)
    ssd_conv_w = dense(ks[3], (DEPTH, SSD_CONV, SSD_CONV_DIM), SSD_CONV)
    ssd_conv_b = 0.02 * jax.random.normal(ks[4], (DEPTH, SSD_CONV_DIM), f32)
    dt0 = jnp.exp(jax.random.uniform(ks[5], (DEPTH, SSD_HEADS), f32, math.log(1e-3), math.log(1e-1)))
    dt_bias = dt0 + jnp.log(-jnp.expm1(-dt0))
    a_log = jnp.log(jax.random.uniform(ks[6], (DEPTH, SSD_HEADS), f32, 1.0, 16.0))
    d_skip = 1.0 + 0.1 * jax.random.normal(ks[7], (DEPTH, SSD_HEADS), f32)
    ssd_norm_g = gain(ks[8], (DEPTH, D_INNER))
    w_ssd_proj = dense(ks[9], (DEPTH, D_INNER, D_MODEL), D_INNER)
    q_norm_g = gain(ks[10], (DEPTH, ATTN_HEAD_DIM))
    k_norm_g = gain(ks[11], (DEPTH, ATTN_HEAD_DIM))
    w_attn_proj = dense(ks[12], (DEPTH, ATTN_OUT, D_MODEL), ATTN_OUT)
    w_out = dense(ks[13], (DEPTH, D_MODEL, D_MODEL), D_MODEL)
    norm2_g = gain(ks[14], (DEPTH, D_MODEL))
    w_up = dense(ks[15], (DEPTH, D_MODEL, 2 * D_FF), D_MODEL)
    ffn_conv_w = dense(ks[16], (DEPTH, FFN_CONV, 2 * D_FF), FFN_CONV)
    ffn_conv_b = 0.02 * jax.random.normal(ks[17], (DEPTH, 2 * D_FF), f32)
    w_down = dense(ks[18], (DEPTH, D_FF, D_MODEL), D_FF)
    return {'x': x, 'norm1_g': norm1_g, 'w_in': w_in, 'ssd_conv_w': ssd_conv_w, 'ssd_conv_b': ssd_conv_b,
            'dt_bias': dt_bias, 'a_log': a_log, 'd_skip': d_skip, 'ssd_norm_g': ssd_norm_g,
            'w_ssd_proj': w_ssd_proj, 'q_norm_g': q_norm_g, 'k_norm_g': k_norm_g,
            'w_attn_proj': w_attn_proj, 'w_out': w_out, 'norm2_g': norm2_g, 'w_up': w_up,
            'ffn_conv_w': ffn_conv_w, 'ffn_conv_b': ffn_conv_b, 'w_down': w_down}


def _fwd_reference(x, norm1_g, w_in, ssd_conv_w, ssd_conv_b, dt_bias, a_log, d_skip, ssd_norm_g,
              w_ssd_proj, q_norm_g, k_norm_g, w_attn_proj, w_out, norm2_g, w_up,
              ffn_conv_w, ffn_conv_b, w_down):
    for l in range(DEPTH):
        h = rms_norm(x, norm1_g[l])
        proj = h @ w_in[l]
        z, xbc, dt_raw, q, k, v, g_ssd, g_attn = jnp.split(proj, IN_SPLITS, axis=-1)
        y_ssd = ssd_branch(z, xbc, dt_raw, ssd_conv_w[l], ssd_conv_b[l], dt_bias[l], a_log[l],
                           d_skip[l], ssd_norm_g[l])
        y_attn = attn_branch(q, k, v, q_norm_g[l], k_norm_g[l])
        merged = (jax.nn.sigmoid(g_ssd) * (y_ssd @ w_ssd_proj[l])
                  + jax.nn.sigmoid(g_attn) * (y_attn @ w_attn_proj[l]))
        x = x + merged @ w_out[l]
        h2 = rms_norm(x, norm2_g[l])
        u = causal_dwconv(h2 @ w_up[l], ffn_conv_w[l], ffn_conv_b[l])
        u_gate, u_val = jnp.split(u, 2, axis=-1)
        x = x + (jax.nn.silu(u_gate) * u_val) @ w_down[l]
    return x


import jax as _jax
import jax.numpy as _jnp

TWIN_FORMAT = 'train_step'
FWD_PARAMS = ['x', 'norm1_g', 'w_in', 'ssd_conv_w', 'ssd_conv_b', 'dt_bias', 'a_log', 'd_skip', 'ssd_norm_g', 'w_ssd_proj', 'q_norm_g', 'k_norm_g', 'w_attn_proj', 'w_out', 'norm2_g', 'w_up', 'ffn_conv_w', 'ffn_conv_b', 'w_down']
TWIN_WEIGHTS = ['norm1_g', 'w_in', 'ssd_conv_w', 'ssd_conv_b', 'dt_bias', 'a_log', 'd_skip', 'ssd_norm_g', 'w_ssd_proj', 'q_norm_g', 'k_norm_g', 'w_attn_proj', 'w_out', 'norm2_g', 'w_up', 'ffn_conv_w', 'ffn_conv_b', 'w_down']
TWIN_DIFF_INPUT = 'x'
TWIN_INPUTS = ['x', 'norm1_g', 'w_in', 'ssd_conv_w', 'ssd_conv_b', 'dt_bias', 'a_log', 'd_skip', 'ssd_norm_g', 'w_ssd_proj', 'q_norm_g', 'k_norm_g', 'w_attn_proj', 'w_out', 'norm2_g', 'w_up', 'ffn_conv_w', 'ffn_conv_b', 'w_down', 'loss_target', 'm_norm1_g', 'm_w_in', 'm_ssd_conv_w', 'm_ssd_conv_b', 'm_dt_bias', 'm_a_log', 'm_d_skip', 'm_ssd_norm_g', 'm_w_ssd_proj', 'm_q_norm_g', 'm_k_norm_g', 'm_w_attn_proj', 'm_w_out', 'm_norm2_g', 'm_w_up', 'm_ffn_conv_w', 'm_ffn_conv_b', 'm_w_down', 'v_norm1_g', 'v_w_in', 'v_ssd_conv_w', 'v_ssd_conv_b', 'v_dt_bias', 'v_a_log', 'v_d_skip', 'v_ssd_norm_g', 'v_w_ssd_proj', 'v_q_norm_g', 'v_k_norm_g', 'v_w_attn_proj', 'v_w_out', 'v_norm2_g', 'v_w_up', 'v_ffn_conv_w', 'v_ffn_conv_b', 'v_w_down']
TWIN_OUTPUTS = ['loss', 'grad_x', 'grad_norm1_g', 'grad_w_in', 'grad_ssd_conv_w', 'grad_ssd_conv_b', 'grad_dt_bias', 'grad_a_log', 'grad_d_skip', 'grad_ssd_norm_g', 'grad_w_ssd_proj', 'grad_q_norm_g', 'grad_k_norm_g', 'grad_w_attn_proj', 'grad_w_out', 'grad_norm2_g', 'grad_w_up', 'grad_ffn_conv_w', 'grad_ffn_conv_b', 'grad_w_down', 'delta_norm1_g', 'delta_w_in', 'delta_ssd_conv_w', 'delta_ssd_conv_b', 'delta_dt_bias', 'delta_a_log', 'delta_d_skip', 'delta_ssd_norm_g', 'delta_w_ssd_proj', 'delta_q_norm_g', 'delta_k_norm_g', 'delta_w_attn_proj', 'delta_w_out', 'delta_norm2_g', 'delta_w_up', 'delta_ffn_conv_w', 'delta_ffn_conv_b', 'delta_w_down', 'new_m_norm1_g', 'new_m_w_in', 'new_m_ssd_conv_w', 'new_m_ssd_conv_b', 'new_m_dt_bias', 'new_m_a_log', 'new_m_d_skip', 'new_m_ssd_norm_g', 'new_m_w_ssd_proj', 'new_m_q_norm_g', 'new_m_k_norm_g', 'new_m_w_attn_proj', 'new_m_w_out', 'new_m_norm2_g', 'new_m_w_up', 'new_m_ffn_conv_w', 'new_m_ffn_conv_b', 'new_m_w_down', 'new_v_norm1_g', 'new_v_w_in', 'new_v_ssd_conv_w', 'new_v_ssd_conv_b', 'new_v_dt_bias', 'new_v_a_log', 'new_v_d_skip', 'new_v_ssd_norm_g', 'new_v_w_ssd_proj', 'new_v_q_norm_g', 'new_v_k_norm_g', 'new_v_w_attn_proj', 'new_v_w_out', 'new_v_norm2_g', 'new_v_w_up', 'new_v_ffn_conv_w', 'new_v_ffn_conv_b', 'new_v_w_down']
TWIN_LEAF_KINDS = {'loss': 'loss', 'grad_x': 'grad_x', 'grad_norm1_g': 'grad_w', 'grad_w_in': 'grad_w', 'grad_ssd_conv_w': 'grad_w', 'grad_ssd_conv_b': 'grad_w', 'grad_dt_bias': 'grad_w', 'grad_a_log': 'grad_w', 'grad_d_skip': 'grad_w', 'grad_ssd_norm_g': 'grad_w', 'grad_w_ssd_proj': 'grad_w', 'grad_q_norm_g': 'grad_w', 'grad_k_norm_g': 'grad_w', 'grad_w_attn_proj': 'grad_w', 'grad_w_out': 'grad_w', 'grad_norm2_g': 'grad_w', 'grad_w_up': 'grad_w', 'grad_ffn_conv_w': 'grad_w', 'grad_ffn_conv_b': 'grad_w', 'grad_w_down': 'grad_w', 'delta_norm1_g': 'delta_w', 'delta_w_in': 'delta_w', 'delta_ssd_conv_w': 'delta_w', 'delta_ssd_conv_b': 'delta_w', 'delta_dt_bias': 'delta_w', 'delta_a_log': 'delta_w', 'delta_d_skip': 'delta_w', 'delta_ssd_norm_g': 'delta_w', 'delta_w_ssd_proj': 'delta_w', 'delta_q_norm_g': 'delta_w', 'delta_k_norm_g': 'delta_w', 'delta_w_attn_proj': 'delta_w', 'delta_w_out': 'delta_w', 'delta_norm2_g': 'delta_w', 'delta_w_up': 'delta_w', 'delta_ffn_conv_w': 'delta_w', 'delta_ffn_conv_b': 'delta_w', 'delta_w_down': 'delta_w', 'new_m_norm1_g': 'new_m', 'new_m_w_in': 'new_m', 'new_m_ssd_conv_w': 'new_m', 'new_m_ssd_conv_b': 'new_m', 'new_m_dt_bias': 'new_m', 'new_m_a_log': 'new_m', 'new_m_d_skip': 'new_m', 'new_m_ssd_norm_g': 'new_m', 'new_m_w_ssd_proj': 'new_m', 'new_m_q_norm_g': 'new_m', 'new_m_k_norm_g': 'new_m', 'new_m_w_attn_proj': 'new_m', 'new_m_w_out': 'new_m', 'new_m_norm2_g': 'new_m', 'new_m_w_up': 'new_m', 'new_m_ffn_conv_w': 'new_m', 'new_m_ffn_conv_b': 'new_m', 'new_m_w_down': 'new_m', 'new_v_norm1_g': 'new_v', 'new_v_w_in': 'new_v', 'new_v_ssd_conv_w': 'new_v', 'new_v_ssd_conv_b': 'new_v', 'new_v_dt_bias': 'new_v', 'new_v_a_log': 'new_v', 'new_v_d_skip': 'new_v', 'new_v_ssd_norm_g': 'new_v', 'new_v_w_ssd_proj': 'new_v', 'new_v_q_norm_g': 'new_v', 'new_v_k_norm_g': 'new_v', 'new_v_w_attn_proj': 'new_v', 'new_v_w_out': 'new_v', 'new_v_norm2_g': 'new_v', 'new_v_w_up': 'new_v', 'new_v_ffn_conv_w': 'new_v', 'new_v_ffn_conv_b': 'new_v', 'new_v_w_down': 'new_v'}


def _forward(args):
    return _fwd_reference(*[args[k] for k in FWD_PARAMS])


def _output_shape():
    out = _jax.eval_shape(lambda: _forward(_fwd_setup_inputs(0)))
    return out.shape, out.dtype

N_MICROBATCH = 1
ADAM_LR = 0.001
ADAM_B1 = 0.9
ADAM_B2 = 0.999
ADAM_EPS = 1e-08
ADAM_WD = 0.01
ADAM_STEP = 10
PER_EXAMPLE_BATCH_AXIS = {'x': 0, 'loss_target': 0}
SHARED_INPUTS = []
_WEIGHT_DTYPES = {'norm1_g': _jnp.float32, 'w_in': _jnp.float32, 'ssd_conv_w': _jnp.float32, 'ssd_conv_b': _jnp.float32, 'dt_bias': _jnp.float32, 'a_log': _jnp.float32, 'd_skip': _jnp.float32, 'ssd_norm_g': _jnp.float32, 'w_ssd_proj': _jnp.float32, 'q_norm_g': _jnp.float32, 'k_norm_g': _jnp.float32, 'w_attn_proj': _jnp.float32, 'w_out': _jnp.float32, 'norm2_g': _jnp.float32, 'w_up': _jnp.float32, 'ffn_conv_w': _jnp.float32, 'ffn_conv_b': _jnp.float32, 'w_down': _jnp.float32}
MOMENT_SCALE = {'norm1_g': 1.144969e+00, 'w_in': 1.679070e-01, 'ssd_conv_w': 4.889685e-01, 'ssd_conv_b': 2.053235e+00, 'dt_bias': 6.103580e-01, 'a_log': 4.973021e+00, 'd_skip': 4.080255e+00, 'ssd_norm_g': 1.215782e+01, 'w_ssd_proj': 1.424862e+00, 'q_norm_g': 8.972635e-01, 'k_norm_g': 9.027455e-01, 'w_attn_proj': 7.606192e-02, 'w_out': 1.334578e+00, 'norm2_g': 5.169736e+01, 'w_up': 5.350610e-01, 'ffn_conv_w': 7.157305e+00, 'ffn_conv_b': 6.438437e+00, 'w_down': 4.932462e-01}


def _to_microbatches(a, axis):
    t = _jnp.moveaxis(a, axis, 0)
    t = t.reshape((N_MICROBATCH, t.shape[0] // N_MICROBATCH) + t.shape[1:])
    return _jnp.moveaxis(t, 1, axis + 1)


def setup_inputs(seed: int = 0) -> dict:
    inp = _fwd_setup_inputs(seed)
    key = _jax.random.fold_in(_jax.random.key(seed), 7919)
    shape, _ = _output_shape()
    out = dict(inp)
    out["loss_target"] = _jax.random.normal(_jax.random.fold_in(key, 0), shape, _jnp.float32)
    for i, name in enumerate(TWIN_WEIGHTS):
        w = inp[name].astype(_jnp.float32)
        if MOMENT_SCALE is None:
            s = _jnp.sqrt(_jnp.mean(_jnp.square(w)) + 1e-30)
        else:
            s = MOMENT_SCALE[name]
        km, kv = _jax.random.split(_jax.random.fold_in(key, i + 1))
        out[name] = w
        out["m_" + name] = s * _jax.random.normal(km, w.shape, _jnp.float32)
        out["v_" + name] = (s * s) * _jax.random.uniform(kv, w.shape, _jnp.float32, 0.5, 1.5)
    if N_MICROBATCH > 1:
        for name, axis in PER_EXAMPLE_BATCH_AXIS.items():
            out[name] = _to_microbatches(out[name], axis)
    return {'x': out['x'], 'norm1_g': out['norm1_g'], 'w_in': out['w_in'], 'ssd_conv_w': out['ssd_conv_w'], 'ssd_conv_b': out['ssd_conv_b'], 'dt_bias': out['dt_bias'], 'a_log': out['a_log'], 'd_skip': out['d_skip'], 'ssd_norm_g': out['ssd_norm_g'], 'w_ssd_proj': out['w_ssd_proj'], 'q_norm_g': out['q_norm_g'], 'k_norm_g': out['k_norm_g'], 'w_attn_proj': out['w_attn_proj'], 'w_out': out['w_out'], 'norm2_g': out['norm2_g'], 'w_up': out['w_up'], 'ffn_conv_w': out['ffn_conv_w'], 'ffn_conv_b': out['ffn_conv_b'], 'w_down': out['w_down'], 'loss_target': out['loss_target'], 'm_norm1_g': out['m_norm1_g'], 'm_w_in': out['m_w_in'], 'm_ssd_conv_w': out['m_ssd_conv_w'], 'm_ssd_conv_b': out['m_ssd_conv_b'], 'm_dt_bias': out['m_dt_bias'], 'm_a_log': out['m_a_log'], 'm_d_skip': out['m_d_skip'], 'm_ssd_norm_g': out['m_ssd_norm_g'], 'm_w_ssd_proj': out['m_w_ssd_proj'], 'm_q_norm_g': out['m_q_norm_g'], 'm_k_norm_g': out['m_k_norm_g'], 'm_w_attn_proj': out['m_w_attn_proj'], 'm_w_out': out['m_w_out'], 'm_norm2_g': out['m_norm2_g'], 'm_w_up': out['m_w_up'], 'm_ffn_conv_w': out['m_ffn_conv_w'], 'm_ffn_conv_b': out['m_ffn_conv_b'], 'm_w_down': out['m_w_down'], 'v_norm1_g': out['v_norm1_g'], 'v_w_in': out['v_w_in'], 'v_ssd_conv_w': out['v_ssd_conv_w'], 'v_ssd_conv_b': out['v_ssd_conv_b'], 'v_dt_bias': out['v_dt_bias'], 'v_a_log': out['v_a_log'], 'v_d_skip': out['v_d_skip'], 'v_ssd_norm_g': out['v_ssd_norm_g'], 'v_w_ssd_proj': out['v_w_ssd_proj'], 'v_q_norm_g': out['v_q_norm_g'], 'v_k_norm_g': out['v_k_norm_g'], 'v_w_attn_proj': out['v_w_attn_proj'], 'v_w_out': out['v_w_out'], 'v_norm2_g': out['v_norm2_g'], 'v_w_up': out['v_w_up'], 'v_ffn_conv_w': out['v_ffn_conv_w'], 'v_ffn_conv_b': out['v_ffn_conv_b'], 'v_w_down': out['v_w_down']}


def _loss(weights, diff, rest, loss_target):
    with _jax.named_scope("forward"):
        args = {**rest, TWIN_DIFF_INPUT: diff, **{k: w.astype(_WEIGHT_DTYPES[k]) for k, w in weights.items()}}
        y = _forward(args)
    with _jax.named_scope("loss_head"):
        err = _jnp.square(y.astype(_jnp.float32) - loss_target)
        return 0.5 * _jnp.sum(_jnp.mean(err, axis=-1)) if err.ndim else 0.5 * err


def _adamw(w, g, m, v):
    m = ADAM_B1 * m + (1.0 - ADAM_B1) * g
    v = ADAM_B2 * v + (1.0 - ADAM_B2) * _jnp.square(g)
    m_hat = m / (1.0 - ADAM_B1 ** ADAM_STEP)
    v_hat = v / (1.0 - ADAM_B2 ** ADAM_STEP)
    delta = -ADAM_LR * (m_hat / (_jnp.sqrt(v_hat) + ADAM_EPS) + ADAM_WD * w)
    return delta, m, v


def reference(x, norm1_g, w_in, ssd_conv_w, ssd_conv_b, dt_bias, a_log, d_skip, ssd_norm_g, w_ssd_proj, q_norm_g, k_norm_g, w_attn_proj, w_out, norm2_g, w_up, ffn_conv_w, ffn_conv_b, w_down, loss_target, m_norm1_g, m_w_in, m_ssd_conv_w, m_ssd_conv_b, m_dt_bias, m_a_log, m_d_skip, m_ssd_norm_g, m_w_ssd_proj, m_q_norm_g, m_k_norm_g, m_w_attn_proj, m_w_out, m_norm2_g, m_w_up, m_ffn_conv_w, m_ffn_conv_b, m_w_down, v_norm1_g, v_w_in, v_ssd_conv_w, v_ssd_conv_b, v_dt_bias, v_a_log, v_d_skip, v_ssd_norm_g, v_w_ssd_proj, v_q_norm_g, v_k_norm_g, v_w_attn_proj, v_w_out, v_norm2_g, v_w_up, v_ffn_conv_w, v_ffn_conv_b, v_w_down):
    given = dict(x=x, norm1_g=norm1_g, w_in=w_in, ssd_conv_w=ssd_conv_w, ssd_conv_b=ssd_conv_b, dt_bias=dt_bias, a_log=a_log, d_skip=d_skip, ssd_norm_g=ssd_norm_g, w_ssd_proj=w_ssd_proj, q_norm_g=q_norm_g, k_norm_g=k_norm_g, w_attn_proj=w_attn_proj, w_out=w_out, norm2_g=norm2_g, w_up=w_up, ffn_conv_w=ffn_conv_w, ffn_conv_b=ffn_conv_b, w_down=w_down, loss_target=loss_target, m_norm1_g=m_norm1_g, m_w_in=m_w_in, m_ssd_conv_w=m_ssd_conv_w, m_ssd_conv_b=m_ssd_conv_b, m_dt_bias=m_dt_bias, m_a_log=m_a_log, m_d_skip=m_d_skip, m_ssd_norm_g=m_ssd_norm_g, m_w_ssd_proj=m_w_ssd_proj, m_q_norm_g=m_q_norm_g, m_k_norm_g=m_k_norm_g, m_w_attn_proj=m_w_attn_proj, m_w_out=m_w_out, m_norm2_g=m_norm2_g, m_w_up=m_w_up, m_ffn_conv_w=m_ffn_conv_w, m_ffn_conv_b=m_ffn_conv_b, m_w_down=m_w_down, v_norm1_g=v_norm1_g, v_w_in=v_w_in, v_ssd_conv_w=v_ssd_conv_w, v_ssd_conv_b=v_ssd_conv_b, v_dt_bias=v_dt_bias, v_a_log=v_a_log, v_d_skip=v_d_skip, v_ssd_norm_g=v_ssd_norm_g, v_w_ssd_proj=v_w_ssd_proj, v_q_norm_g=v_q_norm_g, v_k_norm_g=v_k_norm_g, v_w_attn_proj=v_w_attn_proj, v_w_out=v_w_out, v_norm2_g=v_norm2_g, v_w_up=v_w_up, v_ffn_conv_w=v_ffn_conv_w, v_ffn_conv_b=v_ffn_conv_b, v_w_down=v_w_down)
    weights = {n: given[n] for n in TWIN_WEIGHTS}
    shared = {n: given[n] for n in SHARED_INPUTS}
    per_example = {n: given[n] for n in ['x']}
    grad_fn = _jax.value_and_grad(_loss, argnums=(0, 1))

    def one_microbatch(ex, loss_target):
        ex = dict(ex)
        diff = ex.pop(TWIN_DIFF_INPUT)
        return grad_fn(weights, diff, {**shared, **ex}, loss_target)

    if N_MICROBATCH == 1:
        loss, (grad_w, grad_x) = one_microbatch(per_example, given["loss_target"])
    else:
        def body(carry, xs):
            loss_sum, grad_sum = carry
            l_k, (gw_k, gx_k) = one_microbatch(xs[0], xs[1])
            with _jax.named_scope("update"):
                return (loss_sum + l_k, _jax.tree.map(_jnp.add, grad_sum, gw_k)), gx_k

        init = (_jnp.zeros((), _jnp.float32), _jax.tree.map(_jnp.zeros_like, weights))
        (loss, grad_w), grad_x = _jax.lax.scan(body, init, (per_example, given["loss_target"]))
    with _jax.named_scope("update"):
        delta_w, new_m, new_v = {}, {}, {}
        for n in TWIN_WEIGHTS:
            delta_w[n], new_m[n], new_v[n] = _adamw(weights[n], grad_w[n], given["m_" + n], given["v_" + n])
    return (loss, grad_x, *[grad_w[n] for n in TWIN_WEIGHTS], *[delta_w[n] for n in TWIN_WEIGHTS],
            *[new_m[n] for n in TWIN_WEIGHTS], *[new_v[n] for n in TWIN_WEIGHTS])
```

```python
import jax
import jax.numpy as jnp
from jax import lax
from jax.experimental import pallas as pl
from jax.experimental.pallas import tpu as pltpu

F32 = jnp.float32
BF16 = jnp.bfloat16
MXU = jnp.bfloat16
HIGHEST = lax.Precision.HIGHEST
VMEM_LIMIT_BYTES = 48 * 1024 * 1024
SUBLANES = 8
LANES = 128
N_DEV = 8

D_MODEL = 1024
D_INNER = 2048
SSD_P = 64
SSD_H = 32
SSD_G = 8
SSD_K = SSD_H // SSD_G
SSD_N = 128
SSD_Q = 128
SSD_CONV = 4
CONV_DIM = D_INNER + 2 * SSD_G * SSD_N
NORM_GROUP = D_INNER // SSD_G
ATT_GROUPS = 3
ATT_H = 8
ATT_HD = 64
ATT_BLK = 128
ATT_OUT = ATT_H * ATT_HD
ATT_DILATIONS = (1, 4, 16)
ATT_SCALE = ATT_HD ** -0.5
D_FF = 2816
FFN_CONV = 3
EPS = 1e-6
NEG = -1e30
IN_WIDTHS = (D_INNER, CONV_DIM, SSD_H, 3 * ATT_OUT, 3 * ATT_OUT, 3 * ATT_OUT, D_MODEL, D_MODEL)

ADAM_LR = 0.001
ADAM_B1 = 0.9
ADAM_B2 = 0.999
ADAM_EPS = 1e-08
ADAM_WD = 0.01
ADAM_STEP = 10


def _mm(a, b, dims):
    return lax.dot_general(a.astype(MXU), b.astype(MXU), (dims, ((), ())), preferred_element_type=F32)


def _dot_nn(a, b):
    return _mm(a, b, ((1,), (0,)))


def _dot_nt(a, b):
    return _mm(a, b, ((1,), (1,)))


def _dot_tn(a, b):
    return _mm(a, b, ((0,), (0,)))


def _dot_f32(a, b):
    return lax.dot_general(a, b, (((1,), (0,)), ((), ())), precision=HIGHEST, preferred_element_type=F32)


def _sigmoid(x):
    return 1.0 / (1.0 + jnp.exp(-x))


def _silu(x):
    return x * _sigmoid(x)


def _silu_grad(x):
    s = _sigmoid(x)
    return s * (1.0 + x * (1.0 - s))


def _softplus(x):
    return jnp.maximum(x, 0.0) + jnp.log(1.0 + jnp.exp(-jnp.abs(x)))


def _rms_fwd(x, g):
    r = lax.rsqrt(jnp.mean(x * x, axis=-1, keepdims=True) + EPS)
    return x * r * g


def _rms_bwd(x, g, dy):
    r = lax.rsqrt(jnp.mean(x * x, axis=-1, keepdims=True) + EPS)
    xh = x * r
    dyg = dy * g
    dx = r * (dyg - xh * jnp.mean(dyg * xh, axis=-1, keepdims=True))
    return dx, jnp.sum(dy * xh, axis=0, keepdims=True)


def _onehot_row(h, n=LANES):
    return (lax.broadcasted_iota(jnp.int32, (1, n), 1) == h).astype(F32)


def _onehot_col(h, n=LANES):
    return (lax.broadcasted_iota(jnp.int32, (n, 1), 0) == h).astype(F32)


def _head_expand_matrix():
    r = lax.broadcasted_iota(jnp.int32, (LANES, ATT_OUT), 0)
    c = lax.broadcasted_iota(jnp.int32, (LANES, ATT_OUT), 1)
    return (c // ATT_HD == r).astype(F32)


def _split_bf16(x, parts):
    out = []
    for _ in range(parts - 1):
        hi = x.astype(BF16).astype(F32)
        out.append(hi)
        x = x - hi
    out.append(x)
    return out


def _expand_heads(w):
    e = _head_expand_matrix()
    return sum(_dot_nn(p, e) for p in _split_bf16(w, 2))


def _reduce_heads(x):
    e = _head_expand_matrix()
    return sum(_dot_nt(p, e) for p in _split_bf16(x, 3))


def _shift_prev(cur, halo, s, first):
    if s == 0:
        return cur
    tb, w = cur.shape
    rolled = pltpu.roll(cur, s, 0)
    hr = jnp.where(first, 0.0, pltpu.roll(halo, s, 0))
    fix = jnp.concatenate([hr, jnp.zeros((tb - SUBLANES, w), cur.dtype)], axis=0)
    rows = lax.broadcasted_iota(jnp.int32, cur.shape, 0)
    return jnp.where(rows < s, fix, rolled)


def _shift_next(cur, halo, s, last):
    if s == 0:
        return cur
    tb, w = cur.shape
    rolled = pltpu.roll(cur, tb - s, 0)
    hr = jnp.where(last, 0.0, pltpu.roll(halo, SUBLANES - s, 0))
    fix = jnp.concatenate([jnp.zeros((tb - SUBLANES, w), cur.dtype), hr], axis=0)
    rows = lax.broadcasted_iota(jnp.int32, cur.shape, 0)
    return jnp.where(rows >= tb - s, fix, rolled)


def _conv_prev(x, halo, w, first, taps):
    acc = None
    for i in range(taps):
        term = w[i:i + 1, :] * _shift_prev(x, halo, taps - 1 - i, first)
        acc = term if acc is None else acc + term
    return acc


def _params(sem):
    return pltpu.CompilerParams(dimension_semantics=sem, vmem_limit_bytes=VMEM_LIMIT_BYTES)


def _pick(dim, target):
    if dim <= target:
        return dim
    best = None
    for t in range(LANES, target + 1, LANES):
        if dim % t == 0:
            best = t
    assert best is not None, (dim, target)
    return best


def matmul(a, b, name, ta=False, tb=False, add=None, out_dtype=F32, tm=512, tn=1536, tk=1024):
    assert not (ta and tb)
    m, k = (a.shape[1], a.shape[0]) if ta else a.shape
    n = b.shape[0] if tb else b.shape[1]
    assert (b.shape[1] if tb else b.shape[0]) == k
    tm, tn, tk = _pick(m, tm), _pick(n, tn), _pick(k, tk)
    nk = k // tk
    dims = ((0,), (0,)) if ta else (((1,), (1,)) if tb else ((1,), (0,)))

    def body(*refs):
        if add is None:
            a_ref, b_ref, o_ref, acc = refs
        else:
            a_ref, b_ref, add_ref, o_ref, acc = refs
        kk = pl.program_id(2)

        @pl.when(kk == 0)
        def _():
            acc[...] = jnp.zeros_like(acc)

        acc[...] += _mm(a_ref[...], b_ref[...], dims)

        @pl.when(kk == nk - 1)
        def _():
            r = acc[...]
            if add is not None:
                r = r + add_ref[...].astype(F32)
            o_ref[...] = r.astype(out_dtype)

    a_spec = pl.BlockSpec((tk, tm), lambda i, j, kk: (kk, i)) if ta else pl.BlockSpec((tm, tk), lambda i, j, kk: (i, kk))
    b_spec = pl.BlockSpec((tn, tk), lambda i, j, kk: (j, kk)) if tb else pl.BlockSpec((tk, tn), lambda i, j, kk: (kk, j))
    in_specs = [a_spec, b_spec]
    args = [a, b]
    if add is not None:
        in_specs.append(pl.BlockSpec((tm, tn), lambda i, j, kk: (i, j)))
        args.append(add)
    return pl.pallas_call(
        body, name=name,
        grid=(m // tm, n // tn, nk),
        in_specs=in_specs,
        out_specs=pl.BlockSpec((tm, tn), lambda i, j, kk: (i, j)),
        out_shape=jax.ShapeDtypeStruct((m, n), out_dtype),
        scratch_shapes=[pltpu.VMEM((tm, tn), F32)],
        compiler_params=_params(("parallel", "parallel", "arbitrary")),
    )(*args)


class _Ctx:
    def __init__(self, first, last):
        self.first = first
        self.last = last


def rowwise(fn, name, rows, seq, tb, ncol, ins, params=(), outs=(), accs=()):
    assert rows % tb == 0 and seq % tb == 0 and tb % 16 == 0
    bps = seq // tb
    nrow = rows // tb
    r8 = tb // SUBLANES
    args, in_specs = [], []
    for arr, w, off, halo in ins:
        args.append(arr)
        in_specs.append(pl.BlockSpec((tb, w), lambda j, i, off=off: (i, off + j)))
        if halo == "prev":
            args.append(arr)
            in_specs.append(pl.BlockSpec((SUBLANES, w), lambda j, i, off=off: (jnp.maximum(i * r8 - 1, 0), off + j)))
        elif halo == "next":
            args.append(arr)
            in_specs.append(pl.BlockSpec(
                (SUBLANES, w), lambda j, i, off=off: (jnp.minimum((i + 1) * r8, rows // SUBLANES - 1), off + j)))
    for arr, w, off in params:
        args.append(arr)
        if w is None:
            in_specs.append(pl.BlockSpec(arr.shape, lambda j, i: (0, 0)))
        else:
            in_specs.append(pl.BlockSpec((arr.shape[0], w), lambda j, i, off=off: (0, off + j)))
    out_shape, out_specs = [], []
    for total, w, off, dt in outs:
        out_shape.append(jax.ShapeDtypeStruct((rows, total), dt))
        out_specs.append(pl.BlockSpec((tb, w), lambda j, i, off=off: (i, off + j)))
    for r, w in accs:
        out_shape.append(jax.ShapeDtypeStruct((r, ncol * w), F32))
        out_specs.append(pl.BlockSpec((r, w), lambda j, i: (0, j)))
    n_out, n_acc = len(outs), len(accs)

    def body(*refs):
        i = pl.program_id(1)
        pos = 0
        vals = []
        for _, _, _, halo in ins:
            cur = refs[pos][...]
            pos += 1
            if halo is None:
                vals.append(cur)
            else:
                vals.append((cur, refs[pos][...]))
                pos += 1
        for _ in params:
            vals.append(refs[pos][...])
            pos += 1
        ctx = _Ctx(i % bps == 0, i % bps == bps - 1)
        res = fn(ctx, *vals)
        if not isinstance(res, (tuple, list)):
            res = (res,)
        assert len(res) == n_out + n_acc
        for q in range(n_out):
            refs[pos + q][...] = res[q].astype(refs[pos + q].dtype)
        for q in range(n_acc):
            ref, val = refs[pos + n_out + q], res[n_out + q]

            @pl.when(i == 0)
            def _(ref=ref, val=val):
                ref[...] = val

            @pl.when(i != 0)
            def _(ref=ref, val=val):
                ref[...] += val

    res = pl.pallas_call(
        body, name=name,
        grid=(ncol, nrow),
        in_specs=in_specs,
        out_specs=out_specs,
        out_shape=out_shape,
        compiler_params=_params(("parallel", "arbitrary")),
    )(*args)
    return res


def _tri(lower):
    r = lax.broadcasted_iota(jnp.int32, (SSD_Q, SSD_Q), 0)
    c = lax.broadcasted_iota(jnp.int32, (SSD_Q, SSD_Q), 1)
    return r >= c if lower else r <= c


def ssd_fwd(xact, dtraw, dt_bias, a_log, d_skip, n_seq, seq):
    nc = seq // SSD_Q
    rows = n_seq * seq

    def body(xact_ref, dtraw_ref, bias_ref, alog_ref, dskip_ref, y_ref, sin_ref, state, cs_s, cst_s, dt_s):
        c = pl.program_id(1)

        @pl.when(c == 0)
        def _():
            state[...] = jnp.zeros_like(state)

        sin_ref[0] = state[...]
        dt = _softplus(dtraw_ref[...] + bias_ref[...])
        a = dt * (-jnp.exp(alog_ref[...]))
        cs = _dot_f32(_tri(True).astype(F32), a)
        cs_s[...] = cs
        cst_s[...] = cs.T
        dt_s[...] = dt
        causal = _tri(True)
        for g in range(SSD_G):
            bg = xact_ref[:, pl.ds(D_INNER + g * SSD_N, SSD_N)]
            cg = xact_ref[:, pl.ds(D_INNER + (SSD_G + g) * SSD_N, SSD_N)]
            gm = _dot_nt(cg, bg)
            for k in range(SSD_K):
                h = g * SSD_K + k
                col = cs_s[:, pl.ds(h, 1)]
                row = cst_s[pl.ds(h, 1), :]
                last = cs_s[pl.ds(SSD_Q - 1, 1), pl.ds(h, 1)]
                decay = jnp.exp(jnp.where(causal, col - row, NEG))
                xh = xact_ref[:, pl.ds(h * SSD_P, SSD_P)]
                xd = xh * dt_s[:, pl.ds(h, 1)]
                sh = state[h]
                y = _dot_nn(gm * decay, xd) + jnp.exp(col) * _dot_nt(cg, sh) + dskip_ref[:, pl.ds(h, 1)] * xh
                y_ref[:, pl.ds(h * SSD_P, SSD_P)] = y
                state[h] = jnp.exp(last) * sh + _dot_tn(jnp.exp(last - col) * xd, bg)

    vec = pl.BlockSpec((1, LANES), lambda b, c: (0, 0))
    return pl.pallas_call(
        body, name="ssd_fwd",
        grid=(n_seq, nc),
        in_specs=[pl.BlockSpec((SSD_Q, CONV_DIM), lambda b, c: (b * nc + c, 0)),
                  pl.BlockSpec((SSD_Q, LANES), lambda b, c: (b * nc + c, 0)), vec, vec, vec],
        out_specs=[pl.BlockSpec((SSD_Q, D_INNER), lambda b, c: (b * nc + c, 0)),
                   pl.BlockSpec((1, SSD_H, SSD_P, SSD_N), lambda b, c: (b * nc + c, 0, 0, 0))],
        out_shape=[jax.ShapeDtypeStruct((rows, D_INNER), F32),
                   jax.ShapeDtypeStruct((n_seq * nc, SSD_H, SSD_P, SSD_N), F32)],
        scratch_shapes=[pltpu.VMEM((SSD_H, SSD_P, SSD_N), F32), pltpu.VMEM((SSD_Q, LANES), F32),
                        pltpu.VMEM((LANES, SSD_Q), F32), pltpu.VMEM((SSD_Q, LANES), F32)],
        compiler_params=_params(("arbitrary", "arbitrary")),
    )(xact, dtraw, dt_bias, a_log, d_skip)


def ssd_bwd(xact, dtraw, dt_bias, a_log, d_skip, sin, dy, n_seq, seq):
    nc = seq // SSD_Q
    rows = n_seq * seq

    def body(xact_ref, dtraw_ref, bias_ref, alog_ref, dskip_ref, sin_ref, dy_ref,
             dx_ref, ddt_ref, dbias_ref, dalog_ref, ddskip_ref, dstate, cs_s, cst_s, dt_s):
        b, c = pl.program_id(0), pl.program_id(1)

        @pl.when(c == 0)
        def _():
            dstate[...] = jnp.zeros_like(dstate)

        pre = dtraw_ref[...] + bias_ref[...]
        dt = _softplus(pre)
        a_neg = -jnp.exp(alog_ref[...])
        cs = _dot_f32(_tri(True).astype(F32), dt * a_neg)
        cs_s[...] = cs
        cst_s[...] = cs.T
        dt_s[...] = dt
        causal = _tri(True)
        is_last_row = lax.broadcasted_iota(jnp.int32, (SSD_Q, 1), 0) == SSD_Q - 1
        dcs_cf = jnp.zeros((SSD_Q, LANES), F32)
        dcs_rf = jnp.zeros((LANES, SSD_Q), F32)
        ddt_cf = jnp.zeros((SSD_Q, LANES), F32)
        dd_vec = jnp.zeros((1, LANES), F32)
        for g in range(SSD_G):
            bg = xact_ref[:, pl.ds(D_INNER + g * SSD_N, SSD_N)]
            cg = xact_ref[:, pl.ds(D_INNER + (SSD_G + g) * SSD_N, SSD_N)]
            gm = _dot_nt(cg, bg)
            dgm = jnp.zeros((SSD_Q, SSD_Q), F32)
            dbg = jnp.zeros((SSD_Q, SSD_N), F32)
            dcg = jnp.zeros((SSD_Q, SSD_N), F32)
            for k in range(SSD_K):
                h = g * SSD_K + k
                col = cs_s[:, pl.ds(h, 1)]
                row = cst_s[pl.ds(h, 1), :]
                last = cs_s[pl.ds(SSD_Q - 1, 1), pl.ds(h, 1)]
                dth = dt_s[:, pl.ds(h, 1)]
                dsk = dskip_ref[:, pl.ds(h, 1)]
                decay = jnp.exp(jnp.where(causal, col - row, NEG))
                xh = xact_ref[:, pl.ds(h * SSD_P, SSD_P)]
                dyh = dy_ref[:, pl.ds(h * SSD_P, SSD_P)]
                xd = xh * dth
                sh = sin_ref[0, h]
                dsn = dstate[h]
                mm = gm * decay
                e_col = jnp.exp(col)
                e_last = jnp.exp(last)
                w = jnp.exp(last - col)
                dm = _dot_nt(dyh, xd)
                dxd = _dot_tn(mm, dyh)
                dseg = dm * mm
                dgm = dgm + dm * decay
                dcol = jnp.sum(dseg, axis=1, keepdims=True)
                drow = -jnp.sum(dseg, axis=0, keepdims=True)
                y_off = e_col * _dot_nt(cg, sh)
                dcol = dcol + jnp.sum(dyh * y_off, axis=1, keepdims=True)
                dcs_ = e_col * dyh
                dcg = dcg + _dot_nn(dcs_, sh)
                dsp = _dot_tn(dcs_, cg)
                dsp = dsp + e_last * dsn
                dlast = jnp.sum(dsn * sh, keepdims=True) * e_last
                t = w * xd
                dbg = dbg + _dot_nn(t, dsn)
                dt_ = _dot_nt(bg, dsn)
                dxd = dxd + w * dt_
                dw = jnp.sum(dt_ * xd, axis=1, keepdims=True) * w
                dlast = dlast + jnp.sum(dw, keepdims=True)
                dcol = dcol - dw + jnp.where(is_last_row, dlast, 0.0)
                dx_ref[:, pl.ds(h * SSD_P, SSD_P)] = dxd * dth + dsk * dyh
                oh_r = _onehot_row(h)
                dcs_cf = dcs_cf + dcol * oh_r
                dcs_rf = dcs_rf + _onehot_col(h) * drow
                ddt_cf = ddt_cf + jnp.sum(dxd * xh, axis=1, keepdims=True) * oh_r
                dd_vec = dd_vec + jnp.sum(dyh * xh, keepdims=True) * oh_r
                dstate[h] = dsp
            dx_ref[:, pl.ds(D_INNER + g * SSD_N, SSD_N)] = dbg + _dot_tn(dgm, cg)
            dx_ref[:, pl.ds(D_INNER + (SSD_G + g) * SSD_N, SSD_N)] = dcg + _dot_nn(dgm, bg)
        dcs = dcs_cf + dcs_rf.T
        da = _dot_f32(_tri(False).astype(F32), dcs)
        ddt = ddt_cf + da * a_neg
        ddtraw = ddt * _sigmoid(pre)
        ddt_ref[...] = ddtraw.astype(ddt_ref.dtype)
        dbias = jnp.sum(ddtraw, axis=0, keepdims=True)
        dalog = jnp.sum(da * dt, axis=0, keepdims=True) * a_neg
        first_step = jnp.logical_and(b == 0, c == 0)

        @pl.when(first_step)
        def _():
            dbias_ref[...] = dbias
            dalog_ref[...] = dalog
            ddskip_ref[...] = dd_vec

        @pl.when(jnp.logical_not(first_step))
        def _():
            dbias_ref[...] += dbias
            dalog_ref[...] += dalog
            ddskip_ref[...] += dd_vec

    def rowblk(b, c):
        return b * nc + (nc - 1 - c)

    vec = pl.BlockSpec((1, LANES), lambda b, c: (0, 0))
    return pl.pallas_call(
        body, name="ssd_bwd",
        grid=(n_seq, nc),
        in_specs=[pl.BlockSpec((SSD_Q, CONV_DIM), lambda b, c: (rowblk(b, c), 0)),
                  pl.BlockSpec((SSD_Q, LANES), lambda b, c: (rowblk(b, c), 0)), vec, vec, vec,
                  pl.BlockSpec((1, SSD_H, SSD_P, SSD_N), lambda b, c: (rowblk(b, c), 0, 0, 0)),
                  pl.BlockSpec((SSD_Q, D_INNER), lambda b, c: (rowblk(b, c), 0))],
        out_specs=[pl.BlockSpec((SSD_Q, CONV_DIM), lambda b, c: (rowblk(b, c), 0)),
                   pl.BlockSpec((SSD_Q, LANES), lambda b, c: (rowblk(b, c), 0)), vec, vec, vec],
        out_shape=[jax.ShapeDtypeStruct((rows, CONV_DIM), F32), jax.ShapeDtypeStruct((rows, LANES), BF16),
                   jax.ShapeDtypeStruct((1, LANES), F32), jax.ShapeDtypeStruct((1, LANES), F32),
                   jax.ShapeDtypeStruct((1, LANES), F32)],
        scratch_shapes=[pltpu.VMEM((SSD_H, SSD_P, SSD_N), F32), pltpu.VMEM((SSD_Q, LANES), F32),
                        pltpu.VMEM((LANES, SSD_Q), F32), pltpu.VMEM((SSD_Q, LANES), F32)],
        compiler_params=_params(("arbitrary", "arbitrary")),
    )(xact, dtraw, dt_bias, a_log, d_skip, sin, dy)


def _band_masks():
    qi = lax.broadcasted_iota(jnp.int32, (ATT_BLK, ATT_BLK), 0)
    ki = lax.broadcasted_iota(jnp.int32, (ATT_BLK, ATT_BLK), 1)
    return qi >= ki, qi <= ki


def attn_fwd(qkv, gq, gk, n_seq, seq, dil, name):
    length = seq // dil
    nb = length // ATT_BLK
    w3 = 3 * ATT_OUT

    def body(cur_ref, prev_ref, gq_ref, gk_ref, o_ref, lse_ref):
        n = pl.program_id(2)
        m_cur, m_prev = _band_masks()
        m_prev = jnp.logical_and(m_prev, n > 0)
        gq_, gk_ = gq_ref[...], gk_ref[...]
        lse_blk = jnp.zeros((ATT_BLK, LANES), F32)
        for h in range(ATT_H):
            q = _rms_fwd(cur_ref[0, :, pl.ds(h * ATT_HD, ATT_HD)], gq_)
            kc = _rms_fwd(cur_ref[0, :, pl.ds(ATT_OUT + h * ATT_HD, ATT_HD)], gk_)
            kp = _rms_fwd(prev_ref[0, :, pl.ds(ATT_OUT + h * ATT_HD, ATT_HD)], gk_)
            vc = cur_ref[0, :, pl.ds(2 * ATT_OUT + h * ATT_HD, ATT_HD)]
            vp = prev_ref[0, :, pl.ds(2 * ATT_OUT + h * ATT_HD, ATT_HD)]
            sc = jnp.where(m_cur, _dot_nt(q, kc) * ATT_SCALE, NEG)
            sp = jnp.where(m_prev, _dot_nt(q, kp) * ATT_SCALE, NEG)
            mx = jnp.maximum(jnp.max(sc, axis=1, keepdims=True), jnp.max(sp, axis=1, keepdims=True))
            pc = jnp.exp(sc - mx)
            pp = jnp.exp(sp - mx)
            den = jnp.sum(pc, axis=1, keepdims=True) + jnp.sum(pp, axis=1, keepdims=True)
            o_ref[0, :, pl.ds(h * ATT_HD, ATT_HD)] = (_dot_nn(pc, vc) + _dot_nn(pp, vp)) / den
            lse_blk = lse_blk + (mx + jnp.log(den)) * _onehot_row(h)
        lse_ref[0] = lse_blk

    view = qkv.reshape(n_seq, length, dil * w3)
    gspec = pl.BlockSpec((1, ATT_HD), lambda b, r, n: (0, 0))
    o, lse = pl.pallas_call(
        body, name=name,
        grid=(n_seq, dil, nb),
        in_specs=[pl.BlockSpec((1, ATT_BLK, w3), lambda b, r, n: (b, n, r)),
                  pl.BlockSpec((1, ATT_BLK, w3), lambda b, r, n: (b, jnp.maximum(n - 1, 0), r)), gspec, gspec],
        out_specs=[pl.BlockSpec((1, ATT_BLK, ATT_OUT), lambda b, r, n: (b, n, r)),
                   pl.BlockSpec((1, ATT_BLK, LANES), lambda b, r, n: (b, n, r))],
        out_shape=[jax.ShapeDtypeStruct((n_seq, length, dil * ATT_OUT), F32),
                   jax.ShapeDtypeStruct((n_seq, length, dil * LANES), F32)],
        compiler_params=_params(("parallel", "parallel", "arbitrary")),
    )(view, view, gq, gk)
    return o.reshape(n_seq * seq, ATT_OUT), lse.reshape(n_seq * seq, LANES)


def attn_bwd(qkv, gq, gk, do, lse, wts, rsum, n_seq, seq, dil, name):
    length = seq // dil
    nb = length // ATT_BLK
    w3 = 3 * ATT_OUT

    def body(prev_ref, cur_ref, nxt_ref, gq_ref, gk_ref, do_c, do_x, lse_c, lse_x, wt_c, wt_x, rs_c, rs_x,
             dqkv_ref, dgq_ref, dgk_ref):
        b, r, n = pl.program_id(0), pl.program_id(1), pl.program_id(2)
        m_cur, m_band = _band_masks()
        m_prev = jnp.logical_and(m_band, n > 0)
        m_next = jnp.logical_and(m_band, n < nb - 1)
        gq_, gk_ = gq_ref[...], gk_ref[...]
        dgq = jnp.zeros((1, ATT_HD), F32)
        dgk = jnp.zeros((1, ATT_HD), F32)
        for h in range(ATT_H):
            hs = pl.ds(h * ATT_HD, ATT_HD)
            ks = pl.ds(ATT_OUT + h * ATT_HD, ATT_HD)
            vs = pl.ds(2 * ATT_OUT + h * ATT_HD, ATT_HD)
            one = pl.ds(h, 1)
            q_raw, k_raw = cur_ref[0, :, hs], cur_ref[0, :, ks]
            qc = _rms_fwd(q_raw, gq_)
            qx = _rms_fwd(nxt_ref[0, :, hs], gq_)
            kc = _rms_fwd(k_raw, gk_)
            kp = _rms_fwd(prev_ref[0, :, ks], gk_)
            vc, vp = cur_ref[0, :, vs], prev_ref[0, :, vs]
            wc, wx = wt_c[0, :, one], wt_x[0, :, one]
            dog_c = do_c[0, :, hs] * wc
            dog_x = do_x[0, :, hs] * wx
            dl_c = -wc * rs_c[0, :, one]
            dl_x = -wx * rs_x[0, :, one]
            lc, lx = lse_c[0, :, one], lse_x[0, :, one]
            p_cc = jnp.exp(jnp.where(m_cur, _dot_nt(qc, kc) * ATT_SCALE - lc, NEG))
            p_cp = jnp.exp(jnp.where(m_prev, _dot_nt(qc, kp) * ATT_SCALE - lc, NEG))
            p_xc = jnp.exp(jnp.where(m_next, _dot_nt(qx, kc) * ATT_SCALE - lx, NEG))
            ds_cc = p_cc * (_dot_nt(dog_c, vc) + dl_c)
            ds_cp = p_cp * (_dot_nt(dog_c, vp) + dl_c)
            ds_xc = p_xc * (_dot_nt(dog_x, vc) + dl_x)
            dqn = (_dot_nn(ds_cc, kc) + _dot_nn(ds_cp, kp)) * ATT_SCALE
            dkn = (_dot_tn(ds_cc, qc) + _dot_tn(ds_xc, qx)) * ATT_SCALE
            dv = _dot_tn(p_cc, dog_c) + _dot_tn(p_xc, dog_x)
            dq, dgq_h = _rms_bwd(q_raw, gq_, dqn)
            dk, dgk_h = _rms_bwd(k_raw, gk_, dkn)
            dqkv_ref[0, :, hs] = dq.astype(dqkv_ref.dtype)
            dqkv_ref[0, :, ks] = dk.astype(dqkv_ref.dtype)
            dqkv_ref[0, :, vs] = dv.astype(dqkv_ref.dtype)
            dgq = dgq + dgq_h
            dgk = dgk + dgk_h
        first_step = jnp.logical_and(jnp.logical_and(b == 0, r == 0), n == 0)

        @pl.when(first_step)
        def _():
            dgq_ref[...] = dgq
            dgk_ref[...] = dgk

        @pl.when(jnp.logical_not(first_step))
        def _():
            dgq_ref[...] += dgq
            dgk_ref[...] += dgk

    view = qkv.reshape(n_seq, length, dil * w3)
    do_v = do.reshape(n_seq, length, dil * ATT_OUT)
    lse_v = lse.reshape(n_seq, length, dil * LANES)
    wts_v = wts.reshape(n_seq, length, dil * LANES)
    rs_v = rsum.reshape(n_seq, length, dil * LANES)

    def at(shift, width):
        if shift < 0:
            return pl.BlockSpec((1, ATT_BLK, width), lambda b, r, n: (b, jnp.maximum(n - 1, 0), r))
        if shift > 0:
            return pl.BlockSpec((1, ATT_BLK, width), lambda b, r, n: (b, jnp.minimum(n + 1, nb - 1), r))
        return pl.BlockSpec((1, ATT_BLK, width), lambda b, r, n: (b, n, r))

    gspec = pl.BlockSpec((1, ATT_HD), lambda b, r, n: (0, 0))
    dqkv, dgq, dgk = pl.pallas_call(
        body, name=name,
        grid=(n_seq, dil, nb),
        in_specs=[at(-1, w3), at(0, w3), at(1, w3), gspec, gspec,
                  at(0, ATT_OUT), at(1, ATT_OUT), at(0, LANES), at(1, LANES),
                  at(0, LANES), at(1, LANES), at(0, LANES), at(1, LANES)],
        out_specs=[at(0, w3), gspec, gspec],
        out_shape=[jax.ShapeDtypeStruct((n_seq, length, dil * w3), BF16),
                   jax.ShapeDtypeStruct((1, ATT_HD), F32), jax.ShapeDtypeStruct((1, ATT_HD), F32)],
        compiler_params=_params(("arbitrary", "arbitrary", "arbitrary")),
    )(view, view, view, gq, gk, do_v, do_v, lse_v, lse_v, wts_v, wts_v, rs_v, rs_v)
    return dqkv.reshape(n_seq * seq, w3), dgq, dgk


def exchange(items, name):
    n = len(items)

    def body(*refs):
        in_refs, out_refs = refs[:n], refs[n:2 * n]
        send_sems, recv_sems, local_sems = refs[2 * n:]
        x, y, c = lax.axis_index("x"), lax.axis_index("y"), lax.axis_index("c")
        me = 4 * x + 2 * y + c

        def peer(k):
            px = 1 - x if k & 4 else x
            py = 1 - y if k & 2 else y
            pc = 1 - c if k & 1 else c
            return (px, py, pc), 4 * px + 2 * py + pc

        def remote(t, k):
            dev, pid = peer(k)
            src = in_refs[t] if items[t][1] == "gather" else in_refs[t].at[pid]
            return pltpu.make_async_remote_copy(
                src_ref=src, dst_ref=out_refs[t].at[me], send_sem=send_sems.at[t, k], recv_sem=recv_sems.at[t, k],
                device_id=dev, device_id_type=pl.DeviceIdType.MESH)

        def arrival(t, k):
            dev, pid = peer(k)
            src = in_refs[t] if items[t][1] == "gather" else in_refs[t].at[pid]
            return pltpu.make_async_remote_copy(
                src_ref=src, dst_ref=out_refs[t].at[pid], send_sem=send_sems.at[t, k], recv_sem=recv_sems.at[t, k],
                device_id=dev, device_id_type=pl.DeviceIdType.MESH)

        def own(t):
            src = in_refs[t] if items[t][1] == "gather" else in_refs[t].at[me]
            return pltpu.make_async_copy(src, out_refs[t].at[me], local_sems.at[t])

        for t in range(n):
            own(t).start()
            for k in range(1, N_DEV):
                remote(t, k).start()
        for t in range(n):
            for k in range(1, N_DEV):
                arrival(t, k).wait_recv()
        for t in range(n):
            for k in range(1, N_DEV):
                remote(t, k).wait_send()
            own(t).wait()

    out_shape = []
    for arr, mode in items:
        shp = arr.shape if mode == "gather" else arr.shape[1:]
        out_shape.append(jax.ShapeDtypeStruct((N_DEV,) + tuple(shp), arr.dtype))
    anyspec = pl.BlockSpec(memory_space=pl.ANY)
    return pl.pallas_call(
        body, name=name,
        in_specs=[anyspec] * n,
        out_specs=[anyspec] * n,
        out_shape=out_shape,
        scratch_shapes=[pltpu.SemaphoreType.DMA((n, N_DEV)), pltpu.SemaphoreType.DMA((n, N_DEV)),
                        pltpu.SemaphoreType.DMA((n,))],
    )(*[a for a, _ in items])


def adamw(parts, w, m, v, name):
    r, c = w.shape
    rb = r if r <= 512 else (128 if c > 1024 else 256)
    assert r % rb == 0

    def body(p_ref, w_ref, m_ref, v_ref, g_out, d_out, m_out, v_out):
        g = p_ref[0]
        for i in range(1, N_DEV):
            g = g + p_ref[i]
        m_new = ADAM_B1 * m_ref[...] + (1.0 - ADAM_B1) * g
        v_new = ADAM_B2 * v_ref[...] + (1.0 - ADAM_B2) * (g * g)
        m_hat = m_new / (1.0 - ADAM_B1 ** ADAM_STEP)
        v_hat = v_new / (1.0 - ADAM_B2 ** ADAM_STEP)
        g_out[...] = g
        d_out[...] = -ADAM_LR * (m_hat / (jnp.sqrt(v_hat) + ADAM_EPS) + ADAM_WD * w_ref[...])
        m_out[...] = m_new
        v_out[...] = v_new

    blk = pl.BlockSpec((rb, c), lambda i: (i, 0))
    return pl.pallas_call(
        body, name=name,
        grid=(r // rb,),
        in_specs=[pl.BlockSpec((N_DEV, rb, c), lambda i: (0, i, 0)), blk, blk, blk],
        out_specs=[blk] * 4,
        out_shape=[jax.ShapeDtypeStruct((r, c), F32)] * 4,
        compiler_params=_params(("parallel",)),
    )(parts, w, m, v)


def _pad_lanes(vec, n=LANES):
    return jnp.pad(vec, ((0, 0), (0, n - vec.shape[1])))


def local_step(x, target, w):
    n_seq, seq, _ = x.shape
    rows = n_seq * seq
    x = x.reshape(rows, D_MODEL)
    target = target.reshape(rows, D_MODEL)
    mx = lambda a: a.astype(MXU)

    splits = [sum(IN_WIDTHS[:i]) for i in range(len(IN_WIDTHS) + 1)]
    w_in = w["w_in"]
    part = lambda i: w_in[:, splits[i]:splits[i + 1]]
    w_z, w_xbc, w_gs, w_ga = mx(part(0)), mx(part(1)), mx(part(6)), mx(part(7))
    w_dt = mx(_pad_lanes(part(2)))
    w_qkv = [mx(jnp.concatenate([part(3 + t)[:, g * ATT_OUT:(g + 1) * ATT_OUT] for t in range(3)], axis=1))
             for g in range(ATT_GROUPS)]
    w_sp, w_ap, w_o, w_d = mx(w["w_ssd_proj"]), mx(w["w_attn_proj"]), mx(w["w_out"]), mx(w["w_down"])
    w_ug, w_uv = mx(w["w_up"][:, :D_FF]), mx(w["w_up"][:, D_FF:])
    conv_w, conv_b = w["ssd_conv_w"], w["ssd_conv_b"]
    fconv_w, fconv_b = w["ffn_conv_w"], w["ffn_conv_b"]
    dt_bias, a_log, d_skip = _pad_lanes(w["dt_bias"]), _pad_lanes(w["a_log"]), _pad_lanes(w["d_skip"])
    g1, g2, gn, gq, gk = w["norm1_g"], w["norm2_g"], w["ssd_norm_g"], w["q_norm_g"], w["k_norm_g"]

    tb = min(256, seq)
    cw = 512
    rw = lambda fn, name, ncol, ins, params=(), outs=(), accs=(): rowwise(
        fn, name, rows, seq, tb, ncol, ins, params, outs, accs)

    (h,) = rw(lambda ctx, xv, g: _rms_fwd(xv, g), "rms1_fwd", 1, [(x, D_MODEL, 0, None)], [(g1, None, 0)],
              [(D_MODEL, D_MODEL, 0, MXU)])
    z = matmul(h, w_z, "mm_z")
    xbc = matmul(h, w_xbc, "mm_xbc")
    dtraw = matmul(h, w_dt, "mm_dt")
    qkv = [matmul(h, w_qkv[g], f"mm_qkv{g}") for g in range(ATT_GROUPS)]
    gs = matmul(h, w_gs, "mm_gs")
    ga = matmul(h, w_ga, "mm_ga")

    def conv_silu(ctx, xh, wv, bv):
        return _silu(bv + _conv_prev(xh[0], xh[1], wv, ctx.first, SSD_CONV))

    (xact,) = rw(conv_silu, "ssd_conv_fwd", CONV_DIM // cw, [(xbc, cw, 0, "prev")],
                 [(conv_w, cw, 0), (conv_b, cw, 0)], [(CONV_DIM, cw, 0, F32)])
    y, sin = ssd_fwd(xact, dtraw, dt_bias, a_log, d_skip, n_seq, seq)

    def gated_norm(ctx, yv, zv, g):
        yz = yv * _silu(zv)
        return jnp.concatenate([_rms_fwd(yz[:, i:i + NORM_GROUP], g[:, i:i + NORM_GROUP])
                                for i in range(0, cw, NORM_GROUP)], axis=1)

    (y_ssd,) = rw(gated_norm, "ssd_post_fwd", D_INNER // cw, [(y, cw, 0, None), (z, cw, 0, None)], [(gn, cw, 0)],
                  [(D_INNER, cw, 0, MXU)])

    att = [attn_fwd(qkv[g], gq, gk, n_seq, seq, ATT_DILATIONS[g], f"attn_fwd{g}") for g in range(ATT_GROUPS)]

    def combine(ctx, o0, o1, o2, l0, l1, l2):
        mxl = jnp.maximum(jnp.maximum(l0, l1), l2)
        e = [jnp.exp(l - mxl) for l in (l0, l1, l2)]
        inv = 1.0 / (e[0] + e[1] + e[2])
        ws = [ei * inv for ei in e]
        out = sum(_expand_heads(wi) * oi for wi, oi in zip(ws, (o0, o1, o2)))
        return (out, *ws)

    y_attn, wt0, wt1, wt2 = rw(
        combine, "attn_combine", 1,
        [(att[g][0], ATT_OUT, 0, None) for g in range(3)] + [(att[g][1], LANES, 0, None) for g in range(3)], [],
        [(ATT_OUT, ATT_OUT, 0, F32)] + [(LANES, LANES, 0, F32)] * 3)
    wts = (wt0, wt1, wt2)

    ps = matmul(y_ssd, w_sp, "mm_ssd_proj")
    pa = matmul(y_attn, w_ap, "mm_attn_proj")
    (merged,) = rw(lambda ctx, a, b, c, d: _sigmoid(c) * a + _sigmoid(d) * b, "merge_fwd", D_MODEL // cw,
                   [(ps, cw, 0, None), (pa, cw, 0, None), (gs, cw, 0, None), (ga, cw, 0, None)], [],
                   [(D_MODEL, cw, 0, MXU)])
    x1 = matmul(merged, w_o, "mm_out", add=x)
    (h2,) = rw(lambda ctx, xv, g: _rms_fwd(xv, g), "rms2_fwd", 1, [(x1, D_MODEL, 0, None)], [(g2, None, 0)],
               [(D_MODEL, D_MODEL, 0, MXU)])
    up_g = matmul(h2, w_ug, "mm_up_g")
    up_v = matmul(h2, w_uv, "mm_up_v")
    fw = 256
    nfc = D_FF // fw

    def mlp_act(ctx, ug, uv, wg, wv, bg, bv):
        cg = bg + _conv_prev(ug[0], ug[1], wg, ctx.first, FFN_CONV)
        cv = bv + _conv_prev(uv[0], uv[1], wv, ctx.first, FFN_CONV)
        return _silu(cg) * cv

    (act,) = rw(mlp_act, "mlp_act_fwd", nfc, [(up_g, fw, 0, "prev"), (up_v, fw, 0, "prev")],
                [(fconv_w, fw, 0), (fconv_w, fw, nfc), (fconv_b, fw, 0), (fconv_b, fw, nfc)], [(D_FF, fw, 0, MXU)])
    x2 = matmul(act, w_d, "mm_down", add=x1)

    def loss_fn(ctx, xv, tv):
        d = xv - tv
        return d * (1.0 / D_MODEL), jnp.sum(d * d, axis=0, keepdims=True)

    dx2, sq = rw(loss_fn, "loss", 1, [(x2, D_MODEL, 0, None), (target, D_MODEL, 0, None)], [],
                 [(D_MODEL, D_MODEL, 0, F32)], [(1, D_MODEL)])

    grads = {}
    dact = matmul(dx2, w_d, "mm_d_act", tb=True)
    grads["w_down"] = matmul(act, dx2, "mm_dw_down", ta=True)

    def mlp_act_bwd(ctx, da, ug, uv, wg, wv, bg, bv):
        cg = bg + _conv_prev(ug[0], ug[1], wg, ctx.first, FFN_CONV)
        cv = bv + _conv_prev(uv[0], uv[1], wv, ctx.first, FFN_CONV)
        return da * cv * _silu_grad(cg), da * _silu(cg)

    dcg, dcv = rw(mlp_act_bwd, "mlp_act_bwd", nfc, [(dact, fw, 0, None), (up_g, fw, 0, "prev"), (up_v, fw, 0, "prev")],
                  [(fconv_w, fw, 0), (fconv_w, fw, nfc), (fconv_b, fw, 0), (fconv_b, fw, nfc)],
                  [(D_FF, fw, 0, F32), (D_FF, fw, 0, F32)])

    def conv_bwd(taps):
        def fn(ctx, dpre, xin, wv):
            dpre_c, dpre_h = dpre
            x_c, x_h = xin
            dx = None
            dws = []
            for i in range(taps):
                term = wv[i:i + 1, :] * _shift_next(dpre_c, dpre_h, taps - 1 - i, ctx.last)
                dx = term if dx is None else dx + term
                dws.append(jnp.sum(dpre_c * _shift_prev(x_c, x_h, taps - 1 - i, ctx.first), axis=0, keepdims=True))
            return dx, jnp.concatenate(dws, axis=0), jnp.sum(dpre_c, axis=0, keepdims=True)
        return fn

    dup_g, dfw_g, dfb_g = rw(conv_bwd(FFN_CONV), "ffn_conv_bwd_g", nfc, [(dcg, fw, 0, "next"), (up_g, fw, 0, "prev")],
                             [(fconv_w, fw, 0)], [(D_FF, fw, 0, MXU)], [(FFN_CONV, fw), (1, fw)])
    dup_v, dfw_v, dfb_v = rw(conv_bwd(FFN_CONV), "ffn_conv_bwd_v", nfc, [(dcv, fw, 0, "next"), (up_v, fw, 0, "prev")],
                             [(fconv_w, fw, nfc)], [(D_FF, fw, 0, MXU)], [(FFN_CONV, fw), (1, fw)])
    grads["ffn_conv_w"] = jnp.concatenate([dfw_g, dfw_v], axis=1)
    grads["ffn_conv_b"] = jnp.concatenate([dfb_g, dfb_v], axis=1)
    dh2 = matmul(dup_g, w_ug, "mm_dh2_g", tb=True)
    dh2 = matmul(dup_v, w_uv, "mm_dh2_v", tb=True, add=dh2)
    grads["w_up"] = jnp.concatenate([matmul(h2, dup_g, "mm_dw_up_g", ta=True),
                                     matmul(h2, dup_v, "mm_dw_up_v", ta=True)], axis=1)

    def rms_bwd_fn(ctx, xv, dh_, dres, g):
        dxv, dg = _rms_bwd(xv, g, dh_)
        return dres + dxv, dg

    dx1, grads["norm2_g"] = rw(rms_bwd_fn, "rms2_bwd", 1,
                               [(x1, D_MODEL, 0, None), (dh2, D_MODEL, 0, None), (dx2, D_MODEL, 0, None)],
                               [(g2, None, 0)], [(D_MODEL, D_MODEL, 0, F32)], [(1, D_MODEL)])

    dmerged = matmul(dx1, w_o, "mm_d_merged", tb=True)
    grads["w_out"] = matmul(merged, dx1, "mm_dw_out", ta=True)

    def merge_bwd(ctx, dm, a, b, c, d):
        sc, sd = _sigmoid(c), _sigmoid(d)
        return dm * sc, dm * sd, dm * a * sc * (1.0 - sc), dm * b * sd * (1.0 - sd)

    dps, dpa, dgs, dga = rw(merge_bwd, "merge_bwd", D_MODEL // cw,
                            [(dmerged, cw, 0, None), (ps, cw, 0, None), (pa, cw, 0, None), (gs, cw, 0, None),
                             (ga, cw, 0, None)], [], [(D_MODEL, cw, 0, MXU)] * 4)
    dy_ssd = matmul(dps, w_sp, "mm_d_y_ssd", tb=True)
    grads["w_ssd_proj"] = matmul(y_ssd, dps, "mm_dw_ssd_proj", ta=True)
    dy_attn = matmul(dpa, w_ap, "mm_d_y_attn", tb=True)
    grads["w_attn_proj"] = matmul(y_attn, dpa, "mm_dw_attn_proj", ta=True)

    (rsum,) = rw(lambda ctx, a, b: _reduce_heads(a * b), "attn_rsum", 1,
                 [(dy_attn, ATT_OUT, 0, None), (y_attn, ATT_OUT, 0, None)], [], [(LANES, LANES, 0, F32)])
    dqkv, dgq, dgk = [], 0.0, 0.0
    for g in range(ATT_GROUPS):
        d_, a_, b_ = attn_bwd(qkv[g], gq, gk, dy_attn, att[g][1], wts[g], rsum, n_seq, seq, ATT_DILATIONS[g],
                              f"attn_bwd{g}")
        dqkv.append(d_)
        dgq, dgk = dgq + a_, dgk + b_
    grads["q_norm_g"], grads["k_norm_g"] = dgq, dgk

    def gated_norm_bwd(ctx, dyn, yv, zv, g):
        sz = _silu(zv)
        yz = yv * sz
        dyz, dgs_ = [], []
        for i in range(0, cw, NORM_GROUP):
            a, b = _rms_bwd(yz[:, i:i + NORM_GROUP], g[:, i:i + NORM_GROUP], dyn[:, i:i + NORM_GROUP])
            dyz.append(a)
            dgs_.append(b)
        dyz = jnp.concatenate(dyz, axis=1)
        return dyz * sz, dyz * yv * _silu_grad(zv), jnp.concatenate(dgs_, axis=1)

    dy, dz, grads["ssd_norm_g"] = rw(gated_norm_bwd, "ssd_post_bwd", D_INNER // cw,
                                     [(dy_ssd, cw, 0, None), (y, cw, 0, None), (z, cw, 0, None)], [(gn, cw, 0)],
                                     [(D_INNER, cw, 0, F32), (D_INNER, cw, 0, MXU)], [(1, cw)])
    dxact, ddt, dbias, dalog, ddskip = ssd_bwd(xact, dtraw, dt_bias, a_log, d_skip, sin, dy, n_seq, seq)
    grads["dt_bias"], grads["a_log"], grads["d_skip"] = dbias[:, :SSD_H], dalog[:, :SSD_H], ddskip[:, :SSD_H]

    def conv_silu_bwd(ctx, dxa, xh, wv, bv):
        return dxa * _silu_grad(bv + _conv_prev(xh[0], xh[1], wv, ctx.first, SSD_CONV))

    (dpre,) = rw(conv_silu_bwd, "ssd_conv_bwd_act", CONV_DIM // cw, [(dxact, cw, 0, None), (xbc, cw, 0, "prev")],
                 [(conv_w, cw, 0), (conv_b, cw, 0)], [(CONV_DIM, cw, 0, F32)])
    dxbc, grads["ssd_conv_w"], grads["ssd_conv_b"] = rw(
        conv_bwd(SSD_CONV), "ssd_conv_bwd", CONV_DIM // cw, [(dpre, cw, 0, "next"), (xbc, cw, 0, "prev")],
        [(conv_w, cw, 0)], [(CONV_DIM, cw, 0, MXU)], [(SSD_CONV, cw), (1, cw)])

    pieces = [(dz, w_z, "z"), (dxbc, w_xbc, "xbc"), (ddt, w_dt, "dt"), (dgs, w_gs, "gs"), (dga, w_ga, "ga")]
    pieces += [(dqkv[g], w_qkv[g], f"qkv{g}") for g in range(ATT_GROUPS)]
    dh, dws = None, {}
    for dpart, wpart, tag in pieces:
        dh = matmul(dpart, wpart, f"mm_dh_{tag}", tb=True, add=dh)
        dws[tag] = matmul(h, dpart, f"mm_dw_{tag}", ta=True)
    dq_parts = [[dws[f"qkv{g}"][:, t * ATT_OUT:(t + 1) * ATT_OUT] for g in range(ATT_GROUPS)] for t in range(3)]
    grads["w_in"] = jnp.concatenate(
        [dws["z"], dws["xbc"], dws["dt"][:, :SSD_H]] + [p for t in range(3) for p in dq_parts[t]] + [dws["gs"], dws["ga"]],
        axis=1)
    grad_x, grads["norm1_g"] = rw(rms_bwd_fn, "rms1_bwd", 1,
                                  [(x, D_MODEL, 0, None), (dh, D_MODEL, 0, None), (dx1, D_MODEL, 0, None)],
                                  [(g1, None, 0)], [(D_MODEL, D_MODEL, 0, F32)], [(1, D_MODEL)])
    return sq, grad_x.reshape(n_seq, seq, D_MODEL), grads


COL_SHARDED = ("w_in", "ssd_conv_w", "w_attn_proj", "w_up", "ffn_conv_w")
ROW_SHARDED = ("w_ssd_proj", "w_out", "w_down")
REPLICATED = ("norm1_g", "ssd_conv_b", "dt_bias", "a_log", "d_skip", "ssd_norm_g", "q_norm_g", "k_norm_g",
              "norm2_g", "ffn_conv_b")
WEIGHTS = ("norm1_g", "w_in", "ssd_conv_w", "ssd_conv_b", "dt_bias", "a_log", "d_skip", "ssd_norm_g", "w_ssd_proj",
           "q_norm_g", "k_norm_g", "w_attn_proj", "w_out", "norm2_g", "w_up", "ffn_conv_w", "ffn_conv_b", "w_down")
PACK_ROWS, PACK_COLS = 8, 2048


def _pack(vals):
    flat = jnp.concatenate([vals[n].reshape(-1) for n in REPLICATED])
    return jnp.pad(flat, (0, PACK_ROWS * PACK_COLS - flat.shape[0])).reshape(PACK_ROWS, PACK_COLS)


def _unpack(packed, like):
    flat = packed.reshape(-1)
    out, pos = {}, 0
    for n in REPLICATED:
        size = like[n].size
        out[n] = flat[pos:pos + size].reshape(like[n].shape)
        pos += size
    return out


def step(x, target, wsh, msh, vsh):
    sharded = COL_SHARDED + ROW_SHARDED
    gathered = exchange([(wsh[n], "gather") for n in sharded], "ag_weights")
    full = {n: wsh[n] for n in REPLICATED}
    for n, g in zip(sharded, gathered):
        if n in COL_SHARDED:
            full[n] = jnp.transpose(g, (1, 0, 2)).reshape(g.shape[1], N_DEV * g.shape[2])
        else:
            full[n] = g.reshape(N_DEV * g.shape[1], g.shape[2])

    sq, grad_x, grads = local_step(x, target, full)

    slabs = []
    for n in sharded:
        g = grads[n]
        if n in COL_SHARDED:
            slabs.append(jnp.transpose(g.reshape(g.shape[0], N_DEV, g.shape[1] // N_DEV), (1, 0, 2)))
        else:
            slabs.append(g.reshape(N_DEV, g.shape[0] // N_DEV, g.shape[1]))
    packed_g = _pack({n: grads[n] for n in REPLICATED})
    received = exchange([(s, "scatter") for s in slabs] + [(packed_g, "gather")], "rs_grads")

    out_g, out_d, out_m, out_v = {}, {}, {}, {}
    for n, parts in zip(sharded, received[:-1]):
        out_g[n], out_d[n], out_m[n], out_v[n] = adamw(parts, wsh[n], msh[n], vsh[n], f"adamw_{n}")
    pk = adamw(received[-1], _pack(wsh), _pack(msh), _pack(vsh), "adamw_small")
    for dst, packed in zip((out_g, out_d, out_m, out_v), pk):
        dst.update(_unpack(packed, wsh))
    loss = lax.psum(0.5 * jnp.sum(sq) / D_MODEL, ("x", "y", "c"))
    return loss, grad_x, out_g, out_d, out_m, out_v


def kernel(x, norm1_g, w_in, ssd_conv_w, ssd_conv_b, dt_bias, a_log, d_skip, ssd_norm_g, w_ssd_proj, q_norm_g, k_norm_g, w_attn_proj, w_out, norm2_g, w_up, ffn_conv_w, ffn_conv_b, w_down, loss_target, m_norm1_g, m_w_in, m_ssd_conv_w, m_ssd_conv_b, m_dt_bias, m_a_log, m_d_skip, m_ssd_norm_g, m_w_ssd_proj, m_q_norm_g, m_k_norm_g, m_w_attn_proj, m_w_out, m_norm2_g, m_w_up, m_ffn_conv_w, m_ffn_conv_b, m_w_down, v_norm1_g, v_w_in, v_ssd_conv_w, v_ssd_conv_b, v_dt_bias, v_a_log, v_d_skip, v_ssd_norm_g, v_w_ssd_proj, v_q_norm_g, v_k_norm_g, v_w_attn_proj, v_w_out, v_norm2_g, v_w_up, v_ffn_conv_w, v_ffn_conv_b, v_w_down):
    ws = (norm1_g, w_in, ssd_conv_w, ssd_conv_b, dt_bias, a_log, d_skip, ssd_norm_g, w_ssd_proj, q_norm_g, k_norm_g,
          w_attn_proj, w_out, norm2_g, w_up, ffn_conv_w, ffn_conv_b, w_down)
    ms = (m_norm1_g, m_w_in, m_ssd_conv_w, m_ssd_conv_b, m_dt_bias, m_a_log, m_d_skip, m_ssd_norm_g, m_w_ssd_proj,
          m_q_norm_g, m_k_norm_g, m_w_attn_proj, m_w_out, m_norm2_g, m_w_up, m_ffn_conv_w, m_ffn_conv_b, m_w_down)
    vs = (v_norm1_g, v_w_in, v_ssd_conv_w, v_ssd_conv_b, v_dt_bias, v_a_log, v_d_skip, v_ssd_norm_g, v_w_ssd_proj,
          v_q_norm_g, v_k_norm_g, v_w_attn_proj, v_w_out, v_norm2_g, v_w_up, v_ffn_conv_w, v_ffn_conv_b, v_w_down)
    strip = lambda a: a[0] if a.ndim == 3 else a
    wsh = {n: strip(a) for n, a in zip(WEIGHTS, ws)}
    msh = {n: strip(a) for n, a in zip(WEIGHTS, ms)}
    vsh = {n: strip(a) for n, a in zip(WEIGHTS, vs)}
    loss, grad_x, g, d, m, v = step(x, loss_target, wsh, msh, vsh)
    lead = lambda dct: [dct[n][None] if a.ndim == 3 else dct[n] for n, a in zip(WEIGHTS, ws)]
    return (loss, grad_x, *lead(g), *lead(d), *lead(m), *lead(v))
```

```python
import jax
import jax.numpy as jnp
from jax import lax
from jax.experimental import pallas as pl
from jax.experimental.pallas import tpu as pltpu

F32 = jnp.float32
BF16 = jnp.bfloat16
MXU = jnp.bfloat16
HIGHEST = lax.Precision.HIGHEST
VMEM_LIMIT_BYTES = 48 * 1024 * 1024
SUBLANES = 8
LANES = 128
N_DEV = 8

D_MODEL = 1024
D_INNER = 2048
SSD_P = 64
SSD_H = 32
SSD_G = 8
SSD_K = SSD_H // SSD_G
SSD_N = 128
SSD_Q = 128
SSD_CONV = 4
CONV_DIM = D_INNER + 2 * SSD_G * SSD_N
NORM_GROUP = D_INNER // SSD_G
ATT_GROUPS = 3
ATT_H = 8
ATT_HD = 64
ATT_BLK = 128
ATT_OUT = ATT_H * ATT_HD
ATT_DILATIONS = (1, 4, 16)
ATT_SCALE = ATT_HD ** -0.5
D_FF = 2816
FFN_CONV = 3
EPS = 1e-6
NEG = -1e30
IN_WIDTHS = (D_INNER, CONV_DIM, SSD_H, 3 * ATT_OUT, 3 * ATT_OUT, 3 * ATT_OUT, D_MODEL, D_MODEL)

ADAM_LR = 0.001
ADAM_B1 = 0.9
ADAM_B2 = 0.999
ADAM_EPS = 1e-08
ADAM_WD = 0.01
ADAM_STEP = 10


def _mm(a, b, dims):
    return lax.dot_general(a.astype(MXU), b.astype(MXU), (dims, ((), ())), preferred_element_type=F32)


def _dot_nn(a, b):
    return _mm(a, b, ((1,), (0,)))


def _dot_nt(a, b):
    return _mm(a, b, ((1,), (1,)))


def _dot_tn(a, b):
    return _mm(a, b, ((0,), (0,)))


def _dot_f32(a, b):
    return lax.dot_general(a, b, (((1,), (0,)), ((), ())), precision=HIGHEST, preferred_element_type=F32)


def _sigmoid(x):
    return 1.0 / (1.0 + jnp.exp(-x))


def _silu(x):
    return x * _sigmoid(x)


def _silu_grad(x):
    s = _sigmoid(x)
    return s * (1.0 + x * (1.0 - s))


def _softplus(x):
    return jnp.maximum(x, 0.0) + jnp.log(1.0 + jnp.exp(-jnp.abs(x)))


def _rms_fwd(x, g):
    r = lax.rsqrt(jnp.mean(x * x, axis=-1, keepdims=True) + EPS)
    return x * r * g


def _rms_bwd(x, g, dy):
    r = lax.rsqrt(jnp.mean(x * x, axis=-1, keepdims=True) + EPS)
    xh = x * r
    dyg = dy * g
    dx = r * (dyg - xh * jnp.mean(dyg * xh, axis=-1, keepdims=True))
    return dx, jnp.sum(dy * xh, axis=0, keepdims=True)


def _onehot_row(h, n=LANES):
    return (lax.broadcasted_iota(jnp.int32, (1, n), 1) == h).astype(F32)


def _onehot_col(h, n=LANES):
    return (lax.broadcasted_iota(jnp.int32, (n, 1), 0) == h).astype(F32)


def _head_expand_matrix():
    r = lax.broadcasted_iota(jnp.int32, (LANES, ATT_OUT), 0)
    c = lax.broadcasted_iota(jnp.int32, (LANES, ATT_OUT), 1)
    return (c // ATT_HD == r).astype(F32)


def _split_bf16(x, parts):
    out = []
    for _ in range(parts - 1):
        hi = x.astype(BF16).astype(F32)
        out.append(hi)
        x = x - hi
    out.append(x)
    return out


def _expand_heads(w):
    e = _head_expand_matrix()
    return sum(_dot_nn(p, e) for p in _split_bf16(w, 2))


def _reduce_heads(x):
    e = _head_expand_matrix()
    return sum(_dot_nt(p, e) for p in _split_bf16(x, 3))


def _shift_prev(cur, halo, s, first):
    if s == 0:
        return cur
    rolled = pltpu.roll(cur, s, 0)
    hr = jnp.where(first, 0.0, pltpu.roll(halo, s, 0))
    rows = lax.broadcasted_iota(jnp.int32, halo.shape, 0)
    return jnp.concatenate([jnp.where(rows < s, hr, rolled[:SUBLANES]), rolled[SUBLANES:]], axis=0)


def _shift_next(cur, halo, s, last):
    if s == 0:
        return cur
    tb = cur.shape[0]
    rolled = pltpu.roll(cur, tb - s, 0)
    hr = jnp.where(last, 0.0, pltpu.roll(halo, SUBLANES - s, 0))
    rows = lax.broadcasted_iota(jnp.int32, halo.shape, 0)
    tail = jnp.where(rows >= SUBLANES - s, hr, rolled[tb - SUBLANES:])
    return jnp.concatenate([rolled[:tb - SUBLANES], tail], axis=0)


def _conv_prev(x, halo, w, first, taps):
    acc = None
    for i in range(taps):
        term = w[i:i + 1, :] * _shift_prev(x, halo, taps - 1 - i, first)
        acc = term if acc is None else acc + term
    return acc


def _params(sem):
    return pltpu.CompilerParams(dimension_semantics=sem, vmem_limit_bytes=VMEM_LIMIT_BYTES)


def _pick(dim, target):
    if dim <= target:
        return dim
    best = None
    for t in range(LANES, target + 1, LANES):
        if dim % t == 0:
            best = t
    assert best is not None, (dim, target)
    return best


def matmul(a, b, name, ta=False, tb=False, add=None, out_dtype=F32, tm=512, tn=1536, tk=1024):
    assert not (ta and tb)
    m, k = (a.shape[1], a.shape[0]) if ta else a.shape
    n = b.shape[0] if tb else b.shape[1]
    assert (b.shape[1] if tb else b.shape[0]) == k
    tm, tn, tk = _pick(m, tm), _pick(n, tn), _pick(k, tk)
    nk = k // tk
    dims = ((0,), (0,)) if ta else (((1,), (1,)) if tb else ((1,), (0,)))

    def body(*refs):
        if add is None:
            a_ref, b_ref, o_ref, acc = refs
        else:
            a_ref, b_ref, add_ref, o_ref, acc = refs
        kk = pl.program_id(2)

        @pl.when(kk == 0)
        def _():
            acc[...] = jnp.zeros_like(acc)

        acc[...] += _mm(a_ref[...], b_ref[...], dims)

        @pl.when(kk == nk - 1)
        def _():
            r = acc[...]
            if add is not None:
                r = r + add_ref[...].astype(F32)
            o_ref[...] = r.astype(out_dtype)

    a_spec = pl.BlockSpec((tk, tm), lambda i, j, kk: (kk, i)) if ta else pl.BlockSpec((tm, tk), lambda i, j, kk: (i, kk))
    b_spec = pl.BlockSpec((tn, tk), lambda i, j, kk: (j, kk)) if tb else pl.BlockSpec((tk, tn), lambda i, j, kk: (kk, j))
    in_specs = [a_spec, b_spec]
    args = [a, b]
    if add is not None:
        in_specs.append(pl.BlockSpec((tm, tn), lambda i, j, kk: (i, j)))
        args.append(add)
    return pl.pallas_call(
        body, name=name,
        grid=(m // tm, n // tn, nk),
        in_specs=in_specs,
        out_specs=pl.BlockSpec((tm, tn), lambda i, j, kk: (i, j)),
        out_shape=jax.ShapeDtypeStruct((m, n), out_dtype),
        scratch_shapes=[pltpu.VMEM((tm, tn), F32)],
        compiler_params=_params(("parallel", "parallel", "arbitrary")),
    )(*args)


class _Ctx:
    def __init__(self, first, last):
        self.first = first
        self.last = last


def rowwise(fn, name, rows, seq, tb, ncol, ins, params=(), outs=(), accs=()):
    assert rows % tb == 0 and seq % tb == 0 and tb % 16 == 0
    bps = seq // tb
    nrow = rows // tb
    r8 = tb // SUBLANES
    args, in_specs = [], []
    for arr, w, off, halo in ins:
        args.append(arr)
        in_specs.append(pl.BlockSpec((tb, w), lambda j, i, off=off: (i, off + j)))
        if halo == "prev":
            args.append(arr)
            in_specs.append(pl.BlockSpec((SUBLANES, w), lambda j, i, off=off: (jnp.maximum(i * r8 - 1, 0), off + j)))
        elif halo == "next":
            args.append(arr)
            in_specs.append(pl.BlockSpec(
                (SUBLANES, w), lambda j, i, off=off: (jnp.minimum((i + 1) * r8, rows // SUBLANES - 1), off + j)))
    for arr, w, off in params:
        args.append(arr)
        if w is None:
            in_specs.append(pl.BlockSpec(arr.shape, lambda j, i: (0, 0)))
        else:
            in_specs.append(pl.BlockSpec((arr.shape[0], w), lambda j, i, off=off: (0, off + j)))
    out_shape, out_specs = [], []
    for total, w, off, dt in outs:
        out_shape.append(jax.ShapeDtypeStruct((rows, total), dt))
        out_specs.append(pl.BlockSpec((tb, w), lambda j, i, off=off: (i, off + j)))
    for r, w in accs:
        out_shape.append(jax.ShapeDtypeStruct((r, ncol * w), F32))
        out_specs.append(pl.BlockSpec((r, w), lambda j, i: (0, j)))
    n_out, n_acc = len(outs), len(accs)

    def body(*refs):
        i = pl.program_id(1)
        pos = 0
        vals = []
        for _, _, _, halo in ins:
            cur = refs[pos][...]
            pos += 1
            if halo is None:
                vals.append(cur)
            else:
                vals.append((cur, refs[pos][...]))
                pos += 1
        for _ in params:
            vals.append(refs[pos][...])
            pos += 1
        ctx = _Ctx(i % bps == 0, i % bps == bps - 1)
        res = fn(ctx, *vals)
        if not isinstance(res, (tuple, list)):
            res = (res,)
        assert len(res) == n_out + n_acc
        for q in range(n_out):
            refs[pos + q][...] = res[q].astype(refs[pos + q].dtype)
        for q in range(n_acc):
            ref, val = refs[pos + n_out + q], res[n_out + q]

            @pl.when(i == 0)
            def _(ref=ref, val=val):
                ref[...] = val

            @pl.when(i != 0)
            def _(ref=ref, val=val):
                ref[...] += val

    res = pl.pallas_call(
        body, name=name,
        grid=(ncol, nrow),
        in_specs=in_specs,
        out_specs=out_specs,
        out_shape=out_shape,
        compiler_params=_params(("parallel", "arbitrary")),
    )(*args)
    return res


GROUP_W = SSD_K * SSD_P


def _tri(lower):
    r = lax.broadcasted_iota(jnp.int32, (SSD_Q, SSD_Q), 0)
    c = lax.broadcasted_iota(jnp.int32, (SSD_Q, SSD_Q), 1)
    return r >= c if lower else r <= c


def _group_masks():
    lane = lax.broadcasted_iota(jnp.int32, (1, GROUP_W), 1) // SSD_P
    row = lax.broadcasted_iota(jnp.int32, (GROUP_W, 1), 0) // SSD_P
    return [lane == k for k in range(SSD_K)], [row == k for k in range(SSD_K)]


def _per_head(masks, vals):
    out = jnp.where(masks[0], vals[0], 0.0)
    for m, v in zip(masks[1:], vals[1:]):
        out = jnp.where(m, v, out)
    return out


def _headsum(prod, g):
    j = lax.broadcasted_iota(jnp.int32, (GROUP_W, LANES), 0) // SSD_P
    lane = lax.broadcasted_iota(jnp.int32, (GROUP_W, LANES), 1)
    e = (lane == g * SSD_K + j).astype(F32)
    return sum(_dot_nn(p, e) for p in _split_bf16(prod, 2))


def ssd_fwd(xact, dtraw, dt_bias, a_log, d_skip, n_seq, seq):
    nc = seq // SSD_Q
    rows = n_seq * seq

    def body(xact_ref, dtraw_ref, bias_ref, alog_ref, dskip_ref, y_ref, sin_ref, state, cs_s, cst_s, dt_s):
        c = pl.program_id(1)

        @pl.when(c == 0)
        def _():
            state[...] = jnp.zeros_like(state)

        sin_ref[0] = state[...]
        dt = _softplus(dtraw_ref[...] + bias_ref[...])
        a = dt * (-jnp.exp(alog_ref[...]))
        cs = _dot_f32(_tri(True).astype(F32), a)
        cs_s[...] = cs
        cst_s[...] = cs.T
        dt_s[...] = dt
        causal = _tri(True)
        lane_masks, row_masks = _group_masks()
        for g in range(SSD_G):
            heads = [g * SSD_K + k for k in range(SSD_K)]
            bg = xact_ref[:, pl.ds(D_INNER + g * SSD_N, SSD_N)]
            cg = xact_ref[:, pl.ds(D_INNER + (SSD_G + g) * SSD_N, SSD_N)]
            xg = xact_ref[:, pl.ds(g * GROUP_W, GROUP_W)]
            cols = [cs_s[:, pl.ds(h, 1)] for h in heads]
            lasts = [cs_s[pl.ds(SSD_Q - 1, 1), pl.ds(h, 1)] for h in heads]
            xdg = xg * _per_head(lane_masks, [dt_s[:, pl.ds(h, 1)] for h in heads])
            sg = state[g]
            gm = _dot_nt(cg, bg)
            y = (_per_head(lane_masks, [jnp.exp(c_) for c_ in cols]) * _dot_nt(cg, sg)
                 + _per_head(lane_masks, [dskip_ref[:, pl.ds(h, 1)] for h in heads]) * xg)
            for k, h in enumerate(heads):
                decay = jnp.exp(jnp.where(causal, cols[k] - cst_s[pl.ds(h, 1), :], NEG))
                y = y + _dot_nn(gm * decay, jnp.where(lane_masks[k], xdg, 0.0))
            y_ref[:, pl.ds(g * GROUP_W, GROUP_W)] = y
            w = _per_head(lane_masks, [jnp.exp(l_ - c_) for l_, c_ in zip(lasts, cols)])
            state[g] = _per_head(row_masks, [jnp.exp(l_) for l_ in lasts]) * sg + _dot_tn(w * xdg, bg)

    vec = pl.BlockSpec((1, LANES), lambda b, c: (0, 0))
    return pl.pallas_call(
        body, name="ssd_fwd",
        grid=(n_seq, nc),
        in_specs=[pl.BlockSpec((SSD_Q, CONV_DIM), lambda b, c: (b * nc + c, 0)),
                  pl.BlockSpec((SSD_Q, LANES), lambda b, c: (b * nc + c, 0)), vec, vec, vec],
        out_specs=[pl.BlockSpec((SSD_Q, D_INNER), lambda b, c: (b * nc + c, 0)),
                   pl.BlockSpec((1, SSD_G, GROUP_W, SSD_N), lambda b, c: (b * nc + c, 0, 0, 0))],
        out_shape=[jax.ShapeDtypeStruct((rows, D_INNER), F32),
                   jax.ShapeDtypeStruct((n_seq * nc, SSD_G, GROUP_W, SSD_N), F32)],
        scratch_shapes=[pltpu.VMEM((SSD_G, GROUP_W, SSD_N), F32), pltpu.VMEM((SSD_Q, LANES), F32),
                        pltpu.VMEM((LANES, SSD_Q), F32), pltpu.VMEM((SSD_Q, LANES), F32)],
        compiler_params=_params(("arbitrary", "arbitrary")),
    )(xact, dtraw, dt_bias, a_log, d_skip)


def ssd_bwd(xact, dtraw, dt_bias, a_log, d_skip, sin, dy, n_seq, seq):
    nc = seq // SSD_Q
    rows = n_seq * seq

    def body(xact_ref, dtraw_ref, bias_ref, alog_ref, dskip_ref, sin_ref, dy_ref,
             dx_ref, ddt_ref, dbias_ref, dalog_ref, ddskip_ref, dstate, cs_s, cst_s, dt_s):
        b, c = pl.program_id(0), pl.program_id(1)

        @pl.when(c == 0)
        def _():
            dstate[...] = jnp.zeros_like(dstate)

        pre = dtraw_ref[...] + bias_ref[...]
        dt = _softplus(pre)
        a_neg = -jnp.exp(alog_ref[...])
        cs = _dot_f32(_tri(True).astype(F32), dt * a_neg)
        cs_s[...] = cs
        cst_s[...] = cs.T
        dt_s[...] = dt
        causal, anti = _tri(True), _tri(False)
        is_last_row = lax.broadcasted_iota(jnp.int32, (SSD_Q, 1), 0) == SSD_Q - 1
        lane_masks, row_masks = _group_masks()
        dcs_cf = jnp.zeros((SSD_Q, LANES), F32)
        dcs_rf = jnp.zeros((LANES, SSD_Q), F32)
        ddt_cf = jnp.zeros((SSD_Q, LANES), F32)
        dd_vec = jnp.zeros((1, LANES), F32)
        dlast_vec = jnp.zeros((1, LANES), F32)
        for g in range(SSD_G):
            heads = [g * SSD_K + k for k in range(SSD_K)]
            bg = xact_ref[:, pl.ds(D_INNER + g * SSD_N, SSD_N)]
            cg = xact_ref[:, pl.ds(D_INNER + (SSD_G + g) * SSD_N, SSD_N)]
            xg = xact_ref[:, pl.ds(g * GROUP_W, GROUP_W)]
            dyg = dy_ref[:, pl.ds(g * GROUP_W, GROUP_W)]
            cols = [cs_s[:, pl.ds(h, 1)] for h in heads]
            rws = [cst_s[pl.ds(h, 1), :] for h in heads]
            lasts = [cs_s[pl.ds(SSD_Q - 1, 1), pl.ds(h, 1)] for h in heads]
            e_lasts = [jnp.exp(l_) for l_ in lasts]
            dtg = _per_head(lane_masks, [dt_s[:, pl.ds(h, 1)] for h in heads])
            dskg = _per_head(lane_masks, [dskip_ref[:, pl.ds(h, 1)] for h in heads])
            e_col = _per_head(lane_masks, [jnp.exp(c_) for c_ in cols])
            w = _per_head(lane_masks, [jnp.exp(l_ - c_) for l_, c_ in zip(lasts, cols)])
            xdg = xg * dtg
            sg = sin_ref[0, g]
            dsn = dstate[g]
            gm = _dot_nt(cg, bg)
            gmt = _dot_nt(bg, cg)
            y_off = e_col * _dot_nt(cg, sg)
            d_cs = e_col * dyg
            dcg = _dot_nn(d_cs, sg)
            dsp = _dot_tn(d_cs, cg) + _per_head(row_masks, e_lasts) * dsn
            dbg = _dot_nn(w * xdg, dsn)
            dtt = _dot_nt(bg, dsn)
            dxd = w * dtt
            dw = _headsum(dtt * xdg * w, g)
            dcs_cf = dcs_cf + _headsum(dyg * y_off, g) - dw
            dlast_vec = dlast_vec + jnp.sum(dw, axis=0, keepdims=True)
            dsn_s = dsn * sg
            dgm = jnp.zeros((SSD_Q, SSD_Q), F32)
            for k, h in enumerate(heads):
                seg = cols[k] - rws[k]
                decay = jnp.exp(jnp.where(causal, seg, NEG))
                decay_t = jnp.exp(jnp.where(anti, -seg, NEG))
                dyk = jnp.where(lane_masks[k], dyg, 0.0)
                dm = _dot_nt(dyk, xdg)
                dxd = dxd + _dot_nn(gmt * decay_t, dyk)
                dseg = dm * gm * decay
                dgm = dgm + dm * decay
                oh_r = _onehot_row(h)
                dcs_cf = dcs_cf + jnp.sum(dseg, axis=1, keepdims=True) * oh_r
                dcs_rf = dcs_rf - _onehot_col(h) * jnp.sum(dseg, axis=0, keepdims=True)
                dlast_vec = dlast_vec + jnp.sum(jnp.where(row_masks[k], dsn_s, 0.0), keepdims=True) * e_lasts[k] * oh_r
            dx_ref[:, pl.ds(g * GROUP_W, GROUP_W)] = dxd * dtg + dskg * dyg
            ddt_cf = ddt_cf + _headsum(dxd * xg, g)
            dyx = jnp.broadcast_to(jnp.sum(dyg * xg, axis=0, keepdims=True), (SUBLANES, GROUP_W))
            dd_vec = dd_vec + _headsum(dyx, g)[0:1]
            dstate[g] = dsp
            dx_ref[:, pl.ds(D_INNER + g * SSD_N, SSD_N)] = dbg + _dot_tn(dgm, cg)
            dx_ref[:, pl.ds(D_INNER + (SSD_G + g) * SSD_N, SSD_N)] = dcg + _dot_nn(dgm, bg)
        dcs = dcs_cf + dcs_rf.T + jnp.where(is_last_row, dlast_vec, 0.0)
        da = _dot_f32(_tri(False).astype(F32), dcs)
        ddt = ddt_cf + da * a_neg
        ddtraw = ddt * _sigmoid(pre)
        ddt_ref[...] = ddtraw.astype(ddt_ref.dtype)
        dbias = jnp.sum(ddtraw, axis=0, keepdims=True)
        dalog = jnp.sum(da * dt, axis=0, keepdims=True) * a_neg
        first_step = jnp.logical_and(b == 0, c == 0)

        @pl.when(first_step)
        def _():
            dbias_ref[...] = dbias
            dalog_ref[...] = dalog
            ddskip_ref[...] = dd_vec

        @pl.when(jnp.logical_not(first_step))
        def _():
            dbias_ref[...] += dbias
            dalog_ref[...] += dalog
            ddskip_ref[...] += dd_vec

    def rowblk(b, c):
        return b * nc + (nc - 1 - c)

    vec = pl.BlockSpec((1, LANES), lambda b, c: (0, 0))
    return pl.pallas_call(
        body, name="ssd_bwd",
        grid=(n_seq, nc),
        in_specs=[pl.BlockSpec((SSD_Q, CONV_DIM), lambda b, c: (rowblk(b, c), 0)),
                  pl.BlockSpec((SSD_Q, LANES), lambda b, c: (rowblk(b, c), 0)), vec, vec, vec,
                  pl.BlockSpec((1, SSD_G, GROUP_W, SSD_N), lambda b, c: (rowblk(b, c), 0, 0, 0)),
                  pl.BlockSpec((SSD_Q, D_INNER), lambda b, c: (rowblk(b, c), 0))],
        out_specs=[pl.BlockSpec((SSD_Q, CONV_DIM), lambda b, c: (rowblk(b, c), 0)),
                   pl.BlockSpec((SSD_Q, LANES), lambda b, c: (rowblk(b, c), 0)), vec, vec, vec],
        out_shape=[jax.ShapeDtypeStruct((rows, CONV_DIM), F32), jax.ShapeDtypeStruct((rows, LANES), BF16),
                   jax.ShapeDtypeStruct((1, LANES), F32), jax.ShapeDtypeStruct((1, LANES), F32),
                   jax.ShapeDtypeStruct((1, LANES), F32)],
        scratch_shapes=[pltpu.VMEM((SSD_G, GROUP_W, SSD_N), F32), pltpu.VMEM((SSD_Q, LANES), F32),
                        pltpu.VMEM((LANES, SSD_Q), F32), pltpu.VMEM((SSD_Q, LANES), F32)],
        compiler_params=_params(("arbitrary", "arbitrary")),
    )(xact, dtraw, dt_bias, a_log, d_skip, sin, dy)


def _band_masks():
    qi = lax.broadcasted_iota(jnp.int32, (ATT_BLK, ATT_BLK), 0)
    ki = lax.broadcasted_iota(jnp.int32, (ATT_BLK, ATT_BLK), 1)
    return qi >= ki, qi <= ki


def attn_fwd(qkv, gq, gk, n_seq, seq, dil, name):
    length = seq // dil
    nb = length // ATT_BLK
    w3 = 3 * ATT_OUT

    def body(cur_ref, prev_ref, gq_ref, gk_ref, o_ref, lse_ref):
        n = pl.program_id(2)
        m_cur, m_prev = _band_masks()
        m_prev = jnp.logical_and(m_prev, n > 0)
        gq_, gk_ = gq_ref[...], gk_ref[...]
        lse_blk = jnp.zeros((ATT_BLK, LANES), F32)
        for h in range(ATT_H):
            q = _rms_fwd(cur_ref[0, :, pl.ds(h * ATT_HD, ATT_HD)], gq_)
            kc = _rms_fwd(cur_ref[0, :, pl.ds(ATT_OUT + h * ATT_HD, ATT_HD)], gk_)
            kp = _rms_fwd(prev_ref[0, :, pl.ds(ATT_OUT + h * ATT_HD, ATT_HD)], gk_)
            vc = cur_ref[0, :, pl.ds(2 * ATT_OUT + h * ATT_HD, ATT_HD)]
            vp = prev_ref[0, :, pl.ds(2 * ATT_OUT + h * ATT_HD, ATT_HD)]
            sc = jnp.where(m_cur, _dot_nt(q, kc) * ATT_SCALE, NEG)
            sp = jnp.where(m_prev, _dot_nt(q, kp) * ATT_SCALE, NEG)
            mx = jnp.maximum(jnp.max(sc, axis=1, keepdims=True), jnp.max(sp, axis=1, keepdims=True))
            pc = jnp.exp(sc - mx)
            pp = jnp.exp(sp - mx)
            den = jnp.sum(pc, axis=1, keepdims=True) + jnp.sum(pp, axis=1, keepdims=True)
            o_ref[0, :, pl.ds(h * ATT_HD, ATT_HD)] = (_dot_nn(pc, vc) + _dot_nn(pp, vp)) / den
            lse_blk = lse_blk + (mx + jnp.log(den)) * _onehot_row(h)
        lse_ref[0] = lse_blk

    view = qkv.reshape(n_seq, length, dil * w3)
    gspec = pl.BlockSpec((1, ATT_HD), lambda b, r, n: (0, 0))
    o, lse = pl.pallas_call(
        body, name=name,
        grid=(n_seq, dil, nb),
        in_specs=[pl.BlockSpec((1, ATT_BLK, w3), lambda b, r, n: (b, n, r)),
                  pl.BlockSpec((1, ATT_BLK, w3), lambda b, r, n: (b, jnp.maximum(n - 1, 0), r)), gspec, gspec],
        out_specs=[pl.BlockSpec((1, ATT_BLK, ATT_OUT), lambda b, r, n: (b, n, r)),
                   pl.BlockSpec((1, ATT_BLK, LANES), lambda b, r, n: (b, n, r))],
        out_shape=[jax.ShapeDtypeStruct((n_seq, length, dil * ATT_OUT), F32),
                   jax.ShapeDtypeStruct((n_seq, length, dil * LANES), F32)],
        compiler_params=_params(("parallel", "parallel", "arbitrary")),
    )(view, view, gq, gk)
    return o.reshape(n_seq * seq, ATT_OUT), lse.reshape(n_seq * seq, LANES)


def attn_bwd(qkv, gq, gk, do, lse, wts, rsum, n_seq, seq, dil, name):
    length = seq // dil
    nb = length // ATT_BLK
    w3 = 3 * ATT_OUT

    def body(prev_ref, cur_ref, nxt_ref, gq_ref, gk_ref, do_c, do_x, lse_c, lse_x, wt_c, wt_x, rs_c, rs_x,
             dqkv_ref, dgq_ref, dgk_ref):
        b, r, n = pl.program_id(0), pl.program_id(1), pl.program_id(2)
        m_cur, m_band = _band_masks()
        m_prev = jnp.logical_and(m_band, n > 0)
        m_next = jnp.logical_and(m_band, n < nb - 1)
        gq_, gk_ = gq_ref[...], gk_ref[...]
        dgq = jnp.zeros((1, ATT_HD), F32)
        dgk = jnp.zeros((1, ATT_HD), F32)
        for h in range(ATT_H):
            hs = pl.ds(h * ATT_HD, ATT_HD)
            ks = pl.ds(ATT_OUT + h * ATT_HD, ATT_HD)
            vs = pl.ds(2 * ATT_OUT + h * ATT_HD, ATT_HD)
            one = pl.ds(h, 1)
            q_raw, k_raw = cur_ref[0, :, hs], cur_ref[0, :, ks]
            qc = _rms_fwd(q_raw, gq_)
            qx = _rms_fwd(nxt_ref[0, :, hs], gq_)
            kc = _rms_fwd(k_raw, gk_)
            kp = _rms_fwd(prev_ref[0, :, ks], gk_)
            vc, vp = cur_ref[0, :, vs], prev_ref[0, :, vs]
            wc, wx = wt_c[0, :, one], wt_x[0, :, one]
            dog_c = do_c[0, :, hs] * wc
            dog_x = do_x[0, :, hs] * wx
            dl_c = -wc * rs_c[0, :, one]
            dl_x = -wx * rs_x[0, :, one]
            lc, lx = lse_c[0, :, one], lse_x[0, :, one]
            p_cc = jnp.exp(jnp.where(m_cur, _dot_nt(qc, kc) * ATT_SCALE - lc, NEG))
            p_cp = jnp.exp(jnp.where(m_prev, _dot_nt(qc, kp) * ATT_SCALE - lc, NEG))
            p_xc = jnp.exp(jnp.where(m_next, _dot_nt(qx, kc) * ATT_SCALE - lx, NEG))
            ds_cc = p_cc * (_dot_nt(dog_c, vc) + dl_c)
            ds_cp = p_cp * (_dot_nt(dog_c, vp) + dl_c)
            ds_xc = p_xc * (_dot_nt(dog_x, vc) + dl_x)
            dqn = (_dot_nn(ds_cc, kc) + _dot_nn(ds_cp, kp)) * ATT_SCALE
            dkn = (_dot_tn(ds_cc, qc) + _dot_tn(ds_xc, qx)) * ATT_SCALE
            dv = _dot_tn(p_cc, dog_c) + _dot_tn(p_xc, dog_x)
            dq, dgq_h = _rms_bwd(q_raw, gq_, dqn)
            dk, dgk_h = _rms_bwd(k_raw, gk_, dkn)
            dqkv_ref[0, :, hs] = dq.astype(dqkv_ref.dtype)
            dqkv_ref[0, :, ks] = dk.astype(dqkv_ref.dtype)
            dqkv_ref[0, :, vs] = dv.astype(dqkv_ref.dtype)
            dgq = dgq + dgq_h
            dgk = dgk + dgk_h
        first_step = jnp.logical_and(jnp.logical_and(b == 0, r == 0), n == 0)

        @pl.when(first_step)
        def _():
            dgq_ref[...] = dgq
            dgk_ref[...] = dgk

        @pl.when(jnp.logical_not(first_step))
        def _():
            dgq_ref[...] += dgq
            dgk_ref[...] += dgk

    view = qkv.reshape(n_seq, length, dil * w3)
    do_v = do.reshape(n_seq, length, dil * ATT_OUT)
    lse_v = lse.reshape(n_seq, length, dil * LANES)
    wts_v = wts.reshape(n_seq, length, dil * LANES)
    rs_v = rsum.reshape(n_seq, length, dil * LANES)

    def at(shift, width):
        if shift < 0:
            return pl.BlockSpec((1, ATT_BLK, width), lambda b, r, n: (b, jnp.maximum(n - 1, 0), r))
        if shift > 0:
            return pl.BlockSpec((1, ATT_BLK, width), lambda b, r, n: (b, jnp.minimum(n + 1, nb - 1), r))
        return pl.BlockSpec((1, ATT_BLK, width), lambda b, r, n: (b, n, r))

    gspec = pl.BlockSpec((1, ATT_HD), lambda b, r, n: (0, 0))
    dqkv, dgq, dgk = pl.pallas_call(
        body, name=name,
        grid=(n_seq, dil, nb),
        in_specs=[at(-1, w3), at(0, w3), at(1, w3), gspec, gspec,
                  at(0, ATT_OUT), at(1, ATT_OUT), at(0, LANES), at(1, LANES),
                  at(0, LANES), at(1, LANES), at(0, LANES), at(1, LANES)],
        out_specs=[at(0, w3), gspec, gspec],
        out_shape=[jax.ShapeDtypeStruct((n_seq, length, dil * w3), BF16),
                   jax.ShapeDtypeStruct((1, ATT_HD), F32), jax.ShapeDtypeStruct((1, ATT_HD), F32)],
        compiler_params=_params(("arbitrary", "arbitrary", "arbitrary")),
    )(view, view, view, gq, gk, do_v, do_v, lse_v, lse_v, wts_v, wts_v, rs_v, rs_v)
    return dqkv.reshape(n_seq * seq, w3), dgq, dgk


def exchange(items, name):
    n = len(items)

    def body(*refs):
        in_refs, out_refs = refs[:n], refs[n:2 * n]
        send_sems, recv_sems, local_sems = refs[2 * n:]
        x, y, c = lax.axis_index("x"), lax.axis_index("y"), lax.axis_index("c")
        me = 4 * x + 2 * y + c

        def peer(k):
            px = 1 - x if k & 4 else x
            py = 1 - y if k & 2 else y
            pc = 1 - c if k & 1 else c
            return (px, py, pc), 4 * px + 2 * py + pc

        def remote(t, k):
            dev, pid = peer(k)
            src = in_refs[t] if items[t][1] == "gather" else in_refs[t].at[pid]
            return pltpu.make_async_remote_copy(
                src_ref=src, dst_ref=out_refs[t].at[me], send_sem=send_sems.at[t, k], recv_sem=recv_sems.at[t, k],
                device_id=dev, device_id_type=pl.DeviceIdType.MESH)

        def arrival(t, k):
            dev, pid = peer(k)
            src = in_refs[t] if items[t][1] == "gather" else in_refs[t].at[pid]
            return pltpu.make_async_remote_copy(
                src_ref=src, dst_ref=out_refs[t].at[pid], send_sem=send_sems.at[t, k], recv_sem=recv_sems.at[t, k],
                device_id=dev, device_id_type=pl.DeviceIdType.MESH)

        def own(t):
            src = in_refs[t] if items[t][1] == "gather" else in_refs[t].at[me]
            return pltpu.make_async_copy(src, out_refs[t].at[me], local_sems.at[t])

        for t in range(n):
            own(t).start()
            for k in range(1, N_DEV):
                remote(t, k).start()
        for t in range(n):
            for k in range(1, N_DEV):
                arrival(t, k).wait_recv()
        for t in range(n):
            for k in range(1, N_DEV):
                remote(t, k).wait_send()
            own(t).wait()

    out_shape = []
    for arr, mode in items:
        shp = arr.shape if mode == "gather" else arr.shape[1:]
        out_shape.append(jax.ShapeDtypeStruct((N_DEV,) + tuple(shp), arr.dtype))
    anyspec = pl.BlockSpec(memory_space=pl.ANY)
    return pl.pallas_call(
        body, name=name,
        in_specs=[anyspec] * n,
        out_specs=[anyspec] * n,
        out_shape=out_shape,
        scratch_shapes=[pltpu.SemaphoreType.DMA((n, N_DEV)), pltpu.SemaphoreType.DMA((n, N_DEV)),
                        pltpu.SemaphoreType.DMA((n,))],
    )(*[a for a, _ in items])


def adamw(parts, w, m, v, name):
    r, c = w.shape
    rb = r if r <= 512 else (128 if c > 1024 else 256)
    assert r % rb == 0

    def body(p_ref, w_ref, m_ref, v_ref, g_out, d_out, m_out, v_out):
        g = p_ref[0].astype(F32)
        for i in range(1, N_DEV):
            g = g + p_ref[i].astype(F32)
        m_new = ADAM_B1 * m_ref[...] + (1.0 - ADAM_B1) * g
        v_new = ADAM_B2 * v_ref[...] + (1.0 - ADAM_B2) * (g * g)
        m_hat = m_new / (1.0 - ADAM_B1 ** ADAM_STEP)
        v_hat = v_new / (1.0 - ADAM_B2 ** ADAM_STEP)
        g_out[...] = g
        d_out[...] = -ADAM_LR * (m_hat / (jnp.sqrt(v_hat) + ADAM_EPS) + ADAM_WD * w_ref[...])
        m_out[...] = m_new
        v_out[...] = v_new

    blk = pl.BlockSpec((rb, c), lambda i: (i, 0))
    return pl.pallas_call(
        body, name=name,
        grid=(r // rb,),
        in_specs=[pl.BlockSpec((N_DEV, rb, c), lambda i: (0, i, 0)), blk, blk, blk],
        out_specs=[blk] * 4,
        out_shape=[jax.ShapeDtypeStruct((r, c), F32)] * 4,
        compiler_params=_params(("parallel",)),
    )(parts, w, m, v)


def _pad_lanes(vec, n=LANES):
    return jnp.pad(vec, ((0, 0), (0, n - vec.shape[1])))


def local_step(x, target, w):
    n_seq, seq, _ = x.shape
    rows = n_seq * seq
    x = x.reshape(rows, D_MODEL)
    target = target.reshape(rows, D_MODEL)
    mx = lambda a: a.astype(MXU)

    splits = [sum(IN_WIDTHS[:i]) for i in range(len(IN_WIDTHS) + 1)]
    w_in = w["w_in"]
    part = lambda i: w_in[:, splits[i]:splits[i + 1]]
    w_z, w_xbc, w_gs, w_ga = mx(part(0)), mx(part(1)), mx(part(6)), mx(part(7))
    w_dt = mx(_pad_lanes(part(2)))
    w_qkv = [mx(jnp.concatenate([part(3 + t)[:, g * ATT_OUT:(g + 1) * ATT_OUT] for t in range(3)], axis=1))
             for g in range(ATT_GROUPS)]
    w_sp, w_ap, w_o, w_d = mx(w["w_ssd_proj"]), mx(w["w_attn_proj"]), mx(w["w_out"]), mx(w["w_down"])
    w_ug, w_uv = mx(w["w_up"][:, :D_FF]), mx(w["w_up"][:, D_FF:])
    conv_w, conv_b = w["ssd_conv_w"], w["ssd_conv_b"]
    fconv_w, fconv_b = w["ffn_conv_w"], w["ffn_conv_b"]
    dt_bias, a_log, d_skip = _pad_lanes(w["dt_bias"]), _pad_lanes(w["a_log"]), _pad_lanes(w["d_skip"])
    g1, g2, gn, gq, gk = w["norm1_g"], w["norm2_g"], w["ssd_norm_g"], w["q_norm_g"], w["k_norm_g"]

    tb = min(512, seq)
    tbm = min(256, seq)
    cw = 1024
    rw = lambda fn, name, ncol, ins, params=(), outs=(), accs=(), tb_=tb: rowwise(
        fn, name, rows, seq, tb_, ncol, ins, params, outs, accs)

    (h,) = rw(lambda ctx, xv, g: _rms_fwd(xv, g), "rms1_fwd", 1, [(x, D_MODEL, 0, None)], [(g1, None, 0)],
              [(D_MODEL, D_MODEL, 0, MXU)])
    z = matmul(h, w_z, "mm_z")
    xbc = matmul(h, w_xbc, "mm_xbc")
    dtraw = matmul(h, w_dt, "mm_dt")
    qkv = [matmul(h, w_qkv[g], f"mm_qkv{g}") for g in range(ATT_GROUPS)]
    gs = matmul(h, w_gs, "mm_gs")
    ga = matmul(h, w_ga, "mm_ga")

    def conv_silu(ctx, xh, wv, bv):
        return _silu(bv + _conv_prev(xh[0], xh[1], wv, ctx.first, SSD_CONV))

    (xact,) = rw(conv_silu, "ssd_conv_fwd", CONV_DIM // cw, [(xbc, cw, 0, "prev")],
                 [(conv_w, cw, 0), (conv_b, cw, 0)], [(CONV_DIM, cw, 0, F32)])
    y, sin = ssd_fwd(xact, dtraw, dt_bias, a_log, d_skip, n_seq, seq)

    def gated_norm(ctx, yv, zv, g):
        yz = yv * _silu(zv)
        return jnp.concatenate([_rms_fwd(yz[:, i:i + NORM_GROUP], g[:, i:i + NORM_GROUP])
                                for i in range(0, cw, NORM_GROUP)], axis=1)

    (y_ssd,) = rw(gated_norm, "ssd_post_fwd", D_INNER // cw, [(y, cw, 0, None), (z, cw, 0, None)], [(gn, cw, 0)],
                  [(D_INNER, cw, 0, MXU)])

    att = [attn_fwd(qkv[g], gq, gk, n_seq, seq, ATT_DILATIONS[g], f"attn_fwd{g}") for g in range(ATT_GROUPS)]

    def combine(ctx, o0, o1, o2, l0, l1, l2):
        mxl = jnp.maximum(jnp.maximum(l0, l1), l2)
        e = [jnp.exp(l - mxl) for l in (l0, l1, l2)]
        inv = 1.0 / (e[0] + e[1] + e[2])
        ws = [ei * inv for ei in e]
        out = sum(_expand_heads(wi) * oi for wi, oi in zip(ws, (o0, o1, o2)))
        return (out, *ws)

    y_attn, wt0, wt1, wt2 = rw(
        combine, "attn_combine", 1,
        [(att[g][0], ATT_OUT, 0, None) for g in range(3)] + [(att[g][1], LANES, 0, None) for g in range(3)], [],
        [(ATT_OUT, ATT_OUT, 0, F32)] + [(LANES, LANES, 0, F32)] * 3)
    wts = (wt0, wt1, wt2)

    ps = matmul(y_ssd, w_sp, "mm_ssd_proj")
    pa = matmul(y_attn, w_ap, "mm_attn_proj")
    (merged,) = rw(lambda ctx, a, b, c, d: _sigmoid(c) * a + _sigmoid(d) * b, "merge_fwd", D_MODEL // cw,
                   [(ps, cw, 0, None), (pa, cw, 0, None), (gs, cw, 0, None), (ga, cw, 0, None)], [],
                   [(D_MODEL, cw, 0, MXU)])
    x1 = matmul(merged, w_o, "mm_out", add=x)
    (h2,) = rw(lambda ctx, xv, g: _rms_fwd(xv, g), "rms2_fwd", 1, [(x1, D_MODEL, 0, None)], [(g2, None, 0)],
               [(D_MODEL, D_MODEL, 0, MXU)])
    up_g = matmul(h2, w_ug, "mm_up_g")
    up_v = matmul(h2, w_uv, "mm_up_v")
    fw = D_FF // 2
    nfc = D_FF // fw

    def mlp_act(ctx, ug, uv, wg, wv, bg, bv):
        cg = bg + _conv_prev(ug[0], ug[1], wg, ctx.first, FFN_CONV)
        cv = bv + _conv_prev(uv[0], uv[1], wv, ctx.first, FFN_CONV)
        return _silu(cg) * cv

    (act,) = rw(mlp_act, "mlp_act_fwd", nfc, [(up_g, fw, 0, "prev"), (up_v, fw, 0, "prev")],
                [(fconv_w, fw, 0), (fconv_w, fw, nfc), (fconv_b, fw, 0), (fconv_b, fw, nfc)], [(D_FF, fw, 0, MXU)],
                tb_=tbm)
    x2 = matmul(act, w_d, "mm_down", add=x1)

    def loss_fn(ctx, xv, tv):
        d = xv - tv
        return d * (1.0 / D_MODEL), jnp.sum(d * d, axis=0, keepdims=True)

    dx2, sq = rw(loss_fn, "loss", 1, [(x2, D_MODEL, 0, None), (target, D_MODEL, 0, None)], [],
                 [(D_MODEL, D_MODEL, 0, F32)], [(1, D_MODEL)])

    grads = {}
    dact = matmul(dx2, w_d, "mm_d_act", tb=True)
    grads["w_down"] = matmul(act, dx2, "mm_dw_down", ta=True)

    def mlp_act_bwd(ctx, da, ug, uv, wg, wv, bg, bv):
        cg = bg + _conv_prev(ug[0], ug[1], wg, ctx.first, FFN_CONV)
        cv = bv + _conv_prev(uv[0], uv[1], wv, ctx.first, FFN_CONV)
        return da * cv * _silu_grad(cg), da * _silu(cg)

    dcg, dcv = rw(mlp_act_bwd, "mlp_act_bwd", nfc, [(dact, fw, 0, None), (up_g, fw, 0, "prev"), (up_v, fw, 0, "prev")],
                  [(fconv_w, fw, 0), (fconv_w, fw, nfc), (fconv_b, fw, 0), (fconv_b, fw, nfc)],
                  [(D_FF, fw, 0, F32), (D_FF, fw, 0, F32)], tb_=tbm)

    def conv_bwd(taps):
        def fn(ctx, dpre, xin, wv):
            dpre_c, dpre_h = dpre
            x_c, x_h = xin
            dx = None
            dws = []
            for i in range(taps):
                term = wv[i:i + 1, :] * _shift_next(dpre_c, dpre_h, taps - 1 - i, ctx.last)
                dx = term if dx is None else dx + term
                dws.append(jnp.sum(dpre_c * _shift_prev(x_c, x_h, taps - 1 - i, ctx.first), axis=0, keepdims=True))
            return dx, jnp.concatenate(dws, axis=0), jnp.sum(dpre_c, axis=0, keepdims=True)
        return fn

    dup_g, dfw_g, dfb_g = rw(conv_bwd(FFN_CONV), "ffn_conv_bwd_g", nfc, [(dcg, fw, 0, "next"), (up_g, fw, 0, "prev")],
                             [(fconv_w, fw, 0)], [(D_FF, fw, 0, MXU)], [(FFN_CONV, fw), (1, fw)], tb_=tbm)
    dup_v, dfw_v, dfb_v = rw(conv_bwd(FFN_CONV), "ffn_conv_bwd_v", nfc, [(dcv, fw, 0, "next"), (up_v, fw, 0, "prev")],
                             [(fconv_w, fw, nfc)], [(D_FF, fw, 0, MXU)], [(FFN_CONV, fw), (1, fw)], tb_=tbm)
    grads["ffn_conv_w"] = jnp.concatenate([dfw_g, dfw_v], axis=1)
    grads["ffn_conv_b"] = jnp.concatenate([dfb_g, dfb_v], axis=1)
    dh2 = matmul(dup_g, w_ug, "mm_dh2_g", tb=True)
    dh2 = matmul(dup_v, w_uv, "mm_dh2_v", tb=True, add=dh2)
    grads["w_up"] = jnp.concatenate([matmul(h2, dup_g, "mm_dw_up_g", ta=True),
                                     matmul(h2, dup_v, "mm_dw_up_v", ta=True)], axis=1)

    def rms_bwd_fn(ctx, xv, dh_, dres, g):
        dxv, dg = _rms_bwd(xv, g, dh_)
        return dres + dxv, dg

    dx1, grads["norm2_g"] = rw(rms_bwd_fn, "rms2_bwd", 1,
                               [(x1, D_MODEL, 0, None), (dh2, D_MODEL, 0, None), (dx2, D_MODEL, 0, None)],
                               [(g2, None, 0)], [(D_MODEL, D_MODEL, 0, F32)], [(1, D_MODEL)])

    dmerged = matmul(dx1, w_o, "mm_d_merged", tb=True)
    grads["w_out"] = matmul(merged, dx1, "mm_dw_out", ta=True)

    def merge_bwd(ctx, dm, a, b, c, d):
        sc, sd = _sigmoid(c), _sigmoid(d)
        return dm * sc, dm * sd, dm * a * sc * (1.0 - sc), dm * b * sd * (1.0 - sd)

    dps, dpa, dgs, dga = rw(merge_bwd, "merge_bwd", D_MODEL // cw,
                            [(dmerged, cw, 0, None), (ps, cw, 0, None), (pa, cw, 0, None), (gs, cw, 0, None),
                             (ga, cw, 0, None)], [], [(D_MODEL, cw, 0, MXU)] * 4)
    dy_ssd = matmul(dps, w_sp, "mm_d_y_ssd", tb=True)
    grads["w_ssd_proj"] = matmul(y_ssd, dps, "mm_dw_ssd_proj", ta=True)
    dy_attn = matmul(dpa, w_ap, "mm_d_y_attn", tb=True)
    grads["w_attn_proj"] = matmul(y_attn, dpa, "mm_dw_attn_proj", ta=True)

    (rsum,) = rw(lambda ctx, a, b: _reduce_heads(a * b), "attn_rsum", 1,
                 [(dy_attn, ATT_OUT, 0, None), (y_attn, ATT_OUT, 0, None)], [], [(LANES, LANES, 0, F32)])
    dqkv, dgq, dgk = [], 0.0, 0.0
    for g in range(ATT_GROUPS):
        d_, a_, b_ = attn_bwd(qkv[g], gq, gk, dy_attn, att[g][1], wts[g], rsum, n_seq, seq, ATT_DILATIONS[g],
                              f"attn_bwd{g}")
        dqkv.append(d_)
        dgq, dgk = dgq + a_, dgk + b_
    grads["q_norm_g"], grads["k_norm_g"] = dgq, dgk

    def gated_norm_bwd(ctx, dyn, yv, zv, g):
        sz = _silu(zv)
        yz = yv * sz
        dyz, dgs_ = [], []
        for i in range(0, cw, NORM_GROUP):
            a, b = _rms_bwd(yz[:, i:i + NORM_GROUP], g[:, i:i + NORM_GROUP], dyn[:, i:i + NORM_GROUP])
            dyz.append(a)
            dgs_.append(b)
        dyz = jnp.concatenate(dyz, axis=1)
        return dyz * sz, dyz * yv * _silu_grad(zv), jnp.concatenate(dgs_, axis=1)

    dy, dz, grads["ssd_norm_g"] = rw(gated_norm_bwd, "ssd_post_bwd", D_INNER // cw,
                                     [(dy_ssd, cw, 0, None), (y, cw, 0, None), (z, cw, 0, None)], [(gn, cw, 0)],
                                     [(D_INNER, cw, 0, F32), (D_INNER, cw, 0, MXU)], [(1, cw)])
    dxact, ddt, dbias, dalog, ddskip = ssd_bwd(xact, dtraw, dt_bias, a_log, d_skip, sin, dy, n_seq, seq)
    grads["dt_bias"], grads["a_log"], grads["d_skip"] = dbias[:, :SSD_H], dalog[:, :SSD_H], ddskip[:, :SSD_H]

    def conv_silu_bwd(ctx, dxa, xh, wv, bv):
        return dxa * _silu_grad(bv + _conv_prev(xh[0], xh[1], wv, ctx.first, SSD_CONV))

    (dpre,) = rw(conv_silu_bwd, "ssd_conv_bwd_act", CONV_DIM // cw, [(dxact, cw, 0, None), (xbc, cw, 0, "prev")],
                 [(conv_w, cw, 0), (conv_b, cw, 0)], [(CONV_DIM, cw, 0, F32)])
    dxbc, grads["ssd_conv_w"], grads["ssd_conv_b"] = rw(
        conv_bwd(SSD_CONV), "ssd_conv_bwd", CONV_DIM // cw, [(dpre, cw, 0, "next"), (xbc, cw, 0, "prev")],
        [(conv_w, cw, 0)], [(CONV_DIM, cw, 0, MXU)], [(SSD_CONV, cw), (1, cw)])

    pieces = [(dz, w_z, "z"), (dxbc, w_xbc, "xbc"), (ddt, w_dt, "dt"), (dgs, w_gs, "gs"), (dga, w_ga, "ga")]
    pieces += [(dqkv[g], w_qkv[g], f"qkv{g}") for g in range(ATT_GROUPS)]
    dh, dws = None, {}
    for dpart, wpart, tag in pieces:
        dh = matmul(dpart, wpart, f"mm_dh_{tag}", tb=True, add=dh)
        dws[tag] = matmul(h, dpart, f"mm_dw_{tag}", ta=True)
    dq_parts = [[dws[f"qkv{g}"][:, t * ATT_OUT:(t + 1) * ATT_OUT] for g in range(ATT_GROUPS)] for t in range(3)]
    grads["w_in"] = jnp.concatenate(
        [dws["z"], dws["xbc"], dws["dt"][:, :SSD_H]] + [p for t in range(3) for p in dq_parts[t]] + [dws["gs"], dws["ga"]],
        axis=1)
    grad_x, grads["norm1_g"] = rw(rms_bwd_fn, "rms1_bwd", 1,
                                  [(x, D_MODEL, 0, None), (dh, D_MODEL, 0, None), (dx1, D_MODEL, 0, None)],
                                  [(g1, None, 0)], [(D_MODEL, D_MODEL, 0, F32)], [(1, D_MODEL)])
    return sq, grad_x.reshape(n_seq, seq, D_MODEL), grads


COL_SHARDED = ("w_in", "ssd_conv_w", "w_attn_proj", "w_up", "ffn_conv_w")
ROW_SHARDED = ("w_ssd_proj", "w_out", "w_down")
MATRICES = ("w_in", "w_attn_proj", "w_up", "w_ssd_proj", "w_out", "w_down")
REPLICATED = ("norm1_g", "ssd_conv_b", "dt_bias", "a_log", "d_skip", "ssd_norm_g", "q_norm_g", "k_norm_g",
              "norm2_g", "ffn_conv_b")
WEIGHTS = ("norm1_g", "w_in", "ssd_conv_w", "ssd_conv_b", "dt_bias", "a_log", "d_skip", "ssd_norm_g", "w_ssd_proj",
           "q_norm_g", "k_norm_g", "w_attn_proj", "w_out", "norm2_g", "w_up", "ffn_conv_w", "ffn_conv_b", "w_down")
PACK_ROWS, PACK_COLS = 8, 2048


def _pack(vals):
    flat = jnp.concatenate([vals[n].reshape(-1) for n in REPLICATED])
    return jnp.pad(flat, (0, PACK_ROWS * PACK_COLS - flat.shape[0])).reshape(PACK_ROWS, PACK_COLS)


def _unpack(packed, like):
    flat = packed.reshape(-1)
    out, pos = {}, 0
    for n in REPLICATED:
        size = like[n].size
        out[n] = flat[pos:pos + size].reshape(like[n].shape)
        pos += size
    return out


def step(x, target, wsh, msh, vsh):
    sharded = COL_SHARDED + ROW_SHARDED
    narrow = lambda n, a: a.astype(MXU) if n in MATRICES else a
    gathered = exchange([(narrow(n, wsh[n]), "gather") for n in sharded], "ag_weights")
    full = {n: wsh[n] for n in REPLICATED}
    for n, g in zip(sharded, gathered):
        if n in COL_SHARDED:
            full[n] = jnp.transpose(g, (1, 0, 2)).reshape(g.shape[1], N_DEV * g.shape[2])
        else:
            full[n] = g.reshape(N_DEV * g.shape[1], g.shape[2])

    sq, grad_x, grads = local_step(x, target, full)

    slabs = []
    for n in sharded:
        g = narrow(n, grads[n])
        if n in COL_SHARDED:
            slabs.append(jnp.transpose(g.reshape(g.shape[0], N_DEV, g.shape[1] // N_DEV), (1, 0, 2)))
        else:
            slabs.append(g.reshape(N_DEV, g.shape[0] // N_DEV, g.shape[1]))
    packed_g = _pack({n: grads[n] for n in REPLICATED})
    received = exchange([(s, "scatter") for s in slabs] + [(packed_g, "gather")], "rs_grads")

    out_g, out_d, out_m, out_v = {}, {}, {}, {}
    for n, parts in zip(sharded, received[:-1]):
        out_g[n], out_d[n], out_m[n], out_v[n] = adamw(parts, wsh[n], msh[n], vsh[n], f"adamw_{n}")
    pk = adamw(received[-1], _pack(wsh), _pack(msh), _pack(vsh), "adamw_small")
    for dst, packed in zip((out_g, out_d, out_m, out_v), pk):
        dst.update(_unpack(packed, wsh))
    loss = lax.psum(0.5 * jnp.sum(sq) / D_MODEL, ("x", "y", "c"))
    return loss, grad_x, out_g, out_d, out_m, out_v


def kernel(x, norm1_g, w_in, ssd_conv_w, ssd_conv_b, dt_bias, a_log, d_skip, ssd_norm_g, w_ssd_proj, q_norm_g, k_norm_g, w_attn_proj, w_out, norm2_g, w_up, ffn_conv_w, ffn_conv_b, w_down, loss_target, m_norm1_g, m_w_in, m_ssd_conv_w, m_ssd_conv_b, m_dt_bias, m_a_log, m_d_skip, m_ssd_norm_g, m_w_ssd_proj, m_q_norm_g, m_k_norm_g, m_w_attn_proj, m_w_out, m_norm2_g, m_w_up, m_ffn_conv_w, m_ffn_conv_b, m_w_down, v_norm1_g, v_w_in, v_ssd_conv_w, v_ssd_conv_b, v_dt_bias, v_a_log, v_d_skip, v_ssd_norm_g, v_w_ssd_proj, v_q_norm_g, v_k_norm_g, v_w_attn_proj, v_w_out, v_norm2_g, v_w_up, v_ffn_conv_w, v_ffn_conv_b, v_w_down):
    ws = (norm1_g, w_in, ssd_conv_w, ssd_conv_b, dt_bias, a_log, d_skip, ssd_norm_g, w_ssd_proj, q_norm_g, k_norm_g,
          w_attn_proj, w_out, norm2_g, w_up, ffn_conv_w, ffn_conv_b, w_down)
    ms = (m_norm1_g, m_w_in, m_ssd_conv_w, m_ssd_conv_b, m_dt_bias, m_a_log, m_d_skip, m_ssd_norm_g, m_w_ssd_proj,
          m_q_norm_g, m_k_norm_g, m_w_attn_proj, m_w_out, m_norm2_g, m_w_up, m_ffn_conv_w, m_ffn_conv_b, m_w_down)
    vs = (v_norm1_g, v_w_in, v_ssd_conv_w, v_ssd_conv_b, v_dt_bias, v_a_log, v_d_skip, v_ssd_norm_g, v_w_ssd_proj,
          v_q_norm_g, v_k_norm_g, v_w_attn_proj, v_w_out, v_norm2_g, v_w_up, v_ffn_conv_w, v_ffn_conv_b, v_w_down)
    strip = lambda a: a[0] if a.ndim == 3 else a
    wsh = {n: strip(a) for n, a in zip(WEIGHTS, ws)}
    msh = {n: strip(a) for n, a in zip(WEIGHTS, ms)}
    vsh = {n: strip(a) for n, a in zip(WEIGHTS, vs)}
    loss, grad_x, g, d, m, v = step(x, loss_target, wsh, msh, vsh)
    lead = lambda dct: [dct[n][None] if a.ndim == 3 else dct[n] for n, a in zip(WEIGHTS, ws)]
    return (loss, grad_x, *lead(g), *lead(d), *lead(m), *lead(v))
```

```python
import jax
import jax.numpy as jnp
from jax import lax
from jax.experimental import pallas as pl
from jax.experimental.pallas import tpu as pltpu

F32 = jnp.float32
BF16 = jnp.bfloat16
MXU = jnp.bfloat16
HIGHEST = lax.Precision.HIGHEST
VMEM_LIMIT_BYTES = 48 * 1024 * 1024
SUBLANES = 8
LANES = 128
N_DEV = 8

D_MODEL = 1024
D_INNER = 2048
SSD_P = 64
SSD_H = 32
SSD_G = 8
SSD_K = SSD_H // SSD_G
SSD_N = 128
SSD_Q = 128
SSD_CONV = 4
CONV_DIM = D_INNER + 2 * SSD_G * SSD_N
NORM_GROUP = D_INNER // SSD_G
ATT_GROUPS = 3
ATT_H = 8
ATT_HD = 64
ATT_BLK = 128
ATT_OUT = ATT_H * ATT_HD
ATT_DILATIONS = (1, 4, 16)
ATT_SCALE = ATT_HD ** -0.5
D_FF = 2816
FFN_CONV = 3
EPS = 1e-6
NEG = -1e30
IN_WIDTHS = (D_INNER, CONV_DIM, SSD_H, 3 * ATT_OUT, 3 * ATT_OUT, 3 * ATT_OUT, D_MODEL, D_MODEL)

ADAM_LR = 0.001
ADAM_B1 = 0.9
ADAM_B2 = 0.999
ADAM_EPS = 1e-08
ADAM_WD = 0.01
ADAM_STEP = 10


def _mm(a, b, dims):
    return lax.dot_general(a.astype(MXU), b.astype(MXU), (dims, ((), ())), preferred_element_type=F32)


def _dot_nn(a, b):
    return _mm(a, b, ((1,), (0,)))


def _dot_nt(a, b):
    return _mm(a, b, ((1,), (1,)))


def _dot_tn(a, b):
    return _mm(a, b, ((0,), (0,)))


def _dot_f32(a, b):
    return lax.dot_general(a, b, (((1,), (0,)), ((), ())), precision=HIGHEST, preferred_element_type=F32)


def _sigmoid(x):
    return 1.0 / (1.0 + jnp.exp(-x))


def _silu(x):
    return x * _sigmoid(x)


def _silu_grad(x):
    s = _sigmoid(x)
    return s * (1.0 + x * (1.0 - s))


def _softplus(x):
    return jnp.maximum(x, 0.0) + jnp.log(1.0 + jnp.exp(-jnp.abs(x)))


def _rms_fwd(x, g):
    r = lax.rsqrt(jnp.mean(x * x, axis=-1, keepdims=True) + EPS)
    return x * r * g


def _rms_bwd(x, g, dy):
    r = lax.rsqrt(jnp.mean(x * x, axis=-1, keepdims=True) + EPS)
    xh = x * r
    dyg = dy * g
    dx = r * (dyg - xh * jnp.mean(dyg * xh, axis=-1, keepdims=True))
    return dx, jnp.sum(dy * xh, axis=0, keepdims=True)


def _onehot_row(h, n=LANES):
    return (lax.broadcasted_iota(jnp.int32, (1, n), 1) == h).astype(F32)


def _onehot_col(h, n=LANES):
    return (lax.broadcasted_iota(jnp.int32, (n, 1), 0) == h).astype(F32)


def _head_expand_matrix():
    r = lax.broadcasted_iota(jnp.int32, (LANES, ATT_OUT), 0)
    c = lax.broadcasted_iota(jnp.int32, (LANES, ATT_OUT), 1)
    return (c // ATT_HD == r).astype(F32)


def _split_bf16(x, parts):
    out = []
    for _ in range(parts - 1):
        hi = x.astype(BF16).astype(F32)
        out.append(hi)
        x = x - hi
    out.append(x)
    return out


def _expand_heads(w):
    e = _head_expand_matrix()
    return sum(_dot_nn(p, e) for p in _split_bf16(w, 2))


def _reduce_heads(x):
    e = _head_expand_matrix()
    return sum(_dot_nt(p, e) for p in _split_bf16(x, 3))


def _shift_prev(cur, halo, s, first):
    if s == 0:
        return cur
    rolled = pltpu.roll(cur, s, 0)
    hr = jnp.where(first, 0.0, pltpu.roll(halo, s, 0))
    rows = lax.broadcasted_iota(jnp.int32, halo.shape, 0)
    head = jnp.where(rows < s, hr, rolled[:SUBLANES])
    if cur.shape[0] == SUBLANES:
        return head
    return jnp.concatenate([head, rolled[SUBLANES:]], axis=0)


def _shift_next(cur, halo, s, last):
    if s == 0:
        return cur
    tb = cur.shape[0]
    rolled = pltpu.roll(cur, tb - s, 0)
    hr = jnp.where(last, 0.0, pltpu.roll(halo, SUBLANES - s, 0))
    rows = lax.broadcasted_iota(jnp.int32, halo.shape, 0)
    tail = jnp.where(rows >= SUBLANES - s, hr, rolled[tb - SUBLANES:])
    return jnp.concatenate([rolled[:tb - SUBLANES], tail], axis=0)


def _conv_prev(x, halo, w, first, taps):
    acc = None
    for i in range(taps):
        term = w[i:i + 1, :] * _shift_prev(x, halo, taps - 1 - i, first)
        acc = term if acc is None else acc + term
    return acc


def _conv_pre(x, w, b, first, taps):
    cur, prev8, next8 = x
    tail = cur[cur.shape[0] - SUBLANES:]
    return b + _conv_prev(cur, prev8, w, first, taps), b + _conv_prev(next8, tail, w, False, taps)


def _conv_bwd(dpre, dpre_next8, x, w, ctx, taps):
    cur, prev8, _ = x
    dx, dws = None, []
    for i in range(taps):
        term = w[i:i + 1, :] * _shift_next(dpre, dpre_next8, taps - 1 - i, ctx.last)
        dx = term if dx is None else dx + term
        dws.append(jnp.sum(dpre * _shift_prev(cur, prev8, taps - 1 - i, ctx.first), axis=0, keepdims=True))
    return dx, jnp.concatenate(dws, axis=0), jnp.sum(dpre, axis=0, keepdims=True)


def _params(sem):
    return pltpu.CompilerParams(dimension_semantics=sem, vmem_limit_bytes=VMEM_LIMIT_BYTES)


def _pick(dim, target):
    if dim <= target:
        return dim
    best = None
    for t in range(LANES, target + 1, LANES):
        if dim % t == 0:
            best = t
    assert best is not None, (dim, target)
    return best


MATMUL_VMEM_BUDGET = 34 * 1024 * 1024


def _matmul_tiles(m, n, k, a_bytes, b_bytes, add_bytes, out_bytes):
    tn = _pick(n, 1536)
    for tk_target in (k, 4096, 2048, 1024, 512):
        tk = _pick(k, tk_target)
        for tm_target in (2048, 1024, 512, 256, 128):
            tm = _pick(m, tm_target) if m > tm_target else m
            if tm > 2048:
                continue
            blocks = tm * tk * a_bytes + tk * tn * b_bytes + tm * tn * (add_bytes + out_bytes)
            need = 2 * blocks + (tm * tn * 4 if tk < k else 0) + tm * tn * 4
            if need <= MATMUL_VMEM_BUDGET:
                return tm, tn, tk
    raise ValueError((m, n, k))


def matmul(a, b, name, ta=False, tb=False, add=None, out_dtype=F32):
    assert not (ta and tb)
    m, k = (a.shape[1], a.shape[0]) if ta else a.shape
    n = b.shape[0] if tb else b.shape[1]
    assert (b.shape[1] if tb else b.shape[0]) == k
    tm, tn, tk = _matmul_tiles(m, n, k, a.dtype.itemsize, b.dtype.itemsize,
                               0 if add is None else add.dtype.itemsize, jnp.dtype(out_dtype).itemsize)
    nk = k // tk
    dims = ((0,), (0,)) if ta else (((1,), (1,)) if tb else ((1,), (0,)))

    def body(*refs):
        if add is None:
            a_ref, b_ref, o_ref = refs[:3]
        else:
            a_ref, b_ref, add_ref, o_ref = refs[:4]

        def finish(r):
            if add is not None:
                r = r + add_ref[...].astype(F32)
            o_ref[...] = r.astype(out_dtype)

        if nk == 1:
            finish(_mm(a_ref[...], b_ref[...], dims))
            return
        acc = refs[-1]
        kk = pl.program_id(2)

        @pl.when(kk == 0)
        def _():
            acc[...] = jnp.zeros_like(acc)

        acc[...] += _mm(a_ref[...], b_ref[...], dims)

        @pl.when(kk == nk - 1)
        def _():
            finish(acc[...])

    a_spec = pl.BlockSpec((tk, tm), lambda i, j, kk: (kk, i)) if ta else pl.BlockSpec((tm, tk), lambda i, j, kk: (i, kk))
    b_spec = pl.BlockSpec((tn, tk), lambda i, j, kk: (j, kk)) if tb else pl.BlockSpec((tk, tn), lambda i, j, kk: (kk, j))
    in_specs = [a_spec, b_spec]
    args = [a, b]
    if add is not None:
        in_specs.append(pl.BlockSpec((tm, tn), lambda i, j, kk: (i, j)))
        args.append(add)
    return pl.pallas_call(
        body, name=name,
        grid=(m // tm, n // tn, nk),
        in_specs=in_specs,
        out_specs=pl.BlockSpec((tm, tn), lambda i, j, kk: (i, j)),
        out_shape=jax.ShapeDtypeStruct((m, n), out_dtype),
        scratch_shapes=[] if nk == 1 else [pltpu.VMEM((tm, tn), F32)],
        compiler_params=_params(("parallel", "parallel", "arbitrary")),
    )(*args)


class _Ctx:
    def __init__(self, first, last):
        self.first = first
        self.last = last


def rowwise(fn, name, rows, seq, tb, ncol, ins, params=(), outs=(), accs=()):
    assert rows % tb == 0 and seq % tb == 0 and tb % 16 == 0
    bps = seq // tb
    nrow = rows // tb
    r8 = tb // SUBLANES
    args, in_specs = [], []
    for arr, w, off, halo in ins:
        args.append(arr)
        in_specs.append(pl.BlockSpec((tb, w), lambda j, i, off=off: (i, off + j)))
        if halo in ("prev", "both"):
            args.append(arr)
            in_specs.append(pl.BlockSpec((SUBLANES, w), lambda j, i, off=off: (jnp.maximum(i * r8 - 1, 0), off + j)))
        if halo in ("next", "both"):
            args.append(arr)
            in_specs.append(pl.BlockSpec(
                (SUBLANES, w), lambda j, i, off=off: (jnp.minimum((i + 1) * r8, rows // SUBLANES - 1), off + j)))
    for arr, w, off in params:
        args.append(arr)
        if w is None:
            in_specs.append(pl.BlockSpec(arr.shape, lambda j, i: (0, 0)))
        else:
            in_specs.append(pl.BlockSpec((arr.shape[0], w), lambda j, i, off=off: (0, off + j)))
    out_shape, out_specs = [], []
    for total, w, off, dt in outs:
        out_shape.append(jax.ShapeDtypeStruct((rows, total), dt))
        out_specs.append(pl.BlockSpec((tb, w), lambda j, i, off=off: (i, off + j)))
    for r, w in accs:
        out_shape.append(jax.ShapeDtypeStruct((r, ncol * w), F32))
        out_specs.append(pl.BlockSpec((r, w), lambda j, i: (0, j)))
    n_out, n_acc = len(outs), len(accs)

    def body(*refs):
        i = pl.program_id(1)
        pos = 0
        vals = []
        for _, _, _, halo in ins:
            cur = refs[pos][...]
            pos += 1
            if halo is None:
                vals.append(cur)
            elif halo == "both":
                vals.append((cur, refs[pos][...], refs[pos + 1][...]))
                pos += 2
            else:
                vals.append((cur, refs[pos][...]))
                pos += 1
        for _ in params:
            vals.append(refs[pos][...])
            pos += 1
        ctx = _Ctx(i % bps == 0, i % bps == bps - 1)
        res = fn(ctx, *vals)
        if not isinstance(res, (tuple, list)):
            res = (res,)
        assert len(res) == n_out + n_acc
        for q in range(n_out):
            refs[pos + q][...] = res[q].astype(refs[pos + q].dtype)
        for q in range(n_acc):
            ref, val = refs[pos + n_out + q], res[n_out + q]

            @pl.when(i == 0)
            def _(ref=ref, val=val):
                ref[...] = val

            @pl.when(i != 0)
            def _(ref=ref, val=val):
                ref[...] += val

    res = pl.pallas_call(
        body, name=name,
        grid=(ncol, nrow),
        in_specs=in_specs,
        out_specs=out_specs,
        out_shape=out_shape,
        compiler_params=_params(("parallel", "arbitrary")),
    )(*args)
    return res


GROUP_W = SSD_K * SSD_P


def _tri(lower):
    r = lax.broadcasted_iota(jnp.int32, (SSD_Q, SSD_Q), 0)
    c = lax.broadcasted_iota(jnp.int32, (SSD_Q, SSD_Q), 1)
    return r >= c if lower else r <= c


def _group_masks():
    lane = lax.broadcasted_iota(jnp.int32, (1, GROUP_W), 1) // SSD_P
    row = lax.broadcasted_iota(jnp.int32, (GROUP_W, 1), 0) // SSD_P
    return [lane == k for k in range(SSD_K)], [row == k for k in range(SSD_K)]


def _per_head(masks, vals):
    out = jnp.where(masks[0], vals[0], 0.0)
    for m, v in zip(masks[1:], vals[1:]):
        out = jnp.where(m, v, out)
    return out


def _headsum(prod, g):
    j = lax.broadcasted_iota(jnp.int32, (GROUP_W, LANES), 0) // SSD_P
    lane = lax.broadcasted_iota(jnp.int32, (GROUP_W, LANES), 1)
    e = (lane == g * SSD_K + j).astype(F32)
    return sum(_dot_nn(p, e) for p in _split_bf16(prod, 2))


def ssd_fwd(xact, dtraw, dt_bias, a_log, d_skip, n_seq, seq):
    nc = seq // SSD_Q
    rows = n_seq * seq

    def body(xact_ref, dtraw_ref, bias_ref, alog_ref, dskip_ref, y_ref, sin_ref, state, cs_s, cst_s, dt_s):
        c = pl.program_id(1)

        @pl.when(c == 0)
        def _():
            state[...] = jnp.zeros_like(state)

        sin_ref[0] = state[...]
        dt = _softplus(dtraw_ref[...] + bias_ref[...])
        a = dt * (-jnp.exp(alog_ref[...]))
        cs = _dot_f32(_tri(True).astype(F32), a)
        cs_s[...] = cs
        cst_s[...] = cs.T
        dt_s[...] = dt
        causal = _tri(True)
        lane_masks, row_masks = _group_masks()
        for g in range(SSD_G):
            heads = [g * SSD_K + k for k in range(SSD_K)]
            bg = xact_ref[:, pl.ds(D_INNER + g * SSD_N, SSD_N)]
            cg = xact_ref[:, pl.ds(D_INNER + (SSD_G + g) * SSD_N, SSD_N)]
            xg = xact_ref[:, pl.ds(g * GROUP_W, GROUP_W)]
            cols = [cs_s[:, pl.ds(h, 1)] for h in heads]
            lasts = [cs_s[pl.ds(SSD_Q - 1, 1), pl.ds(h, 1)] for h in heads]
            xdg = xg * _per_head(lane_masks, [dt_s[:, pl.ds(h, 1)] for h in heads])
            sg = state[g]
            gm = _dot_nt(cg, bg)
            y = (_per_head(lane_masks, [jnp.exp(c_) for c_ in cols]) * _dot_nt(cg, sg)
                 + _per_head(lane_masks, [dskip_ref[:, pl.ds(h, 1)] for h in heads]) * xg)
            for k, h in enumerate(heads):
                decay = jnp.exp(jnp.where(causal, cols[k] - cst_s[pl.ds(h, 1), :], NEG))
                y = y + _dot_nn(gm * decay, jnp.where(lane_masks[k], xdg, 0.0))
            y_ref[:, pl.ds(g * GROUP_W, GROUP_W)] = y
            w = _per_head(lane_masks, [jnp.exp(l_ - c_) for l_, c_ in zip(lasts, cols)])
            state[g] = _per_head(row_masks, [jnp.exp(l_) for l_ in lasts]) * sg + _dot_tn(w * xdg, bg)

    vec = pl.BlockSpec((1, LANES), lambda b, c: (0, 0))
    return pl.pallas_call(
        body, name="ssd_fwd",
        grid=(n_seq, nc),
        in_specs=[pl.BlockSpec((SSD_Q, CONV_DIM), lambda b, c: (b * nc + c, 0)),
                  pl.BlockSpec((SSD_Q, LANES), lambda b, c: (b * nc + c, 0)), vec, vec, vec],
        out_specs=[pl.BlockSpec((SSD_Q, D_INNER), lambda b, c: (b * nc + c, 0)),
                   pl.BlockSpec((1, SSD_G, GROUP_W, SSD_N), lambda b, c: (b * nc + c, 0, 0, 0))],
        out_shape=[jax.ShapeDtypeStruct((rows, D_INNER), F32),
                   jax.ShapeDtypeStruct((n_seq * nc, SSD_G, GROUP_W, SSD_N), F32)],
        scratch_shapes=[pltpu.VMEM((SSD_G, GROUP_W, SSD_N), F32), pltpu.VMEM((SSD_Q, LANES), F32),
                        pltpu.VMEM((LANES, SSD_Q), F32), pltpu.VMEM((SSD_Q, LANES), F32)],
        compiler_params=_params(("arbitrary", "arbitrary")),
    )(xact, dtraw, dt_bias, a_log, d_skip)


def ssd_bwd(xact, dtraw, dt_bias, a_log, d_skip, sin, dy, n_seq, seq):
    nc = seq // SSD_Q
    rows = n_seq * seq

    def body(xact_ref, dtraw_ref, bias_ref, alog_ref, dskip_ref, sin_ref, dy_ref,
             dx_ref, ddt_ref, dbias_ref, dalog_ref, ddskip_ref, dstate, cs_s, cst_s, dt_s):
        b, c = pl.program_id(0), pl.program_id(1)

        @pl.when(c == 0)
        def _():
            dstate[...] = jnp.zeros_like(dstate)

        pre = dtraw_ref[...] + bias_ref[...]
        dt = _softplus(pre)
        a_neg = -jnp.exp(alog_ref[...])
        cs = _dot_f32(_tri(True).astype(F32), dt * a_neg)
        cs_s[...] = cs
        cst_s[...] = cs.T
        dt_s[...] = dt
        causal, anti = _tri(True), _tri(False)
        is_last_row = lax.broadcasted_iota(jnp.int32, (SSD_Q, 1), 0) == SSD_Q - 1
        lane_masks, row_masks = _group_masks()
        dcs_cf = jnp.zeros((SSD_Q, LANES), F32)
        dcs_rf = jnp.zeros((LANES, SSD_Q), F32)
        ddt_cf = jnp.zeros((SSD_Q, LANES), F32)
        dd_vec = jnp.zeros((1, LANES), F32)
        dlast_vec = jnp.zeros((1, LANES), F32)
        for g in range(SSD_G):
            heads = [g * SSD_K + k for k in range(SSD_K)]
            bg = xact_ref[:, pl.ds(D_INNER + g * SSD_N, SSD_N)]
            cg = xact_ref[:, pl.ds(D_INNER + (SSD_G + g) * SSD_N, SSD_N)]
            xg = xact_ref[:, pl.ds(g * GROUP_W, GROUP_W)]
            dyg = dy_ref[:, pl.ds(g * GROUP_W, GROUP_W)]
            cols = [cs_s[:, pl.ds(h, 1)] for h in heads]
            rws = [cst_s[pl.ds(h, 1), :] for h in heads]
            lasts = [cs_s[pl.ds(SSD_Q - 1, 1), pl.ds(h, 1)] for h in heads]
            e_lasts = [jnp.exp(l_) for l_ in lasts]
            dtg = _per_head(lane_masks, [dt_s[:, pl.ds(h, 1)] for h in heads])
            dskg = _per_head(lane_masks, [dskip_ref[:, pl.ds(h, 1)] for h in heads])
            e_col = _per_head(lane_masks, [jnp.exp(c_) for c_ in cols])
            w = _per_head(lane_masks, [jnp.exp(l_ - c_) for l_, c_ in zip(lasts, cols)])
            xdg = xg * dtg
            sg = sin_ref[0, g]
            dsn = dstate[g]
            gm = _dot_nt(cg, bg)
            gmt = _dot_nt(bg, cg)
            y_off = e_col * _dot_nt(cg, sg)
            d_cs = e_col * dyg
            dcg = _dot_nn(d_cs, sg)
            dsp = _dot_tn(d_cs, cg) + _per_head(row_masks, e_lasts) * dsn
            dbg = _dot_nn(w * xdg, dsn)
            dtt = _dot_nt(bg, dsn)
            dxd = w * dtt
            dw = _headsum(dtt * xdg * w, g)
            dcs_cf = dcs_cf + _headsum(dyg * y_off, g) - dw
            dlast_vec = dlast_vec + jnp.sum(dw, axis=0, keepdims=True)
            dsn_s = dsn * sg
            dgm = jnp.zeros((SSD_Q, SSD_Q), F32)
            for k, h in enumerate(heads):
                seg = cols[k] - rws[k]
                decay = jnp.exp(jnp.where(causal, seg, NEG))
                decay_t = jnp.exp(jnp.where(anti, -seg, NEG))
                dyk = jnp.where(lane_masks[k], dyg, 0.0)
                dm = _dot_nt(dyk, xdg)
                dxd = dxd + _dot_nn(gmt * decay_t, dyk)
                dseg = dm * gm * decay
                dgm = dgm + dm * decay
                oh_r = _onehot_row(h)
                dcs_cf = dcs_cf + jnp.sum(dseg, axis=1, keepdims=True) * oh_r
                dcs_rf = dcs_rf - _onehot_col(h) * jnp.sum(dseg, axis=0, keepdims=True)
                dlast_vec = dlast_vec + jnp.sum(jnp.where(row_masks[k], dsn_s, 0.0), keepdims=True) * e_lasts[k] * oh_r
            dx_ref[:, pl.ds(g * GROUP_W, GROUP_W)] = dxd * dtg + dskg * dyg
            ddt_cf = ddt_cf + _headsum(dxd * xg, g)
            dyx = jnp.broadcast_to(jnp.sum(dyg * xg, axis=0, keepdims=True), (SUBLANES, GROUP_W))
            dd_vec = dd_vec + _headsum(dyx, g)[0:1]
            dstate[g] = dsp
            dx_ref[:, pl.ds(D_INNER + g * SSD_N, SSD_N)] = dbg + _dot_tn(dgm, cg)
            dx_ref[:, pl.ds(D_INNER + (SSD_G + g) * SSD_N, SSD_N)] = dcg + _dot_nn(dgm, bg)
        dcs = dcs_cf + dcs_rf.T + jnp.where(is_last_row, dlast_vec, 0.0)
        da = _dot_f32(_tri(False).astype(F32), dcs)
        ddt = ddt_cf + da * a_neg
        ddtraw = ddt * _sigmoid(pre)
        ddt_ref[...] = ddtraw.astype(ddt_ref.dtype)
        dbias = jnp.sum(ddtraw, axis=0, keepdims=True)
        dalog = jnp.sum(da * dt, axis=0, keepdims=True) * a_neg
        first_step = jnp.logical_and(b == 0, c == 0)

        @pl.when(first_step)
        def _():
            dbias_ref[...] = dbias
            dalog_ref[...] = dalog
            ddskip_ref[...] = dd_vec

        @pl.when(jnp.logical_not(first_step))
        def _():
            dbias_ref[...] += dbias
            dalog_ref[...] += dalog
            ddskip_ref[...] += dd_vec

    def rowblk(b, c):
        return b * nc + (nc - 1 - c)

    vec = pl.BlockSpec((1, LANES), lambda b, c: (0, 0))
    return pl.pallas_call(
        body, name="ssd_bwd",
        grid=(n_seq, nc),
        in_specs=[pl.BlockSpec((SSD_Q, CONV_DIM), lambda b, c: (rowblk(b, c), 0)),
                  pl.BlockSpec((SSD_Q, LANES), lambda b, c: (rowblk(b, c), 0)), vec, vec, vec,
                  pl.BlockSpec((1, SSD_G, GROUP_W, SSD_N), lambda b, c: (rowblk(b, c), 0, 0, 0)),
                  pl.BlockSpec((SSD_Q, D_INNER), lambda b, c: (rowblk(b, c), 0))],
        out_specs=[pl.BlockSpec((SSD_Q, CONV_DIM), lambda b, c: (rowblk(b, c), 0)),
                   pl.BlockSpec((SSD_Q, LANES), lambda b, c: (rowblk(b, c), 0)), vec, vec, vec],
        out_shape=[jax.ShapeDtypeStruct((rows, CONV_DIM), F32), jax.ShapeDtypeStruct((rows, LANES), BF16),
                   jax.ShapeDtypeStruct((1, LANES), F32), jax.ShapeDtypeStruct((1, LANES), F32),
                   jax.ShapeDtypeStruct((1, LANES), F32)],
        scratch_shapes=[pltpu.VMEM((SSD_G, GROUP_W, SSD_N), F32), pltpu.VMEM((SSD_Q, LANES), F32),
                        pltpu.VMEM((LANES, SSD_Q), F32), pltpu.VMEM((SSD_Q, LANES), F32)],
        compiler_params=_params(("arbitrary", "arbitrary")),
    )(xact, dtraw, dt_bias, a_log, d_skip, sin, dy)


def _band_masks():
    qi = lax.broadcasted_iota(jnp.int32, (ATT_BLK, ATT_BLK), 0)
    ki = lax.broadcasted_iota(jnp.int32, (ATT_BLK, ATT_BLK), 1)
    return qi >= ki, qi <= ki


def attn_fwd(qkv, gq, gk, n_seq, seq, dil, name):
    length = seq // dil
    nb = length // ATT_BLK
    w3 = 3 * ATT_OUT

    def body(cur_ref, prev_ref, gq_ref, gk_ref, o_ref, lse_ref):
        n = pl.program_id(2)
        m_cur, m_prev = _band_masks()
        m_prev = jnp.logical_and(m_prev, n > 0)
        gq_, gk_ = gq_ref[...], gk_ref[...]
        lse_blk = jnp.zeros((ATT_BLK, LANES), F32)
        for h in range(ATT_H):
            q = _rms_fwd(cur_ref[0, :, pl.ds(h * ATT_HD, ATT_HD)], gq_)
            kc = _rms_fwd(cur_ref[0, :, pl.ds(ATT_OUT + h * ATT_HD, ATT_HD)], gk_)
            kp = _rms_fwd(prev_ref[0, :, pl.ds(ATT_OUT + h * ATT_HD, ATT_HD)], gk_)
            vc = cur_ref[0, :, pl.ds(2 * ATT_OUT + h * ATT_HD, ATT_HD)]
            vp = prev_ref[0, :, pl.ds(2 * ATT_OUT + h * ATT_HD, ATT_HD)]
            sc = jnp.where(m_cur, _dot_nt(q, kc) * ATT_SCALE, NEG)
            sp = jnp.where(m_prev, _dot_nt(q, kp) * ATT_SCALE, NEG)
            mx = jnp.maximum(jnp.max(sc, axis=1, keepdims=True), jnp.max(sp, axis=1, keepdims=True))
            pc = jnp.exp(sc - mx)
            pp = jnp.exp(sp - mx)
            den = jnp.sum(pc, axis=1, keepdims=True) + jnp.sum(pp, axis=1, keepdims=True)
            o_ref[0, :, pl.ds(h * ATT_HD, ATT_HD)] = (_dot_nn(pc, vc) + _dot_nn(pp, vp)) / den
            lse_blk = lse_blk + (mx + jnp.log(den)) * _onehot_row(h)
        lse_ref[0] = lse_blk

    view = qkv.reshape(n_seq, length, dil * w3)
    gspec = pl.BlockSpec((1, ATT_HD), lambda b, r, n: (0, 0))
    o, lse = pl.pallas_call(
        body, name=name,
        grid=(n_seq, dil, nb),
        in_specs=[pl.BlockSpec((1, ATT_BLK, w3), lambda b, r, n: (b, n, r)),
                  pl.BlockSpec((1, ATT_BLK, w3), lambda b, r, n: (b, jnp.maximum(n - 1, 0), r)), gspec, gspec],
        out_specs=[pl.BlockSpec((1, ATT_BLK, ATT_OUT), lambda b, r, n: (b, n, r)),
                   pl.BlockSpec((1, ATT_BLK, LANES), lambda b, r, n: (b, n, r))],
        out_shape=[jax.ShapeDtypeStruct((n_seq, length, dil * ATT_OUT), F32),
                   jax.ShapeDtypeStruct((n_seq, length, dil * LANES), F32)],
        compiler_params=_params(("parallel", "parallel", "arbitrary")),
    )(view, view, gq, gk)
    return o.reshape(n_seq * seq, ATT_OUT), lse.reshape(n_seq * seq, LANES)


def attn_bwd(qkv, gq, gk, do, lse, wts, rsum, n_seq, seq, dil, name):
    length = seq // dil
    nb = length // ATT_BLK
    w3 = 3 * ATT_OUT

    def body(prev_ref, cur_ref, nxt_ref, gq_ref, gk_ref, do_c, do_x, lse_c, lse_x, wt_c, wt_x, rs_c, rs_x,
             dqkv_ref, dgq_ref, dgk_ref):
        b, r, n = pl.program_id(0), pl.program_id(1), pl.program_id(2)
        m_cur, m_band = _band_masks()
        m_prev = jnp.logical_and(m_band, n > 0)
        m_next = jnp.logical_and(m_band, n < nb - 1)
        gq_, gk_ = gq_ref[...], gk_ref[...]
        dgq = jnp.zeros((1, ATT_HD), F32)
        dgk = jnp.zeros((1, ATT_HD), F32)
        for h in range(ATT_H):
            hs = pl.ds(h * ATT_HD, ATT_HD)
            ks = pl.ds(ATT_OUT + h * ATT_HD, ATT_HD)
            vs = pl.ds(2 * ATT_OUT + h * ATT_HD, ATT_HD)
            one = pl.ds(h, 1)
            q_raw, k_raw = cur_ref[0, :, hs], cur_ref[0, :, ks]
            qc = _rms_fwd(q_raw, gq_)
            qx = _rms_fwd(nxt_ref[0, :, hs], gq_)
            kc = _rms_fwd(k_raw, gk_)
            kp = _rms_fwd(prev_ref[0, :, ks], gk_)
            vc, vp = cur_ref[0, :, vs], prev_ref[0, :, vs]
            wc, wx = wt_c[0, :, one], wt_x[0, :, one]
            dog_c = do_c[0, :, hs] * wc
            dog_x = do_x[0, :, hs] * wx
            dl_c = -wc * rs_c[0, :, one]
            dl_x = -wx * rs_x[0, :, one]
            lc, lx = lse_c[0, :, one], lse_x[0, :, one]
            p_cc = jnp.exp(jnp.where(m_cur, _dot_nt(qc, kc) * ATT_SCALE - lc, NEG))
            p_cp = jnp.exp(jnp.where(m_prev, _dot_nt(qc, kp) * ATT_SCALE - lc, NEG))
            p_xc = jnp.exp(jnp.where(m_next, _dot_nt(qx, kc) * ATT_SCALE - lx, NEG))
            ds_cc = p_cc * (_dot_nt(dog_c, vc) + dl_c)
            ds_cp = p_cp * (_dot_nt(dog_c, vp) + dl_c)
            ds_xc = p_xc * (_dot_nt(dog_x, vc) + dl_x)
            dqn = (_dot_nn(ds_cc, kc) + _dot_nn(ds_cp, kp)) * ATT_SCALE
            dkn = (_dot_tn(ds_cc, qc) + _dot_tn(ds_xc, qx)) * ATT_SCALE
            dv = _dot_tn(p_cc, dog_c) + _dot_tn(p_xc, dog_x)
            dq, dgq_h = _rms_bwd(q_raw, gq_, dqn)
            dk, dgk_h = _rms_bwd(k_raw, gk_, dkn)
            dqkv_ref[0, :, hs] = dq.astype(dqkv_ref.dtype)
            dqkv_ref[0, :, ks] = dk.astype(dqkv_ref.dtype)
            dqkv_ref[0, :, vs] = dv.astype(dqkv_ref.dtype)
            dgq = dgq + dgq_h
            dgk = dgk + dgk_h
        first_step = jnp.logical_and(jnp.logical_and(b == 0, r == 0), n == 0)

        @pl.when(first_step)
        def _():
            dgq_ref[...] = dgq
            dgk_ref[...] = dgk

        @pl.when(jnp.logical_not(first_step))
        def _():
            dgq_ref[...] += dgq
            dgk_ref[...] += dgk

    view = qkv.reshape(n_seq, length, dil * w3)
    do_v = do.reshape(n_seq, length, dil * ATT_OUT)
    lse_v = lse.reshape(n_seq, length, dil * LANES)
    wts_v = wts.reshape(n_seq, length, dil * LANES)
    rs_v = rsum.reshape(n_seq, length, dil * LANES)

    def at(shift, width):
        if shift < 0:
            return pl.BlockSpec((1, ATT_BLK, width), lambda b, r, n: (b, jnp.maximum(n - 1, 0), r))
        if shift > 0:
            return pl.BlockSpec((1, ATT_BLK, width), lambda b, r, n: (b, jnp.minimum(n + 1, nb - 1), r))
        return pl.BlockSpec((1, ATT_BLK, width), lambda b, r, n: (b, n, r))

    gspec = pl.BlockSpec((1, ATT_HD), lambda b, r, n: (0, 0))
    dqkv, dgq, dgk = pl.pallas_call(
        body, name=name,
        grid=(n_seq, dil, nb),
        in_specs=[at(-1, w3), at(0, w3), at(1, w3), gspec, gspec,
                  at(0, ATT_OUT), at(1, ATT_OUT), at(0, LANES), at(1, LANES),
                  at(0, LANES), at(1, LANES), at(0, LANES), at(1, LANES)],
        out_specs=[at(0, w3), gspec, gspec],
        out_shape=[jax.ShapeDtypeStruct((n_seq, length, dil * w3), BF16),
                   jax.ShapeDtypeStruct((1, ATT_HD), F32), jax.ShapeDtypeStruct((1, ATT_HD), F32)],
        compiler_params=_params(("arbitrary", "arbitrary", "arbitrary")),
    )(view, view, view, gq, gk, do_v, do_v, lse_v, lse_v, wts_v, wts_v, rs_v, rs_v)
    return dqkv.reshape(n_seq * seq, w3), dgq, dgk


OTHER_CHIPS = (4, 2, 6)


def exchange(items, name):
    n = len(items)

    def body(*refs):
        in_refs, out_refs = refs[:n], refs[n:2 * n]
        send_sems, recv_sems, local_sems = refs[2 * n:]
        x, y, c = lax.axis_index("x"), lax.axis_index("y"), lax.axis_index("c")
        me = 4 * x + 2 * y + c

        def peer(k):
            px = 1 - x if k & 4 else x
            py = 1 - y if k & 2 else y
            pc = 1 - c if k & 1 else c
            return (px, py, pc), 4 * px + 2 * py + pc

        def remote(t, k):
            dev, pid = peer(k)
            src = in_refs[t] if items[t][1] == "gather" else in_refs[t].at[pid]
            return pltpu.make_async_remote_copy(
                src_ref=src, dst_ref=out_refs[t].at[me], send_sem=send_sems.at[t, k], recv_sem=recv_sems.at[t, k],
                device_id=dev, device_id_type=pl.DeviceIdType.MESH)

        def arrival(t, k):
            dev, pid = peer(k)
            src = in_refs[t] if items[t][1] == "gather" else in_refs[t].at[pid]
            return pltpu.make_async_remote_copy(
                src_ref=src, dst_ref=out_refs[t].at[pid], send_sem=send_sems.at[t, k], recv_sem=recv_sems.at[t, k],
                device_id=dev, device_id_type=pl.DeviceIdType.MESH)

        def own(t):
            src = in_refs[t] if items[t][1] == "gather" else in_refs[t].at[me]
            return pltpu.make_async_copy(src, out_refs[t].at[me], local_sems.at[t])

        def forward(t, k, from_sibling):
            sib, _ = peer(1)
            _, pid = peer(k + 1 if from_sibling else k)
            slot = out_refs[t].at[pid]
            return pltpu.make_async_remote_copy(
                src_ref=slot, dst_ref=slot, send_sem=send_sems.at[t, k + 1], recv_sem=recv_sems.at[t, k + 1],
                device_id=sib, device_id_type=pl.DeviceIdType.MESH)

        def direct(t):
            return (1,) + OTHER_CHIPS if items[t][1] == "gather" else tuple(range(1, N_DEV))

        for t in range(n):
            own(t).start()
            for k in direct(t):
                remote(t, k).start()
        for t in range(n):
            if items[t][1] == "gather":
                for k in OTHER_CHIPS:
                    arrival(t, k).wait_recv()
                    forward(t, k, False).start()
        for t in range(n):
            if items[t][1] == "gather":
                arrival(t, 1).wait_recv()
                for k in OTHER_CHIPS:
                    forward(t, k, True).wait_recv()
            else:
                for k in direct(t):
                    arrival(t, k).wait_recv()
        for t in range(n):
            for k in direct(t):
                remote(t, k).wait_send()
            if items[t][1] == "gather":
                for k in OTHER_CHIPS:
                    forward(t, k, False).wait_send()
            own(t).wait()

    out_shape = []
    for arr, mode in items:
        shp = arr.shape if mode == "gather" else arr.shape[1:]
        out_shape.append(jax.ShapeDtypeStruct((N_DEV,) + tuple(shp), arr.dtype))
    anyspec = pl.BlockSpec(memory_space=pl.ANY)
    return pl.pallas_call(
        body, name=name,
        in_specs=[anyspec] * n,
        out_specs=[anyspec] * n,
        out_shape=out_shape,
        scratch_shapes=[pltpu.SemaphoreType.DMA((n, N_DEV)), pltpu.SemaphoreType.DMA((n, N_DEV)),
                        pltpu.SemaphoreType.DMA((n,))],
    )(*[a for a, _ in items])


def adamw(parts, w, m, v, name):
    r, c = w.shape
    rb = r if r <= 512 else (128 if c > 1024 else 256)
    assert r % rb == 0

    def body(p_ref, w_ref, m_ref, v_ref, g_out, d_out, m_out, v_out):
        g = p_ref[0].astype(F32)
        for i in range(1, N_DEV):
            g = g + p_ref[i].astype(F32)
        m_new = ADAM_B1 * m_ref[...] + (1.0 - ADAM_B1) * g
        v_new = ADAM_B2 * v_ref[...] + (1.0 - ADAM_B2) * (g * g)
        m_hat = m_new / (1.0 - ADAM_B1 ** ADAM_STEP)
        v_hat = v_new / (1.0 - ADAM_B2 ** ADAM_STEP)
        g_out[...] = g
        d_out[...] = -ADAM_LR * (m_hat / (jnp.sqrt(v_hat) + ADAM_EPS) + ADAM_WD * w_ref[...])
        m_out[...] = m_new
        v_out[...] = v_new

    blk = pl.BlockSpec((rb, c), lambda i: (i, 0))
    return pl.pallas_call(
        body, name=name,
        grid=(r // rb,),
        in_specs=[pl.BlockSpec((N_DEV, rb, c), lambda i: (0, i, 0)), blk, blk, blk],
        out_specs=[blk] * 4,
        out_shape=[jax.ShapeDtypeStruct((r, c), F32)] * 4,
        compiler_params=_params(("parallel",)),
    )(parts, w, m, v)


def _pad_lanes(vec, n=LANES):
    return jnp.pad(vec, ((0, 0), (0, n - vec.shape[1])))


def local_step(x, target, w):
    n_seq, seq, _ = x.shape
    rows = n_seq * seq
    x = x.reshape(rows, D_MODEL)
    target = target.reshape(rows, D_MODEL)
    mx = lambda a: a.astype(MXU)

    splits = [sum(IN_WIDTHS[:i]) for i in range(len(IN_WIDTHS) + 1)]
    w_in = w["w_in"]
    part = lambda i: w_in[:, splits[i]:splits[i + 1]]
    w_z, w_xbc, w_gs, w_ga = mx(part(0)), mx(part(1)), mx(part(6)), mx(part(7))
    w_dt = mx(_pad_lanes(part(2)))
    w_qkv = [mx(jnp.concatenate([part(3 + t)[:, g * ATT_OUT:(g + 1) * ATT_OUT] for t in range(3)], axis=1))
             for g in range(ATT_GROUPS)]
    w_sp, w_ap, w_o, w_d = mx(w["w_ssd_proj"]), mx(w["w_attn_proj"]), mx(w["w_out"]), mx(w["w_down"])
    w_ug, w_uv = mx(w["w_up"][:, :D_FF]), mx(w["w_up"][:, D_FF:])
    conv_w, conv_b = w["ssd_conv_w"], w["ssd_conv_b"]
    fconv_w, fconv_b = w["ffn_conv_w"], w["ffn_conv_b"]
    dt_bias, a_log, d_skip = _pad_lanes(w["dt_bias"]), _pad_lanes(w["a_log"]), _pad_lanes(w["d_skip"])
    g1, g2, gn, gq, gk = w["norm1_g"], w["norm2_g"], w["ssd_norm_g"], w["q_norm_g"], w["k_norm_g"]

    tb = min(512, seq)
    tbm = min(256, seq)
    cw = 1024
    rw = lambda fn, name, ncol, ins, params=(), outs=(), accs=(), tb_=tb: rowwise(
        fn, name, rows, seq, tb_, ncol, ins, params, outs, accs)

    (h,) = rw(lambda ctx, xv, g: _rms_fwd(xv, g), "rms1_fwd", 1, [(x, D_MODEL, 0, None)], [(g1, None, 0)],
              [(D_MODEL, D_MODEL, 0, MXU)])
    z = matmul(h, w_z, "mm_z")
    xbc = matmul(h, w_xbc, "mm_xbc")
    dtraw = matmul(h, w_dt, "mm_dt")
    qkv = [matmul(h, w_qkv[g], f"mm_qkv{g}") for g in range(ATT_GROUPS)]
    gs = matmul(h, w_gs, "mm_gs")
    ga = matmul(h, w_ga, "mm_ga")

    def conv_silu(ctx, xh, wv, bv):
        return _silu(bv + _conv_prev(xh[0], xh[1], wv, ctx.first, SSD_CONV))

    (xact,) = rw(conv_silu, "ssd_conv_fwd", CONV_DIM // cw, [(xbc, cw, 0, "prev")],
                 [(conv_w, cw, 0), (conv_b, cw, 0)], [(CONV_DIM, cw, 0, F32)])
    y, sin = ssd_fwd(xact, dtraw, dt_bias, a_log, d_skip, n_seq, seq)

    def gated_norm(ctx, yv, zv, g):
        yz = yv * _silu(zv)
        return jnp.concatenate([_rms_fwd(yz[:, i:i + NORM_GROUP], g[:, i:i + NORM_GROUP])
                                for i in range(0, cw, NORM_GROUP)], axis=1)

    (y_ssd,) = rw(gated_norm, "ssd_post_fwd", D_INNER // cw, [(y, cw, 0, None), (z, cw, 0, None)], [(gn, cw, 0)],
                  [(D_INNER, cw, 0, MXU)])

    att = [attn_fwd(qkv[g], gq, gk, n_seq, seq, ATT_DILATIONS[g], f"attn_fwd{g}") for g in range(ATT_GROUPS)]

    def combine(ctx, o0, o1, o2, l0, l1, l2):
        mxl = jnp.maximum(jnp.maximum(l0, l1), l2)
        e = [jnp.exp(l - mxl) for l in (l0, l1, l2)]
        inv = 1.0 / (e[0] + e[1] + e[2])
        ws = [ei * inv for ei in e]
        out = sum(_expand_heads(wi) * oi for wi, oi in zip(ws, (o0, o1, o2)))
        return (out, *ws)

    y_attn, wt0, wt1, wt2 = rw(
        combine, "attn_combine", 1,
        [(att[g][0], ATT_OUT, 0, None) for g in range(3)] + [(att[g][1], LANES, 0, None) for g in range(3)], [],
        [(ATT_OUT, ATT_OUT, 0, F32)] + [(LANES, LANES, 0, F32)] * 3)
    wts = (wt0, wt1, wt2)

    ps = matmul(y_ssd, w_sp, "mm_ssd_proj")
    pa = matmul(y_attn, w_ap, "mm_attn_proj")
    (merged,) = rw(lambda ctx, a, b, c, d: _sigmoid(c) * a + _sigmoid(d) * b, "merge_fwd", D_MODEL // cw,
                   [(ps, cw, 0, None), (pa, cw, 0, None), (gs, cw, 0, None), (ga, cw, 0, None)], [],
                   [(D_MODEL, cw, 0, MXU)])
    x1 = matmul(merged, w_o, "mm_out", add=x)
    (h2,) = rw(lambda ctx, xv, g: _rms_fwd(xv, g), "rms2_fwd", 1, [(x1, D_MODEL, 0, None)], [(g2, None, 0)],
               [(D_MODEL, D_MODEL, 0, MXU)])
    up_g = matmul(h2, w_ug, "mm_up_g")
    up_v = matmul(h2, w_uv, "mm_up_v")
    fw = D_FF // 2
    nfc = D_FF // fw

    def mlp_act(ctx, ug, uv, wg, wv, bg, bv):
        cg = bg + _conv_prev(ug[0], ug[1], wg, ctx.first, FFN_CONV)
        cv = bv + _conv_prev(uv[0], uv[1], wv, ctx.first, FFN_CONV)
        return _silu(cg) * cv

    (act,) = rw(mlp_act, "mlp_act_fwd", nfc, [(up_g, fw, 0, "prev"), (up_v, fw, 0, "prev")],
                [(fconv_w, fw, 0), (fconv_w, fw, nfc), (fconv_b, fw, 0), (fconv_b, fw, nfc)], [(D_FF, fw, 0, MXU)],
                tb_=tbm)
    x2 = matmul(act, w_d, "mm_down", add=x1)

    def loss_fn(ctx, xv, tv):
        d = xv - tv
        return d * (1.0 / D_MODEL), jnp.sum(d * d, axis=0, keepdims=True)

    dx2, sq = rw(loss_fn, "loss", 1, [(x2, D_MODEL, 0, None), (target, D_MODEL, 0, None)], [],
                 [(D_MODEL, D_MODEL, 0, F32)], [(1, D_MODEL)])

    grads = {}
    dact = matmul(dx2, w_d, "mm_d_act", tb=True)
    grads["w_down"] = matmul(act, dx2, "mm_dw_down", ta=True)

    def mlp_bwd(ctx, da, ug, uv, wg, wv, bg, bv):
        cg, cg_n = _conv_pre(ug, wg, bg, ctx.first, FFN_CONV)
        cv, cv_n = _conv_pre(uv, wv, bv, ctx.first, FFN_CONV)
        da_c, da_n = da
        dup_g_, dwg, dbg = _conv_bwd(da_c * cv * _silu_grad(cg), da_n * cv_n * _silu_grad(cg_n), ug, wg, ctx, FFN_CONV)
        dup_v_, dwv, dbv = _conv_bwd(da_c * _silu(cg), da_n * _silu(cg_n), uv, wv, ctx, FFN_CONV)
        return dup_g_, dup_v_, dwg, dbg, dwv, dbv

    dup_g, dup_v, dfw_g, dfb_g, dfw_v, dfb_v = rw(
        mlp_bwd, "mlp_bwd", nfc, [(dact, fw, 0, "next"), (up_g, fw, 0, "both"), (up_v, fw, 0, "both")],
        [(fconv_w, fw, 0), (fconv_w, fw, nfc), (fconv_b, fw, 0), (fconv_b, fw, nfc)],
        [(D_FF, fw, 0, MXU), (D_FF, fw, 0, MXU)], [(FFN_CONV, fw), (1, fw), (FFN_CONV, fw), (1, fw)], tb_=tbm)
    grads["ffn_conv_w"] = jnp.concatenate([dfw_g, dfw_v], axis=1)
    grads["ffn_conv_b"] = jnp.concatenate([dfb_g, dfb_v], axis=1)
    dh2 = matmul(dup_g, w_ug, "mm_dh2_g", tb=True)
    dh2 = matmul(dup_v, w_uv, "mm_dh2_v", tb=True, add=dh2)
    grads["w_up"] = jnp.concatenate([matmul(h2, dup_g, "mm_dw_up_g", ta=True),
                                     matmul(h2, dup_v, "mm_dw_up_v", ta=True)], axis=1)

    def rms_bwd_fn(ctx, xv, dh_, dres, g):
        dxv, dg = _rms_bwd(xv, g, dh_)
        return dres + dxv, dg

    dx1, grads["norm2_g"] = rw(rms_bwd_fn, "rms2_bwd", 1,
                               [(x1, D_MODEL, 0, None), (dh2, D_MODEL, 0, None), (dx2, D_MODEL, 0, None)],
                               [(g2, None, 0)], [(D_MODEL, D_MODEL, 0, F32)], [(1, D_MODEL)])

    dmerged = matmul(dx1, w_o, "mm_d_merged", tb=True)
    grads["w_out"] = matmul(merged, dx1, "mm_dw_out", ta=True)

    def merge_bwd(ctx, dm, a, b, c, d):
        sc, sd = _sigmoid(c), _sigmoid(d)
        return dm * sc, dm * sd, dm * a * sc * (1.0 - sc), dm * b * sd * (1.0 - sd)

    dps, dpa, dgs, dga = rw(merge_bwd, "merge_bwd", D_MODEL // cw,
                            [(dmerged, cw, 0, None), (ps, cw, 0, None), (pa, cw, 0, None), (gs, cw, 0, None),
                             (ga, cw, 0, None)], [], [(D_MODEL, cw, 0, MXU)] * 4)
    dy_ssd = matmul(dps, w_sp, "mm_d_y_ssd", tb=True)
    grads["w_ssd_proj"] = matmul(y_ssd, dps, "mm_dw_ssd_proj", ta=True)
    dy_attn = matmul(dpa, w_ap, "mm_d_y_attn", tb=True)
    grads["w_attn_proj"] = matmul(y_attn, dpa, "mm_dw_attn_proj", ta=True)

    (rsum,) = rw(lambda ctx, a, b: _reduce_heads(a * b), "attn_rsum", 1,
                 [(dy_attn, ATT_OUT, 0, None), (y_attn, ATT_OUT, 0, None)], [], [(LANES, LANES, 0, F32)])
    dqkv, dgq, dgk = [], 0.0, 0.0
    for g in range(ATT_GROUPS):
        d_, a_, b_ = attn_bwd(qkv[g], gq, gk, dy_attn, att[g][1], wts[g], rsum, n_seq, seq, ATT_DILATIONS[g],
                              f"attn_bwd{g}")
        dqkv.append(d_)
        dgq, dgk = dgq + a_, dgk + b_
    grads["q_norm_g"], grads["k_norm_g"] = dgq, dgk

    def gated_norm_bwd(ctx, dyn, yv, zv, g):
        sz = _silu(zv)
        yz = yv * sz
        dyz, dgs_ = [], []
        for i in range(0, cw, NORM_GROUP):
            a, b = _rms_bwd(yz[:, i:i + NORM_GROUP], g[:, i:i + NORM_GROUP], dyn[:, i:i + NORM_GROUP])
            dyz.append(a)
            dgs_.append(b)
        dyz = jnp.concatenate(dyz, axis=1)
        return dyz * sz, dyz * yv * _silu_grad(zv), jnp.concatenate(dgs_, axis=1)

    dy, dz, grads["ssd_norm_g"] = rw(gated_norm_bwd, "ssd_post_bwd", D_INNER // cw,
                                     [(dy_ssd, cw, 0, None), (y, cw, 0, None), (z, cw, 0, None)], [(gn, cw, 0)],
                                     [(D_INNER, cw, 0, F32), (D_INNER, cw, 0, MXU)], [(1, cw)])
    dxact, ddt, dbias, dalog, ddskip = ssd_bwd(xact, dtraw, dt_bias, a_log, d_skip, sin, dy, n_seq, seq)
    grads["dt_bias"], grads["a_log"], grads["d_skip"] = dbias[:, :SSD_H], dalog[:, :SSD_H], ddskip[:, :SSD_H]

    def conv_silu_bwd(ctx, dxa, xin, wv, bv):
        pre, pre_n = _conv_pre(xin, wv, bv, ctx.first, SSD_CONV)
        return _conv_bwd(dxa[0] * _silu_grad(pre), dxa[1] * _silu_grad(pre_n), xin, wv, ctx, SSD_CONV)

    dxbc, grads["ssd_conv_w"], grads["ssd_conv_b"] = rw(
        conv_silu_bwd, "ssd_conv_bwd", CONV_DIM // cw, [(dxact, cw, 0, "next"), (xbc, cw, 0, "both")],
        [(conv_w, cw, 0), (conv_b, cw, 0)], [(CONV_DIM, cw, 0, MXU)], [(SSD_CONV, cw), (1, cw)])

    pieces = [(dz, w_z, "z"), (dxbc, w_xbc, "xbc"), (ddt, w_dt, "dt"), (dgs, w_gs, "gs"), (dga, w_ga, "ga")]
    pieces += [(dqkv[g], w_qkv[g], f"qkv{g}") for g in range(ATT_GROUPS)]
    dh, dws = None, {}
    for dpart, wpart, tag in pieces:
        dh = matmul(dpart, wpart, f"mm_dh_{tag}", tb=True, add=dh)
        dws[tag] = matmul(h, dpart, f"mm_dw_{tag}", ta=True)
    dq_parts = [[dws[f"qkv{g}"][:, t * ATT_OUT:(t + 1) * ATT_OUT] for g in range(ATT_GROUPS)] for t in range(3)]
    grads["w_in"] = jnp.concatenate(
        [dws["z"], dws["xbc"], dws["dt"][:, :SSD_H]] + [p for t in range(3) for p in dq_parts[t]] + [dws["gs"], dws["ga"]],
        axis=1)
    grad_x, grads["norm1_g"] = rw(rms_bwd_fn, "rms1_bwd", 1,
                                  [(x, D_MODEL, 0, None), (dh, D_MODEL, 0, None), (dx1, D_MODEL, 0, None)],
                                  [(g1, None, 0)], [(D_MODEL, D_MODEL, 0, F32)], [(1, D_MODEL)])
    return sq, grad_x.reshape(n_seq, seq, D_MODEL), grads


COL_SHARDED = ("w_in", "ssd_conv_w", "w_attn_proj", "w_up", "ffn_conv_w")
ROW_SHARDED = ("w_ssd_proj", "w_out", "w_down")
MATRICES = ("w_in", "w_attn_proj", "w_up", "w_ssd_proj", "w_out", "w_down")
REPLICATED = ("norm1_g", "ssd_conv_b", "dt_bias", "a_log", "d_skip", "ssd_norm_g", "q_norm_g", "k_norm_g",
              "norm2_g", "ffn_conv_b")
WEIGHTS = ("norm1_g", "w_in", "ssd_conv_w", "ssd_conv_b", "dt_bias", "a_log", "d_skip", "ssd_norm_g", "w_ssd_proj",
           "q_norm_g", "k_norm_g", "w_attn_proj", "w_out", "norm2_g", "w_up", "ffn_conv_w", "ffn_conv_b", "w_down")
PACK_ROWS, PACK_COLS = 8, 2048


def _pack(vals):
    flat = jnp.concatenate([vals[n].reshape(-1) for n in REPLICATED])
    return jnp.pad(flat, (0, PACK_ROWS * PACK_COLS - flat.shape[0])).reshape(PACK_ROWS, PACK_COLS)


def _unpack(packed, like):
    flat = packed.reshape(-1)
    out, pos = {}, 0
    for n in REPLICATED:
        size = like[n].size
        out[n] = flat[pos:pos + size].reshape(like[n].shape)
        pos += size
    return out


def step(x, target, wsh, msh, vsh):
    sharded = COL_SHARDED + ROW_SHARDED
    narrow = lambda n, a: a.astype(MXU) if n in MATRICES else a
    gathered = exchange([(narrow(n, wsh[n]), "gather") for n in sharded], "ag_weights")
    full = {n: wsh[n] for n in REPLICATED}
    for n, g in zip(sharded, gathered):
        if n in COL_SHARDED:
            full[n] = jnp.transpose(g, (1, 0, 2)).reshape(g.shape[1], N_DEV * g.shape[2])
        else:
            full[n] = g.reshape(N_DEV * g.shape[1], g.shape[2])

    sq, grad_x, grads = local_step(x, target, full)

    slabs = []
    for n in sharded:
        g = narrow(n, grads[n])
        if n in COL_SHARDED:
            slabs.append(jnp.transpose(g.reshape(g.shape[0], N_DEV, g.shape[1] // N_DEV), (1, 0, 2)))
        else:
            slabs.append(g.reshape(N_DEV, g.shape[0] // N_DEV, g.shape[1]))
    packed_g = _pack({n: grads[n] for n in REPLICATED})
    received = exchange([(s, "scatter") for s in slabs] + [(packed_g, "gather")], "rs_grads")

    out_g, out_d, out_m, out_v = {}, {}, {}, {}
    for n, parts in zip(sharded, received[:-1]):
        out_g[n], out_d[n], out_m[n], out_v[n] = adamw(parts, wsh[n], msh[n], vsh[n], f"adamw_{n}")
    pk = adamw(received[-1], _pack(wsh), _pack(msh), _pack(vsh), "adamw_small")
    for dst, packed in zip((out_g, out_d, out_m, out_v), pk):
        dst.update(_unpack(packed, wsh))
    loss = lax.psum(0.5 * jnp.sum(sq) / D_MODEL, ("x", "y", "c"))
    return loss, grad_x, out_g, out_d, out_m, out_v


def kernel(x, norm1_g, w_in, ssd_conv_w, ssd_conv_b, dt_bias, a_log, d_skip, ssd_norm_g, w_ssd_proj, q_norm_g, k_norm_g, w_attn_proj, w_out, norm2_g, w_up, ffn_conv_w, ffn_conv_b, w_down, loss_target, m_norm1_g, m_w_in, m_ssd_conv_w, m_ssd_conv_b, m_dt_bias, m_a_log, m_d_skip, m_ssd_norm_g, m_w_ssd_proj, m_q_norm_g, m_k_norm_g, m_w_attn_proj, m_w_out, m_norm2_g, m_w_up, m_ffn_conv_w, m_ffn_conv_b, m_w_down, v_norm1_g, v_w_in, v_ssd_conv_w, v_ssd_conv_b, v_dt_bias, v_a_log, v_d_skip, v_ssd_norm_g, v_w_ssd_proj, v_q_norm_g, v_k_norm_g, v_w_attn_proj, v_w_out, v_norm2_g, v_w_up, v_ffn_conv_w, v_ffn_conv_b, v_w_down):
    ws = (norm1_g, w_in, ssd_conv_w, ssd_conv_b, dt_bias, a_log, d_skip, ssd_norm_g, w_ssd_proj, q_norm_g, k_norm_g,
          w_attn_proj, w_out, norm2_g, w_up, ffn_conv_w, ffn_conv_b, w_down)
    ms = (m_norm1_g, m_w_in, m_ssd_conv_w, m_ssd_conv_b, m_dt_bias, m_a_log, m_d_skip, m_ssd_norm_g, m_w_ssd_proj,
          m_q_norm_g, m_k_norm_g, m_w_attn_proj, m_w_out, m_norm2_g, m_w_up, m_ffn_conv_w, m_ffn_conv_b, m_w_down)
    vs = (v_norm1_g, v_w_in, v_ssd_conv_w, v_ssd_conv_b, v_dt_bias, v_a_log, v_d_skip, v_ssd_norm_g, v_w_ssd_proj,
          v_q_norm_g, v_k_norm_g, v_w_attn_proj, v_w_out, v_norm2_g, v_w_up, v_ffn_conv_w, v_ffn_conv_b, v_w_down)
    strip = lambda a: a[0] if a.ndim == 3 else a
    wsh = {n: strip(a) for n, a in zip(WEIGHTS, ws)}
    msh = {n: strip(a) for n, a in zip(WEIGHTS, ms)}
    vsh = {n: strip(a) for n, a in zip(WEIGHTS, vs)}
    loss, grad_x, g, d, m, v = step(x, loss_target, wsh, msh, vsh)
    lead = lambda dct: [dct[n][None] if a.ndim == 3 else dct[n] for n, a in zip(WEIGHTS, ws)]
    return (loss, grad_x, *lead(g), *lead(d), *lead(m), *lead(v))
```

```python
import jax
import jax.numpy as jnp
from jax import lax
from jax.experimental import pallas as pl
from jax.experimental.pallas import tpu as pltpu

F32 = jnp.float32
BF16 = jnp.bfloat16
MXU = jnp.bfloat16
HIGHEST = lax.Precision.HIGHEST
VMEM_LIMIT_BYTES = 48 * 1024 * 1024
SUBLANES = 8
LANES = 128
N_DEV = 8

D_MODEL = 1024
D_INNER = 2048
SSD_P = 64
SSD_H = 32
SSD_G = 8
SSD_K = SSD_H // SSD_G
SSD_N = 128
SSD_Q = 128
SSD_CONV = 4
CONV_DIM = D_INNER + 2 * SSD_G * SSD_N
NORM_GROUP = D_INNER // SSD_G
ATT_GROUPS = 3
ATT_H = 8
ATT_HD = 64
ATT_BLK = 128
ATT_OUT = ATT_H * ATT_HD
ATT_DILATIONS = (1, 4, 16)
ATT_SCALE = ATT_HD ** -0.5
D_FF = 2816
FFN_CONV = 3
EPS = 1e-6
NEG = -1e30
IN_WIDTHS = (D_INNER, CONV_DIM, SSD_H, 3 * ATT_OUT, 3 * ATT_OUT, 3 * ATT_OUT, D_MODEL, D_MODEL)

ADAM_LR = 0.001
ADAM_B1 = 0.9
ADAM_B2 = 0.999
ADAM_EPS = 1e-08
ADAM_WD = 0.01
ADAM_STEP = 10


def _mm(a, b, dims):
    return lax.dot_general(a.astype(MXU), b.astype(MXU), (dims, ((), ())), preferred_element_type=F32)


def _dot_nn(a, b):
    return _mm(a, b, ((1,), (0,)))


def _dot_nt(a, b):
    return _mm(a, b, ((1,), (1,)))


def _dot_tn(a, b):
    return _mm(a, b, ((0,), (0,)))


def _dot_f32(a, b):
    return lax.dot_general(a, b, (((1,), (0,)), ((), ())), precision=HIGHEST, preferred_element_type=F32)


def _sigmoid(x):
    return 1.0 / (1.0 + jnp.exp(-x))


def _silu(x):
    return x * _sigmoid(x)


def _silu_grad(x):
    s = _sigmoid(x)
    return s * (1.0 + x * (1.0 - s))


def _softplus(x):
    return jnp.maximum(x, 0.0) + jnp.log(1.0 + jnp.exp(-jnp.abs(x)))


def _rms_fwd(x, g):
    r = lax.rsqrt(jnp.mean(x * x, axis=-1, keepdims=True) + EPS)
    return x * r * g


def _rms_bwd(x, g, dy):
    r = lax.rsqrt(jnp.mean(x * x, axis=-1, keepdims=True) + EPS)
    xh = x * r
    dyg = dy * g
    dx = r * (dyg - xh * jnp.mean(dyg * xh, axis=-1, keepdims=True))
    return dx, jnp.sum(dy * xh, axis=0, keepdims=True)


def _onehot_row(h, n=LANES):
    return (lax.broadcasted_iota(jnp.int32, (1, n), 1) == h).astype(F32)


def _onehot_col(h, n=LANES):
    return (lax.broadcasted_iota(jnp.int32, (n, 1), 0) == h).astype(F32)


def _head_expand_matrix():
    r = lax.broadcasted_iota(jnp.int32, (LANES, ATT_OUT), 0)
    c = lax.broadcasted_iota(jnp.int32, (LANES, ATT_OUT), 1)
    return (c // ATT_HD == r).astype(F32)


def _split_bf16(x, parts):
    out = []
    for _ in range(parts - 1):
        hi = x.astype(BF16).astype(F32)
        out.append(hi)
        x = x - hi
    out.append(x)
    return out


def _expand_heads(w):
    e = _head_expand_matrix()
    return sum(_dot_nn(p, e) for p in _split_bf16(w, 2))


def _reduce_heads(x):
    e = _head_expand_matrix()
    return sum(_dot_nt(p, e) for p in _split_bf16(x, 3))


def _shift_prev(cur, halo, s, first):
    if s == 0:
        return cur
    rolled = pltpu.roll(cur, s, 0)
    hr = jnp.where(first, 0.0, pltpu.roll(halo, s, 0))
    rows = lax.broadcasted_iota(jnp.int32, halo.shape, 0)
    head = jnp.where(rows < s, hr, rolled[:SUBLANES])
    if cur.shape[0] == SUBLANES:
        return head
    return jnp.concatenate([head, rolled[SUBLANES:]], axis=0)


def _shift_next(cur, halo, s, last):
    if s == 0:
        return cur
    tb = cur.shape[0]
    rolled = pltpu.roll(cur, tb - s, 0)
    hr = jnp.where(last, 0.0, pltpu.roll(halo, SUBLANES - s, 0))
    rows = lax.broadcasted_iota(jnp.int32, halo.shape, 0)
    tail = jnp.where(rows >= SUBLANES - s, hr, rolled[tb - SUBLANES:])
    return jnp.concatenate([rolled[:tb - SUBLANES], tail], axis=0)


def _conv_prev(x, halo, w, first, taps):
    acc = None
    for i in range(taps):
        term = w[i:i + 1, :] * _shift_prev(x, halo, taps - 1 - i, first)
        acc = term if acc is None else acc + term
    return acc


def _conv_pre(x, w, b, first, taps):
    cur, prev8, next8 = x
    tail = cur[cur.shape[0] - SUBLANES:]
    return b + _conv_prev(cur, prev8, w, first, taps), b + _conv_prev(next8, tail, w, False, taps)


def _conv_bwd(dpre, dpre_next8, x, w, ctx, taps):
    cur, prev8, _ = x
    dx, dws = None, []
    for i in range(taps):
        term = w[i:i + 1, :] * _shift_next(dpre, dpre_next8, taps - 1 - i, ctx.last)
        dx = term if dx is None else dx + term
        dws.append(jnp.sum(dpre * _shift_prev(cur, prev8, taps - 1 - i, ctx.first), axis=0, keepdims=True))
    return dx, jnp.concatenate(dws, axis=0), jnp.sum(dpre, axis=0, keepdims=True)


def _params(sem):
    return pltpu.CompilerParams(dimension_semantics=sem, vmem_limit_bytes=VMEM_LIMIT_BYTES)


def _pick(dim, target):
    if dim <= target:
        return dim
    best = None
    for t in range(LANES, target + 1, LANES):
        if dim % t == 0:
            best = t
    assert best is not None, (dim, target)
    return best


MATMUL_VMEM_BUDGET = 34 * 1024 * 1024


def _matmul_tiles(m, n, k, a_bytes, b_bytes, add_bytes, out_bytes):
    tn = _pick(n, 1536)
    for tk_target in (k, 4096, 2048, 1024, 512):
        tk = _pick(k, tk_target)
        for tm_target in (2048, 1024, 512, 256, 128):
            tm = _pick(m, tm_target) if m > tm_target else m
            if tm > 2048:
                continue
            blocks = tm * tk * a_bytes + tk * tn * b_bytes + tm * tn * (add_bytes + out_bytes)
            need = 2 * blocks + (tm * tn * 4 if tk < k else 0) + tm * tn * 4
            if need <= MATMUL_VMEM_BUDGET:
                return tm, tn, tk
    raise ValueError((m, n, k))


def matmul(a, b, name, ta=False, tb=False, add=None, out_dtype=F32):
    assert not (ta and tb)
    m, k = (a.shape[1], a.shape[0]) if ta else a.shape
    n = b.shape[0] if tb else b.shape[1]
    assert (b.shape[1] if tb else b.shape[0]) == k
    tm, tn, tk = _matmul_tiles(m, n, k, a.dtype.itemsize, b.dtype.itemsize,
                               0 if add is None else add.dtype.itemsize, jnp.dtype(out_dtype).itemsize)
    nk = k // tk
    dims = ((0,), (0,)) if ta else (((1,), (1,)) if tb else ((1,), (0,)))

    def body(*refs):
        if add is None:
            a_ref, b_ref, o_ref = refs[:3]
        else:
            a_ref, b_ref, add_ref, o_ref = refs[:4]

        def finish(r):
            if add is not None:
                r = r + add_ref[...].astype(F32)
            o_ref[...] = r.astype(out_dtype)

        if nk == 1:
            finish(_mm(a_ref[...], b_ref[...], dims))
            return
        acc = refs[-1]
        kk = pl.program_id(2)

        @pl.when(kk == 0)
        def _():
            acc[...] = jnp.zeros_like(acc)

        acc[...] += _mm(a_ref[...], b_ref[...], dims)

        @pl.when(kk == nk - 1)
        def _():
            finish(acc[...])

    a_spec = pl.BlockSpec((tk, tm), lambda i, j, kk: (kk, i)) if ta else pl.BlockSpec((tm, tk), lambda i, j, kk: (i, kk))
    b_spec = pl.BlockSpec((tn, tk), lambda i, j, kk: (j, kk)) if tb else pl.BlockSpec((tk, tn), lambda i, j, kk: (kk, j))
    in_specs = [a_spec, b_spec]
    args = [a, b]
    if add is not None:
        in_specs.append(pl.BlockSpec((tm, tn), lambda i, j, kk: (i, j)))
        args.append(add)
    return pl.pallas_call(
        body, name=name,
        grid=(m // tm, n // tn, nk),
        in_specs=in_specs,
        out_specs=pl.BlockSpec((tm, tn), lambda i, j, kk: (i, j)),
        out_shape=jax.ShapeDtypeStruct((m, n), out_dtype),
        scratch_shapes=[] if nk == 1 else [pltpu.VMEM((tm, tn), F32)],
        compiler_params=_params(("parallel", "parallel", "arbitrary")),
    )(*args)


class _Ctx:
    def __init__(self, first, last):
        self.first = first
        self.last = last


def rowwise(fn, name, rows, seq, tb, ncol, ins, params=(), outs=(), accs=()):
    assert rows % tb == 0 and seq % tb == 0 and tb % 16 == 0
    bps = seq // tb
    nrow = rows // tb
    r8 = tb // SUBLANES
    args, in_specs = [], []
    for arr, w, off, halo in ins:
        args.append(arr)
        in_specs.append(pl.BlockSpec((tb, w), lambda j, i, off=off: (i, off + j)))
        if halo in ("prev", "both"):
            args.append(arr)
            in_specs.append(pl.BlockSpec((SUBLANES, w), lambda j, i, off=off: (jnp.maximum(i * r8 - 1, 0), off + j)))
        if halo in ("next", "both"):
            args.append(arr)
            in_specs.append(pl.BlockSpec(
                (SUBLANES, w), lambda j, i, off=off: (jnp.minimum((i + 1) * r8, rows // SUBLANES - 1), off + j)))
    for arr, w, off in params:
        args.append(arr)
        if w is None:
            in_specs.append(pl.BlockSpec(arr.shape, lambda j, i: (0, 0)))
        else:
            in_specs.append(pl.BlockSpec((arr.shape[0], w), lambda j, i, off=off: (0, off + j)))
    out_shape, out_specs = [], []
    for total, w, off, dt in outs:
        out_shape.append(jax.ShapeDtypeStruct((rows, total), dt))
        out_specs.append(pl.BlockSpec((tb, w), lambda j, i, off=off: (i, off + j)))
    for r, w in accs:
        out_shape.append(jax.ShapeDtypeStruct((r, ncol * w), F32))
        out_specs.append(pl.BlockSpec((r, w), lambda j, i: (0, j)))
    n_out, n_acc = len(outs), len(accs)

    def body(*refs):
        i = pl.program_id(1)
        pos = 0
        vals = []
        for _, _, _, halo in ins:
            cur = refs[pos][...]
            pos += 1
            if halo is None:
                vals.append(cur)
            elif halo == "both":
                vals.append((cur, refs[pos][...], refs[pos + 1][...]))
                pos += 2
            else:
                vals.append((cur, refs[pos][...]))
                pos += 1
        for _ in params:
            vals.append(refs[pos][...])
            pos += 1
        ctx = _Ctx(i % bps == 0, i % bps == bps - 1)
        res = fn(ctx, *vals)
        if not isinstance(res, (tuple, list)):
            res = (res,)
        assert len(res) == n_out + n_acc
        for q in range(n_out):
            refs[pos + q][...] = res[q].astype(refs[pos + q].dtype)
        for q in range(n_acc):
            ref, val = refs[pos + n_out + q], res[n_out + q]

            @pl.when(i == 0)
            def _(ref=ref, val=val):
                ref[...] = val

            @pl.when(i != 0)
            def _(ref=ref, val=val):
                ref[...] += val

    res = pl.pallas_call(
        body, name=name,
        grid=(ncol, nrow),
        in_specs=in_specs,
        out_specs=out_specs,
        out_shape=out_shape,
        compiler_params=_params(("parallel", "arbitrary")),
    )(*args)
    return res


GROUP_W = SSD_K * SSD_P


def _tri(lower):
    r = lax.broadcasted_iota(jnp.int32, (SSD_Q, SSD_Q), 0)
    c = lax.broadcasted_iota(jnp.int32, (SSD_Q, SSD_Q), 1)
    return r >= c if lower else r <= c


def _group_masks():
    lane = lax.broadcasted_iota(jnp.int32, (1, GROUP_W), 1) // SSD_P
    row = lax.broadcasted_iota(jnp.int32, (GROUP_W, 1), 0) // SSD_P
    return [lane == k for k in range(SSD_K)], [row == k for k in range(SSD_K)]


def _per_head(masks, vals):
    out = jnp.where(masks[0], vals[0], 0.0)
    for m, v in zip(masks[1:], vals[1:]):
        out = jnp.where(m, v, out)
    return out


def _headsum(prod, g):
    j = lax.broadcasted_iota(jnp.int32, (GROUP_W, LANES), 0) // SSD_P
    lane = lax.broadcasted_iota(jnp.int32, (GROUP_W, LANES), 1)
    e = (lane == g * SSD_K + j).astype(F32)
    return sum(_dot_nn(p, e) for p in _split_bf16(prod, 2))


def ssd_fwd(xact, dtraw, dt_bias, a_log, d_skip, n_seq, seq):
    nc = seq // SSD_Q
    rows = n_seq * seq

    def body(xact_ref, dtraw_ref, bias_ref, alog_ref, dskip_ref, y_ref, sin_ref, state, cs_s, cst_s, dt_s):
        c = pl.program_id(1)

        @pl.when(c == 0)
        def _():
            state[...] = jnp.zeros_like(state)

        sin_ref[0] = state[...]
        dt = _softplus(dtraw_ref[...] + bias_ref[...])
        a = dt * (-jnp.exp(alog_ref[...]))
        cs = _dot_f32(_tri(True).astype(F32), a)
        cs_s[...] = cs
        cst_s[...] = cs.T
        dt_s[...] = dt
        causal = _tri(True)
        lane_masks, row_masks = _group_masks()
        for g in range(SSD_G):
            heads = [g * SSD_K + k for k in range(SSD_K)]
            bg = xact_ref[:, pl.ds(D_INNER + g * SSD_N, SSD_N)]
            cg = xact_ref[:, pl.ds(D_INNER + (SSD_G + g) * SSD_N, SSD_N)]
            xg = xact_ref[:, pl.ds(g * GROUP_W, GROUP_W)]
            cols = [cs_s[:, pl.ds(h, 1)] for h in heads]
            lasts = [cs_s[pl.ds(SSD_Q - 1, 1), pl.ds(h, 1)] for h in heads]
            xdg = xg * _per_head(lane_masks, [dt_s[:, pl.ds(h, 1)] for h in heads])
            sg = state[g]
            gm = _dot_nt(cg, bg)
            y = (_per_head(lane_masks, [jnp.exp(c_) for c_ in cols]) * _dot_nt(cg, sg)
                 + _per_head(lane_masks, [dskip_ref[:, pl.ds(h, 1)] for h in heads]) * xg)
            for k, h in enumerate(heads):
                decay = jnp.exp(jnp.where(causal, cols[k] - cst_s[pl.ds(h, 1), :], NEG))
                y = y + _dot_nn(gm * decay, jnp.where(lane_masks[k], xdg, 0.0))
            y_ref[:, pl.ds(g * GROUP_W, GROUP_W)] = y
            w = _per_head(lane_masks, [jnp.exp(l_ - c_) for l_, c_ in zip(lasts, cols)])
            state[g] = _per_head(row_masks, [jnp.exp(l_) for l_ in lasts]) * sg + _dot_tn(w * xdg, bg)

    vec = pl.BlockSpec((1, LANES), lambda b, c: (0, 0))
    return pl.pallas_call(
        body, name="ssd_fwd",
        grid=(n_seq, nc),
        in_specs=[pl.BlockSpec((SSD_Q, CONV_DIM), lambda b, c: (b * nc + c, 0)),
                  pl.BlockSpec((SSD_Q, LANES), lambda b, c: (b * nc + c, 0)), vec, vec, vec],
        out_specs=[pl.BlockSpec((SSD_Q, D_INNER), lambda b, c: (b * nc + c, 0)),
                   pl.BlockSpec((1, SSD_G, GROUP_W, SSD_N), lambda b, c: (b * nc + c, 0, 0, 0))],
        out_shape=[jax.ShapeDtypeStruct((rows, D_INNER), F32),
                   jax.ShapeDtypeStruct((n_seq * nc, SSD_G, GROUP_W, SSD_N), F32)],
        scratch_shapes=[pltpu.VMEM((SSD_G, GROUP_W, SSD_N), F32), pltpu.VMEM((SSD_Q, LANES), F32),
                        pltpu.VMEM((LANES, SSD_Q), F32), pltpu.VMEM((SSD_Q, LANES), F32)],
        compiler_params=_params(("arbitrary", "arbitrary")),
    )(xact, dtraw, dt_bias, a_log, d_skip)


def ssd_bwd(xact, dtraw, dt_bias, a_log, d_skip, sin, dy, n_seq, seq):
    nc = seq // SSD_Q
    rows = n_seq * seq

    def body(xact_ref, dtraw_ref, bias_ref, alog_ref, dskip_ref, sin_ref, dy_ref,
             dx_ref, ddt_ref, dbias_ref, dalog_ref, ddskip_ref, dstate, cs_s, cst_s, dt_s):
        b, c = pl.program_id(0), pl.program_id(1)

        @pl.when(c == 0)
        def _():
            dstate[...] = jnp.zeros_like(dstate)

        pre = dtraw_ref[...] + bias_ref[...]
        dt = _softplus(pre)
        a_neg = -jnp.exp(alog_ref[...])
        cs = _dot_f32(_tri(True).astype(F32), dt * a_neg)
        cs_s[...] = cs
        cst_s[...] = cs.T
        dt_s[...] = dt
        causal, anti = _tri(True), _tri(False)
        is_last_row = lax.broadcasted_iota(jnp.int32, (SSD_Q, 1), 0) == SSD_Q - 1
        lane_masks, row_masks = _group_masks()
        dcs_cf = jnp.zeros((SSD_Q, LANES), F32)
        dcs_rf = jnp.zeros((LANES, SSD_Q), F32)
        ddt_cf = jnp.zeros((SSD_Q, LANES), F32)
        dd_vec = jnp.zeros((1, LANES), F32)
        dlast_vec = jnp.zeros((1, LANES), F32)
        for g in range(SSD_G):
            heads = [g * SSD_K + k for k in range(SSD_K)]
            bg = xact_ref[:, pl.ds(D_INNER + g * SSD_N, SSD_N)]
            cg = xact_ref[:, pl.ds(D_INNER + (SSD_G + g) * SSD_N, SSD_N)]
            xg = xact_ref[:, pl.ds(g * GROUP_W, GROUP_W)]
            dyg = dy_ref[:, pl.ds(g * GROUP_W, GROUP_W)]
            cols = [cs_s[:, pl.ds(h, 1)] for h in heads]
            rws = [cst_s[pl.ds(h, 1), :] for h in heads]
            lasts = [cs_s[pl.ds(SSD_Q - 1, 1), pl.ds(h, 1)] for h in heads]
            e_lasts = [jnp.exp(l_) for l_ in lasts]
            dtg = _per_head(lane_masks, [dt_s[:, pl.ds(h, 1)] for h in heads])
            dskg = _per_head(lane_masks, [dskip_ref[:, pl.ds(h, 1)] for h in heads])
            e_col = _per_head(lane_masks, [jnp.exp(c_) for c_ in cols])
            w = _per_head(lane_masks, [jnp.exp(l_ - c_) for l_, c_ in zip(lasts, cols)])
            xdg = xg * dtg
            sg = sin_ref[0, g]
            dsn = dstate[g]
            gm = _dot_nt(cg, bg)
            gmt = _dot_nt(bg, cg)
            y_off = e_col * _dot_nt(cg, sg)
            d_cs = e_col * dyg
            dcg = _dot_nn(d_cs, sg)
            dsp = _dot_tn(d_cs, cg) + _per_head(row_masks, e_lasts) * dsn
            dbg = _dot_nn(w * xdg, dsn)
            dtt = _dot_nt(bg, dsn)
            dxd = w * dtt
            dw = _headsum(dtt * xdg * w, g)
            dcs_cf = dcs_cf + _headsum(dyg * y_off, g) - dw
            dlast_vec = dlast_vec + jnp.sum(dw, axis=0, keepdims=True)
            dsn_s = dsn * sg
            dgm = jnp.zeros((SSD_Q, SSD_Q), F32)
            for k, h in enumerate(heads):
                seg = cols[k] - rws[k]
                decay = jnp.exp(jnp.where(causal, seg, NEG))
                decay_t = jnp.exp(jnp.where(anti, -seg, NEG))
                dyk = jnp.where(lane_masks[k], dyg, 0.0)
                dm = _dot_nt(dyk, xdg)
                dxd = dxd + _dot_nn(gmt * decay_t, dyk)
                dseg = dm * gm * decay
                dgm = dgm + dm * decay
                oh_r = _onehot_row(h)
                dcs_cf = dcs_cf + jnp.sum(dseg, axis=1, keepdims=True) * oh_r
                dcs_rf = dcs_rf - _onehot_col(h) * jnp.sum(dseg, axis=0, keepdims=True)
                dlast_vec = dlast_vec + jnp.sum(jnp.where(row_masks[k], dsn_s, 0.0), keepdims=True) * e_lasts[k] * oh_r
            dx_ref[:, pl.ds(g * GROUP_W, GROUP_W)] = dxd * dtg + dskg * dyg
            ddt_cf = ddt_cf + _headsum(dxd * xg, g)
            dyx = jnp.broadcast_to(jnp.sum(dyg * xg, axis=0, keepdims=True), (SUBLANES, GROUP_W))
            dd_vec = dd_vec + _headsum(dyx, g)[0:1]
            dstate[g] = dsp
            dx_ref[:, pl.ds(D_INNER + g * SSD_N, SSD_N)] = dbg + _dot_tn(dgm, cg)
            dx_ref[:, pl.ds(D_INNER + (SSD_G + g) * SSD_N, SSD_N)] = dcg + _dot_nn(dgm, bg)
        dcs = dcs_cf + dcs_rf.T + jnp.where(is_last_row, dlast_vec, 0.0)
        da = _dot_f32(_tri(False).astype(F32), dcs)
        ddt = ddt_cf + da * a_neg
        ddtraw = ddt * _sigmoid(pre)
        ddt_ref[...] = ddtraw.astype(ddt_ref.dtype)
        dbias = jnp.sum(ddtraw, axis=0, keepdims=True)
        dalog = jnp.sum(da * dt, axis=0, keepdims=True) * a_neg
        first_step = jnp.logical_and(b == 0, c == 0)

        @pl.when(first_step)
        def _():
            dbias_ref[...] = dbias
            dalog_ref[...] = dalog
            ddskip_ref[...] = dd_vec

        @pl.when(jnp.logical_not(first_step))
        def _():
            dbias_ref[...] += dbias
            dalog_ref[...] += dalog
            ddskip_ref[...] += dd_vec

    def rowblk(b, c):
        return b * nc + (nc - 1 - c)

    vec = pl.BlockSpec((1, LANES), lambda b, c: (0, 0))
    return pl.pallas_call(
        body, name="ssd_bwd",
        grid=(n_seq, nc),
        in_specs=[pl.BlockSpec((SSD_Q, CONV_DIM), lambda b, c: (rowblk(b, c), 0)),
                  pl.BlockSpec((SSD_Q, LANES), lambda b, c: (rowblk(b, c), 0)), vec, vec, vec,
                  pl.BlockSpec((1, SSD_G, GROUP_W, SSD_N), lambda b, c: (rowblk(b, c), 0, 0, 0)),
                  pl.BlockSpec((SSD_Q, D_INNER), lambda b, c: (rowblk(b, c), 0))],
        out_specs=[pl.BlockSpec((SSD_Q, CONV_DIM), lambda b, c: (rowblk(b, c), 0)),
                   pl.BlockSpec((SSD_Q, LANES), lambda b, c: (rowblk(b, c), 0)), vec, vec, vec],
        out_shape=[jax.ShapeDtypeStruct((rows, CONV_DIM), F32), jax.ShapeDtypeStruct((rows, LANES), BF16),
                   jax.ShapeDtypeStruct((1, LANES), F32), jax.ShapeDtypeStruct((1, LANES), F32),
                   jax.ShapeDtypeStruct((1, LANES), F32)],
        scratch_shapes=[pltpu.VMEM((SSD_G, GROUP_W, SSD_N), F32), pltpu.VMEM((SSD_Q, LANES), F32),
                        pltpu.VMEM((LANES, SSD_Q), F32), pltpu.VMEM((SSD_Q, LANES), F32)],
        compiler_params=_params(("arbitrary", "arbitrary")),
    )(xact, dtraw, dt_bias, a_log, d_skip, sin, dy)


QKV_W = 3 * ATT_OUT
PAIR_W = 2 * ATT_HD
HEAD_PAIRS = ATT_H // 2
PREP_ROWS = 512


def _dilated(a, n_seq, seq, dil):
    return a.reshape(n_seq * (seq // dil), dil * a.shape[1])


def _head_rstd(x):
    e = _head_expand_matrix()
    xx = x * x
    return lax.rsqrt(sum(_dot_nt(p, e) for p in _split_bf16(xx, 2)) * (1.0 / ATT_HD) + EPS)


def _head_rms_bwd(x, g_t, dy):
    r = _expand_heads(_head_rstd(x))
    xh = x * r
    dyg = dy * g_t
    mean = _expand_heads(_reduce_heads(dyg * xh) * (1.0 / ATT_HD))
    return r * (dyg - xh * mean), jnp.sum(dy * xh, axis=0, keepdims=True)


def qk_prep(qkv, gq_t, gk_t, rows, name):
    tb = min(PREP_ROWS, rows)

    def body(x_ref, gq_ref, gk_ref, o_ref):
        q = x_ref[:, pl.ds(0, ATT_OUT)]
        k = x_ref[:, pl.ds(ATT_OUT, ATT_OUT)]
        o_ref[:, pl.ds(0, ATT_OUT)] = (q * _expand_heads(_head_rstd(q)) * (gq_ref[...] * ATT_SCALE)).astype(o_ref.dtype)
        o_ref[:, pl.ds(ATT_OUT, ATT_OUT)] = (k * _expand_heads(_head_rstd(k)) * gk_ref[...]).astype(o_ref.dtype)
        o_ref[:, pl.ds(2 * ATT_OUT, ATT_OUT)] = x_ref[:, pl.ds(2 * ATT_OUT, ATT_OUT)].astype(o_ref.dtype)

    gspec = pl.BlockSpec((1, ATT_OUT), lambda i: (0, 0))
    blk = pl.BlockSpec((tb, QKV_W), lambda i: (i, 0))
    return pl.pallas_call(
        body, name=name,
        grid=(rows // tb,),
        in_specs=[blk, gspec, gspec],
        out_specs=blk,
        out_shape=jax.ShapeDtypeStruct((rows, QKV_W), MXU),
        compiler_params=_params(("parallel",)),
    )(qkv, gq_t, gk_t)


def _lane_hi():
    return lax.broadcasted_iota(jnp.int32, (1, PAIR_W), 1) >= ATT_HD


def _band_mask2(first_valid, query_rows):
    i = lax.broadcasted_iota(jnp.int32, (ATT_BLK, 2 * ATT_BLK), 0)
    j = lax.broadcasted_iota(jnp.int32, (ATT_BLK, 2 * ATT_BLK), 1)
    left = j < ATT_BLK
    right = jnp.logical_not(left)
    if query_rows:
        return jnp.logical_or(jnp.logical_and(jnp.logical_and(left, i <= j), first_valid),
                              jnp.logical_and(right, i >= j - ATT_BLK))
    return jnp.logical_or(jnp.logical_and(left, j >= i),
                          jnp.logical_and(jnp.logical_and(right, j - ATT_BLK <= i), first_valid))


def _only_head(slab, hi):
    keep = _lane_hi() if hi else jnp.logical_not(_lane_hi())
    return jnp.where(keep, slab, jnp.zeros_like(slab))


def attn_fwd2(nq, n_seq, seq, dil, name):
    length = seq // dil
    nb = length // ATT_BLK

    def body(cur_ref, prev_ref, o_ref, lse_ref, s_scr, p_scr):
        n = pl.program_id(2)
        mask = _band_mask2(n > 0, True)
        for h in range(ATT_H):
            sl = pl.ds((h // 2) * PAIR_W, PAIR_W)
            ks = pl.ds(ATT_OUT + (h // 2) * PAIR_W, PAIR_W)
            kcat = jnp.concatenate([prev_ref[:, ks], cur_ref[:, ks]], axis=0)
            s_scr[h] = jnp.where(mask, _dot_nt(_only_head(cur_ref[:, sl], h % 2), kcat), NEG)
        s_all = s_scr[...]
        mx = jnp.max(s_all, axis=2, keepdims=True)
        p_all = jnp.exp(s_all - mx)
        den = jnp.sum(p_all, axis=2, keepdims=True)
        p_scr[...] = p_all.astype(p_scr.dtype)
        inv = 1.0 / den
        lse = mx + jnp.log(den)
        lse_blk = jnp.zeros((ATT_BLK, LANES), F32)
        for h in range(ATT_H):
            lse_blk = lse_blk + lse[h] * _onehot_row(h)
        lse_ref[...] = lse_blk
        for pr in range(HEAD_PAIRS):
            vs = pl.ds(2 * ATT_OUT + pr * PAIR_W, PAIR_W)
            vcat = jnp.concatenate([prev_ref[:, vs], cur_ref[:, vs]], axis=0)
            lo = _dot_nn(p_scr[2 * pr], vcat) * inv[2 * pr]
            hi = _dot_nn(p_scr[2 * pr + 1], vcat) * inv[2 * pr + 1]
            o_ref[:, pl.ds(pr * PAIR_W, PAIR_W)] = jnp.where(_lane_hi(), hi, lo)

    return pl.pallas_call(
        body, name=name,
        grid=(n_seq, dil, nb),
        in_specs=[pl.BlockSpec((ATT_BLK, QKV_W), lambda b, r, n: (b * nb + n, r)),
                  pl.BlockSpec((ATT_BLK, QKV_W), lambda b, r, n: (b * nb + jnp.maximum(n - 1, 0), r))],
        out_specs=[pl.BlockSpec((ATT_BLK, ATT_OUT), lambda b, r, n: (b * nb + n, r)),
                   pl.BlockSpec((ATT_BLK, LANES), lambda b, r, n: (b * nb + n, r))],
        out_shape=[jax.ShapeDtypeStruct((n_seq * length, dil * ATT_OUT), F32),
                   jax.ShapeDtypeStruct((n_seq * length, dil * LANES), F32)],
        scratch_shapes=[pltpu.VMEM((ATT_H, ATT_BLK, 2 * ATT_BLK), F32), pltpu.VMEM((ATT_H, ATT_BLK, 2 * ATT_BLK), MXU)],
        compiler_params=_params(("parallel", "parallel", "arbitrary")),
    )(nq, nq)


def attn_bwd2(nq, do, lse, wts, rsum, n_seq, seq, dil, name):
    length = seq // dil
    nb = length // ATT_BLK

    def body(prev_ref, cur_ref, nxt_ref, do_c, do_x, lse_c, lse_x, wt_c, wt_x, rs_c, rs_x, dn_ref):
        n = pl.program_id(2)
        mask_q = _band_mask2(n > 0, True)
        mask_k = _band_mask2(n < nb - 1, False)
        wc, wx = wt_c[...], wt_x[...]
        lse_t = jnp.concatenate([lse_c[...].T, lse_x[...].T], axis=1)
        dl_t = jnp.concatenate([(-wc * rs_c[...]).T, (-wx * rs_x[...]).T], axis=1)
        for pr in range(HEAD_PAIRS):
            sl = pl.ds(pr * PAIR_W, PAIR_W)
            ks = pl.ds(ATT_OUT + pr * PAIR_W, PAIR_W)
            vs = pl.ds(2 * ATT_OUT + pr * PAIR_W, PAIR_W)
            he, ho = pl.ds(2 * pr, 1), pl.ds(2 * pr + 1, 1)
            q_c, k_c, v_c = cur_ref[:, sl], cur_ref[:, ks], cur_ref[:, vs]
            qcat = jnp.concatenate([q_c, nxt_ref[:, sl]], axis=0)
            kcat = jnp.concatenate([prev_ref[:, ks], k_c], axis=0)
            vcat = jnp.concatenate([prev_ref[:, vs], v_c], axis=0)
            dog_c = do_c[:, sl] * jnp.where(_lane_hi(), wt_c[:, ho], wt_c[:, he])
            dog_x = do_x[:, sl] * jnp.where(_lane_hi(), wt_x[:, ho], wt_x[:, he])
            dog = jnp.concatenate([dog_c, dog_x], axis=0).astype(MXU)
            res = []
            for hi in (0, 1):
                h = 2 * pr + hi
                one = pl.ds(h, 1)
                dl_col = -wt_c[:, one] * rs_c[:, one]
                p_q = jnp.exp(jnp.where(mask_q, _dot_nt(_only_head(q_c, hi), kcat) - lse_c[:, one], NEG))
                ds_q = p_q * (_dot_nt(_only_head(dog[:ATT_BLK], hi), vcat) + dl_col)
                dq = _dot_nn(ds_q, kcat)
                p_t = jnp.exp(jnp.where(mask_k, _dot_nt(_only_head(k_c, hi), qcat) - lse_t[h:h + 1, :], NEG))
                ds_t = p_t * (_dot_nt(_only_head(v_c, hi), dog) + dl_t[h:h + 1, :])
                res.append((dq, _dot_nn(ds_t, qcat), _dot_nn(p_t, dog)))
            for t, dst in enumerate((sl, ks, vs)):
                dn_ref[:, dst] = jnp.where(_lane_hi(), res[1][t], res[0][t])

    def at(shift, width):
        if shift < 0:
            return pl.BlockSpec((ATT_BLK, width), lambda b, r, n: (b * nb + jnp.maximum(n - 1, 0), r))
        if shift > 0:
            return pl.BlockSpec((ATT_BLK, width), lambda b, r, n: (b * nb + jnp.minimum(n + 1, nb - 1), r))
        return pl.BlockSpec((ATT_BLK, width), lambda b, r, n: (b * nb + n, r))

    return pl.pallas_call(
        body, name=name,
        grid=(n_seq, dil, nb),
        in_specs=[at(-1, QKV_W), at(0, QKV_W), at(1, QKV_W), at(0, ATT_OUT), at(1, ATT_OUT),
                  at(0, LANES), at(1, LANES), at(0, LANES), at(1, LANES), at(0, LANES), at(1, LANES)],
        out_specs=at(0, QKV_W),
        out_shape=jax.ShapeDtypeStruct((n_seq * length, dil * QKV_W), F32),
        compiler_params=_params(("parallel", "parallel", "arbitrary")),
    )(nq, nq, nq, do, do, lse, lse, wts, wts, rsum, rsum)


def qk_post(qkv, dn, gq_t, gk_t, rows, name):
    tb = min(PREP_ROWS, rows)

    def body(x_ref, dn_ref, gq_ref, gk_ref, o_ref, dgq_ref, dgk_ref):
        i = pl.program_id(0)
        qs, ks, vs = pl.ds(0, ATT_OUT), pl.ds(ATT_OUT, ATT_OUT), pl.ds(2 * ATT_OUT, ATT_OUT)
        dq, dgq = _head_rms_bwd(x_ref[:, qs], gq_ref[...], dn_ref[:, qs] * ATT_SCALE)
        dk, dgk = _head_rms_bwd(x_ref[:, ks], gk_ref[...], dn_ref[:, ks])
        o_ref[:, qs] = dq.astype(o_ref.dtype)
        o_ref[:, ks] = dk.astype(o_ref.dtype)
        o_ref[:, vs] = dn_ref[:, vs].astype(o_ref.dtype)

        @pl.when(i == 0)
        def _():
            dgq_ref[...] = dgq
            dgk_ref[...] = dgk

        @pl.when(i != 0)
        def _():
            dgq_ref[...] += dgq
            dgk_ref[...] += dgk

    gspec = pl.BlockSpec((1, ATT_OUT), lambda i: (0, 0))
    blk = pl.BlockSpec((tb, QKV_W), lambda i: (i, 0))
    return pl.pallas_call(
        body, name=name,
        grid=(rows // tb,),
        in_specs=[blk, blk, gspec, gspec],
        out_specs=[blk, gspec, gspec],
        out_shape=[jax.ShapeDtypeStruct((rows, QKV_W), MXU), jax.ShapeDtypeStruct((1, ATT_OUT), F32),
                   jax.ShapeDtypeStruct((1, ATT_OUT), F32)],
        compiler_params=_params(("arbitrary",)),
    )(qkv, dn, gq_t, gk_t)


OTHER_CHIPS = (4, 2, 6)


def exchange(items, name):
    n = len(items)

    def body(*refs):
        in_refs, out_refs = refs[:n], refs[n:2 * n]
        send_sems, recv_sems, local_sems = refs[2 * n:]
        x, y, c = lax.axis_index("x"), lax.axis_index("y"), lax.axis_index("c")
        me = 4 * x + 2 * y + c

        def peer(k):
            px = 1 - x if k & 4 else x
            py = 1 - y if k & 2 else y
            pc = 1 - c if k & 1 else c
            return (px, py, pc), 4 * px + 2 * py + pc

        def remote(t, k):
            dev, pid = peer(k)
            src = in_refs[t] if items[t][1] == "gather" else in_refs[t].at[pid]
            return pltpu.make_async_remote_copy(
                src_ref=src, dst_ref=out_refs[t].at[me], send_sem=send_sems.at[t, k], recv_sem=recv_sems.at[t, k],
                device_id=dev, device_id_type=pl.DeviceIdType.MESH)

        def arrival(t, k):
            dev, pid = peer(k)
            src = in_refs[t] if items[t][1] == "gather" else in_refs[t].at[pid]
            return pltpu.make_async_remote_copy(
                src_ref=src, dst_ref=out_refs[t].at[pid], send_sem=send_sems.at[t, k], recv_sem=recv_sems.at[t, k],
                device_id=dev, device_id_type=pl.DeviceIdType.MESH)

        def own(t):
            src = in_refs[t] if items[t][1] == "gather" else in_refs[t].at[me]
            return pltpu.make_async_copy(src, out_refs[t].at[me], local_sems.at[t])

        def forward(t, k, from_sibling):
            sib, _ = peer(1)
            _, pid = peer(k + 1 if from_sibling else k)
            slot = out_refs[t].at[pid]
            return pltpu.make_async_remote_copy(
                src_ref=slot, dst_ref=slot, send_sem=send_sems.at[t, k + 1], recv_sem=recv_sems.at[t, k + 1],
                device_id=sib, device_id_type=pl.DeviceIdType.MESH)

        def direct(t):
            return (1,) + OTHER_CHIPS if items[t][1] == "gather" else tuple(range(1, N_DEV))

        for t in range(n):
            own(t).start()
            for k in direct(t):
                remote(t, k).start()
        for t in range(n):
            if items[t][1] == "gather":
                for k in OTHER_CHIPS:
                    arrival(t, k).wait_recv()
                    forward(t, k, False).start()
        for t in range(n):
            if items[t][1] == "gather":
                arrival(t, 1).wait_recv()
                for k in OTHER_CHIPS:
                    forward(t, k, True).wait_recv()
            else:
                for k in direct(t):
                    arrival(t, k).wait_recv()
        for t in range(n):
            for k in direct(t):
                remote(t, k).wait_send()
            if items[t][1] == "gather":
                for k in OTHER_CHIPS:
                    forward(t, k, False).wait_send()
            own(t).wait()

    out_shape = []
    for arr, mode in items:
        shp = arr.shape if mode == "gather" else arr.shape[1:]
        out_shape.append(jax.ShapeDtypeStruct((N_DEV,) + tuple(shp), arr.dtype))
    anyspec = pl.BlockSpec(memory_space=pl.ANY)
    return pl.pallas_call(
        body, name=name,
        in_specs=[anyspec] * n,
        out_specs=[anyspec] * n,
        out_shape=out_shape,
        scratch_shapes=[pltpu.SemaphoreType.DMA((n, N_DEV)), pltpu.SemaphoreType.DMA((n, N_DEV)),
                        pltpu.SemaphoreType.DMA((n,))],
    )(*[a for a, _ in items])


def adamw(parts, w, m, v, name):
    r, c = w.shape
    rb = r if r <= 512 else (128 if c > 1024 else 256)
    assert r % rb == 0

    def body(p_ref, w_ref, m_ref, v_ref, g_out, d_out, m_out, v_out):
        g = p_ref[0].astype(F32)
        for i in range(1, N_DEV):
            g = g + p_ref[i].astype(F32)
        m_new = ADAM_B1 * m_ref[...] + (1.0 - ADAM_B1) * g
        v_new = ADAM_B2 * v_ref[...] + (1.0 - ADAM_B2) * (g * g)
        m_hat = m_new / (1.0 - ADAM_B1 ** ADAM_STEP)
        v_hat = v_new / (1.0 - ADAM_B2 ** ADAM_STEP)
        g_out[...] = g
        d_out[...] = -ADAM_LR * (m_hat / (jnp.sqrt(v_hat) + ADAM_EPS) + ADAM_WD * w_ref[...])
        m_out[...] = m_new
        v_out[...] = v_new

    blk = pl.BlockSpec((rb, c), lambda i: (i, 0))
    return pl.pallas_call(
        body, name=name,
        grid=(r // rb,),
        in_specs=[pl.BlockSpec((N_DEV, rb, c), lambda i: (0, i, 0)), blk, blk, blk],
        out_specs=[blk] * 4,
        out_shape=[jax.ShapeDtypeStruct((r, c), F32)] * 4,
        compiler_params=_params(("parallel",)),
    )(parts, w, m, v)


def _pad_lanes(vec, n=LANES):
    return jnp.pad(vec, ((0, 0), (0, n - vec.shape[1])))


def local_step(x, target, w):
    n_seq, seq, _ = x.shape
    rows = n_seq * seq
    x = x.reshape(rows, D_MODEL)
    target = target.reshape(rows, D_MODEL)
    mx = lambda a: a.astype(MXU)

    splits = [sum(IN_WIDTHS[:i]) for i in range(len(IN_WIDTHS) + 1)]
    w_in = w["w_in"]
    part = lambda i: w_in[:, splits[i]:splits[i + 1]]
    w_z, w_xbc, w_gs, w_ga = mx(part(0)), mx(part(1)), mx(part(6)), mx(part(7))
    w_dt = mx(_pad_lanes(part(2)))
    w_qkv = [mx(jnp.concatenate([part(3 + t)[:, g * ATT_OUT:(g + 1) * ATT_OUT] for t in range(3)], axis=1))
             for g in range(ATT_GROUPS)]
    w_sp, w_ap, w_o, w_d = mx(w["w_ssd_proj"]), mx(w["w_attn_proj"]), mx(w["w_out"]), mx(w["w_down"])
    w_ug, w_uv = mx(w["w_up"][:, :D_FF]), mx(w["w_up"][:, D_FF:])
    conv_w, conv_b = w["ssd_conv_w"], w["ssd_conv_b"]
    fconv_w, fconv_b = w["ffn_conv_w"], w["ffn_conv_b"]
    dt_bias, a_log, d_skip = _pad_lanes(w["dt_bias"]), _pad_lanes(w["a_log"]), _pad_lanes(w["d_skip"])
    g1, g2, gn, gq, gk = w["norm1_g"], w["norm2_g"], w["ssd_norm_g"], w["q_norm_g"], w["k_norm_g"]

    tb = min(512, seq)
    tbm = min(256, seq)
    cw = 1024
    rw = lambda fn, name, ncol, ins, params=(), outs=(), accs=(), tb_=tb: rowwise(
        fn, name, rows, seq, tb_, ncol, ins, params, outs, accs)

    (h,) = rw(lambda ctx, xv, g: _rms_fwd(xv, g), "rms1_fwd", 1, [(x, D_MODEL, 0, None)], [(g1, None, 0)],
              [(D_MODEL, D_MODEL, 0, MXU)])
    z = matmul(h, w_z, "mm_z")
    xbc = matmul(h, w_xbc, "mm_xbc")
    dtraw = matmul(h, w_dt, "mm_dt")
    qkv = [matmul(h, w_qkv[g], f"mm_qkv{g}") for g in range(ATT_GROUPS)]
    gs = matmul(h, w_gs, "mm_gs")
    ga = matmul(h, w_ga, "mm_ga")

    def conv_silu(ctx, xh, wv, bv):
        return _silu(bv + _conv_prev(xh[0], xh[1], wv, ctx.first, SSD_CONV))

    (xact,) = rw(conv_silu, "ssd_conv_fwd", CONV_DIM // cw, [(xbc, cw, 0, "prev")],
                 [(conv_w, cw, 0), (conv_b, cw, 0)], [(CONV_DIM, cw, 0, F32)])
    y, sin = ssd_fwd(xact, dtraw, dt_bias, a_log, d_skip, n_seq, seq)

    def gated_norm(ctx, yv, zv, g):
        yz = yv * _silu(zv)
        return jnp.concatenate([_rms_fwd(yz[:, i:i + NORM_GROUP], g[:, i:i + NORM_GROUP])
                                for i in range(0, cw, NORM_GROUP)], axis=1)

    (y_ssd,) = rw(gated_norm, "ssd_post_fwd", D_INNER // cw, [(y, cw, 0, None), (z, cw, 0, None)], [(gn, cw, 0)],
                  [(D_INNER, cw, 0, MXU)])

    gq_t, gk_t = jnp.tile(gq, (1, ATT_H)), jnp.tile(gk, (1, ATT_H))
    dilated = lambda a, g: _dilated(a, n_seq, seq, ATT_DILATIONS[g])
    nq = [dilated(qk_prep(qkv[g], gq_t, gk_t, rows, f"qk_prep{g}"), g) for g in range(ATT_GROUPS)]
    att = [attn_fwd2(nq[g], n_seq, seq, ATT_DILATIONS[g], f"attn_fwd{g}") for g in range(ATT_GROUPS)]

    def combine(ctx, o0, o1, o2, l0, l1, l2):
        mxl = jnp.maximum(jnp.maximum(l0, l1), l2)
        e = [jnp.exp(l - mxl) for l in (l0, l1, l2)]
        inv = 1.0 / (e[0] + e[1] + e[2])
        ws = [ei * inv for ei in e]
        out = sum(_expand_heads(wi) * oi for wi, oi in zip(ws, (o0, o1, o2)))
        return (out, *ws)

    y_attn, wt0, wt1, wt2 = rw(
        combine, "attn_combine", 1,
        [(att[g][0].reshape(rows, ATT_OUT), ATT_OUT, 0, None) for g in range(3)]
        + [(att[g][1].reshape(rows, LANES), LANES, 0, None) for g in range(3)], [],
        [(ATT_OUT, ATT_OUT, 0, F32)] + [(LANES, LANES, 0, F32)] * 3)
    wts = (wt0, wt1, wt2)

    ps = matmul(y_ssd, w_sp, "mm_ssd_proj")
    pa = matmul(y_attn, w_ap, "mm_attn_proj")
    (merged,) = rw(lambda ctx, a, b, c, d: _sigmoid(c) * a + _sigmoid(d) * b, "merge_fwd", D_MODEL // cw,
                   [(ps, cw, 0, None), (pa, cw, 0, None), (gs, cw, 0, None), (ga, cw, 0, None)], [],
                   [(D_MODEL, cw, 0, MXU)])
    x1 = matmul(merged, w_o, "mm_out", add=x)
    (h2,) = rw(lambda ctx, xv, g: _rms_fwd(xv, g), "rms2_fwd", 1, [(x1, D_MODEL, 0, None)], [(g2, None, 0)],
               [(D_MODEL, D_MODEL, 0, MXU)])
    up_g = matmul(h2, w_ug, "mm_up_g")
    up_v = matmul(h2, w_uv, "mm_up_v")
    fw = D_FF // 2
    nfc = D_FF // fw

    def mlp_act(ctx, ug, uv, wg, wv, bg, bv):
        cg = bg + _conv_prev(ug[0], ug[1], wg, ctx.first, FFN_CONV)
        cv = bv + _conv_prev(uv[0], uv[1], wv, ctx.first, FFN_CONV)
        return _silu(cg) * cv

    (act,) = rw(mlp_act, "mlp_act_fwd", nfc, [(up_g, fw, 0, "prev"), (up_v, fw, 0, "prev")],
                [(fconv_w, fw, 0), (fconv_w, fw, nfc), (fconv_b, fw, 0), (fconv_b, fw, nfc)], [(D_FF, fw, 0, MXU)],
                tb_=tbm)
    x2 = matmul(act, w_d, "mm_down", add=x1)

    def loss_fn(ctx, xv, tv):
        d = xv - tv
        return d * (1.0 / D_MODEL), jnp.sum(d * d, axis=0, keepdims=True)

    dx2, sq = rw(loss_fn, "loss", 1, [(x2, D_MODEL, 0, None), (target, D_MODEL, 0, None)], [],
                 [(D_MODEL, D_MODEL, 0, F32)], [(1, D_MODEL)])

    grads = {}
    dact = matmul(dx2, w_d, "mm_d_act", tb=True)
    grads["w_down"] = matmul(act, dx2, "mm_dw_down", ta=True)

    def mlp_bwd(ctx, da, ug, uv, wg, wv, bg, bv):
        cg, cg_n = _conv_pre(ug, wg, bg, ctx.first, FFN_CONV)
        cv, cv_n = _conv_pre(uv, wv, bv, ctx.first, FFN_CONV)
        da_c, da_n = da
        dup_g_, dwg, dbg = _conv_bwd(da_c * cv * _silu_grad(cg), da_n * cv_n * _silu_grad(cg_n), ug, wg, ctx, FFN_CONV)
        dup_v_, dwv, dbv = _conv_bwd(da_c * _silu(cg), da_n * _silu(cg_n), uv, wv, ctx, FFN_CONV)
        return dup_g_, dup_v_, dwg, dbg, dwv, dbv

    dup_g, dup_v, dfw_g, dfb_g, dfw_v, dfb_v = rw(
        mlp_bwd, "mlp_bwd", nfc, [(dact, fw, 0, "next"), (up_g, fw, 0, "both"), (up_v, fw, 0, "both")],
        [(fconv_w, fw, 0), (fconv_w, fw, nfc), (fconv_b, fw, 0), (fconv_b, fw, nfc)],
        [(D_FF, fw, 0, MXU), (D_FF, fw, 0, MXU)], [(FFN_CONV, fw), (1, fw), (FFN_CONV, fw), (1, fw)], tb_=tbm)
    grads["ffn_conv_w"] = jnp.concatenate([dfw_g, dfw_v], axis=1)
    grads["ffn_conv_b"] = jnp.concatenate([dfb_g, dfb_v], axis=1)
    dh2 = matmul(dup_g, w_ug, "mm_dh2_g", tb=True)
    dh2 = matmul(dup_v, w_uv, "mm_dh2_v", tb=True, add=dh2)
    grads["w_up"] = jnp.concatenate([matmul(h2, dup_g, "mm_dw_up_g", ta=True),
                                     matmul(h2, dup_v, "mm_dw_up_v", ta=True)], axis=1)

    def rms_bwd_fn(ctx, xv, dh_, dres, g):
        dxv, dg = _rms_bwd(xv, g, dh_)
        return dres + dxv, dg

    dx1, grads["norm2_g"] = rw(rms_bwd_fn, "rms2_bwd", 1,
                               [(x1, D_MODEL, 0, None), (dh2, D_MODEL, 0, None), (dx2, D_MODEL, 0, None)],
                               [(g2, None, 0)], [(D_MODEL, D_MODEL, 0, F32)], [(1, D_MODEL)])

    dmerged = matmul(dx1, w_o, "mm_d_merged", tb=True)
    grads["w_out"] = matmul(merged, dx1, "mm_dw_out", ta=True)

    def merge_bwd(ctx, dm, a, b, c, d):
        sc, sd = _sigmoid(c), _sigmoid(d)
        return dm * sc, dm * sd, dm * a * sc * (1.0 - sc), dm * b * sd * (1.0 - sd)

    dps, dpa, dgs, dga = rw(merge_bwd, "merge_bwd", D_MODEL // cw,
                            [(dmerged, cw, 0, None), (ps, cw, 0, None), (pa, cw, 0, None), (gs, cw, 0, None),
                             (ga, cw, 0, None)], [], [(D_MODEL, cw, 0, MXU)] * 4)
    dy_ssd = matmul(dps, w_sp, "mm_d_y_ssd", tb=True)
    grads["w_ssd_proj"] = matmul(y_ssd, dps, "mm_dw_ssd_proj", ta=True)
    dy_attn = matmul(dpa, w_ap, "mm_d_y_attn", tb=True)
    grads["w_attn_proj"] = matmul(y_attn, dpa, "mm_dw_attn_proj", ta=True)

    (rsum,) = rw(lambda ctx, a, b: _reduce_heads(a * b), "attn_rsum", 1,
                 [(dy_attn, ATT_OUT, 0, None), (y_attn, ATT_OUT, 0, None)], [], [(LANES, LANES, 0, F32)])
    dqkv, dgq, dgk = [], 0.0, 0.0
    for g in range(ATT_GROUPS):
        dn = attn_bwd2(nq[g], dilated(dy_attn, g), att[g][1], dilated(wts[g], g), dilated(rsum, g), n_seq, seq,
                       ATT_DILATIONS[g], f"attn_bwd{g}")
        d_, a_, b_ = qk_post(qkv[g], dn.reshape(rows, QKV_W), gq_t, gk_t, rows, f"qk_post{g}")
        dqkv.append(d_)
        dgq, dgk = dgq + a_, dgk + b_
    per_head = lambda v: jnp.sum(v.reshape(ATT_H, ATT_HD), axis=0, keepdims=True)
    grads["q_norm_g"], grads["k_norm_g"] = per_head(dgq), per_head(dgk)

    def gated_norm_bwd(ctx, dyn, yv, zv, g):
        sz = _silu(zv)
        yz = yv * sz
        dyz, dgs_ = [], []
        for i in range(0, cw, NORM_GROUP):
            a, b = _rms_bwd(yz[:, i:i + NORM_GROUP], g[:, i:i + NORM_GROUP], dyn[:, i:i + NORM_GROUP])
            dyz.append(a)
            dgs_.append(b)
        dyz = jnp.concatenate(dyz, axis=1)
        return dyz * sz, dyz * yv * _silu_grad(zv), jnp.concatenate(dgs_, axis=1)

    dy, dz, grads["ssd_norm_g"] = rw(gated_norm_bwd, "ssd_post_bwd", D_INNER // cw,
                                     [(dy_ssd, cw, 0, None), (y, cw, 0, None), (z, cw, 0, None)], [(gn, cw, 0)],
                                     [(D_INNER, cw, 0, F32), (D_INNER, cw, 0, MXU)], [(1, cw)])
    dxact, ddt, dbias, dalog, ddskip = ssd_bwd(xact, dtraw, dt_bias, a_log, d_skip, sin, dy, n_seq, seq)
    grads["dt_bias"], grads["a_log"], grads["d_skip"] = dbias[:, :SSD_H], dalog[:, :SSD_H], ddskip[:, :SSD_H]

    def conv_silu_bwd(ctx, dxa, xin, wv, bv):
        pre, pre_n = _conv_pre(xin, wv, bv, ctx.first, SSD_CONV)
        return _conv_bwd(dxa[0] * _silu_grad(pre), dxa[1] * _silu_grad(pre_n), xin, wv, ctx, SSD_CONV)

    dxbc, grads["ssd_conv_w"], grads["ssd_conv_b"] = rw(
        conv_silu_bwd, "ssd_conv_bwd", CONV_DIM // cw, [(dxact, cw, 0, "next"), (xbc, cw, 0, "both")],
        [(conv_w, cw, 0), (conv_b, cw, 0)], [(CONV_DIM, cw, 0, MXU)], [(SSD_CONV, cw), (1, cw)])

    pieces = [(dz, w_z, "z"), (dxbc, w_xbc, "xbc"), (ddt, w_dt, "dt"), (dgs, w_gs, "gs"), (dga, w_ga, "ga")]
    pieces += [(dqkv[g], w_qkv[g], f"qkv{g}") for g in range(ATT_GROUPS)]
    dh, dws = None, {}
    for dpart, wpart, tag in pieces:
        dh = matmul(dpart, wpart, f"mm_dh_{tag}", tb=True, add=dh)
        dws[tag] = matmul(h, dpart, f"mm_dw_{tag}", ta=True)
    dq_parts = [[dws[f"qkv{g}"][:, t * ATT_OUT:(t + 1) * ATT_OUT] for g in range(ATT_GROUPS)] for t in range(3)]
    grads["w_in"] = jnp.concatenate(
        [dws["z"], dws["xbc"], dws["dt"][:, :SSD_H]] + [p for t in range(3) for p in dq_parts[t]] + [dws["gs"], dws["ga"]],
        axis=1)
    grad_x, grads["norm1_g"] = rw(rms_bwd_fn, "rms1_bwd", 1,
                                  [(x, D_MODEL, 0, None), (dh, D_MODEL, 0, None), (dx1, D_MODEL, 0, None)],
                                  [(g1, None, 0)], [(D_MODEL, D_MODEL, 0, F32)], [(1, D_MODEL)])
    return sq, grad_x.reshape(n_seq, seq, D_MODEL), grads


COL_SHARDED = ("w_in", "ssd_conv_w", "w_attn_proj", "w_up", "ffn_conv_w")
ROW_SHARDED = ("w_ssd_proj", "w_out", "w_down")
MATRICES = ("w_in", "w_attn_proj", "w_up", "w_ssd_proj", "w_out", "w_down")
REPLICATED = ("norm1_g", "ssd_conv_b", "dt_bias", "a_log", "d_skip", "ssd_norm_g", "q_norm_g", "k_norm_g",
              "norm2_g", "ffn_conv_b")
WEIGHTS = ("norm1_g", "w_in", "ssd_conv_w", "ssd_conv_b", "dt_bias", "a_log", "d_skip", "ssd_norm_g", "w_ssd_proj",
           "q_norm_g", "k_norm_g", "w_attn_proj", "w_out", "norm2_g", "w_up", "ffn_conv_w", "ffn_conv_b", "w_down")
PACK_ROWS, PACK_COLS = 8, 2048


def _pack(vals):
    flat = jnp.concatenate([vals[n].reshape(-1) for n in REPLICATED])
    return jnp.pad(flat, (0, PACK_ROWS * PACK_COLS - flat.shape[0])).reshape(PACK_ROWS, PACK_COLS)


def _unpack(packed, like):
    flat = packed.reshape(-1)
    out, pos = {}, 0
    for n in REPLICATED:
        size = like[n].size
        out[n] = flat[pos:pos + size].reshape(like[n].shape)
        pos += size
    return out


def step(x, target, wsh, msh, vsh):
    sharded = COL_SHARDED + ROW_SHARDED
    narrow = lambda n, a: a.astype(MXU) if n in MATRICES else a
    gathered = exchange([(narrow(n, wsh[n]), "gather") for n in sharded], "ag_weights")
    full = {n: wsh[n] for n in REPLICATED}
    for n, g in zip(sharded, gathered):
        if n in COL_SHARDED:
            full[n] = jnp.transpose(g, (1, 0, 2)).reshape(g.shape[1], N_DEV * g.shape[2])
        else:
            full[n] = g.reshape(N_DEV * g.shape[1], g.shape[2])

    sq, grad_x, grads = local_step(x, target, full)

    slabs = []
    for n in sharded:
        g = narrow(n, grads[n])
        if n in COL_SHARDED:
            slabs.append(jnp.transpose(g.reshape(g.shape[0], N_DEV, g.shape[1] // N_DEV), (1, 0, 2)))
        else:
            slabs.append(g.reshape(N_DEV, g.shape[0] // N_DEV, g.shape[1]))
    packed_g = _pack({n: grads[n] for n in REPLICATED})
    received = exchange([(s, "scatter") for s in slabs] + [(packed_g, "gather")], "rs_grads")

    out_g, out_d, out_m, out_v = {}, {}, {}, {}
    for n, parts in zip(sharded, received[:-1]):
        out_g[n], out_d[n], out_m[n], out_v[n] = adamw(parts, wsh[n], msh[n], vsh[n], f"adamw_{n}")
    pk = adamw(received[-1], _pack(wsh), _pack(msh), _pack(vsh), "adamw_small")
    for dst, packed in zip((out_g, out_d, out_m, out_v), pk):
        dst.update(_unpack(packed, wsh))
    loss = lax.psum(0.5 * jnp.sum(sq) / D_MODEL, ("x", "y", "c"))
    return loss, grad_x, out_g, out_d, out_m, out_v


def kernel(x, norm1_g, w_in, ssd_conv_w, ssd_conv_b, dt_bias, a_log, d_skip, ssd_norm_g, w_ssd_proj, q_norm_g, k_norm_g, w_attn_proj, w_out, norm2_g, w_up, ffn_conv_w, ffn_conv_b, w_down, loss_target, m_norm1_g, m_w_in, m_ssd_conv_w, m_ssd_conv_b, m_dt_bias, m_a_log, m_d_skip, m_ssd_norm_g, m_w_ssd_proj, m_q_norm_g, m_k_norm_g, m_w_attn_proj, m_w_out, m_norm2_g, m_w_up, m_ffn_conv_w, m_ffn_conv_b, m_w_down, v_norm1_g, v_w_in, v_ssd_conv_w, v_ssd_conv_b, v_dt_bias, v_a_log, v_d_skip, v_ssd_norm_g, v_w_ssd_proj, v_q_norm_g, v_k_norm_g, v_w_attn_proj, v_w_out, v_norm2_g, v_w_up, v_ffn_conv_w, v_ffn_conv_b, v_w_down):
    ws = (norm1_g, w_in, ssd_conv_w, ssd_conv_b, dt_bias, a_log, d_skip, ssd_norm_g, w_ssd_proj, q_norm_g, k_norm_g,
          w_attn_proj, w_out, norm2_g, w_up, ffn_conv_w, ffn_conv_b, w_down)
    ms = (m_norm1_g, m_w_in, m_ssd_conv_w, m_ssd_conv_b, m_dt_bias, m_a_log, m_d_skip, m_ssd_norm_g, m_w_ssd_proj,
          m_q_norm_g, m_k_norm_g, m_w_attn_proj, m_w_out, m_norm2_g, m_w_up, m_ffn_conv_w, m_ffn_conv_b, m_w_down)
    vs = (v_norm1_g, v_w_in, v_ssd_conv_w, v_ssd_conv_b, v_dt_bias, v_a_log, v_d_skip, v_ssd_norm_g, v_w_ssd_proj,
          v_q_norm_g, v_k_norm_g, v_w_attn_proj, v_w_out, v_norm2_g, v_w_up, v_ffn_conv_w, v_ffn_conv_b, v_w_down)
    strip = lambda a: a[0] if a.ndim == 3 else a
    wsh = {n: strip(a) for n, a in zip(WEIGHTS, ws)}
    msh = {n: strip(a) for n, a in zip(WEIGHTS, ms)}
    vsh = {n: strip(a) for n, a in zip(WEIGHTS, vs)}
    loss, grad_x, g, d, m, v = step(x, loss_target, wsh, msh, vsh)
    lead = lambda dct: [dct[n][None] if a.ndim == 3 else dct[n] for n, a in zip(WEIGHTS, ws)]
    return (loss, grad_x, *lead(g), *lead(d), *lead(m), *lead(v))
```

```python
import jax
import jax.numpy as jnp
from jax import lax
from jax.experimental import pallas as pl
from jax.experimental.pallas import tpu as pltpu

F32 = jnp.float32
BF16 = jnp.bfloat16
MXU = jnp.bfloat16
HIGHEST = lax.Precision.HIGHEST
VMEM_LIMIT_BYTES = 48 * 1024 * 1024
SUBLANES = 8
LANES = 128
N_DEV = 8

D_MODEL = 1024
D_INNER = 2048
SSD_P = 64
SSD_H = 32
SSD_G = 8
SSD_K = SSD_H // SSD_G
SSD_N = 128
SSD_Q = 128
SSD_CONV = 4
CONV_DIM = D_INNER + 2 * SSD_G * SSD_N
NORM_GROUP = D_INNER // SSD_G
ATT_GROUPS = 3
ATT_H = 8
ATT_HD = 64
ATT_BLK = 128
ATT_OUT = ATT_H * ATT_HD
ATT_DILATIONS = (1, 4, 16)
ATT_SCALE = ATT_HD ** -0.5
D_FF = 2816
FFN_CONV = 3
EPS = 1e-6
NEG = -1e30
IN_WIDTHS = (D_INNER, CONV_DIM, SSD_H, 3 * ATT_OUT, 3 * ATT_OUT, 3 * ATT_OUT, D_MODEL, D_MODEL)

ADAM_LR = 0.001
ADAM_B1 = 0.9
ADAM_B2 = 0.999
ADAM_EPS = 1e-08
ADAM_WD = 0.01
ADAM_STEP = 10


def _mm(a, b, dims):
    return lax.dot_general(a.astype(MXU), b.astype(MXU), (dims, ((), ())), preferred_element_type=F32)


def _dot_nn(a, b):
    return _mm(a, b, ((1,), (0,)))


def _dot_nt(a, b):
    return _mm(a, b, ((1,), (1,)))


def _dot_tn(a, b):
    return _mm(a, b, ((0,), (0,)))


def _dot_f32(a, b):
    return lax.dot_general(a, b, (((1,), (0,)), ((), ())), precision=HIGHEST, preferred_element_type=F32)


def _sigmoid(x):
    return 1.0 / (1.0 + jnp.exp(-x))


def _silu(x):
    return x * _sigmoid(x)


def _silu_grad(x):
    s = _sigmoid(x)
    return s * (1.0 + x * (1.0 - s))


def _softplus(x):
    return jnp.maximum(x, 0.0) + jnp.log(1.0 + jnp.exp(-jnp.abs(x)))


def _rms_fwd(x, g):
    r = lax.rsqrt(jnp.mean(x * x, axis=-1, keepdims=True) + EPS)
    return x * r * g


def _rms_bwd(x, g, dy):
    r = lax.rsqrt(jnp.mean(x * x, axis=-1, keepdims=True) + EPS)
    xh = x * r
    dyg = dy * g
    dx = r * (dyg - xh * jnp.mean(dyg * xh, axis=-1, keepdims=True))
    return dx, jnp.sum(dy * xh, axis=0, keepdims=True)


def _onehot_row(h, n=LANES):
    return (lax.broadcasted_iota(jnp.int32, (1, n), 1) == h).astype(F32)


def _onehot_col(h, n=LANES):
    return (lax.broadcasted_iota(jnp.int32, (n, 1), 0) == h).astype(F32)


def _head_expand_matrix():
    r = lax.broadcasted_iota(jnp.int32, (LANES, ATT_OUT), 0)
    c = lax.broadcasted_iota(jnp.int32, (LANES, ATT_OUT), 1)
    return (c // ATT_HD == r).astype(F32)


def _split_bf16(x, parts):
    out = []
    for _ in range(parts - 1):
        hi = x.astype(BF16).astype(F32)
        out.append(hi)
        x = x - hi
    out.append(x)
    return out


def _expand_heads(w):
    e = _head_expand_matrix()
    return sum(_dot_nn(p, e) for p in _split_bf16(w, 2))


def _reduce_heads(x):
    e = _head_expand_matrix()
    return sum(_dot_nt(p, e) for p in _split_bf16(x, 3))


def _shift_prev(cur, halo, s, first):
    if s == 0:
        return cur
    rolled = pltpu.roll(cur, s, 0)
    hr = jnp.where(first, 0.0, pltpu.roll(halo, s, 0))
    rows = lax.broadcasted_iota(jnp.int32, halo.shape, 0)
    head = jnp.where(rows < s, hr, rolled[:SUBLANES])
    if cur.shape[0] == SUBLANES:
        return head
    return jnp.concatenate([head, rolled[SUBLANES:]], axis=0)


def _shift_next(cur, halo, s, last):
    if s == 0:
        return cur
    tb = cur.shape[0]
    rolled = pltpu.roll(cur, tb - s, 0)
    hr = jnp.where(last, 0.0, pltpu.roll(halo, SUBLANES - s, 0))
    rows = lax.broadcasted_iota(jnp.int32, halo.shape, 0)
    tail = jnp.where(rows >= SUBLANES - s, hr, rolled[tb - SUBLANES:])
    return jnp.concatenate([rolled[:tb - SUBLANES], tail], axis=0)


def _conv_prev(x, halo, w, first, taps):
    acc = None
    for i in range(taps):
        term = w[i:i + 1, :] * _shift_prev(x, halo, taps - 1 - i, first)
        acc = term if acc is None else acc + term
    return acc


def _conv_pre(x, w, b, first, taps):
    cur, prev8, next8 = x
    tail = cur[cur.shape[0] - SUBLANES:]
    return b + _conv_prev(cur, prev8, w, first, taps), b + _conv_prev(next8, tail, w, False, taps)


def _conv_bwd(dpre, dpre_next8, x, w, ctx, taps):
    cur, prev8, _ = x
    dx, dws = None, []
    for i in range(taps):
        term = w[i:i + 1, :] * _shift_next(dpre, dpre_next8, taps - 1 - i, ctx.last)
        dx = term if dx is None else dx + term
        dws.append(jnp.sum(dpre * _shift_prev(cur, prev8, taps - 1 - i, ctx.first), axis=0, keepdims=True))
    return dx, jnp.concatenate(dws, axis=0), jnp.sum(dpre, axis=0, keepdims=True)


def _params(sem):
    return pltpu.CompilerParams(dimension_semantics=sem, vmem_limit_bytes=VMEM_LIMIT_BYTES)


def _pick(dim, target):
    if dim <= target:
        return dim
    best = None
    for t in range(LANES, target + 1, LANES):
        if dim % t == 0:
            best = t
    assert best is not None, (dim, target)
    return best


MATMUL_VMEM_BUDGET = 34 * 1024 * 1024


V7X_MXU_FLOPS = 996e12
V7X_HBM_BYTES_PER_S = 3.4e12
GRID_STEP_S = 0.35e-6


def _tile_sizes(dim, cap):
    return [t for t in range(LANES, min(dim, cap) + 1, LANES) if dim % t == 0] or [dim]


def _matmul_tiles(m, n, k, a_bytes, b_bytes, add_bytes, out_bytes):
    best = None
    for tk in _tile_sizes(k, 8192):
        nk = k // tk
        for tn in _tile_sizes(n, 2048):
            for tm in _tile_sizes(m, 2048):
                io = tm * tk * a_bytes + tk * tn * b_bytes
                ends = tm * tn * (add_bytes + out_bytes)
                need = 2 * (io + ends) + tm * tn * 4 * (2 if nk > 1 else 1)
                if need > MATMUL_VMEM_BUDGET:
                    continue
                step = max(2.0 * tm * tn * tk / V7X_MXU_FLOPS, (io + ends / nk) / V7X_HBM_BYTES_PER_S)
                if nk > 1:
                    step += tm * tn * 8 / V7X_HBM_BYTES_PER_S
                cost = (m // tm) * (n // tn) * nk * (step + GRID_STEP_S)
                if best is None or cost < best[0]:
                    best = (cost, tm, tn, tk)
    if best is None:
        raise ValueError((m, n, k))
    return best[1:]


def matmul(a, b, name, ta=False, tb=False, add=None, out_dtype=F32):
    assert not (ta and tb)
    m, k = (a.shape[1], a.shape[0]) if ta else a.shape
    n = b.shape[0] if tb else b.shape[1]
    assert (b.shape[1] if tb else b.shape[0]) == k
    tm, tn, tk = _matmul_tiles(m, n, k, a.dtype.itemsize, b.dtype.itemsize,
                               0 if add is None else add.dtype.itemsize, jnp.dtype(out_dtype).itemsize)
    nk = k // tk
    dims = ((0,), (0,)) if ta else (((1,), (1,)) if tb else ((1,), (0,)))

    def body(*refs):
        if add is None:
            a_ref, b_ref, o_ref = refs[:3]
        else:
            a_ref, b_ref, add_ref, o_ref = refs[:4]

        def finish(r):
            if add is not None:
                r = r + add_ref[...].astype(F32)
            o_ref[...] = r.astype(out_dtype)

        if nk == 1:
            finish(_mm(a_ref[...], b_ref[...], dims))
            return
        acc = refs[-1]
        kk = pl.program_id(2)

        @pl.when(kk == 0)
        def _():
            acc[...] = jnp.zeros_like(acc)

        acc[...] += _mm(a_ref[...], b_ref[...], dims)

        @pl.when(kk == nk - 1)
        def _():
            finish(acc[...])

    a_spec = pl.BlockSpec((tk, tm), lambda i, j, kk: (kk, i)) if ta else pl.BlockSpec((tm, tk), lambda i, j, kk: (i, kk))
    b_spec = pl.BlockSpec((tn, tk), lambda i, j, kk: (j, kk)) if tb else pl.BlockSpec((tk, tn), lambda i, j, kk: (kk, j))
    in_specs = [a_spec, b_spec]
    args = [a, b]
    if add is not None:
        in_specs.append(pl.BlockSpec((tm, tn), lambda i, j, kk: (i, j)))
        args.append(add)
    return pl.pallas_call(
        body, name=name,
        grid=(m // tm, n // tn, nk),
        in_specs=in_specs,
        out_specs=pl.BlockSpec((tm, tn), lambda i, j, kk: (i, j)),
        out_shape=jax.ShapeDtypeStruct((m, n), out_dtype),
        scratch_shapes=[] if nk == 1 else [pltpu.VMEM((tm, tn), F32)],
        compiler_params=_params(("parallel", "parallel", "arbitrary")),
    )(*args)


class _Ctx:
    def __init__(self, first, last):
        self.first = first
        self.last = last


def rowwise(fn, name, rows, seq, tb, ncol, ins, params=(), outs=(), accs=()):
    assert rows % tb == 0 and seq % tb == 0 and tb % 16 == 0
    bps = seq // tb
    nrow = rows // tb
    r8 = tb // SUBLANES
    args, in_specs = [], []
    for arr, w, off, halo in ins:
        args.append(arr)
        in_specs.append(pl.BlockSpec((tb, w), lambda j, i, off=off: (i, off + j)))
        if halo in ("prev", "both"):
            args.append(arr)
            in_specs.append(pl.BlockSpec((SUBLANES, w), lambda j, i, off=off: (jnp.maximum(i * r8 - 1, 0), off + j)))
        if halo in ("next", "both"):
            args.append(arr)
            in_specs.append(pl.BlockSpec(
                (SUBLANES, w), lambda j, i, off=off: (jnp.minimum((i + 1) * r8, rows // SUBLANES - 1), off + j)))
    for arr, w, off in params:
        args.append(arr)
        if w is None:
            in_specs.append(pl.BlockSpec(arr.shape, lambda j, i: (0, 0)))
        else:
            in_specs.append(pl.BlockSpec((arr.shape[0], w), lambda j, i, off=off: (0, off + j)))
    out_shape, out_specs = [], []
    for total, w, off, dt in outs:
        out_shape.append(jax.ShapeDtypeStruct((rows, total), dt))
        out_specs.append(pl.BlockSpec((tb, w), lambda j, i, off=off: (i, off + j)))
    for r, w in accs:
        out_shape.append(jax.ShapeDtypeStruct((r, ncol * w), F32))
        out_specs.append(pl.BlockSpec((r, w), lambda j, i: (0, j)))
    n_out, n_acc = len(outs), len(accs)

    def body(*refs):
        i = pl.program_id(1)
        pos = 0
        vals = []
        for _, _, _, halo in ins:
            cur = refs[pos][...]
            pos += 1
            if halo is None:
                vals.append(cur)
            elif halo == "both":
                vals.append((cur, refs[pos][...], refs[pos + 1][...]))
                pos += 2
            else:
                vals.append((cur, refs[pos][...]))
                pos += 1
        for _ in params:
            vals.append(refs[pos][...])
            pos += 1
        ctx = _Ctx(i % bps == 0, i % bps == bps - 1)
        res = fn(ctx, *vals)
        if not isinstance(res, (tuple, list)):
            res = (res,)
        assert len(res) == n_out + n_acc
        for q in range(n_out):
            refs[pos + q][...] = res[q].astype(refs[pos + q].dtype)
        for q in range(n_acc):
            ref, val = refs[pos + n_out + q], res[n_out + q]

            @pl.when(i == 0)
            def _(ref=ref, val=val):
                ref[...] = val

            @pl.when(i != 0)
            def _(ref=ref, val=val):
                ref[...] += val

    res = pl.pallas_call(
        body, name=name,
        grid=(ncol, nrow),
        in_specs=in_specs,
        out_specs=out_specs,
        out_shape=out_shape,
        compiler_params=_params(("parallel", "arbitrary")),
    )(*args)
    return res


GROUP_W = SSD_K * SSD_P


def _tri(lower):
    r = lax.broadcasted_iota(jnp.int32, (SSD_Q, SSD_Q), 0)
    c = lax.broadcasted_iota(jnp.int32, (SSD_Q, SSD_Q), 1)
    return r >= c if lower else r <= c


def _group_masks():
    lane = lax.broadcasted_iota(jnp.int32, (1, GROUP_W), 1) // SSD_P
    row = lax.broadcasted_iota(jnp.int32, (GROUP_W, 1), 0) // SSD_P
    return [lane == k for k in range(SSD_K)], [row == k for k in range(SSD_K)]


def _per_head(masks, vals):
    out = jnp.where(masks[0], vals[0], 0.0)
    for m, v in zip(masks[1:], vals[1:]):
        out = jnp.where(m, v, out)
    return out


def _headsum(prod, g):
    j = lax.broadcasted_iota(jnp.int32, (GROUP_W, LANES), 0) // SSD_P
    lane = lax.broadcasted_iota(jnp.int32, (GROUP_W, LANES), 1)
    e = (lane == g * SSD_K + j).astype(F32)
    return sum(_dot_nn(p, e) for p in _split_bf16(prod, 2))


def ssd_fwd(xact, dtraw, dt_bias, a_log, d_skip, n_seq, seq):
    nc = seq // SSD_Q
    rows = n_seq * seq

    def body(xact_ref, dtraw_ref, bias_ref, alog_ref, dskip_ref, y_ref, sin_ref, state, cs_s, cst_s, dt_s):
        c = pl.program_id(1)

        @pl.when(c == 0)
        def _():
            state[...] = jnp.zeros_like(state)

        sin_ref[0] = state[...]
        dt = _softplus(dtraw_ref[...] + bias_ref[...])
        a = dt * (-jnp.exp(alog_ref[...]))
        cs = _dot_f32(_tri(True).astype(F32), a)
        cs_s[...] = cs
        cst_s[...] = cs.T
        dt_s[...] = dt
        causal = _tri(True)
        lane_masks, row_masks = _group_masks()
        for g in range(SSD_G):
            heads = [g * SSD_K + k for k in range(SSD_K)]
            bg = xact_ref[:, pl.ds(D_INNER + g * SSD_N, SSD_N)]
            cg = xact_ref[:, pl.ds(D_INNER + (SSD_G + g) * SSD_N, SSD_N)]
            xg = xact_ref[:, pl.ds(g * GROUP_W, GROUP_W)]
            cols = [cs_s[:, pl.ds(h, 1)] for h in heads]
            lasts = [cs_s[pl.ds(SSD_Q - 1, 1), pl.ds(h, 1)] for h in heads]
            xdg = xg * _per_head(lane_masks, [dt_s[:, pl.ds(h, 1)] for h in heads])
            sg = state[g]
            gm = _dot_nt(cg, bg)
            y = (_per_head(lane_masks, [jnp.exp(c_) for c_ in cols]) * _dot_nt(cg, sg)
                 + _per_head(lane_masks, [dskip_ref[:, pl.ds(h, 1)] for h in heads]) * xg)
            for k, h in enumerate(heads):
                decay = jnp.exp(jnp.where(causal, cols[k] - cst_s[pl.ds(h, 1), :], NEG))
                y = y + _dot_nn(gm * decay, jnp.where(lane_masks[k], xdg, 0.0))
            y_ref[:, pl.ds(g * GROUP_W, GROUP_W)] = y
            w = _per_head(lane_masks, [jnp.exp(l_ - c_) for l_, c_ in zip(lasts, cols)])
            state[g] = _per_head(row_masks, [jnp.exp(l_) for l_ in lasts]) * sg + _dot_tn(w * xdg, bg)

    vec = pl.BlockSpec((1, LANES), lambda b, c: (0, 0))
    return pl.pallas_call(
        body, name="ssd_fwd",
        grid=(n_seq, nc),
        in_specs=[pl.BlockSpec((SSD_Q, CONV_DIM), lambda b, c: (b * nc + c, 0)),
                  pl.BlockSpec((SSD_Q, LANES), lambda b, c: (b * nc + c, 0)), vec, vec, vec],
        out_specs=[pl.BlockSpec((SSD_Q, D_INNER), lambda b, c: (b * nc + c, 0)),
                   pl.BlockSpec((1, SSD_G, GROUP_W, SSD_N), lambda b, c: (b * nc + c, 0, 0, 0))],
        out_shape=[jax.ShapeDtypeStruct((rows, D_INNER), F32),
                   jax.ShapeDtypeStruct((n_seq * nc, SSD_G, GROUP_W, SSD_N), F32)],
        scratch_shapes=[pltpu.VMEM((SSD_G, GROUP_W, SSD_N), F32), pltpu.VMEM((SSD_Q, LANES), F32),
                        pltpu.VMEM((LANES, SSD_Q), F32), pltpu.VMEM((SSD_Q, LANES), F32)],
        compiler_params=_params(("arbitrary", "arbitrary")),
    )(xact, dtraw, dt_bias, a_log, d_skip)


def ssd_bwd(xact, dtraw, dt_bias, a_log, d_skip, sin, dy, n_seq, seq):
    nc = seq // SSD_Q
    rows = n_seq * seq

    def body(xact_ref, dtraw_ref, bias_ref, alog_ref, dskip_ref, sin_ref, dy_ref,
             dx_ref, ddt_ref, dbias_ref, dalog_ref, ddskip_ref, dstate, cs_s, cst_s, dt_s):
        b, c = pl.program_id(0), pl.program_id(1)

        @pl.when(c == 0)
        def _():
            dstate[...] = jnp.zeros_like(dstate)

        pre = dtraw_ref[...] + bias_ref[...]
        dt = _softplus(pre)
        a_neg = -jnp.exp(alog_ref[...])
        cs = _dot_f32(_tri(True).astype(F32), dt * a_neg)
        cs_s[...] = cs
        cst_s[...] = cs.T
        dt_s[...] = dt
        causal, anti = _tri(True), _tri(False)
        is_last_row = lax.broadcasted_iota(jnp.int32, (SSD_Q, 1), 0) == SSD_Q - 1
        lane_masks, row_masks = _group_masks()
        dcs_cf = jnp.zeros((SSD_Q, LANES), F32)
        dcs_rf = jnp.zeros((LANES, SSD_Q), F32)
        ddt_cf = jnp.zeros((SSD_Q, LANES), F32)
        dd_vec = jnp.zeros((1, LANES), F32)
        dlast_vec = jnp.zeros((1, LANES), F32)
        for g in range(SSD_G):
            heads = [g * SSD_K + k for k in range(SSD_K)]
            bg = xact_ref[:, pl.ds(D_INNER + g * SSD_N, SSD_N)]
            cg = xact_ref[:, pl.ds(D_INNER + (SSD_G + g) * SSD_N, SSD_N)]
            xg = xact_ref[:, pl.ds(g * GROUP_W, GROUP_W)]
            dyg = dy_ref[:, pl.ds(g * GROUP_W, GROUP_W)]
            cols = [cs_s[:, pl.ds(h, 1)] for h in heads]
            rws = [cst_s[pl.ds(h, 1), :] for h in heads]
            lasts = [cs_s[pl.ds(SSD_Q - 1, 1), pl.ds(h, 1)] for h in heads]
            e_lasts = [jnp.exp(l_) for l_ in lasts]
            dtg = _per_head(lane_masks, [dt_s[:, pl.ds(h, 1)] for h in heads])
            dskg = _per_head(lane_masks, [dskip_ref[:, pl.ds(h, 1)] for h in heads])
            e_col = _per_head(lane_masks, [jnp.exp(c_) for c_ in cols])
            w = _per_head(lane_masks, [jnp.exp(l_ - c_) for l_, c_ in zip(lasts, cols)])
            xdg = xg * dtg
            sg = sin_ref[0, g]
            dsn = dstate[g]
            gm = _dot_nt(cg, bg)
            gmt = _dot_nt(bg, cg)
            y_off = e_col * _dot_nt(cg, sg)
            d_cs = e_col * dyg
            dcg = _dot_nn(d_cs, sg)
            dsp = _dot_tn(d_cs, cg) + _per_head(row_masks, e_lasts) * dsn
            dbg = _dot_nn(w * xdg, dsn)
            dtt = _dot_nt(bg, dsn)
            dxd = w * dtt
            dw = _headsum(dtt * xdg * w, g)
            dcs_cf = dcs_cf + _headsum(dyg * y_off, g) - dw
            dlast_vec = dlast_vec + jnp.sum(dw, axis=0, keepdims=True)
            dsn_s = dsn * sg
            dgm = jnp.zeros((SSD_Q, SSD_Q), F32)
            for k, h in enumerate(heads):
                seg = cols[k] - rws[k]
                decay = jnp.exp(jnp.where(causal, seg, NEG))
                decay_t = jnp.exp(jnp.where(anti, -seg, NEG))
                dyk = jnp.where(lane_masks[k], dyg, 0.0)
                dm = _dot_nt(dyk, xdg)
                dxd = dxd + _dot_nn(gmt * decay_t, dyk)
                dseg = dm * gm * decay
                dgm = dgm + dm * decay
                oh_r = _onehot_row(h)
                dcs_cf = dcs_cf + jnp.sum(dseg, axis=1, keepdims=True) * oh_r
                dcs_rf = dcs_rf - _onehot_col(h) * jnp.sum(dseg, axis=0, keepdims=True)
                dlast_vec = dlast_vec + jnp.sum(jnp.where(row_masks[k], dsn_s, 0.0), keepdims=True) * e_lasts[k] * oh_r
            dx_ref[:, pl.ds(g * GROUP_W, GROUP_W)] = dxd * dtg + dskg * dyg
            ddt_cf = ddt_cf + _headsum(dxd * xg, g)
            dyx = jnp.broadcast_to(jnp.sum(dyg * xg, axis=0, keepdims=True), (SUBLANES, GROUP_W))
            dd_vec = dd_vec + _headsum(dyx, g)[0:1]
            dstate[g] = dsp
            dx_ref[:, pl.ds(D_INNER + g * SSD_N, SSD_N)] = dbg + _dot_tn(dgm, cg)
            dx_ref[:, pl.ds(D_INNER + (SSD_G + g) * SSD_N, SSD_N)] = dcg + _dot_nn(dgm, bg)
        dcs = dcs_cf + dcs_rf.T + jnp.where(is_last_row, dlast_vec, 0.0)
        da = _dot_f32(_tri(False).astype(F32), dcs)
        ddt = ddt_cf + da * a_neg
        ddtraw = ddt * _sigmoid(pre)
        ddt_ref[...] = ddtraw.astype(ddt_ref.dtype)
        dbias = jnp.sum(ddtraw, axis=0, keepdims=True)
        dalog = jnp.sum(da * dt, axis=0, keepdims=True) * a_neg
        first_step = jnp.logical_and(b == 0, c == 0)

        @pl.when(first_step)
        def _():
            dbias_ref[...] = dbias
            dalog_ref[...] = dalog
            ddskip_ref[...] = dd_vec

        @pl.when(jnp.logical_not(first_step))
        def _():
            dbias_ref[...] += dbias
            dalog_ref[...] += dalog
            ddskip_ref[...] += dd_vec

    def rowblk(b, c):
        return b * nc + (nc - 1 - c)

    vec = pl.BlockSpec((1, LANES), lambda b, c: (0, 0))
    return pl.pallas_call(
        body, name="ssd_bwd",
        grid=(n_seq, nc),
        in_specs=[pl.BlockSpec((SSD_Q, CONV_DIM), lambda b, c: (rowblk(b, c), 0)),
                  pl.BlockSpec((SSD_Q, LANES), lambda b, c: (rowblk(b, c), 0)), vec, vec, vec,
                  pl.BlockSpec((1, SSD_G, GROUP_W, SSD_N), lambda b, c: (rowblk(b, c), 0, 0, 0)),
                  pl.BlockSpec((SSD_Q, D_INNER), lambda b, c: (rowblk(b, c), 0))],
        out_specs=[pl.BlockSpec((SSD_Q, CONV_DIM), lambda b, c: (rowblk(b, c), 0)),
                   pl.BlockSpec((SSD_Q, LANES), lambda b, c: (rowblk(b, c), 0)), vec, vec, vec],
        out_shape=[jax.ShapeDtypeStruct((rows, CONV_DIM), F32), jax.ShapeDtypeStruct((rows, LANES), BF16),
                   jax.ShapeDtypeStruct((1, LANES), F32), jax.ShapeDtypeStruct((1, LANES), F32),
                   jax.ShapeDtypeStruct((1, LANES), F32)],
        scratch_shapes=[pltpu.VMEM((SSD_G, GROUP_W, SSD_N), F32), pltpu.VMEM((SSD_Q, LANES), F32),
                        pltpu.VMEM((LANES, SSD_Q), F32), pltpu.VMEM((SSD_Q, LANES), F32)],
        compiler_params=_params(("arbitrary", "arbitrary")),
    )(xact, dtraw, dt_bias, a_log, d_skip, sin, dy)


QKV_W = 3 * ATT_OUT
PAIR_W = 2 * ATT_HD
HEAD_PAIRS = ATT_H // 2
PREP_ROWS = 512


def _dilated(a, n_seq, seq, dil):
    return a.reshape(n_seq * (seq // dil), dil * a.shape[1])


def _head_sums(x, fn):
    lo = jnp.logical_not(lax.broadcasted_iota(jnp.int32, (1, 2 * ATT_HD), 1) >= ATT_HD)
    parts = []
    for p in range(ATT_H // 2):
        slab = x[:, p * 2 * ATT_HD:(p + 1) * 2 * ATT_HD]
        s_lo = fn(jnp.sum(jnp.where(lo, slab, 0.0), axis=1, keepdims=True))
        s_hi = fn(jnp.sum(jnp.where(lo, 0.0, slab), axis=1, keepdims=True))
        parts.append(jnp.where(lo, s_lo, s_hi))
    return jnp.concatenate(parts, axis=1)


def _head_rstd(x):
    return _head_sums(x * x, lambda s: lax.rsqrt(s * (1.0 / ATT_HD) + EPS))


def _head_rms_bwd(x, g_t, dy):
    r = _head_rstd(x)
    xh = x * r
    dyg = dy * g_t
    mean = _head_sums(dyg * xh, lambda s: s * (1.0 / ATT_HD))
    return r * (dyg - xh * mean), jnp.sum(dy * xh, axis=0, keepdims=True)


def qk_prep(qkv, gq_t, gk_t, rows, name):
    tb = min(PREP_ROWS, rows)

    def body(x_ref, gq_ref, gk_ref, o_ref):
        q = x_ref[:, pl.ds(0, ATT_OUT)]
        k = x_ref[:, pl.ds(ATT_OUT, ATT_OUT)]
        o_ref[:, pl.ds(0, ATT_OUT)] = (q * _head_rstd(q) * (gq_ref[...] * ATT_SCALE)).astype(o_ref.dtype)
        o_ref[:, pl.ds(ATT_OUT, ATT_OUT)] = (k * _head_rstd(k) * gk_ref[...]).astype(o_ref.dtype)
        o_ref[:, pl.ds(2 * ATT_OUT, ATT_OUT)] = x_ref[:, pl.ds(2 * ATT_OUT, ATT_OUT)].astype(o_ref.dtype)

    gspec = pl.BlockSpec((1, ATT_OUT), lambda i: (0, 0))
    blk = pl.BlockSpec((tb, QKV_W), lambda i: (i, 0))
    return pl.pallas_call(
        body, name=name,
        grid=(rows // tb,),
        in_specs=[blk, gspec, gspec],
        out_specs=blk,
        out_shape=jax.ShapeDtypeStruct((rows, QKV_W), MXU),
        compiler_params=_params(("parallel",)),
    )(qkv, gq_t, gk_t)


def _lane_hi():
    return lax.broadcasted_iota(jnp.int32, (1, PAIR_W), 1) >= ATT_HD


def _band_mask2(first_valid, query_rows):
    i = lax.broadcasted_iota(jnp.int32, (ATT_BLK, 2 * ATT_BLK), 0)
    j = lax.broadcasted_iota(jnp.int32, (ATT_BLK, 2 * ATT_BLK), 1)
    left = j < ATT_BLK
    right = jnp.logical_not(left)
    if query_rows:
        return jnp.logical_or(jnp.logical_and(jnp.logical_and(left, i <= j), first_valid),
                              jnp.logical_and(right, i >= j - ATT_BLK))
    return jnp.logical_or(jnp.logical_and(left, j >= i),
                          jnp.logical_and(jnp.logical_and(right, j - ATT_BLK <= i), first_valid))


def _only_head(slab, hi):
    keep = _lane_hi() if hi else jnp.logical_not(_lane_hi())
    return jnp.where(keep, slab, jnp.zeros_like(slab))


def attn_fwd2(nq, n_seq, seq, dil, name):
    length = seq // dil
    nb = length // ATT_BLK

    def body(cur_ref, prev_ref, o_ref, lse_ref, s_scr, p_scr):
        n = pl.program_id(2)
        mask = _band_mask2(n > 0, True)
        for h in range(ATT_H):
            sl = pl.ds((h // 2) * PAIR_W, PAIR_W)
            ks = pl.ds(ATT_OUT + (h // 2) * PAIR_W, PAIR_W)
            kcat = jnp.concatenate([prev_ref[:, ks], cur_ref[:, ks]], axis=0)
            s_scr[h] = jnp.where(mask, _dot_nt(_only_head(cur_ref[:, sl], h % 2), kcat), NEG)
        s_all = s_scr[...]
        mx = jnp.max(s_all, axis=2, keepdims=True)
        p_all = jnp.exp(s_all - mx)
        den = jnp.sum(p_all, axis=2, keepdims=True)
        p_scr[...] = p_all.astype(p_scr.dtype)
        inv = 1.0 / den
        lse = mx + jnp.log(den)
        lse_blk = jnp.zeros((ATT_BLK, LANES), F32)
        for h in range(ATT_H):
            lse_blk = lse_blk + lse[h] * _onehot_row(h)
        lse_ref[...] = lse_blk
        for pr in range(HEAD_PAIRS):
            vs = pl.ds(2 * ATT_OUT + pr * PAIR_W, PAIR_W)
            vcat = jnp.concatenate([prev_ref[:, vs], cur_ref[:, vs]], axis=0)
            lo = _dot_nn(p_scr[2 * pr], vcat) * inv[2 * pr]
            hi = _dot_nn(p_scr[2 * pr + 1], vcat) * inv[2 * pr + 1]
            o_ref[:, pl.ds(pr * PAIR_W, PAIR_W)] = jnp.where(_lane_hi(), hi, lo)

    return pl.pallas_call(
        body, name=name,
        grid=(n_seq, dil, nb),
        in_specs=[pl.BlockSpec((ATT_BLK, QKV_W), lambda b, r, n: (b * nb + n, r)),
                  pl.BlockSpec((ATT_BLK, QKV_W), lambda b, r, n: (b * nb + jnp.maximum(n - 1, 0), r))],
        out_specs=[pl.BlockSpec((ATT_BLK, ATT_OUT), lambda b, r, n: (b * nb + n, r)),
                   pl.BlockSpec((ATT_BLK, LANES), lambda b, r, n: (b * nb + n, r))],
        out_shape=[jax.ShapeDtypeStruct((n_seq * length, dil * ATT_OUT), F32),
                   jax.ShapeDtypeStruct((n_seq * length, dil * LANES), F32)],
        scratch_shapes=[pltpu.VMEM((ATT_H, ATT_BLK, 2 * ATT_BLK), F32), pltpu.VMEM((ATT_H, ATT_BLK, 2 * ATT_BLK), MXU)],
        compiler_params=_params(("parallel", "parallel", "arbitrary")),
    )(nq, nq)


def attn_bwd2(nq, do, lse, wts, rsum, n_seq, seq, dil, name):
    length = seq // dil
    nb = length // ATT_BLK

    def body(prev_ref, cur_ref, nxt_ref, do_c, do_x, lse_c, lse_x, wt_c, wt_x, rs_c, rs_x, dn_ref):
        n = pl.program_id(2)
        mask_q = _band_mask2(n > 0, True)
        mask_k = _band_mask2(n < nb - 1, False)
        wc, wx = wt_c[...], wt_x[...]
        lse_t = jnp.concatenate([lse_c[...].T, lse_x[...].T], axis=1)
        dl_t = jnp.concatenate([(-wc * rs_c[...]).T, (-wx * rs_x[...]).T], axis=1)
        for pr in range(HEAD_PAIRS):
            sl = pl.ds(pr * PAIR_W, PAIR_W)
            ks = pl.ds(ATT_OUT + pr * PAIR_W, PAIR_W)
            vs = pl.ds(2 * ATT_OUT + pr * PAIR_W, PAIR_W)
            he, ho = pl.ds(2 * pr, 1), pl.ds(2 * pr + 1, 1)
            q_c, k_c, v_c = cur_ref[:, sl], cur_ref[:, ks], cur_ref[:, vs]
            qcat = jnp.concatenate([q_c, nxt_ref[:, sl]], axis=0)
            kcat = jnp.concatenate([prev_ref[:, ks], k_c], axis=0)
            vcat = jnp.concatenate([prev_ref[:, vs], v_c], axis=0)
            dog_c = do_c[:, sl] * jnp.where(_lane_hi(), wt_c[:, ho], wt_c[:, he])
            dog_x = do_x[:, sl] * jnp.where(_lane_hi(), wt_x[:, ho], wt_x[:, he])
            dog = jnp.concatenate([dog_c, dog_x], axis=0).astype(MXU)
            res = []
            for hi in (0, 1):
                h = 2 * pr + hi
                one = pl.ds(h, 1)
                dl_col = -wt_c[:, one] * rs_c[:, one]
                p_q = jnp.exp(jnp.where(mask_q, _dot_nt(_only_head(q_c, hi), kcat) - lse_c[:, one], NEG))
                ds_q = p_q * (_dot_nt(_only_head(dog[:ATT_BLK], hi), vcat) + dl_col)
                dq = _dot_nn(ds_q, kcat)
                p_t = jnp.exp(jnp.where(mask_k, _dot_nt(_only_head(k_c, hi), qcat) - lse_t[h:h + 1, :], NEG))
                ds_t = p_t * (_dot_nt(_only_head(v_c, hi), dog) + dl_t[h:h + 1, :])
                res.append((dq, _dot_nn(ds_t, qcat), _dot_nn(p_t, dog)))
            for t, dst in enumerate((sl, ks, vs)):
                dn_ref[:, dst] = jnp.where(_lane_hi(), res[1][t], res[0][t])

    def at(shift, width):
        if shift < 0:
            return pl.BlockSpec((ATT_BLK, width), lambda b, r, n: (b * nb + jnp.maximum(n - 1, 0), r))
        if shift > 0:
            return pl.BlockSpec((ATT_BLK, width), lambda b, r, n: (b * nb + jnp.minimum(n + 1, nb - 1), r))
        return pl.BlockSpec((ATT_BLK, width), lambda b, r, n: (b * nb + n, r))

    return pl.pallas_call(
        body, name=name,
        grid=(n_seq, dil, nb),
        in_specs=[at(-1, QKV_W), at(0, QKV_W), at(1, QKV_W), at(0, ATT_OUT), at(1, ATT_OUT),
                  at(0, LANES), at(1, LANES), at(0, LANES), at(1, LANES), at(0, LANES), at(1, LANES)],
        out_specs=at(0, QKV_W),
        out_shape=jax.ShapeDtypeStruct((n_seq * length, dil * QKV_W), F32),
        compiler_params=_params(("parallel", "parallel", "arbitrary")),
    )(nq, nq, nq, do, do, lse, lse, wts, wts, rsum, rsum)


def qk_post(qkv, dn, gq_t, gk_t, rows, name):
    tb = min(PREP_ROWS, rows)

    def body(x_ref, dn_ref, gq_ref, gk_ref, o_ref, dgq_ref, dgk_ref):
        i = pl.program_id(0)
        qs, ks, vs = pl.ds(0, ATT_OUT), pl.ds(ATT_OUT, ATT_OUT), pl.ds(2 * ATT_OUT, ATT_OUT)
        dq, dgq = _head_rms_bwd(x_ref[:, qs], gq_ref[...], dn_ref[:, qs] * ATT_SCALE)
        dk, dgk = _head_rms_bwd(x_ref[:, ks], gk_ref[...], dn_ref[:, ks])
        o_ref[:, qs] = dq.astype(o_ref.dtype)
        o_ref[:, ks] = dk.astype(o_ref.dtype)
        o_ref[:, vs] = dn_ref[:, vs].astype(o_ref.dtype)

        @pl.when(i == 0)
        def _():
            dgq_ref[...] = dgq
            dgk_ref[...] = dgk

        @pl.when(i != 0)
        def _():
            dgq_ref[...] += dgq
            dgk_ref[...] += dgk

    gspec = pl.BlockSpec((1, ATT_OUT), lambda i: (0, 0))
    blk = pl.BlockSpec((tb, QKV_W), lambda i: (i, 0))
    return pl.pallas_call(
        body, name=name,
        grid=(rows // tb,),
        in_specs=[blk, blk, gspec, gspec],
        out_specs=[blk, gspec, gspec],
        out_shape=[jax.ShapeDtypeStruct((rows, QKV_W), MXU), jax.ShapeDtypeStruct((1, ATT_OUT), F32),
                   jax.ShapeDtypeStruct((1, ATT_OUT), F32)],
        compiler_params=_params(("arbitrary",)),
    )(qkv, dn, gq_t, gk_t)


OTHER_CHIPS = (4, 2, 6)


def exchange(items, name):
    n = len(items)

    def body(*refs):
        in_refs, out_refs = refs[:n], refs[n:2 * n]
        send_sems, recv_sems, local_sems = refs[2 * n:]
        x, y, c = lax.axis_index("x"), lax.axis_index("y"), lax.axis_index("c")
        me = 4 * x + 2 * y + c

        def peer(k):
            px = 1 - x if k & 4 else x
            py = 1 - y if k & 2 else y
            pc = 1 - c if k & 1 else c
            return (px, py, pc), 4 * px + 2 * py + pc

        def remote(t, k):
            dev, pid = peer(k)
            src = in_refs[t] if items[t][1] == "gather" else in_refs[t].at[pid]
            return pltpu.make_async_remote_copy(
                src_ref=src, dst_ref=out_refs[t].at[me], send_sem=send_sems.at[t, k], recv_sem=recv_sems.at[t, k],
                device_id=dev, device_id_type=pl.DeviceIdType.MESH)

        def arrival(t, k):
            dev, pid = peer(k)
            src = in_refs[t] if items[t][1] == "gather" else in_refs[t].at[pid]
            return pltpu.make_async_remote_copy(
                src_ref=src, dst_ref=out_refs[t].at[pid], send_sem=send_sems.at[t, k], recv_sem=recv_sems.at[t, k],
                device_id=dev, device_id_type=pl.DeviceIdType.MESH)

        def own(t):
            src = in_refs[t] if items[t][1] == "gather" else in_refs[t].at[me]
            return pltpu.make_async_copy(src, out_refs[t].at[me], local_sems.at[t])

        def forward(t, k, from_sibling):
            sib, _ = peer(1)
            _, pid = peer(k + 1 if from_sibling else k)
            slot = out_refs[t].at[pid]
            return pltpu.make_async_remote_copy(
                src_ref=slot, dst_ref=slot, send_sem=send_sems.at[t, k + 1], recv_sem=recv_sems.at[t, k + 1],
                device_id=sib, device_id_type=pl.DeviceIdType.MESH)

        def direct(t):
            return (1,) + OTHER_CHIPS if items[t][1] == "gather" else tuple(range(1, N_DEV))

        for t in range(n):
            own(t).start()
            for k in direct(t):
                remote(t, k).start()
        for t in range(n):
            if items[t][1] == "gather":
                for k in OTHER_CHIPS:
                    arrival(t, k).wait_recv()
                    forward(t, k, False).start()
        for t in range(n):
            if items[t][1] == "gather":
                arrival(t, 1).wait_recv()
                for k in OTHER_CHIPS:
                    forward(t, k, True).wait_recv()
            else:
                for k in direct(t):
                    arrival(t, k).wait_recv()
        for t in range(n):
            for k in direct(t):
                remote(t, k).wait_send()
            if items[t][1] == "gather":
                for k in OTHER_CHIPS:
                    forward(t, k, False).wait_send()
            own(t).wait()

    out_shape = []
    for arr, mode in items:
        shp = arr.shape if mode == "gather" else arr.shape[1:]
        out_shape.append(jax.ShapeDtypeStruct((N_DEV,) + tuple(shp), arr.dtype))
    anyspec = pl.BlockSpec(memory_space=pl.ANY)
    return pl.pallas_call(
        body, name=name,
        in_specs=[anyspec] * n,
        out_specs=[anyspec] * n,
        out_shape=out_shape,
        scratch_shapes=[pltpu.SemaphoreType.DMA((n, N_DEV)), pltpu.SemaphoreType.DMA((n, N_DEV)),
                        pltpu.SemaphoreType.DMA((n,))],
    )(*[a for a, _ in items])


N_CHIPS = N_DEV // 2


def sibling_swap(arrays, name):
    n = len(arrays)

    def body(*refs):
        in_refs, out_refs, send_sems, recv_sems = refs[:n], refs[n:2 * n], refs[2 * n], refs[2 * n + 1]
        x, y, c = lax.axis_index("x"), lax.axis_index("y"), lax.axis_index("c")

        def copy(t, q):
            return pltpu.make_async_remote_copy(
                src_ref=in_refs[t].at[2 * q + (1 - c)], dst_ref=out_refs[t].at[q],
                send_sem=send_sems.at[t, q], recv_sem=recv_sems.at[t, q],
                device_id=(x, y, 1 - c), device_id_type=pl.DeviceIdType.MESH)

        for t in range(n):
            for q in range(N_CHIPS):
                copy(t, q).start()
        for t in range(n):
            for q in range(N_CHIPS):
                copy(t, q).wait_recv()
        for t in range(n):
            for q in range(N_CHIPS):
                copy(t, q).wait_send()

    anyspec = pl.BlockSpec(memory_space=pl.ANY)
    return pl.pallas_call(
        body, name=name,
        in_specs=[anyspec] * n,
        out_specs=[anyspec] * n,
        out_shape=[jax.ShapeDtypeStruct((N_CHIPS,) + a.shape[1:], a.dtype) for a in arrays],
        scratch_shapes=[pltpu.SemaphoreType.DMA((n, N_CHIPS)), pltpu.SemaphoreType.DMA((n, N_CHIPS))],
    )(*arrays)


def chip_exchange(arrays, name):
    n = len(arrays)

    def body(*refs):
        in_refs, out_refs = refs[:n], refs[n:2 * n]
        send_sems, recv_sems, local_sems = refs[2 * n:]
        x, y, c = lax.axis_index("x"), lax.axis_index("y"), lax.axis_index("c")
        mine = 2 * x + y

        def peer(k):
            px = 1 - x if k & 4 else x
            py = 1 - y if k & 2 else y
            return (px, py, c), 2 * px + py

        def remote(t, k, arriving):
            dev, q = peer(k)
            return pltpu.make_async_remote_copy(
                src_ref=in_refs[t].at[q], dst_ref=out_refs[t].at[q if arriving else mine],
                send_sem=send_sems.at[t, k], recv_sem=recv_sems.at[t, k],
                device_id=dev, device_id_type=pl.DeviceIdType.MESH)

        def own(t):
            return pltpu.make_async_copy(in_refs[t].at[mine], out_refs[t].at[mine], local_sems.at[t])

        for t in range(n):
            own(t).start()
            for k in OTHER_CHIPS:
                remote(t, k, False).start()
        for t in range(n):
            for k in OTHER_CHIPS:
                remote(t, k, True).wait_recv()
        for t in range(n):
            for k in OTHER_CHIPS:
                remote(t, k, False).wait_send()
            own(t).wait()

    anyspec = pl.BlockSpec(memory_space=pl.ANY)
    return pl.pallas_call(
        body, name=name,
        in_specs=[anyspec] * n,
        out_specs=[anyspec] * n,
        out_shape=[jax.ShapeDtypeStruct(a.shape, a.dtype) for a in arrays],
        scratch_shapes=[pltpu.SemaphoreType.DMA((n, N_DEV)), pltpu.SemaphoreType.DMA((n, N_DEV)),
                        pltpu.SemaphoreType.DMA((n,))],
    )(*arrays)


def pair_add(a, b, name):
    _, r, c = a.shape
    rb = r if r <= 512 else (128 if c > 1024 else 256)
    assert r % rb == 0

    def body(a_ref, b_ref, o_ref):
        o_ref[...] = (a_ref[...].astype(F32) + b_ref[...].astype(F32)).astype(o_ref.dtype)

    blk = pl.BlockSpec((1, rb, c), lambda q, i: (q, i, 0))
    return pl.pallas_call(
        body, name=name,
        grid=(N_CHIPS, r // rb),
        in_specs=[blk, blk],
        out_specs=blk,
        out_shape=jax.ShapeDtypeStruct(a.shape, a.dtype),
        compiler_params=_params(("parallel", "parallel")),
    )(a, b)


def adamw(parts, w, m, v, name):
    r, c = w.shape
    n_parts = parts.shape[0]
    rb = r if r <= 512 else (128 if c > 1024 else 256)
    assert r % rb == 0

    def body(p_ref, w_ref, m_ref, v_ref, g_out, d_out, m_out, v_out):
        g = p_ref[0].astype(F32)
        for i in range(1, n_parts):
            g = g + p_ref[i].astype(F32)
        m_new = ADAM_B1 * m_ref[...] + (1.0 - ADAM_B1) * g
        v_new = ADAM_B2 * v_ref[...] + (1.0 - ADAM_B2) * (g * g)
        m_hat = m_new / (1.0 - ADAM_B1 ** ADAM_STEP)
        v_hat = v_new / (1.0 - ADAM_B2 ** ADAM_STEP)
        g_out[...] = g
        d_out[...] = -ADAM_LR * (m_hat / (jnp.sqrt(v_hat) + ADAM_EPS) + ADAM_WD * w_ref[...])
        m_out[...] = m_new
        v_out[...] = v_new

    blk = pl.BlockSpec((rb, c), lambda i: (i, 0))
    return pl.pallas_call(
        body, name=name,
        grid=(r // rb,),
        in_specs=[pl.BlockSpec((n_parts, rb, c), lambda i: (0, i, 0)), blk, blk, blk],
        out_specs=[blk] * 4,
        out_shape=[jax.ShapeDtypeStruct((r, c), F32)] * 4,
        compiler_params=_params(("parallel",)),
    )(parts, w, m, v)


def _pad_lanes(vec, n=LANES):
    return jnp.pad(vec, ((0, 0), (0, n - vec.shape[1])))


def local_step(x, target, w):
    n_seq, seq, _ = x.shape
    rows = n_seq * seq
    x = x.reshape(rows, D_MODEL)
    target = target.reshape(rows, D_MODEL)
    mx = lambda a: a.astype(MXU)

    splits = [sum(IN_WIDTHS[:i]) for i in range(len(IN_WIDTHS) + 1)]
    w_in = w["w_in"]
    part = lambda i: w_in[:, splits[i]:splits[i + 1]]
    w_z, w_xbc, w_gs, w_ga = mx(part(0)), mx(part(1)), mx(part(6)), mx(part(7))
    w_dt = mx(_pad_lanes(part(2)))
    w_qkv = [mx(jnp.concatenate([part(3 + t)[:, g * ATT_OUT:(g + 1) * ATT_OUT] for t in range(3)], axis=1))
             for g in range(ATT_GROUPS)]
    w_sp, w_ap, w_o, w_d = mx(w["w_ssd_proj"]), mx(w["w_attn_proj"]), mx(w["w_out"]), mx(w["w_down"])
    w_ug, w_uv = mx(w["w_up"][:, :D_FF]), mx(w["w_up"][:, D_FF:])
    conv_w, conv_b = w["ssd_conv_w"], w["ssd_conv_b"]
    fconv_w, fconv_b = w["ffn_conv_w"], w["ffn_conv_b"]
    dt_bias, a_log, d_skip = _pad_lanes(w["dt_bias"]), _pad_lanes(w["a_log"]), _pad_lanes(w["d_skip"])
    g1, g2, gn, gq, gk = w["norm1_g"], w["norm2_g"], w["ssd_norm_g"], w["q_norm_g"], w["k_norm_g"]

    tb = min(512, seq)
    tbm = min(256, seq)
    cw = 1024
    rw = lambda fn, name, ncol, ins, params=(), outs=(), accs=(), tb_=tb: rowwise(
        fn, name, rows, seq, tb_, ncol, ins, params, outs, accs)

    (h,) = rw(lambda ctx, xv, g: _rms_fwd(xv, g), "rms1_fwd", 1, [(x, D_MODEL, 0, None)], [(g1, None, 0)],
              [(D_MODEL, D_MODEL, 0, MXU)])
    z = matmul(h, w_z, "mm_z")
    xbc = matmul(h, w_xbc, "mm_xbc")
    dtraw = matmul(h, w_dt, "mm_dt")
    qkv = [matmul(h, w_qkv[g], f"mm_qkv{g}") for g in range(ATT_GROUPS)]
    gs = matmul(h, w_gs, "mm_gs")
    ga = matmul(h, w_ga, "mm_ga")

    def conv_silu(ctx, xh, wv, bv):
        return _silu(bv + _conv_prev(xh[0], xh[1], wv, ctx.first, SSD_CONV))

    (xact,) = rw(conv_silu, "ssd_conv_fwd", CONV_DIM // cw, [(xbc, cw, 0, "prev")],
                 [(conv_w, cw, 0), (conv_b, cw, 0)], [(CONV_DIM, cw, 0, F32)])
    y, sin = ssd_fwd(xact, dtraw, dt_bias, a_log, d_skip, n_seq, seq)

    def gated_norm(ctx, yv, zv, g):
        yz = yv * _silu(zv)
        return jnp.concatenate([_rms_fwd(yz[:, i:i + NORM_GROUP], g[:, i:i + NORM_GROUP])
                                for i in range(0, cw, NORM_GROUP)], axis=1)

    (y_ssd,) = rw(gated_norm, "ssd_post_fwd", D_INNER // cw, [(y, cw, 0, None), (z, cw, 0, None)], [(gn, cw, 0)],
                  [(D_INNER, cw, 0, MXU)])

    gq_t, gk_t = jnp.tile(gq, (1, ATT_H)), jnp.tile(gk, (1, ATT_H))
    dilated = lambda a, g: _dilated(a, n_seq, seq, ATT_DILATIONS[g])
    nq = [dilated(qk_prep(qkv[g], gq_t, gk_t, rows, f"qk_prep{g}"), g) for g in range(ATT_GROUPS)]
    att = [attn_fwd2(nq[g], n_seq, seq, ATT_DILATIONS[g], f"attn_fwd{g}") for g in range(ATT_GROUPS)]

    def combine(ctx, o0, o1, o2, l0, l1, l2):
        mxl = jnp.maximum(jnp.maximum(l0, l1), l2)
        e = [jnp.exp(l - mxl) for l in (l0, l1, l2)]
        inv = 1.0 / (e[0] + e[1] + e[2])
        ws = [ei * inv for ei in e]
        out = sum(_expand_heads(wi) * oi for wi, oi in zip(ws, (o0, o1, o2)))
        return (out, *ws)

    y_attn, wt0, wt1, wt2 = rw(
        combine, "attn_combine", 1,
        [(att[g][0].reshape(rows, ATT_OUT), ATT_OUT, 0, None) for g in range(3)]
        + [(att[g][1].reshape(rows, LANES), LANES, 0, None) for g in range(3)], [],
        [(ATT_OUT, ATT_OUT, 0, F32)] + [(LANES, LANES, 0, F32)] * 3)
    wts = (wt0, wt1, wt2)

    ps = matmul(y_ssd, w_sp, "mm_ssd_proj")
    pa = matmul(y_attn, w_ap, "mm_attn_proj")
    (merged,) = rw(lambda ctx, a, b, c, d: _sigmoid(c) * a + _sigmoid(d) * b, "merge_fwd", D_MODEL // cw,
                   [(ps, cw, 0, None), (pa, cw, 0, None), (gs, cw, 0, None), (ga, cw, 0, None)], [],
                   [(D_MODEL, cw, 0, MXU)])
    x1 = matmul(merged, w_o, "mm_out", add=x)
    (h2,) = rw(lambda ctx, xv, g: _rms_fwd(xv, g), "rms2_fwd", 1, [(x1, D_MODEL, 0, None)], [(g2, None, 0)],
               [(D_MODEL, D_MODEL, 0, MXU)])
    up_g = matmul(h2, w_ug, "mm_up_g")
    up_v = matmul(h2, w_uv, "mm_up_v")
    fw = D_FF // 2
    nfc = D_FF // fw

    def mlp_act(ctx, ug, uv, wg, wv, bg, bv):
        cg = bg + _conv_prev(ug[0], ug[1], wg, ctx.first, FFN_CONV)
        cv = bv + _conv_prev(uv[0], uv[1], wv, ctx.first, FFN_CONV)
        return _silu(cg) * cv

    (act,) = rw(mlp_act, "mlp_act_fwd", nfc, [(up_g, fw, 0, "prev"), (up_v, fw, 0, "prev")],
                [(fconv_w, fw, 0), (fconv_w, fw, nfc), (fconv_b, fw, 0), (fconv_b, fw, nfc)], [(D_FF, fw, 0, MXU)],
                tb_=tbm)
    x2 = matmul(act, w_d, "mm_down", add=x1)

    def loss_fn(ctx, xv, tv):
        d = xv - tv
        g = d * (1.0 / D_MODEL)
        return g, g, jnp.sum(d * d, axis=0, keepdims=True)

    dx2, dx2_m, sq = rw(loss_fn, "loss", 1, [(x2, D_MODEL, 0, None), (target, D_MODEL, 0, None)], [],
                        [(D_MODEL, D_MODEL, 0, F32), (D_MODEL, D_MODEL, 0, MXU)], [(1, D_MODEL)])

    grads = {}
    dact = matmul(dx2_m, w_d, "mm_d_act", tb=True)
    grads["w_down"] = matmul(act, dx2_m, "mm_dw_down", ta=True, out_dtype=MXU)

    def mlp_bwd(ctx, da, ug, uv, wg, wv, bg, bv):
        cg, cg_n = _conv_pre(ug, wg, bg, ctx.first, FFN_CONV)
        cv, cv_n = _conv_pre(uv, wv, bv, ctx.first, FFN_CONV)
        da_c, da_n = da
        dup_g_, dwg, dbg = _conv_bwd(da_c * cv * _silu_grad(cg), da_n * cv_n * _silu_grad(cg_n), ug, wg, ctx, FFN_CONV)
        dup_v_, dwv, dbv = _conv_bwd(da_c * _silu(cg), da_n * _silu(cg_n), uv, wv, ctx, FFN_CONV)
        return dup_g_, dup_v_, dwg, dbg, dwv, dbv

    dup_g, dup_v, dfw_g, dfb_g, dfw_v, dfb_v = rw(
        mlp_bwd, "mlp_bwd", nfc, [(dact, fw, 0, "next"), (up_g, fw, 0, "both"), (up_v, fw, 0, "both")],
        [(fconv_w, fw, 0), (fconv_w, fw, nfc), (fconv_b, fw, 0), (fconv_b, fw, nfc)],
        [(D_FF, fw, 0, MXU), (D_FF, fw, 0, MXU)], [(FFN_CONV, fw), (1, fw), (FFN_CONV, fw), (1, fw)], tb_=tbm)
    grads["ffn_conv_w"] = jnp.concatenate([dfw_g, dfw_v], axis=1)
    grads["ffn_conv_b"] = jnp.concatenate([dfb_g, dfb_v], axis=1)
    dh2 = matmul(dup_g, w_ug, "mm_dh2_g", tb=True)
    dh2 = matmul(dup_v, w_uv, "mm_dh2_v", tb=True, add=dh2)
    grads["w_up"] = jnp.concatenate([matmul(h2, dup_g, "mm_dw_up_g", ta=True, out_dtype=MXU),
                                     matmul(h2, dup_v, "mm_dw_up_v", ta=True, out_dtype=MXU)], axis=1)

    def rms_bwd_fn(ctx, xv, dh_, dres, g):
        dxv, dg = _rms_bwd(xv, g, dh_)
        return dres + dxv, dg

    def rms_bwd_fn2(ctx, xv, dh_, dres, g):
        dxv, dg = rms_bwd_fn(ctx, xv, dh_, dres, g)
        return dxv, dxv, dg

    dx1, dx1_m, grads["norm2_g"] = rw(
        rms_bwd_fn2, "rms2_bwd", 1, [(x1, D_MODEL, 0, None), (dh2, D_MODEL, 0, None), (dx2, D_MODEL, 0, None)],
        [(g2, None, 0)], [(D_MODEL, D_MODEL, 0, F32), (D_MODEL, D_MODEL, 0, MXU)], [(1, D_MODEL)])

    dmerged = matmul(dx1_m, w_o, "mm_d_merged", tb=True)
    grads["w_out"] = matmul(merged, dx1_m, "mm_dw_out", ta=True, out_dtype=MXU)

    def merge_bwd(ctx, dm, a, b, c, d):
        sc, sd = _sigmoid(c), _sigmoid(d)
        return dm * sc, dm * sd, dm * a * sc * (1.0 - sc), dm * b * sd * (1.0 - sd)

    dps, dpa, dgs, dga = rw(merge_bwd, "merge_bwd", D_MODEL // cw,
                            [(dmerged, cw, 0, None), (ps, cw, 0, None), (pa, cw, 0, None), (gs, cw, 0, None),
                             (ga, cw, 0, None)], [], [(D_MODEL, cw, 0, MXU)] * 4)
    dy_ssd = matmul(dps, w_sp, "mm_d_y_ssd", tb=True)
    grads["w_ssd_proj"] = matmul(y_ssd, dps, "mm_dw_ssd_proj", ta=True, out_dtype=MXU)
    dy_attn = matmul(dpa, w_ap, "mm_d_y_attn", tb=True)
    grads["w_attn_proj"] = matmul(y_attn, dpa, "mm_dw_attn_proj", ta=True, out_dtype=MXU)

    (rsum,) = rw(lambda ctx, a, b: _reduce_heads(a * b), "attn_rsum", 1,
                 [(dy_attn, ATT_OUT, 0, None), (y_attn, ATT_OUT, 0, None)], [], [(LANES, LANES, 0, F32)])
    dqkv, dgq, dgk = [], 0.0, 0.0
    for g in range(ATT_GROUPS):
        dn = attn_bwd2(nq[g], dilated(dy_attn, g), att[g][1], dilated(wts[g], g), dilated(rsum, g), n_seq, seq,
                       ATT_DILATIONS[g], f"attn_bwd{g}")
        d_, a_, b_ = qk_post(qkv[g], dn.reshape(rows, QKV_W), gq_t, gk_t, rows, f"qk_post{g}")
        dqkv.append(d_)
        dgq, dgk = dgq + a_, dgk + b_
    per_head = lambda v: jnp.sum(v.reshape(ATT_H, ATT_HD), axis=0, keepdims=True)
    grads["q_norm_g"], grads["k_norm_g"] = per_head(dgq), per_head(dgk)

    def gated_norm_bwd(ctx, dyn, yv, zv, g):
        sz = _silu(zv)
        yz = yv * sz
        dyz, dgs_ = [], []
        for i in range(0, cw, NORM_GROUP):
            a, b = _rms_bwd(yz[:, i:i + NORM_GROUP], g[:, i:i + NORM_GROUP], dyn[:, i:i + NORM_GROUP])
            dyz.append(a)
            dgs_.append(b)
        dyz = jnp.concatenate(dyz, axis=1)
        return dyz * sz, dyz * yv * _silu_grad(zv), jnp.concatenate(dgs_, axis=1)

    dy, dz, grads["ssd_norm_g"] = rw(gated_norm_bwd, "ssd_post_bwd", D_INNER // cw,
                                     [(dy_ssd, cw, 0, None), (y, cw, 0, None), (z, cw, 0, None)], [(gn, cw, 0)],
                                     [(D_INNER, cw, 0, F32), (D_INNER, cw, 0, MXU)], [(1, cw)])
    dxact, ddt, dbias, dalog, ddskip = ssd_bwd(xact, dtraw, dt_bias, a_log, d_skip, sin, dy, n_seq, seq)
    grads["dt_bias"], grads["a_log"], grads["d_skip"] = dbias[:, :SSD_H], dalog[:, :SSD_H], ddskip[:, :SSD_H]

    def conv_silu_bwd(ctx, dxa, xin, wv, bv):
        pre, pre_n = _conv_pre(xin, wv, bv, ctx.first, SSD_CONV)
        return _conv_bwd(dxa[0] * _silu_grad(pre), dxa[1] * _silu_grad(pre_n), xin, wv, ctx, SSD_CONV)

    dxbc, grads["ssd_conv_w"], grads["ssd_conv_b"] = rw(
        conv_silu_bwd, "ssd_conv_bwd", CONV_DIM // cw, [(dxact, cw, 0, "next"), (xbc, cw, 0, "both")],
        [(conv_w, cw, 0), (conv_b, cw, 0)], [(CONV_DIM, cw, 0, MXU)], [(SSD_CONV, cw), (1, cw)])

    pieces = [(dz, w_z, "z"), (dxbc, w_xbc, "xbc"), (ddt, w_dt, "dt"), (dgs, w_gs, "gs"), (dga, w_ga, "ga")]
    pieces += [(dqkv[g], w_qkv[g], f"qkv{g}") for g in range(ATT_GROUPS)]
    dh, dws = None, {}
    for dpart, wpart, tag in pieces:
        dh = matmul(dpart, wpart, f"mm_dh_{tag}", tb=True, add=dh)
        dws[tag] = matmul(h, dpart, f"mm_dw_{tag}", ta=True, out_dtype=MXU)
    dq_parts = [[dws[f"qkv{g}"][:, t * ATT_OUT:(t + 1) * ATT_OUT] for g in range(ATT_GROUPS)] for t in range(3)]
    grads["w_in"] = jnp.concatenate(
        [dws["z"], dws["xbc"], dws["dt"][:, :SSD_H]] + [p for t in range(3) for p in dq_parts[t]] + [dws["gs"], dws["ga"]],
        axis=1)
    grad_x, grads["norm1_g"] = rw(rms_bwd_fn, "rms1_bwd", 1,
                                  [(x, D_MODEL, 0, None), (dh, D_MODEL, 0, None), (dx1, D_MODEL, 0, None)],
                                  [(g1, None, 0)], [(D_MODEL, D_MODEL, 0, F32)], [(1, D_MODEL)])
    return sq, grad_x.reshape(n_seq, seq, D_MODEL), grads


COL_SHARDED = ("w_in", "ssd_conv_w", "w_attn_proj", "w_up", "ffn_conv_w")
ROW_SHARDED = ("w_ssd_proj", "w_out", "w_down")
MATRICES = ("w_in", "w_attn_proj", "w_up", "w_ssd_proj", "w_out", "w_down")
REPLICATED = ("norm1_g", "ssd_conv_b", "dt_bias", "a_log", "d_skip", "ssd_norm_g", "q_norm_g", "k_norm_g",
              "norm2_g", "ffn_conv_b")
WEIGHTS = ("norm1_g", "w_in", "ssd_conv_w", "ssd_conv_b", "dt_bias", "a_log", "d_skip", "ssd_norm_g", "w_ssd_proj",
           "q_norm_g", "k_norm_g", "w_attn_proj", "w_out", "norm2_g", "w_up", "ffn_conv_w", "ffn_conv_b", "w_down")
PACK_ROWS, PACK_COLS = 8, 2048


def _pack(vals):
    flat = jnp.concatenate([vals[n].reshape(-1) for n in REPLICATED])
    return jnp.pad(flat, (0, PACK_ROWS * PACK_COLS - flat.shape[0])).reshape(PACK_ROWS, PACK_COLS)


def _unpack(packed, like):
    flat = packed.reshape(-1)
    out, pos = {}, 0
    for n in REPLICATED:
        size = like[n].size
        out[n] = flat[pos:pos + size].reshape(like[n].shape)
        pos += size
    return out


def step(x, target, wsh, msh, vsh):
    sharded = COL_SHARDED + ROW_SHARDED
    narrow = lambda n, a: a.astype(MXU) if n in MATRICES else a
    gathered = exchange([(narrow(n, wsh[n]), "gather") for n in sharded], "ag_weights")
    full = {n: wsh[n] for n in REPLICATED}
    for n, g in zip(sharded, gathered):
        if n in COL_SHARDED:
            full[n] = jnp.transpose(g, (1, 0, 2)).reshape(g.shape[1], N_DEV * g.shape[2])
        else:
            full[n] = g.reshape(N_DEV * g.shape[1], g.shape[2])

    sq, grad_x, grads = local_step(x, target, full)

    slabs = []
    for n in sharded:
        g = narrow(n, grads[n])
        if n in COL_SHARDED:
            slabs.append(jnp.transpose(g.reshape(g.shape[0], N_DEV, g.shape[1] // N_DEV), (1, 0, 2)))
        else:
            slabs.append(g.reshape(N_DEV, g.shape[0] // N_DEV, g.shape[1]))
    packed_g = _pack({n: grads[n] for n in REPLICATED})
    core = lax.axis_index("c")
    from_sibling = sibling_swap(slabs, "rs_sibling")
    chip_parts = []
    for n, s, f in zip(sharded, slabs, from_sibling):
        mine = lax.dynamic_index_in_dim(s.reshape((N_CHIPS, 2) + s.shape[1:]), core, axis=1, keepdims=False)
        chip_parts.append(pair_add(mine, f, f"rs_add_{n}"))
    received = chip_exchange(chip_parts, "rs_chips")
    (small,) = exchange([(packed_g, "gather")], "ag_small")

    out_g, out_d, out_m, out_v = {}, {}, {}, {}
    for n, parts in zip(sharded, received):
        out_g[n], out_d[n], out_m[n], out_v[n] = adamw(parts, wsh[n], msh[n], vsh[n], f"adamw_{n}")
    pk = adamw(small, _pack(wsh), _pack(msh), _pack(vsh), "adamw_small")
    for dst, packed in zip((out_g, out_d, out_m, out_v), pk):
        dst.update(_unpack(packed, wsh))
    loss = lax.psum(0.5 * jnp.sum(sq) / D_MODEL, ("x", "y", "c"))
    return loss, grad_x, out_g, out_d, out_m, out_v


def kernel(x, norm1_g, w_in, ssd_conv_w, ssd_conv_b, dt_bias, a_log, d_skip, ssd_norm_g, w_ssd_proj, q_norm_g, k_norm_g, w_attn_proj, w_out, norm2_g, w_up, ffn_conv_w, ffn_conv_b, w_down, loss_target, m_norm1_g, m_w_in, m_ssd_conv_w, m_ssd_conv_b, m_dt_bias, m_a_log, m_d_skip, m_ssd_norm_g, m_w_ssd_proj, m_q_norm_g, m_k_norm_g, m_w_attn_proj, m_w_out, m_norm2_g, m_w_up, m_ffn_conv_w, m_ffn_conv_b, m_w_down, v_norm1_g, v_w_in, v_ssd_conv_w, v_ssd_conv_b, v_dt_bias, v_a_log, v_d_skip, v_ssd_norm_g, v_w_ssd_proj, v_q_norm_g, v_k_norm_g, v_w_attn_proj, v_w_out, v_norm2_g, v_w_up, v_ffn_conv_w, v_ffn_conv_b, v_w_down):
    ws = (norm1_g, w_in, ssd_conv_w, ssd_conv_b, dt_bias, a_log, d_skip, ssd_norm_g, w_ssd_proj, q_norm_g, k_norm_g,
          w_attn_proj, w_out, norm2_g, w_up, ffn_conv_w, ffn_conv_b, w_down)
    ms = (m_norm1_g, m_w_in, m_ssd_conv_w, m_ssd_conv_b, m_dt_bias, m_a_log, m_d_skip, m_ssd_norm_g, m_w_ssd_proj,
          m_q_norm_g, m_k_norm_g, m_w_attn_proj, m_w_out, m_norm2_g, m_w_up, m_ffn_conv_w, m_ffn_conv_b, m_w_down)
    vs = (v_norm1_g, v_w_in, v_ssd_conv_w, v_ssd_conv_b, v_dt_bias, v_a_log, v_d_skip, v_ssd_norm_g, v_w_ssd_proj,
          v_q_norm_g, v_k_norm_g, v_w_attn_proj, v_w_out, v_norm2_g, v_w_up, v_ffn_conv_w, v_ffn_conv_b, v_w_down)
    strip = lambda a: a[0] if a.ndim == 3 else a
    wsh = {n: strip(a) for n, a in zip(WEIGHTS, ws)}
    msh = {n: strip(a) for n, a in zip(WEIGHTS, ms)}
    vsh = {n: strip(a) for n, a in zip(WEIGHTS, vs)}
    loss, grad_x, g, d, m, v = step(x, loss_target, wsh, msh, vsh)
    lead = lambda dct: [dct[n][None] if a.ndim == 3 else dct[n] for n, a in zip(WEIGHTS, ws)]
    return (loss, grad_x, *lead(g), *lead(d), *lead(m), *lead(v))
```

```python
import jax
import jax.numpy as jnp
from jax import lax
from jax.experimental import pallas as pl
from jax.experimental.pallas import tpu as pltpu

F32 = jnp.float32
BF16 = jnp.bfloat16
MXU = jnp.bfloat16
HIGHEST = lax.Precision.HIGHEST
VMEM_LIMIT_BYTES = 48 * 1024 * 1024
SUBLANES = 8
LANES = 128
N_DEV = 8

D_MODEL = 1024
D_INNER = 2048
SSD_P = 64
SSD_H = 32
SSD_G = 8
SSD_K = SSD_H // SSD_G
SSD_N = 128
SSD_Q = 128
SSD_CONV = 4
CONV_DIM = D_INNER + 2 * SSD_G * SSD_N
NORM_GROUP = D_INNER // SSD_G
ATT_GROUPS = 3
ATT_H = 8
ATT_HD = 64
ATT_BLK = 128
ATT_OUT = ATT_H * ATT_HD
ATT_DILATIONS = (1, 4, 16)
ATT_SCALE = ATT_HD ** -0.5
D_FF = 2816
FFN_CONV = 3
EPS = 1e-6
NEG = -1e30
IN_WIDTHS = (D_INNER, CONV_DIM, SSD_H, 3 * ATT_OUT, 3 * ATT_OUT, 3 * ATT_OUT, D_MODEL, D_MODEL)

ADAM_LR = 0.001
ADAM_B1 = 0.9
ADAM_B2 = 0.999
ADAM_EPS = 1e-08
ADAM_WD = 0.01
ADAM_STEP = 10


def _mm(a, b, dims):
    return lax.dot_general(a.astype(MXU), b.astype(MXU), (dims, ((), ())), preferred_element_type=F32)


def _dot_nn(a, b):
    return _mm(a, b, ((1,), (0,)))


def _dot_nt(a, b):
    return _mm(a, b, ((1,), (1,)))


def _dot_tn(a, b):
    return _mm(a, b, ((0,), (0,)))


def _dot_f32(a, b):
    return lax.dot_general(a, b, (((1,), (0,)), ((), ())), precision=HIGHEST, preferred_element_type=F32)


def _sigmoid(x):
    return 1.0 / (1.0 + jnp.exp(-x))


def _silu(x):
    return x * _sigmoid(x)


def _silu_grad(x):
    s = _sigmoid(x)
    return s * (1.0 + x * (1.0 - s))


def _softplus(x):
    return jnp.maximum(x, 0.0) + jnp.log(1.0 + jnp.exp(-jnp.abs(x)))


def _rms_fwd(x, g):
    r = lax.rsqrt(jnp.mean(x * x, axis=-1, keepdims=True) + EPS)
    return x * r * g


def _rms_bwd(x, g, dy):
    r = lax.rsqrt(jnp.mean(x * x, axis=-1, keepdims=True) + EPS)
    xh = x * r
    dyg = dy * g
    dx = r * (dyg - xh * jnp.mean(dyg * xh, axis=-1, keepdims=True))
    return dx, jnp.sum(dy * xh, axis=0, keepdims=True)


def _onehot_row(h, n=LANES):
    return (lax.broadcasted_iota(jnp.int32, (1, n), 1) == h).astype(F32)


def _onehot_col(h, n=LANES):
    return (lax.broadcasted_iota(jnp.int32, (n, 1), 0) == h).astype(F32)


def _head_expand_matrix():
    r = lax.broadcasted_iota(jnp.int32, (LANES, ATT_OUT), 0)
    c = lax.broadcasted_iota(jnp.int32, (LANES, ATT_OUT), 1)
    return (c // ATT_HD == r).astype(F32)


def _split_bf16(x, parts):
    out = []
    for _ in range(parts - 1):
        hi = x.astype(BF16).astype(F32)
        out.append(hi)
        x = x - hi
    out.append(x)
    return out


def _expand_heads(w):
    e = _head_expand_matrix()
    return sum(_dot_nn(p, e) for p in _split_bf16(w, 2))


def _reduce_heads(x):
    e = _head_expand_matrix()
    return sum(_dot_nt(p, e) for p in _split_bf16(x, 3))


def _shift_prev(cur, halo, s, first):
    if s == 0:
        return cur
    rolled = pltpu.roll(cur, s, 0)
    hr = jnp.where(first, 0.0, pltpu.roll(halo, s, 0))
    rows = lax.broadcasted_iota(jnp.int32, halo.shape, 0)
    head = jnp.where(rows < s, hr, rolled[:SUBLANES])
    if cur.shape[0] == SUBLANES:
        return head
    return jnp.concatenate([head, rolled[SUBLANES:]], axis=0)


def _shift_next(cur, halo, s, last):
    if s == 0:
        return cur
    tb = cur.shape[0]
    rolled = pltpu.roll(cur, tb - s, 0)
    hr = jnp.where(last, 0.0, pltpu.roll(halo, SUBLANES - s, 0))
    rows = lax.broadcasted_iota(jnp.int32, halo.shape, 0)
    tail = jnp.where(rows >= SUBLANES - s, hr, rolled[tb - SUBLANES:])
    return jnp.concatenate([rolled[:tb - SUBLANES], tail], axis=0)


def _conv_prev(x, halo, w, first, taps):
    acc = None
    for i in range(taps):
        term = w[i:i + 1, :] * _shift_prev(x, halo, taps - 1 - i, first)
        acc = term if acc is None else acc + term
    return acc


def _conv_pre(x, w, b, first, taps):
    cur, prev8, next8 = x
    tail = cur[cur.shape[0] - SUBLANES:]
    return b + _conv_prev(cur, prev8, w, first, taps), b + _conv_prev(next8, tail, w, False, taps)


def _conv_bwd(dpre, dpre_next8, x, w, ctx, taps):
    cur, prev8, _ = x
    dx, dws = None, []
    for i in range(taps):
        term = w[i:i + 1, :] * _shift_next(dpre, dpre_next8, taps - 1 - i, ctx.last)
        dx = term if dx is None else dx + term
        dws.append(jnp.sum(dpre * _shift_prev(cur, prev8, taps - 1 - i, ctx.first), axis=0, keepdims=True))
    return dx, jnp.concatenate(dws, axis=0), jnp.sum(dpre, axis=0, keepdims=True)


def _params(sem):
    return pltpu.CompilerParams(dimension_semantics=sem, vmem_limit_bytes=VMEM_LIMIT_BYTES)


def _pick(dim, target):
    if dim <= target:
        return dim
    best = None
    for t in range(LANES, target + 1, LANES):
        if dim % t == 0:
            best = t
    assert best is not None, (dim, target)
    return best


MATMUL_VMEM_BUDGET = 34 * 1024 * 1024


V7X_MXU_FLOPS = 996e12
V7X_HBM_BYTES_PER_S = 3.4e12
GRID_STEP_S = 0.35e-6


def _tile_sizes(dim, cap):
    return [t for t in range(LANES, min(dim, cap) + 1, LANES) if dim % t == 0] or [dim]


def _matmul_tiles(m, n, k, a_bytes, b_bytes, add_bytes, out_bytes):
    best = None
    for tk in _tile_sizes(k, 8192):
        nk = k // tk
        for tn in _tile_sizes(n, 2048):
            for tm in _tile_sizes(m, 2048):
                io = tm * tk * a_bytes + tk * tn * b_bytes
                ends = tm * tn * (add_bytes + out_bytes)
                need = 2 * (io + ends) + tm * tn * 4 * (2 if nk > 1 else 1)
                if need > MATMUL_VMEM_BUDGET:
                    continue
                step = max(2.0 * tm * tn * tk / V7X_MXU_FLOPS, (io + ends / nk) / V7X_HBM_BYTES_PER_S)
                if nk > 1:
                    step += tm * tn * 8 / V7X_HBM_BYTES_PER_S
                cost = (m // tm) * (n // tn) * nk * (step + GRID_STEP_S)
                if best is None or cost < best[0]:
                    best = (cost, tm, tn, tk)
    if best is None:
        raise ValueError((m, n, k))
    return best[1:]


def matmul(a, b, name, ta=False, tb=False, add=None, out_dtype=F32):
    assert not (ta and tb)
    m, k = (a.shape[1], a.shape[0]) if ta else a.shape
    n = b.shape[0] if tb else b.shape[1]
    assert (b.shape[1] if tb else b.shape[0]) == k
    tm, tn, tk = _matmul_tiles(m, n, k, a.dtype.itemsize, b.dtype.itemsize,
                               0 if add is None else add.dtype.itemsize, jnp.dtype(out_dtype).itemsize)
    nk = k // tk
    dims = ((0,), (0,)) if ta else (((1,), (1,)) if tb else ((1,), (0,)))

    def body(*refs):
        if add is None:
            a_ref, b_ref, o_ref = refs[:3]
        else:
            a_ref, b_ref, add_ref, o_ref = refs[:4]

        def finish(r):
            if add is not None:
                r = r + add_ref[...].astype(F32)
            o_ref[...] = r.astype(out_dtype)

        if nk == 1:
            finish(_mm(a_ref[...], b_ref[...], dims))
            return
        acc = refs[-1]
        kk = pl.program_id(2)

        @pl.when(kk == 0)
        def _():
            acc[...] = jnp.zeros_like(acc)

        acc[...] += _mm(a_ref[...], b_ref[...], dims)

        @pl.when(kk == nk - 1)
        def _():
            finish(acc[...])

    a_spec = pl.BlockSpec((tk, tm), lambda i, j, kk: (kk, i)) if ta else pl.BlockSpec((tm, tk), lambda i, j, kk: (i, kk))
    b_spec = pl.BlockSpec((tn, tk), lambda i, j, kk: (j, kk)) if tb else pl.BlockSpec((tk, tn), lambda i, j, kk: (kk, j))
    in_specs = [a_spec, b_spec]
    args = [a, b]
    if add is not None:
        in_specs.append(pl.BlockSpec((tm, tn), lambda i, j, kk: (i, j)))
        args.append(add)
    return pl.pallas_call(
        body, name=name,
        grid=(m // tm, n // tn, nk),
        in_specs=in_specs,
        out_specs=pl.BlockSpec((tm, tn), lambda i, j, kk: (i, j)),
        out_shape=jax.ShapeDtypeStruct((m, n), out_dtype),
        scratch_shapes=[] if nk == 1 else [pltpu.VMEM((tm, tn), F32)],
        compiler_params=_params(("parallel", "parallel", "arbitrary")),
    )(*args)


class _Ctx:
    def __init__(self, first, last):
        self.first = first
        self.last = last


def rowwise(fn, name, rows, seq, tb, ncol, ins, params=(), outs=(), accs=()):
    assert rows % tb == 0 and seq % tb == 0 and tb % 16 == 0
    bps = seq // tb
    nrow = rows // tb
    r8 = tb // SUBLANES
    args, in_specs = [], []
    for arr, w, off, halo in ins:
        args.append(arr)
        in_specs.append(pl.BlockSpec((tb, w), lambda j, i, off=off: (i, off + j)))
        if halo in ("prev", "both"):
            args.append(arr)
            in_specs.append(pl.BlockSpec((SUBLANES, w), lambda j, i, off=off: (jnp.maximum(i * r8 - 1, 0), off + j)))
        if halo in ("next", "both"):
            args.append(arr)
            in_specs.append(pl.BlockSpec(
                (SUBLANES, w), lambda j, i, off=off: (jnp.minimum((i + 1) * r8, rows // SUBLANES - 1), off + j)))
    for arr, w, off in params:
        args.append(arr)
        if w is None:
            in_specs.append(pl.BlockSpec(arr.shape, lambda j, i: (0, 0)))
        else:
            in_specs.append(pl.BlockSpec((arr.shape[0], w), lambda j, i, off=off: (0, off + j)))
    out_shape, out_specs = [], []
    for total, w, off, dt in outs:
        out_shape.append(jax.ShapeDtypeStruct((rows, total), dt))
        out_specs.append(pl.BlockSpec((tb, w), lambda j, i, off=off: (i, off + j)))
    for r, w in accs:
        out_shape.append(jax.ShapeDtypeStruct((r, ncol * w), F32))
        out_specs.append(pl.BlockSpec((r, w), lambda j, i: (0, j)))
    n_out, n_acc = len(outs), len(accs)

    def body(*refs):
        i = pl.program_id(1)
        pos = 0
        vals = []
        for _, _, _, halo in ins:
            cur = refs[pos][...]
            pos += 1
            if halo is None:
                vals.append(cur)
            elif halo == "both":
                vals.append((cur, refs[pos][...], refs[pos + 1][...]))
                pos += 2
            else:
                vals.append((cur, refs[pos][...]))
                pos += 1
        for _ in params:
            vals.append(refs[pos][...])
            pos += 1
        ctx = _Ctx(i % bps == 0, i % bps == bps - 1)
        res = fn(ctx, *vals)
        if not isinstance(res, (tuple, list)):
            res = (res,)
        assert len(res) == n_out + n_acc
        for q in range(n_out):
            refs[pos + q][...] = res[q].astype(refs[pos + q].dtype)
        for q in range(n_acc):
            ref, val = refs[pos + n_out + q], res[n_out + q]

            @pl.when(i == 0)
            def _(ref=ref, val=val):
                ref[...] = val

            @pl.when(i != 0)
            def _(ref=ref, val=val):
                ref[...] += val

    res = pl.pallas_call(
        body, name=name,
        grid=(ncol, nrow),
        in_specs=in_specs,
        out_specs=out_specs,
        out_shape=out_shape,
        compiler_params=_params(("parallel", "arbitrary")),
    )(*args)
    return res


def _direct_exchange(modes, in_refs, out_refs, send_sems, recv_sems, local_sems):
    n = len(modes)
    x, y, c = lax.axis_index("x"), lax.axis_index("y"), lax.axis_index("c")
    me = 4 * x + 2 * y + c

    def copy(t, k, arriving):
        px = 1 - x if k & 4 else x
        py = 1 - y if k & 2 else y
        pc = 1 - c if k & 1 else c
        pid = 4 * px + 2 * py + pc
        src = in_refs[t] if modes[t] == "gather" else in_refs[t].at[pid]
        return pltpu.make_async_remote_copy(
            src_ref=src, dst_ref=out_refs[t].at[pid if arriving else me], send_sem=send_sems.at[t, k],
            recv_sem=recv_sems.at[t, k], device_id=(px, py, pc), device_id_type=pl.DeviceIdType.MESH)

    def own(t):
        src = in_refs[t] if modes[t] == "gather" else in_refs[t].at[me]
        return pltpu.make_async_copy(src, out_refs[t].at[me], local_sems.at[t])

    def start():
        for t in range(n):
            own(t).start()
            for k in range(1, N_DEV):
                copy(t, k, False).start()

    def finish():
        for t in range(n):
            for k in range(1, N_DEV):
                copy(t, k, True).wait_recv()
        for t in range(n):
            for k in range(1, N_DEV):
                copy(t, k, False).wait_send()
            own(t).wait()

    return start, finish


def _exchange_specs(items):
    n = len(items)
    out_shape = [jax.ShapeDtypeStruct((N_DEV,) + tuple(a.shape if mode == "gather" else a.shape[1:]), a.dtype)
                 for a, mode in items]
    anyspec = pl.BlockSpec(memory_space=pl.ANY)
    sems = [pltpu.SemaphoreType.DMA((n, N_DEV)), pltpu.SemaphoreType.DMA((n, N_DEV)), pltpu.SemaphoreType.DMA((n,))]
    return [anyspec] * n, [anyspec] * n, out_shape, sems


GROUP_W = SSD_K * SSD_P


def _tri(lower):
    r = lax.broadcasted_iota(jnp.int32, (SSD_Q, SSD_Q), 0)
    c = lax.broadcasted_iota(jnp.int32, (SSD_Q, SSD_Q), 1)
    return r >= c if lower else r <= c


def _group_masks():
    lane = lax.broadcasted_iota(jnp.int32, (1, GROUP_W), 1) // SSD_P
    row = lax.broadcasted_iota(jnp.int32, (GROUP_W, 1), 0) // SSD_P
    return [lane == k for k in range(SSD_K)], [row == k for k in range(SSD_K)]


def _per_head(masks, vals):
    out = jnp.where(masks[0], vals[0], 0.0)
    for m, v in zip(masks[1:], vals[1:]):
        out = jnp.where(m, v, out)
    return out


def _headsum(prod, g, lane_masks):
    out = None
    for k in range(SSD_K):
        term = jnp.sum(jnp.where(lane_masks[k], prod, 0.0), axis=1, keepdims=True) * _onehot_row(g * SSD_K + k)
        out = term if out is None else out + term
    return out


def ssd_fwd(xact, dtraw, dt_bias, a_log, d_skip, n_seq, seq, comm=()):
    nc = seq // SSD_Q
    rows = n_seq * seq
    nx = len(comm)

    def body(*refs):
        xact_ref, dtraw_ref, bias_ref, alog_ref, dskip_ref = refs[:5]
        y_ref, sin_ref = refs[5 + nx:7 + nx]
        state, cs_s, cst_s, dt_s = refs[7 + 2 * nx:11 + 2 * nx]
        b, c = pl.program_id(0), pl.program_id(1)
        if nx:
            start, finish = _direct_exchange([m for _, m in comm], refs[5:5 + nx], refs[7 + nx:7 + 2 * nx],
                                             *refs[11 + 2 * nx:])
            pl.when(jnp.logical_and(b == 0, c == 0))(start)

        @pl.when(c == 0)
        def _():
            state[...] = jnp.zeros_like(state)

        sin_ref[0] = state[...]
        dt = _softplus(dtraw_ref[...] + bias_ref[...])
        a = dt * (-jnp.exp(alog_ref[...]))
        cs = _dot_f32(_tri(True).astype(F32), a)
        cs_s[...] = cs
        cst_s[...] = cs.T
        dt_s[...] = dt
        causal = _tri(True)
        lane_masks, row_masks = _group_masks()
        for g in range(SSD_G):
            heads = [g * SSD_K + k for k in range(SSD_K)]
            bg = xact_ref[:, pl.ds(D_INNER + g * SSD_N, SSD_N)]
            cg = xact_ref[:, pl.ds(D_INNER + (SSD_G + g) * SSD_N, SSD_N)]
            xg = xact_ref[:, pl.ds(g * GROUP_W, GROUP_W)]
            cols = [cs_s[:, pl.ds(h, 1)] for h in heads]
            lasts = [cs_s[pl.ds(SSD_Q - 1, 1), pl.ds(h, 1)] for h in heads]
            xdg = xg * _per_head(lane_masks, [dt_s[:, pl.ds(h, 1)] for h in heads])
            sg = state[g]
            gm = _dot_nt(cg, bg)
            y = (_per_head(lane_masks, [jnp.exp(c_) for c_ in cols]) * _dot_nt(cg, sg)
                 + _per_head(lane_masks, [dskip_ref[:, pl.ds(h, 1)] for h in heads]) * xg)
            mats = [gm * jnp.exp(jnp.where(causal, cols[k] - cst_s[pl.ds(h, 1), :], NEG)) for k, h in enumerate(heads)]
            y4 = _dot_nn(jnp.concatenate(mats, axis=0), xdg)
            y = y + _per_head(lane_masks, [y4[k * SSD_Q:(k + 1) * SSD_Q] for k in range(SSD_K)])
            y_ref[:, pl.ds(g * GROUP_W, GROUP_W)] = y
            w = _per_head(lane_masks, [jnp.exp(l_ - c_) for l_, c_ in zip(lasts, cols)])
            state[g] = _per_head(row_masks, [jnp.exp(l_) for l_ in lasts]) * sg + _dot_tn(w * xdg, bg)
        if nx:
            pl.when(jnp.logical_and(b == n_seq - 1, c == nc - 1))(finish)

    vec = pl.BlockSpec((1, LANES), lambda b, c: (0, 0))
    x_in, x_out, x_shape, x_sems = _exchange_specs(comm)
    return pl.pallas_call(
        body, name="ssd_fwd",
        grid=(n_seq, nc),
        in_specs=[pl.BlockSpec((SSD_Q, CONV_DIM), lambda b, c: (b * nc + c, 0)),
                  pl.BlockSpec((SSD_Q, LANES), lambda b, c: (b * nc + c, 0)), vec, vec, vec] + x_in,
        out_specs=[pl.BlockSpec((SSD_Q, D_INNER), lambda b, c: (b * nc + c, 0)),
                   pl.BlockSpec((1, SSD_G, GROUP_W, SSD_N), lambda b, c: (b * nc + c, 0, 0, 0))] + x_out,
        out_shape=[jax.ShapeDtypeStruct((rows, D_INNER), F32),
                   jax.ShapeDtypeStruct((n_seq * nc, SSD_G, GROUP_W, SSD_N), F32)] + x_shape,
        scratch_shapes=[pltpu.VMEM((SSD_G, GROUP_W, SSD_N), F32), pltpu.VMEM((SSD_Q, LANES), F32),
                        pltpu.VMEM((LANES, SSD_Q), F32), pltpu.VMEM((SSD_Q, LANES), F32)] + (x_sems if nx else []),
        compiler_params=_params(("arbitrary", "arbitrary")),
    )(xact, dtraw, dt_bias, a_log, d_skip, *[a for a, _ in comm])


def ssd_bwd(xact, dtraw, dt_bias, a_log, d_skip, sin, dy, n_seq, seq, comm=()):
    nc = seq // SSD_Q
    rows = n_seq * seq
    nx = len(comm)

    def body(*refs):
        xact_ref, dtraw_ref, bias_ref, alog_ref, dskip_ref, sin_ref, dy_ref = refs[:7]
        dx_ref, ddt_ref, dbias_ref, dalog_ref, ddskip_ref = refs[7 + nx:12 + nx]
        dstate, cs_s, cst_s, dt_s = refs[12 + 2 * nx:16 + 2 * nx]
        b, c = pl.program_id(0), pl.program_id(1)
        if nx:
            start, finish = _direct_exchange([m for _, m in comm], refs[7:7 + nx], refs[12 + nx:12 + 2 * nx],
                                             *refs[16 + 2 * nx:])
            pl.when(jnp.logical_and(b == 0, c == 0))(start)

        @pl.when(c == 0)
        def _():
            dstate[...] = jnp.zeros_like(dstate)

        pre = dtraw_ref[...] + bias_ref[...]
        dt = _softplus(pre)
        a_neg = -jnp.exp(alog_ref[...])
        cs = _dot_f32(_tri(True).astype(F32), dt * a_neg)
        cs_s[...] = cs
        cst_s[...] = cs.T
        dt_s[...] = dt
        causal, anti = _tri(True), _tri(False)
        is_last_row = lax.broadcasted_iota(jnp.int32, (SSD_Q, 1), 0) == SSD_Q - 1
        lane_masks, row_masks = _group_masks()
        dcs_cf = jnp.zeros((SSD_Q, LANES), F32)
        dcs_rf = jnp.zeros((LANES, SSD_Q), F32)
        ddt_cf = jnp.zeros((SSD_Q, LANES), F32)
        dd_vec = jnp.zeros((1, LANES), F32)
        dlast_vec = jnp.zeros((1, LANES), F32)
        for g in range(SSD_G):
            heads = [g * SSD_K + k for k in range(SSD_K)]
            bg = xact_ref[:, pl.ds(D_INNER + g * SSD_N, SSD_N)]
            cg = xact_ref[:, pl.ds(D_INNER + (SSD_G + g) * SSD_N, SSD_N)]
            xg = xact_ref[:, pl.ds(g * GROUP_W, GROUP_W)]
            dyg = dy_ref[:, pl.ds(g * GROUP_W, GROUP_W)]
            cols = [cs_s[:, pl.ds(h, 1)] for h in heads]
            rws = [cst_s[pl.ds(h, 1), :] for h in heads]
            lasts = [cs_s[pl.ds(SSD_Q - 1, 1), pl.ds(h, 1)] for h in heads]
            e_lasts = [jnp.exp(l_) for l_ in lasts]
            dtg = _per_head(lane_masks, [dt_s[:, pl.ds(h, 1)] for h in heads])
            dskg = _per_head(lane_masks, [dskip_ref[:, pl.ds(h, 1)] for h in heads])
            e_col = _per_head(lane_masks, [jnp.exp(c_) for c_ in cols])
            w = _per_head(lane_masks, [jnp.exp(l_ - c_) for l_, c_ in zip(lasts, cols)])
            xdg = xg * dtg
            sg = sin_ref[0, g]
            dsn = dstate[g]
            gm = _dot_nt(cg, bg)
            gmt = _dot_nt(bg, cg)
            y_off = e_col * _dot_nt(cg, sg)
            d_cs = e_col * dyg
            dcg = _dot_nn(d_cs, sg)
            dsp = _dot_tn(d_cs, cg) + _per_head(row_masks, e_lasts) * dsn
            dbg = _dot_nn(w * xdg, dsn)
            dtt = _dot_nt(bg, dsn)
            dxd = w * dtt
            dw = _headsum(dtt * xdg * w, g, lane_masks)
            dcs_cf = dcs_cf + _headsum(dyg * y_off, g, lane_masks) - dw
            dlast_vec = dlast_vec + jnp.sum(dw, axis=0, keepdims=True)
            dsn_s = dsn * sg
            segs = [cols[k] - rws[k] for k in range(SSD_K)]
            decays = [jnp.exp(jnp.where(causal, s_, NEG)) for s_ in segs]
            dm4 = _dot_nt(jnp.concatenate([jnp.where(m_, dyg, 0.0) for m_ in lane_masks], axis=0), xdg)
            z4 = _dot_nn(jnp.concatenate([gmt * jnp.exp(jnp.where(anti, -s_, NEG)) for s_ in segs], axis=0), dyg)
            dxd = dxd + _per_head(lane_masks, [z4[k * SSD_Q:(k + 1) * SSD_Q] for k in range(SSD_K)])
            dgm = jnp.zeros((SSD_Q, SSD_Q), F32)
            for k, h in enumerate(heads):
                dm = dm4[k * SSD_Q:(k + 1) * SSD_Q]
                dseg = dm * gm * decays[k]
                dgm = dgm + dm * decays[k]
                oh_r = _onehot_row(h)
                dcs_cf = dcs_cf + jnp.sum(dseg, axis=1, keepdims=True) * oh_r
                dcs_rf = dcs_rf - _onehot_col(h) * jnp.sum(dseg, axis=0, keepdims=True)
                dlast_vec = dlast_vec + jnp.sum(jnp.where(row_masks[k], dsn_s, 0.0), keepdims=True) * e_lasts[k] * oh_r
            dx_ref[:, pl.ds(g * GROUP_W, GROUP_W)] = dxd * dtg + dskg * dyg
            ddt_cf = ddt_cf + _headsum(dxd * xg, g, lane_masks)
            dd_vec = dd_vec + jnp.sum(_headsum(dyg * xg, g, lane_masks), axis=0, keepdims=True)
            dstate[g] = dsp
            dx_ref[:, pl.ds(D_INNER + g * SSD_N, SSD_N)] = dbg + _dot_tn(dgm, cg)
            dx_ref[:, pl.ds(D_INNER + (SSD_G + g) * SSD_N, SSD_N)] = dcg + _dot_nn(dgm, bg)
        dcs = dcs_cf + dcs_rf.T + jnp.where(is_last_row, dlast_vec, 0.0)
        da = _dot_f32(_tri(False).astype(F32), dcs)
        ddt = ddt_cf + da * a_neg
        ddtraw = ddt * _sigmoid(pre)
        ddt_ref[...] = ddtraw.astype(ddt_ref.dtype)
        dbias = jnp.sum(ddtraw, axis=0, keepdims=True)
        dalog = jnp.sum(da * dt, axis=0, keepdims=True) * a_neg
        first_step = jnp.logical_and(b == 0, c == 0)

        @pl.when(first_step)
        def _():
            dbias_ref[...] = dbias
            dalog_ref[...] = dalog
            ddskip_ref[...] = dd_vec

        @pl.when(jnp.logical_not(first_step))
        def _():
            dbias_ref[...] += dbias
            dalog_ref[...] += dalog
            ddskip_ref[...] += dd_vec

        if nx:
            pl.when(jnp.logical_and(b == n_seq - 1, c == nc - 1))(finish)

    def rowblk(b, c):
        return b * nc + (nc - 1 - c)

    vec = pl.BlockSpec((1, LANES), lambda b, c: (0, 0))
    x_in, x_out, x_shape, x_sems = _exchange_specs(comm)
    return pl.pallas_call(
        body, name="ssd_bwd",
        grid=(n_seq, nc),
        in_specs=[pl.BlockSpec((SSD_Q, CONV_DIM), lambda b, c: (rowblk(b, c), 0)),
                  pl.BlockSpec((SSD_Q, LANES), lambda b, c: (rowblk(b, c), 0)), vec, vec, vec,
                  pl.BlockSpec((1, SSD_G, GROUP_W, SSD_N), lambda b, c: (rowblk(b, c), 0, 0, 0)),
                  pl.BlockSpec((SSD_Q, D_INNER), lambda b, c: (rowblk(b, c), 0))] + x_in,
        out_specs=[pl.BlockSpec((SSD_Q, CONV_DIM), lambda b, c: (rowblk(b, c), 0)),
                   pl.BlockSpec((SSD_Q, LANES), lambda b, c: (rowblk(b, c), 0)), vec, vec, vec] + x_out,
        out_shape=[jax.ShapeDtypeStruct((rows, CONV_DIM), F32), jax.ShapeDtypeStruct((rows, LANES), BF16),
                   jax.ShapeDtypeStruct((1, LANES), F32), jax.ShapeDtypeStruct((1, LANES), F32),
                   jax.ShapeDtypeStruct((1, LANES), F32)] + x_shape,
        scratch_shapes=[pltpu.VMEM((SSD_G, GROUP_W, SSD_N), F32), pltpu.VMEM((SSD_Q, LANES), F32),
                        pltpu.VMEM((LANES, SSD_Q), F32), pltpu.VMEM((SSD_Q, LANES), F32)] + (x_sems if nx else []),
        compiler_params=_params(("arbitrary", "arbitrary")),
    )(xact, dtraw, dt_bias, a_log, d_skip, sin, dy, *[a for a, _ in comm])


QKV_W = 3 * ATT_OUT
PAIR_W = 2 * ATT_HD
HEAD_PAIRS = ATT_H // 2
PREP_ROWS = 512


def _dilated(a, n_seq, seq, dil):
    return a.reshape(n_seq * (seq // dil), dil * a.shape[1])


def _head_sums(x, fn):
    lo = jnp.logical_not(lax.broadcasted_iota(jnp.int32, (1, 2 * ATT_HD), 1) >= ATT_HD)
    parts = []
    for p in range(ATT_H // 2):
        slab = x[:, p * 2 * ATT_HD:(p + 1) * 2 * ATT_HD]
        s_lo = fn(jnp.sum(jnp.where(lo, slab, 0.0), axis=1, keepdims=True))
        s_hi = fn(jnp.sum(jnp.where(lo, 0.0, slab), axis=1, keepdims=True))
        parts.append(jnp.where(lo, s_lo, s_hi))
    return jnp.concatenate(parts, axis=1)


def _head_rstd(x):
    return _head_sums(x * x, lambda s: lax.rsqrt(s * (1.0 / ATT_HD) + EPS))


def _head_rms_bwd(x, g_t, dy):
    r = _head_rstd(x)
    xh = x * r
    dyg = dy * g_t
    mean = _head_sums(dyg * xh, lambda s: s * (1.0 / ATT_HD))
    return r * (dyg - xh * mean), jnp.sum(dy * xh, axis=0, keepdims=True)


def qk_prep(qkv, gq_t, gk_t, rows, name):
    tb = min(PREP_ROWS, rows)

    def body(x_ref, gq_ref, gk_ref, o_ref):
        q = x_ref[:, pl.ds(0, ATT_OUT)]
        k = x_ref[:, pl.ds(ATT_OUT, ATT_OUT)]
        o_ref[:, pl.ds(0, ATT_OUT)] = (q * _head_rstd(q) * (gq_ref[...] * ATT_SCALE)).astype(o_ref.dtype)
        o_ref[:, pl.ds(ATT_OUT, ATT_OUT)] = (k * _head_rstd(k) * gk_ref[...]).astype(o_ref.dtype)
        o_ref[:, pl.ds(2 * ATT_OUT, ATT_OUT)] = x_ref[:, pl.ds(2 * ATT_OUT, ATT_OUT)].astype(o_ref.dtype)

    gspec = pl.BlockSpec((1, ATT_OUT), lambda i: (0, 0))
    blk = pl.BlockSpec((tb, QKV_W), lambda i: (i, 0))
    return pl.pallas_call(
        body, name=name,
        grid=(rows // tb,),
        in_specs=[blk, gspec, gspec],
        out_specs=blk,
        out_shape=jax.ShapeDtypeStruct((rows, QKV_W), MXU),
        compiler_params=_params(("parallel",)),
    )(qkv, gq_t, gk_t)


def _lane_hi():
    return lax.broadcasted_iota(jnp.int32, (1, PAIR_W), 1) >= ATT_HD


def _band_mask2(first_valid, query_rows):
    i = lax.broadcasted_iota(jnp.int32, (ATT_BLK, 2 * ATT_BLK), 0)
    j = lax.broadcasted_iota(jnp.int32, (ATT_BLK, 2 * ATT_BLK), 1)
    left = j < ATT_BLK
    right = jnp.logical_not(left)
    if query_rows:
        return jnp.logical_or(jnp.logical_and(jnp.logical_and(left, i <= j), first_valid),
                              jnp.logical_and(right, i >= j - ATT_BLK))
    return jnp.logical_or(jnp.logical_and(left, j >= i),
                          jnp.logical_and(jnp.logical_and(right, j - ATT_BLK <= i), first_valid))


def _only_head(slab, hi):
    keep = _lane_hi() if hi else jnp.logical_not(_lane_hi())
    return jnp.where(keep, slab, jnp.zeros_like(slab))


def attn_fwd2(nq, n_seq, seq, dil, name):
    length = seq // dil
    nb = length // ATT_BLK

    def body(cur_ref, prev_ref, o_ref, lse_ref, s_scr, p_scr):
        n = pl.program_id(2)
        mask = _band_mask2(n > 0, True)
        for h in range(ATT_H):
            sl = pl.ds((h // 2) * PAIR_W, PAIR_W)
            ks = pl.ds(ATT_OUT + (h // 2) * PAIR_W, PAIR_W)
            kcat = jnp.concatenate([prev_ref[:, ks], cur_ref[:, ks]], axis=0)
            s_scr[h] = jnp.where(mask, _dot_nt(_only_head(cur_ref[:, sl], h % 2), kcat), NEG)
        s_all = s_scr[...]
        mx = jnp.max(s_all, axis=2, keepdims=True)
        p_all = jnp.exp(s_all - mx)
        den = jnp.sum(p_all, axis=2, keepdims=True)
        p_scr[...] = p_all.astype(p_scr.dtype)
        inv = 1.0 / den
        lse = mx + jnp.log(den)
        lse_blk = jnp.zeros((ATT_BLK, LANES), F32)
        for h in range(ATT_H):
            lse_blk = lse_blk + lse[h] * _onehot_row(h)
        lse_ref[...] = lse_blk
        for pr in range(HEAD_PAIRS):
            vs = pl.ds(2 * ATT_OUT + pr * PAIR_W, PAIR_W)
            vcat = jnp.concatenate([prev_ref[:, vs], cur_ref[:, vs]], axis=0)
            lo = _dot_nn(p_scr[2 * pr], vcat) * inv[2 * pr]
            hi = _dot_nn(p_scr[2 * pr + 1], vcat) * inv[2 * pr + 1]
            o_ref[:, pl.ds(pr * PAIR_W, PAIR_W)] = jnp.where(_lane_hi(), hi, lo)

    return pl.pallas_call(
        body, name=name,
        grid=(n_seq, dil, nb),
        in_specs=[pl.BlockSpec((ATT_BLK, QKV_W), lambda b, r, n: (b * nb + n, r)),
                  pl.BlockSpec((ATT_BLK, QKV_W), lambda b, r, n: (b * nb + jnp.maximum(n - 1, 0), r))],
        out_specs=[pl.BlockSpec((ATT_BLK, ATT_OUT), lambda b, r, n: (b * nb + n, r)),
                   pl.BlockSpec((ATT_BLK, LANES), lambda b, r, n: (b * nb + n, r))],
        out_shape=[jax.ShapeDtypeStruct((n_seq * length, dil * ATT_OUT), F32),
                   jax.ShapeDtypeStruct((n_seq * length, dil * LANES), F32)],
        scratch_shapes=[pltpu.VMEM((ATT_H, ATT_BLK, 2 * ATT_BLK), F32), pltpu.VMEM((ATT_H, ATT_BLK, 2 * ATT_BLK), MXU)],
        compiler_params=_params(("parallel", "parallel", "arbitrary")),
    )(nq, nq)


def attn_bwd2(nq, do, lse, wts, rsum, n_seq, seq, dil, name):
    length = seq // dil
    nb = length // ATT_BLK

    def body(prev_ref, cur_ref, nxt_ref, do_c, do_x, lse_c, lse_x, wt_c, wt_x, rs_c, rs_x, dn_ref):
        n = pl.program_id(2)
        mask_q = _band_mask2(n > 0, True)
        mask_k = _band_mask2(n < nb - 1, False)
        wc, wx = wt_c[...], wt_x[...]
        lse_t = jnp.concatenate([lse_c[...].T, lse_x[...].T], axis=1)
        dl_t = jnp.concatenate([(-wc * rs_c[...]).T, (-wx * rs_x[...]).T], axis=1)
        for pr in range(HEAD_PAIRS):
            sl = pl.ds(pr * PAIR_W, PAIR_W)
            ks = pl.ds(ATT_OUT + pr * PAIR_W, PAIR_W)
            vs = pl.ds(2 * ATT_OUT + pr * PAIR_W, PAIR_W)
            he, ho = pl.ds(2 * pr, 1), pl.ds(2 * pr + 1, 1)
            q_c, k_c, v_c = cur_ref[:, sl], cur_ref[:, ks], cur_ref[:, vs]
            qcat = jnp.concatenate([q_c, nxt_ref[:, sl]], axis=0)
            kcat = jnp.concatenate([prev_ref[:, ks], k_c], axis=0)
            vcat = jnp.concatenate([prev_ref[:, vs], v_c], axis=0)
            dog_c = do_c[:, sl] * jnp.where(_lane_hi(), wt_c[:, ho], wt_c[:, he])
            dog_x = do_x[:, sl] * jnp.where(_lane_hi(), wt_x[:, ho], wt_x[:, he])
            dog = jnp.concatenate([dog_c, dog_x], axis=0).astype(MXU)
            res = []
            for hi in (0, 1):
                h = 2 * pr + hi
                one = pl.ds(h, 1)
                dl_col = -wt_c[:, one] * rs_c[:, one]
                p_q = jnp.exp(jnp.where(mask_q, _dot_nt(_only_head(q_c, hi), kcat) - lse_c[:, one], NEG))
                ds_q = p_q * (_dot_nt(_only_head(dog[:ATT_BLK], hi), vcat) + dl_col)
                dq = _dot_nn(ds_q, kcat)
                p_t = jnp.exp(jnp.where(mask_k, _dot_nt(_only_head(k_c, hi), qcat) - lse_t[h:h + 1, :], NEG))
                ds_t = p_t * (_dot_nt(_only_head(v_c, hi), dog) + dl_t[h:h + 1, :])
                res.append((dq, _dot_nn(ds_t, qcat), _dot_nn(p_t, dog)))
            for t, dst in enumerate((sl, ks, vs)):
                dn_ref[:, dst] = jnp.where(_lane_hi(), res[1][t], res[0][t])

    def at(shift, width):
        if shift < 0:
            return pl.BlockSpec((ATT_BLK, width), lambda b, r, n: (b * nb + jnp.maximum(n - 1, 0), r))
        if shift > 0:
            return pl.BlockSpec((ATT_BLK, width), lambda b, r, n: (b * nb + jnp.minimum(n + 1, nb - 1), r))
        return pl.BlockSpec((ATT_BLK, width), lambda b, r, n: (b * nb + n, r))

    return pl.pallas_call(
        body, name=name,
        grid=(n_seq, dil, nb),
        in_specs=[at(-1, QKV_W), at(0, QKV_W), at(1, QKV_W), at(0, ATT_OUT), at(1, ATT_OUT),
                  at(0, LANES), at(1, LANES), at(0, LANES), at(1, LANES), at(0, LANES), at(1, LANES)],
        out_specs=at(0, QKV_W),
        out_shape=jax.ShapeDtypeStruct((n_seq * length, dil * QKV_W), F32),
        compiler_params=_params(("parallel", "parallel", "arbitrary")),
    )(nq, nq, nq, do, do, lse, lse, wts, wts, rsum, rsum)


def qk_post(qkv, dn, gq_t, gk_t, rows, name):
    tb = min(PREP_ROWS, rows)

    def body(x_ref, dn_ref, gq_ref, gk_ref, o_ref, dgq_ref, dgk_ref):
        i = pl.program_id(0)
        qs, ks, vs = pl.ds(0, ATT_OUT), pl.ds(ATT_OUT, ATT_OUT), pl.ds(2 * ATT_OUT, ATT_OUT)
        dq, dgq = _head_rms_bwd(x_ref[:, qs], gq_ref[...], dn_ref[:, qs] * ATT_SCALE)
        dk, dgk = _head_rms_bwd(x_ref[:, ks], gk_ref[...], dn_ref[:, ks])
        o_ref[:, qs] = dq.astype(o_ref.dtype)
        o_ref[:, ks] = dk.astype(o_ref.dtype)
        o_ref[:, vs] = dn_ref[:, vs].astype(o_ref.dtype)

        @pl.when(i == 0)
        def _():
            dgq_ref[...] = dgq
            dgk_ref[...] = dgk

        @pl.when(i != 0)
        def _():
            dgq_ref[...] += dgq
            dgk_ref[...] += dgk

    gspec = pl.BlockSpec((1, ATT_OUT), lambda i: (0, 0))
    blk = pl.BlockSpec((tb, QKV_W), lambda i: (i, 0))
    return pl.pallas_call(
        body, name=name,
        grid=(rows // tb,),
        in_specs=[blk, blk, gspec, gspec],
        out_specs=[blk, gspec, gspec],
        out_shape=[jax.ShapeDtypeStruct((rows, QKV_W), MXU), jax.ShapeDtypeStruct((1, ATT_OUT), F32),
                   jax.ShapeDtypeStruct((1, ATT_OUT), F32)],
        compiler_params=_params(("arbitrary",)),
    )(qkv, dn, gq_t, gk_t)


OTHER_CHIPS = (4, 2, 6)


def exchange(items, name):
    n = len(items)

    def body(*refs):
        in_refs, out_refs = refs[:n], refs[n:2 * n]
        send_sems, recv_sems, local_sems = refs[2 * n:]
        x, y, c = lax.axis_index("x"), lax.axis_index("y"), lax.axis_index("c")
        me = 4 * x + 2 * y + c

        def peer(k):
            px = 1 - x if k & 4 else x
            py = 1 - y if k & 2 else y
            pc = 1 - c if k & 1 else c
            return (px, py, pc), 4 * px + 2 * py + pc

        def remote(t, k):
            dev, pid = peer(k)
            src = in_refs[t] if items[t][1] == "gather" else in_refs[t].at[pid]
            return pltpu.make_async_remote_copy(
                src_ref=src, dst_ref=out_refs[t].at[me], send_sem=send_sems.at[t, k], recv_sem=recv_sems.at[t, k],
                device_id=dev, device_id_type=pl.DeviceIdType.MESH)

        def arrival(t, k):
            dev, pid = peer(k)
            src = in_refs[t] if items[t][1] == "gather" else in_refs[t].at[pid]
            return pltpu.make_async_remote_copy(
                src_ref=src, dst_ref=out_refs[t].at[pid], send_sem=send_sems.at[t, k], recv_sem=recv_sems.at[t, k],
                device_id=dev, device_id_type=pl.DeviceIdType.MESH)

        def own(t):
            src = in_refs[t] if items[t][1] == "gather" else in_refs[t].at[me]
            return pltpu.make_async_copy(src, out_refs[t].at[me], local_sems.at[t])

        def forward(t, k, from_sibling):
            sib, _ = peer(1)
            _, pid = peer(k + 1 if from_sibling else k)
            slot = out_refs[t].at[pid]
            return pltpu.make_async_remote_copy(
                src_ref=slot, dst_ref=slot, send_sem=send_sems.at[t, k + 1], recv_sem=recv_sems.at[t, k + 1],
                device_id=sib, device_id_type=pl.DeviceIdType.MESH)

        def direct(t):
            return (1,) + OTHER_CHIPS if items[t][1] == "gather" else tuple(range(1, N_DEV))

        for t in range(n):
            own(t).start()
            for k in direct(t):
                remote(t, k).start()
        for t in range(n):
            if items[t][1] == "gather":
                for k in OTHER_CHIPS:
                    arrival(t, k).wait_recv()
                    forward(t, k, False).start()
        for t in range(n):
            if items[t][1] == "gather":
                arrival(t, 1).wait_recv()
                for k in OTHER_CHIPS:
                    forward(t, k, True).wait_recv()
            else:
                for k in direct(t):
                    arrival(t, k).wait_recv()
        for t in range(n):
            for k in direct(t):
                remote(t, k).wait_send()
            if items[t][1] == "gather":
                for k in OTHER_CHIPS:
                    forward(t, k, False).wait_send()
            own(t).wait()

    out_shape = []
    for arr, mode in items:
        shp = arr.shape if mode == "gather" else arr.shape[1:]
        out_shape.append(jax.ShapeDtypeStruct((N_DEV,) + tuple(shp), arr.dtype))
    anyspec = pl.BlockSpec(memory_space=pl.ANY)
    return pl.pallas_call(
        body, name=name,
        in_specs=[anyspec] * n,
        out_specs=[anyspec] * n,
        out_shape=out_shape,
        scratch_shapes=[pltpu.SemaphoreType.DMA((n, N_DEV)), pltpu.SemaphoreType.DMA((n, N_DEV)),
                        pltpu.SemaphoreType.DMA((n,))],
    )(*[a for a, _ in items])


N_CHIPS = N_DEV // 2


def sibling_swap(arrays, name):
    n = len(arrays)

    def body(*refs):
        in_refs, out_refs, send_sems, recv_sems = refs[:n], refs[n:2 * n], refs[2 * n], refs[2 * n + 1]
        x, y, c = lax.axis_index("x"), lax.axis_index("y"), lax.axis_index("c")

        def copy(t, q):
            return pltpu.make_async_remote_copy(
                src_ref=in_refs[t].at[2 * q + (1 - c)], dst_ref=out_refs[t].at[q],
                send_sem=send_sems.at[t, q], recv_sem=recv_sems.at[t, q],
                device_id=(x, y, 1 - c), device_id_type=pl.DeviceIdType.MESH)

        for t in range(n):
            for q in range(N_CHIPS):
                copy(t, q).start()
        for t in range(n):
            for q in range(N_CHIPS):
                copy(t, q).wait_recv()
        for t in range(n):
            for q in range(N_CHIPS):
                copy(t, q).wait_send()

    anyspec = pl.BlockSpec(memory_space=pl.ANY)
    return pl.pallas_call(
        body, name=name,
        in_specs=[anyspec] * n,
        out_specs=[anyspec] * n,
        out_shape=[jax.ShapeDtypeStruct((N_CHIPS,) + a.shape[1:], a.dtype) for a in arrays],
        scratch_shapes=[pltpu.SemaphoreType.DMA((n, N_CHIPS)), pltpu.SemaphoreType.DMA((n, N_CHIPS))],
    )(*arrays)


def chip_exchange(arrays, name):
    n = len(arrays)

    def body(*refs):
        in_refs, out_refs = refs[:n], refs[n:2 * n]
        send_sems, recv_sems, local_sems = refs[2 * n:]
        x, y, c = lax.axis_index("x"), lax.axis_index("y"), lax.axis_index("c")
        mine = 2 * x + y

        def peer(k):
            px = 1 - x if k & 4 else x
            py = 1 - y if k & 2 else y
            return (px, py, c), 2 * px + py

        def remote(t, k, arriving):
            dev, q = peer(k)
            return pltpu.make_async_remote_copy(
                src_ref=in_refs[t].at[q], dst_ref=out_refs[t].at[q if arriving else mine],
                send_sem=send_sems.at[t, k], recv_sem=recv_sems.at[t, k],
                device_id=dev, device_id_type=pl.DeviceIdType.MESH)

        def own(t):
            return pltpu.make_async_copy(in_refs[t].at[mine], out_refs[t].at[mine], local_sems.at[t])

        for t in range(n):
            own(t).start()
            for k in OTHER_CHIPS:
                remote(t, k, False).start()
        for t in range(n):
            for k in OTHER_CHIPS:
                remote(t, k, True).wait_recv()
        for t in range(n):
            for k in OTHER_CHIPS:
                remote(t, k, False).wait_send()
            own(t).wait()

    anyspec = pl.BlockSpec(memory_space=pl.ANY)
    return pl.pallas_call(
        body, name=name,
        in_specs=[anyspec] * n,
        out_specs=[anyspec] * n,
        out_shape=[jax.ShapeDtypeStruct(a.shape, a.dtype) for a in arrays],
        scratch_shapes=[pltpu.SemaphoreType.DMA((n, N_DEV)), pltpu.SemaphoreType.DMA((n, N_DEV)),
                        pltpu.SemaphoreType.DMA((n,))],
    )(*arrays)


def pair_add(a, b, name):
    _, r, c = a.shape
    rb = r if r <= 512 else (128 if c > 1024 else 256)
    assert r % rb == 0

    def body(a_ref, b_ref, o_ref):
        o_ref[...] = (a_ref[...].astype(F32) + b_ref[...].astype(F32)).astype(o_ref.dtype)

    blk = pl.BlockSpec((1, rb, c), lambda q, i: (q, i, 0))
    return pl.pallas_call(
        body, name=name,
        grid=(N_CHIPS, r // rb),
        in_specs=[blk, blk],
        out_specs=blk,
        out_shape=jax.ShapeDtypeStruct(a.shape, a.dtype),
        compiler_params=_params(("parallel", "parallel")),
    )(a, b)


def adamw(parts, w, m, v, name):
    r, c = w.shape
    n_parts = parts.shape[0]
    rb = r if r <= 512 else (128 if c > 1024 else 256)
    assert r % rb == 0

    def body(p_ref, w_ref, m_ref, v_ref, g_out, d_out, m_out, v_out):
        g = p_ref[0].astype(F32)
        for i in range(1, n_parts):
            g = g + p_ref[i].astype(F32)
        m_new = ADAM_B1 * m_ref[...] + (1.0 - ADAM_B1) * g
        v_new = ADAM_B2 * v_ref[...] + (1.0 - ADAM_B2) * (g * g)
        m_hat = m_new / (1.0 - ADAM_B1 ** ADAM_STEP)
        v_hat = v_new / (1.0 - ADAM_B2 ** ADAM_STEP)
        g_out[...] = g
        d_out[...] = -ADAM_LR * (m_hat / (jnp.sqrt(v_hat) + ADAM_EPS) + ADAM_WD * w_ref[...])
        m_out[...] = m_new
        v_out[...] = v_new

    blk = pl.BlockSpec((rb, c), lambda i: (i, 0))
    return pl.pallas_call(
        body, name=name,
        grid=(r // rb,),
        in_specs=[pl.BlockSpec((n_parts, rb, c), lambda i: (0, i, 0)), blk, blk, blk],
        out_specs=[blk] * 4,
        out_shape=[jax.ShapeDtypeStruct((r, c), F32)] * 4,
        compiler_params=_params(("parallel",)),
    )(parts, w, m, v)


def _pad_lanes(vec, n=LANES):
    return jnp.pad(vec, ((0, 0), (0, n - vec.shape[1])))


COL_SHARDED = ("w_in", "ssd_conv_w", "w_attn_proj", "w_up", "ffn_conv_w")
ROW_SHARDED = ("w_ssd_proj", "w_out", "w_down")
MATRICES = ("w_in", "w_attn_proj", "w_up", "w_ssd_proj", "w_out", "w_down")
LATE = ("w_ssd_proj", "w_attn_proj", "w_out", "w_up", "ffn_conv_w", "w_down")


def _narrow(name, a):
    return a.astype(MXU) if name in MATRICES else a


def _from_gathered(name, g):
    if name in COL_SHARDED:
        return jnp.transpose(g, (1, 0, 2)).reshape(g.shape[1], N_DEV * g.shape[2])
    return g.reshape(N_DEV * g.shape[1], g.shape[2])


def _to_slabs(name, g):
    if name in COL_SHARDED:
        return jnp.transpose(g.reshape(g.shape[0], N_DEV, g.shape[1] // N_DEV), (1, 0, 2))
    return g.reshape(N_DEV, g.shape[0] // N_DEV, g.shape[1])


def local_step(x, target, w, late=None):
    n_seq, seq, _ = x.shape
    rows = n_seq * seq
    x = x.reshape(rows, D_MODEL)
    target = target.reshape(rows, D_MODEL)
    mx = lambda a: a.astype(MXU)

    splits = [sum(IN_WIDTHS[:i]) for i in range(len(IN_WIDTHS) + 1)]
    w_in = w["w_in"]
    part = lambda i: w_in[:, splits[i]:splits[i + 1]]
    w_z, w_xbc, w_gs, w_ga = mx(part(0)), mx(part(1)), mx(part(6)), mx(part(7))
    w_dt = mx(_pad_lanes(part(2)))
    w_qkv = [mx(jnp.concatenate([part(3 + t)[:, g * ATT_OUT:(g + 1) * ATT_OUT] for t in range(3)], axis=1))
             for g in range(ATT_GROUPS)]
    conv_w, conv_b, fconv_b = w["ssd_conv_w"], w["ssd_conv_b"], w["ffn_conv_b"]
    dt_bias, a_log, d_skip = _pad_lanes(w["dt_bias"]), _pad_lanes(w["a_log"]), _pad_lanes(w["d_skip"])
    g1, g2, gn, gq, gk = w["norm1_g"], w["norm2_g"], w["ssd_norm_g"], w["q_norm_g"], w["k_norm_g"]

    tb = min(512, seq)
    tbm = min(256, seq)
    cw = 1024
    rw = lambda fn, name, ncol, ins, params=(), outs=(), accs=(), tb_=tb: rowwise(
        fn, name, rows, seq, tb_, ncol, ins, params, outs, accs)

    (h,) = rw(lambda ctx, xv, g: _rms_fwd(xv, g), "rms1_fwd", 1, [(x, D_MODEL, 0, None)], [(g1, None, 0)],
              [(D_MODEL, D_MODEL, 0, MXU)])
    z = matmul(h, w_z, "mm_z")
    xbc = matmul(h, w_xbc, "mm_xbc")
    dtraw = matmul(h, w_dt, "mm_dt")
    qkv = [matmul(h, w_qkv[g], f"mm_qkv{g}") for g in range(ATT_GROUPS)]
    gs = matmul(h, w_gs, "mm_gs")
    ga = matmul(h, w_ga, "mm_ga")

    def conv_silu(ctx, xh, wv, bv):
        return _silu(bv + _conv_prev(xh[0], xh[1], wv, ctx.first, SSD_CONV))

    (xact,) = rw(conv_silu, "ssd_conv_fwd", CONV_DIM // cw, [(xbc, cw, 0, "prev")],
                 [(conv_w, cw, 0), (conv_b, cw, 0)], [(CONV_DIM, cw, 0, F32)])
    if late is None:
        y, sin = ssd_fwd(xact, dtraw, dt_bias, a_log, d_skip, n_seq, seq)
    else:
        y, sin, *gathered = ssd_fwd(xact, dtraw, dt_bias, a_log, d_skip, n_seq, seq,
                                    comm=[(late[n], "gather") for n in LATE])
        w = {**w, **{n: _from_gathered(n, g) for n, g in zip(LATE, gathered)}}
    w_sp, w_ap, w_o, w_d = mx(w["w_ssd_proj"]), mx(w["w_attn_proj"]), mx(w["w_out"]), mx(w["w_down"])
    w_ug, w_uv = mx(w["w_up"][:, :D_FF]), mx(w["w_up"][:, D_FF:])
    fconv_w = w["ffn_conv_w"]

    def gated_norm(ctx, yv, zv, g):
        yz = yv * _silu(zv)
        return jnp.concatenate([_rms_fwd(yz[:, i:i + NORM_GROUP], g[:, i:i + NORM_GROUP])
                                for i in range(0, cw, NORM_GROUP)], axis=1)

    (y_ssd,) = rw(gated_norm, "ssd_post_fwd", D_INNER // cw, [(y, cw, 0, None), (z, cw, 0, None)], [(gn, cw, 0)],
                  [(D_INNER, cw, 0, MXU)])

    gq_t, gk_t = jnp.tile(gq, (1, ATT_H)), jnp.tile(gk, (1, ATT_H))
    dilated = lambda a, g: _dilated(a, n_seq, seq, ATT_DILATIONS[g])
    nq = [dilated(qk_prep(qkv[g], gq_t, gk_t, rows, f"qk_prep{g}"), g) for g in range(ATT_GROUPS)]
    att = [attn_fwd2(nq[g], n_seq, seq, ATT_DILATIONS[g], f"attn_fwd{g}") for g in range(ATT_GROUPS)]

    def combine(ctx, o0, o1, o2, l0, l1, l2):
        mxl = jnp.maximum(jnp.maximum(l0, l1), l2)
        e = [jnp.exp(l - mxl) for l in (l0, l1, l2)]
        inv = 1.0 / (e[0] + e[1] + e[2])
        ws = [ei * inv for ei in e]
        out = sum(_expand_heads(wi) * oi for wi, oi in zip(ws, (o0, o1, o2)))
        return (out, *ws)

    y_attn, wt0, wt1, wt2 = rw(
        combine, "attn_combine", 1,
        [(att[g][0].reshape(rows, ATT_OUT), ATT_OUT, 0, None) for g in range(3)]
        + [(att[g][1].reshape(rows, LANES), LANES, 0, None) for g in range(3)], [],
        [(ATT_OUT, ATT_OUT, 0, F32)] + [(LANES, LANES, 0, F32)] * 3)
    wts = (wt0, wt1, wt2)

    ps = matmul(y_ssd, w_sp, "mm_ssd_proj")
    pa = matmul(y_attn, w_ap, "mm_attn_proj")
    (merged,) = rw(lambda ctx, a, b, c, d: _sigmoid(c) * a + _sigmoid(d) * b, "merge_fwd", D_MODEL // cw,
                   [(ps, cw, 0, None), (pa, cw, 0, None), (gs, cw, 0, None), (ga, cw, 0, None)], [],
                   [(D_MODEL, cw, 0, MXU)])
    x1 = matmul(merged, w_o, "mm_out", add=x)
    (h2,) = rw(lambda ctx, xv, g: _rms_fwd(xv, g), "rms2_fwd", 1, [(x1, D_MODEL, 0, None)], [(g2, None, 0)],
               [(D_MODEL, D_MODEL, 0, MXU)])
    up_g = matmul(h2, w_ug, "mm_up_g")
    up_v = matmul(h2, w_uv, "mm_up_v")
    fw = D_FF // 2
    nfc = D_FF // fw

    def mlp_act(ctx, ug, uv, wg, wv, bg, bv):
        cg = bg + _conv_prev(ug[0], ug[1], wg, ctx.first, FFN_CONV)
        cv = bv + _conv_prev(uv[0], uv[1], wv, ctx.first, FFN_CONV)
        return _silu(cg) * cv

    (act,) = rw(mlp_act, "mlp_act_fwd", nfc, [(up_g, fw, 0, "prev"), (up_v, fw, 0, "prev")],
                [(fconv_w, fw, 0), (fconv_w, fw, nfc), (fconv_b, fw, 0), (fconv_b, fw, nfc)], [(D_FF, fw, 0, MXU)],
                tb_=tbm)
    x2 = matmul(act, w_d, "mm_down", add=x1)

    def loss_fn(ctx, xv, tv):
        d = xv - tv
        g = d * (1.0 / D_MODEL)
        return g, g, jnp.sum(d * d, axis=0, keepdims=True)

    dx2, dx2_m, sq = rw(loss_fn, "loss", 1, [(x2, D_MODEL, 0, None), (target, D_MODEL, 0, None)], [],
                        [(D_MODEL, D_MODEL, 0, F32), (D_MODEL, D_MODEL, 0, MXU)], [(1, D_MODEL)])

    grads = {}
    dact = matmul(dx2_m, w_d, "mm_d_act", tb=True)
    grads["w_down"] = matmul(act, dx2_m, "mm_dw_down", ta=True, out_dtype=MXU)

    def mlp_bwd(ctx, da, ug, uv, wg, wv, bg, bv):
        cg, cg_n = _conv_pre(ug, wg, bg, ctx.first, FFN_CONV)
        cv, cv_n = _conv_pre(uv, wv, bv, ctx.first, FFN_CONV)
        da_c, da_n = da
        dup_g_, dwg, dbg = _conv_bwd(da_c * cv * _silu_grad(cg), da_n * cv_n * _silu_grad(cg_n), ug, wg, ctx, FFN_CONV)
        dup_v_, dwv, dbv = _conv_bwd(da_c * _silu(cg), da_n * _silu(cg_n), uv, wv, ctx, FFN_CONV)
        return dup_g_, dup_v_, dwg, dbg, dwv, dbv

    dup_g, dup_v, dfw_g, dfb_g, dfw_v, dfb_v = rw(
        mlp_bwd, "mlp_bwd", nfc, [(dact, fw, 0, "next"), (up_g, fw, 0, "both"), (up_v, fw, 0, "both")],
        [(fconv_w, fw, 0), (fconv_w, fw, nfc), (fconv_b, fw, 0), (fconv_b, fw, nfc)],
        [(D_FF, fw, 0, MXU), (D_FF, fw, 0, MXU)], [(FFN_CONV, fw), (1, fw), (FFN_CONV, fw), (1, fw)], tb_=tbm)
    grads["ffn_conv_w"] = jnp.concatenate([dfw_g, dfw_v], axis=1)
    grads["ffn_conv_b"] = jnp.concatenate([dfb_g, dfb_v], axis=1)
    dh2 = matmul(dup_g, w_ug, "mm_dh2_g", tb=True)
    dh2 = matmul(dup_v, w_uv, "mm_dh2_v", tb=True, add=dh2)
    grads["w_up"] = jnp.concatenate([matmul(h2, dup_g, "mm_dw_up_g", ta=True, out_dtype=MXU),
                                     matmul(h2, dup_v, "mm_dw_up_v", ta=True, out_dtype=MXU)], axis=1)

    def rms_bwd_fn(ctx, xv, dh_, dres, g):
        dxv, dg = _rms_bwd(xv, g, dh_)
        return dres + dxv, dg

    def rms_bwd_fn2(ctx, xv, dh_, dres, g):
        dxv, dg = rms_bwd_fn(ctx, xv, dh_, dres, g)
        return dxv, dxv, dg

    dx1, dx1_m, grads["norm2_g"] = rw(
        rms_bwd_fn2, "rms2_bwd", 1, [(x1, D_MODEL, 0, None), (dh2, D_MODEL, 0, None), (dx2, D_MODEL, 0, None)],
        [(g2, None, 0)], [(D_MODEL, D_MODEL, 0, F32), (D_MODEL, D_MODEL, 0, MXU)], [(1, D_MODEL)])

    dmerged = matmul(dx1_m, w_o, "mm_d_merged", tb=True)
    grads["w_out"] = matmul(merged, dx1_m, "mm_dw_out", ta=True, out_dtype=MXU)

    def merge_bwd(ctx, dm, a, b, c, d):
        sc, sd = _sigmoid(c), _sigmoid(d)
        return dm * sc, dm * sd, dm * a * sc * (1.0 - sc), dm * b * sd * (1.0 - sd)

    dps, dpa, dgs, dga = rw(merge_bwd, "merge_bwd", D_MODEL // cw,
                            [(dmerged, cw, 0, None), (ps, cw, 0, None), (pa, cw, 0, None), (gs, cw, 0, None),
                             (ga, cw, 0, None)], [], [(D_MODEL, cw, 0, MXU)] * 4)
    dy_ssd = matmul(dps, w_sp, "mm_d_y_ssd", tb=True)
    grads["w_ssd_proj"] = matmul(y_ssd, dps, "mm_dw_ssd_proj", ta=True, out_dtype=MXU)
    dy_attn = matmul(dpa, w_ap, "mm_d_y_attn", tb=True)
    grads["w_attn_proj"] = matmul(y_attn, dpa, "mm_dw_attn_proj", ta=True, out_dtype=MXU)

    (rsum,) = rw(lambda ctx, a, b: _reduce_heads(a * b), "attn_rsum", 1,
                 [(dy_attn, ATT_OUT, 0, None), (y_attn, ATT_OUT, 0, None)], [], [(LANES, LANES, 0, F32)])
    dqkv, dgq, dgk = [], 0.0, 0.0
    for g in range(ATT_GROUPS):
        dn = attn_bwd2(nq[g], dilated(dy_attn, g), att[g][1], dilated(wts[g], g), dilated(rsum, g), n_seq, seq,
                       ATT_DILATIONS[g], f"attn_bwd{g}")
        d_, a_, b_ = qk_post(qkv[g], dn.reshape(rows, QKV_W), gq_t, gk_t, rows, f"qk_post{g}")
        dqkv.append(d_)
        dgq, dgk = dgq + a_, dgk + b_
    per_head = lambda v: jnp.sum(v.reshape(ATT_H, ATT_HD), axis=0, keepdims=True)
    grads["q_norm_g"], grads["k_norm_g"] = per_head(dgq), per_head(dgk)

    def gated_norm_bwd(ctx, dyn, yv, zv, g):
        sz = _silu(zv)
        yz = yv * sz
        dyz, dgs_ = [], []
        for i in range(0, cw, NORM_GROUP):
            a, b = _rms_bwd(yz[:, i:i + NORM_GROUP], g[:, i:i + NORM_GROUP], dyn[:, i:i + NORM_GROUP])
            dyz.append(a)
            dgs_.append(b)
        dyz = jnp.concatenate(dyz, axis=1)
        return dyz * sz, dyz * yv * _silu_grad(zv), jnp.concatenate(dgs_, axis=1)

    dy, dz, grads["ssd_norm_g"] = rw(gated_norm_bwd, "ssd_post_bwd", D_INNER // cw,
                                     [(dy_ssd, cw, 0, None), (y, cw, 0, None), (z, cw, 0, None)], [(gn, cw, 0)],
                                     [(D_INNER, cw, 0, F32), (D_INNER, cw, 0, MXU)], [(1, cw)])
    if late is None:
        dxact, ddt, dbias, dalog, ddskip = ssd_bwd(xact, dtraw, dt_bias, a_log, d_skip, sin, dy, n_seq, seq)
    else:
        dxact, ddt, dbias, dalog, ddskip, *parts = ssd_bwd(
            xact, dtraw, dt_bias, a_log, d_skip, sin, dy, n_seq, seq,
            comm=[(_to_slabs(n, _narrow(n, grads[n])), "scatter") for n in LATE])
        grads.update(zip(LATE, parts))
    grads["dt_bias"], grads["a_log"], grads["d_skip"] = dbias[:, :SSD_H], dalog[:, :SSD_H], ddskip[:, :SSD_H]

    def conv_silu_bwd(ctx, dxa, xin, wv, bv):
        pre, pre_n = _conv_pre(xin, wv, bv, ctx.first, SSD_CONV)
        return _conv_bwd(dxa[0] * _silu_grad(pre), dxa[1] * _silu_grad(pre_n), xin, wv, ctx, SSD_CONV)

    dxbc, grads["ssd_conv_w"], grads["ssd_conv_b"] = rw(
        conv_silu_bwd, "ssd_conv_bwd", CONV_DIM // cw, [(dxact, cw, 0, "next"), (xbc, cw, 0, "both")],
        [(conv_w, cw, 0), (conv_b, cw, 0)], [(CONV_DIM, cw, 0, MXU)], [(SSD_CONV, cw), (1, cw)])

    pieces = [(dz, w_z, "z"), (dxbc, w_xbc, "xbc"), (ddt, w_dt, "dt"), (dgs, w_gs, "gs"), (dga, w_ga, "ga")]
    pieces += [(dqkv[g], w_qkv[g], f"qkv{g}") for g in range(ATT_GROUPS)]
    dh, dws = None, {}
    for dpart, wpart, tag in pieces:
        dh = matmul(dpart, wpart, f"mm_dh_{tag}", tb=True, add=dh)
        dws[tag] = matmul(h, dpart, f"mm_dw_{tag}", ta=True, out_dtype=MXU)
    dq_parts = [[dws[f"qkv{g}"][:, t * ATT_OUT:(t + 1) * ATT_OUT] for g in range(ATT_GROUPS)] for t in range(3)]
    grads["w_in"] = jnp.concatenate(
        [dws["z"], dws["xbc"], dws["dt"][:, :SSD_H]] + [p for t in range(3) for p in dq_parts[t]] + [dws["gs"], dws["ga"]],
        axis=1)
    grad_x, grads["norm1_g"] = rw(rms_bwd_fn, "rms1_bwd", 1,
                                  [(x, D_MODEL, 0, None), (dh, D_MODEL, 0, None), (dx1, D_MODEL, 0, None)],
                                  [(g1, None, 0)], [(D_MODEL, D_MODEL, 0, F32)], [(1, D_MODEL)])
    return sq, grad_x.reshape(n_seq, seq, D_MODEL), grads


EARLY = ("w_in", "ssd_conv_w")
REPLICATED = ("norm1_g", "ssd_conv_b", "dt_bias", "a_log", "d_skip", "ssd_norm_g", "q_norm_g", "k_norm_g",
              "norm2_g", "ffn_conv_b")
WEIGHTS = ("norm1_g", "w_in", "ssd_conv_w", "ssd_conv_b", "dt_bias", "a_log", "d_skip", "ssd_norm_g", "w_ssd_proj",
           "q_norm_g", "k_norm_g", "w_attn_proj", "w_out", "norm2_g", "w_up", "ffn_conv_w", "ffn_conv_b", "w_down")
PACK_ROWS, PACK_COLS = 8, 2048


def _pack(vals):
    flat = jnp.concatenate([vals[n].reshape(-1) for n in REPLICATED])
    return jnp.pad(flat, (0, PACK_ROWS * PACK_COLS - flat.shape[0])).reshape(PACK_ROWS, PACK_COLS)


def _unpack(packed, like):
    flat = packed.reshape(-1)
    out, pos = {}, 0
    for n in REPLICATED:
        size = like[n].size
        out[n] = flat[pos:pos + size].reshape(like[n].shape)
        pos += size
    return out


def step(x, target, wsh, msh, vsh):
    gathered = exchange([(_narrow(n, wsh[n]), "gather") for n in EARLY], "ag_weights")
    full = {n: wsh[n] for n in REPLICATED}
    full.update({n: _from_gathered(n, g) for n, g in zip(EARLY, gathered)})

    sq, grad_x, grads = local_step(x, target, full, late={n: _narrow(n, wsh[n]) for n in LATE})

    slabs = [_to_slabs(n, _narrow(n, grads[n])) for n in EARLY]
    packed_g = _pack({n: grads[n] for n in REPLICATED})
    core = lax.axis_index("c")
    from_sibling = sibling_swap(slabs, "rs_sibling")
    chip_parts = []
    for n, s, f in zip(EARLY, slabs, from_sibling):
        mine = lax.dynamic_index_in_dim(s.reshape((N_CHIPS, 2) + s.shape[1:]), core, axis=1, keepdims=False)
        chip_parts.append(pair_add(mine, f, f"rs_add_{n}"))
    received = dict(zip(EARLY, chip_exchange(chip_parts, "rs_chips")))
    received.update({n: grads[n] for n in LATE})
    (small,) = exchange([(packed_g, "gather")], "ag_small")

    out_g, out_d, out_m, out_v = {}, {}, {}, {}
    for n, parts in received.items():
        out_g[n], out_d[n], out_m[n], out_v[n] = adamw(parts, wsh[n], msh[n], vsh[n], f"adamw_{n}")
    pk = adamw(small, _pack(wsh), _pack(msh), _pack(vsh), "adamw_small")
    for dst, packed in zip((out_g, out_d, out_m, out_v), pk):
        dst.update(_unpack(packed, wsh))
    loss = lax.psum(0.5 * jnp.sum(sq) / D_MODEL, ("x", "y", "c"))
    return loss, grad_x, out_g, out_d, out_m, out_v


def kernel(x, norm1_g, w_in, ssd_conv_w, ssd_conv_b, dt_bias, a_log, d_skip, ssd_norm_g, w_ssd_proj, q_norm_g, k_norm_g, w_attn_proj, w_out, norm2_g, w_up, ffn_conv_w, ffn_conv_b, w_down, loss_target, m_norm1_g, m_w_in, m_ssd_conv_w, m_ssd_conv_b, m_dt_bias, m_a_log, m_d_skip, m_ssd_norm_g, m_w_ssd_proj, m_q_norm_g, m_k_norm_g, m_w_attn_proj, m_w_out, m_norm2_g, m_w_up, m_ffn_conv_w, m_ffn_conv_b, m_w_down, v_norm1_g, v_w_in, v_ssd_conv_w, v_ssd_conv_b, v_dt_bias, v_a_log, v_d_skip, v_ssd_norm_g, v_w_ssd_proj, v_q_norm_g, v_k_norm_g, v_w_attn_proj, v_w_out, v_norm2_g, v_w_up, v_ffn_conv_w, v_ffn_conv_b, v_w_down):
    ws = (norm1_g, w_in, ssd_conv_w, ssd_conv_b, dt_bias, a_log, d_skip, ssd_norm_g, w_ssd_proj, q_norm_g, k_norm_g,
          w_attn_proj, w_out, norm2_g, w_up, ffn_conv_w, ffn_conv_b, w_down)
    ms = (m_norm1_g, m_w_in, m_ssd_conv_w, m_ssd_conv_b, m_dt_bias, m_a_log, m_d_skip, m_ssd_norm_g, m_w_ssd_proj,
          m_q_norm_g, m_k_norm_g, m_w_attn_proj, m_w_out, m_norm2_g, m_w_up, m_ffn_conv_w, m_ffn_conv_b, m_w_down)
    vs = (v_norm1_g, v_w_in, v_ssd_conv_w, v_ssd_conv_b, v_dt_bias, v_a_log, v_d_skip, v_ssd_norm_g, v_w_ssd_proj,
          v_q_norm_g, v_k_norm_g, v_w_attn_proj, v_w_out, v_norm2_g, v_w_up, v_ffn_conv_w, v_ffn_conv_b, v_w_down)
    strip = lambda a: a[0] if a.ndim == 3 else a
    wsh = {n: strip(a) for n, a in zip(WEIGHTS, ws)}
    msh = {n: strip(a) for n, a in zip(WEIGHTS, ms)}
    vsh = {n: strip(a) for n, a in zip(WEIGHTS, vs)}
    loss, grad_x, g, d, m, v = step(x, loss_target, wsh, msh, vsh)
    lead = lambda dct: [dct[n][None] if a.ndim == 3 else dct[n] for n, a in zip(WEIGHTS, ws)]
    return (loss, grad_x, *lead(g), *lead(d), *lead(m), *lead(v))
```

```python
import jax
import jax.numpy as jnp
from jax import lax
from jax.experimental import pallas as pl
from jax.experimental.pallas import tpu as pltpu

F32 = jnp.float32
BF16 = jnp.bfloat16
MXU = jnp.bfloat16
HIGHEST = lax.Precision.HIGHEST
VMEM_LIMIT_BYTES = 48 * 1024 * 1024
SUBLANES = 8
LANES = 128
N_DEV = 8

D_MODEL = 1024
D_INNER = 2048
SSD_P = 64
SSD_H = 32
SSD_G = 8
SSD_K = SSD_H // SSD_G
SSD_N = 128
SSD_Q = 128
SSD_CONV = 4
CONV_DIM = D_INNER + 2 * SSD_G * SSD_N
NORM_GROUP = D_INNER // SSD_G
ATT_GROUPS = 3
ATT_H = 8
ATT_HD = 64
ATT_BLK = 128
ATT_OUT = ATT_H * ATT_HD
ATT_DILATIONS = (1, 4, 16)
ATT_SCALE = ATT_HD ** -0.5
D_FF = 2816
FFN_CONV = 3
EPS = 1e-6
NEG = -1e30
IN_WIDTHS = (D_INNER, CONV_DIM, SSD_H, 3 * ATT_OUT, 3 * ATT_OUT, 3 * ATT_OUT, D_MODEL, D_MODEL)

ADAM_LR = 0.001
ADAM_B1 = 0.9
ADAM_B2 = 0.999
ADAM_EPS = 1e-08
ADAM_WD = 0.01
ADAM_STEP = 10


def _mm(a, b, dims):
    return lax.dot_general(a.astype(MXU), b.astype(MXU), (dims, ((), ())), preferred_element_type=F32)


def _dot_nn(a, b):
    return _mm(a, b, ((1,), (0,)))


def _dot_nt(a, b):
    return _mm(a, b, ((1,), (1,)))


def _dot_tn(a, b):
    return _mm(a, b, ((0,), (0,)))


def _dot_f32(a, b):
    return lax.dot_general(a, b, (((1,), (0,)), ((), ())), precision=HIGHEST, preferred_element_type=F32)


def _sigmoid(x):
    return 1.0 / (1.0 + jnp.exp(-x))


def _silu(x):
    return x * _sigmoid(x)


def _silu_grad(x):
    s = _sigmoid(x)
    return s * (1.0 + x * (1.0 - s))


def _softplus(x):
    return jnp.maximum(x, 0.0) + jnp.log(1.0 + jnp.exp(-jnp.abs(x)))


def _rms_fwd(x, g):
    r = lax.rsqrt(jnp.mean(x * x, axis=-1, keepdims=True) + EPS)
    return x * r * g


def _rms_bwd(x, g, dy):
    r = lax.rsqrt(jnp.mean(x * x, axis=-1, keepdims=True) + EPS)
    xh = x * r
    dyg = dy * g
    dx = r * (dyg - xh * jnp.mean(dyg * xh, axis=-1, keepdims=True))
    return dx, jnp.sum(dy * xh, axis=0, keepdims=True)


def _onehot_row(h, n=LANES):
    return (lax.broadcasted_iota(jnp.int32, (1, n), 1) == h).astype(F32)


def _onehot_col(h, n=LANES):
    return (lax.broadcasted_iota(jnp.int32, (n, 1), 0) == h).astype(F32)


def _head_expand_matrix():
    r = lax.broadcasted_iota(jnp.int32, (LANES, ATT_OUT), 0)
    c = lax.broadcasted_iota(jnp.int32, (LANES, ATT_OUT), 1)
    return (c // ATT_HD == r).astype(F32)


def _split_bf16(x, parts):
    out = []
    for _ in range(parts - 1):
        hi = x.astype(BF16).astype(F32)
        out.append(hi)
        x = x - hi
    out.append(x)
    return out


def _expand_heads(w):
    e = _head_expand_matrix()
    return sum(_dot_nn(p, e) for p in _split_bf16(w, 2))


def _reduce_heads(x):
    e = _head_expand_matrix()
    return sum(_dot_nt(p, e) for p in _split_bf16(x, 3))


def _shift_prev(cur, halo, s, first):
    if s == 0:
        return cur
    rolled = pltpu.roll(cur, s, 0)
    hr = jnp.where(first, 0.0, pltpu.roll(halo, s, 0))
    rows = lax.broadcasted_iota(jnp.int32, halo.shape, 0)
    head = jnp.where(rows < s, hr, rolled[:SUBLANES])
    if cur.shape[0] == SUBLANES:
        return head
    return jnp.concatenate([head, rolled[SUBLANES:]], axis=0)


def _shift_next(cur, halo, s, last):
    if s == 0:
        return cur
    tb = cur.shape[0]
    rolled = pltpu.roll(cur, tb - s, 0)
    hr = jnp.where(last, 0.0, pltpu.roll(halo, SUBLANES - s, 0))
    rows = lax.broadcasted_iota(jnp.int32, halo.shape, 0)
    tail = jnp.where(rows >= SUBLANES - s, hr, rolled[tb - SUBLANES:])
    return jnp.concatenate([rolled[:tb - SUBLANES], tail], axis=0)


def _conv_prev(x, halo, w, first, taps):
    acc = None
    for i in range(taps):
        term = w[i:i + 1, :] * _shift_prev(x, halo, taps - 1 - i, first)
        acc = term if acc is None else acc + term
    return acc


def _conv_pre(x, w, b, first, taps):
    cur, prev8, next8 = x
    tail = cur[cur.shape[0] - SUBLANES:]
    return b + _conv_prev(cur, prev8, w, first, taps), b + _conv_prev(next8, tail, w, False, taps)


def _conv_bwd(dpre, dpre_next8, x, w, ctx, taps):
    cur, prev8, _ = x
    dx, dws = None, []
    for i in range(taps):
        term = w[i:i + 1, :] * _shift_next(dpre, dpre_next8, taps - 1 - i, ctx.last)
        dx = term if dx is None else dx + term
        dws.append(jnp.sum(dpre * _shift_prev(cur, prev8, taps - 1 - i, ctx.first), axis=0, keepdims=True))
    return dx, jnp.concatenate(dws, axis=0), jnp.sum(dpre, axis=0, keepdims=True)


def _params(sem):
    return pltpu.CompilerParams(dimension_semantics=sem, vmem_limit_bytes=VMEM_LIMIT_BYTES)


def _pick(dim, target):
    if dim <= target:
        return dim
    best = None
    for t in range(LANES, target + 1, LANES):
        if dim % t == 0:
            best = t
    assert best is not None, (dim, target)
    return best


MATMUL_VMEM_BUDGET = 34 * 1024 * 1024


V7X_MXU_FLOPS = 996e12
V7X_HBM_BYTES_PER_S = 3.4e12
GRID_STEP_S = 0.35e-6


def _tile_sizes(dim, cap):
    return [t for t in range(LANES, min(dim, cap) + 1, LANES) if dim % t == 0] or [dim]


def _matmul_tiles(m, n, k, a_bytes, b_bytes, add_bytes, out_bytes):
    best = None
    for tk in _tile_sizes(k, 8192):
        nk = k // tk
        for tn in _tile_sizes(n, 2048):
            for tm in _tile_sizes(m, 2048):
                io = tm * tk * a_bytes + tk * tn * b_bytes
                ends = tm * tn * (add_bytes + out_bytes)
                need = 2 * (io + ends) + tm * tn * 4 * (2 if nk > 1 else 1)
                if need > MATMUL_VMEM_BUDGET:
                    continue
                step = max(2.0 * tm * tn * tk / V7X_MXU_FLOPS, (io + ends / nk) / V7X_HBM_BYTES_PER_S)
                if nk > 1:
                    step += tm * tn * 8 / V7X_HBM_BYTES_PER_S
                cost = (m // tm) * (n // tn) * nk * (step + GRID_STEP_S)
                if best is None or cost < best[0]:
                    best = (cost, tm, tn, tk)
    if best is None:
        raise ValueError((m, n, k))
    return best[1:]


def matmul(a, b, name, ta=False, tb=False, add=None, out_dtype=F32):
    assert not (ta and tb)
    m, k = (a.shape[1], a.shape[0]) if ta else a.shape
    n = b.shape[0] if tb else b.shape[1]
    assert (b.shape[1] if tb else b.shape[0]) == k
    tm, tn, tk = _matmul_tiles(m, n, k, a.dtype.itemsize, b.dtype.itemsize,
                               0 if add is None else add.dtype.itemsize, jnp.dtype(out_dtype).itemsize)
    nk = k // tk
    dims = ((0,), (0,)) if ta else (((1,), (1,)) if tb else ((1,), (0,)))

    def body(*refs):
        if add is None:
            a_ref, b_ref, o_ref = refs[:3]
        else:
            a_ref, b_ref, add_ref, o_ref = refs[:4]

        def finish(r):
            if add is not None:
                r = r + add_ref[...].astype(F32)
            o_ref[...] = r.astype(out_dtype)

        if nk == 1:
            finish(_mm(a_ref[...], b_ref[...], dims))
            return
        acc = refs[-1]
        kk = pl.program_id(2)

        @pl.when(kk == 0)
        def _():
            acc[...] = jnp.zeros_like(acc)

        acc[...] += _mm(a_ref[...], b_ref[...], dims)

        @pl.when(kk == nk - 1)
        def _():
            finish(acc[...])

    a_spec = pl.BlockSpec((tk, tm), lambda i, j, kk: (kk, i)) if ta else pl.BlockSpec((tm, tk), lambda i, j, kk: (i, kk))
    b_spec = pl.BlockSpec((tn, tk), lambda i, j, kk: (j, kk)) if tb else pl.BlockSpec((tk, tn), lambda i, j, kk: (kk, j))
    in_specs = [a_spec, b_spec]
    args = [a, b]
    if add is not None:
        in_specs.append(pl.BlockSpec((tm, tn), lambda i, j, kk: (i, j)))
        args.append(add)
    return pl.pallas_call(
        body, name=name,
        grid=(m // tm, n // tn, nk),
        in_specs=in_specs,
        out_specs=pl.BlockSpec((tm, tn), lambda i, j, kk: (i, j)),
        out_shape=jax.ShapeDtypeStruct((m, n), out_dtype),
        scratch_shapes=[] if nk == 1 else [pltpu.VMEM((tm, tn), F32)],
        compiler_params=_params(("parallel", "parallel", "arbitrary")),
    )(*args)


class _Ctx:
    def __init__(self, first, last):
        self.first = first
        self.last = last


def rowwise(fn, name, rows, seq, tb, ncol, ins, params=(), outs=(), accs=()):
    assert rows % tb == 0 and seq % tb == 0 and tb % 16 == 0
    bps = seq // tb
    nrow = rows // tb
    r8 = tb // SUBLANES
    args, in_specs = [], []
    for arr, w, off, halo in ins:
        args.append(arr)
        in_specs.append(pl.BlockSpec((tb, w), lambda j, i, off=off: (i, off + j)))
        if halo in ("prev", "both"):
            args.append(arr)
            in_specs.append(pl.BlockSpec((SUBLANES, w), lambda j, i, off=off: (jnp.maximum(i * r8 - 1, 0), off + j)))
        if halo in ("next", "both"):
            args.append(arr)
            in_specs.append(pl.BlockSpec(
                (SUBLANES, w), lambda j, i, off=off: (jnp.minimum((i + 1) * r8, rows // SUBLANES - 1), off + j)))
    for arr, w, off in params:
        args.append(arr)
        if w is None:
            in_specs.append(pl.BlockSpec(arr.shape, lambda j, i: (0, 0)))
        else:
            in_specs.append(pl.BlockSpec((arr.shape[0], w), lambda j, i, off=off: (0, off + j)))
    out_shape, out_specs = [], []
    for total, w, off, dt in outs:
        out_shape.append(jax.ShapeDtypeStruct((rows, total), dt))
        out_specs.append(pl.BlockSpec((tb, w), lambda j, i, off=off: (i, off + j)))
    for r, w in accs:
        out_shape.append(jax.ShapeDtypeStruct((r, ncol * w), F32))
        out_specs.append(pl.BlockSpec((r, w), lambda j, i: (0, j)))
    n_out, n_acc = len(outs), len(accs)

    def body(*refs):
        i = pl.program_id(1)
        pos = 0
        vals = []
        for _, _, _, halo in ins:
            cur = refs[pos][...]
            pos += 1
            if halo is None:
                vals.append(cur)
            elif halo == "both":
                vals.append((cur, refs[pos][...], refs[pos + 1][...]))
                pos += 2
            else:
                vals.append((cur, refs[pos][...]))
                pos += 1
        for _ in params:
            vals.append(refs[pos][...])
            pos += 1
        ctx = _Ctx(i % bps == 0, i % bps == bps - 1)
        res = fn(ctx, *vals)
        if not isinstance(res, (tuple, list)):
            res = (res,)
        assert len(res) == n_out + n_acc
        for q in range(n_out):
            refs[pos + q][...] = res[q].astype(refs[pos + q].dtype)
        for q in range(n_acc):
            ref, val = refs[pos + n_out + q], res[n_out + q]

            @pl.when(i == 0)
            def _(ref=ref, val=val):
                ref[...] = val

            @pl.when(i != 0)
            def _(ref=ref, val=val):
                ref[...] += val

    res = pl.pallas_call(
        body, name=name,
        grid=(ncol, nrow),
        in_specs=in_specs,
        out_specs=out_specs,
        out_shape=out_shape,
        compiler_params=_params(("parallel", "arbitrary")),
    )(*args)
    return res


def _direct_exchange(modes, in_refs, out_refs, send_sems, recv_sems, local_sems):
    n = len(modes)
    x, y, c = lax.axis_index("x"), lax.axis_index("y"), lax.axis_index("c")
    me = 4 * x + 2 * y + c

    def copy(t, k, arriving):
        px = 1 - x if k & 4 else x
        py = 1 - y if k & 2 else y
        pc = 1 - c if k & 1 else c
        pid = 4 * px + 2 * py + pc
        src = in_refs[t] if modes[t] == "gather" else in_refs[t].at[pid]
        return pltpu.make_async_remote_copy(
            src_ref=src, dst_ref=out_refs[t].at[pid if arriving else me], send_sem=send_sems.at[t, k],
            recv_sem=recv_sems.at[t, k], device_id=(px, py, pc), device_id_type=pl.DeviceIdType.MESH)

    def own(t):
        src = in_refs[t] if modes[t] == "gather" else in_refs[t].at[me]
        return pltpu.make_async_copy(src, out_refs[t].at[me], local_sems.at[t])

    def start():
        for t in range(n):
            own(t).start()
            for k in range(1, N_DEV):
                copy(t, k, False).start()

    def finish():
        for t in range(n):
            for k in range(1, N_DEV):
                copy(t, k, True).wait_recv()
        for t in range(n):
            for k in range(1, N_DEV):
                copy(t, k, False).wait_send()
            own(t).wait()

    return start, finish


def _exchange_specs(items):
    n = len(items)
    out_shape = [jax.ShapeDtypeStruct((N_DEV,) + tuple(a.shape if mode == "gather" else a.shape[1:]), a.dtype)
                 for a, mode in items]
    anyspec = pl.BlockSpec(memory_space=pl.ANY)
    sems = [pltpu.SemaphoreType.DMA((n, N_DEV)), pltpu.SemaphoreType.DMA((n, N_DEV)), pltpu.SemaphoreType.DMA((n,))]
    return [anyspec] * n, [anyspec] * n, out_shape, sems


GROUP_W = SSD_K * SSD_P


def _tri(lower):
    r = lax.broadcasted_iota(jnp.int32, (SSD_Q, SSD_Q), 0)
    c = lax.broadcasted_iota(jnp.int32, (SSD_Q, SSD_Q), 1)
    return r >= c if lower else r <= c


def _group_masks():
    lane = lax.broadcasted_iota(jnp.int32, (1, GROUP_W), 1) // SSD_P
    row = lax.broadcasted_iota(jnp.int32, (GROUP_W, 1), 0) // SSD_P
    return [lane == k for k in range(SSD_K)], [row == k for k in range(SSD_K)]


def _per_head(masks, vals):
    out = jnp.where(masks[0], vals[0], 0.0)
    for m, v in zip(masks[1:], vals[1:]):
        out = jnp.where(m, v, out)
    return out


def _headsum(prod, g, lane_masks):
    out = None
    for k in range(SSD_K):
        term = jnp.sum(jnp.where(lane_masks[k], prod, 0.0), axis=1, keepdims=True) * _onehot_row(g * SSD_K + k)
        out = term if out is None else out + term
    return out


def ssd_fwd(xact, dtraw, dt_bias, a_log, d_skip, n_seq, seq, comm=()):
    nc = seq // SSD_Q
    rows = n_seq * seq
    nx = len(comm)

    def body(*refs):
        xact_ref, dtraw_ref, bias_ref, alog_ref, dskip_ref = refs[:5]
        y_ref, sin_ref = refs[5 + nx:7 + nx]
        state, cs_s, cst_s, dt_s = refs[7 + 2 * nx:11 + 2 * nx]
        b, c = pl.program_id(0), pl.program_id(1)
        if nx:
            start, finish = _direct_exchange([m for _, m in comm], refs[5:5 + nx], refs[7 + nx:7 + 2 * nx],
                                             *refs[11 + 2 * nx:])
            pl.when(jnp.logical_and(b == 0, c == 0))(start)

        @pl.when(c == 0)
        def _():
            state[...] = jnp.zeros_like(state)

        sin_ref[0] = state[...]
        dt = _softplus(dtraw_ref[...] + bias_ref[...])
        a = dt * (-jnp.exp(alog_ref[...]))
        cs = _dot_f32(_tri(True).astype(F32), a)
        cs_s[...] = cs
        cst_s[...] = cs.T
        dt_s[...] = dt
        causal = _tri(True)
        lane_masks, row_masks = _group_masks()
        for g in range(SSD_G):
            heads = [g * SSD_K + k for k in range(SSD_K)]
            bg = xact_ref[:, pl.ds(D_INNER + g * SSD_N, SSD_N)]
            cg = xact_ref[:, pl.ds(D_INNER + (SSD_G + g) * SSD_N, SSD_N)]
            xg = xact_ref[:, pl.ds(g * GROUP_W, GROUP_W)]
            cols = [cs_s[:, pl.ds(h, 1)] for h in heads]
            lasts = [cs_s[pl.ds(SSD_Q - 1, 1), pl.ds(h, 1)] for h in heads]
            xdg = xg * _per_head(lane_masks, [dt_s[:, pl.ds(h, 1)] for h in heads])
            sg = state[g]
            gm = _dot_nt(cg, bg)
            y = (_per_head(lane_masks, [jnp.exp(c_) for c_ in cols]) * _dot_nt(cg, sg)
                 + _per_head(lane_masks, [dskip_ref[:, pl.ds(h, 1)] for h in heads]) * xg)
            mats = [gm * jnp.exp(jnp.where(causal, cols[k] - cst_s[pl.ds(h, 1), :], NEG)) for k, h in enumerate(heads)]
            y4 = _dot_nn(jnp.concatenate(mats, axis=0), xdg)
            y = y + _per_head(lane_masks, [y4[k * SSD_Q:(k + 1) * SSD_Q] for k in range(SSD_K)])
            y_ref[:, pl.ds(g * GROUP_W, GROUP_W)] = y
            w = _per_head(lane_masks, [jnp.exp(l_ - c_) for l_, c_ in zip(lasts, cols)])
            state[g] = _per_head(row_masks, [jnp.exp(l_) for l_ in lasts]) * sg + _dot_tn(w * xdg, bg)
        if nx:
            pl.when(jnp.logical_and(b == n_seq - 1, c == nc - 1))(finish)

    vec = pl.BlockSpec((1, LANES), lambda b, c: (0, 0))
    x_in, x_out, x_shape, x_sems = _exchange_specs(comm)
    return pl.pallas_call(
        body, name="ssd_fwd",
        grid=(n_seq, nc),
        in_specs=[pl.BlockSpec((SSD_Q, CONV_DIM), lambda b, c: (b * nc + c, 0)),
                  pl.BlockSpec((SSD_Q, LANES), lambda b, c: (b * nc + c, 0)), vec, vec, vec] + x_in,
        out_specs=[pl.BlockSpec((SSD_Q, D_INNER), lambda b, c: (b * nc + c, 0)),
                   pl.BlockSpec((1, SSD_G, GROUP_W, SSD_N), lambda b, c: (b * nc + c, 0, 0, 0))] + x_out,
        out_shape=[jax.ShapeDtypeStruct((rows, D_INNER), F32),
                   jax.ShapeDtypeStruct((n_seq * nc, SSD_G, GROUP_W, SSD_N), F32)] + x_shape,
        scratch_shapes=[pltpu.VMEM((SSD_G, GROUP_W, SSD_N), F32), pltpu.VMEM((SSD_Q, LANES), F32),
                        pltpu.VMEM((LANES, SSD_Q), F32), pltpu.VMEM((SSD_Q, LANES), F32)] + (x_sems if nx else []),
        compiler_params=_params(("arbitrary", "arbitrary")),
    )(xact, dtraw, dt_bias, a_log, d_skip, *[a for a, _ in comm])


def ssd_bwd(xact, dtraw, dt_bias, a_log, d_skip, sin, dy, n_seq, seq, comm=()):
    nc = seq // SSD_Q
    rows = n_seq * seq
    nx = len(comm)

    def body(*refs):
        xact_ref, dtraw_ref, bias_ref, alog_ref, dskip_ref, sin_ref, dy_ref = refs[:7]
        dx_ref, ddt_ref, dbias_ref, dalog_ref, ddskip_ref = refs[7 + nx:12 + nx]
        dstate, cs_s, cst_s, dt_s = refs[12 + 2 * nx:16 + 2 * nx]
        b, c = pl.program_id(0), pl.program_id(1)
        if nx:
            start, finish = _direct_exchange([m for _, m in comm], refs[7:7 + nx], refs[12 + nx:12 + 2 * nx],
                                             *refs[16 + 2 * nx:])
            pl.when(jnp.logical_and(b == 0, c == 0))(start)

        @pl.when(c == 0)
        def _():
            dstate[...] = jnp.zeros_like(dstate)

        pre = dtraw_ref[...] + bias_ref[...]
        dt = _softplus(pre)
        a_neg = -jnp.exp(alog_ref[...])
        cs = _dot_f32(_tri(True).astype(F32), dt * a_neg)
        cs_s[...] = cs
        cst_s[...] = cs.T
        dt_s[...] = dt
        causal, anti = _tri(True), _tri(False)
        is_last_row = lax.broadcasted_iota(jnp.int32, (SSD_Q, 1), 0) == SSD_Q - 1
        lane_masks, row_masks = _group_masks()
        dcs_cf = jnp.zeros((SSD_Q, LANES), F32)
        dcs_rf = jnp.zeros((LANES, SSD_Q), F32)
        ddt_cf = jnp.zeros((SSD_Q, LANES), F32)
        dd_vec = jnp.zeros((1, LANES), F32)
        dlast_vec = jnp.zeros((1, LANES), F32)
        for g in range(SSD_G):
            heads = [g * SSD_K + k for k in range(SSD_K)]
            bg = xact_ref[:, pl.ds(D_INNER + g * SSD_N, SSD_N)]
            cg = xact_ref[:, pl.ds(D_INNER + (SSD_G + g) * SSD_N, SSD_N)]
            xg = xact_ref[:, pl.ds(g * GROUP_W, GROUP_W)]
            dyg = dy_ref[:, pl.ds(g * GROUP_W, GROUP_W)]
            cols = [cs_s[:, pl.ds(h, 1)] for h in heads]
            rws = [cst_s[pl.ds(h, 1), :] for h in heads]
            lasts = [cs_s[pl.ds(SSD_Q - 1, 1), pl.ds(h, 1)] for h in heads]
            e_lasts = [jnp.exp(l_) for l_ in lasts]
            dtg = _per_head(lane_masks, [dt_s[:, pl.ds(h, 1)] for h in heads])
            dskg = _per_head(lane_masks, [dskip_ref[:, pl.ds(h, 1)] for h in heads])
            e_col = _per_head(lane_masks, [jnp.exp(c_) for c_ in cols])
            w = _per_head(lane_masks, [jnp.exp(l_ - c_) for l_, c_ in zip(lasts, cols)])
            xdg = xg * dtg
            sg = sin_ref[0, g]
            dsn = dstate[g]
            gm = _dot_nt(cg, bg)
            gmt = _dot_nt(bg, cg)
            y_off = e_col * _dot_nt(cg, sg)
            d_cs = e_col * dyg
            dcg = _dot_nn(d_cs, sg)
            dsp = _dot_tn(d_cs, cg) + _per_head(row_masks, e_lasts) * dsn
            dbg = _dot_nn(w * xdg, dsn)
            dtt = _dot_nt(bg, dsn)
            dxd = w * dtt
            dw = _headsum(dtt * xdg * w, g, lane_masks)
            dcs_cf = dcs_cf + _headsum(dyg * y_off, g, lane_masks) - dw
            dlast_vec = dlast_vec + jnp.sum(dw, axis=0, keepdims=True)
            dsn_s = dsn * sg
            segs = [cols[k] - rws[k] for k in range(SSD_K)]
            decays = [jnp.exp(jnp.where(causal, s_, NEG)) for s_ in segs]
            dm4 = _dot_nt(jnp.concatenate([jnp.where(m_, dyg, 0.0) for m_ in lane_masks], axis=0), xdg)
            z4 = _dot_nn(jnp.concatenate([gmt * jnp.exp(jnp.where(anti, -s_, NEG)) for s_ in segs], axis=0), dyg)
            dxd = dxd + _per_head(lane_masks, [z4[k * SSD_Q:(k + 1) * SSD_Q] for k in range(SSD_K)])
            dgm = jnp.zeros((SSD_Q, SSD_Q), F32)
            for k, h in enumerate(heads):
                dm = dm4[k * SSD_Q:(k + 1) * SSD_Q]
                dseg = dm * gm * decays[k]
                dgm = dgm + dm * decays[k]
                oh_r = _onehot_row(h)
                dcs_cf = dcs_cf + jnp.sum(dseg, axis=1, keepdims=True) * oh_r
                dcs_rf = dcs_rf - _onehot_col(h) * jnp.sum(dseg, axis=0, keepdims=True)
                dlast_vec = dlast_vec + jnp.sum(jnp.where(row_masks[k], dsn_s, 0.0), keepdims=True) * e_lasts[k] * oh_r
            dx_ref[:, pl.ds(g * GROUP_W, GROUP_W)] = dxd * dtg + dskg * dyg
            ddt_cf = ddt_cf + _headsum(dxd * xg, g, lane_masks)
            dd_vec = dd_vec + jnp.sum(_headsum(dyg * xg, g, lane_masks), axis=0, keepdims=True)
            dstate[g] = dsp
            dx_ref[:, pl.ds(D_INNER + g * SSD_N, SSD_N)] = dbg + _dot_tn(dgm, cg)
            dx_ref[:, pl.ds(D_INNER + (SSD_G + g) * SSD_N, SSD_N)] = dcg + _dot_nn(dgm, bg)
        dcs = dcs_cf + dcs_rf.T + jnp.where(is_last_row, dlast_vec, 0.0)
        da = _dot_f32(_tri(False).astype(F32), dcs)
        ddt = ddt_cf + da * a_neg
        ddtraw = ddt * _sigmoid(pre)
        ddt_ref[...] = ddtraw.astype(ddt_ref.dtype)
        dbias = jnp.sum(ddtraw, axis=0, keepdims=True)
        dalog = jnp.sum(da * dt, axis=0, keepdims=True) * a_neg
        first_step = jnp.logical_and(b == 0, c == 0)

        @pl.when(first_step)
        def _():
            dbias_ref[...] = dbias
            dalog_ref[...] = dalog
            ddskip_ref[...] = dd_vec

        @pl.when(jnp.logical_not(first_step))
        def _():
            dbias_ref[...] += dbias
            dalog_ref[...] += dalog
            ddskip_ref[...] += dd_vec

        if nx:
            pl.when(jnp.logical_and(b == n_seq - 1, c == nc - 1))(finish)

    def rowblk(b, c):
        return b * nc + (nc - 1 - c)

    vec = pl.BlockSpec((1, LANES), lambda b, c: (0, 0))
    x_in, x_out, x_shape, x_sems = _exchange_specs(comm)
    return pl.pallas_call(
        body, name="ssd_bwd",
        grid=(n_seq, nc),
        in_specs=[pl.BlockSpec((SSD_Q, CONV_DIM), lambda b, c: (rowblk(b, c), 0)),
                  pl.BlockSpec((SSD_Q, LANES), lambda b, c: (rowblk(b, c), 0)), vec, vec, vec,
                  pl.BlockSpec((1, SSD_G, GROUP_W, SSD_N), lambda b, c: (rowblk(b, c), 0, 0, 0)),
                  pl.BlockSpec((SSD_Q, D_INNER), lambda b, c: (rowblk(b, c), 0))] + x_in,
        out_specs=[pl.BlockSpec((SSD_Q, CONV_DIM), lambda b, c: (rowblk(b, c), 0)),
                   pl.BlockSpec((SSD_Q, LANES), lambda b, c: (rowblk(b, c), 0)), vec, vec, vec] + x_out,
        out_shape=[jax.ShapeDtypeStruct((rows, CONV_DIM), F32), jax.ShapeDtypeStruct((rows, LANES), BF16),
                   jax.ShapeDtypeStruct((1, LANES), F32), jax.ShapeDtypeStruct((1, LANES), F32),
                   jax.ShapeDtypeStruct((1, LANES), F32)] + x_shape,
        scratch_shapes=[pltpu.VMEM((SSD_G, GROUP_W, SSD_N), F32), pltpu.VMEM((SSD_Q, LANES), F32),
                        pltpu.VMEM((LANES, SSD_Q), F32), pltpu.VMEM((SSD_Q, LANES), F32)] + (x_sems if nx else []),
        compiler_params=_params(("arbitrary", "arbitrary")),
    )(xact, dtraw, dt_bias, a_log, d_skip, sin, dy, *[a for a, _ in comm])


QKV_W = 3 * ATT_OUT
PAIR_W = 2 * ATT_HD
HEAD_PAIRS = ATT_H // 2
PREP_ROWS = 512


def _dilated(a, n_seq, seq, dil):
    return a.reshape(n_seq * (seq // dil), dil * a.shape[1])


def _head_sums(x, fn):
    lo = jnp.logical_not(lax.broadcasted_iota(jnp.int32, (1, 2 * ATT_HD), 1) >= ATT_HD)
    parts = []
    for p in range(ATT_H // 2):
        slab = x[:, p * 2 * ATT_HD:(p + 1) * 2 * ATT_HD]
        s_lo = fn(jnp.sum(jnp.where(lo, slab, 0.0), axis=1, keepdims=True))
        s_hi = fn(jnp.sum(jnp.where(lo, 0.0, slab), axis=1, keepdims=True))
        parts.append(jnp.where(lo, s_lo, s_hi))
    return jnp.concatenate(parts, axis=1)


def _head_rstd(x):
    return _head_sums(x * x, lambda s: lax.rsqrt(s * (1.0 / ATT_HD) + EPS))


def _head_rms_bwd(x, g_t, dy):
    r = _head_rstd(x)
    xh = x * r
    dyg = dy * g_t
    mean = _head_sums(dyg * xh, lambda s: s * (1.0 / ATT_HD))
    return r * (dyg - xh * mean), jnp.sum(dy * xh, axis=0, keepdims=True)


def qk_prep(qkv, gq_t, gk_t, rows, name):
    tb = min(PREP_ROWS, rows)

    def body(x_ref, gq_ref, gk_ref, o_ref):
        q = x_ref[:, pl.ds(0, ATT_OUT)]
        k = x_ref[:, pl.ds(ATT_OUT, ATT_OUT)]
        o_ref[:, pl.ds(0, ATT_OUT)] = (q * _head_rstd(q) * (gq_ref[...] * ATT_SCALE)).astype(o_ref.dtype)
        o_ref[:, pl.ds(ATT_OUT, ATT_OUT)] = (k * _head_rstd(k) * gk_ref[...]).astype(o_ref.dtype)
        o_ref[:, pl.ds(2 * ATT_OUT, ATT_OUT)] = x_ref[:, pl.ds(2 * ATT_OUT, ATT_OUT)].astype(o_ref.dtype)

    gspec = pl.BlockSpec((1, ATT_OUT), lambda i: (0, 0))
    blk = pl.BlockSpec((tb, QKV_W), lambda i: (i, 0))
    return pl.pallas_call(
        body, name=name,
        grid=(rows // tb,),
        in_specs=[blk, gspec, gspec],
        out_specs=blk,
        out_shape=jax.ShapeDtypeStruct((rows, QKV_W), MXU),
        compiler_params=_params(("parallel",)),
    )(qkv, gq_t, gk_t)


def _lane_hi():
    return lax.broadcasted_iota(jnp.int32, (1, PAIR_W), 1) >= ATT_HD


def _band_mask2(first_valid, query_rows):
    i = lax.broadcasted_iota(jnp.int32, (ATT_BLK, 2 * ATT_BLK), 0)
    j = lax.broadcasted_iota(jnp.int32, (ATT_BLK, 2 * ATT_BLK), 1)
    left = j < ATT_BLK
    right = jnp.logical_not(left)
    if query_rows:
        return jnp.logical_or(jnp.logical_and(jnp.logical_and(left, i <= j), first_valid),
                              jnp.logical_and(right, i >= j - ATT_BLK))
    return jnp.logical_or(jnp.logical_and(left, j >= i),
                          jnp.logical_and(jnp.logical_and(right, j - ATT_BLK <= i), first_valid))


def _only_head(slab, hi):
    keep = _lane_hi() if hi else jnp.logical_not(_lane_hi())
    return jnp.where(keep, slab, jnp.zeros_like(slab))


def attn_fwd2(nq, n_seq, seq, dil, name):
    length = seq // dil
    nb = length // ATT_BLK

    def body(cur_ref, prev_ref, o_ref, lse_ref, s_scr, p_scr):
        n = pl.program_id(2)
        mask = _band_mask2(n > 0, True)
        for h in range(ATT_H):
            sl = pl.ds((h // 2) * PAIR_W, PAIR_W)
            ks = pl.ds(ATT_OUT + (h // 2) * PAIR_W, PAIR_W)
            kcat = jnp.concatenate([prev_ref[:, ks], cur_ref[:, ks]], axis=0)
            s_scr[h] = jnp.where(mask, _dot_nt(_only_head(cur_ref[:, sl], h % 2), kcat), NEG)
        s_all = s_scr[...]
        mx = jnp.max(s_all, axis=2, keepdims=True)
        p_all = jnp.exp(s_all - mx)
        den = jnp.sum(p_all, axis=2, keepdims=True)
        p_scr[...] = p_all.astype(p_scr.dtype)
        inv = 1.0 / den
        lse = mx + jnp.log(den)
        lse_blk = jnp.zeros((ATT_BLK, LANES), F32)
        for h in range(ATT_H):
            lse_blk = lse_blk + lse[h] * _onehot_row(h)
        lse_ref[...] = lse_blk
        for pr in range(HEAD_PAIRS):
            vs = pl.ds(2 * ATT_OUT + pr * PAIR_W, PAIR_W)
            vcat = jnp.concatenate([prev_ref[:, vs], cur_ref[:, vs]], axis=0)
            lo = _dot_nn(p_scr[2 * pr], vcat) * inv[2 * pr]
            hi = _dot_nn(p_scr[2 * pr + 1], vcat) * inv[2 * pr + 1]
            o_ref[:, pl.ds(pr * PAIR_W, PAIR_W)] = jnp.where(_lane_hi(), hi, lo)

    return pl.pallas_call(
        body, name=name,
        grid=(n_seq, dil, nb),
        in_specs=[pl.BlockSpec((ATT_BLK, QKV_W), lambda b, r, n: (b * nb + n, r)),
                  pl.BlockSpec((ATT_BLK, QKV_W), lambda b, r, n: (b * nb + jnp.maximum(n - 1, 0), r))],
        out_specs=[pl.BlockSpec((ATT_BLK, ATT_OUT), lambda b, r, n: (b * nb + n, r)),
                   pl.BlockSpec((ATT_BLK, LANES), lambda b, r, n: (b * nb + n, r))],
        out_shape=[jax.ShapeDtypeStruct((n_seq * length, dil * ATT_OUT), F32),
                   jax.ShapeDtypeStruct((n_seq * length, dil * LANES), F32)],
        scratch_shapes=[pltpu.VMEM((ATT_H, ATT_BLK, 2 * ATT_BLK), F32), pltpu.VMEM((ATT_H, ATT_BLK, 2 * ATT_BLK), MXU)],
        compiler_params=_params(("parallel", "parallel", "arbitrary")),
    )(nq, nq)


def attn_bwd2(nq, do, lse, wts, rsum, n_seq, seq, dil, name):
    length = seq // dil
    nb = length // ATT_BLK

    def body(prev_ref, cur_ref, nxt_ref, do_c, do_x, lse_c, lse_x, wt_c, wt_x, rs_c, rs_x, dn_ref):
        n = pl.program_id(2)
        mask_q = _band_mask2(n > 0, True)
        mask_k = _band_mask2(n < nb - 1, False)
        wc, wx = wt_c[...], wt_x[...]
        lse_t = jnp.concatenate([lse_c[...].T, lse_x[...].T], axis=1)
        dl_t = jnp.concatenate([(-wc * rs_c[...]).T, (-wx * rs_x[...]).T], axis=1)
        for pr in range(HEAD_PAIRS):
            sl = pl.ds(pr * PAIR_W, PAIR_W)
            ks = pl.ds(ATT_OUT + pr * PAIR_W, PAIR_W)
            vs = pl.ds(2 * ATT_OUT + pr * PAIR_W, PAIR_W)
            he, ho = pl.ds(2 * pr, 1), pl.ds(2 * pr + 1, 1)
            q_c, k_c, v_c = cur_ref[:, sl], cur_ref[:, ks], cur_ref[:, vs]
            qcat = jnp.concatenate([q_c, nxt_ref[:, sl]], axis=0)
            kcat = jnp.concatenate([prev_ref[:, ks], k_c], axis=0)
            vcat = jnp.concatenate([prev_ref[:, vs], v_c], axis=0)
            dog_c = do_c[:, sl] * jnp.where(_lane_hi(), wt_c[:, ho], wt_c[:, he])
            dog_x = do_x[:, sl] * jnp.where(_lane_hi(), wt_x[:, ho], wt_x[:, he])
            dog = jnp.concatenate([dog_c, dog_x], axis=0).astype(MXU)
            res = []
            for hi in (0, 1):
                h = 2 * pr + hi
                one = pl.ds(h, 1)
                dl_col = -wt_c[:, one] * rs_c[:, one]
                p_q = jnp.exp(jnp.where(mask_q, _dot_nt(_only_head(q_c, hi), kcat) - lse_c[:, one], NEG))
                ds_q = p_q * (_dot_nt(_only_head(dog[:ATT_BLK], hi), vcat) + dl_col)
                dq = _dot_nn(ds_q, kcat)
                p_t = jnp.exp(jnp.where(mask_k, _dot_nt(_only_head(k_c, hi), qcat) - lse_t[h:h + 1, :], NEG))
                ds_t = p_t * (_dot_nt(_only_head(v_c, hi), dog) + dl_t[h:h + 1, :])
                res.append((dq, _dot_nn(ds_t, qcat), _dot_nn(p_t, dog)))
            for t, dst in enumerate((sl, ks, vs)):
                dn_ref[:, dst] = jnp.where(_lane_hi(), res[1][t], res[0][t])

    def at(shift, width):
        if shift < 0:
            return pl.BlockSpec((ATT_BLK, width), lambda b, r, n: (b * nb + jnp.maximum(n - 1, 0), r))
        if shift > 0:
            return pl.BlockSpec((ATT_BLK, width), lambda b, r, n: (b * nb + jnp.minimum(n + 1, nb - 1), r))
        return pl.BlockSpec((ATT_BLK, width), lambda b, r, n: (b * nb + n, r))

    return pl.pallas_call(
        body, name=name,
        grid=(n_seq, dil, nb),
        in_specs=[at(-1, QKV_W), at(0, QKV_W), at(1, QKV_W), at(0, ATT_OUT), at(1, ATT_OUT),
                  at(0, LANES), at(1, LANES), at(0, LANES), at(1, LANES), at(0, LANES), at(1, LANES)],
        out_specs=at(0, QKV_W),
        out_shape=jax.ShapeDtypeStruct((n_seq * length, dil * QKV_W), F32),
        compiler_params=_params(("parallel", "parallel", "arbitrary")),
    )(nq, nq, nq, do, do, lse, lse, wts, wts, rsum, rsum)


def qk_post(qkv, dn, gq_t, gk_t, rows, name):
    tb = min(PREP_ROWS, rows)

    def body(x_ref, dn_ref, gq_ref, gk_ref, o_ref, dgq_ref, dgk_ref):
        i = pl.program_id(0)
        qs, ks, vs = pl.ds(0, ATT_OUT), pl.ds(ATT_OUT, ATT_OUT), pl.ds(2 * ATT_OUT, ATT_OUT)
        dq, dgq = _head_rms_bwd(x_ref[:, qs], gq_ref[...], dn_ref[:, qs] * ATT_SCALE)
        dk, dgk = _head_rms_bwd(x_ref[:, ks], gk_ref[...], dn_ref[:, ks])
        o_ref[:, qs] = dq.astype(o_ref.dtype)
        o_ref[:, ks] = dk.astype(o_ref.dtype)
        o_ref[:, vs] = dn_ref[:, vs].astype(o_ref.dtype)

        @pl.when(i == 0)
        def _():
            dgq_ref[...] = dgq
            dgk_ref[...] = dgk

        @pl.when(i != 0)
        def _():
            dgq_ref[...] += dgq
            dgk_ref[...] += dgk

    gspec = pl.BlockSpec((1, ATT_OUT), lambda i: (0, 0))
    blk = pl.BlockSpec((tb, QKV_W), lambda i: (i, 0))
    return pl.pallas_call(
        body, name=name,
        grid=(rows // tb,),
        in_specs=[blk, blk, gspec, gspec],
        out_specs=[blk, gspec, gspec],
        out_shape=[jax.ShapeDtypeStruct((rows, QKV_W), MXU), jax.ShapeDtypeStruct((1, ATT_OUT), F32),
                   jax.ShapeDtypeStruct((1, ATT_OUT), F32)],
        compiler_params=_params(("arbitrary",)),
    )(qkv, dn, gq_t, gk_t)


OTHER_CHIPS = (4, 2, 6)


def exchange(items, name):
    n = len(items)

    def body(*refs):
        in_refs, out_refs = refs[:n], refs[n:2 * n]
        send_sems, recv_sems, local_sems = refs[2 * n:]
        x, y, c = lax.axis_index("x"), lax.axis_index("y"), lax.axis_index("c")
        me = 4 * x + 2 * y + c

        def peer(k):
            px = 1 - x if k & 4 else x
            py = 1 - y if k & 2 else y
            pc = 1 - c if k & 1 else c
            return (px, py, pc), 4 * px + 2 * py + pc

        def remote(t, k):
            dev, pid = peer(k)
            src = in_refs[t] if items[t][1] == "gather" else in_refs[t].at[pid]
            return pltpu.make_async_remote_copy(
                src_ref=src, dst_ref=out_refs[t].at[me], send_sem=send_sems.at[t, k], recv_sem=recv_sems.at[t, k],
                device_id=dev, device_id_type=pl.DeviceIdType.MESH)

        def arrival(t, k):
            dev, pid = peer(k)
            src = in_refs[t] if items[t][1] == "gather" else in_refs[t].at[pid]
            return pltpu.make_async_remote_copy(
                src_ref=src, dst_ref=out_refs[t].at[pid], send_sem=send_sems.at[t, k], recv_sem=recv_sems.at[t, k],
                device_id=dev, device_id_type=pl.DeviceIdType.MESH)

        def own(t):
            src = in_refs[t] if items[t][1] == "gather" else in_refs[t].at[me]
            return pltpu.make_async_copy(src, out_refs[t].at[me], local_sems.at[t])

        def forward(t, k, from_sibling):
            sib, _ = peer(1)
            _, pid = peer(k + 1 if from_sibling else k)
            slot = out_refs[t].at[pid]
            return pltpu.make_async_remote_copy(
                src_ref=slot, dst_ref=slot, send_sem=send_sems.at[t, k + 1], recv_sem=recv_sems.at[t, k + 1],
                device_id=sib, device_id_type=pl.DeviceIdType.MESH)

        def direct(t):
            return (1,) + OTHER_CHIPS if items[t][1] == "gather" else tuple(range(1, N_DEV))

        for t in range(n):
            own(t).start()
            for k in direct(t):
                remote(t, k).start()
        for t in range(n):
            if items[t][1] == "gather":
                for k in OTHER_CHIPS:
                    arrival(t, k).wait_recv()
                    forward(t, k, False).start()
        for t in range(n):
            if items[t][1] == "gather":
                arrival(t, 1).wait_recv()
                for k in OTHER_CHIPS:
                    forward(t, k, True).wait_recv()
            else:
                for k in direct(t):
                    arrival(t, k).wait_recv()
        for t in range(n):
            for k in direct(t):
                remote(t, k).wait_send()
            if items[t][1] == "gather":
                for k in OTHER_CHIPS:
                    forward(t, k, False).wait_send()
            own(t).wait()

    out_shape = []
    for arr, mode in items:
        shp = arr.shape if mode == "gather" else arr.shape[1:]
        out_shape.append(jax.ShapeDtypeStruct((N_DEV,) + tuple(shp), arr.dtype))
    anyspec = pl.BlockSpec(memory_space=pl.ANY)
    return pl.pallas_call(
        body, name=name,
        in_specs=[anyspec] * n,
        out_specs=[anyspec] * n,
        out_shape=out_shape,
        scratch_shapes=[pltpu.SemaphoreType.DMA((n, N_DEV)), pltpu.SemaphoreType.DMA((n, N_DEV)),
                        pltpu.SemaphoreType.DMA((n,))],
    )(*[a for a, _ in items])


N_CHIPS = N_DEV // 2


def sibling_swap(arrays, name):
    n = len(arrays)

    def body(*refs):
        in_refs, out_refs, send_sems, recv_sems = refs[:n], refs[n:2 * n], refs[2 * n], refs[2 * n + 1]
        x, y, c = lax.axis_index("x"), lax.axis_index("y"), lax.axis_index("c")

        def copy(t, q):
            return pltpu.make_async_remote_copy(
                src_ref=in_refs[t].at[2 * q + (1 - c)], dst_ref=out_refs[t].at[q],
                send_sem=send_sems.at[t, q], recv_sem=recv_sems.at[t, q],
                device_id=(x, y, 1 - c), device_id_type=pl.DeviceIdType.MESH)

        for t in range(n):
            for q in range(N_CHIPS):
                copy(t, q).start()
        for t in range(n):
            for q in range(N_CHIPS):
                copy(t, q).wait_recv()
        for t in range(n):
            for q in range(N_CHIPS):
                copy(t, q).wait_send()

    anyspec = pl.BlockSpec(memory_space=pl.ANY)
    return pl.pallas_call(
        body, name=name,
        in_specs=[anyspec] * n,
        out_specs=[anyspec] * n,
        out_shape=[jax.ShapeDtypeStruct((N_CHIPS,) + a.shape[1:], a.dtype) for a in arrays],
        scratch_shapes=[pltpu.SemaphoreType.DMA((n, N_CHIPS)), pltpu.SemaphoreType.DMA((n, N_CHIPS))],
    )(*arrays)


def chip_exchange(arrays, name):
    n = len(arrays)

    def body(*refs):
        in_refs, out_refs = refs[:n], refs[n:2 * n]
        send_sems, recv_sems, local_sems = refs[2 * n:]
        x, y, c = lax.axis_index("x"), lax.axis_index("y"), lax.axis_index("c")
        mine = 2 * x + y

        def peer(k):
            px = 1 - x if k & 4 else x
            py = 1 - y if k & 2 else y
            return (px, py, c), 2 * px + py

        def remote(t, k, arriving):
            dev, q = peer(k)
            return pltpu.make_async_remote_copy(
                src_ref=in_refs[t].at[q], dst_ref=out_refs[t].at[q if arriving else mine],
                send_sem=send_sems.at[t, k], recv_sem=recv_sems.at[t, k],
                device_id=dev, device_id_type=pl.DeviceIdType.MESH)

        def own(t):
            return pltpu.make_async_copy(in_refs[t].at[mine], out_refs[t].at[mine], local_sems.at[t])

        for t in range(n):
            own(t).start()
            for k in OTHER_CHIPS:
                remote(t, k, False).start()
        for t in range(n):
            for k in OTHER_CHIPS:
                remote(t, k, True).wait_recv()
        for t in range(n):
            for k in OTHER_CHIPS:
                remote(t, k, False).wait_send()
            own(t).wait()

    anyspec = pl.BlockSpec(memory_space=pl.ANY)
    return pl.pallas_call(
        body, name=name,
        in_specs=[anyspec] * n,
        out_specs=[anyspec] * n,
        out_shape=[jax.ShapeDtypeStruct(a.shape, a.dtype) for a in arrays],
        scratch_shapes=[pltpu.SemaphoreType.DMA((n, N_DEV)), pltpu.SemaphoreType.DMA((n, N_DEV)),
                        pltpu.SemaphoreType.DMA((n,))],
    )(*arrays)


def pair_add(a, b, name):
    _, r, c = a.shape
    rb = r if r <= 512 else (128 if c > 1024 else 256)
    assert r % rb == 0

    def body(a_ref, b_ref, o_ref):
        o_ref[...] = (a_ref[...].astype(F32) + b_ref[...].astype(F32)).astype(o_ref.dtype)

    blk = pl.BlockSpec((1, rb, c), lambda q, i: (q, i, 0))
    return pl.pallas_call(
        body, name=name,
        grid=(N_CHIPS, r // rb),
        in_specs=[blk, blk],
        out_specs=blk,
        out_shape=jax.ShapeDtypeStruct(a.shape, a.dtype),
        compiler_params=_params(("parallel", "parallel")),
    )(a, b)


def adamw(parts, w, m, v, name):
    r, c = w.shape[-2:]
    n_parts = parts.shape[0]
    rb = r if r <= 512 else (128 if c > 1024 else 256)
    assert r % rb == 0

    def body(p_ref, w_ref, m_ref, v_ref, g_out, d_out, m_out, v_out):
        g = p_ref[0].astype(F32)
        for i in range(1, n_parts):
            g = g + p_ref[i].astype(F32)
        m_new = ADAM_B1 * m_ref[...] + (1.0 - ADAM_B1) * g
        v_new = ADAM_B2 * v_ref[...] + (1.0 - ADAM_B2) * (g * g)
        m_hat = m_new / (1.0 - ADAM_B1 ** ADAM_STEP)
        v_hat = v_new / (1.0 - ADAM_B2 ** ADAM_STEP)
        g_out[...] = g
        d_out[...] = -ADAM_LR * (m_hat / (jnp.sqrt(v_hat) + ADAM_EPS) + ADAM_WD * w_ref[...])
        m_out[...] = m_new
        v_out[...] = v_new

    if w.ndim == 3:
        blk = pl.BlockSpec((None, rb, c), lambda i: (0, i, 0))
    else:
        blk = pl.BlockSpec((rb, c), lambda i: (i, 0))
    return pl.pallas_call(
        body, name=name,
        grid=(r // rb,),
        in_specs=[pl.BlockSpec((n_parts, rb, c), lambda i: (0, i, 0)), blk, blk, blk],
        out_specs=[blk] * 4,
        out_shape=[jax.ShapeDtypeStruct(w.shape, F32)] * 4,
        compiler_params=_params(("parallel",)),
    )(parts, w, m, v)


def _pad_lanes(vec, n=LANES):
    return jnp.pad(vec, ((0, 0), (0, n - vec.shape[1])))


COL_SHARDED = ("w_in", "ssd_conv_w", "w_attn_proj", "w_up", "ffn_conv_w")
ROW_SHARDED = ("w_ssd_proj", "w_out", "w_down")
MATRICES = ("w_in", "w_attn_proj", "w_up", "w_ssd_proj", "w_out", "w_down")
LATE = ("w_ssd_proj", "w_attn_proj", "w_out", "w_up", "ffn_conv_w", "w_down")


def _narrow(name, a):
    return a.astype(MXU) if name in MATRICES else a


def _from_gathered(name, g):
    if name in COL_SHARDED:
        return jnp.transpose(g, (1, 0, 2)).reshape(g.shape[1], N_DEV * g.shape[2])
    return g.reshape(N_DEV * g.shape[1], g.shape[2])


def _to_slabs(name, g):
    if name in COL_SHARDED:
        return jnp.transpose(g.reshape(g.shape[0], N_DEV, g.shape[1] // N_DEV), (1, 0, 2))
    return g.reshape(N_DEV, g.shape[0] // N_DEV, g.shape[1])


def _columns(m, a, b):
    if m.ndim == 2:
        return m[:, a:b]
    c = m.shape[2]
    cuts = [m[j][:, max(a - j * c, 0):min(b - j * c, c)] for j in range(a // c, (b - 1) // c + 1)]
    return cuts[0] if len(cuts) == 1 else jnp.concatenate(cuts, axis=1)


def _column_shards(pieces, c):
    shards = []
    for j in range(N_DEV):
        cuts = []
        for start, arr in pieces:
            lo, hi = max(j * c - start, 0), min((j + 1) * c - start, arr.shape[1])
            if lo < hi:
                cuts.append(arr[:, lo:hi])
        shards.append(cuts[0] if len(cuts) == 1 else jnp.concatenate(cuts, axis=1))
    return jnp.stack(shards)


def local_step(x, target, w, late=None):
    n_seq, seq, _ = x.shape
    rows = n_seq * seq
    x = x.reshape(rows, D_MODEL)
    target = target.reshape(rows, D_MODEL)
    mx = lambda a: a.astype(MXU)

    splits = [sum(IN_WIDTHS[:i]) for i in range(len(IN_WIDTHS) + 1)]
    w_in = w["w_in"]
    part = lambda i: _columns(w_in, splits[i], splits[i + 1])
    w_z, w_xbc, w_gs, w_ga = mx(part(0)), mx(part(1)), mx(part(6)), mx(part(7))
    w_dt = mx(_pad_lanes(part(2)))
    head_group = lambda t, g: (splits[3 + t] + g * ATT_OUT, splits[3 + t] + (g + 1) * ATT_OUT)
    w_qkv = [mx(jnp.concatenate([_columns(w_in, *head_group(t, g)) for t in range(3)], axis=1))
             for g in range(ATT_GROUPS)]
    conv_w, conv_b, fconv_b = w["ssd_conv_w"], w["ssd_conv_b"], w["ffn_conv_b"]
    dt_bias, a_log, d_skip = _pad_lanes(w["dt_bias"]), _pad_lanes(w["a_log"]), _pad_lanes(w["d_skip"])
    g1, g2, gn, gq, gk = w["norm1_g"], w["norm2_g"], w["ssd_norm_g"], w["q_norm_g"], w["k_norm_g"]

    tb = min(512, seq)
    tbm = min(256, seq)
    cw = 1024
    rw = lambda fn, name, ncol, ins, params=(), outs=(), accs=(), tb_=tb: rowwise(
        fn, name, rows, seq, tb_, ncol, ins, params, outs, accs)

    (h,) = rw(lambda ctx, xv, g: _rms_fwd(xv, g), "rms1_fwd", 1, [(x, D_MODEL, 0, None)], [(g1, None, 0)],
              [(D_MODEL, D_MODEL, 0, MXU)])
    z = matmul(h, w_z, "mm_z")
    xbc = matmul(h, w_xbc, "mm_xbc")
    dtraw = matmul(h, w_dt, "mm_dt")
    qkv = [matmul(h, w_qkv[g], f"mm_qkv{g}") for g in range(ATT_GROUPS)]
    gs = matmul(h, w_gs, "mm_gs")
    ga = matmul(h, w_ga, "mm_ga")

    def conv_silu(ctx, xh, wv, bv):
        return _silu(bv + _conv_prev(xh[0], xh[1], wv, ctx.first, SSD_CONV))

    (xact,) = rw(conv_silu, "ssd_conv_fwd", CONV_DIM // cw, [(xbc, cw, 0, "prev")],
                 [(conv_w, cw, 0), (conv_b, cw, 0)], [(CONV_DIM, cw, 0, F32)])
    if late is None:
        y, sin = ssd_fwd(xact, dtraw, dt_bias, a_log, d_skip, n_seq, seq)
    else:
        y, sin, *gathered = ssd_fwd(xact, dtraw, dt_bias, a_log, d_skip, n_seq, seq,
                                    comm=[(late[n], "gather") for n in LATE])
        w = {**w, **{n: g if n == "w_up" else _from_gathered(n, g) for n, g in zip(LATE, gathered)}}
    w_sp, w_ap, w_o, w_d = mx(w["w_ssd_proj"]), mx(w["w_attn_proj"]), mx(w["w_out"]), mx(w["w_down"])
    w_ug, w_uv = mx(_columns(w["w_up"], 0, D_FF)), mx(_columns(w["w_up"], D_FF, 2 * D_FF))
    fconv_w = w["ffn_conv_w"]

    def gated_norm(ctx, yv, zv, g):
        yz = yv * _silu(zv)
        return jnp.concatenate([_rms_fwd(yz[:, i:i + NORM_GROUP], g[:, i:i + NORM_GROUP])
                                for i in range(0, cw, NORM_GROUP)], axis=1)

    (y_ssd,) = rw(gated_norm, "ssd_post_fwd", D_INNER // cw, [(y, cw, 0, None), (z, cw, 0, None)], [(gn, cw, 0)],
                  [(D_INNER, cw, 0, MXU)])

    gq_t, gk_t = jnp.tile(gq, (1, ATT_H)), jnp.tile(gk, (1, ATT_H))
    dilated = lambda a, g: _dilated(a, n_seq, seq, ATT_DILATIONS[g])
    nq = [dilated(qk_prep(qkv[g], gq_t, gk_t, rows, f"qk_prep{g}"), g) for g in range(ATT_GROUPS)]
    att = [attn_fwd2(nq[g], n_seq, seq, ATT_DILATIONS[g], f"attn_fwd{g}") for g in range(ATT_GROUPS)]

    def combine(ctx, o0, o1, o2, l0, l1, l2):
        mxl = jnp.maximum(jnp.maximum(l0, l1), l2)
        e = [jnp.exp(l - mxl) for l in (l0, l1, l2)]
        inv = 1.0 / (e[0] + e[1] + e[2])
        ws = [ei * inv for ei in e]
        out = sum(_expand_heads(wi) * oi for wi, oi in zip(ws, (o0, o1, o2)))
        return (out, *ws)

    y_attn, wt0, wt1, wt2 = rw(
        combine, "attn_combine", 1,
        [(att[g][0].reshape(rows, ATT_OUT), ATT_OUT, 0, None) for g in range(3)]
        + [(att[g][1].reshape(rows, LANES), LANES, 0, None) for g in range(3)], [],
        [(ATT_OUT, ATT_OUT, 0, F32)] + [(LANES, LANES, 0, F32)] * 3)
    wts = (wt0, wt1, wt2)

    ps = matmul(y_ssd, w_sp, "mm_ssd_proj")
    pa = matmul(y_attn, w_ap, "mm_attn_proj")
    (merged,) = rw(lambda ctx, a, b, c, d: _sigmoid(c) * a + _sigmoid(d) * b, "merge_fwd", D_MODEL // cw,
                   [(ps, cw, 0, None), (pa, cw, 0, None), (gs, cw, 0, None), (ga, cw, 0, None)], [],
                   [(D_MODEL, cw, 0, MXU)])
    x1 = matmul(merged, w_o, "mm_out", add=x)
    (h2,) = rw(lambda ctx, xv, g: _rms_fwd(xv, g), "rms2_fwd", 1, [(x1, D_MODEL, 0, None)], [(g2, None, 0)],
               [(D_MODEL, D_MODEL, 0, MXU)])
    up_g = matmul(h2, w_ug, "mm_up_g")
    up_v = matmul(h2, w_uv, "mm_up_v")
    fw = D_FF // 2
    nfc = D_FF // fw

    def mlp_act(ctx, ug, uv, wg, wv, bg, bv):
        cg = bg + _conv_prev(ug[0], ug[1], wg, ctx.first, FFN_CONV)
        cv = bv + _conv_prev(uv[0], uv[1], wv, ctx.first, FFN_CONV)
        return _silu(cg) * cv

    (act,) = rw(mlp_act, "mlp_act_fwd", nfc, [(up_g, fw, 0, "prev"), (up_v, fw, 0, "prev")],
                [(fconv_w, fw, 0), (fconv_w, fw, nfc), (fconv_b, fw, 0), (fconv_b, fw, nfc)], [(D_FF, fw, 0, MXU)],
                tb_=tbm)
    x2 = matmul(act, w_d, "mm_down", add=x1)

    def loss_fn(ctx, xv, tv):
        d = xv - tv
        g = d * (1.0 / D_MODEL)
        return g, g, jnp.sum(d * d, axis=0, keepdims=True)

    dx2, dx2_m, sq = rw(loss_fn, "loss", 1, [(x2, D_MODEL, 0, None), (target, D_MODEL, 0, None)], [],
                        [(D_MODEL, D_MODEL, 0, F32), (D_MODEL, D_MODEL, 0, MXU)], [(1, D_MODEL)])

    grads = {}
    dact = matmul(dx2_m, w_d, "mm_d_act", tb=True)
    grads["w_down"] = matmul(act, dx2_m, "mm_dw_down", ta=True, out_dtype=MXU)

    def mlp_bwd(ctx, da, ug, uv, wg, wv, bg, bv):
        cg, cg_n = _conv_pre(ug, wg, bg, ctx.first, FFN_CONV)
        cv, cv_n = _conv_pre(uv, wv, bv, ctx.first, FFN_CONV)
        da_c, da_n = da
        dup_g_, dwg, dbg = _conv_bwd(da_c * cv * _silu_grad(cg), da_n * cv_n * _silu_grad(cg_n), ug, wg, ctx, FFN_CONV)
        dup_v_, dwv, dbv = _conv_bwd(da_c * _silu(cg), da_n * _silu(cg_n), uv, wv, ctx, FFN_CONV)
        return dup_g_, dup_v_, dwg, dbg, dwv, dbv

    dup_g, dup_v, dfw_g, dfb_g, dfw_v, dfb_v = rw(
        mlp_bwd, "mlp_bwd", nfc, [(dact, fw, 0, "next"), (up_g, fw, 0, "both"), (up_v, fw, 0, "both")],
        [(fconv_w, fw, 0), (fconv_w, fw, nfc), (fconv_b, fw, 0), (fconv_b, fw, nfc)],
        [(D_FF, fw, 0, MXU), (D_FF, fw, 0, MXU)], [(FFN_CONV, fw), (1, fw), (FFN_CONV, fw), (1, fw)], tb_=tbm)
    grads["ffn_conv_w"] = jnp.concatenate([dfw_g, dfw_v], axis=1)
    grads["ffn_conv_b"] = jnp.concatenate([dfb_g, dfb_v], axis=1)
    dh2 = matmul(dup_g, w_ug, "mm_dh2_g", tb=True)
    dh2 = matmul(dup_v, w_uv, "mm_dh2_v", tb=True, add=dh2)
    dw_up = [(0, matmul(h2, dup_g, "mm_dw_up_g", ta=True, out_dtype=MXU)),
             (D_FF, matmul(h2, dup_v, "mm_dw_up_v", ta=True, out_dtype=MXU))]
    if w["w_up"].ndim == 3:
        grads["w_up"] = _column_shards(dw_up, w["w_up"].shape[2])
    else:
        grads["w_up"] = jnp.concatenate([p for _, p in dw_up], axis=1)

    def rms_bwd_fn(ctx, xv, dh_, dres, g):
        dxv, dg = _rms_bwd(xv, g, dh_)
        return dres + dxv, dg

    def rms_bwd_fn2(ctx, xv, dh_, dres, g):
        dxv, dg = rms_bwd_fn(ctx, xv, dh_, dres, g)
        return dxv, dxv, dg

    dx1, dx1_m, grads["norm2_g"] = rw(
        rms_bwd_fn2, "rms2_bwd", 1, [(x1, D_MODEL, 0, None), (dh2, D_MODEL, 0, None), (dx2, D_MODEL, 0, None)],
        [(g2, None, 0)], [(D_MODEL, D_MODEL, 0, F32), (D_MODEL, D_MODEL, 0, MXU)], [(1, D_MODEL)])

    dmerged = matmul(dx1_m, w_o, "mm_d_merged", tb=True)
    grads["w_out"] = matmul(merged, dx1_m, "mm_dw_out", ta=True, out_dtype=MXU)

    def merge_bwd(ctx, dm, a, b, c, d):
        sc, sd = _sigmoid(c), _sigmoid(d)
        return dm * sc, dm * sd, dm * a * sc * (1.0 - sc), dm * b * sd * (1.0 - sd)

    dps, dpa, dgs, dga = rw(merge_bwd, "merge_bwd", D_MODEL // cw,
                            [(dmerged, cw, 0, None), (ps, cw, 0, None), (pa, cw, 0, None), (gs, cw, 0, None),
                             (ga, cw, 0, None)], [], [(D_MODEL, cw, 0, MXU)] * 4)
    dy_ssd = matmul(dps, w_sp, "mm_d_y_ssd", tb=True)
    grads["w_ssd_proj"] = matmul(y_ssd, dps, "mm_dw_ssd_proj", ta=True, out_dtype=MXU)
    dy_attn = matmul(dpa, w_ap, "mm_d_y_attn", tb=True)
    grads["w_attn_proj"] = matmul(y_attn, dpa, "mm_dw_attn_proj", ta=True, out_dtype=MXU)

    (rsum,) = rw(lambda ctx, a, b: _reduce_heads(a * b), "attn_rsum", 1,
                 [(dy_attn, ATT_OUT, 0, None), (y_attn, ATT_OUT, 0, None)], [], [(LANES, LANES, 0, F32)])
    dqkv, dgq, dgk = [], 0.0, 0.0
    for g in range(ATT_GROUPS):
        dn = attn_bwd2(nq[g], dilated(dy_attn, g), att[g][1], dilated(wts[g], g), dilated(rsum, g), n_seq, seq,
                       ATT_DILATIONS[g], f"attn_bwd{g}")
        d_, a_, b_ = qk_post(qkv[g], dn.reshape(rows, QKV_W), gq_t, gk_t, rows, f"qk_post{g}")
        dqkv.append(d_)
        dgq, dgk = dgq + a_, dgk + b_
    per_head = lambda v: jnp.sum(v.reshape(ATT_H, ATT_HD), axis=0, keepdims=True)
    grads["q_norm_g"], grads["k_norm_g"] = per_head(dgq), per_head(dgk)

    def gated_norm_bwd(ctx, dyn, yv, zv, g):
        sz = _silu(zv)
        yz = yv * sz
        dyz, dgs_ = [], []
        for i in range(0, cw, NORM_GROUP):
            a, b = _rms_bwd(yz[:, i:i + NORM_GROUP], g[:, i:i + NORM_GROUP], dyn[:, i:i + NORM_GROUP])
            dyz.append(a)
            dgs_.append(b)
        dyz = jnp.concatenate(dyz, axis=1)
        return dyz * sz, dyz * yv * _silu_grad(zv), jnp.concatenate(dgs_, axis=1)

    dy, dz, grads["ssd_norm_g"] = rw(gated_norm_bwd, "ssd_post_bwd", D_INNER // cw,
                                     [(dy_ssd, cw, 0, None), (y, cw, 0, None), (z, cw, 0, None)], [(gn, cw, 0)],
                                     [(D_INNER, cw, 0, F32), (D_INNER, cw, 0, MXU)], [(1, cw)])
    if late is None:
        dxact, ddt, dbias, dalog, ddskip = ssd_bwd(xact, dtraw, dt_bias, a_log, d_skip, sin, dy, n_seq, seq)
    else:
        dxact, ddt, dbias, dalog, ddskip, *parts = ssd_bwd(
            xact, dtraw, dt_bias, a_log, d_skip, sin, dy, n_seq, seq,
            comm=[(grads[n] if n == "w_up" else _to_slabs(n, _narrow(n, grads[n])), "scatter") for n in LATE])
        grads.update(zip(LATE, parts))
    grads["dt_bias"], grads["a_log"], grads["d_skip"] = dbias[:, :SSD_H], dalog[:, :SSD_H], ddskip[:, :SSD_H]

    def conv_silu_bwd(ctx, dxa, xin, wv, bv):
        pre, pre_n = _conv_pre(xin, wv, bv, ctx.first, SSD_CONV)
        return _conv_bwd(dxa[0] * _silu_grad(pre), dxa[1] * _silu_grad(pre_n), xin, wv, ctx, SSD_CONV)

    dxbc, grads["ssd_conv_w"], grads["ssd_conv_b"] = rw(
        conv_silu_bwd, "ssd_conv_bwd", CONV_DIM // cw, [(dxact, cw, 0, "next"), (xbc, cw, 0, "both")],
        [(conv_w, cw, 0), (conv_b, cw, 0)], [(CONV_DIM, cw, 0, MXU)], [(SSD_CONV, cw), (1, cw)])

    pieces = [(dz, w_z, "z"), (dxbc, w_xbc, "xbc"), (ddt, w_dt, "dt"), (dgs, w_gs, "gs"), (dga, w_ga, "ga")]
    pieces += [(dqkv[g], w_qkv[g], f"qkv{g}") for g in range(ATT_GROUPS)]
    dh, dws = None, {}
    for dpart, wpart, tag in pieces:
        dh = matmul(dpart, wpart, f"mm_dh_{tag}", tb=True, add=dh)
        dws[tag] = matmul(h, dpart, f"mm_dw_{tag}", ta=True, out_dtype=MXU)
    dw_in = [(splits[0], dws["z"]), (splits[1], dws["xbc"]), (splits[2], dws["dt"][:, :SSD_H])]
    dw_in += [(head_group(t, g)[0], dws[f"qkv{g}"][:, t * ATT_OUT:(t + 1) * ATT_OUT])
              for t in range(3) for g in range(ATT_GROUPS)]
    dw_in += [(splits[6], dws["gs"]), (splits[7], dws["ga"])]
    if w_in.ndim == 3:
        grads["w_in"] = _column_shards(dw_in, w_in.shape[2])
    else:
        grads["w_in"] = jnp.concatenate([p for _, p in dw_in], axis=1)
    grad_x, grads["norm1_g"] = rw(rms_bwd_fn, "rms1_bwd", 1,
                                  [(x, D_MODEL, 0, None), (dh, D_MODEL, 0, None), (dx1, D_MODEL, 0, None)],
                                  [(g1, None, 0)], [(D_MODEL, D_MODEL, 0, F32)], [(1, D_MODEL)])
    return sq, grad_x.reshape(n_seq, seq, D_MODEL), grads


EARLY = ("w_in", "ssd_conv_w")
REPLICATED = ("norm1_g", "ssd_conv_b", "dt_bias", "a_log", "d_skip", "ssd_norm_g", "q_norm_g", "k_norm_g",
              "norm2_g", "ffn_conv_b")
WEIGHTS = ("norm1_g", "w_in", "ssd_conv_w", "ssd_conv_b", "dt_bias", "a_log", "d_skip", "ssd_norm_g", "w_ssd_proj",
           "q_norm_g", "k_norm_g", "w_attn_proj", "w_out", "norm2_g", "w_up", "ffn_conv_w", "ffn_conv_b", "w_down")
PACK_ROWS, PACK_COLS = 8, 2048


def _pack(vals):
    flat = jnp.concatenate([vals[n].reshape(-1) for n in REPLICATED])
    return jnp.pad(flat, (0, PACK_ROWS * PACK_COLS - flat.shape[0])).reshape(PACK_ROWS, PACK_COLS)


def _unpack(packed, like):
    flat = packed.reshape(-1)
    out, pos = {}, 0
    for n in REPLICATED:
        size = like[n].size
        out[n] = flat[pos:pos + size].reshape(like[n].shape)
        pos += size
    return out


def step(x, target, w_raw, m_raw, v_raw):
    wsh = {n: a[0] if a.ndim == 3 else a for n, a in w_raw.items()}
    gathered = exchange([(_narrow(n, wsh[n]), "gather") for n in EARLY], "ag_weights")
    full = {n: wsh[n] for n in REPLICATED}
    full.update({n: g if n == "w_in" else _from_gathered(n, g) for n, g in zip(EARLY, gathered)})

    sq, grad_x, grads = local_step(x, target, full, late={n: _narrow(n, wsh[n]) for n in LATE})

    slabs = [grads[n] if n == "w_in" else _to_slabs(n, _narrow(n, grads[n])) for n in EARLY]
    packed_g = _pack({n: grads[n] for n in REPLICATED})
    core = lax.axis_index("c")
    from_sibling = sibling_swap(slabs, "rs_sibling")
    chip_parts = []
    for n, s, f in zip(EARLY, slabs, from_sibling):
        mine = lax.dynamic_index_in_dim(s.reshape((N_CHIPS, 2) + s.shape[1:]), core, axis=1, keepdims=False)
        chip_parts.append(pair_add(mine, f, f"rs_add_{n}"))
    received = dict(zip(EARLY, chip_exchange(chip_parts, "rs_chips")))
    received.update({n: grads[n] for n in LATE})
    (small,) = exchange([(packed_g, "gather")], "ag_small")

    out_g, out_d, out_m, out_v = {}, {}, {}, {}
    for n, parts in received.items():
        out_g[n], out_d[n], out_m[n], out_v[n] = adamw(parts, w_raw[n], m_raw[n], v_raw[n], f"adamw_{n}")
    pk = adamw(small, _pack(w_raw), _pack(m_raw), _pack(v_raw), "adamw_small")
    for dst, packed in zip((out_g, out_d, out_m, out_v), pk):
        dst.update(_unpack(packed, w_raw))
    loss = lax.psum(0.5 * jnp.sum(sq) / D_MODEL, ("x", "y", "c"))
    return loss, grad_x, out_g, out_d, out_m, out_v


def kernel(x, norm1_g, w_in, ssd_conv_w, ssd_conv_b, dt_bias, a_log, d_skip, ssd_norm_g, w_ssd_proj, q_norm_g, k_norm_g, w_attn_proj, w_out, norm2_g, w_up, ffn_conv_w, ffn_conv_b, w_down, loss_target, m_norm1_g, m_w_in, m_ssd_conv_w, m_ssd_conv_b, m_dt_bias, m_a_log, m_d_skip, m_ssd_norm_g, m_w_ssd_proj, m_q_norm_g, m_k_norm_g, m_w_attn_proj, m_w_out, m_norm2_g, m_w_up, m_ffn_conv_w, m_ffn_conv_b, m_w_down, v_norm1_g, v_w_in, v_ssd_conv_w, v_ssd_conv_b, v_dt_bias, v_a_log, v_d_skip, v_ssd_norm_g, v_w_ssd_proj, v_q_norm_g, v_k_norm_g, v_w_attn_proj, v_w_out, v_norm2_g, v_w_up, v_ffn_conv_w, v_ffn_conv_b, v_w_down):
    ws = (norm1_g, w_in, ssd_conv_w, ssd_conv_b, dt_bias, a_log, d_skip, ssd_norm_g, w_ssd_proj, q_norm_g, k_norm_g,
          w_attn_proj, w_out, norm2_g, w_up, ffn_conv_w, ffn_conv_b, w_down)
    ms = (m_norm1_g, m_w_in, m_ssd_conv_w, m_ssd_conv_b, m_dt_bias, m_a_log, m_d_skip, m_ssd_norm_g, m_w_ssd_proj,
          m_q_norm_g, m_k_norm_g, m_w_attn_proj, m_w_out, m_norm2_g, m_w_up, m_ffn_conv_w, m_ffn_conv_b, m_w_down)
    vs = (v_norm1_g, v_w_in, v_ssd_conv_w, v_ssd_conv_b, v_dt_bias, v_a_log, v_d_skip, v_ssd_norm_g, v_w_ssd_proj,
          v_q_norm_g, v_k_norm_g, v_w_attn_proj, v_w_out, v_norm2_g, v_w_up, v_ffn_conv_w, v_ffn_conv_b, v_w_down)
    loss, grad_x, g, d, m, v = step(x, loss_target, dict(zip(WEIGHTS, ws)), dict(zip(WEIGHTS, ms)), dict(zip(WEIGHTS, vs)))
    ordered = lambda dct: [dct[n] for n in WEIGHTS]
    return (loss, grad_x, *ordered(g), *ordered(d), *ordered(m), *ordered(v))
```

```python
import jax
import jax.numpy as jnp
from jax import lax
from jax.experimental import pallas as pl
from jax.experimental.pallas import tpu as pltpu

F32 = jnp.float32
BF16 = jnp.bfloat16
MXU = jnp.bfloat16
HIGHEST = lax.Precision.HIGHEST
VMEM_LIMIT_BYTES = 48 * 1024 * 1024
SUBLANES = 8
LANES = 128
N_DEV = 8

D_MODEL = 1024
D_INNER = 2048
SSD_P = 64
SSD_H = 32
SSD_G = 8
SSD_K = SSD_H // SSD_G
SSD_N = 128
SSD_Q = 128
SSD_CONV = 4
CONV_DIM = D_INNER + 2 * SSD_G * SSD_N
NORM_GROUP = D_INNER // SSD_G
ATT_GROUPS = 3
ATT_H = 8
ATT_HD = 64
ATT_BLK = 128
ATT_OUT = ATT_H * ATT_HD
ATT_DILATIONS = (1, 4, 16)
ATT_SCALE = ATT_HD ** -0.5
D_FF = 2816
FFN_CONV = 3
EPS = 1e-6
NEG = -1e30
IN_WIDTHS = (D_INNER, CONV_DIM, SSD_H, 3 * ATT_OUT, 3 * ATT_OUT, 3 * ATT_OUT, D_MODEL, D_MODEL)

ADAM_LR = 0.001
ADAM_B1 = 0.9
ADAM_B2 = 0.999
ADAM_EPS = 1e-08
ADAM_WD = 0.01
ADAM_STEP = 10


def _mm(a, b, dims):
    return lax.dot_general(a.astype(MXU), b.astype(MXU), (dims, ((), ())), preferred_element_type=F32)


def _dot_nn(a, b):
    return _mm(a, b, ((1,), (0,)))


def _dot_nt(a, b):
    return _mm(a, b, ((1,), (1,)))


def _dot_tn(a, b):
    return _mm(a, b, ((0,), (0,)))


def _dot_f32(a, b):
    return lax.dot_general(a, b, (((1,), (0,)), ((), ())), precision=HIGHEST, preferred_element_type=F32)


def _sigmoid(x):
    return 1.0 / (1.0 + jnp.exp(-x))


def _silu(x):
    return x * _sigmoid(x)


def _silu_grad(x):
    s = _sigmoid(x)
    return s * (1.0 + x * (1.0 - s))


def _softplus(x):
    return jnp.maximum(x, 0.0) + jnp.log(1.0 + jnp.exp(-jnp.abs(x)))


def _rms_fwd(x, g):
    r = lax.rsqrt(jnp.mean(x * x, axis=-1, keepdims=True) + EPS)
    return x * r * g


def _rms_bwd(x, g, dy):
    r = lax.rsqrt(jnp.mean(x * x, axis=-1, keepdims=True) + EPS)
    xh = x * r
    dyg = dy * g
    dx = r * (dyg - xh * jnp.mean(dyg * xh, axis=-1, keepdims=True))
    return dx, jnp.sum(dy * xh, axis=0, keepdims=True)


def _onehot_row(h, n=LANES):
    return (lax.broadcasted_iota(jnp.int32, (1, n), 1) == h).astype(F32)


def _onehot_col(h, n=LANES):
    return (lax.broadcasted_iota(jnp.int32, (n, 1), 0) == h).astype(F32)


def _head_expand_matrix():
    r = lax.broadcasted_iota(jnp.int32, (LANES, ATT_OUT), 0)
    c = lax.broadcasted_iota(jnp.int32, (LANES, ATT_OUT), 1)
    return (c // ATT_HD == r).astype(F32)


def _split_bf16(x, parts):
    out = []
    for _ in range(parts - 1):
        hi = x.astype(BF16).astype(F32)
        out.append(hi)
        x = x - hi
    out.append(x)
    return out


def _expand_heads(w):
    e = _head_expand_matrix()
    return sum(_dot_nn(p, e) for p in _split_bf16(w, 2))


def _reduce_heads(x):
    e = _head_expand_matrix()
    return sum(_dot_nt(p, e) for p in _split_bf16(x, 3))


def _shift_prev(cur, halo, s, first):
    if s == 0:
        return cur
    rolled = pltpu.roll(cur, s, 0)
    hr = jnp.where(first, 0.0, pltpu.roll(halo, s, 0))
    rows = lax.broadcasted_iota(jnp.int32, halo.shape, 0)
    head = jnp.where(rows < s, hr, rolled[:SUBLANES])
    if cur.shape[0] == SUBLANES:
        return head
    return jnp.concatenate([head, rolled[SUBLANES:]], axis=0)


def _shift_next(cur, halo, s, last):
    if s == 0:
        return cur
    tb = cur.shape[0]
    rolled = pltpu.roll(cur, tb - s, 0)
    hr = jnp.where(last, 0.0, pltpu.roll(halo, SUBLANES - s, 0))
    rows = lax.broadcasted_iota(jnp.int32, halo.shape, 0)
    tail = jnp.where(rows >= SUBLANES - s, hr, rolled[tb - SUBLANES:])
    return jnp.concatenate([rolled[:tb - SUBLANES], tail], axis=0)


def _conv_prev(x, halo, w, first, taps):
    acc = None
    for i in range(taps):
        term = w[i:i + 1, :] * _shift_prev(x, halo, taps - 1 - i, first)
        acc = term if acc is None else acc + term
    return acc


def _conv_pre(x, w, b, first, taps):
    cur, prev8, next8 = x
    tail = cur[cur.shape[0] - SUBLANES:]
    return b + _conv_prev(cur, prev8, w, first, taps), b + _conv_prev(next8, tail, w, False, taps)


def _conv_bwd(dpre, dpre_next8, x, w, ctx, taps):
    cur, prev8, _ = x
    dx, dws = None, []
    for i in range(taps):
        term = w[i:i + 1, :] * _shift_next(dpre, dpre_next8, taps - 1 - i, ctx.last)
        dx = term if dx is None else dx + term
        dws.append(jnp.sum(dpre * _shift_prev(cur, prev8, taps - 1 - i, ctx.first), axis=0, keepdims=True))
    return dx, jnp.concatenate(dws, axis=0), jnp.sum(dpre, axis=0, keepdims=True)


def _params(sem):
    return pltpu.CompilerParams(dimension_semantics=sem, vmem_limit_bytes=VMEM_LIMIT_BYTES)


MATMUL_VMEM_BUDGET = 34 * 1024 * 1024


V7X_MXU_FLOPS = 996e12
V7X_HBM_BYTES_PER_S = 3.4e12
GRID_STEP_S = 0.35e-6


def _tile_sizes(dim, cap):
    return [t for t in range(LANES, min(dim, cap) + 1, LANES) if dim % t == 0] or [dim]


def _matmul_tiles(m, n, k, a_bytes, b_bytes, add_bytes, out_bytes):
    best = None
    for tk in _tile_sizes(k, 8192):
        nk = k // tk
        for tn in _tile_sizes(n, 2048):
            for tm in _tile_sizes(m, 2048):
                io = tm * tk * a_bytes + tk * tn * b_bytes
                ends = tm * tn * (add_bytes + out_bytes)
                need = 2 * (io + ends) + tm * tn * 4 * (2 if nk > 1 else 1)
                if need > MATMUL_VMEM_BUDGET:
                    continue
                step = max(2.0 * tm * tn * tk / V7X_MXU_FLOPS, (io + ends / nk) / V7X_HBM_BYTES_PER_S)
                if nk > 1:
                    step += tm * tn * 8 / V7X_HBM_BYTES_PER_S
                cost = (m // tm) * (n // tn) * nk * (step + GRID_STEP_S)
                if best is None or cost < best[0]:
                    best = (cost, tm, tn, tk)
    if best is None:
        raise ValueError((m, n, k))
    return best[1:]


def matmul(a, b, name, ta=False, tb=False, add=None, out_dtype=F32):
    assert not (ta and tb)
    m, k = (a.shape[1], a.shape[0]) if ta else a.shape
    n = b.shape[0] if tb else b.shape[1]
    assert (b.shape[1] if tb else b.shape[0]) == k
    tm, tn, tk = _matmul_tiles(m, n, k, a.dtype.itemsize, b.dtype.itemsize,
                               0 if add is None else add.dtype.itemsize, jnp.dtype(out_dtype).itemsize)
    nk = k // tk
    dims = ((0,), (0,)) if ta else (((1,), (1,)) if tb else ((1,), (0,)))

    def body(*refs):
        if add is None:
            a_ref, b_ref, o_ref = refs[:3]
        else:
            a_ref, b_ref, add_ref, o_ref = refs[:4]

        def finish(r):
            if add is not None:
                r = r + add_ref[...].astype(F32)
            o_ref[...] = r.astype(out_dtype)

        if nk == 1:
            finish(_mm(a_ref[...], b_ref[...], dims))
            return
        acc = refs[-1]
        kk = pl.program_id(2)

        @pl.when(kk == 0)
        def _():
            acc[...] = jnp.zeros_like(acc)

        acc[...] += _mm(a_ref[...], b_ref[...], dims)

        @pl.when(kk == nk - 1)
        def _():
            finish(acc[...])

    a_spec = pl.BlockSpec((tk, tm), lambda i, j, kk: (kk, i)) if ta else pl.BlockSpec((tm, tk), lambda i, j, kk: (i, kk))
    b_spec = pl.BlockSpec((tn, tk), lambda i, j, kk: (j, kk)) if tb else pl.BlockSpec((tk, tn), lambda i, j, kk: (kk, j))
    in_specs = [a_spec, b_spec]
    args = [a, b]
    if add is not None:
        in_specs.append(pl.BlockSpec((tm, tn), lambda i, j, kk: (i, j)))
        args.append(add)
    return pl.pallas_call(
        body, name=name,
        grid=(m // tm, n // tn, nk),
        in_specs=in_specs,
        out_specs=pl.BlockSpec((tm, tn), lambda i, j, kk: (i, j)),
        out_shape=jax.ShapeDtypeStruct((m, n), out_dtype),
        scratch_shapes=[] if nk == 1 else [pltpu.VMEM((tm, tn), F32)],
        compiler_params=_params(("parallel", "parallel", "arbitrary")),
    )(*args)


class _Ctx:
    def __init__(self, first, last):
        self.first = first
        self.last = last


def rowwise(fn, name, rows, seq, tb, ncol, ins, params=(), outs=(), accs=()):
    assert rows % tb == 0 and seq % tb == 0 and tb % 16 == 0
    bps = seq // tb
    nrow = rows // tb
    r8 = tb // SUBLANES
    args, in_specs = [], []
    for arr, w, off, halo in ins:
        args.append(arr)
        in_specs.append(pl.BlockSpec((tb, w), lambda j, i, off=off: (i, off + j)))
        if halo in ("prev", "both"):
            args.append(arr)
            in_specs.append(pl.BlockSpec((SUBLANES, w), lambda j, i, off=off: (jnp.maximum(i * r8 - 1, 0), off + j)))
        if halo in ("next", "both"):
            args.append(arr)
            in_specs.append(pl.BlockSpec(
                (SUBLANES, w), lambda j, i, off=off: (jnp.minimum((i + 1) * r8, rows // SUBLANES - 1), off + j)))
    for arr, w, off in params:
        args.append(arr)
        if w is None:
            in_specs.append(pl.BlockSpec(arr.shape, lambda j, i: (0, 0)))
        else:
            in_specs.append(pl.BlockSpec((arr.shape[0], w), lambda j, i, off=off: (0, off + j)))
    out_shape, out_specs = [], []
    for total, w, off, dt in outs:
        out_shape.append(jax.ShapeDtypeStruct((rows, total), dt))
        out_specs.append(pl.BlockSpec((tb, w), lambda j, i, off=off: (i, off + j)))
    for r, w in accs:
        out_shape.append(jax.ShapeDtypeStruct((r, ncol * w), F32))
        out_specs.append(pl.BlockSpec((r, w), lambda j, i: (0, j)))
    n_out, n_acc = len(outs), len(accs)

    def body(*refs):
        i = pl.program_id(1)
        pos = 0
        vals = []
        for _, _, _, halo in ins:
            cur = refs[pos][...]
            pos += 1
            if halo is None:
                vals.append(cur)
            elif halo == "both":
                vals.append((cur, refs[pos][...], refs[pos + 1][...]))
                pos += 2
            else:
                vals.append((cur, refs[pos][...]))
                pos += 1
        for _ in params:
            vals.append(refs[pos][...])
            pos += 1
        ctx = _Ctx(i % bps == 0, i % bps == bps - 1)
        res = fn(ctx, *vals)
        if not isinstance(res, (tuple, list)):
            res = (res,)
        assert len(res) == n_out + n_acc
        for q in range(n_out):
            refs[pos + q][...] = res[q].astype(refs[pos + q].dtype)
        for q in range(n_acc):
            ref, val = refs[pos + n_out + q], res[n_out + q]

            @pl.when(i == 0)
            def _(ref=ref, val=val):
                ref[...] = val

            @pl.when(i != 0)
            def _(ref=ref, val=val):
                ref[...] += val

    res = pl.pallas_call(
        body, name=name,
        grid=(ncol, nrow),
        in_specs=in_specs,
        out_specs=out_specs,
        out_shape=out_shape,
        compiler_params=_params(("parallel", "arbitrary")),
    )(*args)
    return res


def _direct_exchange(modes, in_refs, out_refs, send_sems, recv_sems, local_sems):
    n = len(modes)
    x, y, c = lax.axis_index("x"), lax.axis_index("y"), lax.axis_index("c")
    me = 4 * x + 2 * y + c

    def copy(t, k, arriving):
        px = 1 - x if k & 4 else x
        py = 1 - y if k & 2 else y
        pc = 1 - c if k & 1 else c
        pid = 4 * px + 2 * py + pc
        src = in_refs[t] if modes[t] == "gather" else in_refs[t].at[pid]
        return pltpu.make_async_remote_copy(
            src_ref=src, dst_ref=out_refs[t].at[pid if arriving else me], send_sem=send_sems.at[t, k],
            recv_sem=recv_sems.at[t, k], device_id=(px, py, pc), device_id_type=pl.DeviceIdType.MESH)

    def own(t):
        src = in_refs[t] if modes[t] == "gather" else in_refs[t].at[me]
        return pltpu.make_async_copy(src, out_refs[t].at[me], local_sems.at[t])

    def start():
        for t in range(n):
            own(t).start()
            for k in range(1, N_DEV):
                copy(t, k, False).start()

    def finish():
        for t in range(n):
            for k in range(1, N_DEV):
                copy(t, k, True).wait_recv()
        for t in range(n):
            for k in range(1, N_DEV):
                copy(t, k, False).wait_send()
            own(t).wait()

    return start, finish


def _exchange_specs(items):
    n = len(items)
    out_shape = [jax.ShapeDtypeStruct((N_DEV,) + tuple(a.shape if mode == "gather" else a.shape[1:]), a.dtype)
                 for a, mode in items]
    anyspec = pl.BlockSpec(memory_space=pl.ANY)
    sems = [pltpu.SemaphoreType.DMA((n, N_DEV)), pltpu.SemaphoreType.DMA((n, N_DEV)), pltpu.SemaphoreType.DMA((n,))]
    return [anyspec] * n, [anyspec] * n, out_shape, sems


GROUP_W = SSD_K * SSD_P


def _tri(lower):
    r = lax.broadcasted_iota(jnp.int32, (SSD_Q, SSD_Q), 0)
    c = lax.broadcasted_iota(jnp.int32, (SSD_Q, SSD_Q), 1)
    return r >= c if lower else r <= c


def _group_masks():
    lane = lax.broadcasted_iota(jnp.int32, (1, GROUP_W), 1) // SSD_P
    row = lax.broadcasted_iota(jnp.int32, (GROUP_W, 1), 0) // SSD_P
    return [lane == k for k in range(SSD_K)], [row == k for k in range(SSD_K)]


def _per_head(masks, vals):
    out = jnp.where(masks[0], vals[0], 0.0)
    for m, v in zip(masks[1:], vals[1:]):
        out = jnp.where(m, v, out)
    return out


def _headsum(prod, g, lane_masks):
    out = None
    for k in range(SSD_K):
        term = jnp.sum(jnp.where(lane_masks[k], prod, 0.0), axis=1, keepdims=True) * _onehot_row(g * SSD_K + k)
        out = term if out is None else out + term
    return out


def ssd_fwd(xact, dtraw, dt_bias, a_log, d_skip, n_seq, seq, comm=()):
    nc = seq // SSD_Q
    rows = n_seq * seq
    nx = len(comm)

    def body(*refs):
        xact_ref, dtraw_ref, bias_ref, alog_ref, dskip_ref = refs[:5]
        y_ref, sin_ref = refs[5 + nx:7 + nx]
        state, cs_s, cst_s, dt_s = refs[7 + 2 * nx:11 + 2 * nx]
        b, c = pl.program_id(0), pl.program_id(1)
        if nx:
            start, finish = _direct_exchange([m for _, m in comm], refs[5:5 + nx], refs[7 + nx:7 + 2 * nx],
                                             *refs[11 + 2 * nx:])
            pl.when(jnp.logical_and(b == 0, c == 0))(start)

        @pl.when(c == 0)
        def _():
            state[...] = jnp.zeros_like(state)

        sin_ref[0] = state[...]
        dt = _softplus(dtraw_ref[...] + bias_ref[...])
        a = dt * (-jnp.exp(alog_ref[...]))
        cs = _dot_f32(_tri(True).astype(F32), a)
        cs_s[...] = cs
        cst_s[...] = cs.T
        dt_s[...] = dt
        causal = _tri(True)
        lane_masks, row_masks = _group_masks()
        for g in range(SSD_G):
            heads = [g * SSD_K + k for k in range(SSD_K)]
            bg = xact_ref[:, pl.ds(D_INNER + g * SSD_N, SSD_N)]
            cg = xact_ref[:, pl.ds(D_INNER + (SSD_G + g) * SSD_N, SSD_N)]
            xg = xact_ref[:, pl.ds(g * GROUP_W, GROUP_W)]
            cols = [cs_s[:, pl.ds(h, 1)] for h in heads]
            lasts = [cs_s[pl.ds(SSD_Q - 1, 1), pl.ds(h, 1)] for h in heads]
            xdg = xg * _per_head(lane_masks, [dt_s[:, pl.ds(h, 1)] for h in heads])
            sg = state[g]
            gm = _dot_nt(cg, bg)
            y = (_per_head(lane_masks, [jnp.exp(c_) for c_ in cols]) * _dot_nt(cg, sg)
                 + _per_head(lane_masks, [dskip_ref[:, pl.ds(h, 1)] for h in heads]) * xg)
            mats = [gm * jnp.exp(jnp.where(causal, cols[k] - cst_s[pl.ds(h, 1), :], NEG)) for k, h in enumerate(heads)]
            y4 = _dot_nn(jnp.concatenate(mats, axis=0), xdg)
            y = y + _per_head(lane_masks, [y4[k * SSD_Q:(k + 1) * SSD_Q] for k in range(SSD_K)])
            y_ref[:, pl.ds(g * GROUP_W, GROUP_W)] = y
            w = _per_head(lane_masks, [jnp.exp(l_ - c_) for l_, c_ in zip(lasts, cols)])
            state[g] = _per_head(row_masks, [jnp.exp(l_) for l_ in lasts]) * sg + _dot_tn(w * xdg, bg)
        if nx:
            pl.when(jnp.logical_and(b == n_seq - 1, c == nc - 1))(finish)

    vec = pl.BlockSpec((1, LANES), lambda b, c: (0, 0))
    x_in, x_out, x_shape, x_sems = _exchange_specs(comm)
    return pl.pallas_call(
        body, name="ssd_fwd",
        grid=(n_seq, nc),
        in_specs=[pl.BlockSpec((SSD_Q, CONV_DIM), lambda b, c: (b * nc + c, 0)),
                  pl.BlockSpec((SSD_Q, LANES), lambda b, c: (b * nc + c, 0)), vec, vec, vec] + x_in,
        out_specs=[pl.BlockSpec((SSD_Q, D_INNER), lambda b, c: (b * nc + c, 0)),
                   pl.BlockSpec((1, SSD_G, GROUP_W, SSD_N), lambda b, c: (b * nc + c, 0, 0, 0))] + x_out,
        out_shape=[jax.ShapeDtypeStruct((rows, D_INNER), F32),
                   jax.ShapeDtypeStruct((n_seq * nc, SSD_G, GROUP_W, SSD_N), F32)] + x_shape,
        scratch_shapes=[pltpu.VMEM((SSD_G, GROUP_W, SSD_N), F32), pltpu.VMEM((SSD_Q, LANES), F32),
                        pltpu.VMEM((LANES, SSD_Q), F32), pltpu.VMEM((SSD_Q, LANES), F32)] + (x_sems if nx else []),
        compiler_params=_params(("arbitrary", "arbitrary")),
    )(xact, dtraw, dt_bias, a_log, d_skip, *[a for a, _ in comm])


def ssd_bwd(xact, dtraw, dt_bias, a_log, d_skip, sin, dy, n_seq, seq, comm=()):
    nc = seq // SSD_Q
    rows = n_seq * seq
    nx = len(comm)

    def body(*refs):
        xact_ref, dtraw_ref, bias_ref, alog_ref, dskip_ref, sin_ref, dy_ref = refs[:7]
        dx_ref, ddt_ref, dbias_ref, dalog_ref, ddskip_ref = refs[7 + nx:12 + nx]
        dstate, cs_s, cst_s, dt_s = refs[12 + 2 * nx:16 + 2 * nx]
        b, c = pl.program_id(0), pl.program_id(1)
        if nx:
            start, finish = _direct_exchange([m for _, m in comm], refs[7:7 + nx], refs[12 + nx:12 + 2 * nx],
                                             *refs[16 + 2 * nx:])
            pl.when(jnp.logical_and(b == 0, c == 0))(start)

        @pl.when(c == 0)
        def _():
            dstate[...] = jnp.zeros_like(dstate)

        pre = dtraw_ref[...] + bias_ref[...]
        dt = _softplus(pre)
        a_neg = -jnp.exp(alog_ref[...])
        cs = _dot_f32(_tri(True).astype(F32), dt * a_neg)
        cs_s[...] = cs
        cst_s[...] = cs.T
        dt_s[...] = dt
        causal, anti = _tri(True), _tri(False)
        is_last_row = lax.broadcasted_iota(jnp.int32, (SSD_Q, 1), 0) == SSD_Q - 1
        lane_masks, row_masks = _group_masks()
        dcs_cf = jnp.zeros((SSD_Q, LANES), F32)
        dcs_rf = jnp.zeros((LANES, SSD_Q), F32)
        ddt_cf = jnp.zeros((SSD_Q, LANES), F32)
        dd_vec = jnp.zeros((1, LANES), F32)
        dlast_vec = jnp.zeros((1, LANES), F32)
        for g in range(SSD_G):
            heads = [g * SSD_K + k for k in range(SSD_K)]
            bg = xact_ref[:, pl.ds(D_INNER + g * SSD_N, SSD_N)]
            cg = xact_ref[:, pl.ds(D_INNER + (SSD_G + g) * SSD_N, SSD_N)]
            xg = xact_ref[:, pl.ds(g * GROUP_W, GROUP_W)]
            dyg = dy_ref[:, pl.ds(g * GROUP_W, GROUP_W)]
            cols = [cs_s[:, pl.ds(h, 1)] for h in heads]
            rws = [cst_s[pl.ds(h, 1), :] for h in heads]
            lasts = [cs_s[pl.ds(SSD_Q - 1, 1), pl.ds(h, 1)] for h in heads]
            e_lasts = [jnp.exp(l_) for l_ in lasts]
            dtg = _per_head(lane_masks, [dt_s[:, pl.ds(h, 1)] for h in heads])
            dskg = _per_head(lane_masks, [dskip_ref[:, pl.ds(h, 1)] for h in heads])
            e_col = _per_head(lane_masks, [jnp.exp(c_) for c_ in cols])
            w = _per_head(lane_masks, [jnp.exp(l_ - c_) for l_, c_ in zip(lasts, cols)])
            xdg = xg * dtg
            sg = sin_ref[0, g]
            dsn = dstate[g]
            gm = _dot_nt(cg, bg)
            gmt = _dot_nt(bg, cg)
            y_off = e_col * _dot_nt(cg, sg)
            d_cs = e_col * dyg
            dcg = _dot_nn(d_cs, sg)
            dsp = _dot_tn(d_cs, cg) + _per_head(row_masks, e_lasts) * dsn
            dbg = _dot_nn(w * xdg, dsn)
            dtt = _dot_nt(bg, dsn)
            dxd = w * dtt
            dw = _headsum(dtt * xdg * w, g, lane_masks)
            dcs_cf = dcs_cf + _headsum(dyg * y_off, g, lane_masks) - dw
            dlast_vec = dlast_vec + jnp.sum(dw, axis=0, keepdims=True)
            dsn_s = dsn * sg
            segs = [cols[k] - rws[k] for k in range(SSD_K)]
            decays = [jnp.exp(jnp.where(causal, s_, NEG)) for s_ in segs]
            dm4 = _dot_nt(jnp.concatenate([jnp.where(m_, dyg, 0.0) for m_ in lane_masks], axis=0), xdg)
            z4 = _dot_nn(jnp.concatenate([gmt * jnp.exp(jnp.where(anti, -s_, NEG)) for s_ in segs], axis=0), dyg)
            dxd = dxd + _per_head(lane_masks, [z4[k * SSD_Q:(k + 1) * SSD_Q] for k in range(SSD_K)])
            dgm = jnp.zeros((SSD_Q, SSD_Q), F32)
            for k, h in enumerate(heads):
                dm = dm4[k * SSD_Q:(k + 1) * SSD_Q]
                dseg = dm * gm * decays[k]
                dgm = dgm + dm * decays[k]
                oh_r = _onehot_row(h)
                dcs_cf = dcs_cf + jnp.sum(dseg, axis=1, keepdims=True) * oh_r
                dcs_rf = dcs_rf - _onehot_col(h) * jnp.sum(dseg, axis=0, keepdims=True)
                dlast_vec = dlast_vec + jnp.sum(jnp.where(row_masks[k], dsn_s, 0.0), keepdims=True) * e_lasts[k] * oh_r
            dx_ref[:, pl.ds(g * GROUP_W, GROUP_W)] = dxd * dtg + dskg * dyg
            ddt_cf = ddt_cf + _headsum(dxd * xg, g, lane_masks)
            dd_vec = dd_vec + jnp.sum(_headsum(dyg * xg, g, lane_masks), axis=0, keepdims=True)
            dstate[g] = dsp
            dx_ref[:, pl.ds(D_INNER + g * SSD_N, SSD_N)] = dbg + _dot_tn(dgm, cg)
            dx_ref[:, pl.ds(D_INNER + (SSD_G + g) * SSD_N, SSD_N)] = dcg + _dot_nn(dgm, bg)
        dcs = dcs_cf + dcs_rf.T + jnp.where(is_last_row, dlast_vec, 0.0)
        da = _dot_f32(_tri(False).astype(F32), dcs)
        ddt = ddt_cf + da * a_neg
        ddtraw = ddt * _sigmoid(pre)
        ddt_ref[...] = ddtraw.astype(ddt_ref.dtype)
        dbias = jnp.sum(ddtraw, axis=0, keepdims=True)
        dalog = jnp.sum(da * dt, axis=0, keepdims=True) * a_neg
        first_step = jnp.logical_and(b == 0, c == 0)

        @pl.when(first_step)
        def _():
            dbias_ref[...] = dbias
            dalog_ref[...] = dalog
            ddskip_ref[...] = dd_vec

        @pl.when(jnp.logical_not(first_step))
        def _():
            dbias_ref[...] += dbias
            dalog_ref[...] += dalog
            ddskip_ref[...] += dd_vec

        if nx:
            pl.when(jnp.logical_and(b == n_seq - 1, c == nc - 1))(finish)

    def rowblk(b, c):
        return b * nc + (nc - 1 - c)

    vec = pl.BlockSpec((1, LANES), lambda b, c: (0, 0))
    x_in, x_out, x_shape, x_sems = _exchange_specs(comm)
    return pl.pallas_call(
        body, name="ssd_bwd",
        grid=(n_seq, nc),
        in_specs=[pl.BlockSpec((SSD_Q, CONV_DIM), lambda b, c: (rowblk(b, c), 0)),
                  pl.BlockSpec((SSD_Q, LANES), lambda b, c: (rowblk(b, c), 0)), vec, vec, vec,
                  pl.BlockSpec((1, SSD_G, GROUP_W, SSD_N), lambda b, c: (rowblk(b, c), 0, 0, 0)),
                  pl.BlockSpec((SSD_Q, D_INNER), lambda b, c: (rowblk(b, c), 0))] + x_in,
        out_specs=[pl.BlockSpec((SSD_Q, CONV_DIM), lambda b, c: (rowblk(b, c), 0)),
                   pl.BlockSpec((SSD_Q, LANES), lambda b, c: (rowblk(b, c), 0)), vec, vec, vec] + x_out,
        out_shape=[jax.ShapeDtypeStruct((rows, CONV_DIM), F32), jax.ShapeDtypeStruct((rows, LANES), BF16),
                   jax.ShapeDtypeStruct((1, LANES), F32), jax.ShapeDtypeStruct((1, LANES), F32),
                   jax.ShapeDtypeStruct((1, LANES), F32)] + x_shape,
        scratch_shapes=[pltpu.VMEM((SSD_G, GROUP_W, SSD_N), F32), pltpu.VMEM((SSD_Q, LANES), F32),
                        pltpu.VMEM((LANES, SSD_Q), F32), pltpu.VMEM((SSD_Q, LANES), F32)] + (x_sems if nx else []),
        compiler_params=_params(("arbitrary", "arbitrary")),
    )(xact, dtraw, dt_bias, a_log, d_skip, sin, dy, *[a for a, _ in comm])


QKV_W = 3 * ATT_OUT
PAIR_W = 2 * ATT_HD
HEAD_PAIRS = ATT_H // 2
PREP_ROWS = 512


def _by_residue(a, n_seq, seq, dil):
    if dil == 1:
        return a
    return a.reshape(n_seq, seq // dil, dil, a.shape[1]).transpose(0, 2, 1, 3).reshape(a.shape)


def _by_token(a, n_seq, seq, dil):
    if dil == 1:
        return a
    return a.reshape(n_seq, dil, seq // dil, a.shape[1]).transpose(0, 2, 1, 3).reshape(a.shape)


def _head_sums(x, fn):
    lo = jnp.logical_not(lax.broadcasted_iota(jnp.int32, (1, 2 * ATT_HD), 1) >= ATT_HD)
    parts = []
    for p in range(ATT_H // 2):
        slab = x[:, p * 2 * ATT_HD:(p + 1) * 2 * ATT_HD]
        s_lo = fn(jnp.sum(jnp.where(lo, slab, 0.0), axis=1, keepdims=True))
        s_hi = fn(jnp.sum(jnp.where(lo, 0.0, slab), axis=1, keepdims=True))
        parts.append(jnp.where(lo, s_lo, s_hi))
    return jnp.concatenate(parts, axis=1)


def _head_rstd(x):
    return _head_sums(x * x, lambda s: lax.rsqrt(s * (1.0 / ATT_HD) + EPS))


def _head_rms_bwd(x, g_t, dy):
    r = _head_rstd(x)
    xh = x * r
    dyg = dy * g_t
    mean = _head_sums(dyg * xh, lambda s: s * (1.0 / ATT_HD))
    return r * (dyg - xh * mean), jnp.sum(dy * xh, axis=0, keepdims=True)


def qk_prep(qkv, gq_t, gk_t, rows, name):
    tb = min(PREP_ROWS, rows)

    def body(x_ref, gq_ref, gk_ref, o_ref):
        q = x_ref[:, pl.ds(0, ATT_OUT)]
        k = x_ref[:, pl.ds(ATT_OUT, ATT_OUT)]
        o_ref[:, pl.ds(0, ATT_OUT)] = (q * _head_rstd(q) * (gq_ref[...] * ATT_SCALE)).astype(o_ref.dtype)
        o_ref[:, pl.ds(ATT_OUT, ATT_OUT)] = (k * _head_rstd(k) * gk_ref[...]).astype(o_ref.dtype)
        o_ref[:, pl.ds(2 * ATT_OUT, ATT_OUT)] = x_ref[:, pl.ds(2 * ATT_OUT, ATT_OUT)].astype(o_ref.dtype)

    gspec = pl.BlockSpec((1, ATT_OUT), lambda i: (0, 0))
    blk = pl.BlockSpec((tb, QKV_W), lambda i: (i, 0))
    return pl.pallas_call(
        body, name=name,
        grid=(rows // tb,),
        in_specs=[blk, gspec, gspec],
        out_specs=blk,
        out_shape=jax.ShapeDtypeStruct((rows, QKV_W), MXU),
        compiler_params=_params(("parallel",)),
    )(qkv, gq_t, gk_t)


def _lane_hi():
    return lax.broadcasted_iota(jnp.int32, (1, PAIR_W), 1) >= ATT_HD


def _band_mask2(first_valid, query_rows):
    i = lax.broadcasted_iota(jnp.int32, (ATT_BLK, 2 * ATT_BLK), 0)
    j = lax.broadcasted_iota(jnp.int32, (ATT_BLK, 2 * ATT_BLK), 1)
    left = j < ATT_BLK
    right = jnp.logical_not(left)
    if query_rows:
        return jnp.logical_or(jnp.logical_and(jnp.logical_and(left, i <= j), first_valid),
                              jnp.logical_and(right, i >= j - ATT_BLK))
    return jnp.logical_or(jnp.logical_and(left, j >= i),
                          jnp.logical_and(jnp.logical_and(right, j - ATT_BLK <= i), first_valid))


def _only_head(slab, hi):
    keep = _lane_hi() if hi else jnp.logical_not(_lane_hi())
    return jnp.where(keep, slab, jnp.zeros_like(slab))


def attn_fwd(nq, n_seq, seq, dil, name):
    nb = seq // dil // ATT_BLK
    rows = n_seq * seq

    def body(cur_ref, prev_ref, o_ref, lse_ref, s_scr, p_scr):
        n = pl.program_id(1)
        mask = _band_mask2(n > 0, True)
        for h in range(ATT_H):
            sl = pl.ds((h // 2) * PAIR_W, PAIR_W)
            ks = pl.ds(ATT_OUT + (h // 2) * PAIR_W, PAIR_W)
            kcat = jnp.concatenate([prev_ref[:, ks], cur_ref[:, ks]], axis=0)
            s_scr[h] = jnp.where(mask, _dot_nt(_only_head(cur_ref[:, sl], h % 2), kcat), NEG)
        s_all = s_scr[...]
        mx = jnp.max(s_all, axis=2, keepdims=True)
        p_all = jnp.exp(s_all - mx)
        den = jnp.sum(p_all, axis=2, keepdims=True)
        p_scr[...] = p_all.astype(p_scr.dtype)
        inv = 1.0 / den
        lse = mx + jnp.log(den)
        lse_blk = jnp.zeros((ATT_BLK, LANES), F32)
        for h in range(ATT_H):
            lse_blk = lse_blk + lse[h] * _onehot_row(h)
        lse_ref[...] = lse_blk
        for pr in range(HEAD_PAIRS):
            vs = pl.ds(2 * ATT_OUT + pr * PAIR_W, PAIR_W)
            vcat = jnp.concatenate([prev_ref[:, vs], cur_ref[:, vs]], axis=0)
            lo = _dot_nn(p_scr[2 * pr], vcat) * inv[2 * pr]
            hi = _dot_nn(p_scr[2 * pr + 1], vcat) * inv[2 * pr + 1]
            o_ref[:, pl.ds(pr * PAIR_W, PAIR_W)] = jnp.where(_lane_hi(), hi, lo)

    def blk(width, shift):
        if shift:
            return pl.BlockSpec((ATT_BLK, width), lambda s, n: (s * nb + jnp.maximum(n - 1, 0), 0))
        return pl.BlockSpec((ATT_BLK, width), lambda s, n: (s * nb + n, 0))

    return pl.pallas_call(
        body, name=name,
        grid=(n_seq * dil, nb),
        in_specs=[blk(QKV_W, 0), blk(QKV_W, -1)],
        out_specs=[blk(ATT_OUT, 0), blk(LANES, 0)],
        out_shape=[jax.ShapeDtypeStruct((rows, ATT_OUT), F32), jax.ShapeDtypeStruct((rows, LANES), F32)],
        scratch_shapes=[pltpu.VMEM((ATT_H, ATT_BLK, 2 * ATT_BLK), F32), pltpu.VMEM((ATT_H, ATT_BLK, 2 * ATT_BLK), MXU)],
        compiler_params=_params(("parallel", "arbitrary")),
    )(nq, nq)


def attn_bwd(nq, do, lse, wts, rsum, n_seq, seq, dil, name):
    nb = seq // dil // ATT_BLK

    def body(prev_ref, cur_ref, nxt_ref, do_c, do_x, lse_c, lse_x, wt_c, wt_x, rs_c, rs_x, dn_ref):
        n = pl.program_id(1)
        mask_q = _band_mask2(n > 0, True)
        mask_k = _band_mask2(n < nb - 1, False)
        wc, wx = wt_c[...], wt_x[...]
        lse_t = jnp.concatenate([lse_c[...].T, lse_x[...].T], axis=1)
        dl_t = jnp.concatenate([(-wc * rs_c[...]).T, (-wx * rs_x[...]).T], axis=1)
        for pr in range(HEAD_PAIRS):
            sl = pl.ds(pr * PAIR_W, PAIR_W)
            ks = pl.ds(ATT_OUT + pr * PAIR_W, PAIR_W)
            vs = pl.ds(2 * ATT_OUT + pr * PAIR_W, PAIR_W)
            he, ho = pl.ds(2 * pr, 1), pl.ds(2 * pr + 1, 1)
            q_c, k_c, v_c = cur_ref[:, sl], cur_ref[:, ks], cur_ref[:, vs]
            qcat = jnp.concatenate([q_c, nxt_ref[:, sl]], axis=0)
            kcat = jnp.concatenate([prev_ref[:, ks], k_c], axis=0)
            vcat = jnp.concatenate([prev_ref[:, vs], v_c], axis=0)
            dog_c = do_c[:, sl] * jnp.where(_lane_hi(), wt_c[:, ho], wt_c[:, he])
            dog_x = do_x[:, sl] * jnp.where(_lane_hi(), wt_x[:, ho], wt_x[:, he])
            dog = jnp.concatenate([dog_c, dog_x], axis=0).astype(MXU)
            res = []
            for hi in (0, 1):
                h = 2 * pr + hi
                one = pl.ds(h, 1)
                dl_col = -wt_c[:, one] * rs_c[:, one]
                p_q = jnp.exp(jnp.where(mask_q, _dot_nt(_only_head(q_c, hi), kcat) - lse_c[:, one], NEG))
                ds_q = p_q * (_dot_nt(_only_head(dog[:ATT_BLK], hi), vcat) + dl_col)
                dq = _dot_nn(ds_q, kcat)
                p_t = jnp.exp(jnp.where(mask_k, _dot_nt(_only_head(k_c, hi), qcat) - lse_t[h:h + 1, :], NEG))
                ds_t = p_t * (_dot_nt(_only_head(v_c, hi), dog) + dl_t[h:h + 1, :])
                res.append((dq, _dot_nn(ds_t, qcat), _dot_nn(p_t, dog)))
            for t, dst in enumerate((sl, ks, vs)):
                dn_ref[:, dst] = jnp.where(_lane_hi(), res[1][t], res[0][t])

    def at(shift, width):
        if shift < 0:
            return pl.BlockSpec((ATT_BLK, width), lambda s, n: (s * nb + jnp.maximum(n - 1, 0), 0))
        if shift > 0:
            return pl.BlockSpec((ATT_BLK, width), lambda s, n: (s * nb + jnp.minimum(n + 1, nb - 1), 0))
        return pl.BlockSpec((ATT_BLK, width), lambda s, n: (s * nb + n, 0))

    return pl.pallas_call(
        body, name=name,
        grid=(n_seq * dil, nb),
        in_specs=[at(-1, QKV_W), at(0, QKV_W), at(1, QKV_W), at(0, ATT_OUT), at(1, ATT_OUT),
                  at(0, LANES), at(1, LANES), at(0, LANES), at(1, LANES), at(0, LANES), at(1, LANES)],
        out_specs=at(0, QKV_W),
        out_shape=jax.ShapeDtypeStruct((n_seq * seq, QKV_W), F32),
        compiler_params=_params(("parallel", "arbitrary")),
    )(nq, nq, nq, do, do, lse, lse, wts, wts, rsum, rsum)


def qk_post(qkv, dn, gq_t, gk_t, rows, name):
    tb = min(PREP_ROWS, rows)

    def body(x_ref, dn_ref, gq_ref, gk_ref, o_ref, dgq_ref, dgk_ref):
        i = pl.program_id(0)
        qs, ks, vs = pl.ds(0, ATT_OUT), pl.ds(ATT_OUT, ATT_OUT), pl.ds(2 * ATT_OUT, ATT_OUT)
        dq, dgq = _head_rms_bwd(x_ref[:, qs], gq_ref[...], dn_ref[:, qs] * ATT_SCALE)
        dk, dgk = _head_rms_bwd(x_ref[:, ks], gk_ref[...], dn_ref[:, ks])
        o_ref[:, qs] = dq.astype(o_ref.dtype)
        o_ref[:, ks] = dk.astype(o_ref.dtype)
        o_ref[:, vs] = dn_ref[:, vs].astype(o_ref.dtype)

        @pl.when(i == 0)
        def _():
            dgq_ref[...] = dgq
            dgk_ref[...] = dgk

        @pl.when(i != 0)
        def _():
            dgq_ref[...] += dgq
            dgk_ref[...] += dgk

    gspec = pl.BlockSpec((1, ATT_OUT), lambda i: (0, 0))
    blk = pl.BlockSpec((tb, QKV_W), lambda i: (i, 0))
    return pl.pallas_call(
        body, name=name,
        grid=(rows // tb,),
        in_specs=[blk, blk, gspec, gspec],
        out_specs=[blk, gspec, gspec],
        out_shape=[jax.ShapeDtypeStruct((rows, QKV_W), MXU), jax.ShapeDtypeStruct((1, ATT_OUT), F32),
                   jax.ShapeDtypeStruct((1, ATT_OUT), F32)],
        compiler_params=_params(("arbitrary",)),
    )(qkv, dn, gq_t, gk_t)


OTHER_CHIPS = (4, 2, 6)


def all_gather(arrays, name):
    n = len(arrays)

    def body(*refs):
        in_refs, out_refs = refs[:n], refs[n:2 * n]
        send_sems, recv_sems, local_sems = refs[2 * n:]
        x, y, c = lax.axis_index("x"), lax.axis_index("y"), lax.axis_index("c")
        me = 4 * x + 2 * y + c

        def peer(k):
            px = 1 - x if k & 4 else x
            py = 1 - y if k & 2 else y
            pc = 1 - c if k & 1 else c
            return (px, py, pc), 4 * px + 2 * py + pc

        def first(t, k, arriving):
            dev, pid = peer(k)
            return pltpu.make_async_remote_copy(
                src_ref=in_refs[t], dst_ref=out_refs[t].at[pid if arriving else me], send_sem=send_sems.at[t, k],
                recv_sem=recv_sems.at[t, k], device_id=dev, device_id_type=pl.DeviceIdType.MESH)

        def passed(t, k, arriving):
            sib, _ = peer(1)
            _, pid = peer(k + 1 if arriving else k)
            slot = out_refs[t].at[pid]
            return pltpu.make_async_remote_copy(
                src_ref=slot, dst_ref=slot, send_sem=send_sems.at[t, k + 1], recv_sem=recv_sems.at[t, k + 1],
                device_id=sib, device_id_type=pl.DeviceIdType.MESH)

        def own(t):
            return pltpu.make_async_copy(in_refs[t], out_refs[t].at[me], local_sems.at[t])

        for t in range(n):
            own(t).start()
            for k in (1,) + OTHER_CHIPS:
                first(t, k, False).start()
        for t in range(n):
            for k in OTHER_CHIPS:
                first(t, k, True).wait_recv()
                passed(t, k, False).start()
        for t in range(n):
            first(t, 1, True).wait_recv()
            for k in OTHER_CHIPS:
                passed(t, k, True).wait_recv()
        for t in range(n):
            for k in (1,) + OTHER_CHIPS:
                first(t, k, False).wait_send()
            for k in OTHER_CHIPS:
                passed(t, k, False).wait_send()
            own(t).wait()

    anyspec = pl.BlockSpec(memory_space=pl.ANY)
    return pl.pallas_call(
        body, name=name,
        in_specs=[anyspec] * n,
        out_specs=[anyspec] * n,
        out_shape=[jax.ShapeDtypeStruct((N_DEV,) + tuple(a.shape), a.dtype) for a in arrays],
        scratch_shapes=[pltpu.SemaphoreType.DMA((n, N_DEV)), pltpu.SemaphoreType.DMA((n, N_DEV)),
                        pltpu.SemaphoreType.DMA((n,))],
    )(*arrays)


N_CHIPS = N_DEV // 2


def sibling_swap(arrays, name):
    n = len(arrays)

    def body(*refs):
        in_refs, out_refs, send_sems, recv_sems = refs[:n], refs[n:2 * n], refs[2 * n], refs[2 * n + 1]
        x, y, c = lax.axis_index("x"), lax.axis_index("y"), lax.axis_index("c")

        def copy(t, q):
            return pltpu.make_async_remote_copy(
                src_ref=in_refs[t].at[2 * q + (1 - c)], dst_ref=out_refs[t].at[q],
                send_sem=send_sems.at[t, q], recv_sem=recv_sems.at[t, q],
                device_id=(x, y, 1 - c), device_id_type=pl.DeviceIdType.MESH)

        for t in range(n):
            for q in range(N_CHIPS):
                copy(t, q).start()
        for t in range(n):
            for q in range(N_CHIPS):
                copy(t, q).wait_recv()
        for t in range(n):
            for q in range(N_CHIPS):
                copy(t, q).wait_send()

    anyspec = pl.BlockSpec(memory_space=pl.ANY)
    return pl.pallas_call(
        body, name=name,
        in_specs=[anyspec] * n,
        out_specs=[anyspec] * n,
        out_shape=[jax.ShapeDtypeStruct((N_CHIPS,) + a.shape[1:], a.dtype) for a in arrays],
        scratch_shapes=[pltpu.SemaphoreType.DMA((n, N_CHIPS)), pltpu.SemaphoreType.DMA((n, N_CHIPS))],
    )(*arrays)


def chip_exchange(arrays, name):
    n = len(arrays)

    def body(*refs):
        in_refs, out_refs = refs[:n], refs[n:2 * n]
        send_sems, recv_sems, local_sems = refs[2 * n:]
        x, y, c = lax.axis_index("x"), lax.axis_index("y"), lax.axis_index("c")
        mine = 2 * x + y

        def peer(k):
            px = 1 - x if k & 4 else x
            py = 1 - y if k & 2 else y
            return (px, py, c), 2 * px + py

        def remote(t, k, arriving):
            dev, q = peer(k)
            return pltpu.make_async_remote_copy(
                src_ref=in_refs[t].at[q], dst_ref=out_refs[t].at[q if arriving else mine],
                send_sem=send_sems.at[t, k], recv_sem=recv_sems.at[t, k],
                device_id=dev, device_id_type=pl.DeviceIdType.MESH)

        def own(t):
            return pltpu.make_async_copy(in_refs[t].at[mine], out_refs[t].at[mine], local_sems.at[t])

        for t in range(n):
            own(t).start()
            for k in OTHER_CHIPS:
                remote(t, k, False).start()
        for t in range(n):
            for k in OTHER_CHIPS:
                remote(t, k, True).wait_recv()
        for t in range(n):
            for k in OTHER_CHIPS:
                remote(t, k, False).wait_send()
            own(t).wait()

    anyspec = pl.BlockSpec(memory_space=pl.ANY)
    return pl.pallas_call(
        body, name=name,
        in_specs=[anyspec] * n,
        out_specs=[anyspec] * n,
        out_shape=[jax.ShapeDtypeStruct(a.shape, a.dtype) for a in arrays],
        scratch_shapes=[pltpu.SemaphoreType.DMA((n, N_DEV)), pltpu.SemaphoreType.DMA((n, N_DEV)),
                        pltpu.SemaphoreType.DMA((n,))],
    )(*arrays)


def pair_add(a, b, name):
    _, r, c = a.shape
    rb = r if r <= 512 else (128 if c > 1024 else 256)
    assert r % rb == 0

    def body(a_ref, b_ref, o_ref):
        o_ref[...] = (a_ref[...].astype(F32) + b_ref[...].astype(F32)).astype(o_ref.dtype)

    blk = pl.BlockSpec((1, rb, c), lambda q, i: (q, i, 0))
    return pl.pallas_call(
        body, name=name,
        grid=(N_CHIPS, r // rb),
        in_specs=[blk, blk],
        out_specs=blk,
        out_shape=jax.ShapeDtypeStruct(a.shape, a.dtype),
        compiler_params=_params(("parallel", "parallel")),
    )(a, b)


def adamw(parts, w, m, v, name):
    r, c = w.shape[-2:]
    n_parts = parts.shape[0]
    rb = r if r <= 512 else (128 if c > 1024 else 256)
    assert r % rb == 0

    def body(p_ref, w_ref, m_ref, v_ref, g_out, d_out, m_out, v_out):
        g = p_ref[0].astype(F32)
        for i in range(1, n_parts):
            g = g + p_ref[i].astype(F32)
        m_new = ADAM_B1 * m_ref[...] + (1.0 - ADAM_B1) * g
        v_new = ADAM_B2 * v_ref[...] + (1.0 - ADAM_B2) * (g * g)
        m_hat = m_new / (1.0 - ADAM_B1 ** ADAM_STEP)
        v_hat = v_new / (1.0 - ADAM_B2 ** ADAM_STEP)
        g_out[...] = g
        d_out[...] = -ADAM_LR * (m_hat / (jnp.sqrt(v_hat) + ADAM_EPS) + ADAM_WD * w_ref[...])
        m_out[...] = m_new
        v_out[...] = v_new

    if w.ndim == 3:
        blk = pl.BlockSpec((None, rb, c), lambda i: (0, i, 0))
    else:
        blk = pl.BlockSpec((rb, c), lambda i: (i, 0))
    return pl.pallas_call(
        body, name=name,
        grid=(r // rb,),
        in_specs=[pl.BlockSpec((n_parts, rb, c), lambda i: (0, i, 0)), blk, blk, blk],
        out_specs=[blk] * 4,
        out_shape=[jax.ShapeDtypeStruct(w.shape, F32)] * 4,
        compiler_params=_params(("parallel",)),
    )(parts, w, m, v)


def _pad_lanes(vec, n=LANES):
    return jnp.pad(vec, ((0, 0), (0, n - vec.shape[1])))


COL_SHARDED = ("w_in", "ssd_conv_w", "w_attn_proj", "w_up", "ffn_conv_w")
MATRICES = ("w_in", "w_attn_proj", "w_up", "w_ssd_proj", "w_out", "w_down")
LATE = ("w_ssd_proj", "w_attn_proj", "w_out", "w_up", "ffn_conv_w", "w_down")


def _narrow(name, a):
    return a.astype(MXU) if name in MATRICES else a


def _from_gathered(name, g):
    if name in COL_SHARDED:
        return jnp.transpose(g, (1, 0, 2)).reshape(g.shape[1], N_DEV * g.shape[2])
    return g.reshape(N_DEV * g.shape[1], g.shape[2])


def _to_slabs(name, g):
    if name in COL_SHARDED:
        return jnp.transpose(g.reshape(g.shape[0], N_DEV, g.shape[1] // N_DEV), (1, 0, 2))
    return g.reshape(N_DEV, g.shape[0] // N_DEV, g.shape[1])


def _columns(m, a, b):
    if m.ndim == 2:
        return m[:, a:b]
    c = m.shape[2]
    cuts = [m[j][:, max(a - j * c, 0):min(b - j * c, c)] for j in range(a // c, (b - 1) // c + 1)]
    return cuts[0] if len(cuts) == 1 else jnp.concatenate(cuts, axis=1)


def _column_shards(pieces, c):
    shards = []
    for j in range(N_DEV):
        cuts = []
        for start, arr in pieces:
            lo, hi = max(j * c - start, 0), min((j + 1) * c - start, arr.shape[1])
            if lo < hi:
                cuts.append(arr[:, lo:hi])
        shards.append(cuts[0] if len(cuts) == 1 else jnp.concatenate(cuts, axis=1))
    return jnp.stack(shards)


def local_step(x, target, w, late=None):
    n_seq, seq, _ = x.shape
    rows = n_seq * seq
    x = x.reshape(rows, D_MODEL)
    target = target.reshape(rows, D_MODEL)
    mx = lambda a: a.astype(MXU)

    splits = [sum(IN_WIDTHS[:i]) for i in range(len(IN_WIDTHS) + 1)]
    w_in = w["w_in"]
    part = lambda i: _columns(w_in, splits[i], splits[i + 1])
    w_z, w_xbc, w_gs, w_ga = mx(part(0)), mx(part(1)), mx(part(6)), mx(part(7))
    w_dt = mx(_pad_lanes(part(2)))
    head_group = lambda t, g: (splits[3 + t] + g * ATT_OUT, splits[3 + t] + (g + 1) * ATT_OUT)
    w_qkv = [mx(jnp.concatenate([_columns(w_in, *head_group(t, g)) for t in range(3)], axis=1))
             for g in range(ATT_GROUPS)]
    conv_w, conv_b, fconv_b = w["ssd_conv_w"], w["ssd_conv_b"], w["ffn_conv_b"]
    dt_bias, a_log, d_skip = _pad_lanes(w["dt_bias"]), _pad_lanes(w["a_log"]), _pad_lanes(w["d_skip"])
    g1, g2, gn, gq, gk = w["norm1_g"], w["norm2_g"], w["ssd_norm_g"], w["q_norm_g"], w["k_norm_g"]

    tb = min(512, seq)
    tbm = min(256, seq)
    cw = 1024
    rw = lambda fn, name, ncol, ins, params=(), outs=(), accs=(), tb_=tb: rowwise(
        fn, name, rows, seq, tb_, ncol, ins, params, outs, accs)

    (h,) = rw(lambda ctx, xv, g: _rms_fwd(xv, g), "rms1_fwd", 1, [(x, D_MODEL, 0, None)], [(g1, None, 0)],
              [(D_MODEL, D_MODEL, 0, MXU)])
    z = matmul(h, w_z, "mm_z")
    xbc = matmul(h, w_xbc, "mm_xbc")
    dtraw = matmul(h, w_dt, "mm_dt")
    by_residue = lambda a, g: _by_residue(a, n_seq, seq, ATT_DILATIONS[g])
    by_token = lambda a, g: _by_token(a, n_seq, seq, ATT_DILATIONS[g])
    h_res = [by_residue(h, g) for g in range(ATT_GROUPS)]
    qkv = [matmul(h_res[g], w_qkv[g], f"mm_qkv{g}") for g in range(ATT_GROUPS)]
    gs = matmul(h, w_gs, "mm_gs")
    ga = matmul(h, w_ga, "mm_ga")

    def conv_silu(ctx, xh, wv, bv):
        return _silu(bv + _conv_prev(xh[0], xh[1], wv, ctx.first, SSD_CONV))

    (xact,) = rw(conv_silu, "ssd_conv_fwd", CONV_DIM // cw, [(xbc, cw, 0, "prev")],
                 [(conv_w, cw, 0), (conv_b, cw, 0)], [(CONV_DIM, cw, 0, F32)])
    if late is None:
        y, sin = ssd_fwd(xact, dtraw, dt_bias, a_log, d_skip, n_seq, seq)
    else:
        y, sin, *gathered = ssd_fwd(xact, dtraw, dt_bias, a_log, d_skip, n_seq, seq,
                                    comm=[(late[n], "gather") for n in LATE])
        w = {**w, **{n: g if n == "w_up" else _from_gathered(n, g) for n, g in zip(LATE, gathered)}}
    w_sp, w_ap, w_o, w_d = mx(w["w_ssd_proj"]), mx(w["w_attn_proj"]), mx(w["w_out"]), mx(w["w_down"])
    w_ug, w_uv = mx(_columns(w["w_up"], 0, D_FF)), mx(_columns(w["w_up"], D_FF, 2 * D_FF))
    fconv_w = w["ffn_conv_w"]

    def gated_norm(ctx, yv, zv, g):
        yz = yv * _silu(zv)
        return jnp.concatenate([_rms_fwd(yz[:, i:i + NORM_GROUP], g[:, i:i + NORM_GROUP])
                                for i in range(0, cw, NORM_GROUP)], axis=1)

    (y_ssd,) = rw(gated_norm, "ssd_post_fwd", D_INNER // cw, [(y, cw, 0, None), (z, cw, 0, None)], [(gn, cw, 0)],
                  [(D_INNER, cw, 0, MXU)])

    gq_t, gk_t = jnp.tile(gq, (1, ATT_H)), jnp.tile(gk, (1, ATT_H))
    nq = [qk_prep(qkv[g], gq_t, gk_t, rows, f"qk_prep{g}") for g in range(ATT_GROUPS)]
    att = [attn_fwd(nq[g], n_seq, seq, ATT_DILATIONS[g], f"attn_fwd{g}") for g in range(ATT_GROUPS)]

    def combine(ctx, o0, o1, o2, l0, l1, l2):
        mxl = jnp.maximum(jnp.maximum(l0, l1), l2)
        e = [jnp.exp(l - mxl) for l in (l0, l1, l2)]
        inv = 1.0 / (e[0] + e[1] + e[2])
        ws = [ei * inv for ei in e]
        out = sum(_expand_heads(wi) * oi for wi, oi in zip(ws, (o0, o1, o2)))
        return (out, *ws)

    y_attn, wt0, wt1, wt2 = rw(
        combine, "attn_combine", 1,
        [(by_token(att[g][0], g), ATT_OUT, 0, None) for g in range(3)]
        + [(by_token(att[g][1], g), LANES, 0, None) for g in range(3)], [],
        [(ATT_OUT, ATT_OUT, 0, F32)] + [(LANES, LANES, 0, F32)] * 3)
    wts = (wt0, wt1, wt2)

    ps = matmul(y_ssd, w_sp, "mm_ssd_proj")
    pa = matmul(y_attn, w_ap, "mm_attn_proj")
    (merged,) = rw(lambda ctx, a, b, c, d: _sigmoid(c) * a + _sigmoid(d) * b, "merge_fwd", D_MODEL // cw,
                   [(ps, cw, 0, None), (pa, cw, 0, None), (gs, cw, 0, None), (ga, cw, 0, None)], [],
                   [(D_MODEL, cw, 0, MXU)])
    x1 = matmul(merged, w_o, "mm_out", add=x)
    (h2,) = rw(lambda ctx, xv, g: _rms_fwd(xv, g), "rms2_fwd", 1, [(x1, D_MODEL, 0, None)], [(g2, None, 0)],
               [(D_MODEL, D_MODEL, 0, MXU)])
    up_g = matmul(h2, w_ug, "mm_up_g")
    up_v = matmul(h2, w_uv, "mm_up_v")
    fw = D_FF // 2
    nfc = D_FF // fw

    def mlp_act(ctx, ug, uv, wg, wv, bg, bv):
        cg = bg + _conv_prev(ug[0], ug[1], wg, ctx.first, FFN_CONV)
        cv = bv + _conv_prev(uv[0], uv[1], wv, ctx.first, FFN_CONV)
        return _silu(cg) * cv

    (act,) = rw(mlp_act, "mlp_act_fwd", nfc, [(up_g, fw, 0, "prev"), (up_v, fw, 0, "prev")],
                [(fconv_w, fw, 0), (fconv_w, fw, nfc), (fconv_b, fw, 0), (fconv_b, fw, nfc)], [(D_FF, fw, 0, MXU)],
                tb_=tbm)
    x2 = matmul(act, w_d, "mm_down", add=x1)

    def loss_fn(ctx, xv, tv):
        d = xv - tv
        g = d * (1.0 / D_MODEL)
        return g, g, jnp.sum(d * d, axis=0, keepdims=True)

    dx2, dx2_m, sq = rw(loss_fn, "loss", 1, [(x2, D_MODEL, 0, None), (target, D_MODEL, 0, None)], [],
                        [(D_MODEL, D_MODEL, 0, F32), (D_MODEL, D_MODEL, 0, MXU)], [(1, D_MODEL)])

    grads = {}
    dact = matmul(dx2_m, w_d, "mm_d_act", tb=True)
    grads["w_down"] = matmul(act, dx2_m, "mm_dw_down", ta=True, out_dtype=MXU)

    def mlp_bwd(ctx, da, ug, uv, wg, wv, bg, bv):
        cg, cg_n = _conv_pre(ug, wg, bg, ctx.first, FFN_CONV)
        cv, cv_n = _conv_pre(uv, wv, bv, ctx.first, FFN_CONV)
        da_c, da_n = da
        dup_g_, dwg, dbg = _conv_bwd(da_c * cv * _silu_grad(cg), da_n * cv_n * _silu_grad(cg_n), ug, wg, ctx, FFN_CONV)
        dup_v_, dwv, dbv = _conv_bwd(da_c * _silu(cg), da_n * _silu(cg_n), uv, wv, ctx, FFN_CONV)
        return dup_g_, dup_v_, dwg, dbg, dwv, dbv

    dup_g, dup_v, dfw_g, dfb_g, dfw_v, dfb_v = rw(
        mlp_bwd, "mlp_bwd", nfc, [(dact, fw, 0, "next"), (up_g, fw, 0, "both"), (up_v, fw, 0, "both")],
        [(fconv_w, fw, 0), (fconv_w, fw, nfc), (fconv_b, fw, 0), (fconv_b, fw, nfc)],
        [(D_FF, fw, 0, MXU), (D_FF, fw, 0, MXU)], [(FFN_CONV, fw), (1, fw), (FFN_CONV, fw), (1, fw)], tb_=tbm)
    grads["ffn_conv_w"] = jnp.concatenate([dfw_g, dfw_v], axis=1)
    grads["ffn_conv_b"] = jnp.concatenate([dfb_g, dfb_v], axis=1)
    dh2 = matmul(dup_g, w_ug, "mm_dh2_g", tb=True)
    dh2 = matmul(dup_v, w_uv, "mm_dh2_v", tb=True, add=dh2)
    dw_up = [(0, matmul(h2, dup_g, "mm_dw_up_g", ta=True, out_dtype=MXU)),
             (D_FF, matmul(h2, dup_v, "mm_dw_up_v", ta=True, out_dtype=MXU))]
    if w["w_up"].ndim == 3:
        grads["w_up"] = _column_shards(dw_up, w["w_up"].shape[2])
    else:
        grads["w_up"] = jnp.concatenate([p for _, p in dw_up], axis=1)

    def rms_bwd_fn(ctx, xv, dh_, dres, g):
        dxv, dg = _rms_bwd(xv, g, dh_)
        return dres + dxv, dg

    def rms_bwd_fn2(ctx, xv, dh_, dres, g):
        dxv, dg = rms_bwd_fn(ctx, xv, dh_, dres, g)
        return dxv, dxv, dg

    dx1, dx1_m, grads["norm2_g"] = rw(
        rms_bwd_fn2, "rms2_bwd", 1, [(x1, D_MODEL, 0, None), (dh2, D_MODEL, 0, None), (dx2, D_MODEL, 0, None)],
        [(g2, None, 0)], [(D_MODEL, D_MODEL, 0, F32), (D_MODEL, D_MODEL, 0, MXU)], [(1, D_MODEL)])

    dmerged = matmul(dx1_m, w_o, "mm_d_merged", tb=True)
    grads["w_out"] = matmul(merged, dx1_m, "mm_dw_out", ta=True, out_dtype=MXU)

    def merge_bwd(ctx, dm, a, b, c, d):
        sc, sd = _sigmoid(c), _sigmoid(d)
        return dm * sc, dm * sd, dm * a * sc * (1.0 - sc), dm * b * sd * (1.0 - sd)

    dps, dpa, dgs, dga = rw(merge_bwd, "merge_bwd", D_MODEL // cw,
                            [(dmerged, cw, 0, None), (ps, cw, 0, None), (pa, cw, 0, None), (gs, cw, 0, None),
                             (ga, cw, 0, None)], [], [(D_MODEL, cw, 0, MXU)] * 4)
    dy_ssd = matmul(dps, w_sp, "mm_d_y_ssd", tb=True)
    grads["w_ssd_proj"] = matmul(y_ssd, dps, "mm_dw_ssd_proj", ta=True, out_dtype=MXU)
    dy_attn = matmul(dpa, w_ap, "mm_d_y_attn", tb=True)
    grads["w_attn_proj"] = matmul(y_attn, dpa, "mm_dw_attn_proj", ta=True, out_dtype=MXU)

    (rsum,) = rw(lambda ctx, a, b: _reduce_heads(a * b), "attn_rsum", 1,
                 [(dy_attn, ATT_OUT, 0, None), (y_attn, ATT_OUT, 0, None)], [], [(LANES, LANES, 0, F32)])
    dqkv, dgq, dgk = [], 0.0, 0.0
    for g in range(ATT_GROUPS):
        dn = attn_bwd(nq[g], by_residue(dy_attn, g), att[g][1], by_residue(wts[g], g), by_residue(rsum, g), n_seq, seq,
                      ATT_DILATIONS[g], f"attn_bwd{g}")
        d_, a_, b_ = qk_post(qkv[g], dn, gq_t, gk_t, rows, f"qk_post{g}")
        dqkv.append(d_)
        dgq, dgk = dgq + a_, dgk + b_
    per_head = lambda v: jnp.sum(v.reshape(ATT_H, ATT_HD), axis=0, keepdims=True)
    grads["q_norm_g"], grads["k_norm_g"] = per_head(dgq), per_head(dgk)

    def gated_norm_bwd(ctx, dyn, yv, zv, g):
        sz = _silu(zv)
        yz = yv * sz
        dyz, dgs_ = [], []
        for i in range(0, cw, NORM_GROUP):
            a, b = _rms_bwd(yz[:, i:i + NORM_GROUP], g[:, i:i + NORM_GROUP], dyn[:, i:i + NORM_GROUP])
            dyz.append(a)
            dgs_.append(b)
        dyz = jnp.concatenate(dyz, axis=1)
        return dyz * sz, dyz * yv * _silu_grad(zv), jnp.concatenate(dgs_, axis=1)

    dy, dz, grads["ssd_norm_g"] = rw(gated_norm_bwd, "ssd_post_bwd", D_INNER // cw,
                                     [(dy_ssd, cw, 0, None), (y, cw, 0, None), (z, cw, 0, None)], [(gn, cw, 0)],
                                     [(D_INNER, cw, 0, F32), (D_INNER, cw, 0, MXU)], [(1, cw)])
    if late is None:
        dxact, ddt, dbias, dalog, ddskip = ssd_bwd(xact, dtraw, dt_bias, a_log, d_skip, sin, dy, n_seq, seq)
    else:
        dxact, ddt, dbias, dalog, ddskip, *parts = ssd_bwd(
            xact, dtraw, dt_bias, a_log, d_skip, sin, dy, n_seq, seq,
            comm=[(grads[n] if n == "w_up" else _to_slabs(n, _narrow(n, grads[n])), "scatter") for n in LATE])
        grads.update(zip(LATE, parts))
    grads["dt_bias"], grads["a_log"], grads["d_skip"] = dbias[:, :SSD_H], dalog[:, :SSD_H], ddskip[:, :SSD_H]

    def conv_silu_bwd(ctx, dxa, xin, wv, bv):
        pre, pre_n = _conv_pre(xin, wv, bv, ctx.first, SSD_CONV)
        return _conv_bwd(dxa[0] * _silu_grad(pre), dxa[1] * _silu_grad(pre_n), xin, wv, ctx, SSD_CONV)

    dxbc, grads["ssd_conv_w"], grads["ssd_conv_b"] = rw(
        conv_silu_bwd, "ssd_conv_bwd", CONV_DIM // cw, [(dxact, cw, 0, "next"), (xbc, cw, 0, "both")],
        [(conv_w, cw, 0), (conv_b, cw, 0)], [(CONV_DIM, cw, 0, MXU)], [(SSD_CONV, cw), (1, cw)])

    pieces = [(d_, d_, h, w_, tag) for d_, w_, tag in
              ((dz, w_z, "z"), (dxbc, w_xbc, "xbc"), (ddt, w_dt, "dt"), (dgs, w_gs, "gs"), (dga, w_ga, "ga"))]
    pieces += [(by_token(dqkv[g], g), dqkv[g], h_res[g], w_qkv[g], f"qkv{g}") for g in range(ATT_GROUPS)]
    dh, dws = None, {}
    for dpart, dpart_h, h_in, wpart, tag in pieces:
        dh = matmul(dpart, wpart, f"mm_dh_{tag}", tb=True, add=dh)
        dws[tag] = matmul(h_in, dpart_h, f"mm_dw_{tag}", ta=True, out_dtype=MXU)
    dw_in = [(splits[0], dws["z"]), (splits[1], dws["xbc"]), (splits[2], dws["dt"][:, :SSD_H])]
    dw_in += [(head_group(t, g)[0], dws[f"qkv{g}"][:, t * ATT_OUT:(t + 1) * ATT_OUT])
              for t in range(3) for g in range(ATT_GROUPS)]
    dw_in += [(splits[6], dws["gs"]), (splits[7], dws["ga"])]
    if w_in.ndim == 3:
        grads["w_in"] = _column_shards(dw_in, w_in.shape[2])
    else:
        grads["w_in"] = jnp.concatenate([p for _, p in dw_in], axis=1)
    grad_x, grads["norm1_g"] = rw(rms_bwd_fn, "rms1_bwd", 1,
                                  [(x, D_MODEL, 0, None), (dh, D_MODEL, 0, None), (dx1, D_MODEL, 0, None)],
                                  [(g1, None, 0)], [(D_MODEL, D_MODEL, 0, F32)], [(1, D_MODEL)])
    return sq, grad_x.reshape(n_seq, seq, D_MODEL), grads


EARLY = ("w_in", "ssd_conv_w")
REPLICATED = ("norm1_g", "ssd_conv_b", "dt_bias", "a_log", "d_skip", "ssd_norm_g", "q_norm_g", "k_norm_g",
              "norm2_g", "ffn_conv_b")
WEIGHTS = ("norm1_g", "w_in", "ssd_conv_w", "ssd_conv_b", "dt_bias", "a_log", "d_skip", "ssd_norm_g", "w_ssd_proj",
           "q_norm_g", "k_norm_g", "w_attn_proj", "w_out", "norm2_g", "w_up", "ffn_conv_w", "ffn_conv_b", "w_down")
PACK_ROWS, PACK_COLS = 8, 2048


def _pack(vals):
    flat = jnp.concatenate([vals[n].reshape(-1) for n in REPLICATED])
    return jnp.pad(flat, (0, PACK_ROWS * PACK_COLS - flat.shape[0])).reshape(PACK_ROWS, PACK_COLS)


def _unpack(packed, like):
    flat = packed.reshape(-1)
    out, pos = {}, 0
    for n in REPLICATED:
        size = like[n].size
        out[n] = flat[pos:pos + size].reshape(like[n].shape)
        pos += size
    return out


def step(x, target, w_raw, m_raw, v_raw):
    wsh = {n: a[0] if a.ndim == 3 else a for n, a in w_raw.items()}
    gathered = all_gather([_narrow(n, wsh[n]) for n in EARLY], "ag_weights")
    full = {n: wsh[n] for n in REPLICATED}
    full.update({n: g if n == "w_in" else _from_gathered(n, g) for n, g in zip(EARLY, gathered)})

    sq, grad_x, grads = local_step(x, target, full, late={n: _narrow(n, wsh[n]) for n in LATE})

    slabs = [grads[n] if n == "w_in" else _to_slabs(n, _narrow(n, grads[n])) for n in EARLY]
    packed_g = _pack({n: grads[n] for n in REPLICATED})
    core = lax.axis_index("c")
    from_sibling = sibling_swap(slabs, "rs_sibling")
    chip_parts = []
    for n, s, f in zip(EARLY, slabs, from_sibling):
        mine = lax.dynamic_index_in_dim(s.reshape((N_CHIPS, 2) + s.shape[1:]), core, axis=1, keepdims=False)
        chip_parts.append(pair_add(mine, f, f"rs_add_{n}"))
    received = dict(zip(EARLY, chip_exchange(chip_parts, "rs_chips")))
    received.update({n: grads[n] for n in LATE})
    (small,) = all_gather([packed_g], "ag_small")

    out_g, out_d, out_m, out_v = {}, {}, {}, {}
    for n, parts in received.items():
        out_g[n], out_d[n], out_m[n], out_v[n] = adamw(parts, w_raw[n], m_raw[n], v_raw[n], f"adamw_{n}")
    pk = adamw(small, _pack(w_raw), _pack(m_raw), _pack(v_raw), "adamw_small")
    for dst, packed in zip((out_g, out_d, out_m, out_v), pk):
        dst.update(_unpack(packed, w_raw))
    loss = lax.psum(0.5 * jnp.sum(sq) / D_MODEL, ("x", "y", "c"))
    return loss, grad_x, out_g, out_d, out_m, out_v


def kernel(x, norm1_g, w_in, ssd_conv_w, ssd_conv_b, dt_bias, a_log, d_skip, ssd_norm_g, w_ssd_proj, q_norm_g, k_norm_g, w_attn_proj, w_out, norm2_g, w_up, ffn_conv_w, ffn_conv_b, w_down, loss_target, m_norm1_g, m_w_in, m_ssd_conv_w, m_ssd_conv_b, m_dt_bias, m_a_log, m_d_skip, m_ssd_norm_g, m_w_ssd_proj, m_q_norm_g, m_k_norm_g, m_w_attn_proj, m_w_out, m_norm2_g, m_w_up, m_ffn_conv_w, m_ffn_conv_b, m_w_down, v_norm1_g, v_w_in, v_ssd_conv_w, v_ssd_conv_b, v_dt_bias, v_a_log, v_d_skip, v_ssd_norm_g, v_w_ssd_proj, v_q_norm_g, v_k_norm_g, v_w_attn_proj, v_w_out, v_norm2_g, v_w_up, v_ffn_conv_w, v_ffn_conv_b, v_w_down):
    ws = (norm1_g, w_in, ssd_conv_w, ssd_conv_b, dt_bias, a_log, d_skip, ssd_norm_g, w_ssd_proj, q_norm_g, k_norm_g,
          w_attn_proj, w_out, norm2_g, w_up, ffn_conv_w, ffn_conv_b, w_down)
    ms = (m_norm1_g, m_w_in, m_ssd_conv_w, m_ssd_conv_b, m_dt_bias, m_a_log, m_d_skip, m_ssd_norm_g, m_w_ssd_proj,
          m_q_norm_g, m_k_norm_g, m_w_attn_proj, m_w_out, m_norm2_g, m_w_up, m_ffn_conv_w, m_ffn_conv_b, m_w_down)
    vs = (v_norm1_g, v_w_in, v_ssd_conv_w, v_ssd_conv_b, v_dt_bias, v_a_log, v_d_skip, v_ssd_norm_g, v_w_ssd_proj,
          v_q_norm_g, v_k_norm_g, v_w_attn_proj, v_w_out, v_norm2_g, v_w_up, v_ffn_conv_w, v_ffn_conv_b, v_w_down)
    loss, grad_x, g, d, m, v = step(x, loss_target, dict(zip(WEIGHTS, ws)), dict(zip(WEIGHTS, ms)), dict(zip(WEIGHTS, vs)))
    ordered = lambda dct: [dct[n] for n in WEIGHTS]
    return (loss, grad_x, *ordered(g), *ordered(d), *ordered(m), *ordered(v))
```

```python
import functools

import jax
import jax.numpy as jnp
from jax import lax
from jax.experimental import pallas as pl
from jax.experimental.pallas import tpu as pltpu

F32 = jnp.float32
BF16 = jnp.bfloat16
MXU = jnp.bfloat16
HIGHEST = lax.Precision.HIGHEST
VMEM_LIMIT_BYTES = 48 * 1024 * 1024
SUBLANES = 8
LANES = 128
N_DEV = 8

D_MODEL = 1024
D_INNER = 2048
SSD_P = 64
SSD_H = 32
SSD_G = 8
SSD_K = SSD_H // SSD_G
SSD_N = 128
SSD_Q = 128
SSD_CONV = 4
CONV_DIM = D_INNER + 2 * SSD_G * SSD_N
NORM_GROUP = D_INNER // SSD_G
ATT_GROUPS = 3
ATT_H = 8
ATT_HD = 64
ATT_BLK = 128
ATT_OUT = ATT_H * ATT_HD
ATT_DILATIONS = (1, 4, 16)
ATT_SCALE = ATT_HD ** -0.5
D_FF = 2816
FFN_CONV = 3
EPS = 1e-6
NEG = -1e30
IN_WIDTHS = (D_INNER, CONV_DIM, SSD_H, 3 * ATT_OUT, 3 * ATT_OUT, 3 * ATT_OUT, D_MODEL, D_MODEL)

ADAM_LR = 0.001
ADAM_B1 = 0.9
ADAM_B2 = 0.999
ADAM_EPS = 1e-08
ADAM_WD = 0.01
ADAM_STEP = 10


def _mm(a, b, dims):
    return lax.dot_general(a.astype(MXU), b.astype(MXU), (dims, ((), ())), preferred_element_type=F32)


def _dot_nn(a, b):
    return _mm(a, b, ((1,), (0,)))


def _dot_nt(a, b):
    return _mm(a, b, ((1,), (1,)))


def _dot_tn(a, b):
    return _mm(a, b, ((0,), (0,)))


def _dot_f32(a, b):
    return lax.dot_general(a, b, (((1,), (0,)), ((), ())), precision=HIGHEST, preferred_element_type=F32)


def _sigmoid(x):
    return 1.0 / (1.0 + jnp.exp(-x))


def _silu(x):
    return x * _sigmoid(x)


def _silu_grad(x):
    s = _sigmoid(x)
    return s * (1.0 + x * (1.0 - s))


def _softplus(x):
    return jnp.maximum(x, 0.0) + jnp.log(1.0 + jnp.exp(-jnp.abs(x)))


def _rms_fwd(x, g):
    r = lax.rsqrt(jnp.mean(x * x, axis=-1, keepdims=True) + EPS)
    return x * r * g


def _rms_bwd(x, g, dy):
    r = lax.rsqrt(jnp.mean(x * x, axis=-1, keepdims=True) + EPS)
    xh = x * r
    dyg = dy * g
    dx = r * (dyg - xh * jnp.mean(dyg * xh, axis=-1, keepdims=True))
    return dx, jnp.sum(dy * xh, axis=0, keepdims=True)


def _onehot_row(h, n=LANES):
    return (lax.broadcasted_iota(jnp.int32, (1, n), 1) == h).astype(F32)


def _onehot_col(h, n=LANES):
    return (lax.broadcasted_iota(jnp.int32, (n, 1), 0) == h).astype(F32)


def _head_expand_matrix():
    r = lax.broadcasted_iota(jnp.int32, (LANES, ATT_OUT), 0)
    c = lax.broadcasted_iota(jnp.int32, (LANES, ATT_OUT), 1)
    return (c // ATT_HD == r).astype(F32)


def _split_bf16(x, parts):
    out = []
    for _ in range(parts - 1):
        hi = x.astype(BF16).astype(F32)
        out.append(hi)
        x = x - hi
    out.append(x)
    return out


def _expand_heads(w):
    e = _head_expand_matrix()
    return sum(_dot_nn(p, e) for p in _split_bf16(w, 2))


def _reduce_heads(x):
    e = _head_expand_matrix()
    return sum(_dot_nt(p, e) for p in _split_bf16(x, 3))


def _shift_prev(cur, halo, s, first):
    if s == 0:
        return cur
    rolled = pltpu.roll(cur, s, 0)
    hr = jnp.where(first, 0.0, pltpu.roll(halo, s, 0))
    rows = lax.broadcasted_iota(jnp.int32, halo.shape, 0)
    head = jnp.where(rows < s, hr, rolled[:SUBLANES])
    if cur.shape[0] == SUBLANES:
        return head
    return jnp.concatenate([head, rolled[SUBLANES:]], axis=0)


def _shift_next(cur, halo, s, last):
    if s == 0:
        return cur
    tb = cur.shape[0]
    rolled = pltpu.roll(cur, tb - s, 0)
    hr = jnp.where(last, 0.0, pltpu.roll(halo, SUBLANES - s, 0))
    rows = lax.broadcasted_iota(jnp.int32, halo.shape, 0)
    tail = jnp.where(rows >= SUBLANES - s, hr, rolled[tb - SUBLANES:])
    return jnp.concatenate([rolled[:tb - SUBLANES], tail], axis=0)


def _conv_prev(x, halo, w, first, taps):
    acc = None
    for i in range(taps):
        term = w[i:i + 1, :] * _shift_prev(x, halo, taps - 1 - i, first)
        acc = term if acc is None else acc + term
    return acc


def _conv_pre(x, w, b, first, taps):
    cur, prev8, next8 = x
    tail = cur[cur.shape[0] - SUBLANES:]
    return b + _conv_prev(cur, prev8, w, first, taps), b + _conv_prev(next8, tail, w, False, taps)


def _conv_bwd(dpre, dpre_next8, x, w, ctx, taps):
    cur, prev8, _ = x
    dx, dws = None, []
    for i in range(taps):
        term = w[i:i + 1, :] * _shift_next(dpre, dpre_next8, taps - 1 - i, ctx.last)
        dx = term if dx is None else dx + term
        dws.append(jnp.sum(dpre * _shift_prev(cur, prev8, taps - 1 - i, ctx.first), axis=0, keepdims=True))
    return dx, jnp.concatenate(dws, axis=0), jnp.sum(dpre, axis=0, keepdims=True)


def _params(sem):
    return pltpu.CompilerParams(dimension_semantics=sem, vmem_limit_bytes=VMEM_LIMIT_BYTES)


N_CHIPS = N_DEV // 2
OTHER_CHIPS = (4, 2, 6)


class _Hosted:
    def __init__(self, arrays, out_shape, sems, ops):
        self.arrays, self.out_shape, self.sems, self.ops = list(arrays), list(out_shape), list(sems), ops
        self.n = len(self.arrays)
        self.specs = [pl.BlockSpec(memory_space=pl.ANY)] * self.n

    def begin(self, in_refs, out_refs, sem_refs, first):
        start, finish = self.ops(in_refs, out_refs, *sem_refs)
        pl.when(first)(start)
        return finish


NO_EXCHANGE = _Hosted((), (), (), None)


def _peer(k):
    x, y, c = lax.axis_index("x"), lax.axis_index("y"), lax.axis_index("c")
    px = 1 - x if k & 4 else x
    py = 1 - y if k & 2 else y
    pc = 1 - c if k & 1 else c
    return (px, py, pc), 4 * px + 2 * py + pc, 2 * px + py


def _remote(src, dst, send_sems, recv_sems, t, k, dev):
    return pltpu.make_async_remote_copy(src_ref=src, dst_ref=dst, send_sem=send_sems.at[t, k], recv_sem=recv_sems.at[t, k],
                                        device_id=dev, device_id_type=pl.DeviceIdType.MESH)


def direct_exchange(items):
    n = len(items)

    def ops(in_refs, out_refs, send_sems, recv_sems, local_sems):
        _, me, _ = _peer(0)
        part = lambda t, pid: in_refs[t] if items[t][1] == "gather" else in_refs[t].at[pid]

        def copy(t, k, arriving):
            dev, pid, _ = _peer(k)
            return _remote(part(t, pid), out_refs[t].at[pid if arriving else me], send_sems, recv_sems, t, k, dev)

        def own(t):
            return pltpu.make_async_copy(part(t, me), out_refs[t].at[me], local_sems.at[t])

        def start():
            for t in range(n):
                own(t).start()
                for k in range(1, N_DEV):
                    copy(t, k, False).start()

        def finish():
            for t in range(n):
                for k in range(1, N_DEV):
                    copy(t, k, True).wait_recv()
            for t in range(n):
                for k in range(1, N_DEV):
                    copy(t, k, False).wait_send()
                own(t).wait()

        return start, finish

    out_shape = [jax.ShapeDtypeStruct((N_DEV,) + tuple(a.shape if m == "gather" else a.shape[1:]), a.dtype)
                 for a, m in items]
    sems = [pltpu.SemaphoreType.DMA((n, N_DEV)), pltpu.SemaphoreType.DMA((n, N_DEV)), pltpu.SemaphoreType.DMA((n,))]
    return _Hosted([a for a, _ in items], out_shape, sems, ops)


def sibling_exchange(arrays):
    n = len(arrays)

    def ops(in_refs, out_refs, send_sems, recv_sems):
        sib, _, _ = _peer(1)
        c = lax.axis_index("c")
        copy = lambda t, q: _remote(in_refs[t].at[2 * q + (1 - c)], out_refs[t].at[q], send_sems, recv_sems, t, q, sib)

        def start():
            for t in range(n):
                for q in range(N_CHIPS):
                    copy(t, q).start()

        def finish():
            for t in range(n):
                for q in range(N_CHIPS):
                    copy(t, q).wait_recv()
            for t in range(n):
                for q in range(N_CHIPS):
                    copy(t, q).wait_send()

        return start, finish

    out_shape = [jax.ShapeDtypeStruct((N_CHIPS,) + a.shape[1:], a.dtype) for a in arrays]
    sems = [pltpu.SemaphoreType.DMA((n, N_CHIPS)), pltpu.SemaphoreType.DMA((n, N_CHIPS))]
    return _Hosted(arrays, out_shape, sems, ops)


def chip_exchange(arrays):
    n = len(arrays)

    def ops(in_refs, out_refs, send_sems, recv_sems, local_sems):
        _, _, mine = _peer(0)

        def copy(t, k, arriving):
            dev, _, q = _peer(k)
            return _remote(in_refs[t].at[q], out_refs[t].at[q if arriving else mine], send_sems, recv_sems, t, k, dev)

        def own(t):
            return pltpu.make_async_copy(in_refs[t].at[mine], out_refs[t].at[mine], local_sems.at[t])

        def start():
            for t in range(n):
                own(t).start()
                for k in OTHER_CHIPS:
                    copy(t, k, False).start()

        def finish():
            for t in range(n):
                for k in OTHER_CHIPS:
                    copy(t, k, True).wait_recv()
            for t in range(n):
                for k in OTHER_CHIPS:
                    copy(t, k, False).wait_send()
                own(t).wait()

        return start, finish

    out_shape = [jax.ShapeDtypeStruct(a.shape, a.dtype) for a in arrays]
    sems = [pltpu.SemaphoreType.DMA((n, N_DEV)), pltpu.SemaphoreType.DMA((n, N_DEV)), pltpu.SemaphoreType.DMA((n,))]
    return _Hosted(arrays, out_shape, sems, ops)


MATMUL_VMEM_BUDGET = 34 * 1024 * 1024


V7X_MXU_FLOPS = 996e12
V7X_HBM_BYTES_PER_S = 3.4e12
GRID_STEP_S = 0.35e-6


def _tile_sizes(dim, cap):
    return [t for t in range(LANES, min(dim, cap) + 1, LANES) if dim % t == 0] or [dim]


def _matmul_tiles(m, n, k, a_bytes, b_bytes, add_bytes, out_bytes):
    best = None
    for tk in _tile_sizes(k, 8192):
        nk = k // tk
        for tn in _tile_sizes(n, 2048):
            for tm in _tile_sizes(m, 2048):
                io = tm * tk * a_bytes + tk * tn * b_bytes
                ends = tm * tn * (add_bytes + out_bytes)
                need = 2 * (io + ends) + tm * tn * 4 * (2 if nk > 1 else 1)
                if need > MATMUL_VMEM_BUDGET:
                    continue
                step = max(2.0 * tm * tn * tk / V7X_MXU_FLOPS, (io + ends / nk) / V7X_HBM_BYTES_PER_S)
                if nk > 1:
                    step += tm * tn * 8 / V7X_HBM_BYTES_PER_S
                cost = (m // tm) * (n // tn) * nk * (step + GRID_STEP_S)
                if best is None or cost < best[0]:
                    best = (cost, tm, tn, tk)
    if best is None:
        raise ValueError((m, n, k))
    return best[1:]


def matmul(a, b, name, ta=False, tb=False, add=None, out_dtype=F32, comm=NO_EXCHANGE):
    assert not (ta and tb)
    m, k = (a.shape[1], a.shape[0]) if ta else a.shape
    n = b.shape[0] if tb else b.shape[1]
    assert (b.shape[1] if tb else b.shape[0]) == k
    tm, tn, tk = _matmul_tiles(m, n, k, a.dtype.itemsize, b.dtype.itemsize,
                               0 if add is None else add.dtype.itemsize, jnp.dtype(out_dtype).itemsize)
    nk = k // tk
    grid = (m // tm, n // tn, nk)
    dims = ((0,), (0,)) if ta else (((1,), (1,)) if tb else ((1,), (0,)))
    n_in = 2 if add is None else 3
    n_acc = 0 if nk == 1 else 1

    def body(*refs):
        a_ref, b_ref = refs[:2]
        o_ref = refs[n_in + comm.n]
        ids = [pl.program_id(d) for d in range(3)]
        if comm.n:
            first = functools.reduce(jnp.logical_and, [i == 0 for i in ids])
            last = functools.reduce(jnp.logical_and, [i == g - 1 for i, g in zip(ids, grid)])
            done = comm.begin(refs[n_in:n_in + comm.n], refs[n_in + comm.n + 1:n_in + 2 * comm.n + 1],
                              refs[n_in + 2 * comm.n + 1 + n_acc:], first)

        def finish(r):
            if add is not None:
                r = r + refs[2][...].astype(F32)
            o_ref[...] = r.astype(out_dtype)

        if nk == 1:
            finish(_mm(a_ref[...], b_ref[...], dims))
        else:
            acc = refs[n_in + 2 * comm.n + 1]

            @pl.when(ids[2] == 0)
            def _():
                acc[...] = jnp.zeros_like(acc)

            acc[...] += _mm(a_ref[...], b_ref[...], dims)

            @pl.when(ids[2] == nk - 1)
            def _():
                finish(acc[...])

        if comm.n:
            pl.when(last)(done)

    a_spec = pl.BlockSpec((tk, tm), lambda i, j, kk: (kk, i)) if ta else pl.BlockSpec((tm, tk), lambda i, j, kk: (i, kk))
    b_spec = pl.BlockSpec((tn, tk), lambda i, j, kk: (j, kk)) if tb else pl.BlockSpec((tk, tn), lambda i, j, kk: (kk, j))
    in_specs = [a_spec, b_spec]
    args = [a, b]
    if add is not None:
        in_specs.append(pl.BlockSpec((tm, tn), lambda i, j, kk: (i, j)))
        args.append(add)
    res = pl.pallas_call(
        body, name=name,
        grid=grid,
        in_specs=in_specs + comm.specs,
        out_specs=[pl.BlockSpec((tm, tn), lambda i, j, kk: (i, j))] + comm.specs,
        out_shape=[jax.ShapeDtypeStruct((m, n), out_dtype)] + comm.out_shape,
        scratch_shapes=([] if nk == 1 else [pltpu.VMEM((tm, tn), F32)]) + comm.sems,
        compiler_params=_params(("arbitrary",) * 3 if comm.n else ("parallel", "parallel", "arbitrary")),
    )(*args, *comm.arrays)
    return res if comm.n else res[0]


class _Ctx:
    def __init__(self, first, last):
        self.first = first
        self.last = last


def rowwise(fn, name, rows, seq, tb, ncol, ins, params=(), outs=(), accs=()):
    assert rows % tb == 0 and seq % tb == 0 and tb % 16 == 0
    bps = seq // tb
    nrow = rows // tb
    r8 = tb // SUBLANES
    args, in_specs = [], []
    for arr, w, off, halo in ins:
        args.append(arr)
        in_specs.append(pl.BlockSpec((tb, w), lambda j, i, off=off: (i, off + j)))
        if halo in ("prev", "both"):
            args.append(arr)
            in_specs.append(pl.BlockSpec((SUBLANES, w), lambda j, i, off=off: (jnp.maximum(i * r8 - 1, 0), off + j)))
        if halo in ("next", "both"):
            args.append(arr)
            in_specs.append(pl.BlockSpec(
                (SUBLANES, w), lambda j, i, off=off: (jnp.minimum((i + 1) * r8, rows // SUBLANES - 1), off + j)))
    for arr, w, off in params:
        args.append(arr)
        if w is None:
            in_specs.append(pl.BlockSpec(arr.shape, lambda j, i: (0, 0)))
        else:
            in_specs.append(pl.BlockSpec((arr.shape[0], w), lambda j, i, off=off: (0, off + j)))
    out_shape, out_specs = [], []
    for total, w, off, dt in outs:
        out_shape.append(jax.ShapeDtypeStruct((rows, total), dt))
        out_specs.append(pl.BlockSpec((tb, w), lambda j, i, off=off: (i, off + j)))
    for r, w in accs:
        out_shape.append(jax.ShapeDtypeStruct((r, ncol * w), F32))
        out_specs.append(pl.BlockSpec((r, w), lambda j, i: (0, j)))
    n_out, n_acc = len(outs), len(accs)

    def body(*refs):
        i = pl.program_id(1)
        pos = 0
        vals = []
        for _, _, _, halo in ins:
            cur = refs[pos][...]
            pos += 1
            if halo is None:
                vals.append(cur)
            elif halo == "both":
                vals.append((cur, refs[pos][...], refs[pos + 1][...]))
                pos += 2
            else:
                vals.append((cur, refs[pos][...]))
                pos += 1
        for _ in params:
            vals.append(refs[pos][...])
            pos += 1
        ctx = _Ctx(i % bps == 0, i % bps == bps - 1)
        res = fn(ctx, *vals)
        if not isinstance(res, (tuple, list)):
            res = (res,)
        assert len(res) == n_out + n_acc
        for q in range(n_out):
            refs[pos + q][...] = res[q].astype(refs[pos + q].dtype)
        for q in range(n_acc):
            ref, val = refs[pos + n_out + q], res[n_out + q]

            @pl.when(i == 0)
            def _(ref=ref, val=val):
                ref[...] = val

            @pl.when(i != 0)
            def _(ref=ref, val=val):
                ref[...] += val

    res = pl.pallas_call(
        body, name=name,
        grid=(ncol, nrow),
        in_specs=in_specs,
        out_specs=out_specs,
        out_shape=out_shape,
        compiler_params=_params(("parallel", "arbitrary")),
    )(*args)
    return res


GROUP_W = SSD_K * SSD_P


def _tri(lower):
    r = lax.broadcasted_iota(jnp.int32, (SSD_Q, SSD_Q), 0)
    c = lax.broadcasted_iota(jnp.int32, (SSD_Q, SSD_Q), 1)
    return r >= c if lower else r <= c


def _group_masks():
    lane = lax.broadcasted_iota(jnp.int32, (1, GROUP_W), 1) // SSD_P
    row = lax.broadcasted_iota(jnp.int32, (GROUP_W, 1), 0) // SSD_P
    return [lane == k for k in range(SSD_K)], [row == k for k in range(SSD_K)]


def _per_head(masks, vals):
    out = jnp.where(masks[0], vals[0], 0.0)
    for m, v in zip(masks[1:], vals[1:]):
        out = jnp.where(m, v, out)
    return out


def _headsum(prod, g, lane_masks):
    out = None
    for k in range(SSD_K):
        term = jnp.sum(jnp.where(lane_masks[k], prod, 0.0), axis=1, keepdims=True) * _onehot_row(g * SSD_K + k)
        out = term if out is None else out + term
    return out


def ssd_fwd(xact, dtraw, dt_bias, a_log, d_skip, n_seq, seq, comm=NO_EXCHANGE):
    nc = seq // SSD_Q
    rows = n_seq * seq
    nx = comm.n

    def body(*refs):
        xact_ref, dtraw_ref, bias_ref, alog_ref, dskip_ref = refs[:5]
        y_ref, sin_ref = refs[5 + nx:7 + nx]
        state, cs_s, cst_s, dt_s = refs[7 + 2 * nx:11 + 2 * nx]
        b, c = pl.program_id(0), pl.program_id(1)
        if nx:
            finish = comm.begin(refs[5:5 + nx], refs[7 + nx:7 + 2 * nx], refs[11 + 2 * nx:],
                                jnp.logical_and(b == 0, c == 0))

        @pl.when(c == 0)
        def _():
            state[...] = jnp.zeros_like(state)

        sin_ref[0] = state[...]
        dt = _softplus(dtraw_ref[...] + bias_ref[...])
        a = dt * (-jnp.exp(alog_ref[...]))
        cs = _dot_f32(_tri(True).astype(F32), a)
        cs_s[...] = cs
        cst_s[...] = cs.T
        dt_s[...] = dt
        causal = _tri(True)
        lane_masks, row_masks = _group_masks()
        for g in range(SSD_G):
            heads = [g * SSD_K + k for k in range(SSD_K)]
            bg = xact_ref[:, pl.ds(D_INNER + g * SSD_N, SSD_N)]
            cg = xact_ref[:, pl.ds(D_INNER + (SSD_G + g) * SSD_N, SSD_N)]
            xg = xact_ref[:, pl.ds(g * GROUP_W, GROUP_W)]
            cols = [cs_s[:, pl.ds(h, 1)] for h in heads]
            lasts = [cs_s[pl.ds(SSD_Q - 1, 1), pl.ds(h, 1)] for h in heads]
            xdg = xg * _per_head(lane_masks, [dt_s[:, pl.ds(h, 1)] for h in heads])
            sg = state[g]
            gm = _dot_nt(cg, bg)
            y = (_per_head(lane_masks, [jnp.exp(c_) for c_ in cols]) * _dot_nt(cg, sg)
                 + _per_head(lane_masks, [dskip_ref[:, pl.ds(h, 1)] for h in heads]) * xg)
            mats = [gm * jnp.exp(jnp.where(causal, cols[k] - cst_s[pl.ds(h, 1), :], NEG)) for k, h in enumerate(heads)]
            y4 = _dot_nn(jnp.concatenate(mats, axis=0), xdg)
            y = y + _per_head(lane_masks, [y4[k * SSD_Q:(k + 1) * SSD_Q] for k in range(SSD_K)])
            y_ref[:, pl.ds(g * GROUP_W, GROUP_W)] = y
            w = _per_head(lane_masks, [jnp.exp(l_ - c_) for l_, c_ in zip(lasts, cols)])
            state[g] = _per_head(row_masks, [jnp.exp(l_) for l_ in lasts]) * sg + _dot_tn(w * xdg, bg)
        if nx:
            pl.when(jnp.logical_and(b == n_seq - 1, c == nc - 1))(finish)

    vec = pl.BlockSpec((1, LANES), lambda b, c: (0, 0))
    return pl.pallas_call(
        body, name="ssd_fwd",
        grid=(n_seq, nc),
        in_specs=[pl.BlockSpec((SSD_Q, CONV_DIM), lambda b, c: (b * nc + c, 0)),
                  pl.BlockSpec((SSD_Q, LANES), lambda b, c: (b * nc + c, 0)), vec, vec, vec] + comm.specs,
        out_specs=[pl.BlockSpec((SSD_Q, D_INNER), lambda b, c: (b * nc + c, 0)),
                   pl.BlockSpec((1, SSD_G, GROUP_W, SSD_N), lambda b, c: (b * nc + c, 0, 0, 0))] + comm.specs,
        out_shape=[jax.ShapeDtypeStruct((rows, D_INNER), F32),
                   jax.ShapeDtypeStruct((n_seq * nc, SSD_G, GROUP_W, SSD_N), F32)] + comm.out_shape,
        scratch_shapes=[pltpu.VMEM((SSD_G, GROUP_W, SSD_N), F32), pltpu.VMEM((SSD_Q, LANES), F32),
                        pltpu.VMEM((LANES, SSD_Q), F32), pltpu.VMEM((SSD_Q, LANES), F32)] + comm.sems,
        compiler_params=_params(("arbitrary", "arbitrary")),
    )(xact, dtraw, dt_bias, a_log, d_skip, *comm.arrays)


def ssd_bwd(xact, dtraw, dt_bias, a_log, d_skip, sin, dy, n_seq, seq, comm=NO_EXCHANGE):
    nc = seq // SSD_Q
    rows = n_seq * seq
    nx = comm.n

    def body(*refs):
        xact_ref, dtraw_ref, bias_ref, alog_ref, dskip_ref, sin_ref, dy_ref = refs[:7]
        dx_ref, ddt_ref, dbias_ref, dalog_ref, ddskip_ref = refs[7 + nx:12 + nx]
        dstate, cs_s, cst_s, dt_s = refs[12 + 2 * nx:16 + 2 * nx]
        b, c = pl.program_id(0), pl.program_id(1)
        if nx:
            finish = comm.begin(refs[7:7 + nx], refs[12 + nx:12 + 2 * nx], refs[16 + 2 * nx:],
                                jnp.logical_and(b == 0, c == 0))

        @pl.when(c == 0)
        def _():
            dstate[...] = jnp.zeros_like(dstate)

        pre = dtraw_ref[...] + bias_ref[...]
        dt = _softplus(pre)
        a_neg = -jnp.exp(alog_ref[...])
        cs = _dot_f32(_tri(True).astype(F32), dt * a_neg)
        cs_s[...] = cs
        cst_s[...] = cs.T
        dt_s[...] = dt
        causal, anti = _tri(True), _tri(False)
        is_last_row = lax.broadcasted_iota(jnp.int32, (SSD_Q, 1), 0) == SSD_Q - 1
        lane_masks, row_masks = _group_masks()
        dcs_cf = jnp.zeros((SSD_Q, LANES), F32)
        dcs_rf = jnp.zeros((LANES, SSD_Q), F32)
        ddt_cf = jnp.zeros((SSD_Q, LANES), F32)
        dd_vec = jnp.zeros((1, LANES), F32)
        dlast_vec = jnp.zeros((1, LANES), F32)
        for g in range(SSD_G):
            heads = [g * SSD_K + k for k in range(SSD_K)]
            bg = xact_ref[:, pl.ds(D_INNER + g * SSD_N, SSD_N)]
            cg = xact_ref[:, pl.ds(D_INNER + (SSD_G + g) * SSD_N, SSD_N)]
            xg = xact_ref[:, pl.ds(g * GROUP_W, GROUP_W)]
            dyg = dy_ref[:, pl.ds(g * GROUP_W, GROUP_W)]
            cols = [cs_s[:, pl.ds(h, 1)] for h in heads]
            rws = [cst_s[pl.ds(h, 1), :] for h in heads]
            lasts = [cs_s[pl.ds(SSD_Q - 1, 1), pl.ds(h, 1)] for h in heads]
            e_lasts = [jnp.exp(l_) for l_ in lasts]
            dtg = _per_head(lane_masks, [dt_s[:, pl.ds(h, 1)] for h in heads])
            dskg = _per_head(lane_masks, [dskip_ref[:, pl.ds(h, 1)] for h in heads])
            e_col = _per_head(lane_masks, [jnp.exp(c_) for c_ in cols])
            w = _per_head(lane_masks, [jnp.exp(l_ - c_) for l_, c_ in zip(lasts, cols)])
            xdg = xg * dtg
            sg = sin_ref[0, g]
            dsn = dstate[g]
            gm = _dot_nt(cg, bg)
            gmt = _dot_nt(bg, cg)
            y_off = e_col * _dot_nt(cg, sg)
            d_cs = e_col * dyg
            dcg = _dot_nn(d_cs, sg)
            dsp = _dot_tn(d_cs, cg) + _per_head(row_masks, e_lasts) * dsn
            dbg = _dot_nn(w * xdg, dsn)
            dtt = _dot_nt(bg, dsn)
            dxd = w * dtt
            dw = _headsum(dtt * xdg * w, g, lane_masks)
            dcs_cf = dcs_cf + _headsum(dyg * y_off, g, lane_masks) - dw
            dlast_vec = dlast_vec + jnp.sum(dw, axis=0, keepdims=True)
            dsn_s = dsn * sg
            segs = [cols[k] - rws[k] for k in range(SSD_K)]
            decays = [jnp.exp(jnp.where(causal, s_, NEG)) for s_ in segs]
            dm4 = _dot_nt(jnp.concatenate([jnp.where(m_, dyg, 0.0) for m_ in lane_masks], axis=0), xdg)
            z4 = _dot_nn(jnp.concatenate([gmt * jnp.exp(jnp.where(anti, -s_, NEG)) for s_ in segs], axis=0), dyg)
            dxd = dxd + _per_head(lane_masks, [z4[k * SSD_Q:(k + 1) * SSD_Q] for k in range(SSD_K)])
            dgm = jnp.zeros((SSD_Q, SSD_Q), F32)
            for k, h in enumerate(heads):
                dm = dm4[k * SSD_Q:(k + 1) * SSD_Q]
                dseg = dm * gm * decays[k]
                dgm = dgm + dm * decays[k]
                oh_r = _onehot_row(h)
                dcs_cf = dcs_cf + jnp.sum(dseg, axis=1, keepdims=True) * oh_r
                dcs_rf = dcs_rf - _onehot_col(h) * jnp.sum(dseg, axis=0, keepdims=True)
                dlast_vec = dlast_vec + jnp.sum(jnp.where(row_masks[k], dsn_s, 0.0), keepdims=True) * e_lasts[k] * oh_r
            dx_ref[:, pl.ds(g * GROUP_W, GROUP_W)] = dxd * dtg + dskg * dyg
            ddt_cf = ddt_cf + _headsum(dxd * xg, g, lane_masks)
            dd_vec = dd_vec + jnp.sum(_headsum(dyg * xg, g, lane_masks), axis=0, keepdims=True)
            dstate[g] = dsp
            dx_ref[:, pl.ds(D_INNER + g * SSD_N, SSD_N)] = dbg + _dot_tn(dgm, cg)
            dx_ref[:, pl.ds(D_INNER + (SSD_G + g) * SSD_N, SSD_N)] = dcg + _dot_nn(dgm, bg)
        dcs = dcs_cf + dcs_rf.T + jnp.where(is_last_row, dlast_vec, 0.0)
        da = _dot_f32(_tri(False).astype(F32), dcs)
        ddt = ddt_cf + da * a_neg
        ddtraw = ddt * _sigmoid(pre)
        ddt_ref[...] = ddtraw.astype(ddt_ref.dtype)
        dbias = jnp.sum(ddtraw, axis=0, keepdims=True)
        dalog = jnp.sum(da * dt, axis=0, keepdims=True) * a_neg
        first_step = jnp.logical_and(b == 0, c == 0)

        @pl.when(first_step)
        def _():
            dbias_ref[...] = dbias
            dalog_ref[...] = dalog
            ddskip_ref[...] = dd_vec

        @pl.when(jnp.logical_not(first_step))
        def _():
            dbias_ref[...] += dbias
            dalog_ref[...] += dalog
            ddskip_ref[...] += dd_vec

        if nx:
            pl.when(jnp.logical_and(b == n_seq - 1, c == nc - 1))(finish)

    def rowblk(b, c):
        return b * nc + (nc - 1 - c)

    vec = pl.BlockSpec((1, LANES), lambda b, c: (0, 0))
    return pl.pallas_call(
        body, name="ssd_bwd",
        grid=(n_seq, nc),
        in_specs=[pl.BlockSpec((SSD_Q, CONV_DIM), lambda b, c: (rowblk(b, c), 0)),
                  pl.BlockSpec((SSD_Q, LANES), lambda b, c: (rowblk(b, c), 0)), vec, vec, vec,
                  pl.BlockSpec((1, SSD_G, GROUP_W, SSD_N), lambda b, c: (rowblk(b, c), 0, 0, 0)),
                  pl.BlockSpec((SSD_Q, D_INNER), lambda b, c: (rowblk(b, c), 0))] + comm.specs,
        out_specs=[pl.BlockSpec((SSD_Q, CONV_DIM), lambda b, c: (rowblk(b, c), 0)),
                   pl.BlockSpec((SSD_Q, LANES), lambda b, c: (rowblk(b, c), 0)), vec, vec, vec] + comm.specs,
        out_shape=[jax.ShapeDtypeStruct((rows, CONV_DIM), F32), jax.ShapeDtypeStruct((rows, LANES), BF16),
                   jax.ShapeDtypeStruct((1, LANES), F32), jax.ShapeDtypeStruct((1, LANES), F32),
                   jax.ShapeDtypeStruct((1, LANES), F32)] + comm.out_shape,
        scratch_shapes=[pltpu.VMEM((SSD_G, GROUP_W, SSD_N), F32), pltpu.VMEM((SSD_Q, LANES), F32),
                        pltpu.VMEM((LANES, SSD_Q), F32), pltpu.VMEM((SSD_Q, LANES), F32)] + comm.sems,
        compiler_params=_params(("arbitrary", "arbitrary")),
    )(xact, dtraw, dt_bias, a_log, d_skip, sin, dy, *comm.arrays)


QKV_W = 3 * ATT_OUT
PAIR_W = 2 * ATT_HD
HEAD_PAIRS = ATT_H // 2
PREP_ROWS = 512


def _by_residue(a, n_seq, seq, dil):
    if dil == 1:
        return a
    return a.reshape(n_seq, seq // dil, dil, a.shape[1]).transpose(0, 2, 1, 3).reshape(a.shape)


def _by_token(a, n_seq, seq, dil):
    if dil == 1:
        return a
    return a.reshape(n_seq, dil, seq // dil, a.shape[1]).transpose(0, 2, 1, 3).reshape(a.shape)


def _head_sums(x, fn):
    lo = jnp.logical_not(lax.broadcasted_iota(jnp.int32, (1, 2 * ATT_HD), 1) >= ATT_HD)
    parts = []
    for p in range(ATT_H // 2):
        slab = x[:, p * 2 * ATT_HD:(p + 1) * 2 * ATT_HD]
        s_lo = fn(jnp.sum(jnp.where(lo, slab, 0.0), axis=1, keepdims=True))
        s_hi = fn(jnp.sum(jnp.where(lo, 0.0, slab), axis=1, keepdims=True))
        parts.append(jnp.where(lo, s_lo, s_hi))
    return jnp.concatenate(parts, axis=1)


def _head_rstd(x):
    return _head_sums(x * x, lambda s: lax.rsqrt(s * (1.0 / ATT_HD) + EPS))


def _head_rms_bwd(x, g_t, dy):
    r = _head_rstd(x)
    xh = x * r
    dyg = dy * g_t
    mean = _head_sums(dyg * xh, lambda s: s * (1.0 / ATT_HD))
    return r * (dyg - xh * mean), jnp.sum(dy * xh, axis=0, keepdims=True)


def qk_prep(qkv, gq_t, gk_t, rows, name):
    tb = min(PREP_ROWS, rows)

    def body(x_ref, gq_ref, gk_ref, o_ref):
        q = x_ref[:, pl.ds(0, ATT_OUT)]
        k = x_ref[:, pl.ds(ATT_OUT, ATT_OUT)]
        o_ref[:, pl.ds(0, ATT_OUT)] = (q * _head_rstd(q) * (gq_ref[...] * ATT_SCALE)).astype(o_ref.dtype)
        o_ref[:, pl.ds(ATT_OUT, ATT_OUT)] = (k * _head_rstd(k) * gk_ref[...]).astype(o_ref.dtype)
        o_ref[:, pl.ds(2 * ATT_OUT, ATT_OUT)] = x_ref[:, pl.ds(2 * ATT_OUT, ATT_OUT)].astype(o_ref.dtype)

    gspec = pl.BlockSpec((1, ATT_OUT), lambda i: (0, 0))
    blk = pl.BlockSpec((tb, QKV_W), lambda i: (i, 0))
    return pl.pallas_call(
        body, name=name,
        grid=(rows // tb,),
        in_specs=[blk, gspec, gspec],
        out_specs=blk,
        out_shape=jax.ShapeDtypeStruct((rows, QKV_W), MXU),
        compiler_params=_params(("parallel",)),
    )(qkv, gq_t, gk_t)


def _lane_hi():
    return lax.broadcasted_iota(jnp.int32, (1, PAIR_W), 1) >= ATT_HD


def _band_mask2(first_valid, query_rows):
    i = lax.broadcasted_iota(jnp.int32, (ATT_BLK, 2 * ATT_BLK), 0)
    j = lax.broadcasted_iota(jnp.int32, (ATT_BLK, 2 * ATT_BLK), 1)
    left = j < ATT_BLK
    right = jnp.logical_not(left)
    if query_rows:
        return jnp.logical_or(jnp.logical_and(jnp.logical_and(left, i <= j), first_valid),
                              jnp.logical_and(right, i >= j - ATT_BLK))
    return jnp.logical_or(jnp.logical_and(left, j >= i),
                          jnp.logical_and(jnp.logical_and(right, j - ATT_BLK <= i), first_valid))


def _only_head(slab, hi):
    keep = _lane_hi() if hi else jnp.logical_not(_lane_hi())
    return jnp.where(keep, slab, jnp.zeros_like(slab))


def attn_fwd(nq, n_seq, seq, dil, name):
    nb = seq // dil // ATT_BLK
    rows = n_seq * seq

    def body(cur_ref, prev_ref, o_ref, lse_ref, s_scr, p_scr):
        n = pl.program_id(1)
        mask = _band_mask2(n > 0, True)
        for h in range(ATT_H):
            sl = pl.ds((h // 2) * PAIR_W, PAIR_W)
            ks = pl.ds(ATT_OUT + (h // 2) * PAIR_W, PAIR_W)
            kcat = jnp.concatenate([prev_ref[:, ks], cur_ref[:, ks]], axis=0)
            s_scr[h] = jnp.where(mask, _dot_nt(_only_head(cur_ref[:, sl], h % 2), kcat), NEG)
        s_all = s_scr[...]
        mx = jnp.max(s_all, axis=2, keepdims=True)
        p_all = jnp.exp(s_all - mx)
        den = jnp.sum(p_all, axis=2, keepdims=True)
        p_scr[...] = p_all.astype(p_scr.dtype)
        inv = 1.0 / den
        lse = mx + jnp.log(den)
        lse_blk = jnp.zeros((ATT_BLK, LANES), F32)
        for h in range(ATT_H):
            lse_blk = lse_blk + lse[h] * _onehot_row(h)
        lse_ref[...] = lse_blk
        for pr in range(HEAD_PAIRS):
            vs = pl.ds(2 * ATT_OUT + pr * PAIR_W, PAIR_W)
            vcat = jnp.concatenate([prev_ref[:, vs], cur_ref[:, vs]], axis=0)
            lo = _dot_nn(p_scr[2 * pr], vcat) * inv[2 * pr]
            hi = _dot_nn(p_scr[2 * pr + 1], vcat) * inv[2 * pr + 1]
            o_ref[:, pl.ds(pr * PAIR_W, PAIR_W)] = jnp.where(_lane_hi(), hi, lo)

    def blk(width, shift):
        if shift:
            return pl.BlockSpec((ATT_BLK, width), lambda s, n: (s * nb + jnp.maximum(n - 1, 0), 0))
        return pl.BlockSpec((ATT_BLK, width), lambda s, n: (s * nb + n, 0))

    return pl.pallas_call(
        body, name=name,
        grid=(n_seq * dil, nb),
        in_specs=[blk(QKV_W, 0), blk(QKV_W, -1)],
        out_specs=[blk(ATT_OUT, 0), blk(LANES, 0)],
        out_shape=[jax.ShapeDtypeStruct((rows, ATT_OUT), F32), jax.ShapeDtypeStruct((rows, LANES), F32)],
        scratch_shapes=[pltpu.VMEM((ATT_H, ATT_BLK, 2 * ATT_BLK), F32), pltpu.VMEM((ATT_H, ATT_BLK, 2 * ATT_BLK), MXU)],
        compiler_params=_params(("parallel", "arbitrary")),
    )(nq, nq)


def attn_bwd(nq, do, lse, wts, rsum, n_seq, seq, dil, name):
    nb = seq // dil // ATT_BLK

    def body(prev_ref, cur_ref, nxt_ref, do_c, do_x, lse_c, lse_x, wt_c, wt_x, rs_c, rs_x, dn_ref):
        n = pl.program_id(1)
        mask_q = _band_mask2(n > 0, True)
        mask_k = _band_mask2(n < nb - 1, False)
        wc, wx = wt_c[...], wt_x[...]
        lse_t = jnp.concatenate([lse_c[...].T, lse_x[...].T], axis=1)
        dl_t = jnp.concatenate([(-wc * rs_c[...]).T, (-wx * rs_x[...]).T], axis=1)
        for pr in range(HEAD_PAIRS):
            sl = pl.ds(pr * PAIR_W, PAIR_W)
            ks = pl.ds(ATT_OUT + pr * PAIR_W, PAIR_W)
            vs = pl.ds(2 * ATT_OUT + pr * PAIR_W, PAIR_W)
            he, ho = pl.ds(2 * pr, 1), pl.ds(2 * pr + 1, 1)
            q_c, k_c, v_c = cur_ref[:, sl], cur_ref[:, ks], cur_ref[:, vs]
            qcat = jnp.concatenate([q_c, nxt_ref[:, sl]], axis=0)
            kcat = jnp.concatenate([prev_ref[:, ks], k_c], axis=0)
            vcat = jnp.concatenate([prev_ref[:, vs], v_c], axis=0)
            dog_c = do_c[:, sl] * jnp.where(_lane_hi(), wt_c[:, ho], wt_c[:, he])
            dog_x = do_x[:, sl] * jnp.where(_lane_hi(), wt_x[:, ho], wt_x[:, he])
            dog = jnp.concatenate([dog_c, dog_x], axis=0).astype(MXU)
            res = []
            for hi in (0, 1):
                h = 2 * pr + hi
                one = pl.ds(h, 1)
                dl_col = -wt_c[:, one] * rs_c[:, one]
                p_q = jnp.exp(jnp.where(mask_q, _dot_nt(_only_head(q_c, hi), kcat) - lse_c[:, one], NEG))
                ds_q = p_q * (_dot_nt(_only_head(dog[:ATT_BLK], hi), vcat) + dl_col)
                dq = _dot_nn(ds_q, kcat)
                p_t = jnp.exp(jnp.where(mask_k, _dot_nt(_only_head(k_c, hi), qcat) - lse_t[h:h + 1, :], NEG))
                ds_t = p_t * (_dot_nt(_only_head(v_c, hi), dog) + dl_t[h:h + 1, :])
                res.append((dq, _dot_nn(ds_t, qcat), _dot_nn(p_t, dog)))
            for t, dst in enumerate((sl, ks, vs)):
                dn_ref[:, dst] = jnp.where(_lane_hi(), res[1][t], res[0][t])

    def at(shift, width):
        if shift < 0:
            return pl.BlockSpec((ATT_BLK, width), lambda s, n: (s * nb + jnp.maximum(n - 1, 0), 0))
        if shift > 0:
            return pl.BlockSpec((ATT_BLK, width), lambda s, n: (s * nb + jnp.minimum(n + 1, nb - 1), 0))
        return pl.BlockSpec((ATT_BLK, width), lambda s, n: (s * nb + n, 0))

    return pl.pallas_call(
        body, name=name,
        grid=(n_seq * dil, nb),
        in_specs=[at(-1, QKV_W), at(0, QKV_W), at(1, QKV_W), at(0, ATT_OUT), at(1, ATT_OUT),
                  at(0, LANES), at(1, LANES), at(0, LANES), at(1, LANES), at(0, LANES), at(1, LANES)],
        out_specs=at(0, QKV_W),
        out_shape=jax.ShapeDtypeStruct((n_seq * seq, QKV_W), F32),
        compiler_params=_params(("parallel", "arbitrary")),
    )(nq, nq, nq, do, do, lse, lse, wts, wts, rsum, rsum)


def qk_post(qkv, dn, gq_t, gk_t, rows, name):
    tb = min(PREP_ROWS, rows)

    def body(x_ref, dn_ref, gq_ref, gk_ref, o_ref, dgq_ref, dgk_ref):
        i = pl.program_id(0)
        qs, ks, vs = pl.ds(0, ATT_OUT), pl.ds(ATT_OUT, ATT_OUT), pl.ds(2 * ATT_OUT, ATT_OUT)
        dq, dgq = _head_rms_bwd(x_ref[:, qs], gq_ref[...], dn_ref[:, qs] * ATT_SCALE)
        dk, dgk = _head_rms_bwd(x_ref[:, ks], gk_ref[...], dn_ref[:, ks])
        o_ref[:, qs] = dq.astype(o_ref.dtype)
        o_ref[:, ks] = dk.astype(o_ref.dtype)
        o_ref[:, vs] = dn_ref[:, vs].astype(o_ref.dtype)

        @pl.when(i == 0)
        def _():
            dgq_ref[...] = dgq
            dgk_ref[...] = dgk

        @pl.when(i != 0)
        def _():
            dgq_ref[...] += dgq
            dgk_ref[...] += dgk

    gspec = pl.BlockSpec((1, ATT_OUT), lambda i: (0, 0))
    blk = pl.BlockSpec((tb, QKV_W), lambda i: (i, 0))
    return pl.pallas_call(
        body, name=name,
        grid=(rows // tb,),
        in_specs=[blk, blk, gspec, gspec],
        out_specs=[blk, gspec, gspec],
        out_shape=[jax.ShapeDtypeStruct((rows, QKV_W), MXU), jax.ShapeDtypeStruct((1, ATT_OUT), F32),
                   jax.ShapeDtypeStruct((1, ATT_OUT), F32)],
        compiler_params=_params(("arbitrary",)),
    )(qkv, dn, gq_t, gk_t)


def all_gather(arrays, name):
    n = len(arrays)

    def body(*refs):
        in_refs, out_refs = refs[:n], refs[n:2 * n]
        send_sems, recv_sems, local_sems = refs[2 * n:]
        _, me, _ = _peer(0)
        sib, _, _ = _peer(1)

        def first(t, k, arriving):
            dev, pid, _ = _peer(k)
            return _remote(in_refs[t], out_refs[t].at[pid if arriving else me], send_sems, recv_sems, t, k, dev)

        def passed(t, k, arriving):
            slot = out_refs[t].at[_peer(k + 1 if arriving else k)[1]]
            return _remote(slot, slot, send_sems, recv_sems, t, k + 1, sib)

        def own(t):
            return pltpu.make_async_copy(in_refs[t], out_refs[t].at[me], local_sems.at[t])

        for t in range(n):
            own(t).start()
            for k in (1,) + OTHER_CHIPS:
                first(t, k, False).start()
        for t in range(n):
            for k in OTHER_CHIPS:
                first(t, k, True).wait_recv()
                passed(t, k, False).start()
        for t in range(n):
            first(t, 1, True).wait_recv()
            for k in OTHER_CHIPS:
                passed(t, k, True).wait_recv()
        for t in range(n):
            for k in (1,) + OTHER_CHIPS:
                first(t, k, False).wait_send()
            for k in OTHER_CHIPS:
                passed(t, k, False).wait_send()
            own(t).wait()

    anyspec = pl.BlockSpec(memory_space=pl.ANY)
    return pl.pallas_call(
        body, name=name,
        in_specs=[anyspec] * n,
        out_specs=[anyspec] * n,
        out_shape=[jax.ShapeDtypeStruct((N_DEV,) + tuple(a.shape), a.dtype) for a in arrays],
        scratch_shapes=[pltpu.SemaphoreType.DMA((n, N_DEV)), pltpu.SemaphoreType.DMA((n, N_DEV)),
                        pltpu.SemaphoreType.DMA((n,))],
    )(*arrays)


def pair_add(a, b, name):
    _, r, c = a.shape
    rb = r if r <= 512 else (128 if c > 1024 else 256)
    assert r % rb == 0

    def body(a_ref, b_ref, o_ref):
        o_ref[...] = (a_ref[...].astype(F32) + b_ref[...].astype(F32)).astype(o_ref.dtype)

    blk = pl.BlockSpec((1, rb, c), lambda q, i: (q, i, 0))
    return pl.pallas_call(
        body, name=name,
        grid=(N_CHIPS, r // rb),
        in_specs=[blk, blk],
        out_specs=blk,
        out_shape=jax.ShapeDtypeStruct(a.shape, a.dtype),
        compiler_params=_params(("parallel", "parallel")),
    )(a, b)


def adamw(parts, w, m, v, name):
    r, c = w.shape[-2:]
    n_parts = parts.shape[0]
    rb = r if r <= 512 else (128 if c > 1024 else 256)
    assert r % rb == 0

    def body(p_ref, w_ref, m_ref, v_ref, g_out, d_out, m_out, v_out):
        g = p_ref[0].astype(F32)
        for i in range(1, n_parts):
            g = g + p_ref[i].astype(F32)
        m_new = ADAM_B1 * m_ref[...] + (1.0 - ADAM_B1) * g
        v_new = ADAM_B2 * v_ref[...] + (1.0 - ADAM_B2) * (g * g)
        m_hat = m_new / (1.0 - ADAM_B1 ** ADAM_STEP)
        v_hat = v_new / (1.0 - ADAM_B2 ** ADAM_STEP)
        g_out[...] = g
        d_out[...] = -ADAM_LR * (m_hat / (jnp.sqrt(v_hat) + ADAM_EPS) + ADAM_WD * w_ref[...])
        m_out[...] = m_new
        v_out[...] = v_new

    if w.ndim == 3:
        blk = pl.BlockSpec((None, rb, c), lambda i: (0, i, 0))
    else:
        blk = pl.BlockSpec((rb, c), lambda i: (i, 0))
    return pl.pallas_call(
        body, name=name,
        grid=(r // rb,),
        in_specs=[pl.BlockSpec((n_parts, rb, c), lambda i: (0, i, 0)), blk, blk, blk],
        out_specs=[blk] * 4,
        out_shape=[jax.ShapeDtypeStruct(w.shape, F32)] * 4,
        compiler_params=_params(("parallel",)),
    )(parts, w, m, v)


def _pad_lanes(vec, n=LANES):
    return jnp.pad(vec, ((0, 0), (0, n - vec.shape[1])))


COL_SHARDED = ("w_in", "ssd_conv_w", "w_attn_proj", "w_up", "ffn_conv_w")
MATRICES = ("w_in", "w_attn_proj", "w_up", "w_ssd_proj", "w_out", "w_down")
LATE = ("w_ssd_proj", "w_attn_proj", "w_out", "w_up", "ffn_conv_w", "w_down")


def _narrow(name, a):
    return a.astype(MXU) if name in MATRICES else a


def _from_gathered(name, g):
    if name in COL_SHARDED:
        return jnp.transpose(g, (1, 0, 2)).reshape(g.shape[1], N_DEV * g.shape[2])
    return g.reshape(N_DEV * g.shape[1], g.shape[2])


def _to_slabs(name, g):
    if name in COL_SHARDED:
        return jnp.transpose(g.reshape(g.shape[0], N_DEV, g.shape[1] // N_DEV), (1, 0, 2))
    return g.reshape(N_DEV, g.shape[0] // N_DEV, g.shape[1])


def _columns(m, a, b):
    if m.ndim == 2:
        return m[:, a:b]
    c = m.shape[2]
    cuts = [m[j][:, max(a - j * c, 0):min(b - j * c, c)] for j in range(a // c, (b - 1) // c + 1)]
    return cuts[0] if len(cuts) == 1 else jnp.concatenate(cuts, axis=1)


def _column_shards(pieces, c):
    shards = []
    for j in range(N_DEV):
        cuts = []
        for start, arr in pieces:
            lo, hi = max(j * c - start, 0), min((j + 1) * c - start, arr.shape[1])
            if lo < hi:
                cuts.append(arr[:, lo:hi])
        shards.append(cuts[0] if len(cuts) == 1 else jnp.concatenate(cuts, axis=1))
    return jnp.stack(shards)


def local_step(x, target, w, late=None):
    n_seq, seq, _ = x.shape
    rows = n_seq * seq
    x = x.reshape(rows, D_MODEL)
    target = target.reshape(rows, D_MODEL)
    mx = lambda a: a.astype(MXU)

    splits = [sum(IN_WIDTHS[:i]) for i in range(len(IN_WIDTHS) + 1)]
    w_in = w["w_in"]
    part = lambda i: _columns(w_in, splits[i], splits[i + 1])
    w_z, w_xbc, w_gs, w_ga = mx(part(0)), mx(part(1)), mx(part(6)), mx(part(7))
    w_dt = mx(_pad_lanes(part(2)))
    head_group = lambda t, g: (splits[3 + t] + g * ATT_OUT, splits[3 + t] + (g + 1) * ATT_OUT)
    w_qkv = [mx(jnp.concatenate([_columns(w_in, *head_group(t, g)) for t in range(3)], axis=1))
             for g in range(ATT_GROUPS)]
    conv_w, conv_b, fconv_b = w["ssd_conv_w"], w["ssd_conv_b"], w["ffn_conv_b"]
    dt_bias, a_log, d_skip = _pad_lanes(w["dt_bias"]), _pad_lanes(w["a_log"]), _pad_lanes(w["d_skip"])
    g1, g2, gn, gq, gk = w["norm1_g"], w["norm2_g"], w["ssd_norm_g"], w["q_norm_g"], w["k_norm_g"]

    tb = min(512, seq)
    tbm = min(256, seq)
    cw = 1024
    rw = lambda fn, name, ncol, ins, params=(), outs=(), accs=(), tb_=tb: rowwise(
        fn, name, rows, seq, tb_, ncol, ins, params, outs, accs)

    (h,) = rw(lambda ctx, xv, g: _rms_fwd(xv, g), "rms1_fwd", 1, [(x, D_MODEL, 0, None)], [(g1, None, 0)],
              [(D_MODEL, D_MODEL, 0, MXU)])
    z = matmul(h, w_z, "mm_z")
    xbc = matmul(h, w_xbc, "mm_xbc")
    dtraw = matmul(h, w_dt, "mm_dt")
    by_residue = lambda a, g: _by_residue(a, n_seq, seq, ATT_DILATIONS[g])
    by_token = lambda a, g: _by_token(a, n_seq, seq, ATT_DILATIONS[g])
    h_res = [by_residue(h, g) for g in range(ATT_GROUPS)]
    qkv = [matmul(h_res[g], w_qkv[g], f"mm_qkv{g}") for g in range(ATT_GROUPS)]
    gs = matmul(h, w_gs, "mm_gs")
    ga = matmul(h, w_ga, "mm_ga")

    def conv_silu(ctx, xh, wv, bv):
        return _silu(bv + _conv_prev(xh[0], xh[1], wv, ctx.first, SSD_CONV))

    (xact,) = rw(conv_silu, "ssd_conv_fwd", CONV_DIM // cw, [(xbc, cw, 0, "prev")],
                 [(conv_w, cw, 0), (conv_b, cw, 0)], [(CONV_DIM, cw, 0, F32)])
    if late is None:
        y, sin = ssd_fwd(xact, dtraw, dt_bias, a_log, d_skip, n_seq, seq)
    else:
        y, sin, *gathered = ssd_fwd(xact, dtraw, dt_bias, a_log, d_skip, n_seq, seq,
                                    comm=direct_exchange([(late[n], "gather") for n in LATE]))
        w = {**w, **{n: g if n == "w_up" else _from_gathered(n, g) for n, g in zip(LATE, gathered)}}
    w_sp, w_ap, w_o, w_d = mx(w["w_ssd_proj"]), mx(w["w_attn_proj"]), mx(w["w_out"]), mx(w["w_down"])
    w_ug, w_uv = mx(_columns(w["w_up"], 0, D_FF)), mx(_columns(w["w_up"], D_FF, 2 * D_FF))
    fconv_w = w["ffn_conv_w"]

    def gated_norm(ctx, yv, zv, g):
        yz = yv * _silu(zv)
        return jnp.concatenate([_rms_fwd(yz[:, i:i + NORM_GROUP], g[:, i:i + NORM_GROUP])
                                for i in range(0, cw, NORM_GROUP)], axis=1)

    (y_ssd,) = rw(gated_norm, "ssd_post_fwd", D_INNER // cw, [(y, cw, 0, None), (z, cw, 0, None)], [(gn, cw, 0)],
                  [(D_INNER, cw, 0, MXU)])

    gq_t, gk_t = jnp.tile(gq, (1, ATT_H)), jnp.tile(gk, (1, ATT_H))
    nq = [qk_prep(qkv[g], gq_t, gk_t, rows, f"qk_prep{g}") for g in range(ATT_GROUPS)]
    att = [attn_fwd(nq[g], n_seq, seq, ATT_DILATIONS[g], f"attn_fwd{g}") for g in range(ATT_GROUPS)]

    def combine(ctx, o0, o1, o2, l0, l1, l2):
        mxl = jnp.maximum(jnp.maximum(l0, l1), l2)
        e = [jnp.exp(l - mxl) for l in (l0, l1, l2)]
        inv = 1.0 / (e[0] + e[1] + e[2])
        ws = [ei * inv for ei in e]
        out = sum(_expand_heads(wi) * oi for wi, oi in zip(ws, (o0, o1, o2)))
        return (out, *ws)

    y_attn, wt0, wt1, wt2 = rw(
        combine, "attn_combine", 1,
        [(by_token(att[g][0], g), ATT_OUT, 0, None) for g in range(3)]
        + [(by_token(att[g][1], g), LANES, 0, None) for g in range(3)], [],
        [(ATT_OUT, ATT_OUT, 0, F32)] + [(LANES, LANES, 0, F32)] * 3)
    wts = (wt0, wt1, wt2)

    ps = matmul(y_ssd, w_sp, "mm_ssd_proj")
    pa = matmul(y_attn, w_ap, "mm_attn_proj")
    (merged,) = rw(lambda ctx, a, b, c, d: _sigmoid(c) * a + _sigmoid(d) * b, "merge_fwd", D_MODEL // cw,
                   [(ps, cw, 0, None), (pa, cw, 0, None), (gs, cw, 0, None), (ga, cw, 0, None)], [],
                   [(D_MODEL, cw, 0, MXU)])
    x1 = matmul(merged, w_o, "mm_out", add=x)
    (h2,) = rw(lambda ctx, xv, g: _rms_fwd(xv, g), "rms2_fwd", 1, [(x1, D_MODEL, 0, None)], [(g2, None, 0)],
               [(D_MODEL, D_MODEL, 0, MXU)])
    up_g = matmul(h2, w_ug, "mm_up_g")
    up_v = matmul(h2, w_uv, "mm_up_v")
    fw = D_FF // 2
    nfc = D_FF // fw

    def mlp_act(ctx, ug, uv, wg, wv, bg, bv):
        cg = bg + _conv_prev(ug[0], ug[1], wg, ctx.first, FFN_CONV)
        cv = bv + _conv_prev(uv[0], uv[1], wv, ctx.first, FFN_CONV)
        return _silu(cg) * cv

    (act,) = rw(mlp_act, "mlp_act_fwd", nfc, [(up_g, fw, 0, "prev"), (up_v, fw, 0, "prev")],
                [(fconv_w, fw, 0), (fconv_w, fw, nfc), (fconv_b, fw, 0), (fconv_b, fw, nfc)], [(D_FF, fw, 0, MXU)],
                tb_=tbm)
    x2 = matmul(act, w_d, "mm_down", add=x1)

    def loss_fn(ctx, xv, tv):
        d = xv - tv
        g = d * (1.0 / D_MODEL)
        return g, g, jnp.sum(d * d, axis=0, keepdims=True)

    dx2, dx2_m, sq = rw(loss_fn, "loss", 1, [(x2, D_MODEL, 0, None), (target, D_MODEL, 0, None)], [],
                        [(D_MODEL, D_MODEL, 0, F32), (D_MODEL, D_MODEL, 0, MXU)], [(1, D_MODEL)])

    grads = {}
    dact = matmul(dx2_m, w_d, "mm_d_act", tb=True)
    grads["w_down"] = matmul(act, dx2_m, "mm_dw_down", ta=True, out_dtype=MXU)

    def mlp_bwd(ctx, da, ug, uv, wg, wv, bg, bv):
        cg, cg_n = _conv_pre(ug, wg, bg, ctx.first, FFN_CONV)
        cv, cv_n = _conv_pre(uv, wv, bv, ctx.first, FFN_CONV)
        da_c, da_n = da
        dup_g_, dwg, dbg = _conv_bwd(da_c * cv * _silu_grad(cg), da_n * cv_n * _silu_grad(cg_n), ug, wg, ctx, FFN_CONV)
        dup_v_, dwv, dbv = _conv_bwd(da_c * _silu(cg), da_n * _silu(cg_n), uv, wv, ctx, FFN_CONV)
        return dup_g_, dup_v_, dwg, dbg, dwv, dbv

    dup_g, dup_v, dfw_g, dfb_g, dfw_v, dfb_v = rw(
        mlp_bwd, "mlp_bwd", nfc, [(dact, fw, 0, "next"), (up_g, fw, 0, "both"), (up_v, fw, 0, "both")],
        [(fconv_w, fw, 0), (fconv_w, fw, nfc), (fconv_b, fw, 0), (fconv_b, fw, nfc)],
        [(D_FF, fw, 0, MXU), (D_FF, fw, 0, MXU)], [(FFN_CONV, fw), (1, fw), (FFN_CONV, fw), (1, fw)], tb_=tbm)
    grads["ffn_conv_w"] = jnp.concatenate([dfw_g, dfw_v], axis=1)
    grads["ffn_conv_b"] = jnp.concatenate([dfb_g, dfb_v], axis=1)
    dh2 = matmul(dup_g, w_ug, "mm_dh2_g", tb=True)
    dh2 = matmul(dup_v, w_uv, "mm_dh2_v", tb=True, add=dh2)
    dw_up = [(0, matmul(h2, dup_g, "mm_dw_up_g", ta=True, out_dtype=MXU)),
             (D_FF, matmul(h2, dup_v, "mm_dw_up_v", ta=True, out_dtype=MXU))]
    if w["w_up"].ndim == 3:
        grads["w_up"] = _column_shards(dw_up, w["w_up"].shape[2])
    else:
        grads["w_up"] = jnp.concatenate([p for _, p in dw_up], axis=1)

    def rms_bwd_fn(ctx, xv, dh_, dres, g):
        dxv, dg = _rms_bwd(xv, g, dh_)
        return dres + dxv, dg

    def rms_bwd_fn2(ctx, xv, dh_, dres, g):
        dxv, dg = rms_bwd_fn(ctx, xv, dh_, dres, g)
        return dxv, dxv, dg

    dx1, dx1_m, grads["norm2_g"] = rw(
        rms_bwd_fn2, "rms2_bwd", 1, [(x1, D_MODEL, 0, None), (dh2, D_MODEL, 0, None), (dx2, D_MODEL, 0, None)],
        [(g2, None, 0)], [(D_MODEL, D_MODEL, 0, F32), (D_MODEL, D_MODEL, 0, MXU)], [(1, D_MODEL)])

    dmerged = matmul(dx1_m, w_o, "mm_d_merged", tb=True)
    grads["w_out"] = matmul(merged, dx1_m, "mm_dw_out", ta=True, out_dtype=MXU)

    def merge_bwd(ctx, dm, a, b, c, d):
        sc, sd = _sigmoid(c), _sigmoid(d)
        return dm * sc, dm * sd, dm * a * sc * (1.0 - sc), dm * b * sd * (1.0 - sd)

    dps, dpa, dgs, dga = rw(merge_bwd, "merge_bwd", D_MODEL // cw,
                            [(dmerged, cw, 0, None), (ps, cw, 0, None), (pa, cw, 0, None), (gs, cw, 0, None),
                             (ga, cw, 0, None)], [], [(D_MODEL, cw, 0, MXU)] * 4)
    dy_ssd = matmul(dps, w_sp, "mm_d_y_ssd", tb=True)
    grads["w_ssd_proj"] = matmul(y_ssd, dps, "mm_dw_ssd_proj", ta=True, out_dtype=MXU)
    dy_attn = matmul(dpa, w_ap, "mm_d_y_attn", tb=True)
    grads["w_attn_proj"] = matmul(y_attn, dpa, "mm_dw_attn_proj", ta=True, out_dtype=MXU)

    (rsum,) = rw(lambda ctx, a, b: _reduce_heads(a * b), "attn_rsum", 1,
                 [(dy_attn, ATT_OUT, 0, None), (y_attn, ATT_OUT, 0, None)], [], [(LANES, LANES, 0, F32)])
    dqkv, dgq, dgk = [], 0.0, 0.0
    for g in range(ATT_GROUPS):
        dn = attn_bwd(nq[g], by_residue(dy_attn, g), att[g][1], by_residue(wts[g], g), by_residue(rsum, g), n_seq, seq,
                      ATT_DILATIONS[g], f"attn_bwd{g}")
        d_, a_, b_ = qk_post(qkv[g], dn, gq_t, gk_t, rows, f"qk_post{g}")
        dqkv.append(d_)
        dgq, dgk = dgq + a_, dgk + b_
    per_head = lambda v: jnp.sum(v.reshape(ATT_H, ATT_HD), axis=0, keepdims=True)
    grads["q_norm_g"], grads["k_norm_g"] = per_head(dgq), per_head(dgk)

    def gated_norm_bwd(ctx, dyn, yv, zv, g):
        sz = _silu(zv)
        yz = yv * sz
        dyz, dgs_ = [], []
        for i in range(0, cw, NORM_GROUP):
            a, b = _rms_bwd(yz[:, i:i + NORM_GROUP], g[:, i:i + NORM_GROUP], dyn[:, i:i + NORM_GROUP])
            dyz.append(a)
            dgs_.append(b)
        dyz = jnp.concatenate(dyz, axis=1)
        return dyz * sz, dyz * yv * _silu_grad(zv), jnp.concatenate(dgs_, axis=1)

    dy, dz, grads["ssd_norm_g"] = rw(gated_norm_bwd, "ssd_post_bwd", D_INNER // cw,
                                     [(dy_ssd, cw, 0, None), (y, cw, 0, None), (z, cw, 0, None)], [(gn, cw, 0)],
                                     [(D_INNER, cw, 0, F32), (D_INNER, cw, 0, MXU)], [(1, cw)])
    if late is None:
        dxact, ddt, dbias, dalog, ddskip = ssd_bwd(xact, dtraw, dt_bias, a_log, d_skip, sin, dy, n_seq, seq)
    else:
        dxact, ddt, dbias, dalog, ddskip, *parts = ssd_bwd(
            xact, dtraw, dt_bias, a_log, d_skip, sin, dy, n_seq, seq,
            comm=direct_exchange([(grads[n] if n == "w_up" else _to_slabs(n, _narrow(n, grads[n])), "scatter")
                                  for n in LATE]))
        grads.update(zip(LATE, parts))
    grads["dt_bias"], grads["a_log"], grads["d_skip"] = dbias[:, :SSD_H], dalog[:, :SSD_H], ddskip[:, :SSD_H]

    def conv_silu_bwd(ctx, dxa, xin, wv, bv):
        pre, pre_n = _conv_pre(xin, wv, bv, ctx.first, SSD_CONV)
        return _conv_bwd(dxa[0] * _silu_grad(pre), dxa[1] * _silu_grad(pre_n), xin, wv, ctx, SSD_CONV)

    dxbc, grads["ssd_conv_w"], grads["ssd_conv_b"] = rw(
        conv_silu_bwd, "ssd_conv_bwd", CONV_DIM // cw, [(dxact, cw, 0, "next"), (xbc, cw, 0, "both")],
        [(conv_w, cw, 0), (conv_b, cw, 0)], [(CONV_DIM, cw, 0, MXU)], [(SSD_CONV, cw), (1, cw)])

    pieces = [(d_, d_, h, w_, tag) for d_, w_, tag in
              ((dz, w_z, "z"), (dxbc, w_xbc, "xbc"), (ddt, w_dt, "dt"), (dgs, w_gs, "gs"), (dga, w_ga, "ga"))]
    pieces += [(by_token(dqkv[g], g), dqkv[g], h_res[g], w_qkv[g], f"qkv{g}") for g in range(ATT_GROUPS)]
    dws = {tag: matmul(h_in, dpart_h, f"mm_dw_{tag}", ta=True, out_dtype=MXU) for _, dpart_h, h_in, _, tag in pieces}
    dw_in = [(splits[0], dws["z"]), (splits[1], dws["xbc"]), (splits[2], dws["dt"][:, :SSD_H])]
    dw_in += [(head_group(t, g)[0], dws[f"qkv{g}"][:, t * ATT_OUT:(t + 1) * ATT_OUT])
              for t in range(3) for g in range(ATT_GROUPS)]
    dw_in += [(splits[6], dws["gs"]), (splits[7], dws["ga"])]
    if w_in.ndim == 3:
        grads["w_in"] = _column_shards(dw_in, w_in.shape[2])
    else:
        grads["w_in"] = jnp.concatenate([p for _, p in dw_in], axis=1)
    dh = None
    for idx, (dpart, _, _, wpart, tag) in enumerate(pieces):
        comm = NO_EXCHANGE
        if late is not None and idx == 0:
            slabs = [grads[n] if n == "w_in" else _to_slabs(n, _narrow(n, grads[n])) for n in EARLY]
            comm = sibling_exchange(slabs)
        if late is not None and idx == 1:
            core = lax.axis_index("c")
            own = [lax.dynamic_index_in_dim(s.reshape((N_CHIPS, 2) + s.shape[1:]), core, axis=1, keepdims=False)
                   for s in slabs]
            comm = chip_exchange([pair_add(a_, b_, f"rs_add_{n}") for n, a_, b_ in zip(EARLY, own, arrived)])
        dh = matmul(dpart, wpart, f"mm_dh_{tag}", tb=True, add=dh, comm=comm)
        if comm.n:
            dh, *arrived = dh
            if idx == 1:
                grads.update(zip(EARLY, arrived))
    grad_x, grads["norm1_g"] = rw(rms_bwd_fn, "rms1_bwd", 1,
                                  [(x, D_MODEL, 0, None), (dh, D_MODEL, 0, None), (dx1, D_MODEL, 0, None)],
                                  [(g1, None, 0)], [(D_MODEL, D_MODEL, 0, F32)], [(1, D_MODEL)])
    return sq, grad_x.reshape(n_seq, seq, D_MODEL), grads


EARLY = ("w_in", "ssd_conv_w")
REPLICATED = ("norm1_g", "ssd_conv_b", "dt_bias", "a_log", "d_skip", "ssd_norm_g", "q_norm_g", "k_norm_g",
              "norm2_g", "ffn_conv_b")
WEIGHTS = ("norm1_g", "w_in", "ssd_conv_w", "ssd_conv_b", "dt_bias", "a_log", "d_skip", "ssd_norm_g", "w_ssd_proj",
           "q_norm_g", "k_norm_g", "w_attn_proj", "w_out", "norm2_g", "w_up", "ffn_conv_w", "ffn_conv_b", "w_down")
PACK_ROWS, PACK_COLS = 8, 2048


def _pack(vals):
    flat = jnp.concatenate([vals[n].reshape(-1) for n in REPLICATED])
    return jnp.pad(flat, (0, PACK_ROWS * PACK_COLS - flat.shape[0])).reshape(PACK_ROWS, PACK_COLS)


def _unpack(packed, like):
    flat = packed.reshape(-1)
    out, pos = {}, 0
    for n in REPLICATED:
        size = like[n].size
        out[n] = flat[pos:pos + size].reshape(like[n].shape)
        pos += size
    return out


def step(x, target, w_raw, m_raw, v_raw):
    wsh = {n: a[0] if a.ndim == 3 else a for n, a in w_raw.items()}
    gathered = all_gather([_narrow(n, wsh[n]) for n in EARLY], "ag_weights")
    full = {n: wsh[n] for n in REPLICATED}
    full.update({n: g if n == "w_in" else _from_gathered(n, g) for n, g in zip(EARLY, gathered)})

    sq, grad_x, grads = local_step(x, target, full, late={n: _narrow(n, wsh[n]) for n in LATE})

    (small,) = all_gather([_pack({n: grads[n] for n in REPLICATED})], "ag_small")

    out_g, out_d, out_m, out_v = {}, {}, {}, {}
    for n in EARLY + LATE:
        out_g[n], out_d[n], out_m[n], out_v[n] = adamw(grads[n], w_raw[n], m_raw[n], v_raw[n], f"adamw_{n}")
    pk = adamw(small, _pack(w_raw), _pack(m_raw), _pack(v_raw), "adamw_small")
    for dst, packed in zip((out_g, out_d, out_m, out_v), pk):
        dst.update(_unpack(packed, w_raw))
    loss = lax.psum(0.5 * jnp.sum(sq) / D_MODEL, ("x", "y", "c"))
    return loss, grad_x, out_g, out_d, out_m, out_v


def kernel(x, norm1_g, w_in, ssd_conv_w, ssd_conv_b, dt_bias, a_log, d_skip, ssd_norm_g, w_ssd_proj, q_norm_g, k_norm_g, w_attn_proj, w_out, norm2_g, w_up, ffn_conv_w, ffn_conv_b, w_down, loss_target, m_norm1_g, m_w_in, m_ssd_conv_w, m_ssd_conv_b, m_dt_bias, m_a_log, m_d_skip, m_ssd_norm_g, m_w_ssd_proj, m_q_norm_g, m_k_norm_g, m_w_attn_proj, m_w_out, m_norm2_g, m_w_up, m_ffn_conv_w, m_ffn_conv_b, m_w_down, v_norm1_g, v_w_in, v_ssd_conv_w, v_ssd_conv_b, v_dt_bias, v_a_log, v_d_skip, v_ssd_norm_g, v_w_ssd_proj, v_q_norm_g, v_k_norm_g, v_w_attn_proj, v_w_out, v_norm2_g, v_w_up, v_ffn_conv_w, v_ffn_conv_b, v_w_down):
    ws = (norm1_g, w_in, ssd_conv_w, ssd_conv_b, dt_bias, a_log, d_skip, ssd_norm_g, w_ssd_proj, q_norm_g, k_norm_g,
          w_attn_proj, w_out, norm2_g, w_up, ffn_conv_w, ffn_conv_b, w_down)
    ms = (m_norm1_g, m_w_in, m_ssd_conv_w, m_ssd_conv_b, m_dt_bias, m_a_log, m_d_skip, m_ssd_norm_g, m_w_ssd_proj,
          m_q_norm_g, m_k_norm_g, m_w_attn_proj, m_w_out, m_norm2_g, m_w_up, m_ffn_conv_w, m_ffn_conv_b, m_w_down)
    vs = (v_norm1_g, v_w_in, v_ssd_conv_w, v_ssd_conv_b, v_dt_bias, v_a_log, v_d_skip, v_ssd_norm_g, v_w_ssd_proj,
          v_q_norm_g, v_k_norm_g, v_w_attn_proj, v_w_out, v_norm2_g, v_w_up, v_ffn_conv_w, v_ffn_conv_b, v_w_down)
    loss, grad_x, g, d, m, v = step(x, loss_target, dict(zip(WEIGHTS, ws)), dict(zip(WEIGHTS, ms)), dict(zip(WEIGHTS, vs)))
    ordered = lambda dct: [dct[n] for n in WEIGHTS]
    return (loss, grad_x, *ordered(g), *ordered(d), *ordered(m), *ordered(v))
```

```python
import functools

import jax
import jax.numpy as jnp
from jax import lax
from jax.experimental import pallas as pl
from jax.experimental.pallas import tpu as pltpu

F32 = jnp.float32
BF16 = jnp.bfloat16
MXU = jnp.bfloat16
HIGHEST = lax.Precision.HIGHEST
VMEM_LIMIT_BYTES = 48 * 1024 * 1024
SUBLANES = 8
LANES = 128
N_DEV = 8

D_MODEL = 1024
D_INNER = 2048
SSD_P = 64
SSD_H = 32
SSD_G = 8
SSD_K = SSD_H // SSD_G
SSD_N = 128
SSD_Q = 128
SSD_CONV = 4
CONV_DIM = D_INNER + 2 * SSD_G * SSD_N
NORM_GROUP = D_INNER // SSD_G
ATT_GROUPS = 3
ATT_H = 8
ATT_HD = 64
ATT_BLK = 128
ATT_OUT = ATT_H * ATT_HD
ATT_DILATIONS = (1, 4, 16)
ATT_SCALE = ATT_HD ** -0.5
D_FF = 2816
FFN_CONV = 3
EPS = 1e-6
NEG = -1e30
IN_WIDTHS = (D_INNER, CONV_DIM, SSD_H, 3 * ATT_OUT, 3 * ATT_OUT, 3 * ATT_OUT, D_MODEL, D_MODEL)

ADAM_LR = 0.001
ADAM_B1 = 0.9
ADAM_B2 = 0.999
ADAM_EPS = 1e-08
ADAM_WD = 0.01
ADAM_STEP = 10


def _mm(a, b, dims):
    return lax.dot_general(a.astype(MXU), b.astype(MXU), (dims, ((), ())), preferred_element_type=F32)


def _dot_nn(a, b):
    return _mm(a, b, ((1,), (0,)))


def _dot_nt(a, b):
    return _mm(a, b, ((1,), (1,)))


def _dot_tn(a, b):
    return _mm(a, b, ((0,), (0,)))


def _dot_f32(a, b):
    return lax.dot_general(a, b, (((1,), (0,)), ((), ())), precision=HIGHEST, preferred_element_type=F32)


def _sigmoid(x):
    return 1.0 / (1.0 + jnp.exp(-x))


def _silu(x):
    return x * _sigmoid(x)


def _silu_grad(x):
    s = _sigmoid(x)
    return s * (1.0 + x * (1.0 - s))


def _softplus(x):
    return jnp.maximum(x, 0.0) + jnp.log(1.0 + jnp.exp(-jnp.abs(x)))


def _rms_fwd(x, g):
    r = lax.rsqrt(jnp.mean(x * x, axis=-1, keepdims=True) + EPS)
    return x * r * g


def _rms_bwd(x, g, dy):
    r = lax.rsqrt(jnp.mean(x * x, axis=-1, keepdims=True) + EPS)
    xh = x * r
    dyg = dy * g
    dx = r * (dyg - xh * jnp.mean(dyg * xh, axis=-1, keepdims=True))
    return dx, jnp.sum(dy * xh, axis=0, keepdims=True)


def _onehot_row(h, n=LANES):
    return (lax.broadcasted_iota(jnp.int32, (1, n), 1) == h).astype(F32)


def _onehot_col(h, n=LANES):
    return (lax.broadcasted_iota(jnp.int32, (n, 1), 0) == h).astype(F32)


def _head_expand_matrix():
    r = lax.broadcasted_iota(jnp.int32, (LANES, ATT_OUT), 0)
    c = lax.broadcasted_iota(jnp.int32, (LANES, ATT_OUT), 1)
    return (c // ATT_HD == r).astype(F32)


def _split_bf16(x, parts):
    out = []
    for _ in range(parts - 1):
        hi = x.astype(BF16).astype(F32)
        out.append(hi)
        x = x - hi
    out.append(x)
    return out


def _expand_heads(w):
    e = _head_expand_matrix()
    return sum(_dot_nn(p, e) for p in _split_bf16(w, 2))


def _reduce_heads(x):
    e = _head_expand_matrix()
    return sum(_dot_nt(p, e) for p in _split_bf16(x, 3))


def _shift_prev(cur, halo, s, first):
    if s == 0:
        return cur
    rolled = pltpu.roll(cur, s, 0)
    hr = jnp.where(first, 0.0, pltpu.roll(halo, s, 0))
    rows = lax.broadcasted_iota(jnp.int32, halo.shape, 0)
    head = jnp.where(rows < s, hr, rolled[:SUBLANES])
    if cur.shape[0] == SUBLANES:
        return head
    return jnp.concatenate([head, rolled[SUBLANES:]], axis=0)


def _shift_next(cur, halo, s, last):
    if s == 0:
        return cur
    tb = cur.shape[0]
    rolled = pltpu.roll(cur, tb - s, 0)
    hr = jnp.where(last, 0.0, pltpu.roll(halo, SUBLANES - s, 0))
    rows = lax.broadcasted_iota(jnp.int32, halo.shape, 0)
    tail = jnp.where(rows >= SUBLANES - s, hr, rolled[tb - SUBLANES:])
    return jnp.concatenate([rolled[:tb - SUBLANES], tail], axis=0)


def _conv_prev(x, halo, w, first, taps):
    acc = None
    for i in range(taps):
        term = w[i:i + 1, :] * _shift_prev(x, halo, taps - 1 - i, first)
        acc = term if acc is None else acc + term
    return acc


def _conv_pre(x, w, b, first, taps):
    cur, prev8, next8 = x
    tail = cur[cur.shape[0] - SUBLANES:]
    return b + _conv_prev(cur, prev8, w, first, taps), b + _conv_prev(next8, tail, w, False, taps)


def _conv_bwd(dpre, dpre_next8, x, w, ctx, taps):
    cur, prev8, _ = x
    dx, dws = None, []
    for i in range(taps):
        term = w[i:i + 1, :] * _shift_next(dpre, dpre_next8, taps - 1 - i, ctx.last)
        dx = term if dx is None else dx + term
        dws.append(jnp.sum(dpre * _shift_prev(cur, prev8, taps - 1 - i, ctx.first), axis=0, keepdims=True))
    return dx, jnp.concatenate(dws, axis=0), jnp.sum(dpre, axis=0, keepdims=True)


def _params(sem):
    return pltpu.CompilerParams(dimension_semantics=sem, vmem_limit_bytes=VMEM_LIMIT_BYTES)


N_CHIPS = N_DEV // 2
OTHER_CHIPS = (4, 2, 6)


class _Hosted:
    def __init__(self, arrays, out_shape, sems, ops):
        self.arrays, self.out_shape, self.sems, self.ops = list(arrays), list(out_shape), list(sems), ops
        self.n = len(self.arrays)
        self.specs = [pl.BlockSpec(memory_space=pl.ANY)] * self.n

    def begin(self, in_refs, out_refs, sem_refs, first):
        start, finish = self.ops(in_refs, out_refs, *sem_refs)
        pl.when(first)(start)
        return finish


NO_EXCHANGE = _Hosted((), (), (), None)


def _peer(k):
    x, y, c = lax.axis_index("x"), lax.axis_index("y"), lax.axis_index("c")
    px = 1 - x if k & 4 else x
    py = 1 - y if k & 2 else y
    pc = 1 - c if k & 1 else c
    return (px, py, pc), 4 * px + 2 * py + pc, 2 * px + py


def _remote(src, dst, send_sems, recv_sems, t, k, dev):
    return pltpu.make_async_remote_copy(src_ref=src, dst_ref=dst, send_sem=send_sems.at[t, k], recv_sem=recv_sems.at[t, k],
                                        device_id=dev, device_id_type=pl.DeviceIdType.MESH)


def direct_exchange(items):
    n = len(items)

    def ops(in_refs, out_refs, send_sems, recv_sems, local_sems):
        _, me, _ = _peer(0)
        part = lambda t, pid: in_refs[t] if items[t][1] == "gather" else in_refs[t].at[pid]

        def copy(t, k, arriving):
            dev, pid, _ = _peer(k)
            return _remote(part(t, pid), out_refs[t].at[pid if arriving else me], send_sems, recv_sems, t, k, dev)

        def own(t):
            return pltpu.make_async_copy(part(t, me), out_refs[t].at[me], local_sems.at[t])

        def start():
            for t in range(n):
                own(t).start()
                for k in range(1, N_DEV):
                    copy(t, k, False).start()

        def finish():
            for t in range(n):
                for k in range(1, N_DEV):
                    copy(t, k, True).wait_recv()
            for t in range(n):
                for k in range(1, N_DEV):
                    copy(t, k, False).wait_send()
                own(t).wait()

        return start, finish

    out_shape = [jax.ShapeDtypeStruct((N_DEV,) + tuple(a.shape if m == "gather" else a.shape[1:]), a.dtype)
                 for a, m in items]
    sems = [pltpu.SemaphoreType.DMA((n, N_DEV)), pltpu.SemaphoreType.DMA((n, N_DEV)), pltpu.SemaphoreType.DMA((n,))]
    return _Hosted([a for a, _ in items], out_shape, sems, ops)


def sibling_exchange(arrays):
    n = len(arrays)

    def ops(in_refs, out_refs, send_sems, recv_sems):
        sib, _, _ = _peer(1)
        c = lax.axis_index("c")
        copy = lambda t, q: _remote(in_refs[t].at[2 * q + (1 - c)], out_refs[t].at[q], send_sems, recv_sems, t, q, sib)

        def start():
            for t in range(n):
                for q in range(N_CHIPS):
                    copy(t, q).start()

        def finish():
            for t in range(n):
                for q in range(N_CHIPS):
                    copy(t, q).wait_recv()
            for t in range(n):
                for q in range(N_CHIPS):
                    copy(t, q).wait_send()

        return start, finish

    out_shape = [jax.ShapeDtypeStruct((N_CHIPS,) + a.shape[1:], a.dtype) for a in arrays]
    sems = [pltpu.SemaphoreType.DMA((n, N_CHIPS)), pltpu.SemaphoreType.DMA((n, N_CHIPS))]
    return _Hosted(arrays, out_shape, sems, ops)


def chip_exchange(arrays):
    n = len(arrays)

    def ops(in_refs, out_refs, send_sems, recv_sems, local_sems):
        _, _, mine = _peer(0)

        def copy(t, k, arriving):
            dev, _, q = _peer(k)
            return _remote(in_refs[t].at[q], out_refs[t].at[q if arriving else mine], send_sems, recv_sems, t, k, dev)

        def own(t):
            return pltpu.make_async_copy(in_refs[t].at[mine], out_refs[t].at[mine], local_sems.at[t])

        def start():
            for t in range(n):
                own(t).start()
                for k in OTHER_CHIPS:
                    copy(t, k, False).start()

        def finish():
            for t in range(n):
                for k in OTHER_CHIPS:
                    copy(t, k, True).wait_recv()
            for t in range(n):
                for k in OTHER_CHIPS:
                    copy(t, k, False).wait_send()
                own(t).wait()

        return start, finish

    out_shape = [jax.ShapeDtypeStruct(a.shape, a.dtype) for a in arrays]
    sems = [pltpu.SemaphoreType.DMA((n, N_DEV)), pltpu.SemaphoreType.DMA((n, N_DEV)), pltpu.SemaphoreType.DMA((n,))]
    return _Hosted(arrays, out_shape, sems, ops)


MATMUL_VMEM_BUDGET = 34 * 1024 * 1024


V7X_MXU_FLOPS = 996e12
V7X_HBM_BYTES_PER_S = 3.4e12
GRID_STEP_S = 0.35e-6


def _tile_sizes(dim, cap):
    return [t for t in range(LANES, min(dim, cap) + 1, LANES) if dim % t == 0] or [dim]


def _matmul_tiles(m, n, k, a_bytes, b_bytes, add_bytes, out_bytes):
    best = None
    for tk in _tile_sizes(k, 8192):
        nk = k // tk
        for tn in _tile_sizes(n, 2048):
            for tm in _tile_sizes(m, 2048):
                io = tm * tk * a_bytes + tk * tn * b_bytes
                ends = tm * tn * (add_bytes + out_bytes)
                need = 2 * (io + ends) + tm * tn * 4 * (2 if nk > 1 else 1)
                if need > MATMUL_VMEM_BUDGET:
                    continue
                step = max(2.0 * tm * tn * tk / V7X_MXU_FLOPS, (io + ends / nk) / V7X_HBM_BYTES_PER_S)
                if nk > 1:
                    step += tm * tn * 8 / V7X_HBM_BYTES_PER_S
                cost = (m // tm) * (n // tn) * nk * (step + GRID_STEP_S)
                if best is None or cost < best[0]:
                    best = (cost, tm, tn, tk)
    if best is None:
        raise ValueError((m, n, k))
    return best[1:]


def matmul(a, b, name, ta=False, tb=False, add=None, out_dtype=F32, comm=NO_EXCHANGE):
    assert not (ta and tb)
    m, k = (a.shape[1], a.shape[0]) if ta else a.shape
    n = b.shape[0] if tb else b.shape[1]
    assert (b.shape[1] if tb else b.shape[0]) == k
    tm, tn, tk = _matmul_tiles(m, n, k, a.dtype.itemsize, b.dtype.itemsize,
                               0 if add is None else add.dtype.itemsize, jnp.dtype(out_dtype).itemsize)
    nk = k // tk
    grid = (m // tm, n // tn, nk)
    dims = ((0,), (0,)) if ta else (((1,), (1,)) if tb else ((1,), (0,)))
    n_in = 2 if add is None else 3
    n_acc = 0 if nk == 1 else 1

    def body(*refs):
        a_ref, b_ref = refs[:2]
        o_ref = refs[n_in + comm.n]
        ids = [pl.program_id(d) for d in range(3)]
        if comm.n:
            first = functools.reduce(jnp.logical_and, [i == 0 for i in ids])
            last = functools.reduce(jnp.logical_and, [i == g - 1 for i, g in zip(ids, grid)])
            done = comm.begin(refs[n_in:n_in + comm.n], refs[n_in + comm.n + 1:n_in + 2 * comm.n + 1],
                              refs[n_in + 2 * comm.n + 1 + n_acc:], first)

        def finish(r):
            if add is not None:
                r = r + refs[2][...].astype(F32)
            o_ref[...] = r.astype(out_dtype)

        if nk == 1:
            finish(_mm(a_ref[...], b_ref[...], dims))
        else:
            acc = refs[n_in + 2 * comm.n + 1]

            @pl.when(ids[2] == 0)
            def _():
                acc[...] = jnp.zeros_like(acc)

            acc[...] += _mm(a_ref[...], b_ref[...], dims)

            @pl.when(ids[2] == nk - 1)
            def _():
                finish(acc[...])

        if comm.n:
            pl.when(last)(done)

    a_spec = pl.BlockSpec((tk, tm), lambda i, j, kk: (kk, i)) if ta else pl.BlockSpec((tm, tk), lambda i, j, kk: (i, kk))
    b_spec = pl.BlockSpec((tn, tk), lambda i, j, kk: (j, kk)) if tb else pl.BlockSpec((tk, tn), lambda i, j, kk: (kk, j))
    in_specs = [a_spec, b_spec]
    args = [a, b]
    if add is not None:
        in_specs.append(pl.BlockSpec((tm, tn), lambda i, j, kk: (i, j)))
        args.append(add)
    res = pl.pallas_call(
        body, name=name,
        grid=grid,
        in_specs=in_specs + comm.specs,
        out_specs=[pl.BlockSpec((tm, tn), lambda i, j, kk: (i, j))] + comm.specs,
        out_shape=[jax.ShapeDtypeStruct((m, n), out_dtype)] + comm.out_shape,
        scratch_shapes=([] if nk == 1 else [pltpu.VMEM((tm, tn), F32)]) + comm.sems,
        compiler_params=_params(("arbitrary",) * 3 if comm.n else ("parallel", "parallel", "arbitrary")),
    )(*args, *comm.arrays)
    return res if comm.n else res[0]


class _Ctx:
    def __init__(self, first, last):
        self.first = first
        self.last = last


def rowwise(fn, name, rows, seq, tb, ncol, ins, params=(), outs=(), accs=()):
    assert rows % tb == 0 and seq % tb == 0 and tb % 16 == 0
    bps = seq // tb
    nrow = rows // tb
    r8 = tb // SUBLANES
    args, in_specs = [], []
    for arr, w, off, halo in ins:
        args.append(arr)
        in_specs.append(pl.BlockSpec((tb, w), lambda j, i, off=off: (i, off + j)))
        if halo in ("prev", "both"):
            args.append(arr)
            in_specs.append(pl.BlockSpec((SUBLANES, w), lambda j, i, off=off: (jnp.maximum(i * r8 - 1, 0), off + j)))
        if halo in ("next", "both"):
            args.append(arr)
            in_specs.append(pl.BlockSpec(
                (SUBLANES, w), lambda j, i, off=off: (jnp.minimum((i + 1) * r8, rows // SUBLANES - 1), off + j)))
    for arr, w, off in params:
        args.append(arr)
        if w is None:
            in_specs.append(pl.BlockSpec(arr.shape, lambda j, i: (0, 0)))
        else:
            in_specs.append(pl.BlockSpec((arr.shape[0], w), lambda j, i, off=off: (0, off + j)))
    out_shape, out_specs = [], []
    for total, w, off, dt in outs:
        out_shape.append(jax.ShapeDtypeStruct((rows, total), dt))
        out_specs.append(pl.BlockSpec((tb, w), lambda j, i, off=off: (i, off + j)))
    for r, w in accs:
        out_shape.append(jax.ShapeDtypeStruct((r, ncol * w), F32))
        out_specs.append(pl.BlockSpec((r, w), lambda j, i: (0, j)))
    n_out, n_acc = len(outs), len(accs)

    def body(*refs):
        i = pl.program_id(1)
        pos = 0
        vals = []
        for _, _, _, halo in ins:
            cur = refs[pos][...]
            pos += 1
            if halo is None:
                vals.append(cur)
            elif halo == "both":
                vals.append((cur, refs[pos][...], refs[pos + 1][...]))
                pos += 2
            else:
                vals.append((cur, refs[pos][...]))
                pos += 1
        for _ in params:
            vals.append(refs[pos][...])
            pos += 1
        ctx = _Ctx(i % bps == 0, i % bps == bps - 1)
        res = fn(ctx, *vals)
        if not isinstance(res, (tuple, list)):
            res = (res,)
        assert len(res) == n_out + n_acc
        for q in range(n_out):
            refs[pos + q][...] = res[q].astype(refs[pos + q].dtype)
        for q in range(n_acc):
            ref, val = refs[pos + n_out + q], res[n_out + q]

            @pl.when(i == 0)
            def _(ref=ref, val=val):
                ref[...] = val

            @pl.when(i != 0)
            def _(ref=ref, val=val):
                ref[...] += val

    res = pl.pallas_call(
        body, name=name,
        grid=(ncol, nrow),
        in_specs=in_specs,
        out_specs=out_specs,
        out_shape=out_shape,
        compiler_params=_params(("parallel", "arbitrary")),
    )(*args)
    return res


GROUP_W = SSD_K * SSD_P


def _tri(lower):
    r = lax.broadcasted_iota(jnp.int32, (SSD_Q, SSD_Q), 0)
    c = lax.broadcasted_iota(jnp.int32, (SSD_Q, SSD_Q), 1)
    return r >= c if lower else r <= c


def _first_head_lanes():
    return lax.broadcasted_iota(jnp.int32, (1, LANES), 1) < SSD_P


def _column(v, j):
    return v[:, j * LANES:(j + 1) * LANES] if v.shape[-1] == GROUP_W else v


def _per_head(vals):
    first = _first_head_lanes()
    return jnp.concatenate([jnp.where(first, _column(vals[2 * j], j), _column(vals[2 * j + 1], j))
                            for j in range(GROUP_W // LANES)], axis=1)


def _per_head_rows(vals):
    return jnp.concatenate([jnp.broadcast_to(v, (SSD_P, 1)) for v in vals], axis=0)


def _own_columns(slab, k):
    keep = _first_head_lanes() if k % 2 == 0 else jnp.logical_not(_first_head_lanes())
    own = jnp.where(keep, _column(slab, k // 2), 0.0)
    return jnp.concatenate([own, jnp.zeros_like(own)] if k < 2 else [jnp.zeros_like(own), own], axis=1)


def _headsum(prod, g):
    out = None
    for k in range(SSD_K):
        keep = _first_head_lanes() if k % 2 == 0 else jnp.logical_not(_first_head_lanes())
        term = jnp.sum(jnp.where(keep, _column(prod, k // 2), 0.0), axis=1, keepdims=True) * _onehot_row(g * SSD_K + k)
        out = term if out is None else out + term
    return out


def ssd_fwd(xact, dtraw, dt_bias, a_log, d_skip, n_seq, seq, comm=NO_EXCHANGE):
    nc = seq // SSD_Q
    rows = n_seq * seq
    nx = comm.n

    def body(*refs):
        xact_ref, dtraw_ref, bias_ref, alog_ref, dskip_ref = refs[:5]
        y_ref, sin_ref = refs[5 + nx:7 + nx]
        state, cs_s, cst_s, dt_s = refs[7 + 2 * nx:11 + 2 * nx]
        b, c = pl.program_id(0), pl.program_id(1)
        if nx:
            finish = comm.begin(refs[5:5 + nx], refs[7 + nx:7 + 2 * nx], refs[11 + 2 * nx:],
                                jnp.logical_and(b == 0, c == 0))

        @pl.when(c == 0)
        def _():
            state[...] = jnp.zeros_like(state)

        sin_ref[0] = state[...]
        dt = _softplus(dtraw_ref[...] + bias_ref[...])
        a = dt * (-jnp.exp(alog_ref[...]))
        cs = _dot_f32(_tri(True).astype(F32), a)
        cs_s[...] = cs
        cst_s[...] = cs.T
        dt_s[...] = dt
        causal = _tri(True)
        def front(g):
            heads = [g * SSD_K + k for k in range(SSD_K)]
            bg = xact_ref[:, pl.ds(D_INNER + g * SSD_N, SSD_N)]
            cg = xact_ref[:, pl.ds(D_INNER + (SSD_G + g) * SSD_N, SSD_N)]
            xg = xact_ref[:, pl.ds(g * GROUP_W, GROUP_W)]
            cols = [cs_s[:, pl.ds(h, 1)] for h in heads]
            lasts = [cs_s[pl.ds(SSD_Q - 1, 1), pl.ds(h, 1)] for h in heads]
            xdg = xg * _per_head([dt_s[:, pl.ds(h, 1)] for h in heads])
            sg = state[g]
            y = (_per_head([jnp.exp(c_) for c_ in cols]) * _dot_nt(cg, sg)
                 + _per_head([dskip_ref[:, pl.ds(h, 1)] for h in heads]) * xg)
            w = _per_head([jnp.exp(l_ - c_) for l_, c_ in zip(lasts, cols)])
            state[g] = _per_head_rows([jnp.exp(l_) for l_ in lasts]) * sg + _dot_tn(w * xdg, bg)
            return heads, cols, _dot_nt(cg, bg), xdg, y

        def back(g, heads, cols, gm, xdg, y):
            mats = [gm * jnp.exp(jnp.where(causal, cols[k] - cst_s[pl.ds(h, 1), :], NEG)) for k, h in enumerate(heads)]
            y4 = _dot_nn(jnp.concatenate(mats, axis=0), xdg)
            y_ref[:, pl.ds(g * GROUP_W, GROUP_W)] = y + _per_head([y4[k * SSD_Q:(k + 1) * SSD_Q] for k in range(SSD_K)])

        ahead = front(0)
        for g in range(SSD_G):
            cur, ahead = ahead, (front(g + 1) if g + 1 < SSD_G else None)
            back(g, *cur)
        if nx:
            pl.when(jnp.logical_and(b == n_seq - 1, c == nc - 1))(finish)

    vec = pl.BlockSpec((1, LANES), lambda b, c: (0, 0))
    return pl.pallas_call(
        body, name="ssd_fwd",
        grid=(n_seq, nc),
        in_specs=[pl.BlockSpec((SSD_Q, CONV_DIM), lambda b, c: (b * nc + c, 0)),
                  pl.BlockSpec((SSD_Q, LANES), lambda b, c: (b * nc + c, 0)), vec, vec, vec] + comm.specs,
        out_specs=[pl.BlockSpec((SSD_Q, D_INNER), lambda b, c: (b * nc + c, 0)),
                   pl.BlockSpec((1, SSD_G, GROUP_W, SSD_N), lambda b, c: (b * nc + c, 0, 0, 0))] + comm.specs,
        out_shape=[jax.ShapeDtypeStruct((rows, D_INNER), F32),
                   jax.ShapeDtypeStruct((n_seq * nc, SSD_G, GROUP_W, SSD_N), F32)] + comm.out_shape,
        scratch_shapes=[pltpu.VMEM((SSD_G, GROUP_W, SSD_N), F32), pltpu.VMEM((SSD_Q, LANES), F32),
                        pltpu.VMEM((LANES, SSD_Q), F32), pltpu.VMEM((SSD_Q, LANES), F32)] + comm.sems,
        compiler_params=_params(("arbitrary", "arbitrary")),
    )(xact, dtraw, dt_bias, a_log, d_skip, *comm.arrays)


def ssd_bwd(xact, dtraw, dt_bias, a_log, d_skip, sin, dy, n_seq, seq, comm=NO_EXCHANGE):
    nc = seq // SSD_Q
    rows = n_seq * seq
    nx = comm.n

    def body(*refs):
        xact_ref, dtraw_ref, bias_ref, alog_ref, dskip_ref, sin_ref, dy_ref = refs[:7]
        dx_ref, ddt_ref, dbias_ref, dalog_ref, ddskip_ref = refs[7 + nx:12 + nx]
        dstate, cs_s, cst_s, dt_s = refs[12 + 2 * nx:16 + 2 * nx]
        b, c = pl.program_id(0), pl.program_id(1)
        if nx:
            finish = comm.begin(refs[7:7 + nx], refs[12 + nx:12 + 2 * nx], refs[16 + 2 * nx:],
                                jnp.logical_and(b == 0, c == 0))

        @pl.when(c == 0)
        def _():
            dstate[...] = jnp.zeros_like(dstate)

        pre = dtraw_ref[...] + bias_ref[...]
        dt = _softplus(pre)
        a_neg = -jnp.exp(alog_ref[...])
        cs = _dot_f32(_tri(True).astype(F32), dt * a_neg)
        cs_s[...] = cs
        cst_s[...] = cs.T
        dt_s[...] = dt
        causal, anti = _tri(True), _tri(False)
        is_last_row = lax.broadcasted_iota(jnp.int32, (SSD_Q, 1), 0) == SSD_Q - 1
        dcs_cf = jnp.zeros((SSD_Q, LANES), F32)
        dcs_rf = jnp.zeros((LANES, SSD_Q), F32)
        ddt_cf = jnp.zeros((SSD_Q, LANES), F32)
        dd_vec = jnp.zeros((1, LANES), F32)
        dlast_vec = jnp.zeros((1, LANES), F32)
        def front(g):
            heads = [g * SSD_K + k for k in range(SSD_K)]
            v = {"heads": heads}
            bg = v["bg"] = xact_ref[:, pl.ds(D_INNER + g * SSD_N, SSD_N)]
            cg = v["cg"] = xact_ref[:, pl.ds(D_INNER + (SSD_G + g) * SSD_N, SSD_N)]
            xg = v["xg"] = xact_ref[:, pl.ds(g * GROUP_W, GROUP_W)]
            dyg = v["dyg"] = dy_ref[:, pl.ds(g * GROUP_W, GROUP_W)]
            cols = [cs_s[:, pl.ds(h, 1)] for h in heads]
            rws = [cst_s[pl.ds(h, 1), :] for h in heads]
            lasts = [cs_s[pl.ds(SSD_Q - 1, 1), pl.ds(h, 1)] for h in heads]
            e_lasts = v["e_lasts"] = [jnp.exp(l_) for l_ in lasts]
            v["dtg"] = _per_head([dt_s[:, pl.ds(h, 1)] for h in heads])
            v["dskg"] = _per_head([dskip_ref[:, pl.ds(h, 1)] for h in heads])
            e_col = _per_head([jnp.exp(c_) for c_ in cols])
            w = v["w"] = _per_head([jnp.exp(l_ - c_) for l_, c_ in zip(lasts, cols)])
            xdg = v["xdg"] = xg * v["dtg"]
            sg = sin_ref[0, g]
            dsn = dstate[g]
            v["gm"] = _dot_nt(cg, bg)
            gmt = _dot_nt(bg, cg)
            v["y_off"] = e_col * _dot_nt(cg, sg)
            d_cs = e_col * dyg
            v["dcg"] = _dot_nn(d_cs, sg)
            dstate[g] = _dot_tn(d_cs, cg) + _per_head_rows(e_lasts) * dsn
            v["dbg"] = _dot_nn(w * xdg, dsn)
            v["dtt"] = _dot_nt(bg, dsn)
            v["dsn_s"] = dsn * sg
            segs = [cols[k] - rws[k] for k in range(SSD_K)]
            v["decays"] = [jnp.exp(jnp.where(causal, s_, NEG)) for s_ in segs]
            v["dm4"] = _dot_nt(jnp.concatenate([_own_columns(dyg, k) for k in range(SSD_K)], axis=0), xdg)
            v["z4"] = _dot_nn(jnp.concatenate([gmt * jnp.exp(jnp.where(anti, -s_, NEG)) for s_ in segs], axis=0), dyg)
            return v

        def back(g, v, sums):
            dcs_cf, dcs_rf, ddt_cf, dd_vec, dlast_vec = sums
            dxd = v["w"] * v["dtt"] + _per_head([v["z4"][k * SSD_Q:(k + 1) * SSD_Q] for k in range(SSD_K)])
            dw = _headsum(v["dtt"] * v["xdg"] * v["w"], g)
            dcs_cf = dcs_cf + _headsum(v["dyg"] * v["y_off"], g) - dw
            dlast_vec = dlast_vec + jnp.sum(dw, axis=0, keepdims=True)
            dgm = jnp.zeros((SSD_Q, SSD_Q), F32)
            for k, h in enumerate(v["heads"]):
                dm = v["dm4"][k * SSD_Q:(k + 1) * SSD_Q]
                dseg = dm * v["gm"] * v["decays"][k]
                dgm = dgm + dm * v["decays"][k]
                oh_r = _onehot_row(h)
                dcs_cf = dcs_cf + jnp.sum(dseg, axis=1, keepdims=True) * oh_r
                dcs_rf = dcs_rf - _onehot_col(h) * jnp.sum(dseg, axis=0, keepdims=True)
                dlast_vec = dlast_vec + (jnp.sum(v["dsn_s"][k * SSD_P:(k + 1) * SSD_P], keepdims=True)
                                         * v["e_lasts"][k] * oh_r)
            dx_ref[:, pl.ds(g * GROUP_W, GROUP_W)] = dxd * v["dtg"] + v["dskg"] * v["dyg"]
            ddt_cf = ddt_cf + _headsum(dxd * v["xg"], g)
            dd_vec = dd_vec + _headsum(jnp.sum(v["dyg"] * v["xg"], axis=0, keepdims=True), g)
            dx_ref[:, pl.ds(D_INNER + g * SSD_N, SSD_N)] = v["dbg"] + _dot_tn(dgm, v["cg"])
            dx_ref[:, pl.ds(D_INNER + (SSD_G + g) * SSD_N, SSD_N)] = v["dcg"] + _dot_nn(dgm, v["bg"])
            return dcs_cf, dcs_rf, ddt_cf, dd_vec, dlast_vec

        sums = (dcs_cf, dcs_rf, ddt_cf, dd_vec, dlast_vec)
        ahead = front(0)
        for g in range(SSD_G):
            cur, ahead = ahead, (front(g + 1) if g + 1 < SSD_G else None)
            sums = back(g, cur, sums)
        dcs_cf, dcs_rf, ddt_cf, dd_vec, dlast_vec = sums
        dcs = dcs_cf + dcs_rf.T + jnp.where(is_last_row, dlast_vec, 0.0)
        da = _dot_f32(_tri(False).astype(F32), dcs)
        ddt = ddt_cf + da * a_neg
        ddtraw = ddt * _sigmoid(pre)
        ddt_ref[...] = ddtraw.astype(ddt_ref.dtype)
        dbias = jnp.sum(ddtraw, axis=0, keepdims=True)
        dalog = jnp.sum(da * dt, axis=0, keepdims=True) * a_neg
        first_step = jnp.logical_and(b == 0, c == 0)

        @pl.when(first_step)
        def _():
            dbias_ref[...] = dbias
            dalog_ref[...] = dalog
            ddskip_ref[...] = dd_vec

        @pl.when(jnp.logical_not(first_step))
        def _():
            dbias_ref[...] += dbias
            dalog_ref[...] += dalog
            ddskip_ref[...] += dd_vec

        if nx:
            pl.when(jnp.logical_and(b == n_seq - 1, c == nc - 1))(finish)

    def rowblk(b, c):
        return b * nc + (nc - 1 - c)

    vec = pl.BlockSpec((1, LANES), lambda b, c: (0, 0))
    return pl.pallas_call(
        body, name="ssd_bwd",
        grid=(n_seq, nc),
        in_specs=[pl.BlockSpec((SSD_Q, CONV_DIM), lambda b, c: (rowblk(b, c), 0)),
                  pl.BlockSpec((SSD_Q, LANES), lambda b, c: (rowblk(b, c), 0)), vec, vec, vec,
                  pl.BlockSpec((1, SSD_G, GROUP_W, SSD_N), lambda b, c: (rowblk(b, c), 0, 0, 0)),
                  pl.BlockSpec((SSD_Q, D_INNER), lambda b, c: (rowblk(b, c), 0))] + comm.specs,
        out_specs=[pl.BlockSpec((SSD_Q, CONV_DIM), lambda b, c: (rowblk(b, c), 0)),
                   pl.BlockSpec((SSD_Q, LANES), lambda b, c: (rowblk(b, c), 0)), vec, vec, vec] + comm.specs,
        out_shape=[jax.ShapeDtypeStruct((rows, CONV_DIM), F32), jax.ShapeDtypeStruct((rows, LANES), BF16),
                   jax.ShapeDtypeStruct((1, LANES), F32), jax.ShapeDtypeStruct((1, LANES), F32),
                   jax.ShapeDtypeStruct((1, LANES), F32)] + comm.out_shape,
        scratch_shapes=[pltpu.VMEM((SSD_G, GROUP_W, SSD_N), F32), pltpu.VMEM((SSD_Q, LANES), F32),
                        pltpu.VMEM((LANES, SSD_Q), F32), pltpu.VMEM((SSD_Q, LANES), F32)] + comm.sems,
        compiler_params=_params(("arbitrary", "arbitrary")),
    )(xact, dtraw, dt_bias, a_log, d_skip, sin, dy, *comm.arrays)


QKV_W = 3 * ATT_OUT
PAIR_W = 2 * ATT_HD
HEAD_PAIRS = ATT_H // 2
PREP_ROWS = 512


def _by_residue(a, n_seq, seq, dil):
    if dil == 1:
        return a
    return a.reshape(n_seq, seq // dil, dil, a.shape[1]).transpose(0, 2, 1, 3).reshape(a.shape)


def _by_token(a, n_seq, seq, dil):
    if dil == 1:
        return a
    return a.reshape(n_seq, dil, seq // dil, a.shape[1]).transpose(0, 2, 1, 3).reshape(a.shape)


def _head_sums(x, fn):
    lo = jnp.logical_not(lax.broadcasted_iota(jnp.int32, (1, 2 * ATT_HD), 1) >= ATT_HD)
    parts = []
    for p in range(ATT_H // 2):
        slab = x[:, p * 2 * ATT_HD:(p + 1) * 2 * ATT_HD]
        s_lo = fn(jnp.sum(jnp.where(lo, slab, 0.0), axis=1, keepdims=True))
        s_hi = fn(jnp.sum(jnp.where(lo, 0.0, slab), axis=1, keepdims=True))
        parts.append(jnp.where(lo, s_lo, s_hi))
    return jnp.concatenate(parts, axis=1)


def _head_rstd(x):
    return _head_sums(x * x, lambda s: lax.rsqrt(s * (1.0 / ATT_HD) + EPS))


def _head_rms_bwd(x, g_t, dy):
    r = _head_rstd(x)
    xh = x * r
    dyg = dy * g_t
    mean = _head_sums(dyg * xh, lambda s: s * (1.0 / ATT_HD))
    return r * (dyg - xh * mean), jnp.sum(dy * xh, axis=0, keepdims=True)


def qk_prep(qkv, gq_t, gk_t, rows, name):
    tb = min(PREP_ROWS, rows)

    def body(x_ref, gq_ref, gk_ref, o_ref):
        q = x_ref[:, pl.ds(0, ATT_OUT)]
        k = x_ref[:, pl.ds(ATT_OUT, ATT_OUT)]
        o_ref[:, pl.ds(0, ATT_OUT)] = (q * _head_rstd(q) * (gq_ref[...] * ATT_SCALE)).astype(o_ref.dtype)
        o_ref[:, pl.ds(ATT_OUT, ATT_OUT)] = (k * _head_rstd(k) * gk_ref[...]).astype(o_ref.dtype)
        o_ref[:, pl.ds(2 * ATT_OUT, ATT_OUT)] = x_ref[:, pl.ds(2 * ATT_OUT, ATT_OUT)].astype(o_ref.dtype)

    gspec = pl.BlockSpec((1, ATT_OUT), lambda i: (0, 0))
    blk = pl.BlockSpec((tb, QKV_W), lambda i: (i, 0))
    return pl.pallas_call(
        body, name=name,
        grid=(rows // tb,),
        in_specs=[blk, gspec, gspec],
        out_specs=blk,
        out_shape=jax.ShapeDtypeStruct((rows, QKV_W), MXU),
        compiler_params=_params(("parallel",)),
    )(qkv, gq_t, gk_t)


def _lane_hi():
    return lax.broadcasted_iota(jnp.int32, (1, PAIR_W), 1) >= ATT_HD


def _band_mask2(first_valid, query_rows):
    i = lax.broadcasted_iota(jnp.int32, (ATT_BLK, 2 * ATT_BLK), 0)
    j = lax.broadcasted_iota(jnp.int32, (ATT_BLK, 2 * ATT_BLK), 1)
    left = j < ATT_BLK
    right = jnp.logical_not(left)
    if query_rows:
        return jnp.logical_or(jnp.logical_and(jnp.logical_and(left, i <= j), first_valid),
                              jnp.logical_and(right, i >= j - ATT_BLK))
    return jnp.logical_or(jnp.logical_and(left, j >= i),
                          jnp.logical_and(jnp.logical_and(right, j - ATT_BLK <= i), first_valid))


def _only_head(slab, hi):
    keep = _lane_hi() if hi else jnp.logical_not(_lane_hi())
    return jnp.where(keep, slab, jnp.zeros_like(slab))


def attn_fwd(nq, n_seq, seq, dil, name):
    nb = seq // dil // ATT_BLK
    rows = n_seq * seq

    def body(cur_ref, prev_ref, o_ref, lse_ref, s_scr, p_scr):
        n = pl.program_id(1)
        mask = _band_mask2(n > 0, True)
        for h in range(ATT_H):
            sl = pl.ds((h // 2) * PAIR_W, PAIR_W)
            ks = pl.ds(ATT_OUT + (h // 2) * PAIR_W, PAIR_W)
            kcat = jnp.concatenate([prev_ref[:, ks], cur_ref[:, ks]], axis=0)
            s_scr[h] = jnp.where(mask, _dot_nt(_only_head(cur_ref[:, sl], h % 2), kcat), NEG)
        s_all = s_scr[...]
        mx = jnp.max(s_all, axis=2, keepdims=True)
        p_all = jnp.exp(s_all - mx)
        den = jnp.sum(p_all, axis=2, keepdims=True)
        p_scr[...] = p_all.astype(p_scr.dtype)
        inv = 1.0 / den
        lse = mx + jnp.log(den)
        lse_blk = jnp.zeros((ATT_BLK, LANES), F32)
        for h in range(ATT_H):
            lse_blk = lse_blk + lse[h] * _onehot_row(h)
        lse_ref[...] = lse_blk
        for pr in range(HEAD_PAIRS):
            vs = pl.ds(2 * ATT_OUT + pr * PAIR_W, PAIR_W)
            vcat = jnp.concatenate([prev_ref[:, vs], cur_ref[:, vs]], axis=0)
            lo = _dot_nn(p_scr[2 * pr], vcat) * inv[2 * pr]
            hi = _dot_nn(p_scr[2 * pr + 1], vcat) * inv[2 * pr + 1]
            o_ref[:, pl.ds(pr * PAIR_W, PAIR_W)] = jnp.where(_lane_hi(), hi, lo)

    def blk(width, shift):
        if shift:
            return pl.BlockSpec((ATT_BLK, width), lambda s, n: (s * nb + jnp.maximum(n - 1, 0), 0))
        return pl.BlockSpec((ATT_BLK, width), lambda s, n: (s * nb + n, 0))

    return pl.pallas_call(
        body, name=name,
        grid=(n_seq * dil, nb),
        in_specs=[blk(QKV_W, 0), blk(QKV_W, -1)],
        out_specs=[blk(ATT_OUT, 0), blk(LANES, 0)],
        out_shape=[jax.ShapeDtypeStruct((rows, ATT_OUT), F32), jax.ShapeDtypeStruct((rows, LANES), F32)],
        scratch_shapes=[pltpu.VMEM((ATT_H, ATT_BLK, 2 * ATT_BLK), F32), pltpu.VMEM((ATT_H, ATT_BLK, 2 * ATT_BLK), MXU)],
        compiler_params=_params(("parallel", "arbitrary")),
    )(nq, nq)


def attn_bwd(nq, do, lse, wts, rsum, n_seq, seq, dil, name):
    nb = seq // dil // ATT_BLK

    def body(prev_ref, cur_ref, nxt_ref, do_c, do_x, lse_c, lse_x, wt_c, wt_x, rs_c, rs_x, dn_ref):
        n = pl.program_id(1)
        mask_q = _band_mask2(n > 0, True)
        mask_k = _band_mask2(n < nb - 1, False)
        wc, wx = wt_c[...], wt_x[...]
        lse_t = jnp.concatenate([lse_c[...].T, lse_x[...].T], axis=1)
        dl_t = jnp.concatenate([(-wc * rs_c[...]).T, (-wx * rs_x[...]).T], axis=1)
        def operands(pr):
            sl = pl.ds(pr * PAIR_W, PAIR_W)
            ks = pl.ds(ATT_OUT + pr * PAIR_W, PAIR_W)
            vs = pl.ds(2 * ATT_OUT + pr * PAIR_W, PAIR_W)
            he, ho = pl.ds(2 * pr, 1), pl.ds(2 * pr + 1, 1)
            q_c, k_c, v_c = cur_ref[:, sl], cur_ref[:, ks], cur_ref[:, vs]
            dog_c = do_c[:, sl] * jnp.where(_lane_hi(), wt_c[:, ho], wt_c[:, he])
            dog_x = do_x[:, sl] * jnp.where(_lane_hi(), wt_x[:, ho], wt_x[:, he])
            return dict(q_c=q_c, k_c=k_c, v_c=v_c, qcat=jnp.concatenate([q_c, nxt_ref[:, sl]], axis=0),
                        kcat=jnp.concatenate([prev_ref[:, ks], k_c], axis=0),
                        vcat=jnp.concatenate([prev_ref[:, vs], v_c], axis=0),
                        dog=jnp.concatenate([dog_c, dog_x], axis=0).astype(MXU))

        def scores(o, h):
            hi, one = h % 2, pl.ds(h, 1)
            dl_col = -wt_c[:, one] * rs_c[:, one]
            p_q = jnp.exp(jnp.where(mask_q, _dot_nt(_only_head(o["q_c"], hi), o["kcat"]) - lse_c[:, one], NEG))
            ds_q = p_q * (_dot_nt(_only_head(o["dog"][:ATT_BLK], hi), o["vcat"]) + dl_col)
            p_t = jnp.exp(jnp.where(mask_k, _dot_nt(_only_head(o["k_c"], hi), o["qcat"]) - lse_t[h:h + 1, :], NEG))
            ds_t = p_t * (_dot_nt(_only_head(o["v_c"], hi), o["dog"]) + dl_t[h:h + 1, :])
            return ds_q, ds_t, p_t

        ops = [operands(pr) for pr in range(HEAD_PAIRS)]
        ahead = scores(ops[0], 0)
        res = []
        for h in range(ATT_H):
            o = ops[h // 2]
            (ds_q, ds_t, p_t), ahead = ahead, (scores(ops[(h + 1) // 2], h + 1) if h + 1 < ATT_H else None)
            res.append((_dot_nn(ds_q, o["kcat"]), _dot_nn(ds_t, o["qcat"]), _dot_nn(p_t, o["dog"])))
            if h % 2:
                for t, first in enumerate((0, ATT_OUT, 2 * ATT_OUT)):
                    dn_ref[:, pl.ds(first + (h // 2) * PAIR_W, PAIR_W)] = jnp.where(_lane_hi(), res[h][t], res[h - 1][t])

    def at(shift, width):
        if shift < 0:
            return pl.BlockSpec((ATT_BLK, width), lambda s, n: (s * nb + jnp.maximum(n - 1, 0), 0))
        if shift > 0:
            return pl.BlockSpec((ATT_BLK, width), lambda s, n: (s * nb + jnp.minimum(n + 1, nb - 1), 0))
        return pl.BlockSpec((ATT_BLK, width), lambda s, n: (s * nb + n, 0))

    return pl.pallas_call(
        body, name=name,
        grid=(n_seq * dil, nb),
        in_specs=[at(-1, QKV_W), at(0, QKV_W), at(1, QKV_W), at(0, ATT_OUT), at(1, ATT_OUT),
                  at(0, LANES), at(1, LANES), at(0, LANES), at(1, LANES), at(0, LANES), at(1, LANES)],
        out_specs=at(0, QKV_W),
        out_shape=jax.ShapeDtypeStruct((n_seq * seq, QKV_W), F32),
        compiler_params=_params(("parallel", "arbitrary")),
    )(nq, nq, nq, do, do, lse, lse, wts, wts, rsum, rsum)


def qk_post(qkv, dn, gq_t, gk_t, rows, name):
    tb = min(PREP_ROWS, rows)

    def body(x_ref, dn_ref, gq_ref, gk_ref, o_ref, dgq_ref, dgk_ref):
        i = pl.program_id(0)
        qs, ks, vs = pl.ds(0, ATT_OUT), pl.ds(ATT_OUT, ATT_OUT), pl.ds(2 * ATT_OUT, ATT_OUT)
        dq, dgq = _head_rms_bwd(x_ref[:, qs], gq_ref[...], dn_ref[:, qs] * ATT_SCALE)
        dk, dgk = _head_rms_bwd(x_ref[:, ks], gk_ref[...], dn_ref[:, ks])
        o_ref[:, qs] = dq.astype(o_ref.dtype)
        o_ref[:, ks] = dk.astype(o_ref.dtype)
        o_ref[:, vs] = dn_ref[:, vs].astype(o_ref.dtype)

        @pl.when(i == 0)
        def _():
            dgq_ref[...] = dgq
            dgk_ref[...] = dgk

        @pl.when(i != 0)
        def _():
            dgq_ref[...] += dgq
            dgk_ref[...] += dgk

    gspec = pl.BlockSpec((1, ATT_OUT), lambda i: (0, 0))
    blk = pl.BlockSpec((tb, QKV_W), lambda i: (i, 0))
    return pl.pallas_call(
        body, name=name,
        grid=(rows // tb,),
        in_specs=[blk, blk, gspec, gspec],
        out_specs=[blk, gspec, gspec],
        out_shape=[jax.ShapeDtypeStruct((rows, QKV_W), MXU), jax.ShapeDtypeStruct((1, ATT_OUT), F32),
                   jax.ShapeDtypeStruct((1, ATT_OUT), F32)],
        compiler_params=_params(("arbitrary",)),
    )(qkv, dn, gq_t, gk_t)


def all_gather(arrays, name):
    n = len(arrays)

    def body(*refs):
        in_refs, out_refs = refs[:n], refs[n:2 * n]
        send_sems, recv_sems, local_sems = refs[2 * n:]
        _, me, _ = _peer(0)
        sib, _, _ = _peer(1)

        def first(t, k, arriving):
            dev, pid, _ = _peer(k)
            return _remote(in_refs[t], out_refs[t].at[pid if arriving else me], send_sems, recv_sems, t, k, dev)

        def passed(t, k, arriving):
            slot = out_refs[t].at[_peer(k + 1 if arriving else k)[1]]
            return _remote(slot, slot, send_sems, recv_sems, t, k + 1, sib)

        def own(t):
            return pltpu.make_async_copy(in_refs[t], out_refs[t].at[me], local_sems.at[t])

        for t in range(n):
            own(t).start()
            for k in (1,) + OTHER_CHIPS:
                first(t, k, False).start()
        for t in range(n):
            for k in OTHER_CHIPS:
                first(t, k, True).wait_recv()
                passed(t, k, False).start()
        for t in range(n):
            first(t, 1, True).wait_recv()
            for k in OTHER_CHIPS:
                passed(t, k, True).wait_recv()
        for t in range(n):
            for k in (1,) + OTHER_CHIPS:
                first(t, k, False).wait_send()
            for k in OTHER_CHIPS:
                passed(t, k, False).wait_send()
            own(t).wait()

    anyspec = pl.BlockSpec(memory_space=pl.ANY)
    return pl.pallas_call(
        body, name=name,
        in_specs=[anyspec] * n,
        out_specs=[anyspec] * n,
        out_shape=[jax.ShapeDtypeStruct((N_DEV,) + tuple(a.shape), a.dtype) for a in arrays],
        scratch_shapes=[pltpu.SemaphoreType.DMA((n, N_DEV)), pltpu.SemaphoreType.DMA((n, N_DEV)),
                        pltpu.SemaphoreType.DMA((n,))],
    )(*arrays)


def pair_add(a, b, name):
    _, r, c = a.shape
    rb = r if r <= 512 else (128 if c > 1024 else 256)
    assert r % rb == 0

    def body(a_ref, b_ref, o_ref):
        o_ref[...] = (a_ref[...].astype(F32) + b_ref[...].astype(F32)).astype(o_ref.dtype)

    blk = pl.BlockSpec((1, rb, c), lambda q, i: (q, i, 0))
    return pl.pallas_call(
        body, name=name,
        grid=(N_CHIPS, r // rb),
        in_specs=[blk, blk],
        out_specs=blk,
        out_shape=jax.ShapeDtypeStruct(a.shape, a.dtype),
        compiler_params=_params(("parallel", "parallel")),
    )(a, b)


def adamw(parts, w, m, v, name):
    r, c = w.shape[-2:]
    n_parts = parts.shape[0]
    rb = r if r <= 512 else (128 if c > 1024 else 256)
    assert r % rb == 0

    def body(p_ref, w_ref, m_ref, v_ref, g_out, d_out, m_out, v_out):
        g = p_ref[0].astype(F32)
        for i in range(1, n_parts):
            g = g + p_ref[i].astype(F32)
        m_new = ADAM_B1 * m_ref[...] + (1.0 - ADAM_B1) * g
        v_new = ADAM_B2 * v_ref[...] + (1.0 - ADAM_B2) * (g * g)
        m_hat = m_new / (1.0 - ADAM_B1 ** ADAM_STEP)
        v_hat = v_new / (1.0 - ADAM_B2 ** ADAM_STEP)
        g_out[...] = g
        d_out[...] = -ADAM_LR * (m_hat / (jnp.sqrt(v_hat) + ADAM_EPS) + ADAM_WD * w_ref[...])
        m_out[...] = m_new
        v_out[...] = v_new

    if w.ndim == 3:
        blk = pl.BlockSpec((None, rb, c), lambda i: (0, i, 0))
    else:
        blk = pl.BlockSpec((rb, c), lambda i: (i, 0))
    return pl.pallas_call(
        body, name=name,
        grid=(r // rb,),
        in_specs=[pl.BlockSpec((n_parts, rb, c), lambda i: (0, i, 0)), blk, blk, blk],
        out_specs=[blk] * 4,
        out_shape=[jax.ShapeDtypeStruct(w.shape, F32)] * 4,
        compiler_params=_params(("parallel",)),
    )(parts, w, m, v)


def _pad_lanes(vec, n=LANES):
    return jnp.pad(vec, ((0, 0), (0, n - vec.shape[1])))


COL_SHARDED = ("w_in", "ssd_conv_w", "w_attn_proj", "w_up", "ffn_conv_w")
MATRICES = ("w_in", "w_attn_proj", "w_up", "w_ssd_proj", "w_out", "w_down")
LATE = ("w_ssd_proj", "w_attn_proj", "w_out", "w_up", "ffn_conv_w", "w_down")


def _narrow(name, a):
    return a.astype(MXU) if name in MATRICES else a


def _from_gathered(name, g):
    if name in COL_SHARDED:
        return jnp.transpose(g, (1, 0, 2)).reshape(g.shape[1], N_DEV * g.shape[2])
    return g.reshape(N_DEV * g.shape[1], g.shape[2])


def _to_slabs(name, g):
    if name in COL_SHARDED:
        return jnp.transpose(g.reshape(g.shape[0], N_DEV, g.shape[1] // N_DEV), (1, 0, 2))
    return g.reshape(N_DEV, g.shape[0] // N_DEV, g.shape[1])


def _columns(m, a, b):
    if m.ndim == 2:
        return m[:, a:b]
    c = m.shape[2]
    cuts = [m[j][:, max(a - j * c, 0):min(b - j * c, c)] for j in range(a // c, (b - 1) // c + 1)]
    return cuts[0] if len(cuts) == 1 else jnp.concatenate(cuts, axis=1)


def _column_shards(pieces, c):
    shards = []
    for j in range(N_DEV):
        cuts = []
        for start, arr in pieces:
            lo, hi = max(j * c - start, 0), min((j + 1) * c - start, arr.shape[1])
            if lo < hi:
                cuts.append(arr[:, lo:hi])
        shards.append(cuts[0] if len(cuts) == 1 else jnp.concatenate(cuts, axis=1))
    return jnp.stack(shards)


def local_step(x, target, w, late=None):
    n_seq, seq, _ = x.shape
    rows = n_seq * seq
    x = x.reshape(rows, D_MODEL)
    target = target.reshape(rows, D_MODEL)
    mx = lambda a: a.astype(MXU)

    splits = [sum(IN_WIDTHS[:i]) for i in range(len(IN_WIDTHS) + 1)]
    w_in = w["w_in"]
    part = lambda i: _columns(w_in, splits[i], splits[i + 1])
    w_z, w_xbc, w_gs, w_ga = mx(part(0)), mx(part(1)), mx(part(6)), mx(part(7))
    w_dt = mx(_pad_lanes(part(2)))
    head_group = lambda t, g: (splits[3 + t] + g * ATT_OUT, splits[3 + t] + (g + 1) * ATT_OUT)
    w_qkv = [mx(jnp.concatenate([_columns(w_in, *head_group(t, g)) for t in range(3)], axis=1))
             for g in range(ATT_GROUPS)]
    conv_w, conv_b, fconv_b = w["ssd_conv_w"], w["ssd_conv_b"], w["ffn_conv_b"]
    dt_bias, a_log, d_skip = _pad_lanes(w["dt_bias"]), _pad_lanes(w["a_log"]), _pad_lanes(w["d_skip"])
    g1, g2, gn, gq, gk = w["norm1_g"], w["norm2_g"], w["ssd_norm_g"], w["q_norm_g"], w["k_norm_g"]

    tb = min(512, seq)
    tbm = min(256, seq)
    cw = 1024
    rw = lambda fn, name, ncol, ins, params=(), outs=(), accs=(), tb_=tb: rowwise(
        fn, name, rows, seq, tb_, ncol, ins, params, outs, accs)

    (h,) = rw(lambda ctx, xv, g: _rms_fwd(xv, g), "rms1_fwd", 1, [(x, D_MODEL, 0, None)], [(g1, None, 0)],
              [(D_MODEL, D_MODEL, 0, MXU)])
    z = matmul(h, w_z, "mm_z")
    xbc = matmul(h, w_xbc, "mm_xbc")
    dtraw = matmul(h, w_dt, "mm_dt")
    by_residue = lambda a, g: _by_residue(a, n_seq, seq, ATT_DILATIONS[g])
    by_token = lambda a, g: _by_token(a, n_seq, seq, ATT_DILATIONS[g])
    h_res = [by_residue(h, g) for g in range(ATT_GROUPS)]
    qkv = [matmul(h_res[g], w_qkv[g], f"mm_qkv{g}") for g in range(ATT_GROUPS)]
    gs = matmul(h, w_gs, "mm_gs")
    ga = matmul(h, w_ga, "mm_ga")

    def conv_silu(ctx, xh, wv, bv):
        return _silu(bv + _conv_prev(xh[0], xh[1], wv, ctx.first, SSD_CONV))

    (xact,) = rw(conv_silu, "ssd_conv_fwd", CONV_DIM // cw, [(xbc, cw, 0, "prev")],
                 [(conv_w, cw, 0), (conv_b, cw, 0)], [(CONV_DIM, cw, 0, F32)])
    if late is None:
        y, sin = ssd_fwd(xact, dtraw, dt_bias, a_log, d_skip, n_seq, seq)
    else:
        y, sin, *gathered = ssd_fwd(xact, dtraw, dt_bias, a_log, d_skip, n_seq, seq,
                                    comm=direct_exchange([(late[n], "gather") for n in LATE]))
        w = {**w, **{n: g if n == "w_up" else _from_gathered(n, g) for n, g in zip(LATE, gathered)}}
    w_sp, w_ap, w_o, w_d = mx(w["w_ssd_proj"]), mx(w["w_attn_proj"]), mx(w["w_out"]), mx(w["w_down"])
    w_ug, w_uv = mx(_columns(w["w_up"], 0, D_FF)), mx(_columns(w["w_up"], D_FF, 2 * D_FF))
    fconv_w = w["ffn_conv_w"]

    def gated_norm(ctx, yv, zv, g):
        yz = yv * _silu(zv)
        return jnp.concatenate([_rms_fwd(yz[:, i:i + NORM_GROUP], g[:, i:i + NORM_GROUP])
                                for i in range(0, cw, NORM_GROUP)], axis=1)

    (y_ssd,) = rw(gated_norm, "ssd_post_fwd", D_INNER // cw, [(y, cw, 0, None), (z, cw, 0, None)], [(gn, cw, 0)],
                  [(D_INNER, cw, 0, MXU)])

    gq_t, gk_t = jnp.tile(gq, (1, ATT_H)), jnp.tile(gk, (1, ATT_H))
    nq = [qk_prep(qkv[g], gq_t, gk_t, rows, f"qk_prep{g}") for g in range(ATT_GROUPS)]
    att = [attn_fwd(nq[g], n_seq, seq, ATT_DILATIONS[g], f"attn_fwd{g}") for g in range(ATT_GROUPS)]

    def combine(ctx, o0, o1, o2, l0, l1, l2):
        mxl = jnp.maximum(jnp.maximum(l0, l1), l2)
        e = [jnp.exp(l - mxl) for l in (l0, l1, l2)]
        inv = 1.0 / (e[0] + e[1] + e[2])
        ws = [ei * inv for ei in e]
        out = sum(_expand_heads(wi) * oi for wi, oi in zip(ws, (o0, o1, o2)))
        return (out, *ws)

    y_attn, wt0, wt1, wt2 = rw(
        combine, "attn_combine", 1,
        [(by_token(att[g][0], g), ATT_OUT, 0, None) for g in range(3)]
        + [(by_token(att[g][1], g), LANES, 0, None) for g in range(3)], [],
        [(ATT_OUT, ATT_OUT, 0, F32)] + [(LANES, LANES, 0, F32)] * 3)
    wts = (wt0, wt1, wt2)

    ps = matmul(y_ssd, w_sp, "mm_ssd_proj")
    pa = matmul(y_attn, w_ap, "mm_attn_proj")
    (merged,) = rw(lambda ctx, a, b, c, d: _sigmoid(c) * a + _sigmoid(d) * b, "merge_fwd", D_MODEL // cw,
                   [(ps, cw, 0, None), (pa, cw, 0, None), (gs, cw, 0, None), (ga, cw, 0, None)], [],
                   [(D_MODEL, cw, 0, MXU)])
    x1 = matmul(merged, w_o, "mm_out", add=x)
    (h2,) = rw(lambda ctx, xv, g: _rms_fwd(xv, g), "rms2_fwd", 1, [(x1, D_MODEL, 0, None)], [(g2, None, 0)],
               [(D_MODEL, D_MODEL, 0, MXU)])
    up_g = matmul(h2, w_ug, "mm_up_g")
    up_v = matmul(h2, w_uv, "mm_up_v")
    fw = D_FF // 2
    nfc = D_FF // fw

    def mlp_act(ctx, ug, uv, wg, wv, bg, bv):
        cg = bg + _conv_prev(ug[0], ug[1], wg, ctx.first, FFN_CONV)
        cv = bv + _conv_prev(uv[0], uv[1], wv, ctx.first, FFN_CONV)
        return _silu(cg) * cv

    (act,) = rw(mlp_act, "mlp_act_fwd", nfc, [(up_g, fw, 0, "prev"), (up_v, fw, 0, "prev")],
                [(fconv_w, fw, 0), (fconv_w, fw, nfc), (fconv_b, fw, 0), (fconv_b, fw, nfc)], [(D_FF, fw, 0, MXU)],
                tb_=tbm)
    x2 = matmul(act, w_d, "mm_down", add=x1)

    def loss_fn(ctx, xv, tv):
        d = xv - tv
        g = d * (1.0 / D_MODEL)
        return g, g, jnp.sum(d * d, axis=0, keepdims=True)

    dx2, dx2_m, sq = rw(loss_fn, "loss", 1, [(x2, D_MODEL, 0, None), (target, D_MODEL, 0, None)], [],
                        [(D_MODEL, D_MODEL, 0, F32), (D_MODEL, D_MODEL, 0, MXU)], [(1, D_MODEL)])

    grads = {}
    dact = matmul(dx2_m, w_d, "mm_d_act", tb=True)
    grads["w_down"] = matmul(act, dx2_m, "mm_dw_down", ta=True, out_dtype=MXU)

    def mlp_bwd(ctx, da, ug, uv, wg, wv, bg, bv):
        cg, cg_n = _conv_pre(ug, wg, bg, ctx.first, FFN_CONV)
        cv, cv_n = _conv_pre(uv, wv, bv, ctx.first, FFN_CONV)
        da_c, da_n = da
        dup_g_, dwg, dbg = _conv_bwd(da_c * cv * _silu_grad(cg), da_n * cv_n * _silu_grad(cg_n), ug, wg, ctx, FFN_CONV)
        dup_v_, dwv, dbv = _conv_bwd(da_c * _silu(cg), da_n * _silu(cg_n), uv, wv, ctx, FFN_CONV)
        return dup_g_, dup_v_, dwg, dbg, dwv, dbv

    dup_g, dup_v, dfw_g, dfb_g, dfw_v, dfb_v = rw(
        mlp_bwd, "mlp_bwd", nfc, [(dact, fw, 0, "next"), (up_g, fw, 0, "both"), (up_v, fw, 0, "both")],
        [(fconv_w, fw, 0), (fconv_w, fw, nfc), (fconv_b, fw, 0), (fconv_b, fw, nfc)],
        [(D_FF, fw, 0, MXU), (D_FF, fw, 0, MXU)], [(FFN_CONV, fw), (1, fw), (FFN_CONV, fw), (1, fw)], tb_=tbm)
    grads["ffn_conv_w"] = jnp.concatenate([dfw_g, dfw_v], axis=1)
    grads["ffn_conv_b"] = jnp.concatenate([dfb_g, dfb_v], axis=1)
    dh2 = matmul(dup_g, w_ug, "mm_dh2_g", tb=True)
    dh2 = matmul(dup_v, w_uv, "mm_dh2_v", tb=True, add=dh2)
    dw_up = [(0, matmul(h2, dup_g, "mm_dw_up_g", ta=True, out_dtype=MXU)),
             (D_FF, matmul(h2, dup_v, "mm_dw_up_v", ta=True, out_dtype=MXU))]
    if w["w_up"].ndim == 3:
        grads["w_up"] = _column_shards(dw_up, w["w_up"].shape[2])
    else:
        grads["w_up"] = jnp.concatenate([p for _, p in dw_up], axis=1)

    def rms_bwd_fn(ctx, xv, dh_, dres, g):
        dxv, dg = _rms_bwd(xv, g, dh_)
        return dres + dxv, dg

    def rms_bwd_fn2(ctx, xv, dh_, dres, g):
        dxv, dg = rms_bwd_fn(ctx, xv, dh_, dres, g)
        return dxv, dxv, dg

    dx1, dx1_m, grads["norm2_g"] = rw(
        rms_bwd_fn2, "rms2_bwd", 1, [(x1, D_MODEL, 0, None), (dh2, D_MODEL, 0, None), (dx2, D_MODEL, 0, None)],
        [(g2, None, 0)], [(D_MODEL, D_MODEL, 0, F32), (D_MODEL, D_MODEL, 0, MXU)], [(1, D_MODEL)])

    dmerged = matmul(dx1_m, w_o, "mm_d_merged", tb=True)
    grads["w_out"] = matmul(merged, dx1_m, "mm_dw_out", ta=True, out_dtype=MXU)

    def merge_bwd(ctx, dm, a, b, c, d):
        sc, sd = _sigmoid(c), _sigmoid(d)
        return dm * sc, dm * sd, dm * a * sc * (1.0 - sc), dm * b * sd * (1.0 - sd)

    dps, dpa, dgs, dga = rw(merge_bwd, "merge_bwd", D_MODEL // cw,
                            [(dmerged, cw, 0, None), (ps, cw, 0, None), (pa, cw, 0, None), (gs, cw, 0, None),
                             (ga, cw, 0, None)], [], [(D_MODEL, cw, 0, MXU)] * 4)
    dy_ssd = matmul(dps, w_sp, "mm_d_y_ssd", tb=True)
    grads["w_ssd_proj"] = matmul(y_ssd, dps, "mm_dw_ssd_proj", ta=True, out_dtype=MXU)
    dy_attn = matmul(dpa, w_ap, "mm_d_y_attn", tb=True)
    grads["w_attn_proj"] = matmul(y_attn, dpa, "mm_dw_attn_proj", ta=True, out_dtype=MXU)

    (rsum,) = rw(lambda ctx, a, b: _reduce_heads(a * b), "attn_rsum", 1,
                 [(dy_attn, ATT_OUT, 0, None), (y_attn, ATT_OUT, 0, None)], [], [(LANES, LANES, 0, F32)])
    dqkv, dgq, dgk = [], 0.0, 0.0
    for g in range(ATT_GROUPS):
        dn = attn_bwd(nq[g], by_residue(dy_attn, g), att[g][1], by_residue(wts[g], g), by_residue(rsum, g), n_seq, seq,
                      ATT_DILATIONS[g], f"attn_bwd{g}")
        d_, a_, b_ = qk_post(qkv[g], dn, gq_t, gk_t, rows, f"qk_post{g}")
        dqkv.append(d_)
        dgq, dgk = dgq + a_, dgk + b_
    per_head = lambda v: jnp.sum(v.reshape(ATT_H, ATT_HD), axis=0, keepdims=True)
    grads["q_norm_g"], grads["k_norm_g"] = per_head(dgq), per_head(dgk)

    def gated_norm_bwd(ctx, dyn, yv, zv, g):
        sz = _silu(zv)
        yz = yv * sz
        dyz, dgs_ = [], []
        for i in range(0, cw, NORM_GROUP):
            a, b = _rms_bwd(yz[:, i:i + NORM_GROUP], g[:, i:i + NORM_GROUP], dyn[:, i:i + NORM_GROUP])
            dyz.append(a)
            dgs_.append(b)
        dyz = jnp.concatenate(dyz, axis=1)
        return dyz * sz, dyz * yv * _silu_grad(zv), jnp.concatenate(dgs_, axis=1)

    dy, dz, grads["ssd_norm_g"] = rw(gated_norm_bwd, "ssd_post_bwd", D_INNER // cw,
                                     [(dy_ssd, cw, 0, None), (y, cw, 0, None), (z, cw, 0, None)], [(gn, cw, 0)],
                                     [(D_INNER, cw, 0, F32), (D_INNER, cw, 0, MXU)], [(1, cw)])
    if late is None:
        dxact, ddt, dbias, dalog, ddskip = ssd_bwd(xact, dtraw, dt_bias, a_log, d_skip, sin, dy, n_seq, seq)
    else:
        dxact, ddt, dbias, dalog, ddskip, *parts = ssd_bwd(
            xact, dtraw, dt_bias, a_log, d_skip, sin, dy, n_seq, seq,
            comm=direct_exchange([(grads[n] if n == "w_up" else _to_slabs(n, _narrow(n, grads[n])), "scatter")
                                  for n in LATE]))
        grads.update(zip(LATE, parts))
    grads["dt_bias"], grads["a_log"], grads["d_skip"] = dbias[:, :SSD_H], dalog[:, :SSD_H], ddskip[:, :SSD_H]

    def conv_silu_bwd(ctx, dxa, xin, wv, bv):
        pre, pre_n = _conv_pre(xin, wv, bv, ctx.first, SSD_CONV)
        return _conv_bwd(dxa[0] * _silu_grad(pre), dxa[1] * _silu_grad(pre_n), xin, wv, ctx, SSD_CONV)

    dxbc, grads["ssd_conv_w"], grads["ssd_conv_b"] = rw(
        conv_silu_bwd, "ssd_conv_bwd", CONV_DIM // cw, [(dxact, cw, 0, "next"), (xbc, cw, 0, "both")],
        [(conv_w, cw, 0), (conv_b, cw, 0)], [(CONV_DIM, cw, 0, MXU)], [(SSD_CONV, cw), (1, cw)])

    pieces = [(d_, d_, h, w_, tag) for d_, w_, tag in
              ((dz, w_z, "z"), (dxbc, w_xbc, "xbc"), (ddt, w_dt, "dt"), (dgs, w_gs, "gs"), (dga, w_ga, "ga"))]
    pieces += [(by_token(dqkv[g], g), dqkv[g], h_res[g], w_qkv[g], f"qkv{g}") for g in range(ATT_GROUPS)]
    dws = {tag: matmul(h_in, dpart_h, f"mm_dw_{tag}", ta=True, out_dtype=MXU) for _, dpart_h, h_in, _, tag in pieces}
    dw_in = [(splits[0], dws["z"]), (splits[1], dws["xbc"]), (splits[2], dws["dt"][:, :SSD_H])]
    dw_in += [(head_group(t, g)[0], dws[f"qkv{g}"][:, t * ATT_OUT:(t + 1) * ATT_OUT])
              for t in range(3) for g in range(ATT_GROUPS)]
    dw_in += [(splits[6], dws["gs"]), (splits[7], dws["ga"])]
    if w_in.ndim == 3:
        grads["w_in"] = _column_shards(dw_in, w_in.shape[2])
    else:
        grads["w_in"] = jnp.concatenate([p for _, p in dw_in], axis=1)
    dh = None
    for idx, (dpart, _, _, wpart, tag) in enumerate(pieces):
        comm = NO_EXCHANGE
        if late is not None and idx == 0:
            slabs = [grads[n] if n == "w_in" else _to_slabs(n, _narrow(n, grads[n])) for n in EARLY]
            comm = sibling_exchange(slabs)
        if late is not None and idx == 1:
            core = lax.axis_index("c")
            own = [lax.dynamic_index_in_dim(s.reshape((N_CHIPS, 2) + s.shape[1:]), core, axis=1, keepdims=False)
                   for s in slabs]
            comm = chip_exchange([pair_add(a_, b_, f"rs_add_{n}") for n, a_, b_ in zip(EARLY, own, arrived)])
        dh = matmul(dpart, wpart, f"mm_dh_{tag}", tb=True, add=dh, comm=comm)
        if comm.n:
            dh, *arrived = dh
            if idx == 1:
                grads.update(zip(EARLY, arrived))
    grad_x, grads["norm1_g"] = rw(rms_bwd_fn, "rms1_bwd", 1,
                                  [(x, D_MODEL, 0, None), (dh, D_MODEL, 0, None), (dx1, D_MODEL, 0, None)],
                                  [(g1, None, 0)], [(D_MODEL, D_MODEL, 0, F32)], [(1, D_MODEL)])
    return sq, grad_x.reshape(n_seq, seq, D_MODEL), grads


EARLY = ("w_in", "ssd_conv_w")
REPLICATED = ("norm1_g", "ssd_conv_b", "dt_bias", "a_log", "d_skip", "ssd_norm_g", "q_norm_g", "k_norm_g",
              "norm2_g", "ffn_conv_b")
WEIGHTS = ("norm1_g", "w_in", "ssd_conv_w", "ssd_conv_b", "dt_bias", "a_log", "d_skip", "ssd_norm_g", "w_ssd_proj",
           "q_norm_g", "k_norm_g", "w_attn_proj", "w_out", "norm2_g", "w_up", "ffn_conv_w", "ffn_conv_b", "w_down")
PACK_ROWS, PACK_COLS = 8, 2048


def _pack(vals):
    flat = jnp.concatenate([vals[n].reshape(-1) for n in REPLICATED])
    return jnp.pad(flat, (0, PACK_ROWS * PACK_COLS - flat.shape[0])).reshape(PACK_ROWS, PACK_COLS)


def _unpack(packed, like):
    flat = packed.reshape(-1)
    out, pos = {}, 0
    for n in REPLICATED:
        size = like[n].size
        out[n] = flat[pos:pos + size].reshape(like[n].shape)
        pos += size
    return out


def step(x, target, w_raw, m_raw, v_raw):
    wsh = {n: a[0] if a.ndim == 3 else a for n, a in w_raw.items()}
    gathered = all_gather([_narrow(n, wsh[n]) for n in EARLY], "ag_weights")
    full = {n: wsh[n] for n in REPLICATED}
    full.update({n: g if n == "w_in" else _from_gathered(n, g) for n, g in zip(EARLY, gathered)})

    sq, grad_x, grads = local_step(x, target, full, late={n: _narrow(n, wsh[n]) for n in LATE})

    (small,) = all_gather([_pack({n: grads[n] for n in REPLICATED})], "ag_small")

    out_g, out_d, out_m, out_v = {}, {}, {}, {}
    for n in EARLY + LATE:
        out_g[n], out_d[n], out_m[n], out_v[n] = adamw(grads[n], w_raw[n], m_raw[n], v_raw[n], f"adamw_{n}")
    pk = adamw(small, _pack(w_raw), _pack(m_raw), _pack(v_raw), "adamw_small")
    for dst, packed in zip((out_g, out_d, out_m, out_v), pk):
        dst.update(_unpack(packed, w_raw))
    loss = lax.psum(0.5 * jnp.sum(sq) / D_MODEL, ("x", "y", "c"))
    return loss, grad_x, out_g, out_d, out_m, out_v


def kernel(x, norm1_g, w_in, ssd_conv_w, ssd_conv_b, dt_bias, a_log, d_skip, ssd_norm_g, w_ssd_proj, q_norm_g, k_norm_g, w_attn_proj, w_out, norm2_g, w_up, ffn_conv_w, ffn_conv_b, w_down, loss_target, m_norm1_g, m_w_in, m_ssd_conv_w, m_ssd_conv_b, m_dt_bias, m_a_log, m_d_skip, m_ssd_norm_g, m_w_ssd_proj, m_q_norm_g, m_k_norm_g, m_w_attn_proj, m_w_out, m_norm2_g, m_w_up, m_ffn_conv_w, m_ffn_conv_b, m_w_down, v_norm1_g, v_w_in, v_ssd_conv_w, v_ssd_conv_b, v_dt_bias, v_a_log, v_d_skip, v_ssd_norm_g, v_w_ssd_proj, v_q_norm_g, v_k_norm_g, v_w_attn_proj, v_w_out, v_norm2_g, v_w_up, v_ffn_conv_w, v_ffn_conv_b, v_w_down):
    ws = (norm1_g, w_in, ssd_conv_w, ssd_conv_b, dt_bias, a_log, d_skip, ssd_norm_g, w_ssd_proj, q_norm_g, k_norm_g,
          w_attn_proj, w_out, norm2_g, w_up, ffn_conv_w, ffn_conv_b, w_down)
    ms = (m_norm1_g, m_w_in, m_ssd_conv_w, m_ssd_conv_b, m_dt_bias, m_a_log, m_d_skip, m_ssd_norm_g, m_w_ssd_proj,
          m_q_norm_g, m_k_norm_g, m_w_attn_proj, m_w_out, m_norm2_g, m_w_up, m_ffn_conv_w, m_ffn_conv_b, m_w_down)
    vs = (v_norm1_g, v_w_in, v_ssd_conv_w, v_ssd_conv_b, v_dt_bias, v_a_log, v_d_skip, v_ssd_norm_g, v_w_ssd_proj,
          v_q_norm_g, v_k_norm_g, v_w_attn_proj, v_w_out, v_norm2_g, v_w_up, v_ffn_conv_w, v_ffn_conv_b, v_w_down)
    loss, grad_x, g, d, m, v = step(x, loss_target, dict(zip(WEIGHTS, ws)), dict(zip(WEIGHTS, ms)), dict(zip(WEIGHTS, vs)))
    ordered = lambda dct: [dct[n] for n in WEIGHTS]
    return (loss, grad_x, *ordered(g), *ordered(d), *ordered(m), *ordered(v))
```

```python
import functools

import jax
import jax.numpy as jnp
from jax import lax
from jax.experimental import pallas as pl
from jax.experimental.pallas import tpu as pltpu

F32 = jnp.float32
BF16 = jnp.bfloat16
MXU = jnp.bfloat16
HIGHEST = lax.Precision.HIGHEST
VMEM_LIMIT_BYTES = 48 * 1024 * 1024
SUBLANES = 8
LANES = 128
N_DEV = 8

D_MODEL = 1024
D_INNER = 2048
SSD_P = 64
SSD_H = 32
SSD_G = 8
SSD_K = SSD_H // SSD_G
SSD_N = 128
SSD_Q = 128
SSD_CONV = 4
CONV_DIM = D_INNER + 2 * SSD_G * SSD_N
NORM_GROUP = D_INNER // SSD_G
ATT_GROUPS = 3
ATT_H = 8
ATT_HD = 64
ATT_BLK = 128
ATT_OUT = ATT_H * ATT_HD
ATT_DILATIONS = (1, 4, 16)
ATT_SCALE = ATT_HD ** -0.5
D_FF = 2816
FFN_CONV = 3
EPS = 1e-6
NEG = -1e30
IN_WIDTHS = (D_INNER, CONV_DIM, SSD_H, 3 * ATT_OUT, 3 * ATT_OUT, 3 * ATT_OUT, D_MODEL, D_MODEL)

ADAM_LR = 0.001
ADAM_B1 = 0.9
ADAM_B2 = 0.999
ADAM_EPS = 1e-08
ADAM_WD = 0.01
ADAM_STEP = 10


def _mm(a, b, dims):
    return lax.dot_general(a.astype(MXU), b.astype(MXU), (dims, ((), ())), preferred_element_type=F32)


def _dot_nn(a, b):
    return _mm(a, b, ((1,), (0,)))


def _dot_nt(a, b):
    return _mm(a, b, ((1,), (1,)))


def _dot_tn(a, b):
    return _mm(a, b, ((0,), (0,)))


def _dot_f32(a, b):
    return lax.dot_general(a, b, (((1,), (0,)), ((), ())), precision=HIGHEST, preferred_element_type=F32)


def _sigmoid(x):
    return 1.0 / (1.0 + jnp.exp(-x))


def _silu(x):
    return x * _sigmoid(x)


def _silu_grad(x):
    s = _sigmoid(x)
    return s * (1.0 + x * (1.0 - s))


def _softplus(x):
    return jnp.maximum(x, 0.0) + jnp.log(1.0 + jnp.exp(-jnp.abs(x)))


def _rms_fwd(x, g):
    r = lax.rsqrt(jnp.mean(x * x, axis=-1, keepdims=True) + EPS)
    return x * r * g


def _rms_bwd(x, g, dy):
    r = lax.rsqrt(jnp.mean(x * x, axis=-1, keepdims=True) + EPS)
    xh = x * r
    dyg = dy * g
    dx = r * (dyg - xh * jnp.mean(dyg * xh, axis=-1, keepdims=True))
    return dx, jnp.sum(dy * xh, axis=0, keepdims=True)


def _onehot_row(h, n=LANES):
    return (lax.broadcasted_iota(jnp.int32, (1, n), 1) == h).astype(F32)


def _onehot_col(h, n=LANES):
    return (lax.broadcasted_iota(jnp.int32, (n, 1), 0) == h).astype(F32)


def _head_expand_matrix():
    r = lax.broadcasted_iota(jnp.int32, (LANES, ATT_OUT), 0)
    c = lax.broadcasted_iota(jnp.int32, (LANES, ATT_OUT), 1)
    return (c // ATT_HD == r).astype(F32)


def _split_bf16(x, parts):
    out = []
    for _ in range(parts - 1):
        hi = x.astype(BF16).astype(F32)
        out.append(hi)
        x = x - hi
    out.append(x)
    return out


def _expand_heads(w):
    e = _head_expand_matrix()
    return sum(_dot_nn(p, e) for p in _split_bf16(w, 2))


def _reduce_heads(x):
    e = _head_expand_matrix()
    return sum(_dot_nt(p, e) for p in _split_bf16(x, 3))


def _shift_prev(cur, halo, s, first):
    if s == 0:
        return cur
    rolled = pltpu.roll(cur, s, 0)
    hr = jnp.where(first, 0.0, pltpu.roll(halo, s, 0))
    rows = lax.broadcasted_iota(jnp.int32, halo.shape, 0)
    head = jnp.where(rows < s, hr, rolled[:SUBLANES])
    if cur.shape[0] == SUBLANES:
        return head
    return jnp.concatenate([head, rolled[SUBLANES:]], axis=0)


def _shift_next(cur, halo, s, last):
    if s == 0:
        return cur
    tb = cur.shape[0]
    rolled = pltpu.roll(cur, tb - s, 0)
    hr = jnp.where(last, 0.0, pltpu.roll(halo, SUBLANES - s, 0))
    rows = lax.broadcasted_iota(jnp.int32, halo.shape, 0)
    tail = jnp.where(rows >= SUBLANES - s, hr, rolled[tb - SUBLANES:])
    return jnp.concatenate([rolled[:tb - SUBLANES], tail], axis=0)


def _conv_prev(x, halo, w, first, taps):
    acc = None
    for i in range(taps):
        term = w[i:i + 1, :] * _shift_prev(x, halo, taps - 1 - i, first)
        acc = term if acc is None else acc + term
    return acc


def _conv_pre(x, w, b, first, taps):
    cur, prev8, next8 = x
    tail = cur[cur.shape[0] - SUBLANES:]
    return b + _conv_prev(cur, prev8, w, first, taps), b + _conv_prev(next8, tail, w, False, taps)


def _conv_bwd(dpre, dpre_next8, x, w, ctx, taps):
    cur, prev8, _ = x
    dx, dws = None, []
    for i in range(taps):
        term = w[i:i + 1, :] * _shift_next(dpre, dpre_next8, taps - 1 - i, ctx.last)
        dx = term if dx is None else dx + term
        dws.append(jnp.sum(dpre * _shift_prev(cur, prev8, taps - 1 - i, ctx.first), axis=0, keepdims=True))
    return dx, jnp.concatenate(dws, axis=0), jnp.sum(dpre, axis=0, keepdims=True)


def _params(sem):
    return pltpu.CompilerParams(dimension_semantics=sem, vmem_limit_bytes=VMEM_LIMIT_BYTES)


N_CHIPS = N_DEV // 2
OTHER_CHIPS = (4, 2, 6)


class _Hosted:
    def __init__(self, arrays, out_shape, sems, ops):
        self.arrays, self.out_shape, self.sems, self.ops = list(arrays), list(out_shape), list(sems), ops
        self.n = len(self.arrays)
        self.specs = [pl.BlockSpec(memory_space=pl.ANY)] * self.n

    def begin(self, in_refs, out_refs, sem_refs, first):
        start, finish = self.ops(in_refs, out_refs, *sem_refs)
        pl.when(first)(start)
        return finish


NO_EXCHANGE = _Hosted((), (), (), None)


def _peer(k):
    x, y, c = lax.axis_index("x"), lax.axis_index("y"), lax.axis_index("c")
    px = 1 - x if k & 4 else x
    py = 1 - y if k & 2 else y
    pc = 1 - c if k & 1 else c
    return (px, py, pc), 4 * px + 2 * py + pc, 2 * px + py


def _remote(src, dst, send_sems, recv_sems, t, k, dev):
    return pltpu.make_async_remote_copy(src_ref=src, dst_ref=dst, send_sem=send_sems.at[t, k], recv_sem=recv_sems.at[t, k],
                                        device_id=dev, device_id_type=pl.DeviceIdType.MESH)


def direct_exchange(items):
    n = len(items)

    def ops(in_refs, out_refs, send_sems, recv_sems, local_sems):
        _, me, _ = _peer(0)
        part = lambda t, pid: in_refs[t] if items[t][1] == "gather" else in_refs[t].at[pid]

        def copy(t, k, arriving):
            dev, pid, _ = _peer(k)
            return _remote(part(t, pid), out_refs[t].at[pid if arriving else me], send_sems, recv_sems, t, k, dev)

        def own(t):
            return pltpu.make_async_copy(part(t, me), out_refs[t].at[me], local_sems.at[t])

        def start():
            for t in range(n):
                own(t).start()
                for k in range(1, N_DEV):
                    copy(t, k, False).start()

        def finish():
            for t in range(n):
                for k in range(1, N_DEV):
                    copy(t, k, True).wait_recv()
            for t in range(n):
                for k in range(1, N_DEV):
                    copy(t, k, False).wait_send()
                own(t).wait()

        return start, finish

    out_shape = [jax.ShapeDtypeStruct((N_DEV,) + tuple(a.shape if m == "gather" else a.shape[1:]), a.dtype)
                 for a, m in items]
    sems = [pltpu.SemaphoreType.DMA((n, N_DEV)), pltpu.SemaphoreType.DMA((n, N_DEV)), pltpu.SemaphoreType.DMA((n,))]
    return _Hosted([a for a, _ in items], out_shape, sems, ops)


def sibling_exchange(arrays):
    n = len(arrays)

    def ops(in_refs, out_refs, send_sems, recv_sems):
        sib, _, _ = _peer(1)
        c = lax.axis_index("c")
        copy = lambda t, q: _remote(in_refs[t].at[2 * q + (1 - c)], out_refs[t].at[q], send_sems, recv_sems, t, q, sib)

        def start():
            for t in range(n):
                for q in range(N_CHIPS):
                    copy(t, q).start()

        def finish():
            for t in range(n):
                for q in range(N_CHIPS):
                    copy(t, q).wait_recv()
            for t in range(n):
                for q in range(N_CHIPS):
                    copy(t, q).wait_send()

        return start, finish

    out_shape = [jax.ShapeDtypeStruct((N_CHIPS,) + a.shape[1:], a.dtype) for a in arrays]
    sems = [pltpu.SemaphoreType.DMA((n, N_CHIPS)), pltpu.SemaphoreType.DMA((n, N_CHIPS))]
    return _Hosted(arrays, out_shape, sems, ops)


def chip_exchange(arrays):
    n = len(arrays)

    def ops(in_refs, out_refs, send_sems, recv_sems, local_sems):
        _, _, mine = _peer(0)

        def copy(t, k, arriving):
            dev, _, q = _peer(k)
            return _remote(in_refs[t].at[q], out_refs[t].at[q if arriving else mine], send_sems, recv_sems, t, k, dev)

        def own(t):
            return pltpu.make_async_copy(in_refs[t].at[mine], out_refs[t].at[mine], local_sems.at[t])

        def start():
            for t in range(n):
                own(t).start()
                for k in OTHER_CHIPS:
                    copy(t, k, False).start()

        def finish():
            for t in range(n):
                for k in OTHER_CHIPS:
                    copy(t, k, True).wait_recv()
            for t in range(n):
                for k in OTHER_CHIPS:
                    copy(t, k, False).wait_send()
                own(t).wait()

        return start, finish

    out_shape = [jax.ShapeDtypeStruct(a.shape, a.dtype) for a in arrays]
    sems = [pltpu.SemaphoreType.DMA((n, N_DEV)), pltpu.SemaphoreType.DMA((n, N_DEV)), pltpu.SemaphoreType.DMA((n,))]
    return _Hosted(arrays, out_shape, sems, ops)


MATMUL_VMEM_BUDGET = 34 * 1024 * 1024


V7X_MXU_FLOPS = 996e12
V7X_HBM_BYTES_PER_S = 3.4e12
GRID_STEP_S = 0.35e-6


def _tile_sizes(dim, cap):
    return [t for t in range(LANES, min(dim, cap) + 1, LANES) if dim % t == 0] or [dim]


def _matmul_tiles(m, n, k, a_bytes, b_bytes, add_bytes, out_bytes, whole_rows=False):
    best = None
    for tk in _tile_sizes(k, 8192):
        nk = k // tk
        for tn in ([n] if whole_rows else _tile_sizes(n, 2048)):
            for tm in _tile_sizes(m, 2048):
                io = tm * tk * a_bytes + tk * tn * b_bytes
                ends = tm * tn * (add_bytes + out_bytes)
                need = 2 * (io + ends) + tm * tn * 4 * (2 if nk > 1 else 1)
                if need > MATMUL_VMEM_BUDGET:
                    continue
                step = max(2.0 * tm * tn * tk / V7X_MXU_FLOPS, (io + ends / nk) / V7X_HBM_BYTES_PER_S)
                if nk > 1:
                    step += tm * tn * 8 / V7X_HBM_BYTES_PER_S
                cost = (m // tm) * (n // tn) * nk * (step + GRID_STEP_S)
                if best is None or cost < best[0]:
                    best = (cost, tm, tn, tk)
    if best is None:
        raise ValueError((m, n, k))
    return best[1:]


class _Tail:
    def __init__(self, fn, like=(), rows=(), outs=(), n_sums=0):
        self.fn, self.like, self.rows, self.outs, self.n_sums = fn, list(like), list(rows), list(outs), n_sums


def matmul(a, b, name, ta=False, tb=False, add=None, out_dtype=F32, comm=NO_EXCHANGE, tail=None):
    assert not (ta and tb)
    m, k = (a.shape[1], a.shape[0]) if ta else a.shape
    n = b.shape[0] if tb else b.shape[1]
    assert (b.shape[1] if tb else b.shape[0]) == k
    like = ([] if add is None else [add]) + (tail.like if tail else [])
    rows = tail.rows if tail else []
    outs = tail.outs if tail else [out_dtype]
    n_sums = tail.n_sums if tail else 0
    tm, tn, tk = _matmul_tiles(m, n, k, a.dtype.itemsize, b.dtype.itemsize, sum(x.dtype.itemsize for x in like),
                               sum(jnp.dtype(d).itemsize for d in outs), whole_rows=tail is not None)
    nk = k // tk
    grid = (m // tm, n // tn, nk)
    dims = ((0,), (0,)) if ta else (((1,), (1,)) if tb else ((1,), (0,)))
    n_in = 2 + len(like) + len(rows)
    n_out = len(outs) + n_sums
    n_acc = 0 if nk == 1 else 1

    def body(*refs):
        a_ref, b_ref = refs[:2]
        like_refs, row_refs = refs[2:2 + len(like)], refs[2 + len(like):n_in]
        out_refs = refs[n_in + comm.n:n_in + comm.n + n_out]
        ids = [pl.program_id(d) for d in range(3)]
        if comm.n:
            first = functools.reduce(jnp.logical_and, [i == 0 for i in ids])
            last = functools.reduce(jnp.logical_and, [i == g - 1 for i, g in zip(ids, grid)])
            done = comm.begin(refs[n_in:n_in + comm.n], refs[n_in + comm.n + n_out:n_in + 2 * comm.n + n_out],
                              refs[n_in + 2 * comm.n + n_out + n_acc:], first)

        def finish(r):
            if add is not None:
                r = r + like_refs[0][...].astype(F32)
            if tail is None:
                out_refs[0][...] = r.astype(out_dtype)
                return
            vals = tail.fn(r, *[x[...] for x in like_refs[len(like) - len(tail.like):]], *[x[...] for x in row_refs])
            for ref, val in zip(out_refs[:len(outs)], vals):
                ref[...] = val.astype(ref.dtype)
            for ref, val in zip(out_refs[len(outs):], vals[len(outs):]):
                @pl.when(ids[0] == 0)
                def _(ref=ref, val=val):
                    ref[...] = val

                @pl.when(ids[0] != 0)
                def _(ref=ref, val=val):
                    ref[...] += val

        if nk == 1:
            finish(_mm(a_ref[...], b_ref[...], dims))
        else:
            acc = refs[n_in + 2 * comm.n + n_out]

            @pl.when(ids[2] == 0)
            def _():
                acc[...] = jnp.zeros_like(acc)

            acc[...] += _mm(a_ref[...], b_ref[...], dims)

            @pl.when(ids[2] == nk - 1)
            def _():
                finish(acc[...])

        if comm.n:
            pl.when(last)(done)

    a_spec = pl.BlockSpec((tk, tm), lambda i, j, kk: (kk, i)) if ta else pl.BlockSpec((tm, tk), lambda i, j, kk: (i, kk))
    b_spec = pl.BlockSpec((tn, tk), lambda i, j, kk: (j, kk)) if tb else pl.BlockSpec((tk, tn), lambda i, j, kk: (kk, j))
    tile = pl.BlockSpec((tm, tn), lambda i, j, kk: (i, j))
    row = pl.BlockSpec((1, tn), lambda i, j, kk: (0, j))
    sequential = comm.n or n_sums
    res = pl.pallas_call(
        body, name=name,
        grid=grid,
        in_specs=[a_spec, b_spec] + [tile] * len(like) + [row] * len(rows) + comm.specs,
        out_specs=[tile] * len(outs) + [row] * n_sums + comm.specs,
        out_shape=[jax.ShapeDtypeStruct((m, n), d) for d in outs] + [jax.ShapeDtypeStruct((1, n), F32)] * n_sums
        + comm.out_shape,
        scratch_shapes=([] if nk == 1 else [pltpu.VMEM((tm, tn), F32)]) + comm.sems,
        compiler_params=_params(("arbitrary",) * 3 if sequential else ("parallel", "parallel", "arbitrary")),
    )(a, b, *like, *rows, *comm.arrays)
    return res if (comm.n or tail) else res[0]


class _Ctx:
    def __init__(self, first, last):
        self.first = first
        self.last = last


def rowwise(fn, name, rows, seq, tb, ncol, ins, params=(), outs=(), accs=()):
    assert rows % tb == 0 and seq % tb == 0 and tb % 16 == 0
    bps = seq // tb
    nrow = rows // tb
    r8 = tb // SUBLANES
    args, in_specs = [], []
    for arr, w, off, halo in ins:
        args.append(arr)
        in_specs.append(pl.BlockSpec((tb, w), lambda j, i, off=off: (i, off + j)))
        if halo in ("prev", "both"):
            args.append(arr)
            in_specs.append(pl.BlockSpec((SUBLANES, w), lambda j, i, off=off: (jnp.maximum(i * r8 - 1, 0), off + j)))
        if halo in ("next", "both"):
            args.append(arr)
            in_specs.append(pl.BlockSpec(
                (SUBLANES, w), lambda j, i, off=off: (jnp.minimum((i + 1) * r8, rows // SUBLANES - 1), off + j)))
    for arr, w, off in params:
        args.append(arr)
        if w is None:
            in_specs.append(pl.BlockSpec(arr.shape, lambda j, i: (0, 0)))
        else:
            in_specs.append(pl.BlockSpec((arr.shape[0], w), lambda j, i, off=off: (0, off + j)))
    out_shape, out_specs = [], []
    for total, w, off, dt in outs:
        out_shape.append(jax.ShapeDtypeStruct((rows, total), dt))
        out_specs.append(pl.BlockSpec((tb, w), lambda j, i, off=off: (i, off + j)))
    for r, w in accs:
        out_shape.append(jax.ShapeDtypeStruct((r, ncol * w), F32))
        out_specs.append(pl.BlockSpec((r, w), lambda j, i: (0, j)))
    n_out, n_acc = len(outs), len(accs)

    def body(*refs):
        i = pl.program_id(1)
        pos = 0
        vals = []
        for _, _, _, halo in ins:
            cur = refs[pos][...]
            pos += 1
            if halo is None:
                vals.append(cur)
            elif halo == "both":
                vals.append((cur, refs[pos][...], refs[pos + 1][...]))
                pos += 2
            else:
                vals.append((cur, refs[pos][...]))
                pos += 1
        for _ in params:
            vals.append(refs[pos][...])
            pos += 1
        ctx = _Ctx(i % bps == 0, i % bps == bps - 1)
        res = fn(ctx, *vals)
        if not isinstance(res, (tuple, list)):
            res = (res,)
        assert len(res) == n_out + n_acc
        for q in range(n_out):
            refs[pos + q][...] = res[q].astype(refs[pos + q].dtype)
        for q in range(n_acc):
            ref, val = refs[pos + n_out + q], res[n_out + q]

            @pl.when(i == 0)
            def _(ref=ref, val=val):
                ref[...] = val

            @pl.when(i != 0)
            def _(ref=ref, val=val):
                ref[...] += val

    res = pl.pallas_call(
        body, name=name,
        grid=(ncol, nrow),
        in_specs=in_specs,
        out_specs=out_specs,
        out_shape=out_shape,
        compiler_params=_params(("parallel", "arbitrary")),
    )(*args)
    return res


GROUP_W = SSD_K * SSD_P


def _tri(lower):
    r = lax.broadcasted_iota(jnp.int32, (SSD_Q, SSD_Q), 0)
    c = lax.broadcasted_iota(jnp.int32, (SSD_Q, SSD_Q), 1)
    return r >= c if lower else r <= c


def _first_head_lanes():
    return lax.broadcasted_iota(jnp.int32, (1, LANES), 1) < SSD_P


def _column(v, j):
    return v[:, j * LANES:(j + 1) * LANES] if v.shape[-1] == GROUP_W else v


def _per_head(vals):
    first = _first_head_lanes()
    return jnp.concatenate([jnp.where(first, _column(vals[2 * j], j), _column(vals[2 * j + 1], j))
                            for j in range(GROUP_W // LANES)], axis=1)


def _per_head_rows(vals):
    return jnp.concatenate([jnp.broadcast_to(v, (SSD_P, 1)) for v in vals], axis=0)


def _own_columns(slab, k):
    keep = _first_head_lanes() if k % 2 == 0 else jnp.logical_not(_first_head_lanes())
    own = jnp.where(keep, _column(slab, k // 2), 0.0)
    return jnp.concatenate([own, jnp.zeros_like(own)] if k < 2 else [jnp.zeros_like(own), own], axis=1)


def _headsum(prod, g):
    out = None
    for k in range(SSD_K):
        keep = _first_head_lanes() if k % 2 == 0 else jnp.logical_not(_first_head_lanes())
        term = jnp.sum(jnp.where(keep, _column(prod, k // 2), 0.0), axis=1, keepdims=True) * _onehot_row(g * SSD_K + k)
        out = term if out is None else out + term
    return out


def ssd_fwd(xact, dtraw, dt_bias, a_log, d_skip, n_seq, seq, comm=NO_EXCHANGE):
    nc = seq // SSD_Q
    rows = n_seq * seq
    nx = comm.n

    def body(*refs):
        xact_ref, dtraw_ref, bias_ref, alog_ref, dskip_ref = refs[:5]
        y_ref, sin_ref = refs[5 + nx:7 + nx]
        state, cs_s, cst_s, dt_s = refs[7 + 2 * nx:11 + 2 * nx]
        b, c = pl.program_id(0), pl.program_id(1)
        if nx:
            finish = comm.begin(refs[5:5 + nx], refs[7 + nx:7 + 2 * nx], refs[11 + 2 * nx:],
                                jnp.logical_and(b == 0, c == 0))

        @pl.when(c == 0)
        def _():
            state[...] = jnp.zeros_like(state)

        sin_ref[0] = state[...]
        dt = _softplus(dtraw_ref[...] + bias_ref[...])
        a = dt * (-jnp.exp(alog_ref[...]))
        cs = _dot_f32(_tri(True).astype(F32), a)
        cs_s[...] = cs
        cst_s[...] = cs.T
        dt_s[...] = dt
        causal = _tri(True)
        def front(g):
            heads = [g * SSD_K + k for k in range(SSD_K)]
            bg = xact_ref[:, pl.ds(D_INNER + g * SSD_N, SSD_N)]
            cg = xact_ref[:, pl.ds(D_INNER + (SSD_G + g) * SSD_N, SSD_N)]
            xg = xact_ref[:, pl.ds(g * GROUP_W, GROUP_W)]
            cols = [cs_s[:, pl.ds(h, 1)] for h in heads]
            lasts = [cs_s[pl.ds(SSD_Q - 1, 1), pl.ds(h, 1)] for h in heads]
            xdg = xg * _per_head([dt_s[:, pl.ds(h, 1)] for h in heads])
            sg = state[g]
            y = (_per_head([jnp.exp(c_) for c_ in cols]) * _dot_nt(cg, sg)
                 + _per_head([dskip_ref[:, pl.ds(h, 1)] for h in heads]) * xg)
            w = _per_head([jnp.exp(l_ - c_) for l_, c_ in zip(lasts, cols)])
            state[g] = _per_head_rows([jnp.exp(l_) for l_ in lasts]) * sg + _dot_tn(w * xdg, bg)
            return heads, cols, _dot_nt(cg, bg), xdg, y

        def back(g, heads, cols, gm, xdg, y):
            mats = [gm * jnp.exp(jnp.where(causal, cols[k] - cst_s[pl.ds(h, 1), :], NEG)) for k, h in enumerate(heads)]
            y4 = _dot_nn(jnp.concatenate(mats, axis=0), xdg)
            y_ref[:, pl.ds(g * GROUP_W, GROUP_W)] = y + _per_head([y4[k * SSD_Q:(k + 1) * SSD_Q] for k in range(SSD_K)])

        ahead = front(0)
        for g in range(SSD_G):
            cur, ahead = ahead, (front(g + 1) if g + 1 < SSD_G else None)
            back(g, *cur)
        if nx:
            pl.when(jnp.logical_and(b == n_seq - 1, c == nc - 1))(finish)

    vec = pl.BlockSpec((1, LANES), lambda b, c: (0, 0))
    return pl.pallas_call(
        body, name="ssd_fwd",
        grid=(n_seq, nc),
        in_specs=[pl.BlockSpec((SSD_Q, CONV_DIM), lambda b, c: (b * nc + c, 0)),
                  pl.BlockSpec((SSD_Q, LANES), lambda b, c: (b * nc + c, 0)), vec, vec, vec] + comm.specs,
        out_specs=[pl.BlockSpec((SSD_Q, D_INNER), lambda b, c: (b * nc + c, 0)),
                   pl.BlockSpec((1, SSD_G, GROUP_W, SSD_N), lambda b, c: (b * nc + c, 0, 0, 0))] + comm.specs,
        out_shape=[jax.ShapeDtypeStruct((rows, D_INNER), F32),
                   jax.ShapeDtypeStruct((n_seq * nc, SSD_G, GROUP_W, SSD_N), F32)] + comm.out_shape,
        scratch_shapes=[pltpu.VMEM((SSD_G, GROUP_W, SSD_N), F32), pltpu.VMEM((SSD_Q, LANES), F32),
                        pltpu.VMEM((LANES, SSD_Q), F32), pltpu.VMEM((SSD_Q, LANES), F32)] + comm.sems,
        compiler_params=_params(("arbitrary", "arbitrary")),
    )(xact, dtraw, dt_bias, a_log, d_skip, *comm.arrays)


def ssd_bwd(xact, dtraw, dt_bias, a_log, d_skip, sin, dy, n_seq, seq, comm=NO_EXCHANGE):
    nc = seq // SSD_Q
    rows = n_seq * seq
    nx = comm.n

    def body(*refs):
        xact_ref, dtraw_ref, bias_ref, alog_ref, dskip_ref, sin_ref, dy_ref = refs[:7]
        dx_ref, ddt_ref, dbias_ref, dalog_ref, ddskip_ref = refs[7 + nx:12 + nx]
        dstate, cs_s, cst_s, dt_s = refs[12 + 2 * nx:16 + 2 * nx]
        b, c = pl.program_id(0), pl.program_id(1)
        if nx:
            finish = comm.begin(refs[7:7 + nx], refs[12 + nx:12 + 2 * nx], refs[16 + 2 * nx:],
                                jnp.logical_and(b == 0, c == 0))

        @pl.when(c == 0)
        def _():
            dstate[...] = jnp.zeros_like(dstate)

        pre = dtraw_ref[...] + bias_ref[...]
        dt = _softplus(pre)
        a_neg = -jnp.exp(alog_ref[...])
        cs = _dot_f32(_tri(True).astype(F32), dt * a_neg)
        cs_s[...] = cs
        cst_s[...] = cs.T
        dt_s[...] = dt
        causal, anti = _tri(True), _tri(False)
        is_last_row = lax.broadcasted_iota(jnp.int32, (SSD_Q, 1), 0) == SSD_Q - 1
        dcs_cf = jnp.zeros((SSD_Q, LANES), F32)
        dcs_rf = jnp.zeros((LANES, SSD_Q), F32)
        ddt_cf = jnp.zeros((SSD_Q, LANES), F32)
        dd_vec = jnp.zeros((1, LANES), F32)
        dlast_vec = jnp.zeros((1, LANES), F32)
        def front(g):
            heads = [g * SSD_K + k for k in range(SSD_K)]
            v = {"heads": heads}
            bg = v["bg"] = xact_ref[:, pl.ds(D_INNER + g * SSD_N, SSD_N)]
            cg = v["cg"] = xact_ref[:, pl.ds(D_INNER + (SSD_G + g) * SSD_N, SSD_N)]
            xg = v["xg"] = xact_ref[:, pl.ds(g * GROUP_W, GROUP_W)]
            dyg = v["dyg"] = dy_ref[:, pl.ds(g * GROUP_W, GROUP_W)]
            cols = [cs_s[:, pl.ds(h, 1)] for h in heads]
            rws = [cst_s[pl.ds(h, 1), :] for h in heads]
            lasts = [cs_s[pl.ds(SSD_Q - 1, 1), pl.ds(h, 1)] for h in heads]
            e_lasts = v["e_lasts"] = [jnp.exp(l_) for l_ in lasts]
            v["dtg"] = _per_head([dt_s[:, pl.ds(h, 1)] for h in heads])
            v["dskg"] = _per_head([dskip_ref[:, pl.ds(h, 1)] for h in heads])
            e_col = _per_head([jnp.exp(c_) for c_ in cols])
            w = v["w"] = _per_head([jnp.exp(l_ - c_) for l_, c_ in zip(lasts, cols)])
            xdg = v["xdg"] = xg * v["dtg"]
            sg = sin_ref[0, g]
            dsn = dstate[g]
            v["gm"] = _dot_nt(cg, bg)
            gmt = _dot_nt(bg, cg)
            v["y_off"] = e_col * _dot_nt(cg, sg)
            d_cs = e_col * dyg
            v["dcg"] = _dot_nn(d_cs, sg)
            dstate[g] = _dot_tn(d_cs, cg) + _per_head_rows(e_lasts) * dsn
            v["dbg"] = _dot_nn(w * xdg, dsn)
            v["dtt"] = _dot_nt(bg, dsn)
            v["dsn_s"] = dsn * sg
            segs = [cols[k] - rws[k] for k in range(SSD_K)]
            v["decays"] = [jnp.exp(jnp.where(causal, s_, NEG)) for s_ in segs]
            v["dm4"] = _dot_nt(jnp.concatenate([_own_columns(dyg, k) for k in range(SSD_K)], axis=0), xdg)
            v["z4"] = _dot_nn(jnp.concatenate([gmt * jnp.exp(jnp.where(anti, -s_, NEG)) for s_ in segs], axis=0), dyg)
            return v

        def back(g, v, sums):
            dcs_cf, dcs_rf, ddt_cf, dd_vec, dlast_vec = sums
            dxd = v["w"] * v["dtt"] + _per_head([v["z4"][k * SSD_Q:(k + 1) * SSD_Q] for k in range(SSD_K)])
            dw = _headsum(v["dtt"] * v["xdg"] * v["w"], g)
            dcs_cf = dcs_cf + _headsum(v["dyg"] * v["y_off"], g) - dw
            dlast_vec = dlast_vec + jnp.sum(dw, axis=0, keepdims=True)
            dgm = jnp.zeros((SSD_Q, SSD_Q), F32)
            for k, h in enumerate(v["heads"]):
                dm = v["dm4"][k * SSD_Q:(k + 1) * SSD_Q]
                dseg = dm * v["gm"] * v["decays"][k]
                dgm = dgm + dm * v["decays"][k]
                oh_r = _onehot_row(h)
                dcs_cf = dcs_cf + jnp.sum(dseg, axis=1, keepdims=True) * oh_r
                dcs_rf = dcs_rf - _onehot_col(h) * jnp.sum(dseg, axis=0, keepdims=True)
                dlast_vec = dlast_vec + (jnp.sum(v["dsn_s"][k * SSD_P:(k + 1) * SSD_P], keepdims=True)
                                         * v["e_lasts"][k] * oh_r)
            dx_ref[:, pl.ds(g * GROUP_W, GROUP_W)] = dxd * v["dtg"] + v["dskg"] * v["dyg"]
            ddt_cf = ddt_cf + _headsum(dxd * v["xg"], g)
            dd_vec = dd_vec + _headsum(jnp.sum(v["dyg"] * v["xg"], axis=0, keepdims=True), g)
            dx_ref[:, pl.ds(D_INNER + g * SSD_N, SSD_N)] = v["dbg"] + _dot_tn(dgm, v["cg"])
            dx_ref[:, pl.ds(D_INNER + (SSD_G + g) * SSD_N, SSD_N)] = v["dcg"] + _dot_nn(dgm, v["bg"])
            return dcs_cf, dcs_rf, ddt_cf, dd_vec, dlast_vec

        sums = (dcs_cf, dcs_rf, ddt_cf, dd_vec, dlast_vec)
        ahead = front(0)
        for g in range(SSD_G):
            cur, ahead = ahead, (front(g + 1) if g + 1 < SSD_G else None)
            sums = back(g, cur, sums)
        dcs_cf, dcs_rf, ddt_cf, dd_vec, dlast_vec = sums
        dcs = dcs_cf + dcs_rf.T + jnp.where(is_last_row, dlast_vec, 0.0)
        da = _dot_f32(_tri(False).astype(F32), dcs)
        ddt = ddt_cf + da * a_neg
        ddtraw = ddt * _sigmoid(pre)
        ddt_ref[...] = ddtraw.astype(ddt_ref.dtype)
        dbias = jnp.sum(ddtraw, axis=0, keepdims=True)
        dalog = jnp.sum(da * dt, axis=0, keepdims=True) * a_neg
        first_step = jnp.logical_and(b == 0, c == 0)

        @pl.when(first_step)
        def _():
            dbias_ref[...] = dbias
            dalog_ref[...] = dalog
            ddskip_ref[...] = dd_vec

        @pl.when(jnp.logical_not(first_step))
        def _():
            dbias_ref[...] += dbias
            dalog_ref[...] += dalog
            ddskip_ref[...] += dd_vec

        if nx:
            pl.when(jnp.logical_and(b == n_seq - 1, c == nc - 1))(finish)

    def rowblk(b, c):
        return b * nc + (nc - 1 - c)

    vec = pl.BlockSpec((1, LANES), lambda b, c: (0, 0))
    return pl.pallas_call(
        body, name="ssd_bwd",
        grid=(n_seq, nc),
        in_specs=[pl.BlockSpec((SSD_Q, CONV_DIM), lambda b, c: (rowblk(b, c), 0)),
                  pl.BlockSpec((SSD_Q, LANES), lambda b, c: (rowblk(b, c), 0)), vec, vec, vec,
                  pl.BlockSpec((1, SSD_G, GROUP_W, SSD_N), lambda b, c: (rowblk(b, c), 0, 0, 0)),
                  pl.BlockSpec((SSD_Q, D_INNER), lambda b, c: (rowblk(b, c), 0))] + comm.specs,
        out_specs=[pl.BlockSpec((SSD_Q, CONV_DIM), lambda b, c: (rowblk(b, c), 0)),
                   pl.BlockSpec((SSD_Q, LANES), lambda b, c: (rowblk(b, c), 0)), vec, vec, vec] + comm.specs,
        out_shape=[jax.ShapeDtypeStruct((rows, CONV_DIM), F32), jax.ShapeDtypeStruct((rows, LANES), BF16),
                   jax.ShapeDtypeStruct((1, LANES), F32), jax.ShapeDtypeStruct((1, LANES), F32),
                   jax.ShapeDtypeStruct((1, LANES), F32)] + comm.out_shape,
        scratch_shapes=[pltpu.VMEM((SSD_G, GROUP_W, SSD_N), F32), pltpu.VMEM((SSD_Q, LANES), F32),
                        pltpu.VMEM((LANES, SSD_Q), F32), pltpu.VMEM((SSD_Q, LANES), F32)] + comm.sems,
        compiler_params=_params(("arbitrary", "arbitrary")),
    )(xact, dtraw, dt_bias, a_log, d_skip, sin, dy, *comm.arrays)


QKV_W = 3 * ATT_OUT
PAIR_W = 2 * ATT_HD
HEAD_PAIRS = ATT_H // 2
PREP_ROWS = 512


def _by_residue(a, n_seq, seq, dil):
    if dil == 1:
        return a
    return a.reshape(n_seq, seq // dil, dil, a.shape[1]).transpose(0, 2, 1, 3).reshape(a.shape)


def _by_token(a, n_seq, seq, dil):
    if dil == 1:
        return a
    return a.reshape(n_seq, dil, seq // dil, a.shape[1]).transpose(0, 2, 1, 3).reshape(a.shape)


def _head_sums(x, fn):
    lo = jnp.logical_not(lax.broadcasted_iota(jnp.int32, (1, 2 * ATT_HD), 1) >= ATT_HD)
    parts = []
    for p in range(ATT_H // 2):
        slab = x[:, p * 2 * ATT_HD:(p + 1) * 2 * ATT_HD]
        s_lo = fn(jnp.sum(jnp.where(lo, slab, 0.0), axis=1, keepdims=True))
        s_hi = fn(jnp.sum(jnp.where(lo, 0.0, slab), axis=1, keepdims=True))
        parts.append(jnp.where(lo, s_lo, s_hi))
    return jnp.concatenate(parts, axis=1)


def _head_rstd(x):
    return _head_sums(x * x, lambda s: lax.rsqrt(s * (1.0 / ATT_HD) + EPS))


def _head_rms_bwd(x, g_t, dy):
    r = _head_rstd(x)
    xh = x * r
    dyg = dy * g_t
    mean = _head_sums(dyg * xh, lambda s: s * (1.0 / ATT_HD))
    return r * (dyg - xh * mean), jnp.sum(dy * xh, axis=0, keepdims=True)


def qk_prep(qkv, gq_t, gk_t, rows, name):
    tb = min(PREP_ROWS, rows)

    def body(x_ref, gq_ref, gk_ref, o_ref):
        q = x_ref[:, pl.ds(0, ATT_OUT)]
        k = x_ref[:, pl.ds(ATT_OUT, ATT_OUT)]
        o_ref[:, pl.ds(0, ATT_OUT)] = (q * _head_rstd(q) * (gq_ref[...] * ATT_SCALE)).astype(o_ref.dtype)
        o_ref[:, pl.ds(ATT_OUT, ATT_OUT)] = (k * _head_rstd(k) * gk_ref[...]).astype(o_ref.dtype)
        o_ref[:, pl.ds(2 * ATT_OUT, ATT_OUT)] = x_ref[:, pl.ds(2 * ATT_OUT, ATT_OUT)].astype(o_ref.dtype)

    gspec = pl.BlockSpec((1, ATT_OUT), lambda i: (0, 0))
    blk = pl.BlockSpec((tb, QKV_W), lambda i: (i, 0))
    return pl.pallas_call(
        body, name=name,
        grid=(rows // tb,),
        in_specs=[blk, gspec, gspec],
        out_specs=blk,
        out_shape=jax.ShapeDtypeStruct((rows, QKV_W), MXU),
        compiler_params=_params(("parallel",)),
    )(qkv, gq_t, gk_t)


def _lane_hi():
    return lax.broadcasted_iota(jnp.int32, (1, PAIR_W), 1) >= ATT_HD


def _band_mask2(first_valid, query_rows):
    i = lax.broadcasted_iota(jnp.int32, (ATT_BLK, 2 * ATT_BLK), 0)
    j = lax.broadcasted_iota(jnp.int32, (ATT_BLK, 2 * ATT_BLK), 1)
    left = j < ATT_BLK
    right = jnp.logical_not(left)
    if query_rows:
        return jnp.logical_or(jnp.logical_and(jnp.logical_and(left, i <= j), first_valid),
                              jnp.logical_and(right, i >= j - ATT_BLK))
    return jnp.logical_or(jnp.logical_and(left, j >= i),
                          jnp.logical_and(jnp.logical_and(right, j - ATT_BLK <= i), first_valid))


def _only_head(slab, hi):
    keep = _lane_hi() if hi else jnp.logical_not(_lane_hi())
    return jnp.where(keep, slab, jnp.zeros_like(slab))


def attn_fwd(nq, n_seq, seq, dil, name):
    nb = seq // dil // ATT_BLK
    rows = n_seq * seq

    def body(cur_ref, prev_ref, o_ref, lse_ref, s_scr, p_scr):
        n = pl.program_id(1)
        mask = _band_mask2(n > 0, True)
        for h in range(ATT_H):
            sl = pl.ds((h // 2) * PAIR_W, PAIR_W)
            ks = pl.ds(ATT_OUT + (h // 2) * PAIR_W, PAIR_W)
            kcat = jnp.concatenate([prev_ref[:, ks], cur_ref[:, ks]], axis=0)
            s_scr[h] = jnp.where(mask, _dot_nt(_only_head(cur_ref[:, sl], h % 2), kcat), NEG)
        s_all = s_scr[...]
        mx = jnp.max(s_all, axis=2, keepdims=True)
        p_all = jnp.exp(s_all - mx)
        den = jnp.sum(p_all, axis=2, keepdims=True)
        p_scr[...] = p_all.astype(p_scr.dtype)
        inv = 1.0 / den
        lse = mx + jnp.log(den)
        lse_blk = jnp.zeros((ATT_BLK, LANES), F32)
        for h in range(ATT_H):
            lse_blk = lse_blk + lse[h] * _onehot_row(h)
        lse_ref[...] = lse_blk
        for pr in range(HEAD_PAIRS):
            vs = pl.ds(2 * ATT_OUT + pr * PAIR_W, PAIR_W)
            vcat = jnp.concatenate([prev_ref[:, vs], cur_ref[:, vs]], axis=0)
            lo = _dot_nn(p_scr[2 * pr], vcat) * inv[2 * pr]
            hi = _dot_nn(p_scr[2 * pr + 1], vcat) * inv[2 * pr + 1]
            o_ref[:, pl.ds(pr * PAIR_W, PAIR_W)] = jnp.where(_lane_hi(), hi, lo)

    def blk(width, shift):
        if shift:
            return pl.BlockSpec((ATT_BLK, width), lambda s, n: (s * nb + jnp.maximum(n - 1, 0), 0))
        return pl.BlockSpec((ATT_BLK, width), lambda s, n: (s * nb + n, 0))

    return pl.pallas_call(
        body, name=name,
        grid=(n_seq * dil, nb),
        in_specs=[blk(QKV_W, 0), blk(QKV_W, -1)],
        out_specs=[blk(ATT_OUT, 0), blk(LANES, 0)],
        out_shape=[jax.ShapeDtypeStruct((rows, ATT_OUT), F32), jax.ShapeDtypeStruct((rows, LANES), F32)],
        scratch_shapes=[pltpu.VMEM((ATT_H, ATT_BLK, 2 * ATT_BLK), F32), pltpu.VMEM((ATT_H, ATT_BLK, 2 * ATT_BLK), MXU)],
        compiler_params=_params(("parallel", "arbitrary")),
    )(nq, nq)


def attn_bwd(nq, do, lse, wts, rsum, n_seq, seq, dil, name):
    nb = seq // dil // ATT_BLK

    def body(prev_ref, cur_ref, nxt_ref, do_c, do_x, lse_c, lse_x, wt_c, wt_x, rs_c, rs_x, dn_ref):
        n = pl.program_id(1)
        mask_q = _band_mask2(n > 0, True)
        mask_k = _band_mask2(n < nb - 1, False)
        wc, wx = wt_c[...], wt_x[...]
        lse_t = jnp.concatenate([lse_c[...].T, lse_x[...].T], axis=1)
        dl_t = jnp.concatenate([(-wc * rs_c[...]).T, (-wx * rs_x[...]).T], axis=1)
        def operands(pr):
            sl = pl.ds(pr * PAIR_W, PAIR_W)
            ks = pl.ds(ATT_OUT + pr * PAIR_W, PAIR_W)
            vs = pl.ds(2 * ATT_OUT + pr * PAIR_W, PAIR_W)
            he, ho = pl.ds(2 * pr, 1), pl.ds(2 * pr + 1, 1)
            q_c, k_c, v_c = cur_ref[:, sl], cur_ref[:, ks], cur_ref[:, vs]
            dog_c = do_c[:, sl] * jnp.where(_lane_hi(), wt_c[:, ho], wt_c[:, he])
            dog_x = do_x[:, sl] * jnp.where(_lane_hi(), wt_x[:, ho], wt_x[:, he])
            return dict(q_c=q_c, k_c=k_c, v_c=v_c, qcat=jnp.concatenate([q_c, nxt_ref[:, sl]], axis=0),
                        kcat=jnp.concatenate([prev_ref[:, ks], k_c], axis=0),
                        vcat=jnp.concatenate([prev_ref[:, vs], v_c], axis=0),
                        dog=jnp.concatenate([dog_c, dog_x], axis=0).astype(MXU))

        def scores(o, h):
            hi, one = h % 2, pl.ds(h, 1)
            dl_col = -wt_c[:, one] * rs_c[:, one]
            p_q = jnp.exp(jnp.where(mask_q, _dot_nt(_only_head(o["q_c"], hi), o["kcat"]) - lse_c[:, one], NEG))
            ds_q = p_q * (_dot_nt(_only_head(o["dog"][:ATT_BLK], hi), o["vcat"]) + dl_col)
            p_t = jnp.exp(jnp.where(mask_k, _dot_nt(_only_head(o["k_c"], hi), o["qcat"]) - lse_t[h:h + 1, :], NEG))
            ds_t = p_t * (_dot_nt(_only_head(o["v_c"], hi), o["dog"]) + dl_t[h:h + 1, :])
            return ds_q, ds_t, p_t

        ops = [operands(pr) for pr in range(HEAD_PAIRS)]
        ahead = scores(ops[0], 0)
        res = []
        for h in range(ATT_H):
            o = ops[h // 2]
            (ds_q, ds_t, p_t), ahead = ahead, (scores(ops[(h + 1) // 2], h + 1) if h + 1 < ATT_H else None)
            res.append((_dot_nn(ds_q, o["kcat"]), _dot_nn(ds_t, o["qcat"]), _dot_nn(p_t, o["dog"])))
            if h % 2:
                for t, first in enumerate((0, ATT_OUT, 2 * ATT_OUT)):
                    dn_ref[:, pl.ds(first + (h // 2) * PAIR_W, PAIR_W)] = jnp.where(_lane_hi(), res[h][t], res[h - 1][t])

    def at(shift, width):
        if shift < 0:
            return pl.BlockSpec((ATT_BLK, width), lambda s, n: (s * nb + jnp.maximum(n - 1, 0), 0))
        if shift > 0:
            return pl.BlockSpec((ATT_BLK, width), lambda s, n: (s * nb + jnp.minimum(n + 1, nb - 1), 0))
        return pl.BlockSpec((ATT_BLK, width), lambda s, n: (s * nb + n, 0))

    return pl.pallas_call(
        body, name=name,
        grid=(n_seq * dil, nb),
        in_specs=[at(-1, QKV_W), at(0, QKV_W), at(1, QKV_W), at(0, ATT_OUT), at(1, ATT_OUT),
                  at(0, LANES), at(1, LANES), at(0, LANES), at(1, LANES), at(0, LANES), at(1, LANES)],
        out_specs=at(0, QKV_W),
        out_shape=jax.ShapeDtypeStruct((n_seq * seq, QKV_W), F32),
        compiler_params=_params(("parallel", "arbitrary")),
    )(nq, nq, nq, do, do, lse, lse, wts, wts, rsum, rsum)


def qk_post(qkv, dn, gq_t, gk_t, rows, name):
    tb = min(PREP_ROWS, rows)

    def body(x_ref, dn_ref, gq_ref, gk_ref, o_ref, dgq_ref, dgk_ref):
        i = pl.program_id(0)
        qs, ks, vs = pl.ds(0, ATT_OUT), pl.ds(ATT_OUT, ATT_OUT), pl.ds(2 * ATT_OUT, ATT_OUT)
        dq, dgq = _head_rms_bwd(x_ref[:, qs], gq_ref[...], dn_ref[:, qs] * ATT_SCALE)
        dk, dgk = _head_rms_bwd(x_ref[:, ks], gk_ref[...], dn_ref[:, ks])
        o_ref[:, qs] = dq.astype(o_ref.dtype)
        o_ref[:, ks] = dk.astype(o_ref.dtype)
        o_ref[:, vs] = dn_ref[:, vs].astype(o_ref.dtype)

        @pl.when(i == 0)
        def _():
            dgq_ref[...] = dgq
            dgk_ref[...] = dgk

        @pl.when(i != 0)
        def _():
            dgq_ref[...] += dgq
            dgk_ref[...] += dgk

    gspec = pl.BlockSpec((1, ATT_OUT), lambda i: (0, 0))
    blk = pl.BlockSpec((tb, QKV_W), lambda i: (i, 0))
    return pl.pallas_call(
        body, name=name,
        grid=(rows // tb,),
        in_specs=[blk, blk, gspec, gspec],
        out_specs=[blk, gspec, gspec],
        out_shape=[jax.ShapeDtypeStruct((rows, QKV_W), MXU), jax.ShapeDtypeStruct((1, ATT_OUT), F32),
                   jax.ShapeDtypeStruct((1, ATT_OUT), F32)],
        compiler_params=_params(("arbitrary",)),
    )(qkv, dn, gq_t, gk_t)


def all_gather(arrays, name):
    n = len(arrays)

    def body(*refs):
        in_refs, out_refs = refs[:n], refs[n:2 * n]
        send_sems, recv_sems, local_sems = refs[2 * n:]
        _, me, _ = _peer(0)
        sib, _, _ = _peer(1)

        def first(t, k, arriving):
            dev, pid, _ = _peer(k)
            return _remote(in_refs[t], out_refs[t].at[pid if arriving else me], send_sems, recv_sems, t, k, dev)

        def passed(t, k, arriving):
            slot = out_refs[t].at[_peer(k + 1 if arriving else k)[1]]
            return _remote(slot, slot, send_sems, recv_sems, t, k + 1, sib)

        def own(t):
            return pltpu.make_async_copy(in_refs[t], out_refs[t].at[me], local_sems.at[t])

        for t in range(n):
            own(t).start()
            for k in (1,) + OTHER_CHIPS:
                first(t, k, False).start()
        for t in range(n):
            for k in OTHER_CHIPS:
                first(t, k, True).wait_recv()
                passed(t, k, False).start()
        for t in range(n):
            first(t, 1, True).wait_recv()
            for k in OTHER_CHIPS:
                passed(t, k, True).wait_recv()
        for t in range(n):
            for k in (1,) + OTHER_CHIPS:
                first(t, k, False).wait_send()
            for k in OTHER_CHIPS:
                passed(t, k, False).wait_send()
            own(t).wait()

    anyspec = pl.BlockSpec(memory_space=pl.ANY)
    return pl.pallas_call(
        body, name=name,
        in_specs=[anyspec] * n,
        out_specs=[anyspec] * n,
        out_shape=[jax.ShapeDtypeStruct((N_DEV,) + tuple(a.shape), a.dtype) for a in arrays],
        scratch_shapes=[pltpu.SemaphoreType.DMA((n, N_DEV)), pltpu.SemaphoreType.DMA((n, N_DEV)),
                        pltpu.SemaphoreType.DMA((n,))],
    )(*arrays)


def pair_add(a, b, name):
    _, r, c = a.shape
    rb = r if r <= 512 else (128 if c > 1024 else 256)
    assert r % rb == 0

    def body(a_ref, b_ref, o_ref):
        o_ref[...] = (a_ref[...].astype(F32) + b_ref[...].astype(F32)).astype(o_ref.dtype)

    blk = pl.BlockSpec((1, rb, c), lambda q, i: (q, i, 0))
    return pl.pallas_call(
        body, name=name,
        grid=(N_CHIPS, r // rb),
        in_specs=[blk, blk],
        out_specs=blk,
        out_shape=jax.ShapeDtypeStruct(a.shape, a.dtype),
        compiler_params=_params(("parallel", "parallel")),
    )(a, b)


def adamw(parts, w, m, v, name):
    r, c = w.shape[-2:]
    n_parts = parts.shape[0]
    rb = r if r <= 512 else (128 if c > 1024 else 256)
    assert r % rb == 0

    def body(p_ref, w_ref, m_ref, v_ref, g_out, d_out, m_out, v_out):
        g = p_ref[0].astype(F32)
        for i in range(1, n_parts):
            g = g + p_ref[i].astype(F32)
        m_new = ADAM_B1 * m_ref[...] + (1.0 - ADAM_B1) * g
        v_new = ADAM_B2 * v_ref[...] + (1.0 - ADAM_B2) * (g * g)
        m_hat = m_new / (1.0 - ADAM_B1 ** ADAM_STEP)
        v_hat = v_new / (1.0 - ADAM_B2 ** ADAM_STEP)
        g_out[...] = g
        d_out[...] = -ADAM_LR * (m_hat / (jnp.sqrt(v_hat) + ADAM_EPS) + ADAM_WD * w_ref[...])
        m_out[...] = m_new
        v_out[...] = v_new

    if w.ndim == 3:
        blk = pl.BlockSpec((None, rb, c), lambda i: (0, i, 0))
    else:
        blk = pl.BlockSpec((rb, c), lambda i: (i, 0))
    return pl.pallas_call(
        body, name=name,
        grid=(r // rb,),
        in_specs=[pl.BlockSpec((n_parts, rb, c), lambda i: (0, i, 0)), blk, blk, blk],
        out_specs=[blk] * 4,
        out_shape=[jax.ShapeDtypeStruct(w.shape, F32)] * 4,
        compiler_params=_params(("parallel",)),
    )(parts, w, m, v)


def _pad_lanes(vec, n=LANES):
    return jnp.pad(vec, ((0, 0), (0, n - vec.shape[1])))


COL_SHARDED = ("w_in", "ssd_conv_w", "w_attn_proj", "w_up", "ffn_conv_w")
MATRICES = ("w_in", "w_attn_proj", "w_up", "w_ssd_proj", "w_out", "w_down")
LATE = ("w_ssd_proj", "w_attn_proj", "w_out", "w_up", "ffn_conv_w", "w_down")


def _narrow(name, a):
    return a.astype(MXU) if name in MATRICES else a


def _from_gathered(name, g):
    if name in COL_SHARDED:
        return jnp.transpose(g, (1, 0, 2)).reshape(g.shape[1], N_DEV * g.shape[2])
    return g.reshape(N_DEV * g.shape[1], g.shape[2])


def _to_slabs(name, g):
    if name in COL_SHARDED:
        return jnp.transpose(g.reshape(g.shape[0], N_DEV, g.shape[1] // N_DEV), (1, 0, 2))
    return g.reshape(N_DEV, g.shape[0] // N_DEV, g.shape[1])


def _columns(m, a, b):
    if m.ndim == 2:
        return m[:, a:b]
    c = m.shape[2]
    cuts = [m[j][:, max(a - j * c, 0):min(b - j * c, c)] for j in range(a // c, (b - 1) // c + 1)]
    return cuts[0] if len(cuts) == 1 else jnp.concatenate(cuts, axis=1)


def _column_shards(pieces, c):
    shards = []
    for j in range(N_DEV):
        cuts = []
        for start, arr in pieces:
            lo, hi = max(j * c - start, 0), min((j + 1) * c - start, arr.shape[1])
            if lo < hi:
                cuts.append(arr[:, lo:hi])
        shards.append(cuts[0] if len(cuts) == 1 else jnp.concatenate(cuts, axis=1))
    return jnp.stack(shards)


def local_step(x, target, w, late=None):
    n_seq, seq, _ = x.shape
    rows = n_seq * seq
    x = x.reshape(rows, D_MODEL)
    target = target.reshape(rows, D_MODEL)
    mx = lambda a: a.astype(MXU)

    splits = [sum(IN_WIDTHS[:i]) for i in range(len(IN_WIDTHS) + 1)]
    w_in = w["w_in"]
    part = lambda i: _columns(w_in, splits[i], splits[i + 1])
    w_z, w_xbc, w_gs, w_ga = mx(part(0)), mx(part(1)), mx(part(6)), mx(part(7))
    w_dt = mx(_pad_lanes(part(2)))
    head_group = lambda t, g: (splits[3 + t] + g * ATT_OUT, splits[3 + t] + (g + 1) * ATT_OUT)
    w_qkv = [mx(jnp.concatenate([_columns(w_in, *head_group(t, g)) for t in range(3)], axis=1))
             for g in range(ATT_GROUPS)]
    conv_w, conv_b, fconv_b = w["ssd_conv_w"], w["ssd_conv_b"], w["ffn_conv_b"]
    dt_bias, a_log, d_skip = _pad_lanes(w["dt_bias"]), _pad_lanes(w["a_log"]), _pad_lanes(w["d_skip"])
    g1, g2, gn, gq, gk = w["norm1_g"], w["norm2_g"], w["ssd_norm_g"], w["q_norm_g"], w["k_norm_g"]

    tb = min(512, seq)
    tbm = min(256, seq)
    cw = 1024
    rw = lambda fn, name, ncol, ins, params=(), outs=(), accs=(), tb_=tb: rowwise(
        fn, name, rows, seq, tb_, ncol, ins, params, outs, accs)

    (h,) = rw(lambda ctx, xv, g: _rms_fwd(xv, g), "rms1_fwd", 1, [(x, D_MODEL, 0, None)], [(g1, None, 0)],
              [(D_MODEL, D_MODEL, 0, MXU)])
    z = matmul(h, w_z, "mm_z")
    xbc = matmul(h, w_xbc, "mm_xbc")
    dtraw = matmul(h, w_dt, "mm_dt")
    by_residue = lambda a, g: _by_residue(a, n_seq, seq, ATT_DILATIONS[g])
    by_token = lambda a, g: _by_token(a, n_seq, seq, ATT_DILATIONS[g])
    h_res = [by_residue(h, g) for g in range(ATT_GROUPS)]
    qkv = [matmul(h_res[g], w_qkv[g], f"mm_qkv{g}") for g in range(ATT_GROUPS)]
    gs = matmul(h, w_gs, "mm_gs")
    ga = matmul(h, w_ga, "mm_ga")

    def conv_silu(ctx, xh, wv, bv):
        return _silu(bv + _conv_prev(xh[0], xh[1], wv, ctx.first, SSD_CONV))

    (xact,) = rw(conv_silu, "ssd_conv_fwd", CONV_DIM // cw, [(xbc, cw, 0, "prev")],
                 [(conv_w, cw, 0), (conv_b, cw, 0)], [(CONV_DIM, cw, 0, F32)])
    if late is None:
        y, sin = ssd_fwd(xact, dtraw, dt_bias, a_log, d_skip, n_seq, seq)
    else:
        y, sin, *gathered = ssd_fwd(xact, dtraw, dt_bias, a_log, d_skip, n_seq, seq,
                                    comm=direct_exchange([(late[n], "gather") for n in LATE]))
        w = {**w, **{n: g if n == "w_up" else _from_gathered(n, g) for n, g in zip(LATE, gathered)}}
    w_sp, w_ap, w_o, w_d = mx(w["w_ssd_proj"]), mx(w["w_attn_proj"]), mx(w["w_out"]), mx(w["w_down"])
    w_ug, w_uv = mx(_columns(w["w_up"], 0, D_FF)), mx(_columns(w["w_up"], D_FF, 2 * D_FF))
    fconv_w = w["ffn_conv_w"]

    def gated_norm(ctx, yv, zv, g):
        yz = yv * _silu(zv)
        return jnp.concatenate([_rms_fwd(yz[:, i:i + NORM_GROUP], g[:, i:i + NORM_GROUP])
                                for i in range(0, cw, NORM_GROUP)], axis=1)

    (y_ssd,) = rw(gated_norm, "ssd_post_fwd", D_INNER // cw, [(y, cw, 0, None), (z, cw, 0, None)], [(gn, cw, 0)],
                  [(D_INNER, cw, 0, MXU)])

    gq_t, gk_t = jnp.tile(gq, (1, ATT_H)), jnp.tile(gk, (1, ATT_H))
    nq = [qk_prep(qkv[g], gq_t, gk_t, rows, f"qk_prep{g}") for g in range(ATT_GROUPS)]
    att = [attn_fwd(nq[g], n_seq, seq, ATT_DILATIONS[g], f"attn_fwd{g}") for g in range(ATT_GROUPS)]

    def combine(ctx, o0, o1, o2, l0, l1, l2):
        mxl = jnp.maximum(jnp.maximum(l0, l1), l2)
        e = [jnp.exp(l - mxl) for l in (l0, l1, l2)]
        inv = 1.0 / (e[0] + e[1] + e[2])
        ws = [ei * inv for ei in e]
        out = sum(_expand_heads(wi) * oi for wi, oi in zip(ws, (o0, o1, o2)))
        return (out, *ws)

    y_attn, wt0, wt1, wt2 = rw(
        combine, "attn_combine", 1,
        [(by_token(att[g][0], g), ATT_OUT, 0, None) for g in range(3)]
        + [(by_token(att[g][1], g), LANES, 0, None) for g in range(3)], [],
        [(ATT_OUT, ATT_OUT, 0, F32)] + [(LANES, LANES, 0, F32)] * 3)
    wts = (wt0, wt1, wt2)

    ps = matmul(y_ssd, w_sp, "mm_ssd_proj")
    pa = matmul(y_attn, w_ap, "mm_attn_proj")
    (merged,) = rw(lambda ctx, a, b, c, d: _sigmoid(c) * a + _sigmoid(d) * b, "merge_fwd", D_MODEL // cw,
                   [(ps, cw, 0, None), (pa, cw, 0, None), (gs, cw, 0, None), (ga, cw, 0, None)], [],
                   [(D_MODEL, cw, 0, MXU)])
    x1, h2 = matmul(merged, w_o, "mm_out", add=x,
                    tail=_Tail(lambda r, g: (r, _rms_fwd(r, g)), rows=[g2], outs=[F32, MXU]))
    up_g = matmul(h2, w_ug, "mm_up_g")
    up_v = matmul(h2, w_uv, "mm_up_v")
    fw = D_FF // 2
    nfc = D_FF // fw

    def mlp_act(ctx, ug, uv, wg, wv, bg, bv):
        cg = bg + _conv_prev(ug[0], ug[1], wg, ctx.first, FFN_CONV)
        cv = bv + _conv_prev(uv[0], uv[1], wv, ctx.first, FFN_CONV)
        return _silu(cg) * cv

    (act,) = rw(mlp_act, "mlp_act_fwd", nfc, [(up_g, fw, 0, "prev"), (up_v, fw, 0, "prev")],
                [(fconv_w, fw, 0), (fconv_w, fw, nfc), (fconv_b, fw, 0), (fconv_b, fw, nfc)], [(D_FF, fw, 0, MXU)],
                tb_=tbm)
    def loss_tail(out, tv):
        d = out - tv
        g = d * (1.0 / D_MODEL)
        return g, g, jnp.sum(d * d, axis=0, keepdims=True)

    dx2, dx2_m, sq = matmul(act, w_d, "mm_down", add=x1,
                            tail=_Tail(loss_tail, like=[target], outs=[F32, MXU], n_sums=1))

    grads = {}
    dact = matmul(dx2_m, w_d, "mm_d_act", tb=True)
    grads["w_down"] = matmul(act, dx2_m, "mm_dw_down", ta=True, out_dtype=MXU)

    def mlp_bwd(ctx, da, ug, uv, wg, wv, bg, bv):
        cg, cg_n = _conv_pre(ug, wg, bg, ctx.first, FFN_CONV)
        cv, cv_n = _conv_pre(uv, wv, bv, ctx.first, FFN_CONV)
        da_c, da_n = da
        dup_g_, dwg, dbg = _conv_bwd(da_c * cv * _silu_grad(cg), da_n * cv_n * _silu_grad(cg_n), ug, wg, ctx, FFN_CONV)
        dup_v_, dwv, dbv = _conv_bwd(da_c * _silu(cg), da_n * _silu(cg_n), uv, wv, ctx, FFN_CONV)
        return dup_g_, dup_v_, dwg, dbg, dwv, dbv

    dup_g, dup_v, dfw_g, dfb_g, dfw_v, dfb_v = rw(
        mlp_bwd, "mlp_bwd", nfc, [(dact, fw, 0, "next"), (up_g, fw, 0, "both"), (up_v, fw, 0, "both")],
        [(fconv_w, fw, 0), (fconv_w, fw, nfc), (fconv_b, fw, 0), (fconv_b, fw, nfc)],
        [(D_FF, fw, 0, MXU), (D_FF, fw, 0, MXU)], [(FFN_CONV, fw), (1, fw), (FFN_CONV, fw), (1, fw)], tb_=tbm)
    grads["ffn_conv_w"] = jnp.concatenate([dfw_g, dfw_v], axis=1)
    grads["ffn_conv_b"] = jnp.concatenate([dfb_g, dfb_v], axis=1)
    dh2 = matmul(dup_g, w_ug, "mm_dh2_g", tb=True)
    dh2 = matmul(dup_v, w_uv, "mm_dh2_v", tb=True, add=dh2)
    dw_up = [(0, matmul(h2, dup_g, "mm_dw_up_g", ta=True, out_dtype=MXU)),
             (D_FF, matmul(h2, dup_v, "mm_dw_up_v", ta=True, out_dtype=MXU))]
    if w["w_up"].ndim == 3:
        grads["w_up"] = _column_shards(dw_up, w["w_up"].shape[2])
    else:
        grads["w_up"] = jnp.concatenate([p for _, p in dw_up], axis=1)

    def rms_bwd_fn(ctx, xv, dh_, dres, g):
        dxv, dg = _rms_bwd(xv, g, dh_)
        return dres + dxv, dg

    def rms_bwd_fn2(ctx, xv, dh_, dres, g):
        dxv, dg = rms_bwd_fn(ctx, xv, dh_, dres, g)
        return dxv, dxv, dg

    dx1, dx1_m, grads["norm2_g"] = rw(
        rms_bwd_fn2, "rms2_bwd", 1, [(x1, D_MODEL, 0, None), (dh2, D_MODEL, 0, None), (dx2, D_MODEL, 0, None)],
        [(g2, None, 0)], [(D_MODEL, D_MODEL, 0, F32), (D_MODEL, D_MODEL, 0, MXU)], [(1, D_MODEL)])

    dmerged = matmul(dx1_m, w_o, "mm_d_merged", tb=True)
    grads["w_out"] = matmul(merged, dx1_m, "mm_dw_out", ta=True, out_dtype=MXU)

    def merge_bwd(ctx, dm, a, b, c, d):
        sc, sd = _sigmoid(c), _sigmoid(d)
        return dm * sc, dm * sd, dm * a * sc * (1.0 - sc), dm * b * sd * (1.0 - sd)

    dps, dpa, dgs, dga = rw(merge_bwd, "merge_bwd", D_MODEL // cw,
                            [(dmerged, cw, 0, None), (ps, cw, 0, None), (pa, cw, 0, None), (gs, cw, 0, None),
                             (ga, cw, 0, None)], [], [(D_MODEL, cw, 0, MXU)] * 4)
    dy_ssd = matmul(dps, w_sp, "mm_d_y_ssd", tb=True)
    grads["w_ssd_proj"] = matmul(y_ssd, dps, "mm_dw_ssd_proj", ta=True, out_dtype=MXU)
    dy_attn = matmul(dpa, w_ap, "mm_d_y_attn", tb=True)
    grads["w_attn_proj"] = matmul(y_attn, dpa, "mm_dw_attn_proj", ta=True, out_dtype=MXU)

    (rsum,) = rw(lambda ctx, a, b: _reduce_heads(a * b), "attn_rsum", 1,
                 [(dy_attn, ATT_OUT, 0, None), (y_attn, ATT_OUT, 0, None)], [], [(LANES, LANES, 0, F32)])
    dqkv, dgq, dgk = [], 0.0, 0.0
    for g in range(ATT_GROUPS):
        dn = attn_bwd(nq[g], by_residue(dy_attn, g), att[g][1], by_residue(wts[g], g), by_residue(rsum, g), n_seq, seq,
                      ATT_DILATIONS[g], f"attn_bwd{g}")
        d_, a_, b_ = qk_post(qkv[g], dn, gq_t, gk_t, rows, f"qk_post{g}")
        dqkv.append(d_)
        dgq, dgk = dgq + a_, dgk + b_
    per_head = lambda v: jnp.sum(v.reshape(ATT_H, ATT_HD), axis=0, keepdims=True)
    grads["q_norm_g"], grads["k_norm_g"] = per_head(dgq), per_head(dgk)

    def gated_norm_bwd(ctx, dyn, yv, zv, g):
        sz = _silu(zv)
        yz = yv * sz
        dyz, dgs_ = [], []
        for i in range(0, cw, NORM_GROUP):
            a, b = _rms_bwd(yz[:, i:i + NORM_GROUP], g[:, i:i + NORM_GROUP], dyn[:, i:i + NORM_GROUP])
            dyz.append(a)
            dgs_.append(b)
        dyz = jnp.concatenate(dyz, axis=1)
        return dyz * sz, dyz * yv * _silu_grad(zv), jnp.concatenate(dgs_, axis=1)

    dy, dz, grads["ssd_norm_g"] = rw(gated_norm_bwd, "ssd_post_bwd", D_INNER // cw,
                                     [(dy_ssd, cw, 0, None), (y, cw, 0, None), (z, cw, 0, None)], [(gn, cw, 0)],
                                     [(D_INNER, cw, 0, F32), (D_INNER, cw, 0, MXU)], [(1, cw)])
    if late is None:
        dxact, ddt, dbias, dalog, ddskip = ssd_bwd(xact, dtraw, dt_bias, a_log, d_skip, sin, dy, n_seq, seq)
    else:
        dxact, ddt, dbias, dalog, ddskip, *parts = ssd_bwd(
            xact, dtraw, dt_bias, a_log, d_skip, sin, dy, n_seq, seq,
            comm=direct_exchange([(grads[n] if n == "w_up" else _to_slabs(n, _narrow(n, grads[n])), "scatter")
                                  for n in LATE]))
        grads.update(zip(LATE, parts))
    grads["dt_bias"], grads["a_log"], grads["d_skip"] = dbias[:, :SSD_H], dalog[:, :SSD_H], ddskip[:, :SSD_H]

    def conv_silu_bwd(ctx, dxa, xin, wv, bv):
        pre, pre_n = _conv_pre(xin, wv, bv, ctx.first, SSD_CONV)
        return _conv_bwd(dxa[0] * _silu_grad(pre), dxa[1] * _silu_grad(pre_n), xin, wv, ctx, SSD_CONV)

    dxbc, grads["ssd_conv_w"], grads["ssd_conv_b"] = rw(
        conv_silu_bwd, "ssd_conv_bwd", CONV_DIM // cw, [(dxact, cw, 0, "next"), (xbc, cw, 0, "both")],
        [(conv_w, cw, 0), (conv_b, cw, 0)], [(CONV_DIM, cw, 0, MXU)], [(SSD_CONV, cw), (1, cw)])

    pieces = [(d_, d_, h, w_, tag) for d_, w_, tag in
              ((dz, w_z, "z"), (dxbc, w_xbc, "xbc"), (ddt, w_dt, "dt"), (dgs, w_gs, "gs"), (dga, w_ga, "ga"))]
    pieces += [(by_token(dqkv[g], g), dqkv[g], h_res[g], w_qkv[g], f"qkv{g}") for g in range(ATT_GROUPS)]
    dws = {tag: matmul(h_in, dpart_h, f"mm_dw_{tag}", ta=True, out_dtype=MXU) for _, dpart_h, h_in, _, tag in pieces}
    dw_in = [(splits[0], dws["z"]), (splits[1], dws["xbc"]), (splits[2], dws["dt"][:, :SSD_H])]
    dw_in += [(head_group(t, g)[0], dws[f"qkv{g}"][:, t * ATT_OUT:(t + 1) * ATT_OUT])
              for t in range(3) for g in range(ATT_GROUPS)]
    dw_in += [(splits[6], dws["gs"]), (splits[7], dws["ga"])]
    if w_in.ndim == 3:
        grads["w_in"] = _column_shards(dw_in, w_in.shape[2])
    else:
        grads["w_in"] = jnp.concatenate([p for _, p in dw_in], axis=1)
    dh = None
    for idx, (dpart, _, _, wpart, tag) in enumerate(pieces):
        comm = NO_EXCHANGE
        if late is not None and idx == 0:
            slabs = [grads[n] if n == "w_in" else _to_slabs(n, _narrow(n, grads[n])) for n in EARLY]
            comm = sibling_exchange(slabs)
        if late is not None and idx == 1:
            core = lax.axis_index("c")
            own = [lax.dynamic_index_in_dim(s.reshape((N_CHIPS, 2) + s.shape[1:]), core, axis=1, keepdims=False)
                   for s in slabs]
            comm = chip_exchange([pair_add(a_, b_, f"rs_add_{n}") for n, a_, b_ in zip(EARLY, own, arrived)])
        dh = matmul(dpart, wpart, f"mm_dh_{tag}", tb=True, add=dh, comm=comm)
        if comm.n:
            dh, *arrived = dh
            if idx == 1:
                grads.update(zip(EARLY, arrived))
    grad_x, grads["norm1_g"] = rw(rms_bwd_fn, "rms1_bwd", 1,
                                  [(x, D_MODEL, 0, None), (dh, D_MODEL, 0, None), (dx1, D_MODEL, 0, None)],
                                  [(g1, None, 0)], [(D_MODEL, D_MODEL, 0, F32)], [(1, D_MODEL)])
    return sq, grad_x.reshape(n_seq, seq, D_MODEL), grads


EARLY = ("w_in", "ssd_conv_w")
REPLICATED = ("norm1_g", "ssd_conv_b", "dt_bias", "a_log", "d_skip", "ssd_norm_g", "q_norm_g", "k_norm_g",
              "norm2_g", "ffn_conv_b")
WEIGHTS = ("norm1_g", "w_in", "ssd_conv_w", "ssd_conv_b", "dt_bias", "a_log", "d_skip", "ssd_norm_g", "w_ssd_proj",
           "q_norm_g", "k_norm_g", "w_attn_proj", "w_out", "norm2_g", "w_up", "ffn_conv_w", "ffn_conv_b", "w_down")
PACK_ROWS, PACK_COLS = 8, 2048


def _pack(vals):
    flat = jnp.concatenate([vals[n].reshape(-1) for n in REPLICATED])
    return jnp.pad(flat, (0, PACK_ROWS * PACK_COLS - flat.shape[0])).reshape(PACK_ROWS, PACK_COLS)


def _unpack(packed, like):
    flat = packed.reshape(-1)
    out, pos = {}, 0
    for n in REPLICATED:
        size = like[n].size
        out[n] = flat[pos:pos + size].reshape(like[n].shape)
        pos += size
    return out


def step(x, target, w_raw, m_raw, v_raw):
    wsh = {n: a[0] if a.ndim == 3 else a for n, a in w_raw.items()}
    gathered = all_gather([_narrow(n, wsh[n]) for n in EARLY], "ag_weights")
    full = {n: wsh[n] for n in REPLICATED}
    full.update({n: g if n == "w_in" else _from_gathered(n, g) for n, g in zip(EARLY, gathered)})

    sq, grad_x, grads = local_step(x, target, full, late={n: _narrow(n, wsh[n]) for n in LATE})

    (small,) = all_gather([_pack({n: grads[n] for n in REPLICATED})], "ag_small")

    out_g, out_d, out_m, out_v = {}, {}, {}, {}
    for n in EARLY + LATE:
        out_g[n], out_d[n], out_m[n], out_v[n] = adamw(grads[n], w_raw[n], m_raw[n], v_raw[n], f"adamw_{n}")
    pk = adamw(small, _pack(w_raw), _pack(m_raw), _pack(v_raw), "adamw_small")
    for dst, packed in zip((out_g, out_d, out_m, out_v), pk):
        dst.update(_unpack(packed, w_raw))
    loss = lax.psum(0.5 * jnp.sum(sq) / D_MODEL, ("x", "y", "c"))
    return loss, grad_x, out_g, out_d, out_m, out_v


def kernel(x, norm1_g, w_in, ssd_conv_w, ssd_conv_b, dt_bias, a_log, d_skip, ssd_norm_g, w_ssd_proj, q_norm_g, k_norm_g, w_attn_proj, w_out, norm2_g, w_up, ffn_conv_w, ffn_conv_b, w_down, loss_target, m_norm1_g, m_w_in, m_ssd_conv_w, m_ssd_conv_b, m_dt_bias, m_a_log, m_d_skip, m_ssd_norm_g, m_w_ssd_proj, m_q_norm_g, m_k_norm_g, m_w_attn_proj, m_w_out, m_norm2_g, m_w_up, m_ffn_conv_w, m_ffn_conv_b, m_w_down, v_norm1_g, v_w_in, v_ssd_conv_w, v_ssd_conv_b, v_dt_bias, v_a_log, v_d_skip, v_ssd_norm_g, v_w_ssd_proj, v_q_norm_g, v_k_norm_g, v_w_attn_proj, v_w_out, v_norm2_g, v_w_up, v_ffn_conv_w, v_ffn_conv_b, v_w_down):
    ws = (norm1_g, w_in, ssd_conv_w, ssd_conv_b, dt_bias, a_log, d_skip, ssd_norm_g, w_ssd_proj, q_norm_g, k_norm_g,
          w_attn_proj, w_out, norm2_g, w_up, ffn_conv_w, ffn_conv_b, w_down)
    ms = (m_norm1_g, m_w_in, m_ssd_conv_w, m_ssd_conv_b, m_dt_bias, m_a_log, m_d_skip, m_ssd_norm_g, m_w_ssd_proj,
          m_q_norm_g, m_k_norm_g, m_w_attn_proj, m_w_out, m_norm2_g, m_w_up, m_ffn_conv_w, m_ffn_conv_b, m_w_down)
    vs = (v_norm1_g, v_w_in, v_ssd_conv_w, v_ssd_conv_b, v_dt_bias, v_a_log, v_d_skip, v_ssd_norm_g, v_w_ssd_proj,
          v_q_norm_g, v_k_norm_g, v_w_attn_proj, v_w_out, v_norm2_g, v_w_up, v_ffn_conv_w, v_ffn_conv_b, v_w_down)
    loss, grad_x, g, d, m, v = step(x, loss_target, dict(zip(WEIGHTS, ws)), dict(zip(WEIGHTS, ms)), dict(zip(WEIGHTS, vs)))
    ordered = lambda dct: [dct[n] for n in WEIGHTS]
    return (loss, grad_x, *ordered(g), *ordered(d), *ordered(m), *ordered(v))
```

```python
import functools

import jax
import jax.numpy as jnp
from jax import lax
from jax.experimental import pallas as pl
from jax.experimental.pallas import tpu as pltpu

F32 = jnp.float32
BF16 = jnp.bfloat16
MXU = jnp.bfloat16
HIGHEST = lax.Precision.HIGHEST
VMEM_LIMIT_BYTES = 48 * 1024 * 1024
SUBLANES = 8
LANES = 128
N_DEV = 8

D_MODEL = 1024
D_INNER = 2048
SSD_P = 64
SSD_H = 32
SSD_G = 8
SSD_K = SSD_H // SSD_G
SSD_N = 128
SSD_Q = 128
SSD_CONV = 4
CONV_DIM = D_INNER + 2 * SSD_G * SSD_N
NORM_GROUP = D_INNER // SSD_G
ATT_GROUPS = 3
ATT_H = 8
ATT_HD = 64
ATT_BLK = 128
ATT_OUT = ATT_H * ATT_HD
ATT_DILATIONS = (1, 4, 16)
ATT_SCALE = ATT_HD ** -0.5
D_FF = 2816
FFN_CONV = 3
EPS = 1e-6
NEG = -1e30
IN_WIDTHS = (D_INNER, CONV_DIM, SSD_H, 3 * ATT_OUT, 3 * ATT_OUT, 3 * ATT_OUT, D_MODEL, D_MODEL)

ADAM_LR = 0.001
ADAM_B1 = 0.9
ADAM_B2 = 0.999
ADAM_EPS = 1e-08
ADAM_WD = 0.01
ADAM_STEP = 10


def _mm(a, b, dims):
    return lax.dot_general(a.astype(MXU), b.astype(MXU), (dims, ((), ())), preferred_element_type=F32)


def _dot_nn(a, b):
    return _mm(a, b, ((1,), (0,)))


def _dot_nt(a, b):
    return _mm(a, b, ((1,), (1,)))


def _dot_tn(a, b):
    return _mm(a, b, ((0,), (0,)))


def _dot_f32(a, b):
    return lax.dot_general(a, b, (((1,), (0,)), ((), ())), precision=HIGHEST, preferred_element_type=F32)


def _sigmoid(x):
    return 1.0 / (1.0 + jnp.exp(-x))


def _silu(x):
    return x * _sigmoid(x)


def _silu_grad(x):
    s = _sigmoid(x)
    return s * (1.0 + x * (1.0 - s))


def _softplus(x):
    return jnp.maximum(x, 0.0) + jnp.log(1.0 + jnp.exp(-jnp.abs(x)))


def _rms_fwd(x, g):
    r = lax.rsqrt(jnp.mean(x * x, axis=-1, keepdims=True) + EPS)
    return x * r * g


def _rms_bwd(x, g, dy):
    r = lax.rsqrt(jnp.mean(x * x, axis=-1, keepdims=True) + EPS)
    xh = x * r
    dyg = dy * g
    dx = r * (dyg - xh * jnp.mean(dyg * xh, axis=-1, keepdims=True))
    return dx, jnp.sum(dy * xh, axis=0, keepdims=True)


def _onehot_row(h, n=LANES):
    return (lax.broadcasted_iota(jnp.int32, (1, n), 1) == h).astype(F32)


def _onehot_col(h, n=LANES):
    return (lax.broadcasted_iota(jnp.int32, (n, 1), 0) == h).astype(F32)


def _head_expand_matrix():
    r = lax.broadcasted_iota(jnp.int32, (LANES, ATT_OUT), 0)
    c = lax.broadcasted_iota(jnp.int32, (LANES, ATT_OUT), 1)
    return (c // ATT_HD == r).astype(F32)


def _split_bf16(x, parts):
    out = []
    for _ in range(parts - 1):
        hi = x.astype(BF16).astype(F32)
        out.append(hi)
        x = x - hi
    out.append(x)
    return out


def _expand_heads(w):
    e = _head_expand_matrix()
    return sum(_dot_nn(p, e) for p in _split_bf16(w, 2))


def _reduce_heads(x):
    e = _head_expand_matrix()
    return sum(_dot_nt(p, e) for p in _split_bf16(x, 3))


def _shift_prev(cur, halo, s, first):
    if s == 0:
        return cur
    rolled = pltpu.roll(cur, s, 0)
    hr = jnp.where(first, 0.0, pltpu.roll(halo, s, 0))
    rows = lax.broadcasted_iota(jnp.int32, halo.shape, 0)
    head = jnp.where(rows < s, hr, rolled[:SUBLANES])
    if cur.shape[0] == SUBLANES:
        return head
    return jnp.concatenate([head, rolled[SUBLANES:]], axis=0)


def _shift_next(cur, halo, s, last):
    if s == 0:
        return cur
    tb = cur.shape[0]
    rolled = pltpu.roll(cur, tb - s, 0)
    hr = jnp.where(last, 0.0, pltpu.roll(halo, SUBLANES - s, 0))
    rows = lax.broadcasted_iota(jnp.int32, halo.shape, 0)
    tail = jnp.where(rows >= SUBLANES - s, hr, rolled[tb - SUBLANES:])
    return jnp.concatenate([rolled[:tb - SUBLANES], tail], axis=0)


def _conv_prev(x, halo, w, first, taps):
    acc = None
    for i in range(taps):
        term = w[i:i + 1, :] * _shift_prev(x, halo, taps - 1 - i, first)
        acc = term if acc is None else acc + term
    return acc


def _conv_pre(x, w, b, first, taps):
    cur, prev8, next8 = x
    tail = cur[cur.shape[0] - SUBLANES:]
    return b + _conv_prev(cur, prev8, w, first, taps), b + _conv_prev(next8, tail, w, False, taps)


def _conv_bwd(dpre, dpre_next8, x, w, ctx, taps):
    cur, prev8, _ = x
    dx, dws = None, []
    for i in range(taps):
        term = w[i:i + 1, :] * _shift_next(dpre, dpre_next8, taps - 1 - i, ctx.last)
        dx = term if dx is None else dx + term
        dws.append(jnp.sum(dpre * _shift_prev(cur, prev8, taps - 1 - i, ctx.first), axis=0, keepdims=True))
    return dx, jnp.concatenate(dws, axis=0), jnp.sum(dpre, axis=0, keepdims=True)


def _params(sem):
    return pltpu.CompilerParams(dimension_semantics=sem, vmem_limit_bytes=VMEM_LIMIT_BYTES)


N_CHIPS = N_DEV // 2
OTHER_CHIPS = (4, 2, 6)


class _Hosted:
    def __init__(self, arrays, out_shape, sems, ops):
        self.arrays, self.out_shape, self.sems, self.ops = list(arrays), list(out_shape), list(sems), ops
        self.n = len(self.arrays)
        self.specs = [pl.BlockSpec(memory_space=pl.ANY)] * self.n

    def begin(self, in_refs, out_refs, sem_refs, first):
        start, finish = self.ops(in_refs, out_refs, *sem_refs)
        pl.when(first)(start)
        return finish


NO_EXCHANGE = _Hosted((), (), (), None)


def _peer(k):
    x, y, c = lax.axis_index("x"), lax.axis_index("y"), lax.axis_index("c")
    px = 1 - x if k & 4 else x
    py = 1 - y if k & 2 else y
    pc = 1 - c if k & 1 else c
    return (px, py, pc), 4 * px + 2 * py + pc, 2 * px + py


def _remote(src, dst, send_sems, recv_sems, t, k, dev):
    return pltpu.make_async_remote_copy(src_ref=src, dst_ref=dst, send_sem=send_sems.at[t, k], recv_sem=recv_sems.at[t, k],
                                        device_id=dev, device_id_type=pl.DeviceIdType.MESH)


def direct_exchange(items):
    n = len(items)

    def ops(in_refs, out_refs, send_sems, recv_sems, local_sems):
        _, me, _ = _peer(0)
        part = lambda t, pid: in_refs[t] if items[t][1] == "gather" else in_refs[t].at[pid]

        def copy(t, k, arriving):
            dev, pid, _ = _peer(k)
            return _remote(part(t, pid), out_refs[t].at[pid if arriving else me], send_sems, recv_sems, t, k, dev)

        def own(t):
            return pltpu.make_async_copy(part(t, me), out_refs[t].at[me], local_sems.at[t])

        def start():
            for t in range(n):
                own(t).start()
                for k in range(1, N_DEV):
                    copy(t, k, False).start()

        def finish():
            for t in range(n):
                for k in range(1, N_DEV):
                    copy(t, k, True).wait_recv()
            for t in range(n):
                for k in range(1, N_DEV):
                    copy(t, k, False).wait_send()
                own(t).wait()

        return start, finish

    out_shape = [jax.ShapeDtypeStruct((N_DEV,) + tuple(a.shape if m == "gather" else a.shape[1:]), a.dtype)
                 for a, m in items]
    sems = [pltpu.SemaphoreType.DMA((n, N_DEV)), pltpu.SemaphoreType.DMA((n, N_DEV)), pltpu.SemaphoreType.DMA((n,))]
    return _Hosted([a for a, _ in items], out_shape, sems, ops)


def sibling_exchange(arrays):
    n = len(arrays)

    def ops(in_refs, out_refs, send_sems, recv_sems):
        sib, _, _ = _peer(1)
        c = lax.axis_index("c")
        copy = lambda t, q: _remote(in_refs[t].at[2 * q + (1 - c)], out_refs[t].at[q], send_sems, recv_sems, t, q, sib)

        def start():
            for t in range(n):
                for q in range(N_CHIPS):
                    copy(t, q).start()

        def finish():
            for t in range(n):
                for q in range(N_CHIPS):
                    copy(t, q).wait_recv()
            for t in range(n):
                for q in range(N_CHIPS):
                    copy(t, q).wait_send()

        return start, finish

    out_shape = [jax.ShapeDtypeStruct((N_CHIPS,) + a.shape[1:], a.dtype) for a in arrays]
    sems = [pltpu.SemaphoreType.DMA((n, N_CHIPS)), pltpu.SemaphoreType.DMA((n, N_CHIPS))]
    return _Hosted(arrays, out_shape, sems, ops)


def chip_exchange(arrays):
    n = len(arrays)

    def ops(in_refs, out_refs, send_sems, recv_sems, local_sems):
        _, _, mine = _peer(0)

        def copy(t, k, arriving):
            dev, _, q = _peer(k)
            return _remote(in_refs[t].at[q], out_refs[t].at[q if arriving else mine], send_sems, recv_sems, t, k, dev)

        def own(t):
            return pltpu.make_async_copy(in_refs[t].at[mine], out_refs[t].at[mine], local_sems.at[t])

        def start():
            for t in range(n):
                own(t).start()
                for k in OTHER_CHIPS:
                    copy(t, k, False).start()

        def finish():
            for t in range(n):
                for k in OTHER_CHIPS:
                    copy(t, k, True).wait_recv()
            for t in range(n):
                for k in OTHER_CHIPS:
                    copy(t, k, False).wait_send()
                own(t).wait()

        return start, finish

    out_shape = [jax.ShapeDtypeStruct(a.shape, a.dtype) for a in arrays]
    sems = [pltpu.SemaphoreType.DMA((n, N_DEV)), pltpu.SemaphoreType.DMA((n, N_DEV)), pltpu.SemaphoreType.DMA((n,))]
    return _Hosted(arrays, out_shape, sems, ops)


MATMUL_VMEM_BUDGET = 34 * 1024 * 1024


V7X_MXU_FLOPS = 996e12
V7X_HBM_BYTES_PER_S = 3.4e12
GRID_STEP_S = 0.35e-6


def _tile_sizes(dim, cap):
    return [t for t in range(LANES, min(dim, cap) + 1, LANES) if dim % t == 0] or [dim]


def _matmul_tiles(m, n, k, a_bytes, b_bytes, add_bytes, out_bytes, whole_rows=False):
    best = None
    for tk in _tile_sizes(k, 8192):
        nk = k // tk
        for tn in ([n] if whole_rows else _tile_sizes(n, 2048)):
            for tm in _tile_sizes(m, 2048):
                io = tm * tk * a_bytes + tk * tn * b_bytes
                ends = tm * tn * (add_bytes + out_bytes)
                need = 2 * (io + ends) + tm * tn * 4 * (2 if nk > 1 else 1)
                if need > MATMUL_VMEM_BUDGET:
                    continue
                step = max(2.0 * tm * tn * tk / V7X_MXU_FLOPS, (io + ends / nk) / V7X_HBM_BYTES_PER_S)
                if nk > 1:
                    step += tm * tn * 8 / V7X_HBM_BYTES_PER_S
                cost = (m // tm) * (n // tn) * nk * (step + GRID_STEP_S)
                if best is None or cost < best[0]:
                    best = (cost, tm, tn, tk)
    if best is None:
        raise ValueError((m, n, k))
    return best[1:]


class _Tail:
    def __init__(self, fn, like=(), rows=(), outs=(), n_sums=0):
        self.fn, self.like, self.rows, self.outs, self.n_sums = fn, list(like), list(rows), list(outs), n_sums


def matmul(a, b, name, ta=False, tb=False, add=None, out_dtype=F32, comm=NO_EXCHANGE, tail=None):
    assert not (ta and tb)
    m, k = (a.shape[1], a.shape[0]) if ta else a.shape
    n = b.shape[0] if tb else b.shape[1]
    assert (b.shape[1] if tb else b.shape[0]) == k
    like = ([] if add is None else [add]) + (tail.like if tail else [])
    rows = tail.rows if tail else []
    outs = tail.outs if tail else [out_dtype]
    n_sums = tail.n_sums if tail else 0
    tm, tn, tk = _matmul_tiles(m, n, k, a.dtype.itemsize, b.dtype.itemsize, sum(x.dtype.itemsize for x in like),
                               sum(jnp.dtype(d).itemsize for d in outs), whole_rows=tail is not None)
    nk = k // tk
    grid = (m // tm, n // tn, nk)
    dims = ((0,), (0,)) if ta else (((1,), (1,)) if tb else ((1,), (0,)))
    n_in = 2 + len(like) + len(rows)
    n_out = len(outs) + n_sums
    n_acc = 0 if nk == 1 else 1

    def body(*refs):
        a_ref, b_ref = refs[:2]
        like_refs, row_refs = refs[2:2 + len(like)], refs[2 + len(like):n_in]
        out_refs = refs[n_in + comm.n:n_in + comm.n + n_out]
        ids = [pl.program_id(d) for d in range(3)]
        if comm.n:
            first = functools.reduce(jnp.logical_and, [i == 0 for i in ids])
            last = functools.reduce(jnp.logical_and, [i == g - 1 for i, g in zip(ids, grid)])
            done = comm.begin(refs[n_in:n_in + comm.n], refs[n_in + comm.n + n_out:n_in + 2 * comm.n + n_out],
                              refs[n_in + 2 * comm.n + n_out + n_acc:], first)

        def finish(r):
            if add is not None:
                r = r + like_refs[0][...].astype(F32)
            if tail is None:
                out_refs[0][...] = r.astype(out_dtype)
                return
            vals = tail.fn(r, *[x[...] for x in like_refs[len(like) - len(tail.like):]], *[x[...] for x in row_refs])
            for ref, val in zip(out_refs[:len(outs)], vals):
                ref[...] = val.astype(ref.dtype)
            for ref, val in zip(out_refs[len(outs):], vals[len(outs):]):
                @pl.when(ids[0] == 0)
                def _(ref=ref, val=val):
                    ref[...] = val

                @pl.when(ids[0] != 0)
                def _(ref=ref, val=val):
                    ref[...] += val

        if nk == 1:
            finish(_mm(a_ref[...], b_ref[...], dims))
        else:
            acc = refs[n_in + 2 * comm.n + n_out]

            @pl.when(ids[2] == 0)
            def _():
                acc[...] = jnp.zeros_like(acc)

            acc[...] += _mm(a_ref[...], b_ref[...], dims)

            @pl.when(ids[2] == nk - 1)
            def _():
                finish(acc[...])

        if comm.n:
            pl.when(last)(done)

    a_spec = pl.BlockSpec((tk, tm), lambda i, j, kk: (kk, i)) if ta else pl.BlockSpec((tm, tk), lambda i, j, kk: (i, kk))
    b_spec = pl.BlockSpec((tn, tk), lambda i, j, kk: (j, kk)) if tb else pl.BlockSpec((tk, tn), lambda i, j, kk: (kk, j))
    tile = pl.BlockSpec((tm, tn), lambda i, j, kk: (i, j))
    row = pl.BlockSpec((1, tn), lambda i, j, kk: (0, j))
    sequential = comm.n or n_sums
    res = pl.pallas_call(
        body, name=name,
        grid=grid,
        in_specs=[a_spec, b_spec] + [tile] * len(like) + [row] * len(rows) + comm.specs,
        out_specs=[tile] * len(outs) + [row] * n_sums + comm.specs,
        out_shape=[jax.ShapeDtypeStruct((m, n), d) for d in outs] + [jax.ShapeDtypeStruct((1, n), F32)] * n_sums
        + comm.out_shape,
        scratch_shapes=([] if nk == 1 else [pltpu.VMEM((tm, tn), F32)]) + comm.sems,
        compiler_params=_params(("arbitrary",) * 3 if sequential else ("parallel", "parallel", "arbitrary")),
    )(a, b, *like, *rows, *comm.arrays)
    return res if (comm.n or tail) else res[0]


class _Ctx:
    def __init__(self, first, last):
        self.first = first
        self.last = last


def rowwise(fn, name, rows, seq, tb, ncol, ins, params=(), outs=(), accs=()):
    assert rows % tb == 0 and seq % tb == 0 and tb % 16 == 0
    bps = seq // tb
    nrow = rows // tb
    r8 = tb // SUBLANES
    args, in_specs = [], []
    for arr, w, off, halo in ins:
        args.append(arr)
        in_specs.append(pl.BlockSpec((tb, w), lambda j, i, off=off: (i, off + j)))
        if halo in ("prev", "both"):
            args.append(arr)
            in_specs.append(pl.BlockSpec((SUBLANES, w), lambda j, i, off=off: (jnp.maximum(i * r8 - 1, 0), off + j)))
        if halo in ("next", "both"):
            args.append(arr)
            in_specs.append(pl.BlockSpec(
                (SUBLANES, w), lambda j, i, off=off: (jnp.minimum((i + 1) * r8, rows // SUBLANES - 1), off + j)))
    for arr, w, off in params:
        args.append(arr)
        if w is None:
            in_specs.append(pl.BlockSpec(arr.shape, lambda j, i: (0, 0)))
        else:
            in_specs.append(pl.BlockSpec((arr.shape[0], w), lambda j, i, off=off: (0, off + j)))
    out_shape, out_specs = [], []
    for total, w, off, dt in outs:
        out_shape.append(jax.ShapeDtypeStruct((rows, total), dt))
        out_specs.append(pl.BlockSpec((tb, w), lambda j, i, off=off: (i, off + j)))
    for r, w in accs:
        out_shape.append(jax.ShapeDtypeStruct((r, ncol * w), F32))
        out_specs.append(pl.BlockSpec((r, w), lambda j, i: (0, j)))
    n_out, n_acc = len(outs), len(accs)

    def body(*refs):
        i = pl.program_id(1)
        pos = 0
        vals = []
        for _, _, _, halo in ins:
            cur = refs[pos][...]
            pos += 1
            if halo is None:
                vals.append(cur)
            elif halo == "both":
                vals.append((cur, refs[pos][...], refs[pos + 1][...]))
                pos += 2
            else:
                vals.append((cur, refs[pos][...]))
                pos += 1
        for _ in params:
            vals.append(refs[pos][...])
            pos += 1
        ctx = _Ctx(i % bps == 0, i % bps == bps - 1)
        res = fn(ctx, *vals)
        if not isinstance(res, (tuple, list)):
            res = (res,)
        assert len(res) == n_out + n_acc
        for q in range(n_out):
            refs[pos + q][...] = res[q].astype(refs[pos + q].dtype)
        for q in range(n_acc):
            ref, val = refs[pos + n_out + q], res[n_out + q]

            @pl.when(i == 0)
            def _(ref=ref, val=val):
                ref[...] = val

            @pl.when(i != 0)
            def _(ref=ref, val=val):
                ref[...] += val

    res = pl.pallas_call(
        body, name=name,
        grid=(ncol, nrow),
        in_specs=in_specs,
        out_specs=out_specs,
        out_shape=out_shape,
        compiler_params=_params(("parallel", "arbitrary")),
    )(*args)
    return res


GROUP_W = SSD_K * SSD_P


def _tri(lower):
    r = lax.broadcasted_iota(jnp.int32, (SSD_Q, SSD_Q), 0)
    c = lax.broadcasted_iota(jnp.int32, (SSD_Q, SSD_Q), 1)
    return r >= c if lower else r <= c


def _first_head_lanes():
    return lax.broadcasted_iota(jnp.int32, (1, LANES), 1) < SSD_P


def _column(v, j):
    return v[:, j * LANES:(j + 1) * LANES] if v.shape[-1] == GROUP_W else v


def _per_head(vals):
    first = _first_head_lanes()
    return jnp.concatenate([jnp.where(first, _column(vals[2 * j], j), _column(vals[2 * j + 1], j))
                            for j in range(GROUP_W // LANES)], axis=1)


def _per_head_rows(vals):
    return jnp.concatenate([jnp.broadcast_to(v, (SSD_P, 1)) for v in vals], axis=0)


def _own_columns(slab, k):
    keep = _first_head_lanes() if k % 2 == 0 else jnp.logical_not(_first_head_lanes())
    own = jnp.where(keep, _column(slab, k // 2), 0.0)
    return jnp.concatenate([own, jnp.zeros_like(own)] if k < 2 else [jnp.zeros_like(own), own], axis=1)


def _headsum(prod, g):
    out = None
    for k in range(SSD_K):
        keep = _first_head_lanes() if k % 2 == 0 else jnp.logical_not(_first_head_lanes())
        term = jnp.sum(jnp.where(keep, _column(prod, k // 2), 0.0), axis=1, keepdims=True) * _onehot_row(g * SSD_K + k)
        out = term if out is None else out + term
    return out


def ssd_fwd(xact, dtraw, dt_bias, a_log, d_skip, n_seq, seq, comm=NO_EXCHANGE):
    nc = seq // SSD_Q
    rows = n_seq * seq
    nx = comm.n

    def body(*refs):
        xact_ref, dtraw_ref, bias_ref, alog_ref, dskip_ref = refs[:5]
        y_ref, sin_ref = refs[5 + nx:7 + nx]
        state, cs_s, cst_s, dt_s = refs[7 + 2 * nx:11 + 2 * nx]
        b, c = pl.program_id(0), pl.program_id(1)
        if nx:
            finish = comm.begin(refs[5:5 + nx], refs[7 + nx:7 + 2 * nx], refs[11 + 2 * nx:],
                                jnp.logical_and(b == 0, c == 0))

        @pl.when(c == 0)
        def _():
            state[...] = jnp.zeros_like(state)

        sin_ref[0] = state[...]
        dt = _softplus(dtraw_ref[...] + bias_ref[...])
        a = dt * (-jnp.exp(alog_ref[...]))
        cs = _dot_f32(_tri(True).astype(F32), a)
        cs_s[...] = cs
        cst_s[...] = cs.T
        dt_s[...] = dt
        causal = _tri(True)
        def front(g):
            heads = [g * SSD_K + k for k in range(SSD_K)]
            bg = xact_ref[:, pl.ds(D_INNER + g * SSD_N, SSD_N)]
            cg = xact_ref[:, pl.ds(D_INNER + (SSD_G + g) * SSD_N, SSD_N)]
            xg = xact_ref[:, pl.ds(g * GROUP_W, GROUP_W)]
            cols = [cs_s[:, pl.ds(h, 1)] for h in heads]
            lasts = [cs_s[pl.ds(SSD_Q - 1, 1), pl.ds(h, 1)] for h in heads]
            xdg = xg * _per_head([dt_s[:, pl.ds(h, 1)] for h in heads])
            sg = state[g]
            y = (_per_head([jnp.exp(c_) for c_ in cols]) * _dot_nt(cg, sg)
                 + _per_head([dskip_ref[:, pl.ds(h, 1)] for h in heads]) * xg)
            w = _per_head([jnp.exp(l_ - c_) for l_, c_ in zip(lasts, cols)])
            state[g] = _per_head_rows([jnp.exp(l_) for l_ in lasts]) * sg + _dot_tn(w * xdg, bg)
            return heads, cols, _dot_nt(cg, bg), xdg, y

        def back(g, heads, cols, gm, xdg, y):
            mats = [gm * jnp.exp(jnp.where(causal, cols[k] - cst_s[pl.ds(h, 1), :], NEG)) for k, h in enumerate(heads)]
            y4 = _dot_nn(jnp.concatenate(mats, axis=0), xdg)
            y_ref[:, pl.ds(g * GROUP_W, GROUP_W)] = y + _per_head([y4[k * SSD_Q:(k + 1) * SSD_Q] for k in range(SSD_K)])

        ahead = front(0)
        for g in range(SSD_G):
            cur, ahead = ahead, (front(g + 1) if g + 1 < SSD_G else None)
            back(g, *cur)
        if nx:
            pl.when(jnp.logical_and(b == n_seq - 1, c == nc - 1))(finish)

    vec = pl.BlockSpec((1, LANES), lambda b, c: (0, 0))
    return pl.pallas_call(
        body, name="ssd_fwd",
        grid=(n_seq, nc),
        in_specs=[pl.BlockSpec((SSD_Q, CONV_DIM), lambda b, c: (b * nc + c, 0)),
                  pl.BlockSpec((SSD_Q, LANES), lambda b, c: (b * nc + c, 0)), vec, vec, vec] + comm.specs,
        out_specs=[pl.BlockSpec((SSD_Q, D_INNER), lambda b, c: (b * nc + c, 0)),
                   pl.BlockSpec((1, SSD_G, GROUP_W, SSD_N), lambda b, c: (b * nc + c, 0, 0, 0))] + comm.specs,
        out_shape=[jax.ShapeDtypeStruct((rows, D_INNER), F32),
                   jax.ShapeDtypeStruct((n_seq * nc, SSD_G, GROUP_W, SSD_N), F32)] + comm.out_shape,
        scratch_shapes=[pltpu.VMEM((SSD_G, GROUP_W, SSD_N), F32), pltpu.VMEM((SSD_Q, LANES), F32),
                        pltpu.VMEM((LANES, SSD_Q), F32), pltpu.VMEM((SSD_Q, LANES), F32)] + comm.sems,
        compiler_params=_params(("arbitrary", "arbitrary")),
    )(xact, dtraw, dt_bias, a_log, d_skip, *comm.arrays)


def ssd_bwd(xact, dtraw, dt_bias, a_log, d_skip, sin, dy, n_seq, seq, comm=NO_EXCHANGE):
    nc = seq // SSD_Q
    rows = n_seq * seq
    nx = comm.n

    def body(*refs):
        xact_ref, dtraw_ref, bias_ref, alog_ref, dskip_ref, sin_ref, dy_ref = refs[:7]
        dx_ref, ddt_ref, dbias_ref, dalog_ref, ddskip_ref = refs[7 + nx:12 + nx]
        dstate, cs_s, cst_s, dt_s = refs[12 + 2 * nx:16 + 2 * nx]
        b, c = pl.program_id(0), pl.program_id(1)
        if nx:
            finish = comm.begin(refs[7:7 + nx], refs[12 + nx:12 + 2 * nx], refs[16 + 2 * nx:],
                                jnp.logical_and(b == 0, c == 0))

        @pl.when(c == 0)
        def _():
            dstate[...] = jnp.zeros_like(dstate)

        pre = dtraw_ref[...] + bias_ref[...]
        dt = _softplus(pre)
        a_neg = -jnp.exp(alog_ref[...])
        cs = _dot_f32(_tri(True).astype(F32), dt * a_neg)
        cs_s[...] = cs
        cst_s[...] = cs.T
        dt_s[...] = dt
        causal, anti = _tri(True), _tri(False)
        is_last_row = lax.broadcasted_iota(jnp.int32, (SSD_Q, 1), 0) == SSD_Q - 1
        dcs_cf = jnp.zeros((SSD_Q, LANES), F32)
        dcs_rf = jnp.zeros((LANES, SSD_Q), F32)
        ddt_cf = jnp.zeros((SSD_Q, LANES), F32)
        dd_vec = jnp.zeros((1, LANES), F32)
        dlast_vec = jnp.zeros((1, LANES), F32)
        def front(g):
            heads = [g * SSD_K + k for k in range(SSD_K)]
            v = {"heads": heads}
            bg = v["bg"] = xact_ref[:, pl.ds(D_INNER + g * SSD_N, SSD_N)]
            cg = v["cg"] = xact_ref[:, pl.ds(D_INNER + (SSD_G + g) * SSD_N, SSD_N)]
            xg = v["xg"] = xact_ref[:, pl.ds(g * GROUP_W, GROUP_W)]
            dyg = v["dyg"] = dy_ref[:, pl.ds(g * GROUP_W, GROUP_W)]
            cols = [cs_s[:, pl.ds(h, 1)] for h in heads]
            rws = [cst_s[pl.ds(h, 1), :] for h in heads]
            lasts = [cs_s[pl.ds(SSD_Q - 1, 1), pl.ds(h, 1)] for h in heads]
            e_lasts = v["e_lasts"] = [jnp.exp(l_) for l_ in lasts]
            v["dtg"] = _per_head([dt_s[:, pl.ds(h, 1)] for h in heads])
            v["dskg"] = _per_head([dskip_ref[:, pl.ds(h, 1)] for h in heads])
            e_col = _per_head([jnp.exp(c_) for c_ in cols])
            w = v["w"] = _per_head([jnp.exp(l_ - c_) for l_, c_ in zip(lasts, cols)])
            xdg = v["xdg"] = xg * v["dtg"]
            sg = sin_ref[0, g]
            dsn = dstate[g]
            v["gm"] = _dot_nt(cg, bg)
            gmt = _dot_nt(bg, cg)
            v["y_off"] = e_col * _dot_nt(cg, sg)
            d_cs = e_col * dyg
            v["dcg"] = _dot_nn(d_cs, sg)
            dstate[g] = _dot_tn(d_cs, cg) + _per_head_rows(e_lasts) * dsn
            v["dbg"] = _dot_nn(w * xdg, dsn)
            v["dtt"] = _dot_nt(bg, dsn)
            v["dsn_s"] = dsn * sg
            segs = [cols[k] - rws[k] for k in range(SSD_K)]
            v["decays"] = [jnp.exp(jnp.where(causal, s_, NEG)) for s_ in segs]
            v["dm4"] = _dot_nt(jnp.concatenate([_own_columns(dyg, k) for k in range(SSD_K)], axis=0), xdg)
            v["z4"] = _dot_nn(jnp.concatenate([gmt * jnp.exp(jnp.where(anti, -s_, NEG)) for s_ in segs], axis=0), dyg)
            return v

        def back(g, v, sums):
            dcs_cf, dcs_rf, ddt_cf, dd_vec, dlast_vec = sums
            dxd = v["w"] * v["dtt"] + _per_head([v["z4"][k * SSD_Q:(k + 1) * SSD_Q] for k in range(SSD_K)])
            dw = _headsum(v["dtt"] * v["xdg"] * v["w"], g)
            dcs_cf = dcs_cf + _headsum(v["dyg"] * v["y_off"], g) - dw
            dlast_vec = dlast_vec + jnp.sum(dw, axis=0, keepdims=True)
            dgm = jnp.zeros((SSD_Q, SSD_Q), F32)
            for k, h in enumerate(v["heads"]):
                dm = v["dm4"][k * SSD_Q:(k + 1) * SSD_Q]
                dseg = dm * v["gm"] * v["decays"][k]
                dgm = dgm + dm * v["decays"][k]
                oh_r = _onehot_row(h)
                dcs_cf = dcs_cf + jnp.sum(dseg, axis=1, keepdims=True) * oh_r
                dcs_rf = dcs_rf - _onehot_col(h) * jnp.sum(dseg, axis=0, keepdims=True)
                dlast_vec = dlast_vec + (jnp.sum(v["dsn_s"][k * SSD_P:(k + 1) * SSD_P], keepdims=True)
                                         * v["e_lasts"][k] * oh_r)
            dx_ref[:, pl.ds(g * GROUP_W, GROUP_W)] = dxd * v["dtg"] + v["dskg"] * v["dyg"]
            ddt_cf = ddt_cf + _headsum(dxd * v["xg"], g)
            dd_vec = dd_vec + _headsum(jnp.sum(v["dyg"] * v["xg"], axis=0, keepdims=True), g)
            dx_ref[:, pl.ds(D_INNER + g * SSD_N, SSD_N)] = v["dbg"] + _dot_tn(dgm, v["cg"])
            dx_ref[:, pl.ds(D_INNER + (SSD_G + g) * SSD_N, SSD_N)] = v["dcg"] + _dot_nn(dgm, v["bg"])
            return dcs_cf, dcs_rf, ddt_cf, dd_vec, dlast_vec

        sums = (dcs_cf, dcs_rf, ddt_cf, dd_vec, dlast_vec)
        ahead = front(0)
        for g in range(SSD_G):
            cur, ahead = ahead, (front(g + 1) if g + 1 < SSD_G else None)
            sums = back(g, cur, sums)
        dcs_cf, dcs_rf, ddt_cf, dd_vec, dlast_vec = sums
        dcs = dcs_cf + dcs_rf.T + jnp.where(is_last_row, dlast_vec, 0.0)
        da = _dot_f32(_tri(False).astype(F32), dcs)
        ddt = ddt_cf + da * a_neg
        ddtraw = ddt * _sigmoid(pre)
        ddt_ref[...] = ddtraw.astype(ddt_ref.dtype)
        dbias = jnp.sum(ddtraw, axis=0, keepdims=True)
        dalog = jnp.sum(da * dt, axis=0, keepdims=True) * a_neg
        first_step = jnp.logical_and(b == 0, c == 0)

        @pl.when(first_step)
        def _():
            dbias_ref[...] = dbias
            dalog_ref[...] = dalog
            ddskip_ref[...] = dd_vec

        @pl.when(jnp.logical_not(first_step))
        def _():
            dbias_ref[...] += dbias
            dalog_ref[...] += dalog
            ddskip_ref[...] += dd_vec

        if nx:
            pl.when(jnp.logical_and(b == n_seq - 1, c == nc - 1))(finish)

    def rowblk(b, c):
        return b * nc + (nc - 1 - c)

    vec = pl.BlockSpec((1, LANES), lambda b, c: (0, 0))
    return pl.pallas_call(
        body, name="ssd_bwd",
        grid=(n_seq, nc),
        in_specs=[pl.BlockSpec((SSD_Q, CONV_DIM), lambda b, c: (rowblk(b, c), 0)),
                  pl.BlockSpec((SSD_Q, LANES), lambda b, c: (rowblk(b, c), 0)), vec, vec, vec,
                  pl.BlockSpec((1, SSD_G, GROUP_W, SSD_N), lambda b, c: (rowblk(b, c), 0, 0, 0)),
                  pl.BlockSpec((SSD_Q, D_INNER), lambda b, c: (rowblk(b, c), 0))] + comm.specs,
        out_specs=[pl.BlockSpec((SSD_Q, CONV_DIM), lambda b, c: (rowblk(b, c), 0)),
                   pl.BlockSpec((SSD_Q, LANES), lambda b, c: (rowblk(b, c), 0)), vec, vec, vec] + comm.specs,
        out_shape=[jax.ShapeDtypeStruct((rows, CONV_DIM), F32), jax.ShapeDtypeStruct((rows, LANES), BF16),
                   jax.ShapeDtypeStruct((1, LANES), F32), jax.ShapeDtypeStruct((1, LANES), F32),
                   jax.ShapeDtypeStruct((1, LANES), F32)] + comm.out_shape,
        scratch_shapes=[pltpu.VMEM((SSD_G, GROUP_W, SSD_N), F32), pltpu.VMEM((SSD_Q, LANES), F32),
                        pltpu.VMEM((LANES, SSD_Q), F32), pltpu.VMEM((SSD_Q, LANES), F32)] + comm.sems,
        compiler_params=_params(("arbitrary", "arbitrary")),
    )(xact, dtraw, dt_bias, a_log, d_skip, sin, dy, *comm.arrays)


QKV_W = 3 * ATT_OUT
PAIR_W = 2 * ATT_HD
HEAD_PAIRS = ATT_H // 2
PREP_ROWS = 512


def _by_residue(a, n_seq, seq, dil):
    if dil == 1:
        return a
    return a.reshape(n_seq, seq // dil, dil, a.shape[1]).transpose(0, 2, 1, 3).reshape(a.shape)


def _by_token(a, n_seq, seq, dil):
    if dil == 1:
        return a
    return a.reshape(n_seq, dil, seq // dil, a.shape[1]).transpose(0, 2, 1, 3).reshape(a.shape)


def _head_sums(x, fn):
    lo = jnp.logical_not(lax.broadcasted_iota(jnp.int32, (1, 2 * ATT_HD), 1) >= ATT_HD)
    parts = []
    for p in range(ATT_H // 2):
        slab = x[:, p * 2 * ATT_HD:(p + 1) * 2 * ATT_HD]
        s_lo = fn(jnp.sum(jnp.where(lo, slab, 0.0), axis=1, keepdims=True))
        s_hi = fn(jnp.sum(jnp.where(lo, 0.0, slab), axis=1, keepdims=True))
        parts.append(jnp.where(lo, s_lo, s_hi))
    return jnp.concatenate(parts, axis=1)


def _head_rstd(x):
    return _head_sums(x * x, lambda s: lax.rsqrt(s * (1.0 / ATT_HD) + EPS))


def _head_rms_bwd(x, g_t, dy):
    r = _head_rstd(x)
    xh = x * r
    dyg = dy * g_t
    mean = _head_sums(dyg * xh, lambda s: s * (1.0 / ATT_HD))
    return r * (dyg - xh * mean), jnp.sum(dy * xh, axis=0, keepdims=True)


def _lane_hi():
    return lax.broadcasted_iota(jnp.int32, (1, PAIR_W), 1) >= ATT_HD


def _band_mask2(first_valid, query_rows):
    i = lax.broadcasted_iota(jnp.int32, (ATT_BLK, 2 * ATT_BLK), 0)
    j = lax.broadcasted_iota(jnp.int32, (ATT_BLK, 2 * ATT_BLK), 1)
    left = j < ATT_BLK
    right = jnp.logical_not(left)
    if query_rows:
        return jnp.logical_or(jnp.logical_and(jnp.logical_and(left, i <= j), first_valid),
                              jnp.logical_and(right, i >= j - ATT_BLK))
    return jnp.logical_or(jnp.logical_and(left, j >= i),
                          jnp.logical_and(jnp.logical_and(right, j - ATT_BLK <= i), first_valid))


def _only_head(slab, hi):
    keep = _lane_hi() if hi else jnp.logical_not(_lane_hi())
    return jnp.where(keep, slab, jnp.zeros_like(slab))


def attn_fwd(nq, n_seq, seq, dil, name):
    nb = seq // dil // ATT_BLK
    rows = n_seq * seq

    def body(cur_ref, prev_ref, o_ref, lse_ref, s_scr, p_scr):
        n = pl.program_id(1)
        mask = _band_mask2(n > 0, True)
        for h in range(ATT_H):
            sl = pl.ds((h // 2) * PAIR_W, PAIR_W)
            ks = pl.ds(ATT_OUT + (h // 2) * PAIR_W, PAIR_W)
            kcat = jnp.concatenate([prev_ref[:, ks], cur_ref[:, ks]], axis=0)
            s_scr[h] = jnp.where(mask, _dot_nt(_only_head(cur_ref[:, sl], h % 2), kcat), NEG)
        s_all = s_scr[...]
        mx = jnp.max(s_all, axis=2, keepdims=True)
        p_all = jnp.exp(s_all - mx)
        den = jnp.sum(p_all, axis=2, keepdims=True)
        p_scr[...] = p_all.astype(p_scr.dtype)
        inv = 1.0 / den
        lse = mx + jnp.log(den)
        lse_blk = jnp.zeros((ATT_BLK, LANES), F32)
        for h in range(ATT_H):
            lse_blk = lse_blk + lse[h] * _onehot_row(h)
        lse_ref[...] = lse_blk
        for pr in range(HEAD_PAIRS):
            vs = pl.ds(2 * ATT_OUT + pr * PAIR_W, PAIR_W)
            vcat = jnp.concatenate([prev_ref[:, vs], cur_ref[:, vs]], axis=0)
            lo = _dot_nn(p_scr[2 * pr], vcat) * inv[2 * pr]
            hi = _dot_nn(p_scr[2 * pr + 1], vcat) * inv[2 * pr + 1]
            o_ref[:, pl.ds(pr * PAIR_W, PAIR_W)] = jnp.where(_lane_hi(), hi, lo)

    def blk(width, shift):
        if shift:
            return pl.BlockSpec((ATT_BLK, width), lambda s, n: (s * nb + jnp.maximum(n - 1, 0), 0))
        return pl.BlockSpec((ATT_BLK, width), lambda s, n: (s * nb + n, 0))

    return pl.pallas_call(
        body, name=name,
        grid=(n_seq * dil, nb),
        in_specs=[blk(QKV_W, 0), blk(QKV_W, -1)],
        out_specs=[blk(ATT_OUT, 0), blk(LANES, 0)],
        out_shape=[jax.ShapeDtypeStruct((rows, ATT_OUT), F32), jax.ShapeDtypeStruct((rows, LANES), F32)],
        scratch_shapes=[pltpu.VMEM((ATT_H, ATT_BLK, 2 * ATT_BLK), F32), pltpu.VMEM((ATT_H, ATT_BLK, 2 * ATT_BLK), MXU)],
        compiler_params=_params(("parallel", "arbitrary")),
    )(nq, nq)


def attn_bwd(nq, do, lse, wts, rsum, n_seq, seq, dil, name):
    nb = seq // dil // ATT_BLK

    def body(prev_ref, cur_ref, nxt_ref, do_c, do_x, lse_c, lse_x, wt_c, wt_x, rs_c, rs_x, dn_ref):
        n = pl.program_id(1)
        mask_q = _band_mask2(n > 0, True)
        mask_k = _band_mask2(n < nb - 1, False)
        wc, wx = wt_c[...], wt_x[...]
        lse_t = jnp.concatenate([lse_c[...].T, lse_x[...].T], axis=1)
        dl_t = jnp.concatenate([(-wc * rs_c[...]).T, (-wx * rs_x[...]).T], axis=1)
        def operands(pr):
            sl = pl.ds(pr * PAIR_W, PAIR_W)
            ks = pl.ds(ATT_OUT + pr * PAIR_W, PAIR_W)
            vs = pl.ds(2 * ATT_OUT + pr * PAIR_W, PAIR_W)
            he, ho = pl.ds(2 * pr, 1), pl.ds(2 * pr + 1, 1)
            q_c, k_c, v_c = cur_ref[:, sl], cur_ref[:, ks], cur_ref[:, vs]
            dog_c = do_c[:, sl] * jnp.where(_lane_hi(), wt_c[:, ho], wt_c[:, he])
            dog_x = do_x[:, sl] * jnp.where(_lane_hi(), wt_x[:, ho], wt_x[:, he])
            return dict(q_c=q_c, k_c=k_c, v_c=v_c, qcat=jnp.concatenate([q_c, nxt_ref[:, sl]], axis=0),
                        kcat=jnp.concatenate([prev_ref[:, ks], k_c], axis=0),
                        vcat=jnp.concatenate([prev_ref[:, vs], v_c], axis=0),
                        dog=jnp.concatenate([dog_c, dog_x], axis=0).astype(MXU))

        def scores(o, h):
            hi, one = h % 2, pl.ds(h, 1)
            dl_col = -wt_c[:, one] * rs_c[:, one]
            p_q = jnp.exp(jnp.where(mask_q, _dot_nt(_only_head(o["q_c"], hi), o["kcat"]) - lse_c[:, one], NEG))
            ds_q = p_q * (_dot_nt(_only_head(o["dog"][:ATT_BLK], hi), o["vcat"]) + dl_col)
            p_t = jnp.exp(jnp.where(mask_k, _dot_nt(_only_head(o["k_c"], hi), o["qcat"]) - lse_t[h:h + 1, :], NEG))
            ds_t = p_t * (_dot_nt(_only_head(o["v_c"], hi), o["dog"]) + dl_t[h:h + 1, :])
            return ds_q, ds_t, p_t

        ops = [operands(pr) for pr in range(HEAD_PAIRS)]
        ahead = scores(ops[0], 0)
        res = []
        for h in range(ATT_H):
            o = ops[h // 2]
            (ds_q, ds_t, p_t), ahead = ahead, (scores(ops[(h + 1) // 2], h + 1) if h + 1 < ATT_H else None)
            res.append((_dot_nn(ds_q, o["kcat"]), _dot_nn(ds_t, o["qcat"]), _dot_nn(p_t, o["dog"])))
            if h % 2:
                for t, first in enumerate((0, ATT_OUT, 2 * ATT_OUT)):
                    dn_ref[:, pl.ds(first + (h // 2) * PAIR_W, PAIR_W)] = jnp.where(_lane_hi(), res[h][t], res[h - 1][t])

    def at(shift, width):
        if shift < 0:
            return pl.BlockSpec((ATT_BLK, width), lambda s, n: (s * nb + jnp.maximum(n - 1, 0), 0))
        if shift > 0:
            return pl.BlockSpec((ATT_BLK, width), lambda s, n: (s * nb + jnp.minimum(n + 1, nb - 1), 0))
        return pl.BlockSpec((ATT_BLK, width), lambda s, n: (s * nb + n, 0))

    return pl.pallas_call(
        body, name=name,
        grid=(n_seq * dil, nb),
        in_specs=[at(-1, QKV_W), at(0, QKV_W), at(1, QKV_W), at(0, ATT_OUT), at(1, ATT_OUT),
                  at(0, LANES), at(1, LANES), at(0, LANES), at(1, LANES), at(0, LANES), at(1, LANES)],
        out_specs=at(0, QKV_W),
        out_shape=jax.ShapeDtypeStruct((n_seq * seq, QKV_W), F32),
        compiler_params=_params(("parallel", "arbitrary")),
    )(nq, nq, nq, do, do, lse, lse, wts, wts, rsum, rsum)


def qk_post(qkv, dn, gq_t, gk_t, rows, name):
    tb = min(PREP_ROWS, rows)

    def body(x_ref, dn_ref, gq_ref, gk_ref, o_ref, dgq_ref, dgk_ref):
        i = pl.program_id(0)
        qs, ks, vs = pl.ds(0, ATT_OUT), pl.ds(ATT_OUT, ATT_OUT), pl.ds(2 * ATT_OUT, ATT_OUT)
        dq, dgq = _head_rms_bwd(x_ref[:, qs], gq_ref[...], dn_ref[:, qs] * ATT_SCALE)
        dk, dgk = _head_rms_bwd(x_ref[:, ks], gk_ref[...], dn_ref[:, ks])
        o_ref[:, qs] = dq.astype(o_ref.dtype)
        o_ref[:, ks] = dk.astype(o_ref.dtype)
        o_ref[:, vs] = dn_ref[:, vs].astype(o_ref.dtype)

        @pl.when(i == 0)
        def _():
            dgq_ref[...] = dgq
            dgk_ref[...] = dgk

        @pl.when(i != 0)
        def _():
            dgq_ref[...] += dgq
            dgk_ref[...] += dgk

    gspec = pl.BlockSpec((1, ATT_OUT), lambda i: (0, 0))
    blk = pl.BlockSpec((tb, QKV_W), lambda i: (i, 0))
    return pl.pallas_call(
        body, name=name,
        grid=(rows // tb,),
        in_specs=[blk, blk, gspec, gspec],
        out_specs=[blk, gspec, gspec],
        out_shape=[jax.ShapeDtypeStruct((rows, QKV_W), MXU), jax.ShapeDtypeStruct((1, ATT_OUT), F32),
                   jax.ShapeDtypeStruct((1, ATT_OUT), F32)],
        compiler_params=_params(("arbitrary",)),
    )(qkv, dn, gq_t, gk_t)


def all_gather(arrays, name):
    n = len(arrays)

    def body(*refs):
        in_refs, out_refs = refs[:n], refs[n:2 * n]
        send_sems, recv_sems, local_sems = refs[2 * n:]
        _, me, _ = _peer(0)
        sib, _, _ = _peer(1)

        def first(t, k, arriving):
            dev, pid, _ = _peer(k)
            return _remote(in_refs[t], out_refs[t].at[pid if arriving else me], send_sems, recv_sems, t, k, dev)

        def passed(t, k, arriving):
            slot = out_refs[t].at[_peer(k + 1 if arriving else k)[1]]
            return _remote(slot, slot, send_sems, recv_sems, t, k + 1, sib)

        def own(t):
            return pltpu.make_async_copy(in_refs[t], out_refs[t].at[me], local_sems.at[t])

        for t in range(n):
            own(t).start()
            for k in (1,) + OTHER_CHIPS:
                first(t, k, False).start()
        for t in range(n):
            for k in OTHER_CHIPS:
                first(t, k, True).wait_recv()
                passed(t, k, False).start()
        for t in range(n):
            first(t, 1, True).wait_recv()
            for k in OTHER_CHIPS:
                passed(t, k, True).wait_recv()
        for t in range(n):
            for k in (1,) + OTHER_CHIPS:
                first(t, k, False).wait_send()
            for k in OTHER_CHIPS:
                passed(t, k, False).wait_send()
            own(t).wait()

    anyspec = pl.BlockSpec(memory_space=pl.ANY)
    return pl.pallas_call(
        body, name=name,
        in_specs=[anyspec] * n,
        out_specs=[anyspec] * n,
        out_shape=[jax.ShapeDtypeStruct((N_DEV,) + tuple(a.shape), a.dtype) for a in arrays],
        scratch_shapes=[pltpu.SemaphoreType.DMA((n, N_DEV)), pltpu.SemaphoreType.DMA((n, N_DEV)),
                        pltpu.SemaphoreType.DMA((n,))],
    )(*arrays)


def pair_add(a, b, name):
    _, r, c = a.shape
    rb = r if r <= 512 else (128 if c > 1024 else 256)
    assert r % rb == 0

    def body(a_ref, b_ref, o_ref):
        o_ref[...] = (a_ref[...].astype(F32) + b_ref[...].astype(F32)).astype(o_ref.dtype)

    blk = pl.BlockSpec((1, rb, c), lambda q, i: (q, i, 0))
    return pl.pallas_call(
        body, name=name,
        grid=(N_CHIPS, r // rb),
        in_specs=[blk, blk],
        out_specs=blk,
        out_shape=jax.ShapeDtypeStruct(a.shape, a.dtype),
        compiler_params=_params(("parallel", "parallel")),
    )(a, b)


def adamw(parts, w, m, v, name):
    r, c = w.shape[-2:]
    n_parts = parts.shape[0]
    rb = r if r <= 512 else (128 if c > 1024 else 256)
    assert r % rb == 0

    def body(p_ref, w_ref, m_ref, v_ref, g_out, d_out, m_out, v_out):
        g = p_ref[0].astype(F32)
        for i in range(1, n_parts):
            g = g + p_ref[i].astype(F32)
        m_new = ADAM_B1 * m_ref[...] + (1.0 - ADAM_B1) * g
        v_new = ADAM_B2 * v_ref[...] + (1.0 - ADAM_B2) * (g * g)
        m_hat = m_new / (1.0 - ADAM_B1 ** ADAM_STEP)
        v_hat = v_new / (1.0 - ADAM_B2 ** ADAM_STEP)
        g_out[...] = g
        d_out[...] = -ADAM_LR * (m_hat / (jnp.sqrt(v_hat) + ADAM_EPS) + ADAM_WD * w_ref[...])
        m_out[...] = m_new
        v_out[...] = v_new

    if w.ndim == 3:
        blk = pl.BlockSpec((None, rb, c), lambda i: (0, i, 0))
    else:
        blk = pl.BlockSpec((rb, c), lambda i: (i, 0))
    return pl.pallas_call(
        body, name=name,
        grid=(r // rb,),
        in_specs=[pl.BlockSpec((n_parts, rb, c), lambda i: (0, i, 0)), blk, blk, blk],
        out_specs=[blk] * 4,
        out_shape=[jax.ShapeDtypeStruct(w.shape, F32)] * 4,
        compiler_params=_params(("parallel",)),
    )(parts, w, m, v)


def _pad_lanes(vec, n=LANES):
    return jnp.pad(vec, ((0, 0), (0, n - vec.shape[1])))


COL_SHARDED = ("w_in", "ssd_conv_w", "w_attn_proj", "w_up", "ffn_conv_w")
MATRICES = ("w_in", "w_attn_proj", "w_up", "w_ssd_proj", "w_out", "w_down")
LATE = ("w_ssd_proj", "w_attn_proj", "w_out", "w_up", "ffn_conv_w", "w_down")


def _narrow(name, a):
    return a.astype(MXU) if name in MATRICES else a


def _from_gathered(name, g):
    if name in COL_SHARDED:
        return jnp.transpose(g, (1, 0, 2)).reshape(g.shape[1], N_DEV * g.shape[2])
    return g.reshape(N_DEV * g.shape[1], g.shape[2])


def _to_slabs(name, g):
    if name in COL_SHARDED:
        return jnp.transpose(g.reshape(g.shape[0], N_DEV, g.shape[1] // N_DEV), (1, 0, 2))
    return g.reshape(N_DEV, g.shape[0] // N_DEV, g.shape[1])


def _columns(m, a, b):
    if m.ndim == 2:
        return m[:, a:b]
    c = m.shape[2]
    cuts = [m[j][:, max(a - j * c, 0):min(b - j * c, c)] for j in range(a // c, (b - 1) // c + 1)]
    return cuts[0] if len(cuts) == 1 else jnp.concatenate(cuts, axis=1)


def _column_shards(pieces, c):
    shards = []
    for j in range(N_DEV):
        cuts = []
        for start, arr in pieces:
            lo, hi = max(j * c - start, 0), min((j + 1) * c - start, arr.shape[1])
            if lo < hi:
                cuts.append(arr[:, lo:hi])
        shards.append(cuts[0] if len(cuts) == 1 else jnp.concatenate(cuts, axis=1))
    return jnp.stack(shards)


def local_step(x, target, w, late=None):
    n_seq, seq, _ = x.shape
    rows = n_seq * seq
    x = x.reshape(rows, D_MODEL)
    target = target.reshape(rows, D_MODEL)
    mx = lambda a: a.astype(MXU)

    splits = [sum(IN_WIDTHS[:i]) for i in range(len(IN_WIDTHS) + 1)]
    w_in = w["w_in"]
    part = lambda i: _columns(w_in, splits[i], splits[i + 1])
    w_z, w_xbc, w_gs, w_ga = mx(part(0)), mx(part(1)), mx(part(6)), mx(part(7))
    w_dt = mx(_pad_lanes(part(2)))
    head_group = lambda t, g: (splits[3 + t] + g * ATT_OUT, splits[3 + t] + (g + 1) * ATT_OUT)
    w_qkv = [mx(jnp.concatenate([_columns(w_in, *head_group(t, g)) for t in range(3)], axis=1))
             for g in range(ATT_GROUPS)]
    conv_w, conv_b, fconv_b = w["ssd_conv_w"], w["ssd_conv_b"], w["ffn_conv_b"]
    dt_bias, a_log, d_skip = _pad_lanes(w["dt_bias"]), _pad_lanes(w["a_log"]), _pad_lanes(w["d_skip"])
    g1, g2, gn, gq, gk = w["norm1_g"], w["norm2_g"], w["ssd_norm_g"], w["q_norm_g"], w["k_norm_g"]

    tb = min(512, seq)
    tbm = min(256, seq)
    cw = 1024
    rw = lambda fn, name, ncol, ins, params=(), outs=(), accs=(), tb_=tb: rowwise(
        fn, name, rows, seq, tb_, ncol, ins, params, outs, accs)

    (h,) = rw(lambda ctx, xv, g: _rms_fwd(xv, g), "rms1_fwd", 1, [(x, D_MODEL, 0, None)], [(g1, None, 0)],
              [(D_MODEL, D_MODEL, 0, MXU)])
    z = matmul(h, w_z, "mm_z")
    xbc = matmul(h, w_xbc, "mm_xbc")
    dtraw = matmul(h, w_dt, "mm_dt")
    by_residue = lambda a, g: _by_residue(a, n_seq, seq, ATT_DILATIONS[g])
    by_token = lambda a, g: _by_token(a, n_seq, seq, ATT_DILATIONS[g])
    h_res = [by_residue(h, g) for g in range(ATT_GROUPS)]
    gq_t, gk_t = jnp.tile(gq, (1, ATT_H)), jnp.tile(gk, (1, ATT_H))
    qkv_gains = jnp.concatenate([gq_t * ATT_SCALE, gk_t, jnp.ones_like(gk_t)], axis=1)

    def qk_norm(r, gains):
        q, k = r[:, :ATT_OUT], r[:, ATT_OUT:2 * ATT_OUT]
        return r, jnp.concatenate([q * _head_rstd(q), k * _head_rstd(k), r[:, 2 * ATT_OUT:]], axis=1) * gains

    qkv, nq = zip(*[matmul(h_res[g], w_qkv[g], f"mm_qkv{g}", tail=_Tail(qk_norm, rows=[qkv_gains], outs=[F32, MXU]))
                    for g in range(ATT_GROUPS)])
    gs = matmul(h, w_gs, "mm_gs")
    ga = matmul(h, w_ga, "mm_ga")

    def conv_silu(ctx, xh, wv, bv):
        return _silu(bv + _conv_prev(xh[0], xh[1], wv, ctx.first, SSD_CONV))

    (xact,) = rw(conv_silu, "ssd_conv_fwd", CONV_DIM // cw, [(xbc, cw, 0, "prev")],
                 [(conv_w, cw, 0), (conv_b, cw, 0)], [(CONV_DIM, cw, 0, F32)])
    if late is None:
        y, sin = ssd_fwd(xact, dtraw, dt_bias, a_log, d_skip, n_seq, seq)
    else:
        y, sin, *gathered = ssd_fwd(xact, dtraw, dt_bias, a_log, d_skip, n_seq, seq,
                                    comm=direct_exchange([(late[n], "gather") for n in LATE]))
        w = {**w, **{n: g if n == "w_up" else _from_gathered(n, g) for n, g in zip(LATE, gathered)}}
    w_sp, w_ap, w_o, w_d = mx(w["w_ssd_proj"]), mx(w["w_attn_proj"]), mx(w["w_out"]), mx(w["w_down"])
    w_ug, w_uv = mx(_columns(w["w_up"], 0, D_FF)), mx(_columns(w["w_up"], D_FF, 2 * D_FF))
    fconv_w = w["ffn_conv_w"]

    def gated_norm(ctx, yv, zv, g):
        yz = yv * _silu(zv)
        return jnp.concatenate([_rms_fwd(yz[:, i:i + NORM_GROUP], g[:, i:i + NORM_GROUP])
                                for i in range(0, cw, NORM_GROUP)], axis=1)

    (y_ssd,) = rw(gated_norm, "ssd_post_fwd", D_INNER // cw, [(y, cw, 0, None), (z, cw, 0, None)], [(gn, cw, 0)],
                  [(D_INNER, cw, 0, MXU)])

    att = [attn_fwd(nq[g], n_seq, seq, ATT_DILATIONS[g], f"attn_fwd{g}") for g in range(ATT_GROUPS)]

    def combine(ctx, o0, o1, o2, l0, l1, l2):
        mxl = jnp.maximum(jnp.maximum(l0, l1), l2)
        e = [jnp.exp(l - mxl) for l in (l0, l1, l2)]
        inv = 1.0 / (e[0] + e[1] + e[2])
        ws = [ei * inv for ei in e]
        out = sum(_expand_heads(wi) * oi for wi, oi in zip(ws, (o0, o1, o2)))
        return (out, *ws)

    y_attn, wt0, wt1, wt2 = rw(
        combine, "attn_combine", 1,
        [(by_token(att[g][0], g), ATT_OUT, 0, None) for g in range(3)]
        + [(by_token(att[g][1], g), LANES, 0, None) for g in range(3)], [],
        [(ATT_OUT, ATT_OUT, 0, F32)] + [(LANES, LANES, 0, F32)] * 3)
    wts = (wt0, wt1, wt2)

    ps = matmul(y_ssd, w_sp, "mm_ssd_proj")
    pa, merged = matmul(y_attn, w_ap, "mm_attn_proj",
                        tail=_Tail(lambda r, a, c, d: (r, _sigmoid(c) * a + _sigmoid(d) * r), like=[ps, gs, ga],
                                   outs=[F32, MXU]))
    x1, h2 = matmul(merged, w_o, "mm_out", add=x,
                    tail=_Tail(lambda r, g: (r, _rms_fwd(r, g)), rows=[g2], outs=[F32, MXU]))
    up_g = matmul(h2, w_ug, "mm_up_g")
    up_v = matmul(h2, w_uv, "mm_up_v")
    fw = D_FF // 2
    nfc = D_FF // fw

    def mlp_act(ctx, ug, uv, wg, wv, bg, bv):
        cg = bg + _conv_prev(ug[0], ug[1], wg, ctx.first, FFN_CONV)
        cv = bv + _conv_prev(uv[0], uv[1], wv, ctx.first, FFN_CONV)
        return _silu(cg) * cv

    (act,) = rw(mlp_act, "mlp_act_fwd", nfc, [(up_g, fw, 0, "prev"), (up_v, fw, 0, "prev")],
                [(fconv_w, fw, 0), (fconv_w, fw, nfc), (fconv_b, fw, 0), (fconv_b, fw, nfc)], [(D_FF, fw, 0, MXU)],
                tb_=tbm)
    def loss_tail(out, tv):
        d = out - tv
        g = d * (1.0 / D_MODEL)
        return g, g, jnp.sum(d * d, axis=0, keepdims=True)

    dx2, dx2_m, sq = matmul(act, w_d, "mm_down", add=x1,
                            tail=_Tail(loss_tail, like=[target], outs=[F32, MXU], n_sums=1))

    grads = {}
    dact = matmul(dx2_m, w_d, "mm_d_act", tb=True)
    grads["w_down"] = matmul(act, dx2_m, "mm_dw_down", ta=True, out_dtype=MXU)

    def mlp_bwd(ctx, da, ug, uv, wg, wv, bg, bv):
        cg, cg_n = _conv_pre(ug, wg, bg, ctx.first, FFN_CONV)
        cv, cv_n = _conv_pre(uv, wv, bv, ctx.first, FFN_CONV)
        da_c, da_n = da
        dup_g_, dwg, dbg = _conv_bwd(da_c * cv * _silu_grad(cg), da_n * cv_n * _silu_grad(cg_n), ug, wg, ctx, FFN_CONV)
        dup_v_, dwv, dbv = _conv_bwd(da_c * _silu(cg), da_n * _silu(cg_n), uv, wv, ctx, FFN_CONV)
        return dup_g_, dup_v_, dwg, dbg, dwv, dbv

    dup_g, dup_v, dfw_g, dfb_g, dfw_v, dfb_v = rw(
        mlp_bwd, "mlp_bwd", nfc, [(dact, fw, 0, "next"), (up_g, fw, 0, "both"), (up_v, fw, 0, "both")],
        [(fconv_w, fw, 0), (fconv_w, fw, nfc), (fconv_b, fw, 0), (fconv_b, fw, nfc)],
        [(D_FF, fw, 0, MXU), (D_FF, fw, 0, MXU)], [(FFN_CONV, fw), (1, fw), (FFN_CONV, fw), (1, fw)], tb_=tbm)
    grads["ffn_conv_w"] = jnp.concatenate([dfw_g, dfw_v], axis=1)
    grads["ffn_conv_b"] = jnp.concatenate([dfb_g, dfb_v], axis=1)
    dh2 = matmul(dup_g, w_ug, "mm_dh2_g", tb=True)
    dh2 = matmul(dup_v, w_uv, "mm_dh2_v", tb=True, add=dh2)
    dw_up = [(0, matmul(h2, dup_g, "mm_dw_up_g", ta=True, out_dtype=MXU)),
             (D_FF, matmul(h2, dup_v, "mm_dw_up_v", ta=True, out_dtype=MXU))]
    if w["w_up"].ndim == 3:
        grads["w_up"] = _column_shards(dw_up, w["w_up"].shape[2])
    else:
        grads["w_up"] = jnp.concatenate([p for _, p in dw_up], axis=1)

    def rms_bwd_fn(ctx, xv, dh_, dres, g):
        dxv, dg = _rms_bwd(xv, g, dh_)
        return dres + dxv, dg

    def rms_bwd_fn2(ctx, xv, dh_, dres, g):
        dxv, dg = rms_bwd_fn(ctx, xv, dh_, dres, g)
        return dxv, dxv, dg

    dx1, dx1_m, grads["norm2_g"] = rw(
        rms_bwd_fn2, "rms2_bwd", 1, [(x1, D_MODEL, 0, None), (dh2, D_MODEL, 0, None), (dx2, D_MODEL, 0, None)],
        [(g2, None, 0)], [(D_MODEL, D_MODEL, 0, F32), (D_MODEL, D_MODEL, 0, MXU)], [(1, D_MODEL)])

    def merge_bwd(dm, a, b, c, d):
        sc, sd = _sigmoid(c), _sigmoid(d)
        return dm * sc, dm * sd, dm * a * sc * (1.0 - sc), dm * b * sd * (1.0 - sd)

    dps, dpa, dgs, dga = matmul(dx1_m, w_o, "mm_d_merged", tb=True,
                                tail=_Tail(merge_bwd, like=[ps, pa, gs, ga], outs=[MXU] * 4))
    grads["w_out"] = matmul(merged, dx1_m, "mm_dw_out", ta=True, out_dtype=MXU)

    def gated_norm_bwd(dyn, yv, zv, g):
        sz = _silu(zv)
        yz = yv * sz
        dyz, dgs_ = [], []
        for i in range(0, D_INNER, NORM_GROUP):
            a, b = _rms_bwd(yz[:, i:i + NORM_GROUP], g[:, i:i + NORM_GROUP], dyn[:, i:i + NORM_GROUP])
            dyz.append(a)
            dgs_.append(b)
        dyz = jnp.concatenate(dyz, axis=1)
        return dyz * sz, dyz * yv * _silu_grad(zv), jnp.concatenate(dgs_, axis=1)

    dy, dz, grads["ssd_norm_g"] = matmul(dps, w_sp, "mm_d_y_ssd", tb=True,
                                         tail=_Tail(gated_norm_bwd, like=[y, z], rows=[gn], outs=[F32, MXU], n_sums=1))
    grads["w_ssd_proj"] = matmul(y_ssd, dps, "mm_dw_ssd_proj", ta=True, out_dtype=MXU)
    dy_attn = matmul(dpa, w_ap, "mm_d_y_attn", tb=True)
    grads["w_attn_proj"] = matmul(y_attn, dpa, "mm_dw_attn_proj", ta=True, out_dtype=MXU)

    (rsum,) = rw(lambda ctx, a, b: _reduce_heads(a * b), "attn_rsum", 1,
                 [(dy_attn, ATT_OUT, 0, None), (y_attn, ATT_OUT, 0, None)], [], [(LANES, LANES, 0, F32)])
    dqkv, dgq, dgk = [], 0.0, 0.0
    for g in range(ATT_GROUPS):
        dn = attn_bwd(nq[g], by_residue(dy_attn, g), att[g][1], by_residue(wts[g], g), by_residue(rsum, g), n_seq, seq,
                      ATT_DILATIONS[g], f"attn_bwd{g}")
        d_, a_, b_ = qk_post(qkv[g], dn, gq_t, gk_t, rows, f"qk_post{g}")
        dqkv.append(d_)
        dgq, dgk = dgq + a_, dgk + b_
    per_head = lambda v: jnp.sum(v.reshape(ATT_H, ATT_HD), axis=0, keepdims=True)
    grads["q_norm_g"], grads["k_norm_g"] = per_head(dgq), per_head(dgk)

    if late is None:
        dxact, ddt, dbias, dalog, ddskip = ssd_bwd(xact, dtraw, dt_bias, a_log, d_skip, sin, dy, n_seq, seq)
    else:
        dxact, ddt, dbias, dalog, ddskip, *parts = ssd_bwd(
            xact, dtraw, dt_bias, a_log, d_skip, sin, dy, n_seq, seq,
            comm=direct_exchange([(grads[n] if n == "w_up" else _to_slabs(n, _narrow(n, grads[n])), "scatter")
                                  for n in LATE]))
        grads.update(zip(LATE, parts))
    grads["dt_bias"], grads["a_log"], grads["d_skip"] = dbias[:, :SSD_H], dalog[:, :SSD_H], ddskip[:, :SSD_H]

    def conv_silu_bwd(ctx, dxa, xin, wv, bv):
        pre, pre_n = _conv_pre(xin, wv, bv, ctx.first, SSD_CONV)
        return _conv_bwd(dxa[0] * _silu_grad(pre), dxa[1] * _silu_grad(pre_n), xin, wv, ctx, SSD_CONV)

    dxbc, grads["ssd_conv_w"], grads["ssd_conv_b"] = rw(
        conv_silu_bwd, "ssd_conv_bwd", CONV_DIM // cw, [(dxact, cw, 0, "next"), (xbc, cw, 0, "both")],
        [(conv_w, cw, 0), (conv_b, cw, 0)], [(CONV_DIM, cw, 0, MXU)], [(SSD_CONV, cw), (1, cw)])

    pieces = [(d_, d_, h, w_, tag) for d_, w_, tag in
              ((dz, w_z, "z"), (dxbc, w_xbc, "xbc"), (ddt, w_dt, "dt"), (dgs, w_gs, "gs"), (dga, w_ga, "ga"))]
    pieces += [(by_token(dqkv[g], g), dqkv[g], h_res[g], w_qkv[g], f"qkv{g}") for g in range(ATT_GROUPS)]
    dws = {tag: matmul(h_in, dpart_h, f"mm_dw_{tag}", ta=True, out_dtype=MXU) for _, dpart_h, h_in, _, tag in pieces}
    dw_in = [(splits[0], dws["z"]), (splits[1], dws["xbc"]), (splits[2], dws["dt"][:, :SSD_H])]
    dw_in += [(head_group(t, g)[0], dws[f"qkv{g}"][:, t * ATT_OUT:(t + 1) * ATT_OUT])
              for t in range(3) for g in range(ATT_GROUPS)]
    dw_in += [(splits[6], dws["gs"]), (splits[7], dws["ga"])]
    if w_in.ndim == 3:
        grads["w_in"] = _column_shards(dw_in, w_in.shape[2])
    else:
        grads["w_in"] = jnp.concatenate([p for _, p in dw_in], axis=1)
    dh = None
    for idx, (dpart, _, _, wpart, tag) in enumerate(pieces):
        comm = NO_EXCHANGE
        if late is not None and idx == 0:
            slabs = [grads[n] if n == "w_in" else _to_slabs(n, _narrow(n, grads[n])) for n in EARLY]
            comm = sibling_exchange(slabs)
        if late is not None and idx == 1:
            core = lax.axis_index("c")
            own = [lax.dynamic_index_in_dim(s.reshape((N_CHIPS, 2) + s.shape[1:]), core, axis=1, keepdims=False)
                   for s in slabs]
            comm = chip_exchange([pair_add(a_, b_, f"rs_add_{n}") for n, a_, b_ in zip(EARLY, own, arrived)])
        dh = matmul(dpart, wpart, f"mm_dh_{tag}", tb=True, add=dh, comm=comm)
        if comm.n:
            dh, *arrived = dh
            if idx == 1:
                grads.update(zip(EARLY, arrived))
    grad_x, grads["norm1_g"] = rw(rms_bwd_fn, "rms1_bwd", 1,
                                  [(x, D_MODEL, 0, None), (dh, D_MODEL, 0, None), (dx1, D_MODEL, 0, None)],
                                  [(g1, None, 0)], [(D_MODEL, D_MODEL, 0, F32)], [(1, D_MODEL)])
    return sq, grad_x.reshape(n_seq, seq, D_MODEL), grads


EARLY = ("w_in", "ssd_conv_w")
REPLICATED = ("norm1_g", "ssd_conv_b", "dt_bias", "a_log", "d_skip", "ssd_norm_g", "q_norm_g", "k_norm_g",
              "norm2_g", "ffn_conv_b")
WEIGHTS = ("norm1_g", "w_in", "ssd_conv_w", "ssd_conv_b", "dt_bias", "a_log", "d_skip", "ssd_norm_g", "w_ssd_proj",
           "q_norm_g", "k_norm_g", "w_attn_proj", "w_out", "norm2_g", "w_up", "ffn_conv_w", "ffn_conv_b", "w_down")
PACK_ROWS, PACK_COLS = 8, 2048


def _pack(vals):
    flat = jnp.concatenate([vals[n].reshape(-1) for n in REPLICATED])
    return jnp.pad(flat, (0, PACK_ROWS * PACK_COLS - flat.shape[0])).reshape(PACK_ROWS, PACK_COLS)


def _unpack(packed, like):
    flat = packed.reshape(-1)
    out, pos = {}, 0
    for n in REPLICATED:
        size = like[n].size
        out[n] = flat[pos:pos + size].reshape(like[n].shape)
        pos += size
    return out


def step(x, target, w_raw, m_raw, v_raw):
    wsh = {n: a[0] if a.ndim == 3 else a for n, a in w_raw.items()}
    gathered = all_gather([_narrow(n, wsh[n]) for n in EARLY], "ag_weights")
    full = {n: wsh[n] for n in REPLICATED}
    full.update({n: g if n == "w_in" else _from_gathered(n, g) for n, g in zip(EARLY, gathered)})

    sq, grad_x, grads = local_step(x, target, full, late={n: _narrow(n, wsh[n]) for n in LATE})

    (small,) = all_gather([_pack({n: grads[n] for n in REPLICATED})], "ag_small")

    out_g, out_d, out_m, out_v = {}, {}, {}, {}
    for n in EARLY + LATE:
        out_g[n], out_d[n], out_m[n], out_v[n] = adamw(grads[n], w_raw[n], m_raw[n], v_raw[n], f"adamw_{n}")
    pk = adamw(small, _pack(w_raw), _pack(m_raw), _pack(v_raw), "adamw_small")
    for dst, packed in zip((out_g, out_d, out_m, out_v), pk):
        dst.update(_unpack(packed, w_raw))
    loss = lax.psum(0.5 * jnp.sum(sq) / D_MODEL, ("x", "y", "c"))
    return loss, grad_x, out_g, out_d, out_m, out_v


def kernel(x, norm1_g, w_in, ssd_conv_w, ssd_conv_b, dt_bias, a_log, d_skip, ssd_norm_g, w_ssd_proj, q_norm_g, k_norm_g, w_attn_proj, w_out, norm2_g, w_up, ffn_conv_w, ffn_conv_b, w_down, loss_target, m_norm1_g, m_w_in, m_ssd_conv_w, m_ssd_conv_b, m_dt_bias, m_a_log, m_d_skip, m_ssd_norm_g, m_w_ssd_proj, m_q_norm_g, m_k_norm_g, m_w_attn_proj, m_w_out, m_norm2_g, m_w_up, m_ffn_conv_w, m_ffn_conv_b, m_w_down, v_norm1_g, v_w_in, v_ssd_conv_w, v_ssd_conv_b, v_dt_bias, v_a_log, v_d_skip, v_ssd_norm_g, v_w_ssd_proj, v_q_norm_g, v_k_norm_g, v_w_attn_proj, v_w_out, v_norm2_g, v_w_up, v_ffn_conv_w, v_ffn_conv_b, v_w_down):
    ws = (norm1_g, w_in, ssd_conv_w, ssd_conv_b, dt_bias, a_log, d_skip, ssd_norm_g, w_ssd_proj, q_norm_g, k_norm_g,
          w_attn_proj, w_out, norm2_g, w_up, ffn_conv_w, ffn_conv_b, w_down)
    ms = (m_norm1_g, m_w_in, m_ssd_conv_w, m_ssd_conv_b, m_dt_bias, m_a_log, m_d_skip, m_ssd_norm_g, m_w_ssd_proj,
          m_q_norm_g, m_k_norm_g, m_w_attn_proj, m_w_out, m_norm2_g, m_w_up, m_ffn_conv_w, m_ffn_conv_b, m_w_down)
    vs = (v_norm1_g, v_w_in, v_ssd_conv_w, v_ssd_conv_b, v_dt_bias, v_a_log, v_d_skip, v_ssd_norm_g, v_w_ssd_proj,
          v_q_norm_g, v_k_norm_g, v_w_attn_proj, v_w_out, v_norm2_g, v_w_up, v_ffn_conv_w, v_ffn_conv_b, v_w_down)
    loss, grad_x, g, d, m, v = step(x, loss_target, dict(zip(WEIGHTS, ws)), dict(zip(WEIGHTS, ms)), dict(zip(WEIGHTS, vs)))
    ordered = lambda dct: [dct[n] for n in WEIGHTS]
    return (loss, grad_x, *ordered(g), *ordered(d), *ordered(m), *ordered(v))
```

```python
import functools

import jax
import jax.numpy as jnp
from jax import lax
from jax.experimental import pallas as pl
from jax.experimental.pallas import tpu as pltpu

F32 = jnp.float32
BF16 = jnp.bfloat16
MXU = jnp.bfloat16
HIGHEST = lax.Precision.HIGHEST
VMEM_LIMIT_BYTES = 48 * 1024 * 1024
SUBLANES = 8
LANES = 128
N_DEV = 8

D_MODEL = 1024
D_INNER = 2048
SSD_P = 64
SSD_H = 32
SSD_G = 8
SSD_K = SSD_H // SSD_G
SSD_N = 128
SSD_Q = 128
SSD_CONV = 4
CONV_DIM = D_INNER + 2 * SSD_G * SSD_N
NORM_GROUP = D_INNER // SSD_G
ATT_GROUPS = 3
ATT_H = 8
ATT_HD = 64
ATT_BLK = 128
ATT_OUT = ATT_H * ATT_HD
ATT_DILATIONS = (1, 4, 16)
ATT_SCALE = ATT_HD ** -0.5
D_FF = 2816
FFN_CONV = 3
EPS = 1e-6
NEG = -1e30
IN_WIDTHS = (D_INNER, CONV_DIM, SSD_H, 3 * ATT_OUT, 3 * ATT_OUT, 3 * ATT_OUT, D_MODEL, D_MODEL)

ADAM_LR = 0.001
ADAM_B1 = 0.9
ADAM_B2 = 0.999
ADAM_EPS = 1e-08
ADAM_WD = 0.01
ADAM_STEP = 10


def _mm(a, b, dims):
    return lax.dot_general(a.astype(MXU), b.astype(MXU), (dims, ((), ())), preferred_element_type=F32)


def _dot_nn(a, b):
    return _mm(a, b, ((1,), (0,)))


def _dot_nt(a, b):
    return _mm(a, b, ((1,), (1,)))


def _dot_tn(a, b):
    return _mm(a, b, ((0,), (0,)))


def _dot_f32(a, b):
    return lax.dot_general(a, b, (((1,), (0,)), ((), ())), precision=HIGHEST, preferred_element_type=F32)


def _sigmoid(x):
    return 1.0 / (1.0 + jnp.exp(-x))


def _silu(x):
    return x * _sigmoid(x)


def _silu_grad(x):
    s = _sigmoid(x)
    return s * (1.0 + x * (1.0 - s))


def _softplus(x):
    return jnp.maximum(x, 0.0) + jnp.log(1.0 + jnp.exp(-jnp.abs(x)))


def _rms_fwd(x, g):
    r = lax.rsqrt(jnp.mean(x * x, axis=-1, keepdims=True) + EPS)
    return x * r * g


def _rms_bwd(x, g, dy):
    r = lax.rsqrt(jnp.mean(x * x, axis=-1, keepdims=True) + EPS)
    xh = x * r
    dyg = dy * g
    dx = r * (dyg - xh * jnp.mean(dyg * xh, axis=-1, keepdims=True))
    return dx, jnp.sum(dy * xh, axis=0, keepdims=True)


def _onehot_row(h, n=LANES):
    return (lax.broadcasted_iota(jnp.int32, (1, n), 1) == h).astype(F32)


def _onehot_col(h, n=LANES):
    return (lax.broadcasted_iota(jnp.int32, (n, 1), 0) == h).astype(F32)


def _head_expand_matrix():
    r = lax.broadcasted_iota(jnp.int32, (LANES, ATT_OUT), 0)
    c = lax.broadcasted_iota(jnp.int32, (LANES, ATT_OUT), 1)
    return (c // ATT_HD == r).astype(F32)


def _split_bf16(x, parts):
    out = []
    for _ in range(parts - 1):
        hi = x.astype(BF16).astype(F32)
        out.append(hi)
        x = x - hi
    out.append(x)
    return out


def _expand_heads(w):
    e = _head_expand_matrix()
    return sum(_dot_nn(p, e) for p in _split_bf16(w, 2))


def _reduce_heads(x):
    e = _head_expand_matrix()
    return sum(_dot_nt(p, e) for p in _split_bf16(x, 3))


def _shift_prev(cur, halo, s, first):
    if s == 0:
        return cur
    rolled = pltpu.roll(cur, s, 0)
    hr = jnp.where(first, 0.0, pltpu.roll(halo, s, 0))
    rows = lax.broadcasted_iota(jnp.int32, halo.shape, 0)
    head = jnp.where(rows < s, hr, rolled[:SUBLANES])
    if cur.shape[0] == SUBLANES:
        return head
    return jnp.concatenate([head, rolled[SUBLANES:]], axis=0)


def _shift_next(cur, halo, s, last):
    if s == 0:
        return cur
    tb = cur.shape[0]
    rolled = pltpu.roll(cur, tb - s, 0)
    hr = jnp.where(last, 0.0, pltpu.roll(halo, SUBLANES - s, 0))
    rows = lax.broadcasted_iota(jnp.int32, halo.shape, 0)
    tail = jnp.where(rows >= SUBLANES - s, hr, rolled[tb - SUBLANES:])
    return jnp.concatenate([rolled[:tb - SUBLANES], tail], axis=0)


def _conv_prev(x, halo, w, first, taps):
    acc = None
    for i in range(taps):
        term = w[i:i + 1, :] * _shift_prev(x, halo, taps - 1 - i, first)
        acc = term if acc is None else acc + term
    return acc


def _conv_pre(x, w, b, first, taps):
    cur, prev8, next8 = x
    tail = cur[cur.shape[0] - SUBLANES:]
    return b + _conv_prev(cur, prev8, w, first, taps), b + _conv_prev(next8, tail, w, False, taps)


def _conv_bwd(dpre, dpre_next8, x, w, ctx, taps):
    cur, prev8, _ = x
    dx, dws = None, []
    for i in range(taps):
        term = w[i:i + 1, :] * _shift_next(dpre, dpre_next8, taps - 1 - i, ctx.last)
        dx = term if dx is None else dx + term
        dws.append(jnp.sum(dpre * _shift_prev(cur, prev8, taps - 1 - i, ctx.first), axis=0, keepdims=True))
    return dx, jnp.concatenate(dws, axis=0), jnp.sum(dpre, axis=0, keepdims=True)


def _params(sem):
    return pltpu.CompilerParams(dimension_semantics=sem, vmem_limit_bytes=VMEM_LIMIT_BYTES)


N_CHIPS = N_DEV // 2
OTHER_CHIPS = (4, 2, 6)


class _Hosted:
    def __init__(self, arrays, out_shape, sems, ops):
        self.arrays, self.out_shape, self.sems, self.ops = list(arrays), list(out_shape), list(sems), ops
        self.n = len(self.arrays)
        self.specs = [pl.BlockSpec(memory_space=pl.ANY)] * self.n

    def begin(self, in_refs, out_refs, sem_refs, first):
        start, finish = self.ops(in_refs, out_refs, *sem_refs)
        pl.when(first)(start)
        return finish


NO_EXCHANGE = _Hosted((), (), (), None)


def _peer(k):
    x, y, c = lax.axis_index("x"), lax.axis_index("y"), lax.axis_index("c")
    px = 1 - x if k & 4 else x
    py = 1 - y if k & 2 else y
    pc = 1 - c if k & 1 else c
    return (px, py, pc), 4 * px + 2 * py + pc, 2 * px + py


def _remote(src, dst, send_sems, recv_sems, t, k, dev):
    return pltpu.make_async_remote_copy(src_ref=src, dst_ref=dst, send_sem=send_sems.at[t, k], recv_sem=recv_sems.at[t, k],
                                        device_id=dev, device_id_type=pl.DeviceIdType.MESH)


def direct_exchange(items):
    n = len(items)

    def ops(in_refs, out_refs, send_sems, recv_sems, local_sems):
        _, me, _ = _peer(0)
        part = lambda t, pid: in_refs[t] if items[t][1] == "gather" else in_refs[t].at[pid]

        def copy(t, k, arriving):
            dev, pid, _ = _peer(k)
            return _remote(part(t, pid), out_refs[t].at[pid if arriving else me], send_sems, recv_sems, t, k, dev)

        def own(t):
            return pltpu.make_async_copy(part(t, me), out_refs[t].at[me], local_sems.at[t])

        def start():
            for t in range(n):
                own(t).start()
                for k in range(1, N_DEV):
                    copy(t, k, False).start()

        def finish():
            for t in range(n):
                for k in range(1, N_DEV):
                    copy(t, k, True).wait_recv()
            for t in range(n):
                for k in range(1, N_DEV):
                    copy(t, k, False).wait_send()
                own(t).wait()

        return start, finish

    out_shape = [jax.ShapeDtypeStruct((N_DEV,) + tuple(a.shape if m == "gather" else a.shape[1:]), a.dtype)
                 for a, m in items]
    sems = [pltpu.SemaphoreType.DMA((n, N_DEV)), pltpu.SemaphoreType.DMA((n, N_DEV)), pltpu.SemaphoreType.DMA((n,))]
    return _Hosted([a for a, _ in items], out_shape, sems, ops)


def sibling_exchange(arrays):
    n = len(arrays)

    def ops(in_refs, out_refs, send_sems, recv_sems):
        sib, _, _ = _peer(1)
        c = lax.axis_index("c")
        copy = lambda t, q: _remote(in_refs[t].at[2 * q + (1 - c)], out_refs[t].at[q], send_sems, recv_sems, t, q, sib)

        def start():
            for t in range(n):
                for q in range(N_CHIPS):
                    copy(t, q).start()

        def finish():
            for t in range(n):
                for q in range(N_CHIPS):
                    copy(t, q).wait_recv()
            for t in range(n):
                for q in range(N_CHIPS):
                    copy(t, q).wait_send()

        return start, finish

    out_shape = [jax.ShapeDtypeStruct((N_CHIPS,) + a.shape[1:], a.dtype) for a in arrays]
    sems = [pltpu.SemaphoreType.DMA((n, N_CHIPS)), pltpu.SemaphoreType.DMA((n, N_CHIPS))]
    return _Hosted(arrays, out_shape, sems, ops)


def chip_exchange(arrays):
    n = len(arrays)

    def ops(in_refs, out_refs, send_sems, recv_sems, local_sems):
        _, _, mine = _peer(0)

        def copy(t, k, arriving):
            dev, _, q = _peer(k)
            return _remote(in_refs[t].at[q], out_refs[t].at[q if arriving else mine], send_sems, recv_sems, t, k, dev)

        def own(t):
            return pltpu.make_async_copy(in_refs[t].at[mine], out_refs[t].at[mine], local_sems.at[t])

        def start():
            for t in range(n):
                own(t).start()
                for k in OTHER_CHIPS:
                    copy(t, k, False).start()

        def finish():
            for t in range(n):
                for k in OTHER_CHIPS:
                    copy(t, k, True).wait_recv()
            for t in range(n):
                for k in OTHER_CHIPS:
                    copy(t, k, False).wait_send()
                own(t).wait()

        return start, finish

    out_shape = [jax.ShapeDtypeStruct(a.shape, a.dtype) for a in arrays]
    sems = [pltpu.SemaphoreType.DMA((n, N_DEV)), pltpu.SemaphoreType.DMA((n, N_DEV)), pltpu.SemaphoreType.DMA((n,))]
    return _Hosted(arrays, out_shape, sems, ops)


MATMUL_VMEM_BUDGET = 34 * 1024 * 1024


V7X_MXU_FLOPS = 996e12
V7X_HBM_BYTES_PER_S = 3.4e12
GRID_STEP_S = 0.35e-6


def _tile_sizes(dim, cap):
    return [t for t in range(LANES, min(dim, cap) + 1, LANES) if dim % t == 0] or [dim]


def _matmul_tiles(m, n, k, a_bytes, b_bytes, add_bytes, out_bytes, whole_rows=False):
    best = None
    for tk in _tile_sizes(k, 8192):
        nk = k // tk
        for tn in ([n] if whole_rows else _tile_sizes(n, 2048)):
            for tm in _tile_sizes(m, 2048):
                io = tm * tk * a_bytes + tk * tn * b_bytes
                ends = tm * tn * (add_bytes + out_bytes)
                need = 2 * (io + ends) + tm * tn * 4 * (2 if nk > 1 else 1)
                if need > MATMUL_VMEM_BUDGET:
                    continue
                step = max(2.0 * tm * tn * tk / V7X_MXU_FLOPS, (io + ends / nk) / V7X_HBM_BYTES_PER_S)
                if nk > 1:
                    step += tm * tn * 8 / V7X_HBM_BYTES_PER_S
                cost = (m // tm) * (n // tn) * nk * (step + GRID_STEP_S)
                if best is None or cost < best[0]:
                    best = (cost, tm, tn, tk)
    if best is None:
        raise ValueError((m, n, k))
    return best[1:]


class _Tail:
    def __init__(self, fn, like=(), rows=(), outs=(), n_sums=0):
        self.fn, self.like, self.rows, self.outs, self.n_sums = fn, list(like), list(rows), list(outs), n_sums


def matmul(a, b, name, ta=False, tb=False, add=None, out_dtype=F32, comm=NO_EXCHANGE, tail=None):
    assert not (ta and tb)
    m, k = (a.shape[1], a.shape[0]) if ta else a.shape
    n = b.shape[0] if tb else b.shape[1]
    assert (b.shape[1] if tb else b.shape[0]) == k
    like = ([] if add is None else [add]) + (tail.like if tail else [])
    rows = tail.rows if tail else []
    outs = tail.outs if tail else [out_dtype]
    n_sums = tail.n_sums if tail else 0
    tm, tn, tk = _matmul_tiles(m, n, k, a.dtype.itemsize, b.dtype.itemsize, sum(x.dtype.itemsize for x in like),
                               sum(jnp.dtype(d).itemsize for d in outs), whole_rows=tail is not None)
    nk = k // tk
    grid = (m // tm, n // tn, nk)
    dims = ((0,), (0,)) if ta else (((1,), (1,)) if tb else ((1,), (0,)))
    n_in = 2 + len(like) + len(rows)
    n_out = len(outs) + n_sums
    n_acc = 0 if nk == 1 else 1

    def body(*refs):
        a_ref, b_ref = refs[:2]
        like_refs, row_refs = refs[2:2 + len(like)], refs[2 + len(like):n_in]
        out_refs = refs[n_in + comm.n:n_in + comm.n + n_out]
        ids = [pl.program_id(d) for d in range(3)]
        if comm.n:
            first = functools.reduce(jnp.logical_and, [i == 0 for i in ids])
            last = functools.reduce(jnp.logical_and, [i == g - 1 for i, g in zip(ids, grid)])
            done = comm.begin(refs[n_in:n_in + comm.n], refs[n_in + comm.n + n_out:n_in + 2 * comm.n + n_out],
                              refs[n_in + 2 * comm.n + n_out + n_acc:], first)

        def finish(r):
            if add is not None:
                r = r + like_refs[0][...].astype(F32)
            if tail is None:
                out_refs[0][...] = r.astype(out_dtype)
                return
            vals = tail.fn(r, *[x[...] for x in like_refs[len(like) - len(tail.like):]], *[x[...] for x in row_refs])
            for ref, val in zip(out_refs[:len(outs)], vals):
                ref[...] = val.astype(ref.dtype)
            for ref, val in zip(out_refs[len(outs):], vals[len(outs):]):
                @pl.when(ids[0] == 0)
                def _(ref=ref, val=val):
                    ref[...] = val

                @pl.when(ids[0] != 0)
                def _(ref=ref, val=val):
                    ref[...] += val

        if nk == 1:
            finish(_mm(a_ref[...], b_ref[...], dims))
        else:
            acc = refs[n_in + 2 * comm.n + n_out]

            @pl.when(ids[2] == 0)
            def _():
                acc[...] = jnp.zeros_like(acc)

            acc[...] += _mm(a_ref[...], b_ref[...], dims)

            @pl.when(ids[2] == nk - 1)
            def _():
                finish(acc[...])

        if comm.n:
            pl.when(last)(done)

    a_spec = pl.BlockSpec((tk, tm), lambda i, j, kk: (kk, i)) if ta else pl.BlockSpec((tm, tk), lambda i, j, kk: (i, kk))
    b_spec = pl.BlockSpec((tn, tk), lambda i, j, kk: (j, kk)) if tb else pl.BlockSpec((tk, tn), lambda i, j, kk: (kk, j))
    tile = pl.BlockSpec((tm, tn), lambda i, j, kk: (i, j))
    row = pl.BlockSpec((1, tn), lambda i, j, kk: (0, j))
    sequential = comm.n or n_sums
    res = pl.pallas_call(
        body, name=name,
        grid=grid,
        in_specs=[a_spec, b_spec] + [tile] * len(like) + [row] * len(rows) + comm.specs,
        out_specs=[tile] * len(outs) + [row] * n_sums + comm.specs,
        out_shape=[jax.ShapeDtypeStruct((m, n), d) for d in outs] + [jax.ShapeDtypeStruct((1, n), F32)] * n_sums
        + comm.out_shape,
        scratch_shapes=([] if nk == 1 else [pltpu.VMEM((tm, tn), F32)]) + comm.sems,
        compiler_params=_params(("arbitrary",) * 3 if sequential else ("parallel", "parallel", "arbitrary")),
    )(a, b, *like, *rows, *comm.arrays)
    return res if (comm.n or tail) else res[0]


class _Ctx:
    def __init__(self, first, last):
        self.first = first
        self.last = last


def rowwise(fn, name, rows, seq, tb, ncol, ins, params=(), outs=(), accs=()):
    assert rows % tb == 0 and seq % tb == 0 and tb % 16 == 0
    bps = seq // tb
    nrow = rows // tb
    r8 = tb // SUBLANES
    args, in_specs = [], []
    for arr, w, off, halo in ins:
        args.append(arr)
        in_specs.append(pl.BlockSpec((tb, w), lambda j, i, off=off: (i, off + j)))
        if halo in ("prev", "both"):
            args.append(arr)
            in_specs.append(pl.BlockSpec((SUBLANES, w), lambda j, i, off=off: (jnp.maximum(i * r8 - 1, 0), off + j)))
        if halo in ("next", "both"):
            args.append(arr)
            in_specs.append(pl.BlockSpec(
                (SUBLANES, w), lambda j, i, off=off: (jnp.minimum((i + 1) * r8, rows // SUBLANES - 1), off + j)))
    for arr, w, off in params:
        args.append(arr)
        if w is None:
            in_specs.append(pl.BlockSpec(arr.shape, lambda j, i: (0, 0)))
        else:
            in_specs.append(pl.BlockSpec((arr.shape[0], w), lambda j, i, off=off: (0, off + j)))
    out_shape, out_specs = [], []
    for total, w, off, dt in outs:
        out_shape.append(jax.ShapeDtypeStruct((rows, total), dt))
        out_specs.append(pl.BlockSpec((tb, w), lambda j, i, off=off: (i, off + j)))
    for r, w in accs:
        out_shape.append(jax.ShapeDtypeStruct((r, ncol * w), F32))
        out_specs.append(pl.BlockSpec((r, w), lambda j, i: (0, j)))
    n_out, n_acc = len(outs), len(accs)

    def body(*refs):
        i = pl.program_id(1)
        pos = 0
        vals = []
        for _, _, _, halo in ins:
            cur = refs[pos][...]
            pos += 1
            if halo is None:
                vals.append(cur)
            elif halo == "both":
                vals.append((cur, refs[pos][...], refs[pos + 1][...]))
                pos += 2
            else:
                vals.append((cur, refs[pos][...]))
                pos += 1
        for _ in params:
            vals.append(refs[pos][...])
            pos += 1
        ctx = _Ctx(i % bps == 0, i % bps == bps - 1)
        res = fn(ctx, *vals)
        if not isinstance(res, (tuple, list)):
            res = (res,)
        assert len(res) == n_out + n_acc
        for q in range(n_out):
            refs[pos + q][...] = res[q].astype(refs[pos + q].dtype)
        for q in range(n_acc):
            ref, val = refs[pos + n_out + q], res[n_out + q]

            @pl.when(i == 0)
            def _(ref=ref, val=val):
                ref[...] = val

            @pl.when(i != 0)
            def _(ref=ref, val=val):
                ref[...] += val

    res = pl.pallas_call(
        body, name=name,
        grid=(ncol, nrow),
        in_specs=in_specs,
        out_specs=out_specs,
        out_shape=out_shape,
        compiler_params=_params(("parallel", "arbitrary")),
    )(*args)
    return res


GROUP_W = SSD_K * SSD_P


def _tri(lower):
    r = lax.broadcasted_iota(jnp.int32, (SSD_Q, SSD_Q), 0)
    c = lax.broadcasted_iota(jnp.int32, (SSD_Q, SSD_Q), 1)
    return r >= c if lower else r <= c


def _first_head_lanes():
    return lax.broadcasted_iota(jnp.int32, (1, LANES), 1) < SSD_P


def _column(v, j):
    return v[:, j * LANES:(j + 1) * LANES] if v.shape[-1] == GROUP_W else v


def _per_head(vals):
    first = _first_head_lanes()
    return jnp.concatenate([jnp.where(first, _column(vals[2 * j], j), _column(vals[2 * j + 1], j))
                            for j in range(GROUP_W // LANES)], axis=1)


def _per_head_rows(vals):
    return jnp.concatenate([jnp.broadcast_to(v, (SSD_P, 1)) for v in vals], axis=0)


def _own_columns(slab, k):
    keep = _first_head_lanes() if k % 2 == 0 else jnp.logical_not(_first_head_lanes())
    own = jnp.where(keep, _column(slab, k // 2), 0.0)
    return jnp.concatenate([own, jnp.zeros_like(own)] if k < 2 else [jnp.zeros_like(own), own], axis=1)


def _headsum(prod, g):
    out = None
    for k in range(SSD_K):
        keep = _first_head_lanes() if k % 2 == 0 else jnp.logical_not(_first_head_lanes())
        term = jnp.sum(jnp.where(keep, _column(prod, k // 2), 0.0), axis=1, keepdims=True) * _onehot_row(g * SSD_K + k)
        out = term if out is None else out + term
    return out


def ssd_fwd(xact, dtraw, dt_bias, a_log, d_skip, n_seq, seq, comm=NO_EXCHANGE):
    nc = seq // SSD_Q
    rows = n_seq * seq
    nx = comm.n

    def body(*refs):
        xact_ref, dtraw_ref, bias_ref, alog_ref, dskip_ref = refs[:5]
        y_ref, sin_ref = refs[5 + nx:7 + nx]
        state, cs_s, cst_s, dt_s = refs[7 + 2 * nx:11 + 2 * nx]
        b, c = pl.program_id(0), pl.program_id(1)
        if nx:
            finish = comm.begin(refs[5:5 + nx], refs[7 + nx:7 + 2 * nx], refs[11 + 2 * nx:],
                                jnp.logical_and(b == 0, c == 0))

        @pl.when(c == 0)
        def _():
            state[...] = jnp.zeros_like(state)

        sin_ref[0] = state[...]
        dt = _softplus(dtraw_ref[...] + bias_ref[...])
        a = dt * (-jnp.exp(alog_ref[...]))
        cs = _dot_f32(_tri(True).astype(F32), a)
        cs_s[...] = cs
        cst_s[...] = cs.T
        dt_s[...] = dt
        causal = _tri(True)
        def front(g):
            heads = [g * SSD_K + k for k in range(SSD_K)]
            bg = xact_ref[:, pl.ds(D_INNER + g * SSD_N, SSD_N)]
            cg = xact_ref[:, pl.ds(D_INNER + (SSD_G + g) * SSD_N, SSD_N)]
            xg = xact_ref[:, pl.ds(g * GROUP_W, GROUP_W)]
            cols = [cs_s[:, pl.ds(h, 1)] for h in heads]
            lasts = [cs_s[pl.ds(SSD_Q - 1, 1), pl.ds(h, 1)] for h in heads]
            xdg = xg * _per_head([dt_s[:, pl.ds(h, 1)] for h in heads])
            sg = state[g]
            y = (_per_head([jnp.exp(c_) for c_ in cols]) * _dot_nt(cg, sg)
                 + _per_head([dskip_ref[:, pl.ds(h, 1)] for h in heads]) * xg)
            w = _per_head([jnp.exp(l_ - c_) for l_, c_ in zip(lasts, cols)])
            state[g] = _per_head_rows([jnp.exp(l_) for l_ in lasts]) * sg + _dot_tn(w * xdg, bg)
            return heads, cols, _dot_nt(cg, bg), xdg, y

        def back(g, heads, cols, gm, xdg, y):
            mats = [gm * jnp.exp(jnp.where(causal, cols[k] - cst_s[pl.ds(h, 1), :], NEG)) for k, h in enumerate(heads)]
            y4 = _dot_nn(jnp.concatenate(mats, axis=0), xdg)
            y_ref[:, pl.ds(g * GROUP_W, GROUP_W)] = y + _per_head([y4[k * SSD_Q:(k + 1) * SSD_Q] for k in range(SSD_K)])

        ahead = front(0)
        for g in range(SSD_G):
            cur, ahead = ahead, (front(g + 1) if g + 1 < SSD_G else None)
            back(g, *cur)
        if nx:
            pl.when(jnp.logical_and(b == n_seq - 1, c == nc - 1))(finish)

    vec = pl.BlockSpec((1, LANES), lambda b, c: (0, 0))
    return pl.pallas_call(
        body, name="ssd_fwd",
        grid=(n_seq, nc),
        in_specs=[pl.BlockSpec((SSD_Q, CONV_DIM), lambda b, c: (b * nc + c, 0)),
                  pl.BlockSpec((SSD_Q, LANES), lambda b, c: (b * nc + c, 0)), vec, vec, vec] + comm.specs,
        out_specs=[pl.BlockSpec((SSD_Q, D_INNER), lambda b, c: (b * nc + c, 0)),
                   pl.BlockSpec((1, SSD_G, GROUP_W, SSD_N), lambda b, c: (b * nc + c, 0, 0, 0))] + comm.specs,
        out_shape=[jax.ShapeDtypeStruct((rows, D_INNER), F32),
                   jax.ShapeDtypeStruct((n_seq * nc, SSD_G, GROUP_W, SSD_N), F32)] + comm.out_shape,
        scratch_shapes=[pltpu.VMEM((SSD_G, GROUP_W, SSD_N), F32), pltpu.VMEM((SSD_Q, LANES), F32),
                        pltpu.VMEM((LANES, SSD_Q), F32), pltpu.VMEM((SSD_Q, LANES), F32)] + comm.sems,
        compiler_params=_params(("arbitrary", "arbitrary")),
    )(xact, dtraw, dt_bias, a_log, d_skip, *comm.arrays)


def ssd_bwd(xact, dtraw, dt_bias, a_log, d_skip, sin, dy, n_seq, seq, comm=NO_EXCHANGE):
    nc = seq // SSD_Q
    rows = n_seq * seq
    nx = comm.n

    def body(*refs):
        xact_ref, dtraw_ref, bias_ref, alog_ref, dskip_ref, sin_ref, dy_ref = refs[:7]
        dx_ref, ddt_ref, dbias_ref, dalog_ref, ddskip_ref = refs[7 + nx:12 + nx]
        dstate, cs_s, cst_s, dt_s = refs[12 + 2 * nx:16 + 2 * nx]
        b, c = pl.program_id(0), pl.program_id(1)
        if nx:
            finish = comm.begin(refs[7:7 + nx], refs[12 + nx:12 + 2 * nx], refs[16 + 2 * nx:],
                                jnp.logical_and(b == 0, c == 0))

        @pl.when(c == 0)
        def _():
            dstate[...] = jnp.zeros_like(dstate)

        pre = dtraw_ref[...] + bias_ref[...]
        dt = _softplus(pre)
        a_neg = -jnp.exp(alog_ref[...])
        cs = _dot_f32(_tri(True).astype(F32), dt * a_neg)
        cs_s[...] = cs
        cst_s[...] = cs.T
        dt_s[...] = dt
        causal, anti = _tri(True), _tri(False)
        is_last_row = lax.broadcasted_iota(jnp.int32, (SSD_Q, 1), 0) == SSD_Q - 1
        dcs_cf = jnp.zeros((SSD_Q, LANES), F32)
        dcs_rf = jnp.zeros((LANES, SSD_Q), F32)
        ddt_cf = jnp.zeros((SSD_Q, LANES), F32)
        dd_vec = jnp.zeros((1, LANES), F32)
        dlast_vec = jnp.zeros((1, LANES), F32)
        def front(g):
            heads = [g * SSD_K + k for k in range(SSD_K)]
            v = {"heads": heads}
            bg = v["bg"] = xact_ref[:, pl.ds(D_INNER + g * SSD_N, SSD_N)]
            cg = v["cg"] = xact_ref[:, pl.ds(D_INNER + (SSD_G + g) * SSD_N, SSD_N)]
            xg = v["xg"] = xact_ref[:, pl.ds(g * GROUP_W, GROUP_W)]
            dyg = v["dyg"] = dy_ref[:, pl.ds(g * GROUP_W, GROUP_W)]
            cols = [cs_s[:, pl.ds(h, 1)] for h in heads]
            rws = [cst_s[pl.ds(h, 1), :] for h in heads]
            lasts = [cs_s[pl.ds(SSD_Q - 1, 1), pl.ds(h, 1)] for h in heads]
            e_lasts = v["e_lasts"] = [jnp.exp(l_) for l_ in lasts]
            v["dtg"] = _per_head([dt_s[:, pl.ds(h, 1)] for h in heads])
            v["dskg"] = _per_head([dskip_ref[:, pl.ds(h, 1)] for h in heads])
            e_col = _per_head([jnp.exp(c_) for c_ in cols])
            w = v["w"] = _per_head([jnp.exp(l_ - c_) for l_, c_ in zip(lasts, cols)])
            xdg = v["xdg"] = xg * v["dtg"]
            sg = sin_ref[0, g]
            dsn = dstate[g]
            v["gm"] = _dot_nt(cg, bg)
            gmt = _dot_nt(bg, cg)
            v["y_off"] = e_col * _dot_nt(cg, sg)
            d_cs = e_col * dyg
            v["dcg"] = _dot_nn(d_cs, sg)
            dstate[g] = _dot_tn(d_cs, cg) + _per_head_rows(e_lasts) * dsn
            v["dbg"] = _dot_nn(w * xdg, dsn)
            v["dtt"] = _dot_nt(bg, dsn)
            v["dsn_s"] = dsn * sg
            segs = [cols[k] - rws[k] for k in range(SSD_K)]
            v["decays"] = [jnp.exp(jnp.where(causal, s_, NEG)) for s_ in segs]
            v["dm4"] = _dot_nt(jnp.concatenate([_own_columns(dyg, k) for k in range(SSD_K)], axis=0), xdg)
            v["z4"] = _dot_nn(jnp.concatenate([gmt * jnp.exp(jnp.where(anti, -s_, NEG)) for s_ in segs], axis=0), dyg)
            return v

        def back(g, v, sums):
            dcs_cf, dcs_rf, ddt_cf, dd_vec, dlast_vec = sums
            dxd = v["w"] * v["dtt"] + _per_head([v["z4"][k * SSD_Q:(k + 1) * SSD_Q] for k in range(SSD_K)])
            dw = _headsum(v["dtt"] * v["xdg"] * v["w"], g)
            dcs_cf = dcs_cf + _headsum(v["dyg"] * v["y_off"], g) - dw
            dlast_vec = dlast_vec + jnp.sum(dw, axis=0, keepdims=True)
            dgm = jnp.zeros((SSD_Q, SSD_Q), F32)
            for k, h in enumerate(v["heads"]):
                dm = v["dm4"][k * SSD_Q:(k + 1) * SSD_Q]
                dseg = dm * v["gm"] * v["decays"][k]
                dgm = dgm + dm * v["decays"][k]
                oh_r = _onehot_row(h)
                dcs_cf = dcs_cf + jnp.sum(dseg, axis=1, keepdims=True) * oh_r
                dcs_rf = dcs_rf - _onehot_col(h) * jnp.sum(dseg, axis=0, keepdims=True)
                dlast_vec = dlast_vec + (jnp.sum(v["dsn_s"][k * SSD_P:(k + 1) * SSD_P], keepdims=True)
                                         * v["e_lasts"][k] * oh_r)
            dx_ref[:, pl.ds(g * GROUP_W, GROUP_W)] = dxd * v["dtg"] + v["dskg"] * v["dyg"]
            ddt_cf = ddt_cf + _headsum(dxd * v["xg"], g)
            dd_vec = dd_vec + _headsum(jnp.sum(v["dyg"] * v["xg"], axis=0, keepdims=True), g)
            dx_ref[:, pl.ds(D_INNER + g * SSD_N, SSD_N)] = v["dbg"] + _dot_tn(dgm, v["cg"])
            dx_ref[:, pl.ds(D_INNER + (SSD_G + g) * SSD_N, SSD_N)] = v["dcg"] + _dot_nn(dgm, v["bg"])
            return dcs_cf, dcs_rf, ddt_cf, dd_vec, dlast_vec

        sums = (dcs_cf, dcs_rf, ddt_cf, dd_vec, dlast_vec)
        ahead = front(0)
        for g in range(SSD_G):
            cur, ahead = ahead, (front(g + 1) if g + 1 < SSD_G else None)
            sums = back(g, cur, sums)
        dcs_cf, dcs_rf, ddt_cf, dd_vec, dlast_vec = sums
        dcs = dcs_cf + dcs_rf.T + jnp.where(is_last_row, dlast_vec, 0.0)
        da = _dot_f32(_tri(False).astype(F32), dcs)
        ddt = ddt_cf + da * a_neg
        ddtraw = ddt * _sigmoid(pre)
        ddt_ref[...] = ddtraw.astype(ddt_ref.dtype)
        dbias = jnp.sum(ddtraw, axis=0, keepdims=True)
        dalog = jnp.sum(da * dt, axis=0, keepdims=True) * a_neg
        first_step = jnp.logical_and(b == 0, c == 0)

        @pl.when(first_step)
        def _():
            dbias_ref[...] = dbias
            dalog_ref[...] = dalog
            ddskip_ref[...] = dd_vec

        @pl.when(jnp.logical_not(first_step))
        def _():
            dbias_ref[...] += dbias
            dalog_ref[...] += dalog
            ddskip_ref[...] += dd_vec

        if nx:
            pl.when(jnp.logical_and(b == n_seq - 1, c == nc - 1))(finish)

    def rowblk(b, c):
        return b * nc + (nc - 1 - c)

    vec = pl.BlockSpec((1, LANES), lambda b, c: (0, 0))
    return pl.pallas_call(
        body, name="ssd_bwd",
        grid=(n_seq, nc),
        in_specs=[pl.BlockSpec((SSD_Q, CONV_DIM), lambda b, c: (rowblk(b, c), 0)),
                  pl.BlockSpec((SSD_Q, LANES), lambda b, c: (rowblk(b, c), 0)), vec, vec, vec,
                  pl.BlockSpec((1, SSD_G, GROUP_W, SSD_N), lambda b, c: (rowblk(b, c), 0, 0, 0)),
                  pl.BlockSpec((SSD_Q, D_INNER), lambda b, c: (rowblk(b, c), 0))] + comm.specs,
        out_specs=[pl.BlockSpec((SSD_Q, CONV_DIM), lambda b, c: (rowblk(b, c), 0)),
                   pl.BlockSpec((SSD_Q, LANES), lambda b, c: (rowblk(b, c), 0)), vec, vec, vec] + comm.specs,
        out_shape=[jax.ShapeDtypeStruct((rows, CONV_DIM), F32), jax.ShapeDtypeStruct((rows, LANES), BF16),
                   jax.ShapeDtypeStruct((1, LANES), F32), jax.ShapeDtypeStruct((1, LANES), F32),
                   jax.ShapeDtypeStruct((1, LANES), F32)] + comm.out_shape,
        scratch_shapes=[pltpu.VMEM((SSD_G, GROUP_W, SSD_N), F32), pltpu.VMEM((SSD_Q, LANES), F32),
                        pltpu.VMEM((LANES, SSD_Q), F32), pltpu.VMEM((SSD_Q, LANES), F32)] + comm.sems,
        compiler_params=_params(("arbitrary", "arbitrary")),
    )(xact, dtraw, dt_bias, a_log, d_skip, sin, dy, *comm.arrays)


QKV_W = 3 * ATT_OUT
PAIR_W = 2 * ATT_HD
HEAD_PAIRS = ATT_H // 2
PREP_ROWS = 512


def _by_residue(a, n_seq, seq, dil):
    if dil == 1:
        return a
    return a.reshape(n_seq, seq // dil, dil, a.shape[1]).transpose(0, 2, 1, 3).reshape(a.shape)


def _by_token(a, n_seq, seq, dil):
    if dil == 1:
        return a
    return a.reshape(n_seq, dil, seq // dil, a.shape[1]).transpose(0, 2, 1, 3).reshape(a.shape)


def _head_sums(x, fn):
    lo = jnp.logical_not(lax.broadcasted_iota(jnp.int32, (1, 2 * ATT_HD), 1) >= ATT_HD)
    parts = []
    for p in range(ATT_H // 2):
        slab = x[:, p * 2 * ATT_HD:(p + 1) * 2 * ATT_HD]
        s_lo = fn(jnp.sum(jnp.where(lo, slab, 0.0), axis=1, keepdims=True))
        s_hi = fn(jnp.sum(jnp.where(lo, 0.0, slab), axis=1, keepdims=True))
        parts.append(jnp.where(lo, s_lo, s_hi))
    return jnp.concatenate(parts, axis=1)


def _head_rstd(x):
    return _head_sums(x * x, lambda s: lax.rsqrt(s * (1.0 / ATT_HD) + EPS))


def _head_rms_bwd(x, g_t, dy):
    r = _head_rstd(x)
    xh = x * r
    dyg = dy * g_t
    mean = _head_sums(dyg * xh, lambda s: s * (1.0 / ATT_HD))
    return r * (dyg - xh * mean), jnp.sum(dy * xh, axis=0, keepdims=True)


def _lane_hi():
    return lax.broadcasted_iota(jnp.int32, (1, PAIR_W), 1) >= ATT_HD


def _band_mask2(first_valid, query_rows):
    i = lax.broadcasted_iota(jnp.int32, (ATT_BLK, 2 * ATT_BLK), 0)
    j = lax.broadcasted_iota(jnp.int32, (ATT_BLK, 2 * ATT_BLK), 1)
    left = j < ATT_BLK
    right = jnp.logical_not(left)
    if query_rows:
        return jnp.logical_or(jnp.logical_and(jnp.logical_and(left, i <= j), first_valid),
                              jnp.logical_and(right, i >= j - ATT_BLK))
    return jnp.logical_or(jnp.logical_and(left, j >= i),
                          jnp.logical_and(jnp.logical_and(right, j - ATT_BLK <= i), first_valid))


def _only_head(slab, hi):
    keep = _lane_hi() if hi else jnp.logical_not(_lane_hi())
    return jnp.where(keep, slab, jnp.zeros_like(slab))


def attn_fwd(nq, n_seq, seq, dil, name):
    nb = seq // dil // ATT_BLK
    rows = n_seq * seq

    def body(cur_ref, prev_ref, o_ref, lse_ref, s_scr, p_scr):
        n = pl.program_id(1)
        mask = _band_mask2(n > 0, True)
        for h in range(ATT_H):
            sl = pl.ds((h // 2) * PAIR_W, PAIR_W)
            ks = pl.ds(ATT_OUT + (h // 2) * PAIR_W, PAIR_W)
            kcat = jnp.concatenate([prev_ref[:, ks], cur_ref[:, ks]], axis=0)
            s_scr[h] = jnp.where(mask, _dot_nt(_only_head(cur_ref[:, sl], h % 2), kcat), NEG)
        s_all = s_scr[...]
        mx = jnp.max(s_all, axis=2, keepdims=True)
        p_all = jnp.exp(s_all - mx)
        den = jnp.sum(p_all, axis=2, keepdims=True)
        p_scr[...] = p_all.astype(p_scr.dtype)
        inv = 1.0 / den
        lse = mx + jnp.log(den)
        lse_blk = jnp.zeros((ATT_BLK, LANES), F32)
        for h in range(ATT_H):
            lse_blk = lse_blk + lse[h] * _onehot_row(h)
        lse_ref[...] = lse_blk
        for pr in range(HEAD_PAIRS):
            vs = pl.ds(2 * ATT_OUT + pr * PAIR_W, PAIR_W)
            vcat = jnp.concatenate([prev_ref[:, vs], cur_ref[:, vs]], axis=0)
            lo = _dot_nn(p_scr[2 * pr], vcat) * inv[2 * pr]
            hi = _dot_nn(p_scr[2 * pr + 1], vcat) * inv[2 * pr + 1]
            o_ref[:, pl.ds(pr * PAIR_W, PAIR_W)] = jnp.where(_lane_hi(), hi, lo)

    def blk(width, shift):
        if shift:
            return pl.BlockSpec((ATT_BLK, width), lambda s, n: (s * nb + jnp.maximum(n - 1, 0), 0))
        return pl.BlockSpec((ATT_BLK, width), lambda s, n: (s * nb + n, 0))

    return pl.pallas_call(
        body, name=name,
        grid=(n_seq * dil, nb),
        in_specs=[blk(QKV_W, 0), blk(QKV_W, -1)],
        out_specs=[blk(ATT_OUT, 0), blk(LANES, 0)],
        out_shape=[jax.ShapeDtypeStruct((rows, ATT_OUT), F32), jax.ShapeDtypeStruct((rows, LANES), F32)],
        scratch_shapes=[pltpu.VMEM((ATT_H, ATT_BLK, 2 * ATT_BLK), F32), pltpu.VMEM((ATT_H, ATT_BLK, 2 * ATT_BLK), MXU)],
        compiler_params=_params(("parallel", "arbitrary")),
    )(nq, nq)


def attn_bwd(nq, do, lse, wts, rsum, n_seq, seq, dil, name):
    nb = seq // dil // ATT_BLK

    def body(prev_ref, cur_ref, nxt_ref, do_c, do_x, lse_c, lse_x, wt_c, wt_x, rs_c, rs_x, dn_ref):
        n = pl.program_id(1)
        mask_q = _band_mask2(n > 0, True)
        mask_k = _band_mask2(n < nb - 1, False)
        wc, wx = wt_c[...], wt_x[...]
        lse_t = jnp.concatenate([lse_c[...].T, lse_x[...].T], axis=1)
        dl_t = jnp.concatenate([(-wc * rs_c[...]).T, (-wx * rs_x[...]).T], axis=1)
        def operands(pr):
            sl = pl.ds(pr * PAIR_W, PAIR_W)
            ks = pl.ds(ATT_OUT + pr * PAIR_W, PAIR_W)
            vs = pl.ds(2 * ATT_OUT + pr * PAIR_W, PAIR_W)
            he, ho = pl.ds(2 * pr, 1), pl.ds(2 * pr + 1, 1)
            q_c, k_c, v_c = cur_ref[:, sl], cur_ref[:, ks], cur_ref[:, vs]
            dog_c = do_c[:, sl] * jnp.where(_lane_hi(), wt_c[:, ho], wt_c[:, he])
            dog_x = do_x[:, sl] * jnp.where(_lane_hi(), wt_x[:, ho], wt_x[:, he])
            return dict(q_c=q_c, k_c=k_c, v_c=v_c, qcat=jnp.concatenate([q_c, nxt_ref[:, sl]], axis=0),
                        kcat=jnp.concatenate([prev_ref[:, ks], k_c], axis=0),
                        vcat=jnp.concatenate([prev_ref[:, vs], v_c], axis=0),
                        dog=jnp.concatenate([dog_c, dog_x], axis=0).astype(MXU))

        def scores(o, h):
            hi, one = h % 2, pl.ds(h, 1)
            dl_col = -wt_c[:, one] * rs_c[:, one]
            p_q = jnp.exp(jnp.where(mask_q, _dot_nt(_only_head(o["q_c"], hi), o["kcat"]) - lse_c[:, one], NEG))
            ds_q = p_q * (_dot_nt(_only_head(o["dog"][:ATT_BLK], hi), o["vcat"]) + dl_col)
            p_t = jnp.exp(jnp.where(mask_k, _dot_nt(_only_head(o["k_c"], hi), o["qcat"]) - lse_t[h:h + 1, :], NEG))
            ds_t = p_t * (_dot_nt(_only_head(o["v_c"], hi), o["dog"]) + dl_t[h:h + 1, :])
            return ds_q, ds_t, p_t

        ops = [operands(pr) for pr in range(HEAD_PAIRS)]
        ahead = scores(ops[0], 0)
        res = []
        for h in range(ATT_H):
            o = ops[h // 2]
            (ds_q, ds_t, p_t), ahead = ahead, (scores(ops[(h + 1) // 2], h + 1) if h + 1 < ATT_H else None)
            res.append((_dot_nn(ds_q, o["kcat"]), _dot_nn(ds_t, o["qcat"]), _dot_nn(p_t, o["dog"])))
            if h % 2:
                for t, first in enumerate((0, ATT_OUT, 2 * ATT_OUT)):
                    dn_ref[:, pl.ds(first + (h // 2) * PAIR_W, PAIR_W)] = jnp.where(_lane_hi(), res[h][t], res[h - 1][t])

    def at(shift, width):
        if shift < 0:
            return pl.BlockSpec((ATT_BLK, width), lambda s, n: (s * nb + jnp.maximum(n - 1, 0), 0))
        if shift > 0:
            return pl.BlockSpec((ATT_BLK, width), lambda s, n: (s * nb + jnp.minimum(n + 1, nb - 1), 0))
        return pl.BlockSpec((ATT_BLK, width), lambda s, n: (s * nb + n, 0))

    return pl.pallas_call(
        body, name=name,
        grid=(n_seq * dil, nb),
        in_specs=[at(-1, QKV_W), at(0, QKV_W), at(1, QKV_W), at(0, ATT_OUT), at(1, ATT_OUT),
                  at(0, LANES), at(1, LANES), at(0, LANES), at(1, LANES), at(0, LANES), at(1, LANES)],
        out_specs=at(0, QKV_W),
        out_shape=jax.ShapeDtypeStruct((n_seq * seq, QKV_W), F32),
        compiler_params=_params(("parallel", "arbitrary")),
    )(nq, nq, nq, do, do, lse, lse, wts, wts, rsum, rsum)


def qk_post(qkv, dn, gq_t, gk_t, rows, name):
    tb = min(PREP_ROWS, rows)

    def body(x_ref, dn_ref, gq_ref, gk_ref, o_ref, dgq_ref, dgk_ref):
        i = pl.program_id(0)
        qs, ks, vs = pl.ds(0, ATT_OUT), pl.ds(ATT_OUT, ATT_OUT), pl.ds(2 * ATT_OUT, ATT_OUT)
        dq, dgq = _head_rms_bwd(x_ref[:, qs], gq_ref[...], dn_ref[:, qs] * ATT_SCALE)
        dk, dgk = _head_rms_bwd(x_ref[:, ks], gk_ref[...], dn_ref[:, ks])
        o_ref[:, qs] = dq.astype(o_ref.dtype)
        o_ref[:, ks] = dk.astype(o_ref.dtype)
        o_ref[:, vs] = dn_ref[:, vs].astype(o_ref.dtype)

        @pl.when(i == 0)
        def _():
            dgq_ref[...] = dgq
            dgk_ref[...] = dgk

        @pl.when(i != 0)
        def _():
            dgq_ref[...] += dgq
            dgk_ref[...] += dgk

    gspec = pl.BlockSpec((1, ATT_OUT), lambda i: (0, 0))
    blk = pl.BlockSpec((tb, QKV_W), lambda i: (i, 0))
    return pl.pallas_call(
        body, name=name,
        grid=(rows // tb,),
        in_specs=[blk, blk, gspec, gspec],
        out_specs=[blk, gspec, gspec],
        out_shape=[jax.ShapeDtypeStruct((rows, QKV_W), MXU), jax.ShapeDtypeStruct((1, ATT_OUT), F32),
                   jax.ShapeDtypeStruct((1, ATT_OUT), F32)],
        compiler_params=_params(("arbitrary",)),
    )(qkv, dn, gq_t, gk_t)


def all_gather(arrays, name):
    n = len(arrays)

    def body(*refs):
        in_refs, out_refs = refs[:n], refs[n:2 * n]
        send_sems, recv_sems, local_sems = refs[2 * n:]
        _, me, _ = _peer(0)
        sib, _, _ = _peer(1)

        def first(t, k, arriving):
            dev, pid, _ = _peer(k)
            return _remote(in_refs[t], out_refs[t].at[pid if arriving else me], send_sems, recv_sems, t, k, dev)

        def passed(t, k, arriving):
            slot = out_refs[t].at[_peer(k + 1 if arriving else k)[1]]
            return _remote(slot, slot, send_sems, recv_sems, t, k + 1, sib)

        def own(t):
            return pltpu.make_async_copy(in_refs[t], out_refs[t].at[me], local_sems.at[t])

        for t in range(n):
            own(t).start()
            for k in (1,) + OTHER_CHIPS:
                first(t, k, False).start()
        for t in range(n):
            for k in OTHER_CHIPS:
                first(t, k, True).wait_recv()
                passed(t, k, False).start()
        for t in range(n):
            first(t, 1, True).wait_recv()
            for k in OTHER_CHIPS:
                passed(t, k, True).wait_recv()
        for t in range(n):
            for k in (1,) + OTHER_CHIPS:
                first(t, k, False).wait_send()
            for k in OTHER_CHIPS:
                passed(t, k, False).wait_send()
            own(t).wait()

    anyspec = pl.BlockSpec(memory_space=pl.ANY)
    return pl.pallas_call(
        body, name=name,
        in_specs=[anyspec] * n,
        out_specs=[anyspec] * n,
        out_shape=[jax.ShapeDtypeStruct((N_DEV,) + tuple(a.shape), a.dtype) for a in arrays],
        scratch_shapes=[pltpu.SemaphoreType.DMA((n, N_DEV)), pltpu.SemaphoreType.DMA((n, N_DEV)),
                        pltpu.SemaphoreType.DMA((n,))],
    )(*arrays)


def pair_add(a, b, name):
    _, r, c = a.shape
    rb = r if r <= 512 else (128 if c > 1024 else 256)
    assert r % rb == 0

    def body(a_ref, b_ref, o_ref):
        o_ref[...] = (a_ref[...].astype(F32) + b_ref[...].astype(F32)).astype(o_ref.dtype)

    blk = pl.BlockSpec((1, rb, c), lambda q, i: (q, i, 0))
    return pl.pallas_call(
        body, name=name,
        grid=(N_CHIPS, r // rb),
        in_specs=[blk, blk],
        out_specs=blk,
        out_shape=jax.ShapeDtypeStruct(a.shape, a.dtype),
        compiler_params=_params(("parallel", "parallel")),
    )(a, b)


def adamw(parts, w, m, v, name):
    r, c = w.shape[-2:]
    n_parts = parts.shape[0]
    rb = r if r <= 512 else (128 if c > 1024 else 256)
    assert r % rb == 0

    def body(p_ref, w_ref, m_ref, v_ref, g_out, d_out, m_out, v_out):
        g = p_ref[0].astype(F32)
        for i in range(1, n_parts):
            g = g + p_ref[i].astype(F32)
        m_new = ADAM_B1 * m_ref[...] + (1.0 - ADAM_B1) * g
        v_new = ADAM_B2 * v_ref[...] + (1.0 - ADAM_B2) * (g * g)
        m_hat = m_new / (1.0 - ADAM_B1 ** ADAM_STEP)
        v_hat = v_new / (1.0 - ADAM_B2 ** ADAM_STEP)
        g_out[...] = g
        d_out[...] = -ADAM_LR * (m_hat / (jnp.sqrt(v_hat) + ADAM_EPS) + ADAM_WD * w_ref[...])
        m_out[...] = m_new
        v_out[...] = v_new

    if w.ndim == 3:
        blk = pl.BlockSpec((None, rb, c), lambda i: (0, i, 0))
    else:
        blk = pl.BlockSpec((rb, c), lambda i: (i, 0))
    return pl.pallas_call(
        body, name=name,
        grid=(r // rb,),
        in_specs=[pl.BlockSpec((n_parts, rb, c), lambda i: (0, i, 0)), blk, blk, blk],
        out_specs=[blk] * 4,
        out_shape=[jax.ShapeDtypeStruct(w.shape, F32)] * 4,
        compiler_params=_params(("parallel",)),
    )(parts, w, m, v)


def _pad_lanes(vec, n=LANES):
    return jnp.pad(vec, ((0, 0), (0, n - vec.shape[1])))


COL_SHARDED = ("w_in", "ssd_conv_w", "w_attn_proj", "w_up", "ffn_conv_w")
MATRICES = ("w_in", "w_attn_proj", "w_up", "w_ssd_proj", "w_out", "w_down")
LATE = ("w_ssd_proj", "w_attn_proj", "w_out", "w_up", "ffn_conv_w", "w_down")


def _narrow(name, a):
    return a.astype(MXU) if name in MATRICES else a


def _from_gathered(name, g):
    if name in COL_SHARDED:
        return jnp.transpose(g, (1, 0, 2)).reshape(g.shape[1], N_DEV * g.shape[2])
    return g.reshape(N_DEV * g.shape[1], g.shape[2])


def _to_slabs(name, g):
    if name in COL_SHARDED:
        return jnp.transpose(g.reshape(g.shape[0], N_DEV, g.shape[1] // N_DEV), (1, 0, 2))
    return g.reshape(N_DEV, g.shape[0] // N_DEV, g.shape[1])


def _columns(m, a, b):
    if m.ndim == 2:
        return m[:, a:b]
    c = m.shape[2]
    cuts = [m[j][:, max(a - j * c, 0):min(b - j * c, c)] for j in range(a // c, (b - 1) // c + 1)]
    return cuts[0] if len(cuts) == 1 else jnp.concatenate(cuts, axis=1)


def _column_shards(pieces, c):
    shards = []
    for j in range(N_DEV):
        cuts = []
        for start, arr in pieces:
            lo, hi = max(j * c - start, 0), min((j + 1) * c - start, arr.shape[1])
            if lo < hi:
                cuts.append(arr[:, lo:hi])
        shards.append(cuts[0] if len(cuts) == 1 else jnp.concatenate(cuts, axis=1))
    return jnp.stack(shards)


def local_step(x, target, w, late=None):
    n_seq, seq, _ = x.shape
    rows = n_seq * seq
    x = x.reshape(rows, D_MODEL)
    target = target.reshape(rows, D_MODEL)
    mx = lambda a: a.astype(MXU)

    splits = [sum(IN_WIDTHS[:i]) for i in range(len(IN_WIDTHS) + 1)]
    w_in = w["w_in"]
    part = lambda i: _columns(w_in, splits[i], splits[i + 1])
    w_z, w_xbc, w_gs, w_ga = mx(part(0)), mx(part(1)), mx(part(6)), mx(part(7))
    w_dt = mx(_pad_lanes(part(2)))
    head_group = lambda t, g: (splits[3 + t] + g * ATT_OUT, splits[3 + t] + (g + 1) * ATT_OUT)
    w_qkv = [mx(jnp.concatenate([_columns(w_in, *head_group(t, g)) for t in range(3)], axis=1))
             for g in range(ATT_GROUPS)]
    conv_w, conv_b, fconv_b = w["ssd_conv_w"], w["ssd_conv_b"], w["ffn_conv_b"]
    dt_bias, a_log, d_skip = _pad_lanes(w["dt_bias"]), _pad_lanes(w["a_log"]), _pad_lanes(w["d_skip"])
    g1, g2, gn, gq, gk = w["norm1_g"], w["norm2_g"], w["ssd_norm_g"], w["q_norm_g"], w["k_norm_g"]

    tb = min(512, seq)
    tbm = min(256, seq)
    cw = 1024
    rw = lambda fn, name, ncol, ins, params=(), outs=(), accs=(), tb_=tb: rowwise(
        fn, name, rows, seq, tb_, ncol, ins, params, outs, accs)

    (h,) = rw(lambda ctx, xv, g: _rms_fwd(xv, g), "rms1_fwd", 1, [(x, D_MODEL, 0, None)], [(g1, None, 0)],
              [(D_MODEL, D_MODEL, 0, MXU)])
    z = matmul(h, w_z, "mm_z")
    xbc = matmul(h, w_xbc, "mm_xbc")
    dtraw = matmul(h, w_dt, "mm_dt")
    by_residue = lambda a, g: _by_residue(a, n_seq, seq, ATT_DILATIONS[g])
    by_token = lambda a, g: _by_token(a, n_seq, seq, ATT_DILATIONS[g])
    h_res = [by_residue(h, g) for g in range(ATT_GROUPS)]
    gq_t, gk_t = jnp.tile(gq, (1, ATT_H)), jnp.tile(gk, (1, ATT_H))
    qkv_gains = jnp.concatenate([gq_t * ATT_SCALE, gk_t, jnp.ones_like(gk_t)], axis=1)

    def qk_norm(r, gains):
        q, k = r[:, :ATT_OUT], r[:, ATT_OUT:2 * ATT_OUT]
        return r, jnp.concatenate([q * _head_rstd(q), k * _head_rstd(k), r[:, 2 * ATT_OUT:]], axis=1) * gains

    qkv, nq = zip(*[matmul(h_res[g], w_qkv[g], f"mm_qkv{g}", tail=_Tail(qk_norm, rows=[qkv_gains], outs=[F32, MXU]))
                    for g in range(ATT_GROUPS)])
    gs = matmul(h, w_gs, "mm_gs")
    ga = matmul(h, w_ga, "mm_ga")

    def conv_silu(ctx, xh, wv, bv):
        return _silu(bv + _conv_prev(xh[0], xh[1], wv, ctx.first, SSD_CONV))

    (xact,) = rw(conv_silu, "ssd_conv_fwd", CONV_DIM // cw, [(xbc, cw, 0, "prev")],
                 [(conv_w, cw, 0), (conv_b, cw, 0)], [(CONV_DIM, cw, 0, F32)])
    if late is None:
        y, sin = ssd_fwd(xact, dtraw, dt_bias, a_log, d_skip, n_seq, seq)
    else:
        y, sin, *gathered = ssd_fwd(xact, dtraw, dt_bias, a_log, d_skip, n_seq, seq,
                                    comm=direct_exchange([(late[n], "gather") for n in LATE]))
        w = {**w, **{n: g if n == "w_up" else _from_gathered(n, g) for n, g in zip(LATE, gathered)}}
    w_sp, w_ap, w_o, w_d = mx(w["w_ssd_proj"]), mx(w["w_attn_proj"]), mx(w["w_out"]), mx(w["w_down"])
    w_ug, w_uv = mx(_columns(w["w_up"], 0, D_FF)), mx(_columns(w["w_up"], D_FF, 2 * D_FF))
    fconv_w = w["ffn_conv_w"]

    def gated_norm(ctx, yv, zv, g):
        yz = yv * _silu(zv)
        return jnp.concatenate([_rms_fwd(yz[:, i:i + NORM_GROUP], g[:, i:i + NORM_GROUP])
                                for i in range(0, cw, NORM_GROUP)], axis=1)

    (y_ssd,) = rw(gated_norm, "ssd_post_fwd", D_INNER // cw, [(y, cw, 0, None), (z, cw, 0, None)], [(gn, cw, 0)],
                  [(D_INNER, cw, 0, MXU)])

    att = [attn_fwd(nq[g], n_seq, seq, ATT_DILATIONS[g], f"attn_fwd{g}") for g in range(ATT_GROUPS)]

    def combine(ctx, o0, o1, o2, l0, l1, l2):
        mxl = jnp.maximum(jnp.maximum(l0, l1), l2)
        e = [jnp.exp(l - mxl) for l in (l0, l1, l2)]
        inv = 1.0 / (e[0] + e[1] + e[2])
        ws = [ei * inv for ei in e]
        out = sum(_expand_heads(wi) * oi for wi, oi in zip(ws, (o0, o1, o2)))
        return (out, *ws)

    y_attn, wt0, wt1, wt2 = rw(
        combine, "attn_combine", 1,
        [(by_token(att[g][0], g), ATT_OUT, 0, None) for g in range(3)]
        + [(by_token(att[g][1], g), LANES, 0, None) for g in range(3)], [],
        [(ATT_OUT, ATT_OUT, 0, F32)] + [(LANES, LANES, 0, F32)] * 3)
    wts = (wt0, wt1, wt2)

    ps = matmul(y_ssd, w_sp, "mm_ssd_proj")
    pa, merged = matmul(y_attn, w_ap, "mm_attn_proj",
                        tail=_Tail(lambda r, a, c, d: (r, _sigmoid(c) * a + _sigmoid(d) * r), like=[ps, gs, ga],
                                   outs=[F32, MXU]))
    x1, h2 = matmul(merged, w_o, "mm_out", add=x,
                    tail=_Tail(lambda r, g: (r, _rms_fwd(r, g)), rows=[g2], outs=[F32, MXU]))
    up_g = matmul(h2, w_ug, "mm_up_g")
    up_v = matmul(h2, w_uv, "mm_up_v")
    fw = D_FF // 2
    nfc = D_FF // fw

    def mlp_act(ctx, ug, uv, wg, wv, bg, bv):
        cg = bg + _conv_prev(ug[0], ug[1], wg, ctx.first, FFN_CONV)
        cv = bv + _conv_prev(uv[0], uv[1], wv, ctx.first, FFN_CONV)
        return _silu(cg) * cv

    (act,) = rw(mlp_act, "mlp_act_fwd", nfc, [(up_g, fw, 0, "prev"), (up_v, fw, 0, "prev")],
                [(fconv_w, fw, 0), (fconv_w, fw, nfc), (fconv_b, fw, 0), (fconv_b, fw, nfc)], [(D_FF, fw, 0, MXU)],
                tb_=tbm)
    def loss_tail(out, tv):
        d = out - tv
        g = d * (1.0 / D_MODEL)
        return g, g, jnp.sum(d * d, axis=0, keepdims=True)

    dx2, dx2_m, sq = matmul(act, w_d, "mm_down", add=x1,
                            tail=_Tail(loss_tail, like=[target], outs=[F32, MXU], n_sums=1))

    grads = {}
    dact = matmul(dx2_m, w_d, "mm_d_act", tb=True)
    grads["w_down"] = matmul(act, dx2_m, "mm_dw_down", ta=True, out_dtype=MXU)

    def mlp_bwd(ctx, da, ug, uv, wg, wv, bg, bv):
        cg, cg_n = _conv_pre(ug, wg, bg, ctx.first, FFN_CONV)
        cv, cv_n = _conv_pre(uv, wv, bv, ctx.first, FFN_CONV)
        da_c, da_n = da
        dup_g_, dwg, dbg = _conv_bwd(da_c * cv * _silu_grad(cg), da_n * cv_n * _silu_grad(cg_n), ug, wg, ctx, FFN_CONV)
        dup_v_, dwv, dbv = _conv_bwd(da_c * _silu(cg), da_n * _silu(cg_n), uv, wv, ctx, FFN_CONV)
        return dup_g_, dup_v_, dwg, dbg, dwv, dbv

    dup_g, dup_v, dfw_g, dfb_g, dfw_v, dfb_v = rw(
        mlp_bwd, "mlp_bwd", nfc, [(dact, fw, 0, "next"), (up_g, fw, 0, "both"), (up_v, fw, 0, "both")],
        [(fconv_w, fw, 0), (fconv_w, fw, nfc), (fconv_b, fw, 0), (fconv_b, fw, nfc)],
        [(D_FF, fw, 0, MXU), (D_FF, fw, 0, MXU)], [(FFN_CONV, fw), (1, fw), (FFN_CONV, fw), (1, fw)], tb_=tbm)
    grads["ffn_conv_w"] = jnp.concatenate([dfw_g, dfw_v], axis=1)
    grads["ffn_conv_b"] = jnp.concatenate([dfb_g, dfb_v], axis=1)
    def rms_bwd_fn(dh_, xv, dres, g):
        dxv, dg = _rms_bwd(xv, g, dh_)
        return dres + dxv, dg

    def rms_bwd_fn2(dh_, xv, dres, g):
        dxv, dg = rms_bwd_fn(dh_, xv, dres, g)
        return dxv, dxv, dg

    dh2 = matmul(dup_g, w_ug, "mm_dh2_g", tb=True)
    dx1, dx1_m, grads["norm2_g"] = matmul(
        dup_v, w_uv, "mm_dh2_v", tb=True, add=dh2,
        tail=_Tail(rms_bwd_fn2, like=[x1, dx2], rows=[g2], outs=[F32, MXU], n_sums=1))
    dw_up =[(0, matmul(h2, dup_g, "mm_dw_up_g", ta=True, out_dtype=MXU)),
             (D_FF, matmul(h2, dup_v, "mm_dw_up_v", ta=True, out_dtype=MXU))]
    if w["w_up"].ndim == 3:
        grads["w_up"] = _column_shards(dw_up, w["w_up"].shape[2])
    else:
        grads["w_up"] = jnp.concatenate([p for _, p in dw_up], axis=1)

    def merge_bwd(dm, a, b, c, d):
        sc, sd = _sigmoid(c), _sigmoid(d)
        return dm * sc, dm * sd, dm * a * sc * (1.0 - sc), dm * b * sd * (1.0 - sd)

    dps, dpa, dgs, dga = matmul(dx1_m, w_o, "mm_d_merged", tb=True,
                                tail=_Tail(merge_bwd, like=[ps, pa, gs, ga], outs=[MXU] * 4))
    grads["w_out"] = matmul(merged, dx1_m, "mm_dw_out", ta=True, out_dtype=MXU)

    def gated_norm_bwd(dyn, yv, zv, g):
        sz = _silu(zv)
        yz = yv * sz
        dyz, dgs_ = [], []
        for i in range(0, D_INNER, NORM_GROUP):
            a, b = _rms_bwd(yz[:, i:i + NORM_GROUP], g[:, i:i + NORM_GROUP], dyn[:, i:i + NORM_GROUP])
            dyz.append(a)
            dgs_.append(b)
        dyz = jnp.concatenate(dyz, axis=1)
        return dyz * sz, dyz * yv * _silu_grad(zv), jnp.concatenate(dgs_, axis=1)

    dy, dz, grads["ssd_norm_g"] = matmul(dps, w_sp, "mm_d_y_ssd", tb=True,
                                         tail=_Tail(gated_norm_bwd, like=[y, z], rows=[gn], outs=[F32, MXU], n_sums=1))
    grads["w_ssd_proj"] = matmul(y_ssd, dps, "mm_dw_ssd_proj", ta=True, out_dtype=MXU)
    dy_attn = matmul(dpa, w_ap, "mm_d_y_attn", tb=True)
    grads["w_attn_proj"] = matmul(y_attn, dpa, "mm_dw_attn_proj", ta=True, out_dtype=MXU)

    (rsum,) = rw(lambda ctx, a, b: _reduce_heads(a * b), "attn_rsum", 1,
                 [(dy_attn, ATT_OUT, 0, None), (y_attn, ATT_OUT, 0, None)], [], [(LANES, LANES, 0, F32)])
    dqkv, dgq, dgk = [], 0.0, 0.0
    for g in range(ATT_GROUPS):
        dn = attn_bwd(nq[g], by_residue(dy_attn, g), att[g][1], by_residue(wts[g], g), by_residue(rsum, g), n_seq, seq,
                      ATT_DILATIONS[g], f"attn_bwd{g}")
        d_, a_, b_ = qk_post(qkv[g], dn, gq_t, gk_t, rows, f"qk_post{g}")
        dqkv.append(d_)
        dgq, dgk = dgq + a_, dgk + b_
    per_head = lambda v: jnp.sum(v.reshape(ATT_H, ATT_HD), axis=0, keepdims=True)
    grads["q_norm_g"], grads["k_norm_g"] = per_head(dgq), per_head(dgk)

    if late is None:
        dxact, ddt, dbias, dalog, ddskip = ssd_bwd(xact, dtraw, dt_bias, a_log, d_skip, sin, dy, n_seq, seq)
    else:
        dxact, ddt, dbias, dalog, ddskip, *parts = ssd_bwd(
            xact, dtraw, dt_bias, a_log, d_skip, sin, dy, n_seq, seq,
            comm=direct_exchange([(grads[n] if n == "w_up" else _to_slabs(n, _narrow(n, grads[n])), "scatter")
                                  for n in LATE]))
        grads.update(zip(LATE, parts))
    grads["dt_bias"], grads["a_log"], grads["d_skip"] = dbias[:, :SSD_H], dalog[:, :SSD_H], ddskip[:, :SSD_H]

    def conv_silu_bwd(ctx, dxa, xin, wv, bv):
        pre, pre_n = _conv_pre(xin, wv, bv, ctx.first, SSD_CONV)
        return _conv_bwd(dxa[0] * _silu_grad(pre), dxa[1] * _silu_grad(pre_n), xin, wv, ctx, SSD_CONV)

    dxbc, grads["ssd_conv_w"], grads["ssd_conv_b"] = rw(
        conv_silu_bwd, "ssd_conv_bwd", CONV_DIM // cw, [(dxact, cw, 0, "next"), (xbc, cw, 0, "both")],
        [(conv_w, cw, 0), (conv_b, cw, 0)], [(CONV_DIM, cw, 0, MXU)], [(SSD_CONV, cw), (1, cw)])

    pieces = [(d_, d_, h, w_, tag) for d_, w_, tag in
              ((dz, w_z, "z"), (ddt, w_dt, "dt"), (dgs, w_gs, "gs"), (dga, w_ga, "ga"))]
    pieces += [(by_token(dqkv[g], g), dqkv[g], h_res[g], w_qkv[g], f"qkv{g}") for g in range(ATT_GROUPS)]
    pieces += [(dxbc, dxbc, h, w_xbc, "xbc")]
    dws = {tag: matmul(h_in, dpart_h, f"mm_dw_{tag}", ta=True, out_dtype=MXU) for _, dpart_h, h_in, _, tag in pieces}
    dw_in = [(splits[0], dws["z"]), (splits[1], dws["xbc"]), (splits[2], dws["dt"][:, :SSD_H])]
    dw_in += [(head_group(t, g)[0], dws[f"qkv{g}"][:, t * ATT_OUT:(t + 1) * ATT_OUT])
              for t in range(3) for g in range(ATT_GROUPS)]
    dw_in += [(splits[6], dws["gs"]), (splits[7], dws["ga"])]
    if w_in.ndim == 3:
        grads["w_in"] = _column_shards(dw_in, w_in.shape[2])
    else:
        grads["w_in"] = jnp.concatenate([p for _, p in dw_in], axis=1)
    dh = None
    for idx, (dpart, _, _, wpart, tag) in enumerate(pieces):
        comm, tail = NO_EXCHANGE, None
        if late is not None and idx == 0:
            slabs = [grads[n] if n == "w_in" else _to_slabs(n, _narrow(n, grads[n])) for n in EARLY]
            comm = sibling_exchange(slabs)
        if idx == len(pieces) - 1:
            tail = _Tail(rms_bwd_fn, like=[x, dx1], rows=[g1], outs=[F32], n_sums=1)
            if late is not None:
                comm = chip_exchange(summed)
        dh = matmul(dpart, wpart, f"mm_dh_{tag}", tb=True, add=dh, comm=comm, tail=tail)
        if late is not None and idx == 0:
            dh, *arrived = dh
            core = lax.axis_index("c")
            own = [lax.dynamic_index_in_dim(s.reshape((N_CHIPS, 2) + s.shape[1:]), core, axis=1, keepdims=False)
                   for s in slabs]
            summed = [pair_add(a_, b_, f"rs_add_{n}") for n, a_, b_ in zip(EARLY, own, arrived)]
    grad_x, grads["norm1_g"], *arrived = dh
    grads.update(zip(EARLY, arrived))
    return sq, grad_x.reshape(n_seq, seq, D_MODEL), grads


EARLY = ("w_in", "ssd_conv_w")
REPLICATED = ("norm1_g", "ssd_conv_b", "dt_bias", "a_log", "d_skip", "ssd_norm_g", "q_norm_g", "k_norm_g",
              "norm2_g", "ffn_conv_b")
WEIGHTS = ("norm1_g", "w_in", "ssd_conv_w", "ssd_conv_b", "dt_bias", "a_log", "d_skip", "ssd_norm_g", "w_ssd_proj",
           "q_norm_g", "k_norm_g", "w_attn_proj", "w_out", "norm2_g", "w_up", "ffn_conv_w", "ffn_conv_b", "w_down")
PACK_ROWS, PACK_COLS = 8, 2048


def _pack(vals):
    flat = jnp.concatenate([vals[n].reshape(-1) for n in REPLICATED])
    return jnp.pad(flat, (0, PACK_ROWS * PACK_COLS - flat.shape[0])).reshape(PACK_ROWS, PACK_COLS)


def _unpack(packed, like):
    flat = packed.reshape(-1)
    out, pos = {}, 0
    for n in REPLICATED:
        size = like[n].size
        out[n] = flat[pos:pos + size].reshape(like[n].shape)
        pos += size
    return out


def step(x, target, w_raw, m_raw, v_raw):
    wsh = {n: a[0] if a.ndim == 3 else a for n, a in w_raw.items()}
    gathered = all_gather([_narrow(n, wsh[n]) for n in EARLY], "ag_weights")
    full = {n: wsh[n] for n in REPLICATED}
    full.update({n: g if n == "w_in" else _from_gathered(n, g) for n, g in zip(EARLY, gathered)})

    sq, grad_x, grads = local_step(x, target, full, late={n: _narrow(n, wsh[n]) for n in LATE})

    (small,) = all_gather([_pack({n: grads[n] for n in REPLICATED})], "ag_small")

    out_g, out_d, out_m, out_v = {}, {}, {}, {}
    for n in EARLY + LATE:
        out_g[n], out_d[n], out_m[n], out_v[n] = adamw(grads[n], w_raw[n], m_raw[n], v_raw[n], f"adamw_{n}")
    pk = adamw(small, _pack(w_raw), _pack(m_raw), _pack(v_raw), "adamw_small")
    for dst, packed in zip((out_g, out_d, out_m, out_v), pk):
        dst.update(_unpack(packed, w_raw))
    loss = lax.psum(0.5 * jnp.sum(sq) / D_MODEL, ("x", "y", "c"))
    return loss, grad_x, out_g, out_d, out_m, out_v


def kernel(x, norm1_g, w_in, ssd_conv_w, ssd_conv_b, dt_bias, a_log, d_skip, ssd_norm_g, w_ssd_proj, q_norm_g, k_norm_g, w_attn_proj, w_out, norm2_g, w_up, ffn_conv_w, ffn_conv_b, w_down, loss_target, m_norm1_g, m_w_in, m_ssd_conv_w, m_ssd_conv_b, m_dt_bias, m_a_log, m_d_skip, m_ssd_norm_g, m_w_ssd_proj, m_q_norm_g, m_k_norm_g, m_w_attn_proj, m_w_out, m_norm2_g, m_w_up, m_ffn_conv_w, m_ffn_conv_b, m_w_down, v_norm1_g, v_w_in, v_ssd_conv_w, v_ssd_conv_b, v_dt_bias, v_a_log, v_d_skip, v_ssd_norm_g, v_w_ssd_proj, v_q_norm_g, v_k_norm_g, v_w_attn_proj, v_w_out, v_norm2_g, v_w_up, v_ffn_conv_w, v_ffn_conv_b, v_w_down):
    ws = (norm1_g, w_in, ssd_conv_w, ssd_conv_b, dt_bias, a_log, d_skip, ssd_norm_g, w_ssd_proj, q_norm_g, k_norm_g,
          w_attn_proj, w_out, norm2_g, w_up, ffn_conv_w, ffn_conv_b, w_down)
    ms = (m_norm1_g, m_w_in, m_ssd_conv_w, m_ssd_conv_b, m_dt_bias, m_a_log, m_d_skip, m_ssd_norm_g, m_w_ssd_proj,
          m_q_norm_g, m_k_norm_g, m_w_attn_proj, m_w_out, m_norm2_g, m_w_up, m_ffn_conv_w, m_ffn_conv_b, m_w_down)
    vs = (v_norm1_g, v_w_in, v_ssd_conv_w, v_ssd_conv_b, v_dt_bias, v_a_log, v_d_skip, v_ssd_norm_g, v_w_ssd_proj,
          v_q_norm_g, v_k_norm_g, v_w_attn_proj, v_w_out, v_norm2_g, v_w_up, v_ffn_conv_w, v_ffn_conv_b, v_w_down)
    loss, grad_x, g, d, m, v = step(x, loss_target, dict(zip(WEIGHTS, ws)), dict(zip(WEIGHTS, ms)), dict(zip(WEIGHTS, vs)))
    ordered = lambda dct: [dct[n] for n in WEIGHTS]
    return (loss, grad_x, *ordered(g), *ordered(d), *ordered(m), *ordered(v))
```

```python
import functools

import jax
import jax.numpy as jnp
from jax import lax
from jax.experimental import pallas as pl
from jax.experimental.pallas import tpu as pltpu

F32 = jnp.float32
BF16 = jnp.bfloat16
MXU = jnp.bfloat16
HIGHEST = lax.Precision.HIGHEST
VMEM_LIMIT_BYTES = 48 * 1024 * 1024
SUBLANES = 8
LANES = 128
N_DEV = 8

D_MODEL = 1024
D_INNER = 2048
SSD_P = 64
SSD_H = 32
SSD_G = 8
SSD_K = SSD_H // SSD_G
SSD_N = 128
SSD_Q = 128
SSD_CONV = 4
CONV_DIM = D_INNER + 2 * SSD_G * SSD_N
NORM_GROUP = D_INNER // SSD_G
ATT_GROUPS = 3
ATT_H = 8
ATT_HD = 64
ATT_BLK = 128
ATT_OUT = ATT_H * ATT_HD
ATT_DILATIONS = (1, 4, 16)
ATT_SCALE = ATT_HD ** -0.5
D_FF = 2816
FFN_CONV = 3
EPS = 1e-6
NEG = -1e30
IN_WIDTHS = (D_INNER, CONV_DIM, SSD_H, 3 * ATT_OUT, 3 * ATT_OUT, 3 * ATT_OUT, D_MODEL, D_MODEL)

ADAM_LR = 0.001
ADAM_B1 = 0.9
ADAM_B2 = 0.999
ADAM_EPS = 1e-08
ADAM_WD = 0.01
ADAM_STEP = 10


def _mm(a, b, dims):
    return lax.dot_general(a.astype(MXU), b.astype(MXU), (dims, ((), ())), preferred_element_type=F32)


def _dot_nn(a, b):
    return _mm(a, b, ((1,), (0,)))


def _dot_nt(a, b):
    return _mm(a, b, ((1,), (1,)))


def _dot_tn(a, b):
    return _mm(a, b, ((0,), (0,)))


def _dot_f32(a, b):
    return lax.dot_general(a, b, (((1,), (0,)), ((), ())), precision=HIGHEST, preferred_element_type=F32)


def _sigmoid(x):
    return 1.0 / (1.0 + jnp.exp(-x))


def _silu(x):
    return x * _sigmoid(x)


def _silu_grad(x):
    s = _sigmoid(x)
    return s * (1.0 + x * (1.0 - s))


def _softplus(x):
    return jnp.maximum(x, 0.0) + jnp.log(1.0 + jnp.exp(-jnp.abs(x)))


def _rms_fwd(x, g):
    r = lax.rsqrt(jnp.mean(x * x, axis=-1, keepdims=True) + EPS)
    return x * r * g


def _rms_bwd(x, g, dy):
    r = lax.rsqrt(jnp.mean(x * x, axis=-1, keepdims=True) + EPS)
    xh = x * r
    dyg = dy * g
    dx = r * (dyg - xh * jnp.mean(dyg * xh, axis=-1, keepdims=True))
    return dx, jnp.sum(dy * xh, axis=0, keepdims=True)


def _onehot_row(h, n=LANES):
    return (lax.broadcasted_iota(jnp.int32, (1, n), 1) == h).astype(F32)


def _onehot_col(h, n=LANES):
    return (lax.broadcasted_iota(jnp.int32, (n, 1), 0) == h).astype(F32)


def _head_expand_matrix():
    r = lax.broadcasted_iota(jnp.int32, (LANES, ATT_OUT), 0)
    c = lax.broadcasted_iota(jnp.int32, (LANES, ATT_OUT), 1)
    return (c // ATT_HD == r).astype(F32)


def _split_bf16(x, parts):
    out = []
    for _ in range(parts - 1):
        hi = x.astype(BF16).astype(F32)
        out.append(hi)
        x = x - hi
    out.append(x)
    return out


def _expand_heads(w):
    e = _head_expand_matrix()
    return sum(_dot_nn(p, e) for p in _split_bf16(w, 2))


def _reduce_heads(x):
    e = _head_expand_matrix()
    return sum(_dot_nt(p, e) for p in _split_bf16(x, 3))


def _shift_prev(cur, halo, s, first):
    if s == 0:
        return cur
    rolled = pltpu.roll(cur, s, 0)
    hr = jnp.where(first, 0.0, pltpu.roll(halo, s, 0))
    rows = lax.broadcasted_iota(jnp.int32, halo.shape, 0)
    head = jnp.where(rows < s, hr, rolled[:SUBLANES])
    if cur.shape[0] == SUBLANES:
        return head
    return jnp.concatenate([head, rolled[SUBLANES:]], axis=0)


def _shift_next(cur, halo, s, last):
    if s == 0:
        return cur
    tb = cur.shape[0]
    rolled = pltpu.roll(cur, tb - s, 0)
    hr = jnp.where(last, 0.0, pltpu.roll(halo, SUBLANES - s, 0))
    rows = lax.broadcasted_iota(jnp.int32, halo.shape, 0)
    tail = jnp.where(rows >= SUBLANES - s, hr, rolled[tb - SUBLANES:])
    return jnp.concatenate([rolled[:tb - SUBLANES], tail], axis=0)


def _conv_prev(x, halo, w, first, taps):
    acc = None
    for i in range(taps):
        term = w[i:i + 1, :] * _shift_prev(x, halo, taps - 1 - i, first)
        acc = term if acc is None else acc + term
    return acc


def _conv_bwd(dpre, dpre_next8, x, w, ctx, taps):
    dx, dws = None, []
    for i in range(taps):
        ahead = _shift_next(dpre, dpre_next8, taps - 1 - i, ctx.last)
        term = w[i:i + 1, :] * ahead
        dx = term if dx is None else dx + term
        dws.append(jnp.sum(ahead * x, axis=0, keepdims=True))
    return dx, jnp.concatenate(dws, axis=0), jnp.sum(dpre, axis=0, keepdims=True)


def _params(sem):
    return pltpu.CompilerParams(dimension_semantics=sem, vmem_limit_bytes=VMEM_LIMIT_BYTES)


N_CHIPS = N_DEV // 2
OTHER_CHIPS = (4, 2, 6)


class _Hosted:
    def __init__(self, arrays, out_shape, sems, ops):
        self.arrays, self.out_shape, self.sems, self.ops = list(arrays), list(out_shape), list(sems), ops
        self.n = len(self.arrays)
        self.specs = [pl.BlockSpec(memory_space=pl.ANY)] * self.n

    def begin(self, in_refs, out_refs, sem_refs, first):
        start, finish = self.ops(in_refs, out_refs, *sem_refs)
        pl.when(first)(start)
        return finish


NO_EXCHANGE = _Hosted((), (), (), None)


def _peer(k):
    x, y, c = lax.axis_index("x"), lax.axis_index("y"), lax.axis_index("c")
    px = 1 - x if k & 4 else x
    py = 1 - y if k & 2 else y
    pc = 1 - c if k & 1 else c
    return (px, py, pc), 4 * px + 2 * py + pc, 2 * px + py


def _remote(src, dst, send_sems, recv_sems, t, k, dev):
    return pltpu.make_async_remote_copy(src_ref=src, dst_ref=dst, send_sem=send_sems.at[t, k], recv_sem=recv_sems.at[t, k],
                                        device_id=dev, device_id_type=pl.DeviceIdType.MESH)


def direct_exchange(items):
    n = len(items)

    def ops(in_refs, out_refs, send_sems, recv_sems, local_sems):
        _, me, _ = _peer(0)
        part = lambda t, pid: in_refs[t] if items[t][1] == "gather" else in_refs[t].at[pid]

        def copy(t, k, arriving):
            dev, pid, _ = _peer(k)
            return _remote(part(t, pid), out_refs[t].at[pid if arriving else me], send_sems, recv_sems, t, k, dev)

        def own(t):
            return pltpu.make_async_copy(part(t, me), out_refs[t].at[me], local_sems.at[t])

        def start():
            for t in range(n):
                own(t).start()
                for k in range(1, N_DEV):
                    copy(t, k, False).start()

        def finish():
            for t in range(n):
                for k in range(1, N_DEV):
                    copy(t, k, True).wait_recv()
            for t in range(n):
                for k in range(1, N_DEV):
                    copy(t, k, False).wait_send()
                own(t).wait()

        return start, finish

    out_shape = [jax.ShapeDtypeStruct((N_DEV,) + tuple(a.shape if m == "gather" else a.shape[1:]), a.dtype)
                 for a, m in items]
    sems = [pltpu.SemaphoreType.DMA((n, N_DEV)), pltpu.SemaphoreType.DMA((n, N_DEV)), pltpu.SemaphoreType.DMA((n,))]
    return _Hosted([a for a, _ in items], out_shape, sems, ops)


def sibling_exchange(arrays):
    n = len(arrays)

    def ops(in_refs, out_refs, send_sems, recv_sems):
        sib, _, _ = _peer(1)
        c = lax.axis_index("c")
        copy = lambda t, q: _remote(in_refs[t].at[2 * q + (1 - c)], out_refs[t].at[q], send_sems, recv_sems, t, q, sib)

        def start():
            for t in range(n):
                for q in range(N_CHIPS):
                    copy(t, q).start()

        def finish():
            for t in range(n):
                for q in range(N_CHIPS):
                    copy(t, q).wait_recv()
            for t in range(n):
                for q in range(N_CHIPS):
                    copy(t, q).wait_send()

        return start, finish

    out_shape = [jax.ShapeDtypeStruct((N_CHIPS,) + a.shape[1:], a.dtype) for a in arrays]
    sems = [pltpu.SemaphoreType.DMA((n, N_CHIPS)), pltpu.SemaphoreType.DMA((n, N_CHIPS))]
    return _Hosted(arrays, out_shape, sems, ops)


def chip_exchange(arrays):
    n = len(arrays)

    def ops(in_refs, out_refs, send_sems, recv_sems, local_sems):
        _, _, mine = _peer(0)

        def copy(t, k, arriving):
            dev, _, q = _peer(k)
            return _remote(in_refs[t].at[q], out_refs[t].at[q if arriving else mine], send_sems, recv_sems, t, k, dev)

        def own(t):
            return pltpu.make_async_copy(in_refs[t].at[mine], out_refs[t].at[mine], local_sems.at[t])

        def start():
            for t in range(n):
                own(t).start()
                for k in OTHER_CHIPS:
                    copy(t, k, False).start()

        def finish():
            for t in range(n):
                for k in OTHER_CHIPS:
                    copy(t, k, True).wait_recv()
            for t in range(n):
                for k in OTHER_CHIPS:
                    copy(t, k, False).wait_send()
                own(t).wait()

        return start, finish

    out_shape = [jax.ShapeDtypeStruct(a.shape, a.dtype) for a in arrays]
    sems = [pltpu.SemaphoreType.DMA((n, N_DEV)), pltpu.SemaphoreType.DMA((n, N_DEV)), pltpu.SemaphoreType.DMA((n,))]
    return _Hosted(arrays, out_shape, sems, ops)


MATMUL_VMEM_BUDGET = 34 * 1024 * 1024


V7X_MXU_FLOPS = 996e12
V7X_HBM_BYTES_PER_S = 3.4e12
GRID_STEP_S = 0.35e-6


def _tile_sizes(dim, cap):
    return [t for t in range(LANES, min(dim, cap) + 1, LANES) if dim % t == 0] or [dim]


def _matmul_tiles(m, n, k, a_bytes, b_bytes, add_bytes, out_bytes, whole_rows=False):
    best = None
    for tk in _tile_sizes(k, 8192):
        nk = k // tk
        for tn in ([n] if whole_rows else _tile_sizes(n, 2048)):
            for tm in _tile_sizes(m, 2048):
                io = tm * tk * a_bytes + tk * tn * b_bytes
                ends = tm * tn * (add_bytes + out_bytes)
                need = 2 * (io + ends) + tm * tn * 4 * (2 if nk > 1 else 1)
                if need > MATMUL_VMEM_BUDGET:
                    continue
                step = max(2.0 * tm * tn * tk / V7X_MXU_FLOPS, (io + ends / nk) / V7X_HBM_BYTES_PER_S)
                if nk > 1:
                    step += tm * tn * 8 / V7X_HBM_BYTES_PER_S
                cost = (m // tm) * (n // tn) * nk * (step + GRID_STEP_S)
                if best is None or cost < best[0]:
                    best = (cost, tm, tn, tk)
    if best is None:
        raise ValueError((m, n, k))
    return best[1:]


class _Tail:
    def __init__(self, fn, like=(), rows=(), outs=(), n_sums=0):
        self.fn, self.like, self.rows, self.outs, self.n_sums = fn, list(like), list(rows), list(outs), n_sums


def matmul(a, b, name, ta=False, tb=False, add=None, out_dtype=F32, comm=NO_EXCHANGE, tail=None):
    assert not (ta and tb)
    m, k = (a.shape[1], a.shape[0]) if ta else a.shape
    n = b.shape[0] if tb else b.shape[1]
    assert (b.shape[1] if tb else b.shape[0]) == k
    like = ([] if add is None else [add]) + (tail.like if tail else [])
    rows = tail.rows if tail else []
    outs = tail.outs if tail else [out_dtype]
    n_sums = tail.n_sums if tail else 0
    tm, tn, tk = _matmul_tiles(m, n, k, a.dtype.itemsize, b.dtype.itemsize, sum(x.dtype.itemsize for x in like),
                               sum(jnp.dtype(d).itemsize for d in outs), whole_rows=tail is not None)
    nk = k // tk
    grid = (m // tm, n // tn, nk)
    dims = ((0,), (0,)) if ta else (((1,), (1,)) if tb else ((1,), (0,)))
    n_in = 2 + len(like) + len(rows)
    n_out = len(outs) + n_sums
    n_acc = 0 if nk == 1 else 1

    def body(*refs):
        a_ref, b_ref = refs[:2]
        like_refs, row_refs = refs[2:2 + len(like)], refs[2 + len(like):n_in]
        out_refs = refs[n_in + comm.n:n_in + comm.n + n_out]
        ids = [pl.program_id(d) for d in range(3)]
        if comm.n:
            first = functools.reduce(jnp.logical_and, [i == 0 for i in ids])
            last = functools.reduce(jnp.logical_and, [i == g - 1 for i, g in zip(ids, grid)])
            done = comm.begin(refs[n_in:n_in + comm.n], refs[n_in + comm.n + n_out:n_in + 2 * comm.n + n_out],
                              refs[n_in + 2 * comm.n + n_out + n_acc:], first)

        def finish(r):
            if add is not None:
                r = r + like_refs[0][...].astype(F32)
            if tail is None:
                out_refs[0][...] = r.astype(out_dtype)
                return
            vals = tail.fn(r, *[x[...] for x in like_refs[len(like) - len(tail.like):]], *[x[...] for x in row_refs])
            for ref, val in zip(out_refs[:len(outs)], vals):
                ref[...] = val.astype(ref.dtype)
            for ref, val in zip(out_refs[len(outs):], vals[len(outs):]):
                @pl.when(ids[0] == 0)
                def _(ref=ref, val=val):
                    ref[...] = val

                @pl.when(ids[0] != 0)
                def _(ref=ref, val=val):
                    ref[...] += val

        if nk == 1:
            finish(_mm(a_ref[...], b_ref[...], dims))
        else:
            acc = refs[n_in + 2 * comm.n + n_out]

            @pl.when(ids[2] == 0)
            def _():
                acc[...] = jnp.zeros_like(acc)

            acc[...] += _mm(a_ref[...], b_ref[...], dims)

            @pl.when(ids[2] == nk - 1)
            def _():
                finish(acc[...])

        if comm.n:
            pl.when(last)(done)

    a_spec = pl.BlockSpec((tk, tm), lambda i, j, kk: (kk, i)) if ta else pl.BlockSpec((tm, tk), lambda i, j, kk: (i, kk))
    b_spec = pl.BlockSpec((tn, tk), lambda i, j, kk: (j, kk)) if tb else pl.BlockSpec((tk, tn), lambda i, j, kk: (kk, j))
    tile = pl.BlockSpec((tm, tn), lambda i, j, kk: (i, j))
    row = pl.BlockSpec((1, tn), lambda i, j, kk: (0, j))
    sequential = comm.n or n_sums
    res = pl.pallas_call(
        body, name=name,
        grid=grid,
        in_specs=[a_spec, b_spec] + [tile] * len(like) + [row] * len(rows) + comm.specs,
        out_specs=[tile] * len(outs) + [row] * n_sums + comm.specs,
        out_shape=[jax.ShapeDtypeStruct((m, n), d) for d in outs] + [jax.ShapeDtypeStruct((1, n), F32)] * n_sums
        + comm.out_shape,
        scratch_shapes=([] if nk == 1 else [pltpu.VMEM((tm, tn), F32)]) + comm.sems,
        compiler_params=_params(("arbitrary",) * 3 if sequential else ("parallel", "parallel", "arbitrary")),
    )(a, b, *like, *rows, *comm.arrays)
    return res if (comm.n or tail) else res[0]


class _Ctx:
    def __init__(self, first, last):
        self.first = first
        self.last = last


def rowwise(fn, name, rows, seq, tb, ncol, ins, params=(), outs=(), accs=()):
    assert rows % tb == 0 and seq % tb == 0 and tb % 16 == 0
    bps = seq // tb
    nrow = rows // tb
    r8 = tb // SUBLANES
    args, in_specs = [], []
    for arr, w, off, halo in ins:
        args.append(arr)
        in_specs.append(pl.BlockSpec((tb, w), lambda j, i, off=off: (i, off + j)))
        if halo in ("prev", "both"):
            args.append(arr)
            in_specs.append(pl.BlockSpec((SUBLANES, w), lambda j, i, off=off: (jnp.maximum(i * r8 - 1, 0), off + j)))
        if halo in ("next", "both"):
            args.append(arr)
            in_specs.append(pl.BlockSpec(
                (SUBLANES, w), lambda j, i, off=off: (jnp.minimum((i + 1) * r8, rows // SUBLANES - 1), off + j)))
    for arr, w, off in params:
        args.append(arr)
        if w is None:
            in_specs.append(pl.BlockSpec(arr.shape, lambda j, i: (0, 0)))
        else:
            in_specs.append(pl.BlockSpec((arr.shape[0], w), lambda j, i, off=off: (0, off + j)))
    out_shape, out_specs = [], []
    for total, w, off, dt in outs:
        out_shape.append(jax.ShapeDtypeStruct((rows, total), dt))
        out_specs.append(pl.BlockSpec((tb, w), lambda j, i, off=off: (i, off + j)))
    for r, w in accs:
        out_shape.append(jax.ShapeDtypeStruct((r, ncol * w), F32))
        out_specs.append(pl.BlockSpec((r, w), lambda j, i: (0, j)))
    n_out, n_acc = len(outs), len(accs)

    def body(*refs):
        i = pl.program_id(1)
        pos = 0
        vals = []
        for _, _, _, halo in ins:
            cur = refs[pos][...]
            pos += 1
            if halo is None:
                vals.append(cur)
            elif halo == "both":
                vals.append((cur, refs[pos][...], refs[pos + 1][...]))
                pos += 2
            else:
                vals.append((cur, refs[pos][...]))
                pos += 1
        for _ in params:
            vals.append(refs[pos][...])
            pos += 1
        ctx = _Ctx(i % bps == 0, i % bps == bps - 1)
        res = fn(ctx, *vals)
        if not isinstance(res, (tuple, list)):
            res = (res,)
        assert len(res) == n_out + n_acc
        for q in range(n_out):
            refs[pos + q][...] = res[q].astype(refs[pos + q].dtype)
        for q in range(n_acc):
            ref, val = refs[pos + n_out + q], res[n_out + q]

            @pl.when(i == 0)
            def _(ref=ref, val=val):
                ref[...] = val

            @pl.when(i != 0)
            def _(ref=ref, val=val):
                ref[...] += val

    res = pl.pallas_call(
        body, name=name,
        grid=(ncol, nrow),
        in_specs=in_specs,
        out_specs=out_specs,
        out_shape=out_shape,
        compiler_params=_params(("parallel", "arbitrary")),
    )(*args)
    return res


GROUP_W = SSD_K * SSD_P


def _tri(lower):
    r = lax.broadcasted_iota(jnp.int32, (SSD_Q, SSD_Q), 0)
    c = lax.broadcasted_iota(jnp.int32, (SSD_Q, SSD_Q), 1)
    return r >= c if lower else r <= c


def _first_head_lanes():
    return lax.broadcasted_iota(jnp.int32, (1, LANES), 1) < SSD_P


def _column(v, j):
    return v[:, j * LANES:(j + 1) * LANES] if v.shape[-1] == GROUP_W else v


def _per_head(vals):
    first = _first_head_lanes()
    return jnp.concatenate([jnp.where(first, _column(vals[2 * j], j), _column(vals[2 * j + 1], j))
                            for j in range(GROUP_W // LANES)], axis=1)


def _per_head_rows(vals):
    return jnp.concatenate([jnp.broadcast_to(v, (SSD_P, 1)) for v in vals], axis=0)


def _own_columns(slab, k):
    keep = _first_head_lanes() if k % 2 == 0 else jnp.logical_not(_first_head_lanes())
    own = jnp.where(keep, _column(slab, k // 2), 0.0)
    return jnp.concatenate([own, jnp.zeros_like(own)] if k < 2 else [jnp.zeros_like(own), own], axis=1)


def _headsum(prod, g):
    out = None
    for k in range(SSD_K):
        keep = _first_head_lanes() if k % 2 == 0 else jnp.logical_not(_first_head_lanes())
        term = jnp.sum(jnp.where(keep, _column(prod, k // 2), 0.0), axis=1, keepdims=True) * _onehot_row(g * SSD_K + k)
        out = term if out is None else out + term
    return out


def ssd_fwd(xact, dtraw, dt_bias, a_log, d_skip, n_seq, seq, comm=NO_EXCHANGE):
    nc = seq // SSD_Q
    rows = n_seq * seq
    nx = comm.n

    def body(*refs):
        xact_ref, dtraw_ref, bias_ref, alog_ref, dskip_ref = refs[:5]
        y_ref, sin_ref = refs[5 + nx:7 + nx]
        state, cs_s, cst_s, dt_s = refs[7 + 2 * nx:11 + 2 * nx]
        b, c = pl.program_id(0), pl.program_id(1)
        if nx:
            finish = comm.begin(refs[5:5 + nx], refs[7 + nx:7 + 2 * nx], refs[11 + 2 * nx:],
                                jnp.logical_and(b == 0, c == 0))

        @pl.when(c == 0)
        def _():
            state[...] = jnp.zeros_like(state)

        sin_ref[0] = state[...]
        dt = _softplus(dtraw_ref[...] + bias_ref[...])
        a = dt * (-jnp.exp(alog_ref[...]))
        cs = _dot_f32(_tri(True).astype(F32), a)
        cs_s[...] = cs
        cst_s[...] = cs.T
        dt_s[...] = dt
        causal = _tri(True)
        def front(g):
            heads = [g * SSD_K + k for k in range(SSD_K)]
            bg = xact_ref[:, pl.ds(D_INNER + g * SSD_N, SSD_N)]
            cg = xact_ref[:, pl.ds(D_INNER + (SSD_G + g) * SSD_N, SSD_N)]
            xg = xact_ref[:, pl.ds(g * GROUP_W, GROUP_W)]
            cols = [cs_s[:, pl.ds(h, 1)] for h in heads]
            lasts = [cs_s[pl.ds(SSD_Q - 1, 1), pl.ds(h, 1)] for h in heads]
            xdg = xg * _per_head([dt_s[:, pl.ds(h, 1)] for h in heads])
            sg = state[g]
            y = (_per_head([jnp.exp(c_) for c_ in cols]) * _dot_nt(cg, sg)
                 + _per_head([dskip_ref[:, pl.ds(h, 1)] for h in heads]) * xg)
            w = _per_head([jnp.exp(l_ - c_) for l_, c_ in zip(lasts, cols)])
            state[g] = _per_head_rows([jnp.exp(l_) for l_ in lasts]) * sg + _dot_tn(w * xdg, bg)
            return heads, cols, _dot_nt(cg, bg), xdg, y

        def back(g, heads, cols, gm, xdg, y):
            mats = [gm * jnp.exp(jnp.where(causal, cols[k] - cst_s[pl.ds(h, 1), :], NEG)) for k, h in enumerate(heads)]
            y4 = _dot_nn(jnp.concatenate(mats, axis=0), xdg)
            y_ref[:, pl.ds(g * GROUP_W, GROUP_W)] = y + _per_head([y4[k * SSD_Q:(k + 1) * SSD_Q] for k in range(SSD_K)])

        ahead = front(0)
        for g in range(SSD_G):
            cur, ahead = ahead, (front(g + 1) if g + 1 < SSD_G else None)
            back(g, *cur)
        if nx:
            pl.when(jnp.logical_and(b == n_seq - 1, c == nc - 1))(finish)

    vec = pl.BlockSpec((1, LANES), lambda b, c: (0, 0))
    return pl.pallas_call(
        body, name="ssd_fwd",
        grid=(n_seq, nc),
        in_specs=[pl.BlockSpec((SSD_Q, CONV_DIM), lambda b, c: (b * nc + c, 0)),
                  pl.BlockSpec((SSD_Q, LANES), lambda b, c: (b * nc + c, 0)), vec, vec, vec] + comm.specs,
        out_specs=[pl.BlockSpec((SSD_Q, D_INNER), lambda b, c: (b * nc + c, 0)),
                   pl.BlockSpec((1, SSD_G, GROUP_W, SSD_N), lambda b, c: (b * nc + c, 0, 0, 0))] + comm.specs,
        out_shape=[jax.ShapeDtypeStruct((rows, D_INNER), F32),
                   jax.ShapeDtypeStruct((n_seq * nc, SSD_G, GROUP_W, SSD_N), F32)] + comm.out_shape,
        scratch_shapes=[pltpu.VMEM((SSD_G, GROUP_W, SSD_N), F32), pltpu.VMEM((SSD_Q, LANES), F32),
                        pltpu.VMEM((LANES, SSD_Q), F32), pltpu.VMEM((SSD_Q, LANES), F32)] + comm.sems,
        compiler_params=_params(("arbitrary", "arbitrary")),
    )(xact, dtraw, dt_bias, a_log, d_skip, *comm.arrays)


def ssd_bwd(xact, dtraw, dt_bias, a_log, d_skip, sin, dy, n_seq, seq, comm=NO_EXCHANGE):
    nc = seq // SSD_Q
    rows = n_seq * seq
    nx = comm.n

    def body(*refs):
        xact_ref, dtraw_ref, bias_ref, alog_ref, dskip_ref, sin_ref, dy_ref = refs[:7]
        dx_ref, ddt_ref, dbias_ref, dalog_ref, ddskip_ref = refs[7 + nx:12 + nx]
        dstate, cs_s, cst_s, dt_s = refs[12 + 2 * nx:16 + 2 * nx]
        b, c = pl.program_id(0), pl.program_id(1)
        if nx:
            finish = comm.begin(refs[7:7 + nx], refs[12 + nx:12 + 2 * nx], refs[16 + 2 * nx:],
                                jnp.logical_and(b == 0, c == 0))

        @pl.when(c == 0)
        def _():
            dstate[...] = jnp.zeros_like(dstate)

        pre = dtraw_ref[...] + bias_ref[...]
        dt = _softplus(pre)
        a_neg = -jnp.exp(alog_ref[...])
        cs = _dot_f32(_tri(True).astype(F32), dt * a_neg)
        cs_s[...] = cs
        cst_s[...] = cs.T
        dt_s[...] = dt
        causal, anti = _tri(True), _tri(False)
        is_last_row = lax.broadcasted_iota(jnp.int32, (SSD_Q, 1), 0) == SSD_Q - 1
        dcs_cf = jnp.zeros((SSD_Q, LANES), F32)
        dcs_rf = jnp.zeros((LANES, SSD_Q), F32)
        ddt_cf = jnp.zeros((SSD_Q, LANES), F32)
        dd_vec = jnp.zeros((1, LANES), F32)
        dlast_vec = jnp.zeros((1, LANES), F32)
        def front(g):
            heads = [g * SSD_K + k for k in range(SSD_K)]
            v = {"heads": heads}
            bg = v["bg"] = xact_ref[:, pl.ds(D_INNER + g * SSD_N, SSD_N)]
            cg = v["cg"] = xact_ref[:, pl.ds(D_INNER + (SSD_G + g) * SSD_N, SSD_N)]
            xg = v["xg"] = xact_ref[:, pl.ds(g * GROUP_W, GROUP_W)]
            dyg = v["dyg"] = dy_ref[:, pl.ds(g * GROUP_W, GROUP_W)]
            cols = [cs_s[:, pl.ds(h, 1)] for h in heads]
            rws = [cst_s[pl.ds(h, 1), :] for h in heads]
            lasts = [cs_s[pl.ds(SSD_Q - 1, 1), pl.ds(h, 1)] for h in heads]
            e_lasts = v["e_lasts"] = [jnp.exp(l_) for l_ in lasts]
            v["dtg"] = _per_head([dt_s[:, pl.ds(h, 1)] for h in heads])
            v["dskg"] = _per_head([dskip_ref[:, pl.ds(h, 1)] for h in heads])
            e_col = _per_head([jnp.exp(c_) for c_ in cols])
            w = v["w"] = _per_head([jnp.exp(l_ - c_) for l_, c_ in zip(lasts, cols)])
            xdg = v["xdg"] = xg * v["dtg"]
            sg = sin_ref[0, g]
            dsn = dstate[g]
            v["gm"] = _dot_nt(cg, bg)
            gmt = _dot_nt(bg, cg)
            v["y_off"] = e_col * _dot_nt(cg, sg)
            d_cs = e_col * dyg
            v["dcg"] = _dot_nn(d_cs, sg)
            dstate[g] = _dot_tn(d_cs, cg) + _per_head_rows(e_lasts) * dsn
            v["dbg"] = _dot_nn(w * xdg, dsn)
            v["dtt"] = _dot_nt(bg, dsn)
            v["dsn_s"] = dsn * sg
            segs = [cols[k] - rws[k] for k in range(SSD_K)]
            v["decays"] = [jnp.exp(jnp.where(causal, s_, NEG)) for s_ in segs]
            v["dm4"] = _dot_nt(jnp.concatenate([_own_columns(dyg, k) for k in range(SSD_K)], axis=0), xdg)
            v["z4"] = _dot_nn(jnp.concatenate([gmt * jnp.exp(jnp.where(anti, -s_, NEG)) for s_ in segs], axis=0), dyg)
            return v

        def back(g, v, sums):
            dcs_cf, dcs_rf, ddt_cf, dd_vec, dlast_vec = sums
            dxd = v["w"] * v["dtt"] + _per_head([v["z4"][k * SSD_Q:(k + 1) * SSD_Q] for k in range(SSD_K)])
            dw = _headsum(v["dtt"] * v["xdg"] * v["w"], g)
            dcs_cf = dcs_cf + _headsum(v["dyg"] * v["y_off"], g) - dw
            dlast_vec = dlast_vec + jnp.sum(dw, axis=0, keepdims=True)
            dgm = jnp.zeros((SSD_Q, SSD_Q), F32)
            for k, h in enumerate(v["heads"]):
                dm = v["dm4"][k * SSD_Q:(k + 1) * SSD_Q]
                dseg = dm * v["gm"] * v["decays"][k]
                dgm = dgm + dm * v["decays"][k]
                oh_r = _onehot_row(h)
                dcs_cf = dcs_cf + jnp.sum(dseg, axis=1, keepdims=True) * oh_r
                dcs_rf = dcs_rf - _onehot_col(h) * jnp.sum(dseg, axis=0, keepdims=True)
                dlast_vec = dlast_vec + (jnp.sum(v["dsn_s"][k * SSD_P:(k + 1) * SSD_P], keepdims=True)
                                         * v["e_lasts"][k] * oh_r)
            dx_ref[:, pl.ds(g * GROUP_W, GROUP_W)] = dxd * v["dtg"] + v["dskg"] * v["dyg"]
            ddt_cf = ddt_cf + _headsum(dxd * v["xg"], g)
            dd_vec = dd_vec + _headsum(jnp.sum(v["dyg"] * v["xg"], axis=0, keepdims=True), g)
            dx_ref[:, pl.ds(D_INNER + g * SSD_N, SSD_N)] = v["dbg"] + _dot_tn(dgm, v["cg"])
            dx_ref[:, pl.ds(D_INNER + (SSD_G + g) * SSD_N, SSD_N)] = v["dcg"] + _dot_nn(dgm, v["bg"])
            return dcs_cf, dcs_rf, ddt_cf, dd_vec, dlast_vec

        sums = (dcs_cf, dcs_rf, ddt_cf, dd_vec, dlast_vec)
        ahead = front(0)
        for g in range(SSD_G):
            cur, ahead = ahead, (front(g + 1) if g + 1 < SSD_G else None)
            sums = back(g, cur, sums)
        dcs_cf, dcs_rf, ddt_cf, dd_vec, dlast_vec = sums
        dcs = dcs_cf + dcs_rf.T + jnp.where(is_last_row, dlast_vec, 0.0)
        da = _dot_f32(_tri(False).astype(F32), dcs)
        ddt = ddt_cf + da * a_neg
        ddtraw = ddt * _sigmoid(pre)
        ddt_ref[...] = ddtraw.astype(ddt_ref.dtype)
        dbias = jnp.sum(ddtraw, axis=0, keepdims=True)
        dalog = jnp.sum(da * dt, axis=0, keepdims=True) * a_neg
        first_step = jnp.logical_and(b == 0, c == 0)

        @pl.when(first_step)
        def _():
            dbias_ref[...] = dbias
            dalog_ref[...] = dalog
            ddskip_ref[...] = dd_vec

        @pl.when(jnp.logical_not(first_step))
        def _():
            dbias_ref[...] += dbias
            dalog_ref[...] += dalog
            ddskip_ref[...] += dd_vec

        if nx:
            pl.when(jnp.logical_and(b == n_seq - 1, c == nc - 1))(finish)

    def rowblk(b, c):
        return b * nc + (nc - 1 - c)

    vec = pl.BlockSpec((1, LANES), lambda b, c: (0, 0))
    return pl.pallas_call(
        body, name="ssd_bwd",
        grid=(n_seq, nc),
        in_specs=[pl.BlockSpec((SSD_Q, CONV_DIM), lambda b, c: (rowblk(b, c), 0)),
                  pl.BlockSpec((SSD_Q, LANES), lambda b, c: (rowblk(b, c), 0)), vec, vec, vec,
                  pl.BlockSpec((1, SSD_G, GROUP_W, SSD_N), lambda b, c: (rowblk(b, c), 0, 0, 0)),
                  pl.BlockSpec((SSD_Q, D_INNER), lambda b, c: (rowblk(b, c), 0))] + comm.specs,
        out_specs=[pl.BlockSpec((SSD_Q, CONV_DIM), lambda b, c: (rowblk(b, c), 0)),
                   pl.BlockSpec((SSD_Q, LANES), lambda b, c: (rowblk(b, c), 0)), vec, vec, vec] + comm.specs,
        out_shape=[jax.ShapeDtypeStruct((rows, CONV_DIM), F32), jax.ShapeDtypeStruct((rows, LANES), BF16),
                   jax.ShapeDtypeStruct((1, LANES), F32), jax.ShapeDtypeStruct((1, LANES), F32),
                   jax.ShapeDtypeStruct((1, LANES), F32)] + comm.out_shape,
        scratch_shapes=[pltpu.VMEM((SSD_G, GROUP_W, SSD_N), F32), pltpu.VMEM((SSD_Q, LANES), F32),
                        pltpu.VMEM((LANES, SSD_Q), F32), pltpu.VMEM((SSD_Q, LANES), F32)] + comm.sems,
        compiler_params=_params(("arbitrary", "arbitrary")),
    )(xact, dtraw, dt_bias, a_log, d_skip, sin, dy, *comm.arrays)


QKV_W = 3 * ATT_OUT
PAIR_W = 2 * ATT_HD
HEAD_PAIRS = ATT_H // 2
PREP_ROWS = 512


def _by_residue(a, n_seq, seq, dil):
    if dil == 1:
        return a
    return a.reshape(n_seq, seq // dil, dil, a.shape[1]).transpose(0, 2, 1, 3).reshape(a.shape)


def _by_token(a, n_seq, seq, dil):
    if dil == 1:
        return a
    return a.reshape(n_seq, dil, seq // dil, a.shape[1]).transpose(0, 2, 1, 3).reshape(a.shape)


def _head_sums(x, fn):
    lo = jnp.logical_not(lax.broadcasted_iota(jnp.int32, (1, 2 * ATT_HD), 1) >= ATT_HD)
    parts = []
    for p in range(ATT_H // 2):
        slab = x[:, p * 2 * ATT_HD:(p + 1) * 2 * ATT_HD]
        s_lo = fn(jnp.sum(jnp.where(lo, slab, 0.0), axis=1, keepdims=True))
        s_hi = fn(jnp.sum(jnp.where(lo, 0.0, slab), axis=1, keepdims=True))
        parts.append(jnp.where(lo, s_lo, s_hi))
    return jnp.concatenate(parts, axis=1)


def _head_rstd(x):
    return _head_sums(x * x, lambda s: lax.rsqrt(s * (1.0 / ATT_HD) + EPS))


def _head_rms_bwd(x, g_t, dy):
    r = _head_rstd(x)
    xh = x * r
    dyg = dy * g_t
    mean = _head_sums(dyg * xh, lambda s: s * (1.0 / ATT_HD))
    return r * (dyg - xh * mean), jnp.sum(dy * xh, axis=0, keepdims=True)


def _lane_hi():
    return lax.broadcasted_iota(jnp.int32, (1, PAIR_W), 1) >= ATT_HD


def _band_mask2(first_valid, query_rows):
    i = lax.broadcasted_iota(jnp.int32, (ATT_BLK, 2 * ATT_BLK), 0)
    j = lax.broadcasted_iota(jnp.int32, (ATT_BLK, 2 * ATT_BLK), 1)
    left = j < ATT_BLK
    right = jnp.logical_not(left)
    if query_rows:
        return jnp.logical_or(jnp.logical_and(jnp.logical_and(left, i <= j), first_valid),
                              jnp.logical_and(right, i >= j - ATT_BLK))
    return jnp.logical_or(jnp.logical_and(left, j >= i),
                          jnp.logical_and(jnp.logical_and(right, j - ATT_BLK <= i), first_valid))


def _only_head(slab, hi):
    keep = _lane_hi() if hi else jnp.logical_not(_lane_hi())
    return jnp.where(keep, slab, jnp.zeros_like(slab))


def attn_fwd(nq, n_seq, seq, dil, name):
    nb = seq // dil // ATT_BLK
    rows = n_seq * seq

    def body(cur_ref, prev_ref, o_ref, lse_ref, s_scr, p_scr):
        n = pl.program_id(1)
        mask = _band_mask2(n > 0, True)
        for h in range(ATT_H):
            sl = pl.ds((h // 2) * PAIR_W, PAIR_W)
            ks = pl.ds(ATT_OUT + (h // 2) * PAIR_W, PAIR_W)
            kcat = jnp.concatenate([prev_ref[:, ks], cur_ref[:, ks]], axis=0)
            s_scr[h] = jnp.where(mask, _dot_nt(_only_head(cur_ref[:, sl], h % 2), kcat), NEG)
        s_all = s_scr[...]
        mx = jnp.max(s_all, axis=2, keepdims=True)
        p_all = jnp.exp(s_all - mx)
        den = jnp.sum(p_all, axis=2, keepdims=True)
        p_scr[...] = p_all.astype(p_scr.dtype)
        inv = 1.0 / den
        lse = mx + jnp.log(den)
        lse_blk = jnp.zeros((ATT_BLK, LANES), F32)
        for h in range(ATT_H):
            lse_blk = lse_blk + lse[h] * _onehot_row(h)
        lse_ref[...] = lse_blk
        for pr in range(HEAD_PAIRS):
            vs = pl.ds(2 * ATT_OUT + pr * PAIR_W, PAIR_W)
            vcat = jnp.concatenate([prev_ref[:, vs], cur_ref[:, vs]], axis=0)
            lo = _dot_nn(p_scr[2 * pr], vcat) * inv[2 * pr]
            hi = _dot_nn(p_scr[2 * pr + 1], vcat) * inv[2 * pr + 1]
            o_ref[:, pl.ds(pr * PAIR_W, PAIR_W)] = jnp.where(_lane_hi(), hi, lo)

    def blk(width, shift):
        if shift:
            return pl.BlockSpec((ATT_BLK, width), lambda s, n: (s * nb + jnp.maximum(n - 1, 0), 0))
        return pl.BlockSpec((ATT_BLK, width), lambda s, n: (s * nb + n, 0))

    return pl.pallas_call(
        body, name=name,
        grid=(n_seq * dil, nb),
        in_specs=[blk(QKV_W, 0), blk(QKV_W, -1)],
        out_specs=[blk(ATT_OUT, 0), blk(LANES, 0)],
        out_shape=[jax.ShapeDtypeStruct((rows, ATT_OUT), F32), jax.ShapeDtypeStruct((rows, LANES), F32)],
        scratch_shapes=[pltpu.VMEM((ATT_H, ATT_BLK, 2 * ATT_BLK), F32), pltpu.VMEM((ATT_H, ATT_BLK, 2 * ATT_BLK), MXU)],
        compiler_params=_params(("parallel", "arbitrary")),
    )(nq, nq)


def attn_bwd(nq, do, lse, wts, rsum, n_seq, seq, dil, name):
    nb = seq // dil // ATT_BLK

    def body(prev_ref, cur_ref, nxt_ref, do_c, do_x, lse_c, lse_x, wt_c, wt_x, rs_c, rs_x, dn_ref):
        n = pl.program_id(1)
        mask_q = _band_mask2(n > 0, True)
        mask_k = _band_mask2(n < nb - 1, False)
        wc, wx = wt_c[...], wt_x[...]
        lse_t = jnp.concatenate([lse_c[...].T, lse_x[...].T], axis=1)
        dl_t = jnp.concatenate([(-wc * rs_c[...]).T, (-wx * rs_x[...]).T], axis=1)
        def operands(pr):
            sl = pl.ds(pr * PAIR_W, PAIR_W)
            ks = pl.ds(ATT_OUT + pr * PAIR_W, PAIR_W)
            vs = pl.ds(2 * ATT_OUT + pr * PAIR_W, PAIR_W)
            he, ho = pl.ds(2 * pr, 1), pl.ds(2 * pr + 1, 1)
            q_c, k_c, v_c = cur_ref[:, sl], cur_ref[:, ks], cur_ref[:, vs]
            dog_c = do_c[:, sl] * jnp.where(_lane_hi(), wt_c[:, ho], wt_c[:, he])
            dog_x = do_x[:, sl] * jnp.where(_lane_hi(), wt_x[:, ho], wt_x[:, he])
            return dict(q_c=q_c, k_c=k_c, v_c=v_c, qcat=jnp.concatenate([q_c, nxt_ref[:, sl]], axis=0),
                        kcat=jnp.concatenate([prev_ref[:, ks], k_c], axis=0),
                        vcat=jnp.concatenate([prev_ref[:, vs], v_c], axis=0),
                        dog=jnp.concatenate([dog_c, dog_x], axis=0).astype(MXU))

        def scores(o, h):
            hi, one = h % 2, pl.ds(h, 1)
            dl_col = -wt_c[:, one] * rs_c[:, one]
            p_q = jnp.exp(jnp.where(mask_q, _dot_nt(_only_head(o["q_c"], hi), o["kcat"]) - lse_c[:, one], NEG))
            ds_q = p_q * (_dot_nt(_only_head(o["dog"][:ATT_BLK], hi), o["vcat"]) + dl_col)
            p_t = jnp.exp(jnp.where(mask_k, _dot_nt(_only_head(o["k_c"], hi), o["qcat"]) - lse_t[h:h + 1, :], NEG))
            ds_t = p_t * (_dot_nt(_only_head(o["v_c"], hi), o["dog"]) + dl_t[h:h + 1, :])
            return ds_q, ds_t, p_t

        ops = [operands(pr) for pr in range(HEAD_PAIRS)]
        ahead = scores(ops[0], 0)
        res = []
        for h in range(ATT_H):
            o = ops[h // 2]
            (ds_q, ds_t, p_t), ahead = ahead, (scores(ops[(h + 1) // 2], h + 1) if h + 1 < ATT_H else None)
            res.append((_dot_nn(ds_q, o["kcat"]), _dot_nn(ds_t, o["qcat"]), _dot_nn(p_t, o["dog"])))
            if h % 2:
                for t, first in enumerate((0, ATT_OUT, 2 * ATT_OUT)):
                    dn_ref[:, pl.ds(first + (h // 2) * PAIR_W, PAIR_W)] = jnp.where(_lane_hi(), res[h][t], res[h - 1][t])

    def at(shift, width):
        if shift < 0:
            return pl.BlockSpec((ATT_BLK, width), lambda s, n: (s * nb + jnp.maximum(n - 1, 0), 0))
        if shift > 0:
            return pl.BlockSpec((ATT_BLK, width), lambda s, n: (s * nb + jnp.minimum(n + 1, nb - 1), 0))
        return pl.BlockSpec((ATT_BLK, width), lambda s, n: (s * nb + n, 0))

    return pl.pallas_call(
        body, name=name,
        grid=(n_seq * dil, nb),
        in_specs=[at(-1, QKV_W), at(0, QKV_W), at(1, QKV_W), at(0, ATT_OUT), at(1, ATT_OUT),
                  at(0, LANES), at(1, LANES), at(0, LANES), at(1, LANES), at(0, LANES), at(1, LANES)],
        out_specs=at(0, QKV_W),
        out_shape=jax.ShapeDtypeStruct((n_seq * seq, QKV_W), F32),
        compiler_params=_params(("parallel", "arbitrary")),
    )(nq, nq, nq, do, do, lse, lse, wts, wts, rsum, rsum)


def qk_post(qkv, dn, gq_t, gk_t, rows, name):
    tb = min(PREP_ROWS, rows)

    def body(x_ref, dn_ref, gq_ref, gk_ref, o_ref, dgq_ref, dgk_ref):
        i = pl.program_id(0)
        qs, ks, vs = pl.ds(0, ATT_OUT), pl.ds(ATT_OUT, ATT_OUT), pl.ds(2 * ATT_OUT, ATT_OUT)
        dq, dgq = _head_rms_bwd(x_ref[:, qs], gq_ref[...], dn_ref[:, qs] * ATT_SCALE)
        dk, dgk = _head_rms_bwd(x_ref[:, ks], gk_ref[...], dn_ref[:, ks])
        o_ref[:, qs] = dq.astype(o_ref.dtype)
        o_ref[:, ks] = dk.astype(o_ref.dtype)
        o_ref[:, vs] = dn_ref[:, vs].astype(o_ref.dtype)

        @pl.when(i == 0)
        def _():
            dgq_ref[...] = dgq
            dgk_ref[...] = dgk

        @pl.when(i != 0)
        def _():
            dgq_ref[...] += dgq
            dgk_ref[...] += dgk

    gspec = pl.BlockSpec((1, ATT_OUT), lambda i: (0, 0))
    blk = pl.BlockSpec((tb, QKV_W), lambda i: (i, 0))
    return pl.pallas_call(
        body, name=name,
        grid=(rows // tb,),
        in_specs=[blk, blk, gspec, gspec],
        out_specs=[blk, gspec, gspec],
        out_shape=[jax.ShapeDtypeStruct((rows, QKV_W), MXU), jax.ShapeDtypeStruct((1, ATT_OUT), F32),
                   jax.ShapeDtypeStruct((1, ATT_OUT), F32)],
        compiler_params=_params(("arbitrary",)),
    )(qkv, dn, gq_t, gk_t)


def all_gather(arrays, name):
    n = len(arrays)

    def body(*refs):
        in_refs, out_refs = refs[:n], refs[n:2 * n]
        send_sems, recv_sems, local_sems = refs[2 * n:]
        _, me, _ = _peer(0)
        sib, _, _ = _peer(1)

        def first(t, k, arriving):
            dev, pid, _ = _peer(k)
            return _remote(in_refs[t], out_refs[t].at[pid if arriving else me], send_sems, recv_sems, t, k, dev)

        def passed(t, k, arriving):
            slot = out_refs[t].at[_peer(k + 1 if arriving else k)[1]]
            return _remote(slot, slot, send_sems, recv_sems, t, k + 1, sib)

        def own(t):
            return pltpu.make_async_copy(in_refs[t], out_refs[t].at[me], local_sems.at[t])

        for t in range(n):
            own(t).start()
            for k in (1,) + OTHER_CHIPS:
                first(t, k, False).start()
        for t in range(n):
            for k in OTHER_CHIPS:
                first(t, k, True).wait_recv()
                passed(t, k, False).start()
        for t in range(n):
            first(t, 1, True).wait_recv()
            for k in OTHER_CHIPS:
                passed(t, k, True).wait_recv()
        for t in range(n):
            for k in (1,) + OTHER_CHIPS:
                first(t, k, False).wait_send()
            for k in OTHER_CHIPS:
                passed(t, k, False).wait_send()
            own(t).wait()

    anyspec = pl.BlockSpec(memory_space=pl.ANY)
    return pl.pallas_call(
        body, name=name,
        in_specs=[anyspec] * n,
        out_specs=[anyspec] * n,
        out_shape=[jax.ShapeDtypeStruct((N_DEV,) + tuple(a.shape), a.dtype) for a in arrays],
        scratch_shapes=[pltpu.SemaphoreType.DMA((n, N_DEV)), pltpu.SemaphoreType.DMA((n, N_DEV)),
                        pltpu.SemaphoreType.DMA((n,))],
    )(*arrays)


def pair_add(a, b, name):
    _, r, c = a.shape
    rb = r if r <= 512 else (128 if c > 1024 else 256)
    assert r % rb == 0

    def body(a_ref, b_ref, o_ref):
        o_ref[...] = (a_ref[...].astype(F32) + b_ref[...].astype(F32)).astype(o_ref.dtype)

    blk = pl.BlockSpec((1, rb, c), lambda q, i: (q, i, 0))
    return pl.pallas_call(
        body, name=name,
        grid=(N_CHIPS, r // rb),
        in_specs=[blk, blk],
        out_specs=blk,
        out_shape=jax.ShapeDtypeStruct(a.shape, a.dtype),
        compiler_params=_params(("parallel", "parallel")),
    )(a, b)


def adamw(parts, w, m, v, name):
    r, c = w.shape[-2:]
    n_parts = parts.shape[0]
    rb = r if r <= 512 else (128 if c > 1024 else 256)
    assert r % rb == 0

    def body(p_ref, w_ref, m_ref, v_ref, g_out, d_out, m_out, v_out):
        g = p_ref[0].astype(F32)
        for i in range(1, n_parts):
            g = g + p_ref[i].astype(F32)
        m_new = ADAM_B1 * m_ref[...] + (1.0 - ADAM_B1) * g
        v_new = ADAM_B2 * v_ref[...] + (1.0 - ADAM_B2) * (g * g)
        m_hat = m_new / (1.0 - ADAM_B1 ** ADAM_STEP)
        v_hat = v_new / (1.0 - ADAM_B2 ** ADAM_STEP)
        g_out[...] = g
        d_out[...] = -ADAM_LR * (m_hat / (jnp.sqrt(v_hat) + ADAM_EPS) + ADAM_WD * w_ref[...])
        m_out[...] = m_new
        v_out[...] = v_new

    if w.ndim == 3:
        blk = pl.BlockSpec((None, rb, c), lambda i: (0, i, 0))
    else:
        blk = pl.BlockSpec((rb, c), lambda i: (i, 0))
    return pl.pallas_call(
        body, name=name,
        grid=(r // rb,),
        in_specs=[pl.BlockSpec((n_parts, rb, c), lambda i: (0, i, 0)), blk, blk, blk],
        out_specs=[blk] * 4,
        out_shape=[jax.ShapeDtypeStruct(w.shape, F32)] * 4,
        compiler_params=_params(("parallel",)),
    )(parts, w, m, v)


def _pad_lanes(vec, n=LANES):
    return jnp.pad(vec, ((0, 0), (0, n - vec.shape[1])))


COL_SHARDED = ("w_in", "ssd_conv_w", "w_attn_proj", "w_up", "ffn_conv_w")
MATRICES = ("w_in", "w_attn_proj", "w_up", "w_ssd_proj", "w_out", "w_down")
LATE = ("w_ssd_proj", "w_attn_proj", "w_out", "w_up", "ffn_conv_w", "w_down")


def _narrow(name, a):
    return a.astype(MXU) if name in MATRICES else a


def _from_gathered(name, g):
    if name in COL_SHARDED:
        return jnp.transpose(g, (1, 0, 2)).reshape(g.shape[1], N_DEV * g.shape[2])
    return g.reshape(N_DEV * g.shape[1], g.shape[2])


def _to_slabs(name, g):
    if name in COL_SHARDED:
        return jnp.transpose(g.reshape(g.shape[0], N_DEV, g.shape[1] // N_DEV), (1, 0, 2))
    return g.reshape(N_DEV, g.shape[0] // N_DEV, g.shape[1])


def _columns(m, a, b):
    if m.ndim == 2:
        return m[:, a:b]
    c = m.shape[2]
    cuts = [m[j][:, max(a - j * c, 0):min(b - j * c, c)] for j in range(a // c, (b - 1) // c + 1)]
    return cuts[0] if len(cuts) == 1 else jnp.concatenate(cuts, axis=1)


def _column_shards(pieces, c):
    shards = []
    for j in range(N_DEV):
        cuts = []
        for start, arr in pieces:
            lo, hi = max(j * c - start, 0), min((j + 1) * c - start, arr.shape[1])
            if lo < hi:
                cuts.append(arr[:, lo:hi])
        shards.append(cuts[0] if len(cuts) == 1 else jnp.concatenate(cuts, axis=1))
    return jnp.stack(shards)


def local_step(x, target, w, late=None):
    n_seq, seq, _ = x.shape
    rows = n_seq * seq
    x = x.reshape(rows, D_MODEL)
    target = target.reshape(rows, D_MODEL)
    mx = lambda a: a.astype(MXU)

    splits = [sum(IN_WIDTHS[:i]) for i in range(len(IN_WIDTHS) + 1)]
    w_in = w["w_in"]
    part = lambda i: _columns(w_in, splits[i], splits[i + 1])
    w_z, w_xbc, w_gs, w_ga = mx(part(0)), mx(part(1)), mx(part(6)), mx(part(7))
    w_dt = mx(_pad_lanes(part(2)))
    head_group = lambda t, g: (splits[3 + t] + g * ATT_OUT, splits[3 + t] + (g + 1) * ATT_OUT)
    w_qkv = [mx(jnp.concatenate([_columns(w_in, *head_group(t, g)) for t in range(3)], axis=1))
             for g in range(ATT_GROUPS)]
    conv_w, conv_b, fconv_b = w["ssd_conv_w"], w["ssd_conv_b"], w["ffn_conv_b"]
    dt_bias, a_log, d_skip = _pad_lanes(w["dt_bias"]), _pad_lanes(w["a_log"]), _pad_lanes(w["d_skip"])
    g1, g2, gn, gq, gk = w["norm1_g"], w["norm2_g"], w["ssd_norm_g"], w["q_norm_g"], w["k_norm_g"]

    tb = min(512, seq)
    tbm = min(256, seq)
    cw = 1024
    rw = lambda fn, name, ncol, ins, params=(), outs=(), accs=(), tb_=tb: rowwise(
        fn, name, rows, seq, tb_, ncol, ins, params, outs, accs)

    (h,) = rw(lambda ctx, xv, g: _rms_fwd(xv, g), "rms1_fwd", 1, [(x, D_MODEL, 0, None)], [(g1, None, 0)],
              [(D_MODEL, D_MODEL, 0, MXU)])
    z = matmul(h, w_z, "mm_z")
    xbc = matmul(h, w_xbc, "mm_xbc")
    dtraw = matmul(h, w_dt, "mm_dt")
    by_residue = lambda a, g: _by_residue(a, n_seq, seq, ATT_DILATIONS[g])
    by_token = lambda a, g: _by_token(a, n_seq, seq, ATT_DILATIONS[g])
    h_res = [by_residue(h, g) for g in range(ATT_GROUPS)]
    gq_t, gk_t = jnp.tile(gq, (1, ATT_H)), jnp.tile(gk, (1, ATT_H))
    qkv_gains = jnp.concatenate([gq_t * ATT_SCALE, gk_t, jnp.ones_like(gk_t)], axis=1)

    def qk_norm(r, gains):
        q, k = r[:, :ATT_OUT], r[:, ATT_OUT:2 * ATT_OUT]
        return r, jnp.concatenate([q * _head_rstd(q), k * _head_rstd(k), r[:, 2 * ATT_OUT:]], axis=1) * gains

    qkv, nq = zip(*[matmul(h_res[g], w_qkv[g], f"mm_qkv{g}", tail=_Tail(qk_norm, rows=[qkv_gains], outs=[F32, MXU]))
                    for g in range(ATT_GROUPS)])
    gs = matmul(h, w_gs, "mm_gs")
    ga = matmul(h, w_ga, "mm_ga")

    def conv_silu(ctx, xh, wv, bv):
        pre = bv + _conv_prev(xh[0], xh[1], wv, ctx.first, SSD_CONV)
        return _silu(pre), pre

    xact, xpre = rw(conv_silu, "ssd_conv_fwd", CONV_DIM // cw, [(xbc, cw, 0, "prev")],
                    [(conv_w, cw, 0), (conv_b, cw, 0)], [(CONV_DIM, cw, 0, F32), (CONV_DIM, cw, 0, F32)])
    if late is None:
        y, sin = ssd_fwd(xact, dtraw, dt_bias, a_log, d_skip, n_seq, seq)
    else:
        y, sin, *gathered = ssd_fwd(xact, dtraw, dt_bias, a_log, d_skip, n_seq, seq,
                                    comm=direct_exchange([(late[n], "gather") for n in LATE]))
        w = {**w, **{n: g if n == "w_up" else _from_gathered(n, g) for n, g in zip(LATE, gathered)}}
    w_sp, w_ap, w_o, w_d = mx(w["w_ssd_proj"]), mx(w["w_attn_proj"]), mx(w["w_out"]), mx(w["w_down"])
    w_ug, w_uv = mx(_columns(w["w_up"], 0, D_FF)), mx(_columns(w["w_up"], D_FF, 2 * D_FF))
    fconv_w = w["ffn_conv_w"]

    def gated_norm(ctx, yv, zv, g):
        yz = yv * _silu(zv)
        return jnp.concatenate([_rms_fwd(yz[:, i:i + NORM_GROUP], g[:, i:i + NORM_GROUP])
                                for i in range(0, cw, NORM_GROUP)], axis=1)

    (y_ssd,) = rw(gated_norm, "ssd_post_fwd", D_INNER // cw, [(y, cw, 0, None), (z, cw, 0, None)], [(gn, cw, 0)],
                  [(D_INNER, cw, 0, MXU)])

    att = [attn_fwd(nq[g], n_seq, seq, ATT_DILATIONS[g], f"attn_fwd{g}") for g in range(ATT_GROUPS)]

    def combine(ctx, o0, o1, o2, l0, l1, l2):
        mxl = jnp.maximum(jnp.maximum(l0, l1), l2)
        e = [jnp.exp(l - mxl) for l in (l0, l1, l2)]
        inv = 1.0 / (e[0] + e[1] + e[2])
        ws = [ei * inv for ei in e]
        out = sum(_expand_heads(wi) * oi for wi, oi in zip(ws, (o0, o1, o2)))
        return (out, *ws)

    y_attn, wt0, wt1, wt2 = rw(
        combine, "attn_combine", 1,
        [(by_token(att[g][0], g), ATT_OUT, 0, None) for g in range(3)]
        + [(by_token(att[g][1], g), LANES, 0, None) for g in range(3)], [],
        [(ATT_OUT, ATT_OUT, 0, F32)] + [(LANES, LANES, 0, F32)] * 3)
    wts = (wt0, wt1, wt2)

    ps = matmul(y_ssd, w_sp, "mm_ssd_proj")
    pa, merged = matmul(y_attn, w_ap, "mm_attn_proj",
                        tail=_Tail(lambda r, a, c, d: (r, _sigmoid(c) * a + _sigmoid(d) * r), like=[ps, gs, ga],
                                   outs=[F32, MXU]))
    x1, h2 = matmul(merged, w_o, "mm_out", add=x,
                    tail=_Tail(lambda r, g: (r, _rms_fwd(r, g)), rows=[g2], outs=[F32, MXU]))
    up_g = matmul(h2, w_ug, "mm_up_g")
    up_v = matmul(h2, w_uv, "mm_up_v")
    fw = D_FF // 2
    nfc = D_FF // fw

    def mlp_act(ctx, ug, uv, wg, wv, bg, bv):
        cg = bg + _conv_prev(ug[0], ug[1], wg, ctx.first, FFN_CONV)
        cv = bv + _conv_prev(uv[0], uv[1], wv, ctx.first, FFN_CONV)
        return _silu(cg) * cv, cg, cv

    act, conv_g, conv_v = rw(mlp_act, "mlp_act_fwd", nfc, [(up_g, fw, 0, "prev"), (up_v, fw, 0, "prev")],
                             [(fconv_w, fw, 0), (fconv_w, fw, nfc), (fconv_b, fw, 0), (fconv_b, fw, nfc)],
                             [(D_FF, fw, 0, MXU), (D_FF, fw, 0, F32), (D_FF, fw, 0, F32)], tb_=tbm)
    def loss_tail(out, tv):
        d = out - tv
        g = d * (1.0 / D_MODEL)
        return g, g, jnp.sum(d * d, axis=0, keepdims=True)

    dx2, dx2_m, sq = matmul(act, w_d, "mm_down", add=x1,
                            tail=_Tail(loss_tail, like=[target], outs=[F32, MXU], n_sums=1))

    grads = {}
    dact = matmul(dx2_m, w_d, "mm_d_act", tb=True)
    grads["w_down"] = matmul(act, dx2_m, "mm_dw_down", ta=True, out_dtype=MXU)

    def mlp_bwd(ctx, da, cg, cv, ug, uv, wg, wv):
        (da_c, da_n), (cg_c, cg_n), (cv_c, cv_n) = da, cg, cv
        dup_g_, dwg, dbg = _conv_bwd(da_c * cv_c * _silu_grad(cg_c), da_n * cv_n * _silu_grad(cg_n), ug, wg, ctx,
                                     FFN_CONV)
        dup_v_, dwv, dbv = _conv_bwd(da_c * _silu(cg_c), da_n * _silu(cg_n), uv, wv, ctx, FFN_CONV)
        return dup_g_, dup_v_, dwg, dbg, dwv, dbv

    dup_g, dup_v, dfw_g, dfb_g, dfw_v, dfb_v = rw(
        mlp_bwd, "mlp_bwd", nfc,
        [(dact, fw, 0, "next"), (conv_g, fw, 0, "next"), (conv_v, fw, 0, "next"), (up_g, fw, 0, None),
         (up_v, fw, 0, None)], [(fconv_w, fw, 0), (fconv_w, fw, nfc)],
        [(D_FF, fw, 0, MXU), (D_FF, fw, 0, MXU)], [(FFN_CONV, fw), (1, fw), (FFN_CONV, fw), (1, fw)], tb_=tbm)
    grads["ffn_conv_w"] = jnp.concatenate([dfw_g, dfw_v], axis=1)
    grads["ffn_conv_b"] = jnp.concatenate([dfb_g, dfb_v], axis=1)
    def rms_bwd_fn(dh_, xv, dres, g):
        dxv, dg = _rms_bwd(xv, g, dh_)
        return dres + dxv, dg

    def rms_bwd_fn2(dh_, xv, dres, g):
        dxv, dg = rms_bwd_fn(dh_, xv, dres, g)
        return dxv, dxv, dg

    dh2 = matmul(dup_g, w_ug, "mm_dh2_g", tb=True)
    dx1, dx1_m, grads["norm2_g"] = matmul(
        dup_v, w_uv, "mm_dh2_v", tb=True, add=dh2,
        tail=_Tail(rms_bwd_fn2, like=[x1, dx2], rows=[g2], outs=[F32, MXU], n_sums=1))
    dw_up =[(0, matmul(h2, dup_g, "mm_dw_up_g", ta=True, out_dtype=MXU)),
             (D_FF, matmul(h2, dup_v, "mm_dw_up_v", ta=True, out_dtype=MXU))]
    if w["w_up"].ndim == 3:
        grads["w_up"] = _column_shards(dw_up, w["w_up"].shape[2])
    else:
        grads["w_up"] = jnp.concatenate([p for _, p in dw_up], axis=1)

    def merge_bwd(dm, a, b, c, d):
        sc, sd = _sigmoid(c), _sigmoid(d)
        return dm * sc, dm * sd, dm * a * sc * (1.0 - sc), dm * b * sd * (1.0 - sd)

    dps, dpa, dgs, dga = matmul(dx1_m, w_o, "mm_d_merged", tb=True,
                                tail=_Tail(merge_bwd, like=[ps, pa, gs, ga], outs=[MXU] * 4))
    grads["w_out"] = matmul(merged, dx1_m, "mm_dw_out", ta=True, out_dtype=MXU)

    def gated_norm_bwd(dyn, yv, zv, g):
        sz = _silu(zv)
        yz = yv * sz
        dyz, dgs_ = [], []
        for i in range(0, D_INNER, NORM_GROUP):
            a, b = _rms_bwd(yz[:, i:i + NORM_GROUP], g[:, i:i + NORM_GROUP], dyn[:, i:i + NORM_GROUP])
            dyz.append(a)
            dgs_.append(b)
        dyz = jnp.concatenate(dyz, axis=1)
        return dyz * sz, dyz * yv * _silu_grad(zv), jnp.concatenate(dgs_, axis=1)

    dy, dz, grads["ssd_norm_g"] = matmul(dps, w_sp, "mm_d_y_ssd", tb=True,
                                         tail=_Tail(gated_norm_bwd, like=[y, z], rows=[gn], outs=[F32, MXU], n_sums=1))
    grads["w_ssd_proj"] = matmul(y_ssd, dps, "mm_dw_ssd_proj", ta=True, out_dtype=MXU)
    dy_attn = matmul(dpa, w_ap, "mm_d_y_attn", tb=True)
    grads["w_attn_proj"] = matmul(y_attn, dpa, "mm_dw_attn_proj", ta=True, out_dtype=MXU)

    (rsum,) = rw(lambda ctx, a, b: _reduce_heads(a * b), "attn_rsum", 1,
                 [(dy_attn, ATT_OUT, 0, None), (y_attn, ATT_OUT, 0, None)], [], [(LANES, LANES, 0, F32)])
    dqkv, dgq, dgk = [], 0.0, 0.0
    for g in range(ATT_GROUPS):
        dn = attn_bwd(nq[g], by_residue(dy_attn, g), att[g][1], by_residue(wts[g], g), by_residue(rsum, g), n_seq, seq,
                      ATT_DILATIONS[g], f"attn_bwd{g}")
        d_, a_, b_ = qk_post(qkv[g], dn, gq_t, gk_t, rows, f"qk_post{g}")
        dqkv.append(d_)
        dgq, dgk = dgq + a_, dgk + b_
    per_head = lambda v: jnp.sum(v.reshape(ATT_H, ATT_HD), axis=0, keepdims=True)
    grads["q_norm_g"], grads["k_norm_g"] = per_head(dgq), per_head(dgk)

    if late is None:
        dxact, ddt, dbias, dalog, ddskip = ssd_bwd(xact, dtraw, dt_bias, a_log, d_skip, sin, dy, n_seq, seq)
    else:
        dxact, ddt, dbias, dalog, ddskip, *parts = ssd_bwd(
            xact, dtraw, dt_bias, a_log, d_skip, sin, dy, n_seq, seq,
            comm=direct_exchange([(grads[n] if n == "w_up" else _to_slabs(n, _narrow(n, grads[n])), "scatter")
                                  for n in LATE]))
        grads.update(zip(LATE, parts))
    grads["dt_bias"], grads["a_log"], grads["d_skip"] = dbias[:, :SSD_H], dalog[:, :SSD_H], ddskip[:, :SSD_H]

    def conv_silu_bwd(ctx, dxa, pre, xin, wv):
        return _conv_bwd(dxa[0] * _silu_grad(pre[0]), dxa[1] * _silu_grad(pre[1]), xin, wv, ctx, SSD_CONV)

    dxbc, grads["ssd_conv_w"], grads["ssd_conv_b"] = rw(
        conv_silu_bwd, "ssd_conv_bwd", CONV_DIM // cw,
        [(dxact, cw, 0, "next"), (xpre, cw, 0, "next"), (xbc, cw, 0, None)],
        [(conv_w, cw, 0)], [(CONV_DIM, cw, 0, MXU)], [(SSD_CONV, cw), (1, cw)])

    pieces = [(d_, d_, h, w_, tag) for d_, w_, tag in
              ((dz, w_z, "z"), (ddt, w_dt, "dt"), (dgs, w_gs, "gs"), (dga, w_ga, "ga"))]
    pieces += [(by_token(dqkv[g], g), dqkv[g], h_res[g], w_qkv[g], f"qkv{g}") for g in range(ATT_GROUPS)]
    pieces += [(dxbc, dxbc, h, w_xbc, "xbc")]
    dws = {tag: matmul(h_in, dpart_h, f"mm_dw_{tag}", ta=True, out_dtype=MXU) for _, dpart_h, h_in, _, tag in pieces}
    dw_in = [(splits[0], dws["z"]), (splits[1], dws["xbc"]), (splits[2], dws["dt"][:, :SSD_H])]
    dw_in += [(head_group(t, g)[0], dws[f"qkv{g}"][:, t * ATT_OUT:(t + 1) * ATT_OUT])
              for t in range(3) for g in range(ATT_GROUPS)]
    dw_in += [(splits[6], dws["gs"]), (splits[7], dws["ga"])]
    if w_in.ndim == 3:
        grads["w_in"] = _column_shards(dw_in, w_in.shape[2])
    else:
        grads["w_in"] = jnp.concatenate([p for _, p in dw_in], axis=1)
    dh = None
    for idx, (dpart, _, _, wpart, tag) in enumerate(pieces):
        comm, tail = NO_EXCHANGE, None
        if late is not None and idx == 0:
            slabs = [grads[n] if n == "w_in" else _to_slabs(n, _narrow(n, grads[n])) for n in EARLY]
            comm = sibling_exchange(slabs)
        if idx == len(pieces) - 1:
            tail = _Tail(rms_bwd_fn, like=[x, dx1], rows=[g1], outs=[F32], n_sums=1)
            if late is not None:
                comm = chip_exchange(summed)
        dh = matmul(dpart, wpart, f"mm_dh_{tag}", tb=True, add=dh, comm=comm, tail=tail)
        if late is not None and idx == 0:
            dh, *arrived = dh
            core = lax.axis_index("c")
            own = [lax.dynamic_index_in_dim(s.reshape((N_CHIPS, 2) + s.shape[1:]), core, axis=1, keepdims=False)
                   for s in slabs]
            summed = [pair_add(a_, b_, f"rs_add_{n}") for n, a_, b_ in zip(EARLY, own, arrived)]
    grad_x, grads["norm1_g"], *arrived = dh
    grads.update(zip(EARLY, arrived))
    return sq, grad_x.reshape(n_seq, seq, D_MODEL), grads


EARLY = ("w_in", "ssd_conv_w")
REPLICATED = ("norm1_g", "ssd_conv_b", "dt_bias", "a_log", "d_skip", "ssd_norm_g", "q_norm_g", "k_norm_g",
              "norm2_g", "ffn_conv_b")
WEIGHTS = ("norm1_g", "w_in", "ssd_conv_w", "ssd_conv_b", "dt_bias", "a_log", "d_skip", "ssd_norm_g", "w_ssd_proj",
           "q_norm_g", "k_norm_g", "w_attn_proj", "w_out", "norm2_g", "w_up", "ffn_conv_w", "ffn_conv_b", "w_down")
PACK_ROWS, PACK_COLS = 8, 2048


def _pack(vals):
    flat = jnp.concatenate([vals[n].reshape(-1) for n in REPLICATED])
    return jnp.pad(flat, (0, PACK_ROWS * PACK_COLS - flat.shape[0])).reshape(PACK_ROWS, PACK_COLS)


def _unpack(packed, like):
    flat = packed.reshape(-1)
    out, pos = {}, 0
    for n in REPLICATED:
        size = like[n].size
        out[n] = flat[pos:pos + size].reshape(like[n].shape)
        pos += size
    return out


def step(x, target, w_raw, m_raw, v_raw):
    wsh = {n: a[0] if a.ndim == 3 else a for n, a in w_raw.items()}
    gathered = all_gather([_narrow(n, wsh[n]) for n in EARLY], "ag_weights")
    full = {n: wsh[n] for n in REPLICATED}
    full.update({n: g if n == "w_in" else _from_gathered(n, g) for n, g in zip(EARLY, gathered)})

    sq, grad_x, grads = local_step(x, target, full, late={n: _narrow(n, wsh[n]) for n in LATE})

    (small,) = all_gather([_pack({n: grads[n] for n in REPLICATED})], "ag_small")

    out_g, out_d, out_m, out_v = {}, {}, {}, {}
    for n in EARLY + LATE:
        out_g[n], out_d[n], out_m[n], out_v[n] = adamw(grads[n], w_raw[n], m_raw[n], v_raw[n], f"adamw_{n}")
    pk = adamw(small, _pack(w_raw), _pack(m_raw), _pack(v_raw), "adamw_small")
    for dst, packed in zip((out_g, out_d, out_m, out_v), pk):
        dst.update(_unpack(packed, w_raw))
    loss = lax.psum(0.5 * jnp.sum(sq) / D_MODEL, ("x", "y", "c"))
    return loss, grad_x, out_g, out_d, out_m, out_v


def kernel(x, norm1_g, w_in, ssd_conv_w, ssd_conv_b, dt_bias, a_log, d_skip, ssd_norm_g, w_ssd_proj, q_norm_g, k_norm_g, w_attn_proj, w_out, norm2_g, w_up, ffn_conv_w, ffn_conv_b, w_down, loss_target, m_norm1_g, m_w_in, m_ssd_conv_w, m_ssd_conv_b, m_dt_bias, m_a_log, m_d_skip, m_ssd_norm_g, m_w_ssd_proj, m_q_norm_g, m_k_norm_g, m_w_attn_proj, m_w_out, m_norm2_g, m_w_up, m_ffn_conv_w, m_ffn_conv_b, m_w_down, v_norm1_g, v_w_in, v_ssd_conv_w, v_ssd_conv_b, v_dt_bias, v_a_log, v_d_skip, v_ssd_norm_g, v_w_ssd_proj, v_q_norm_g, v_k_norm_g, v_w_attn_proj, v_w_out, v_norm2_g, v_w_up, v_ffn_conv_w, v_ffn_conv_b, v_w_down):
    ws = (norm1_g, w_in, ssd_conv_w, ssd_conv_b, dt_bias, a_log, d_skip, ssd_norm_g, w_ssd_proj, q_norm_g, k_norm_g,
          w_attn_proj, w_out, norm2_g, w_up, ffn_conv_w, ffn_conv_b, w_down)
    ms = (m_norm1_g, m_w_in, m_ssd_conv_w, m_ssd_conv_b, m_dt_bias, m_a_log, m_d_skip, m_ssd_norm_g, m_w_ssd_proj,
          m_q_norm_g, m_k_norm_g, m_w_attn_proj, m_w_out, m_norm2_g, m_w_up, m_ffn_conv_w, m_ffn_conv_b, m_w_down)
    vs = (v_norm1_g, v_w_in, v_ssd_conv_w, v_ssd_conv_b, v_dt_bias, v_a_log, v_d_skip, v_ssd_norm_g, v_w_ssd_proj,
          v_q_norm_g, v_k_norm_g, v_w_attn_proj, v_w_out, v_norm2_g, v_w_up, v_ffn_conv_w, v_ffn_conv_b, v_w_down)
    loss, grad_x, g, d, m, v = step(x, loss_target, dict(zip(WEIGHTS, ws)), dict(zip(WEIGHTS, ms)), dict(zip(WEIGHTS, vs)))
    ordered = lambda dct: [dct[n] for n in WEIGHTS]
    return (loss, grad_x, *ordered(g), *ordered(d), *ordered(m), *ordered(v))
```

```python
import functools

import jax
import jax.numpy as jnp
from jax import lax
from jax.experimental import pallas as pl
from jax.experimental.pallas import tpu as pltpu

F32 = jnp.float32
BF16 = jnp.bfloat16
MXU = jnp.bfloat16
HIGHEST = lax.Precision.HIGHEST
VMEM_LIMIT_BYTES = 48 * 1024 * 1024
SUBLANES = 8
LANES = 128
N_DEV = 8

D_MODEL = 1024
D_INNER = 2048
SSD_P = 64
SSD_H = 32
SSD_G = 8
SSD_K = SSD_H // SSD_G
SSD_N = 128
SSD_Q = 128
SSD_CONV = 4
CONV_DIM = D_INNER + 2 * SSD_G * SSD_N
NORM_GROUP = D_INNER // SSD_G
ATT_GROUPS = 3
ATT_H = 8
ATT_HD = 64
ATT_BLK = 128
ATT_OUT = ATT_H * ATT_HD
ATT_DILATIONS = (1, 4, 16)
ATT_SCALE = ATT_HD ** -0.5
D_FF = 2816
FFN_CONV = 3
EPS = 1e-6
NEG = -1e30
IN_WIDTHS = (D_INNER, CONV_DIM, SSD_H, 3 * ATT_OUT, 3 * ATT_OUT, 3 * ATT_OUT, D_MODEL, D_MODEL)

ADAM_LR = 0.001
ADAM_B1 = 0.9
ADAM_B2 = 0.999
ADAM_EPS = 1e-08
ADAM_WD = 0.01
ADAM_STEP = 10


def _mm(a, b, dims):
    return lax.dot_general(a.astype(MXU), b.astype(MXU), (dims, ((), ())), preferred_element_type=F32)


def _dot_nn(a, b):
    return _mm(a, b, ((1,), (0,)))


def _dot_nt(a, b):
    return _mm(a, b, ((1,), (1,)))


def _dot_tn(a, b):
    return _mm(a, b, ((0,), (0,)))


def _dot_f32(a, b):
    return lax.dot_general(a, b, (((1,), (0,)), ((), ())), precision=HIGHEST, preferred_element_type=F32)


def _sigmoid(x):
    return 1.0 / (1.0 + jnp.exp(-x))


def _silu(x):
    return x * _sigmoid(x)


def _silu_grad(x):
    s = _sigmoid(x)
    return s * (1.0 + x * (1.0 - s))


def _softplus(x):
    return jnp.maximum(x, 0.0) + jnp.log(1.0 + jnp.exp(-jnp.abs(x)))


def _rms_fwd(x, g):
    r = lax.rsqrt(jnp.mean(x * x, axis=-1, keepdims=True) + EPS)
    return x * r * g


def _rms_bwd(x, g, dy):
    r = lax.rsqrt(jnp.mean(x * x, axis=-1, keepdims=True) + EPS)
    xh = x * r
    dyg = dy * g
    dx = r * (dyg - xh * jnp.mean(dyg * xh, axis=-1, keepdims=True))
    return dx, jnp.sum(dy * xh, axis=0, keepdims=True)


def _onehot_row(h, n=LANES):
    return (lax.broadcasted_iota(jnp.int32, (1, n), 1) == h).astype(F32)


def _onehot_col(h, n=LANES):
    return (lax.broadcasted_iota(jnp.int32, (n, 1), 0) == h).astype(F32)


def _head_expand_matrix():
    r = lax.broadcasted_iota(jnp.int32, (LANES, ATT_OUT), 0)
    c = lax.broadcasted_iota(jnp.int32, (LANES, ATT_OUT), 1)
    return (c // ATT_HD == r).astype(F32)


def _split_bf16(x, parts):
    out = []
    for _ in range(parts - 1):
        hi = x.astype(BF16).astype(F32)
        out.append(hi)
        x = x - hi
    out.append(x)
    return out


def _expand_heads(w):
    e = _head_expand_matrix()
    return sum(_dot_nn(p, e) for p in _split_bf16(w, 2))


def _reduce_heads(x):
    e = _head_expand_matrix()
    return sum(_dot_nt(p, e) for p in _split_bf16(x, 3))


def _shift_prev(cur, halo, s, first):
    if s == 0:
        return cur
    rolled = pltpu.roll(cur, s, 0)
    hr = jnp.where(first, 0.0, pltpu.roll(halo, s, 0))
    rows = lax.broadcasted_iota(jnp.int32, halo.shape, 0)
    head = jnp.where(rows < s, hr, rolled[:SUBLANES])
    if cur.shape[0] == SUBLANES:
        return head
    return jnp.concatenate([head, rolled[SUBLANES:]], axis=0)


def _shift_next(cur, halo, s, last):
    if s == 0:
        return cur
    tb = cur.shape[0]
    rolled = pltpu.roll(cur, tb - s, 0)
    hr = jnp.where(last, 0.0, pltpu.roll(halo, SUBLANES - s, 0))
    rows = lax.broadcasted_iota(jnp.int32, halo.shape, 0)
    tail = jnp.where(rows >= SUBLANES - s, hr, rolled[tb - SUBLANES:])
    return jnp.concatenate([rolled[:tb - SUBLANES], tail], axis=0)


def _conv_prev(x, halo, w, first, taps):
    acc = None
    for i in range(taps):
        term = w[i:i + 1, :] * _shift_prev(x, halo, taps - 1 - i, first)
        acc = term if acc is None else acc + term
    return acc


def _conv_bwd(dpre, dpre_next8, x, w, ctx, taps):
    dx, dws = None, []
    for i in range(taps):
        ahead = _shift_next(dpre, dpre_next8, taps - 1 - i, ctx.last)
        term = w[i:i + 1, :] * ahead
        dx = term if dx is None else dx + term
        dws.append(jnp.sum(ahead * x, axis=0, keepdims=True))
    return dx, jnp.concatenate(dws, axis=0), jnp.sum(dpre, axis=0, keepdims=True)


def _params(sem):
    return pltpu.CompilerParams(dimension_semantics=sem, vmem_limit_bytes=VMEM_LIMIT_BYTES)


N_CHIPS = N_DEV // 2
OTHER_CHIPS = (4, 2, 6)


class _Hosted:
    def __init__(self, arrays, out_shape, sems, ops):
        self.arrays, self.out_shape, self.sems, self.ops = list(arrays), list(out_shape), list(sems), ops
        self.n = len(self.arrays)
        self.specs = [pl.BlockSpec(memory_space=pl.ANY)] * self.n

    def begin(self, in_refs, out_refs, sem_refs, first):
        start, finish = self.ops(in_refs, out_refs, *sem_refs)
        pl.when(first)(start)
        return finish


NO_EXCHANGE = _Hosted((), (), (), None)


def _peer(k):
    x, y, c = lax.axis_index("x"), lax.axis_index("y"), lax.axis_index("c")
    px = 1 - x if k & 4 else x
    py = 1 - y if k & 2 else y
    pc = 1 - c if k & 1 else c
    return (px, py, pc), 4 * px + 2 * py + pc, 2 * px + py


def _remote(src, dst, send_sems, recv_sems, t, k, dev):
    return pltpu.make_async_remote_copy(src_ref=src, dst_ref=dst, send_sem=send_sems.at[t, k], recv_sem=recv_sems.at[t, k],
                                        device_id=dev, device_id_type=pl.DeviceIdType.MESH)


def direct_exchange(items):
    n = len(items)

    def ops(in_refs, out_refs, send_sems, recv_sems, local_sems):
        _, me, _ = _peer(0)
        part = lambda t, pid: in_refs[t] if items[t][1] == "gather" else in_refs[t].at[pid]

        def copy(t, k, arriving):
            dev, pid, _ = _peer(k)
            return _remote(part(t, pid), out_refs[t].at[pid if arriving else me], send_sems, recv_sems, t, k, dev)

        def own(t):
            return pltpu.make_async_copy(part(t, me), out_refs[t].at[me], local_sems.at[t])

        def start():
            for t in range(n):
                own(t).start()
                for k in range(1, N_DEV):
                    copy(t, k, False).start()

        def finish():
            for t in range(n):
                for k in range(1, N_DEV):
                    copy(t, k, True).wait_recv()
            for t in range(n):
                for k in range(1, N_DEV):
                    copy(t, k, False).wait_send()
                own(t).wait()

        return start, finish

    out_shape = [jax.ShapeDtypeStruct((N_DEV,) + tuple(a.shape if m == "gather" else a.shape[1:]), a.dtype)
                 for a, m in items]
    sems = [pltpu.SemaphoreType.DMA((n, N_DEV)), pltpu.SemaphoreType.DMA((n, N_DEV)), pltpu.SemaphoreType.DMA((n,))]
    return _Hosted([a for a, _ in items], out_shape, sems, ops)


def sibling_exchange(arrays):
    n = len(arrays)

    def ops(in_refs, out_refs, send_sems, recv_sems):
        sib, _, _ = _peer(1)
        c = lax.axis_index("c")
        copy = lambda t, q: _remote(in_refs[t].at[2 * q + (1 - c)], out_refs[t].at[q], send_sems, recv_sems, t, q, sib)

        def start():
            for t in range(n):
                for q in range(N_CHIPS):
                    copy(t, q).start()

        def finish():
            for t in range(n):
                for q in range(N_CHIPS):
                    copy(t, q).wait_recv()
            for t in range(n):
                for q in range(N_CHIPS):
                    copy(t, q).wait_send()

        return start, finish

    out_shape = [jax.ShapeDtypeStruct((N_CHIPS,) + a.shape[1:], a.dtype) for a in arrays]
    sems = [pltpu.SemaphoreType.DMA((n, N_CHIPS)), pltpu.SemaphoreType.DMA((n, N_CHIPS))]
    return _Hosted(arrays, out_shape, sems, ops)


def chip_exchange(arrays):
    n = len(arrays)

    def ops(in_refs, out_refs, send_sems, recv_sems, local_sems):
        _, _, mine = _peer(0)

        def copy(t, k, arriving):
            dev, _, q = _peer(k)
            return _remote(in_refs[t].at[q], out_refs[t].at[q if arriving else mine], send_sems, recv_sems, t, k, dev)

        def own(t):
            return pltpu.make_async_copy(in_refs[t].at[mine], out_refs[t].at[mine], local_sems.at[t])

        def start():
            for t in range(n):
                own(t).start()
                for k in OTHER_CHIPS:
                    copy(t, k, False).start()

        def finish():
            for t in range(n):
                for k in OTHER_CHIPS:
                    copy(t, k, True).wait_recv()
            for t in range(n):
                for k in OTHER_CHIPS:
                    copy(t, k, False).wait_send()
                own(t).wait()

        return start, finish

    out_shape = [jax.ShapeDtypeStruct(a.shape, a.dtype) for a in arrays]
    sems = [pltpu.SemaphoreType.DMA((n, N_DEV)), pltpu.SemaphoreType.DMA((n, N_DEV)), pltpu.SemaphoreType.DMA((n,))]
    return _Hosted(arrays, out_shape, sems, ops)


MATMUL_VMEM_BUDGET = 34 * 1024 * 1024


V7X_MXU_FLOPS = 996e12
V7X_HBM_BYTES_PER_S = 3.4e12
GRID_STEP_S = 0.35e-6


def _tile_sizes(dim, cap):
    return [t for t in range(LANES, min(dim, cap) + 1, LANES) if dim % t == 0] or [dim]


def _matmul_tiles(m, n, k, a_bytes, b_bytes, add_bytes, out_bytes, whole_rows=False, whole_k=False):
    best = None
    for tk in ([k] if whole_k else _tile_sizes(k, 8192)):
        nk = k // tk
        for tn in ([n] if whole_rows else _tile_sizes(n, 2048)):
            for tm in _tile_sizes(m, 2048):
                io = tm * tk * a_bytes + tk * tn * b_bytes
                ends = tm * tn * (add_bytes + out_bytes)
                need = 2 * (io + ends) + tm * tn * 4 * (2 if nk > 1 else 1)
                if need > MATMUL_VMEM_BUDGET:
                    continue
                step = max(2.0 * tm * tn * tk / V7X_MXU_FLOPS, (io + ends / nk) / V7X_HBM_BYTES_PER_S)
                if nk > 1:
                    step += tm * tn * 8 / V7X_HBM_BYTES_PER_S
                cost = (m // tm) * (n // tn) * nk * (step + GRID_STEP_S)
                if best is None or cost < best[0]:
                    best = (cost, tm, tn, tk)
    if best is None:
        raise ValueError((m, n, k))
    return best[1:]


class _Tail:
    def __init__(self, fn, like=(), rows=(), outs=(), n_sums=0):
        self.fn, self.like, self.rows, self.outs, self.n_sums = fn, list(like), list(rows), list(outs), n_sums


class _Head:
    def __init__(self, fn, like=(), rows=(), out=None):
        self.fn, self.like, self.rows, self.out = fn, list(like), list(rows), out


def matmul(a, b, name, ta=False, tb=False, add=None, out_dtype=F32, comm=NO_EXCHANGE, tail=None, head=None):
    assert not (ta and tb) and not (ta and head)
    m, k = (a.shape[1], a.shape[0]) if ta else a.shape
    n = b.shape[0] if tb else b.shape[1]
    assert (b.shape[1] if tb else b.shape[0]) == k
    like = ([] if add is None else [add]) + (tail.like if tail else [])
    rows = tail.rows if tail else []
    outs = tail.outs if tail else [out_dtype]
    n_sums = tail.n_sums if tail else 0
    h_like, h_rows = (head.like, head.rows) if head else ([], [])
    a_bytes = a.dtype.itemsize + sum(x.dtype.itemsize for x in h_like) + (jnp.dtype(head.out).itemsize if head else 0)
    tm, tn, tk = _matmul_tiles(m, n, k, a_bytes, b.dtype.itemsize, sum(x.dtype.itemsize for x in like),
                               sum(jnp.dtype(d).itemsize for d in outs),
                               whole_rows=tail is not None or head is not None, whole_k=head is not None)
    nk = k // tk
    grid = (m // tm, n // tn, nk)
    dims = ((0,), (0,)) if ta else (((1,), (1,)) if tb else ((1,), (0,)))
    n_plain = 2 + len(like) + len(rows)
    n_in = n_plain + len(h_like) + len(h_rows)
    n_out = len(outs) + n_sums + (1 if head else 0)
    n_acc = 0 if nk == 1 else 1

    def body(*refs):
        a_ref, b_ref = refs[:2]
        like_refs, row_refs = refs[2:2 + len(like)], refs[2 + len(like):n_plain]
        out_refs = refs[n_in + comm.n:n_in + comm.n + n_out]
        a_val = a_ref[...]
        if head:
            a_val = head.fn(a_val, *[x[...] for x in refs[n_plain:n_in]]).astype(head.out)
            out_refs[n_out - 1][...] = a_val
        ids = [pl.program_id(d) for d in range(3)]
        if comm.n:
            first = functools.reduce(jnp.logical_and, [i == 0 for i in ids])
            last = functools.reduce(jnp.logical_and, [i == g - 1 for i, g in zip(ids, grid)])
            done = comm.begin(refs[n_in:n_in + comm.n], refs[n_in + comm.n + n_out:n_in + 2 * comm.n + n_out],
                              refs[n_in + 2 * comm.n + n_out + n_acc:], first)

        def finish(r):
            if add is not None:
                r = r + like_refs[0][...].astype(F32)
            if tail is None:
                out_refs[0][...] = r.astype(out_dtype)
                return
            vals = tail.fn(r, *[x[...] for x in like_refs[len(like) - len(tail.like):]], *[x[...] for x in row_refs])
            for ref, val in zip(out_refs[:len(outs)], vals):
                ref[...] = val.astype(ref.dtype)
            for ref, val in zip(out_refs[len(outs):], vals[len(outs):]):
                @pl.when(ids[0] == 0)
                def _(ref=ref, val=val):
                    ref[...] = val

                @pl.when(ids[0] != 0)
                def _(ref=ref, val=val):
                    ref[...] += val

        if nk == 1:
            finish(_mm(a_val, b_ref[...], dims))
        else:
            acc = refs[n_in + 2 * comm.n + n_out]

            @pl.when(ids[2] == 0)
            def _():
                acc[...] = jnp.zeros_like(acc)

            acc[...] += _mm(a_val, b_ref[...], dims)

            @pl.when(ids[2] == nk - 1)
            def _():
                finish(acc[...])

        if comm.n:
            pl.when(last)(done)

    a_spec = pl.BlockSpec((tk, tm), lambda i, j, kk: (kk, i)) if ta else pl.BlockSpec((tm, tk), lambda i, j, kk: (i, kk))
    b_spec = pl.BlockSpec((tn, tk), lambda i, j, kk: (j, kk)) if tb else pl.BlockSpec((tk, tn), lambda i, j, kk: (kk, j))
    tile = pl.BlockSpec((tm, tn), lambda i, j, kk: (i, j))
    row = pl.BlockSpec((1, tn), lambda i, j, kk: (0, j))
    sequential = comm.n or n_sums
    res = pl.pallas_call(
        body, name=name,
        grid=grid,
        in_specs=[a_spec, b_spec] + [tile] * len(like) + [row] * len(rows) + [a_spec] * len(h_like)
        + [pl.BlockSpec((1, tk), lambda i, j, kk: (0, kk))] * len(h_rows) + comm.specs,
        out_specs=[tile] * len(outs) + [row] * n_sums + ([a_spec] if head else []) + comm.specs,
        out_shape=[jax.ShapeDtypeStruct((m, n), d) for d in outs] + [jax.ShapeDtypeStruct((1, n), F32)] * n_sums
        + ([jax.ShapeDtypeStruct((m, k), head.out)] if head else []) + comm.out_shape,
        scratch_shapes=([] if nk == 1 else [pltpu.VMEM((tm, tn), F32)]) + comm.sems,
        compiler_params=_params(("arbitrary",) * 3 if sequential else ("parallel", "parallel", "arbitrary")),
    )(a, b, *like, *rows, *h_like, *h_rows, *comm.arrays)
    return res if (comm.n or tail or head) else res[0]


class _Ctx:
    def __init__(self, first, last):
        self.first = first
        self.last = last


def rowwise(fn, name, rows, seq, tb, ncol, ins, params=(), outs=(), accs=()):
    assert rows % tb == 0 and seq % tb == 0 and tb % 16 == 0
    bps = seq // tb
    nrow = rows // tb
    r8 = tb // SUBLANES
    args, in_specs = [], []
    for arr, w, off, halo in ins:
        args.append(arr)
        in_specs.append(pl.BlockSpec((tb, w), lambda j, i, off=off: (i, off + j)))
        if halo in ("prev", "both"):
            args.append(arr)
            in_specs.append(pl.BlockSpec((SUBLANES, w), lambda j, i, off=off: (jnp.maximum(i * r8 - 1, 0), off + j)))
        if halo in ("next", "both"):
            args.append(arr)
            in_specs.append(pl.BlockSpec(
                (SUBLANES, w), lambda j, i, off=off: (jnp.minimum((i + 1) * r8, rows // SUBLANES - 1), off + j)))
    for arr, w, off in params:
        args.append(arr)
        if w is None:
            in_specs.append(pl.BlockSpec(arr.shape, lambda j, i: (0, 0)))
        else:
            in_specs.append(pl.BlockSpec((arr.shape[0], w), lambda j, i, off=off: (0, off + j)))
    out_shape, out_specs = [], []
    for total, w, off, dt in outs:
        out_shape.append(jax.ShapeDtypeStruct((rows, total), dt))
        out_specs.append(pl.BlockSpec((tb, w), lambda j, i, off=off: (i, off + j)))
    for r, w in accs:
        out_shape.append(jax.ShapeDtypeStruct((r, ncol * w), F32))
        out_specs.append(pl.BlockSpec((r, w), lambda j, i: (0, j)))
    n_out, n_acc = len(outs), len(accs)

    def body(*refs):
        i = pl.program_id(1)
        pos = 0
        vals = []
        for _, _, _, halo in ins:
            cur = refs[pos][...]
            pos += 1
            if halo is None:
                vals.append(cur)
            elif halo == "both":
                vals.append((cur, refs[pos][...], refs[pos + 1][...]))
                pos += 2
            else:
                vals.append((cur, refs[pos][...]))
                pos += 1
        for _ in params:
            vals.append(refs[pos][...])
            pos += 1
        ctx = _Ctx(i % bps == 0, i % bps == bps - 1)
        res = fn(ctx, *vals)
        if not isinstance(res, (tuple, list)):
            res = (res,)
        assert len(res) == n_out + n_acc
        for q in range(n_out):
            refs[pos + q][...] = res[q].astype(refs[pos + q].dtype)
        for q in range(n_acc):
            ref, val = refs[pos + n_out + q], res[n_out + q]

            @pl.when(i == 0)
            def _(ref=ref, val=val):
                ref[...] = val

            @pl.when(i != 0)
            def _(ref=ref, val=val):
                ref[...] += val

    res = pl.pallas_call(
        body, name=name,
        grid=(ncol, nrow),
        in_specs=in_specs,
        out_specs=out_specs,
        out_shape=out_shape,
        compiler_params=_params(("parallel", "arbitrary")),
    )(*args)
    return res


GROUP_W = SSD_K * SSD_P


def _tri(lower):
    r = lax.broadcasted_iota(jnp.int32, (SSD_Q, SSD_Q), 0)
    c = lax.broadcasted_iota(jnp.int32, (SSD_Q, SSD_Q), 1)
    return r >= c if lower else r <= c


def _first_head_lanes():
    return lax.broadcasted_iota(jnp.int32, (1, LANES), 1) < SSD_P


def _column(v, j):
    return v[:, j * LANES:(j + 1) * LANES] if v.shape[-1] == GROUP_W else v


def _per_head(vals):
    first = _first_head_lanes()
    return jnp.concatenate([jnp.where(first, _column(vals[2 * j], j), _column(vals[2 * j + 1], j))
                            for j in range(GROUP_W // LANES)], axis=1)


def _per_head_rows(vals):
    return jnp.concatenate([jnp.broadcast_to(v, (SSD_P, 1)) for v in vals], axis=0)


def _own_columns(slab, k):
    keep = _first_head_lanes() if k % 2 == 0 else jnp.logical_not(_first_head_lanes())
    own = jnp.where(keep, _column(slab, k // 2), 0.0)
    return jnp.concatenate([own, jnp.zeros_like(own)] if k < 2 else [jnp.zeros_like(own), own], axis=1)


def _headsum(prod, g):
    out = None
    for k in range(SSD_K):
        keep = _first_head_lanes() if k % 2 == 0 else jnp.logical_not(_first_head_lanes())
        term = jnp.sum(jnp.where(keep, _column(prod, k // 2), 0.0), axis=1, keepdims=True) * _onehot_row(g * SSD_K + k)
        out = term if out is None else out + term
    return out


def ssd_fwd(xact, dtraw, dt_bias, a_log, d_skip, n_seq, seq, comm=NO_EXCHANGE):
    nc = seq // SSD_Q
    rows = n_seq * seq
    nx = comm.n

    def body(*refs):
        xact_ref, dtraw_ref, bias_ref, alog_ref, dskip_ref = refs[:5]
        y_ref, sin_ref = refs[5 + nx:7 + nx]
        state, cs_s, cst_s, dt_s = refs[7 + 2 * nx:11 + 2 * nx]
        b, c = pl.program_id(0), pl.program_id(1)
        if nx:
            finish = comm.begin(refs[5:5 + nx], refs[7 + nx:7 + 2 * nx], refs[11 + 2 * nx:],
                                jnp.logical_and(b == 0, c == 0))

        @pl.when(c == 0)
        def _():
            state[...] = jnp.zeros_like(state)

        sin_ref[0] = state[...]
        dt = _softplus(dtraw_ref[...] + bias_ref[...])
        a = dt * (-jnp.exp(alog_ref[...]))
        cs = _dot_f32(_tri(True).astype(F32), a)
        cs_s[...] = cs
        cst_s[...] = cs.T
        dt_s[...] = dt
        causal = _tri(True)
        def front(g):
            heads = [g * SSD_K + k for k in range(SSD_K)]
            bg = xact_ref[:, pl.ds(D_INNER + g * SSD_N, SSD_N)]
            cg = xact_ref[:, pl.ds(D_INNER + (SSD_G + g) * SSD_N, SSD_N)]
            xg = xact_ref[:, pl.ds(g * GROUP_W, GROUP_W)]
            cols = [cs_s[:, pl.ds(h, 1)] for h in heads]
            lasts = [cs_s[pl.ds(SSD_Q - 1, 1), pl.ds(h, 1)] for h in heads]
            xdg = xg * _per_head([dt_s[:, pl.ds(h, 1)] for h in heads])
            sg = state[g]
            y = (_per_head([jnp.exp(c_) for c_ in cols]) * _dot_nt(cg, sg)
                 + _per_head([dskip_ref[:, pl.ds(h, 1)] for h in heads]) * xg)
            w = _per_head([jnp.exp(l_ - c_) for l_, c_ in zip(lasts, cols)])
            state[g] = _per_head_rows([jnp.exp(l_) for l_ in lasts]) * sg + _dot_tn(w * xdg, bg)
            return heads, cols, _dot_nt(cg, bg), xdg, y

        def back(g, heads, cols, gm, xdg, y):
            mats = [gm * jnp.exp(jnp.where(causal, cols[k] - cst_s[pl.ds(h, 1), :], NEG)) for k, h in enumerate(heads)]
            y4 = _dot_nn(jnp.concatenate(mats, axis=0), xdg)
            y_ref[:, pl.ds(g * GROUP_W, GROUP_W)] = y + _per_head([y4[k * SSD_Q:(k + 1) * SSD_Q] for k in range(SSD_K)])

        ahead = front(0)
        for g in range(SSD_G):
            cur, ahead = ahead, (front(g + 1) if g + 1 < SSD_G else None)
            back(g, *cur)
        if nx:
            pl.when(jnp.logical_and(b == n_seq - 1, c == nc - 1))(finish)

    vec = pl.BlockSpec((1, LANES), lambda b, c: (0, 0))
    return pl.pallas_call(
        body, name="ssd_fwd",
        grid=(n_seq, nc),
        in_specs=[pl.BlockSpec((SSD_Q, CONV_DIM), lambda b, c: (b * nc + c, 0)),
                  pl.BlockSpec((SSD_Q, LANES), lambda b, c: (b * nc + c, 0)), vec, vec, vec] + comm.specs,
        out_specs=[pl.BlockSpec((SSD_Q, D_INNER), lambda b, c: (b * nc + c, 0)),
                   pl.BlockSpec((1, SSD_G, GROUP_W, SSD_N), lambda b, c: (b * nc + c, 0, 0, 0))] + comm.specs,
        out_shape=[jax.ShapeDtypeStruct((rows, D_INNER), F32),
                   jax.ShapeDtypeStruct((n_seq * nc, SSD_G, GROUP_W, SSD_N), F32)] + comm.out_shape,
        scratch_shapes=[pltpu.VMEM((SSD_G, GROUP_W, SSD_N), F32), pltpu.VMEM((SSD_Q, LANES), F32),
                        pltpu.VMEM((LANES, SSD_Q), F32), pltpu.VMEM((SSD_Q, LANES), F32)] + comm.sems,
        compiler_params=_params(("arbitrary", "arbitrary")),
    )(xact, dtraw, dt_bias, a_log, d_skip, *comm.arrays)


def ssd_bwd(xact, dtraw, dt_bias, a_log, d_skip, sin, dy, n_seq, seq, comm=NO_EXCHANGE):
    nc = seq // SSD_Q
    rows = n_seq * seq
    nx = comm.n

    def body(*refs):
        xact_ref, dtraw_ref, bias_ref, alog_ref, dskip_ref, sin_ref, dy_ref = refs[:7]
        dx_ref, ddt_ref, dbias_ref, dalog_ref, ddskip_ref = refs[7 + nx:12 + nx]
        dstate, cs_s, cst_s, dt_s = refs[12 + 2 * nx:16 + 2 * nx]
        b, c = pl.program_id(0), pl.program_id(1)
        if nx:
            finish = comm.begin(refs[7:7 + nx], refs[12 + nx:12 + 2 * nx], refs[16 + 2 * nx:],
                                jnp.logical_and(b == 0, c == 0))

        @pl.when(c == 0)
        def _():
            dstate[...] = jnp.zeros_like(dstate)

        pre = dtraw_ref[...] + bias_ref[...]
        dt = _softplus(pre)
        a_neg = -jnp.exp(alog_ref[...])
        cs = _dot_f32(_tri(True).astype(F32), dt * a_neg)
        cs_s[...] = cs
        cst_s[...] = cs.T
        dt_s[...] = dt
        causal, anti = _tri(True), _tri(False)
        is_last_row = lax.broadcasted_iota(jnp.int32, (SSD_Q, 1), 0) == SSD_Q - 1
        dcs_cf = jnp.zeros((SSD_Q, LANES), F32)
        dcs_rf = jnp.zeros((LANES, SSD_Q), F32)
        ddt_cf = jnp.zeros((SSD_Q, LANES), F32)
        dd_vec = jnp.zeros((1, LANES), F32)
        dlast_vec = jnp.zeros((1, LANES), F32)
        def front(g):
            heads = [g * SSD_K + k for k in range(SSD_K)]
            v = {"heads": heads}
            bg = v["bg"] = xact_ref[:, pl.ds(D_INNER + g * SSD_N, SSD_N)]
            cg = v["cg"] = xact_ref[:, pl.ds(D_INNER + (SSD_G + g) * SSD_N, SSD_N)]
            xg = v["xg"] = xact_ref[:, pl.ds(g * GROUP_W, GROUP_W)]
            dyg = v["dyg"] = dy_ref[:, pl.ds(g * GROUP_W, GROUP_W)]
            cols = [cs_s[:, pl.ds(h, 1)] for h in heads]
            rws = [cst_s[pl.ds(h, 1), :] for h in heads]
            lasts = [cs_s[pl.ds(SSD_Q - 1, 1), pl.ds(h, 1)] for h in heads]
            e_lasts = v["e_lasts"] = [jnp.exp(l_) for l_ in lasts]
            v["dtg"] = _per_head([dt_s[:, pl.ds(h, 1)] for h in heads])
            v["dskg"] = _per_head([dskip_ref[:, pl.ds(h, 1)] for h in heads])
            e_col = _per_head([jnp.exp(c_) for c_ in cols])
            w = v["w"] = _per_head([jnp.exp(l_ - c_) for l_, c_ in zip(lasts, cols)])
            xdg = v["xdg"] = xg * v["dtg"]
            sg = sin_ref[0, g]
            dsn = dstate[g]
            v["gm"] = _dot_nt(cg, bg)
            gmt = _dot_nt(bg, cg)
            v["y_off"] = e_col * _dot_nt(cg, sg)
            d_cs = e_col * dyg
            v["dcg"] = _dot_nn(d_cs, sg)
            dstate[g] = _dot_tn(d_cs, cg) + _per_head_rows(e_lasts) * dsn
            v["dbg"] = _dot_nn(w * xdg, dsn)
            v["dtt"] = _dot_nt(bg, dsn)
            v["dsn_s"] = dsn * sg
            segs = [cols[k] - rws[k] for k in range(SSD_K)]
            v["decays"] = [jnp.exp(jnp.where(causal, s_, NEG)) for s_ in segs]
            v["dm4"] = _dot_nt(jnp.concatenate([_own_columns(dyg, k) for k in range(SSD_K)], axis=0), xdg)
            v["z4"] = _dot_nn(jnp.concatenate([gmt * jnp.exp(jnp.where(anti, -s_, NEG)) for s_ in segs], axis=0), dyg)
            return v

        def back(g, v, sums):
            dcs_cf, dcs_rf, ddt_cf, dd_vec, dlast_vec = sums
            dxd = v["w"] * v["dtt"] + _per_head([v["z4"][k * SSD_Q:(k + 1) * SSD_Q] for k in range(SSD_K)])
            dw = _headsum(v["dtt"] * v["xdg"] * v["w"], g)
            dcs_cf = dcs_cf + _headsum(v["dyg"] * v["y_off"], g) - dw
            dlast_vec = dlast_vec + jnp.sum(dw, axis=0, keepdims=True)
            dgm = jnp.zeros((SSD_Q, SSD_Q), F32)
            for k, h in enumerate(v["heads"]):
                dm = v["dm4"][k * SSD_Q:(k + 1) * SSD_Q]
                dseg = dm * v["gm"] * v["decays"][k]
                dgm = dgm + dm * v["decays"][k]
                oh_r = _onehot_row(h)
                dcs_cf = dcs_cf + jnp.sum(dseg, axis=1, keepdims=True) * oh_r
                dcs_rf = dcs_rf - _onehot_col(h) * jnp.sum(dseg, axis=0, keepdims=True)
                dlast_vec = dlast_vec + (jnp.sum(v["dsn_s"][k * SSD_P:(k + 1) * SSD_P], keepdims=True)
                                         * v["e_lasts"][k] * oh_r)
            dx_ref[:, pl.ds(g * GROUP_W, GROUP_W)] = dxd * v["dtg"] + v["dskg"] * v["dyg"]
            ddt_cf = ddt_cf + _headsum(dxd * v["xg"], g)
            dd_vec = dd_vec + _headsum(jnp.sum(v["dyg"] * v["xg"], axis=0, keepdims=True), g)
            dx_ref[:, pl.ds(D_INNER + g * SSD_N, SSD_N)] = v["dbg"] + _dot_tn(dgm, v["cg"])
            dx_ref[:, pl.ds(D_INNER + (SSD_G + g) * SSD_N, SSD_N)] = v["dcg"] + _dot_nn(dgm, v["bg"])
            return dcs_cf, dcs_rf, ddt_cf, dd_vec, dlast_vec

        sums = (dcs_cf, dcs_rf, ddt_cf, dd_vec, dlast_vec)
        ahead = front(0)
        for g in range(SSD_G):
            cur, ahead = ahead, (front(g + 1) if g + 1 < SSD_G else None)
            sums = back(g, cur, sums)
        dcs_cf, dcs_rf, ddt_cf, dd_vec, dlast_vec = sums
        dcs = dcs_cf + dcs_rf.T + jnp.where(is_last_row, dlast_vec, 0.0)
        da = _dot_f32(_tri(False).astype(F32), dcs)
        ddt = ddt_cf + da * a_neg
        ddtraw = ddt * _sigmoid(pre)
        ddt_ref[...] = ddtraw.astype(ddt_ref.dtype)
        dbias = jnp.sum(ddtraw, axis=0, keepdims=True)
        dalog = jnp.sum(da * dt, axis=0, keepdims=True) * a_neg
        first_step = jnp.logical_and(b == 0, c == 0)

        @pl.when(first_step)
        def _():
            dbias_ref[...] = dbias
            dalog_ref[...] = dalog
            ddskip_ref[...] = dd_vec

        @pl.when(jnp.logical_not(first_step))
        def _():
            dbias_ref[...] += dbias
            dalog_ref[...] += dalog
            ddskip_ref[...] += dd_vec

        if nx:
            pl.when(jnp.logical_and(b == n_seq - 1, c == nc - 1))(finish)

    def rowblk(b, c):
        return b * nc + (nc - 1 - c)

    vec = pl.BlockSpec((1, LANES), lambda b, c: (0, 0))
    return pl.pallas_call(
        body, name="ssd_bwd",
        grid=(n_seq, nc),
        in_specs=[pl.BlockSpec((SSD_Q, CONV_DIM), lambda b, c: (rowblk(b, c), 0)),
                  pl.BlockSpec((SSD_Q, LANES), lambda b, c: (rowblk(b, c), 0)), vec, vec, vec,
                  pl.BlockSpec((1, SSD_G, GROUP_W, SSD_N), lambda b, c: (rowblk(b, c), 0, 0, 0)),
                  pl.BlockSpec((SSD_Q, D_INNER), lambda b, c: (rowblk(b, c), 0))] + comm.specs,
        out_specs=[pl.BlockSpec((SSD_Q, CONV_DIM), lambda b, c: (rowblk(b, c), 0)),
                   pl.BlockSpec((SSD_Q, LANES), lambda b, c: (rowblk(b, c), 0)), vec, vec, vec] + comm.specs,
        out_shape=[jax.ShapeDtypeStruct((rows, CONV_DIM), F32), jax.ShapeDtypeStruct((rows, LANES), BF16),
                   jax.ShapeDtypeStruct((1, LANES), F32), jax.ShapeDtypeStruct((1, LANES), F32),
                   jax.ShapeDtypeStruct((1, LANES), F32)] + comm.out_shape,
        scratch_shapes=[pltpu.VMEM((SSD_G, GROUP_W, SSD_N), F32), pltpu.VMEM((SSD_Q, LANES), F32),
                        pltpu.VMEM((LANES, SSD_Q), F32), pltpu.VMEM((SSD_Q, LANES), F32)] + comm.sems,
        compiler_params=_params(("arbitrary", "arbitrary")),
    )(xact, dtraw, dt_bias, a_log, d_skip, sin, dy, *comm.arrays)


QKV_W = 3 * ATT_OUT
PAIR_W = 2 * ATT_HD
HEAD_PAIRS = ATT_H // 2
PREP_ROWS = 512


def _by_residue(a, n_seq, seq, dil):
    if dil == 1:
        return a
    return a.reshape(n_seq, seq // dil, dil, a.shape[1]).transpose(0, 2, 1, 3).reshape(a.shape)


def _by_token(a, n_seq, seq, dil):
    if dil == 1:
        return a
    return a.reshape(n_seq, dil, seq // dil, a.shape[1]).transpose(0, 2, 1, 3).reshape(a.shape)


def _head_sums(x, fn):
    lo = jnp.logical_not(lax.broadcasted_iota(jnp.int32, (1, 2 * ATT_HD), 1) >= ATT_HD)
    parts = []
    for p in range(ATT_H // 2):
        slab = x[:, p * 2 * ATT_HD:(p + 1) * 2 * ATT_HD]
        s_lo = fn(jnp.sum(jnp.where(lo, slab, 0.0), axis=1, keepdims=True))
        s_hi = fn(jnp.sum(jnp.where(lo, 0.0, slab), axis=1, keepdims=True))
        parts.append(jnp.where(lo, s_lo, s_hi))
    return jnp.concatenate(parts, axis=1)


def _head_rstd(x):
    return _head_sums(x * x, lambda s: lax.rsqrt(s * (1.0 / ATT_HD) + EPS))


def _head_rms_bwd(x, g_t, dy):
    r = _head_rstd(x)
    xh = x * r
    dyg = dy * g_t
    mean = _head_sums(dyg * xh, lambda s: s * (1.0 / ATT_HD))
    return r * (dyg - xh * mean), jnp.sum(dy * xh, axis=0, keepdims=True)


def _lane_hi():
    return lax.broadcasted_iota(jnp.int32, (1, PAIR_W), 1) >= ATT_HD


def _band_mask2(first_valid, query_rows):
    i = lax.broadcasted_iota(jnp.int32, (ATT_BLK, 2 * ATT_BLK), 0)
    j = lax.broadcasted_iota(jnp.int32, (ATT_BLK, 2 * ATT_BLK), 1)
    left = j < ATT_BLK
    right = jnp.logical_not(left)
    if query_rows:
        return jnp.logical_or(jnp.logical_and(jnp.logical_and(left, i <= j), first_valid),
                              jnp.logical_and(right, i >= j - ATT_BLK))
    return jnp.logical_or(jnp.logical_and(left, j >= i),
                          jnp.logical_and(jnp.logical_and(right, j - ATT_BLK <= i), first_valid))


def _only_head(slab, hi):
    keep = _lane_hi() if hi else jnp.logical_not(_lane_hi())
    return jnp.where(keep, slab, jnp.zeros_like(slab))


def attn_fwd(nq, n_seq, seq, dil, name):
    nb = seq // dil // ATT_BLK
    rows = n_seq * seq

    def body(cur_ref, prev_ref, o_ref, lse_ref, s_scr, p_scr):
        n = pl.program_id(1)
        mask = _band_mask2(n > 0, True)
        for h in range(ATT_H):
            sl = pl.ds((h // 2) * PAIR_W, PAIR_W)
            ks = pl.ds(ATT_OUT + (h // 2) * PAIR_W, PAIR_W)
            kcat = jnp.concatenate([prev_ref[:, ks], cur_ref[:, ks]], axis=0)
            s_scr[h] = jnp.where(mask, _dot_nt(_only_head(cur_ref[:, sl], h % 2), kcat), NEG)
        s_all = s_scr[...]
        mx = jnp.max(s_all, axis=2, keepdims=True)
        p_all = jnp.exp(s_all - mx)
        den = jnp.sum(p_all, axis=2, keepdims=True)
        p_scr[...] = p_all.astype(p_scr.dtype)
        inv = 1.0 / den
        lse = mx + jnp.log(den)
        lse_blk = jnp.zeros((ATT_BLK, LANES), F32)
        for h in range(ATT_H):
            lse_blk = lse_blk + lse[h] * _onehot_row(h)
        lse_ref[...] = lse_blk
        for pr in range(HEAD_PAIRS):
            vs = pl.ds(2 * ATT_OUT + pr * PAIR_W, PAIR_W)
            vcat = jnp.concatenate([prev_ref[:, vs], cur_ref[:, vs]], axis=0)
            lo = _dot_nn(p_scr[2 * pr], vcat) * inv[2 * pr]
            hi = _dot_nn(p_scr[2 * pr + 1], vcat) * inv[2 * pr + 1]
            o_ref[:, pl.ds(pr * PAIR_W, PAIR_W)] = jnp.where(_lane_hi(), hi, lo)

    def blk(width, shift):
        if shift:
            return pl.BlockSpec((ATT_BLK, width), lambda s, n: (s * nb + jnp.maximum(n - 1, 0), 0))
        return pl.BlockSpec((ATT_BLK, width), lambda s, n: (s * nb + n, 0))

    return pl.pallas_call(
        body, name=name,
        grid=(n_seq * dil, nb),
        in_specs=[blk(QKV_W, 0), blk(QKV_W, -1)],
        out_specs=[blk(ATT_OUT, 0), blk(LANES, 0)],
        out_shape=[jax.ShapeDtypeStruct((rows, ATT_OUT), F32), jax.ShapeDtypeStruct((rows, LANES), F32)],
        scratch_shapes=[pltpu.VMEM((ATT_H, ATT_BLK, 2 * ATT_BLK), F32), pltpu.VMEM((ATT_H, ATT_BLK, 2 * ATT_BLK), MXU)],
        compiler_params=_params(("parallel", "arbitrary")),
    )(nq, nq)


def attn_bwd(nq, do, lse, wts, rsum, n_seq, seq, dil, name):
    nb = seq // dil // ATT_BLK

    def body(prev_ref, cur_ref, nxt_ref, do_c, do_x, lse_c, lse_x, wt_c, wt_x, rs_c, rs_x, dn_ref):
        n = pl.program_id(1)
        mask_q = _band_mask2(n > 0, True)
        mask_k = _band_mask2(n < nb - 1, False)
        wc, wx = wt_c[...], wt_x[...]
        lse_t = jnp.concatenate([lse_c[...].T, lse_x[...].T], axis=1)
        dl_t = jnp.concatenate([(-wc * rs_c[...]).T, (-wx * rs_x[...]).T], axis=1)
        def operands(pr):
            sl = pl.ds(pr * PAIR_W, PAIR_W)
            ks = pl.ds(ATT_OUT + pr * PAIR_W, PAIR_W)
            vs = pl.ds(2 * ATT_OUT + pr * PAIR_W, PAIR_W)
            he, ho = pl.ds(2 * pr, 1), pl.ds(2 * pr + 1, 1)
            q_c, k_c, v_c = cur_ref[:, sl], cur_ref[:, ks], cur_ref[:, vs]
            dog_c = do_c[:, sl] * jnp.where(_lane_hi(), wt_c[:, ho], wt_c[:, he])
            dog_x = do_x[:, sl] * jnp.where(_lane_hi(), wt_x[:, ho], wt_x[:, he])
            return dict(q_c=q_c, k_c=k_c, v_c=v_c, qcat=jnp.concatenate([q_c, nxt_ref[:, sl]], axis=0),
                        kcat=jnp.concatenate([prev_ref[:, ks], k_c], axis=0),
                        vcat=jnp.concatenate([prev_ref[:, vs], v_c], axis=0),
                        dog=jnp.concatenate([dog_c, dog_x], axis=0).astype(MXU))

        def scores(o, h):
            hi, one = h % 2, pl.ds(h, 1)
            dl_col = -wt_c[:, one] * rs_c[:, one]
            p_q = jnp.exp(jnp.where(mask_q, _dot_nt(_only_head(o["q_c"], hi), o["kcat"]) - lse_c[:, one], NEG))
            ds_q = p_q * (_dot_nt(_only_head(o["dog"][:ATT_BLK], hi), o["vcat"]) + dl_col)
            p_t = jnp.exp(jnp.where(mask_k, _dot_nt(_only_head(o["k_c"], hi), o["qcat"]) - lse_t[h:h + 1, :], NEG))
            ds_t = p_t * (_dot_nt(_only_head(o["v_c"], hi), o["dog"]) + dl_t[h:h + 1, :])
            return ds_q, ds_t, p_t

        ops = [operands(pr) for pr in range(HEAD_PAIRS)]
        ahead = scores(ops[0], 0)
        res = []
        for h in range(ATT_H):
            o = ops[h // 2]
            (ds_q, ds_t, p_t), ahead = ahead, (scores(ops[(h + 1) // 2], h + 1) if h + 1 < ATT_H else None)
            res.append((_dot_nn(ds_q, o["kcat"]), _dot_nn(ds_t, o["qcat"]), _dot_nn(p_t, o["dog"])))
            if h % 2:
                for t, first in enumerate((0, ATT_OUT, 2 * ATT_OUT)):
                    dn_ref[:, pl.ds(first + (h // 2) * PAIR_W, PAIR_W)] = jnp.where(_lane_hi(), res[h][t], res[h - 1][t])

    def at(shift, width):
        if shift < 0:
            return pl.BlockSpec((ATT_BLK, width), lambda s, n: (s * nb + jnp.maximum(n - 1, 0), 0))
        if shift > 0:
            return pl.BlockSpec((ATT_BLK, width), lambda s, n: (s * nb + jnp.minimum(n + 1, nb - 1), 0))
        return pl.BlockSpec((ATT_BLK, width), lambda s, n: (s * nb + n, 0))

    return pl.pallas_call(
        body, name=name,
        grid=(n_seq * dil, nb),
        in_specs=[at(-1, QKV_W), at(0, QKV_W), at(1, QKV_W), at(0, ATT_OUT), at(1, ATT_OUT),
                  at(0, LANES), at(1, LANES), at(0, LANES), at(1, LANES), at(0, LANES), at(1, LANES)],
        out_specs=at(0, QKV_W),
        out_shape=jax.ShapeDtypeStruct((n_seq * seq, QKV_W), F32),
        compiler_params=_params(("parallel", "arbitrary")),
    )(nq, nq, nq, do, do, lse, lse, wts, wts, rsum, rsum)


def qk_post(qkv, dn, gq_t, gk_t, rows, name):
    tb = min(PREP_ROWS, rows)

    def body(x_ref, dn_ref, gq_ref, gk_ref, o_ref, dgq_ref, dgk_ref):
        i = pl.program_id(0)
        qs, ks, vs = pl.ds(0, ATT_OUT), pl.ds(ATT_OUT, ATT_OUT), pl.ds(2 * ATT_OUT, ATT_OUT)
        dq, dgq = _head_rms_bwd(x_ref[:, qs], gq_ref[...], dn_ref[:, qs] * ATT_SCALE)
        dk, dgk = _head_rms_bwd(x_ref[:, ks], gk_ref[...], dn_ref[:, ks])
        o_ref[:, qs] = dq.astype(o_ref.dtype)
        o_ref[:, ks] = dk.astype(o_ref.dtype)
        o_ref[:, vs] = dn_ref[:, vs].astype(o_ref.dtype)

        @pl.when(i == 0)
        def _():
            dgq_ref[...] = dgq
            dgk_ref[...] = dgk

        @pl.when(i != 0)
        def _():
            dgq_ref[...] += dgq
            dgk_ref[...] += dgk

    gspec = pl.BlockSpec((1, ATT_OUT), lambda i: (0, 0))
    blk = pl.BlockSpec((tb, QKV_W), lambda i: (i, 0))
    return pl.pallas_call(
        body, name=name,
        grid=(rows // tb,),
        in_specs=[blk, blk, gspec, gspec],
        out_specs=[blk, gspec, gspec],
        out_shape=[jax.ShapeDtypeStruct((rows, QKV_W), MXU), jax.ShapeDtypeStruct((1, ATT_OUT), F32),
                   jax.ShapeDtypeStruct((1, ATT_OUT), F32)],
        compiler_params=_params(("arbitrary",)),
    )(qkv, dn, gq_t, gk_t)


def all_gather(arrays, name):
    n = len(arrays)

    def body(*refs):
        in_refs, out_refs = refs[:n], refs[n:2 * n]
        send_sems, recv_sems, local_sems = refs[2 * n:]
        _, me, _ = _peer(0)
        sib, _, _ = _peer(1)

        def first(t, k, arriving):
            dev, pid, _ = _peer(k)
            return _remote(in_refs[t], out_refs[t].at[pid if arriving else me], send_sems, recv_sems, t, k, dev)

        def passed(t, k, arriving):
            slot = out_refs[t].at[_peer(k + 1 if arriving else k)[1]]
            return _remote(slot, slot, send_sems, recv_sems, t, k + 1, sib)

        def own(t):
            return pltpu.make_async_copy(in_refs[t], out_refs[t].at[me], local_sems.at[t])

        for t in range(n):
            own(t).start()
            for k in (1,) + OTHER_CHIPS:
                first(t, k, False).start()
        for t in range(n):
            for k in OTHER_CHIPS:
                first(t, k, True).wait_recv()
                passed(t, k, False).start()
        for t in range(n):
            first(t, 1, True).wait_recv()
            for k in OTHER_CHIPS:
                passed(t, k, True).wait_recv()
        for t in range(n):
            for k in (1,) + OTHER_CHIPS:
                first(t, k, False).wait_send()
            for k in OTHER_CHIPS:
                passed(t, k, False).wait_send()
            own(t).wait()

    anyspec = pl.BlockSpec(memory_space=pl.ANY)
    return pl.pallas_call(
        body, name=name,
        in_specs=[anyspec] * n,
        out_specs=[anyspec] * n,
        out_shape=[jax.ShapeDtypeStruct((N_DEV,) + tuple(a.shape), a.dtype) for a in arrays],
        scratch_shapes=[pltpu.SemaphoreType.DMA((n, N_DEV)), pltpu.SemaphoreType.DMA((n, N_DEV)),
                        pltpu.SemaphoreType.DMA((n,))],
    )(*arrays)


def pair_add(a, b, name):
    _, r, c = a.shape
    rb = r if r <= 512 else (128 if c > 1024 else 256)
    assert r % rb == 0

    def body(a_ref, b_ref, o_ref):
        o_ref[...] = (a_ref[...].astype(F32) + b_ref[...].astype(F32)).astype(o_ref.dtype)

    blk = pl.BlockSpec((1, rb, c), lambda q, i: (q, i, 0))
    return pl.pallas_call(
        body, name=name,
        grid=(N_CHIPS, r // rb),
        in_specs=[blk, blk],
        out_specs=blk,
        out_shape=jax.ShapeDtypeStruct(a.shape, a.dtype),
        compiler_params=_params(("parallel", "parallel")),
    )(a, b)


def adamw(parts, w, m, v, name):
    r, c = w.shape[-2:]
    n_parts = parts.shape[0]
    rb = r if r <= 512 else (128 if c > 1024 else 256)
    assert r % rb == 0

    def body(p_ref, w_ref, m_ref, v_ref, g_out, d_out, m_out, v_out):
        g = p_ref[0].astype(F32)
        for i in range(1, n_parts):
            g = g + p_ref[i].astype(F32)
        m_new = ADAM_B1 * m_ref[...] + (1.0 - ADAM_B1) * g
        v_new = ADAM_B2 * v_ref[...] + (1.0 - ADAM_B2) * (g * g)
        m_hat = m_new / (1.0 - ADAM_B1 ** ADAM_STEP)
        v_hat = v_new / (1.0 - ADAM_B2 ** ADAM_STEP)
        g_out[...] = g
        d_out[...] = -ADAM_LR * (m_hat / (jnp.sqrt(v_hat) + ADAM_EPS) + ADAM_WD * w_ref[...])
        m_out[...] = m_new
        v_out[...] = v_new

    if w.ndim == 3:
        blk = pl.BlockSpec((None, rb, c), lambda i: (0, i, 0))
    else:
        blk = pl.BlockSpec((rb, c), lambda i: (i, 0))
    return pl.pallas_call(
        body, name=name,
        grid=(r // rb,),
        in_specs=[pl.BlockSpec((n_parts, rb, c), lambda i: (0, i, 0)), blk, blk, blk],
        out_specs=[blk] * 4,
        out_shape=[jax.ShapeDtypeStruct(w.shape, F32)] * 4,
        compiler_params=_params(("parallel",)),
    )(parts, w, m, v)


def _pad_lanes(vec, n=LANES):
    return jnp.pad(vec, ((0, 0), (0, n - vec.shape[1])))


COL_SHARDED = ("w_in", "ssd_conv_w", "w_attn_proj", "w_up", "ffn_conv_w")
MATRICES = ("w_in", "w_attn_proj", "w_up", "w_ssd_proj", "w_out", "w_down")
LATE = ("w_ssd_proj", "w_attn_proj", "w_out", "w_up", "ffn_conv_w", "w_down")


def _narrow(name, a):
    return a.astype(MXU) if name in MATRICES else a


def _from_gathered(name, g):
    if name in COL_SHARDED:
        return jnp.transpose(g, (1, 0, 2)).reshape(g.shape[1], N_DEV * g.shape[2])
    return g.reshape(N_DEV * g.shape[1], g.shape[2])


def _to_slabs(name, g):
    if name in COL_SHARDED:
        return jnp.transpose(g.reshape(g.shape[0], N_DEV, g.shape[1] // N_DEV), (1, 0, 2))
    return g.reshape(N_DEV, g.shape[0] // N_DEV, g.shape[1])


def _columns(m, a, b):
    if m.ndim == 2:
        return m[:, a:b]
    c = m.shape[2]
    cuts = [m[j][:, max(a - j * c, 0):min(b - j * c, c)] for j in range(a // c, (b - 1) // c + 1)]
    return cuts[0] if len(cuts) == 1 else jnp.concatenate(cuts, axis=1)


def _column_shards(pieces, c):
    shards = []
    for j in range(N_DEV):
        cuts = []
        for start, arr in pieces:
            lo, hi = max(j * c - start, 0), min((j + 1) * c - start, arr.shape[1])
            if lo < hi:
                cuts.append(arr[:, lo:hi])
        shards.append(cuts[0] if len(cuts) == 1 else jnp.concatenate(cuts, axis=1))
    return jnp.stack(shards)


def local_step(x, target, w, late=None):
    n_seq, seq, _ = x.shape
    rows = n_seq * seq
    x = x.reshape(rows, D_MODEL)
    target = target.reshape(rows, D_MODEL)
    mx = lambda a: a.astype(MXU)

    splits = [sum(IN_WIDTHS[:i]) for i in range(len(IN_WIDTHS) + 1)]
    w_in = w["w_in"]
    part = lambda i: _columns(w_in, splits[i], splits[i + 1])
    w_z, w_xbc, w_gs, w_ga = mx(part(0)), mx(part(1)), mx(part(6)), mx(part(7))
    w_dt = mx(_pad_lanes(part(2)))
    head_group = lambda t, g: (splits[3 + t] + g * ATT_OUT, splits[3 + t] + (g + 1) * ATT_OUT)
    w_qkv = [mx(jnp.concatenate([_columns(w_in, *head_group(t, g)) for t in range(3)], axis=1))
             for g in range(ATT_GROUPS)]
    conv_w, conv_b, fconv_b = w["ssd_conv_w"], w["ssd_conv_b"], w["ffn_conv_b"]
    dt_bias, a_log, d_skip = _pad_lanes(w["dt_bias"]), _pad_lanes(w["a_log"]), _pad_lanes(w["d_skip"])
    g1, g2, gn, gq, gk = w["norm1_g"], w["norm2_g"], w["ssd_norm_g"], w["q_norm_g"], w["k_norm_g"]

    tb = min(512, seq)
    tbm = min(256, seq)
    cw = 1024
    rw = lambda fn, name, ncol, ins, params=(), outs=(), accs=(), tb_=tb: rowwise(
        fn, name, rows, seq, tb_, ncol, ins, params, outs, accs)

    (h,) = rw(lambda ctx, xv, g: _rms_fwd(xv, g), "rms1_fwd", 1, [(x, D_MODEL, 0, None)], [(g1, None, 0)],
              [(D_MODEL, D_MODEL, 0, MXU)])
    z = matmul(h, w_z, "mm_z")
    xbc = matmul(h, w_xbc, "mm_xbc")
    dtraw = matmul(h, w_dt, "mm_dt")
    by_residue = lambda a, g: _by_residue(a, n_seq, seq, ATT_DILATIONS[g])
    by_token = lambda a, g: _by_token(a, n_seq, seq, ATT_DILATIONS[g])
    h_res = [by_residue(h, g) for g in range(ATT_GROUPS)]
    gq_t, gk_t = jnp.tile(gq, (1, ATT_H)), jnp.tile(gk, (1, ATT_H))
    qkv_gains = jnp.concatenate([gq_t * ATT_SCALE, gk_t, jnp.ones_like(gk_t)], axis=1)

    def qk_norm(r, gains):
        q, k = r[:, :ATT_OUT], r[:, ATT_OUT:2 * ATT_OUT]
        return r, jnp.concatenate([q * _head_rstd(q), k * _head_rstd(k), r[:, 2 * ATT_OUT:]], axis=1) * gains

    qkv, nq = zip(*[matmul(h_res[g], w_qkv[g], f"mm_qkv{g}", tail=_Tail(qk_norm, rows=[qkv_gains], outs=[F32, MXU]))
                    for g in range(ATT_GROUPS)])
    gs = matmul(h, w_gs, "mm_gs")
    ga = matmul(h, w_ga, "mm_ga")

    def conv_silu(ctx, xh, wv, bv):
        pre = bv + _conv_prev(xh[0], xh[1], wv, ctx.first, SSD_CONV)
        return _silu(pre), pre

    xact, xpre = rw(conv_silu, "ssd_conv_fwd", CONV_DIM // cw, [(xbc, cw, 0, "prev")],
                    [(conv_w, cw, 0), (conv_b, cw, 0)], [(CONV_DIM, cw, 0, F32), (CONV_DIM, cw, 0, F32)])
    if late is None:
        y, sin = ssd_fwd(xact, dtraw, dt_bias, a_log, d_skip, n_seq, seq)
    else:
        y, sin, *gathered = ssd_fwd(xact, dtraw, dt_bias, a_log, d_skip, n_seq, seq,
                                    comm=direct_exchange([(late[n], "gather") for n in LATE]))
        w = {**w, **{n: g if n == "w_up" else _from_gathered(n, g) for n, g in zip(LATE, gathered)}}
    w_sp, w_ap, w_o, w_d = mx(w["w_ssd_proj"]), mx(w["w_attn_proj"]), mx(w["w_out"]), mx(w["w_down"])
    w_ug, w_uv = mx(_columns(w["w_up"], 0, D_FF)), mx(_columns(w["w_up"], D_FF, 2 * D_FF))
    fconv_w = w["ffn_conv_w"]

    def gated_norm(yv, zv, g):
        yz = yv * _silu(zv)
        return jnp.concatenate([_rms_fwd(yz[:, i:i + NORM_GROUP], g[:, i:i + NORM_GROUP])
                                for i in range(0, D_INNER, NORM_GROUP)], axis=1)

    att = [attn_fwd(nq[g], n_seq, seq, ATT_DILATIONS[g], f"attn_fwd{g}") for g in range(ATT_GROUPS)]

    def combine(ctx, o0, o1, o2, l0, l1, l2):
        mxl = jnp.maximum(jnp.maximum(l0, l1), l2)
        e = [jnp.exp(l - mxl) for l in (l0, l1, l2)]
        inv = 1.0 / (e[0] + e[1] + e[2])
        ws = [ei * inv for ei in e]
        out = sum(_expand_heads(wi) * oi for wi, oi in zip(ws, (o0, o1, o2)))
        return (out, *ws)

    y_attn, wt0, wt1, wt2 = rw(
        combine, "attn_combine", 1,
        [(by_token(att[g][0], g), ATT_OUT, 0, None) for g in range(3)]
        + [(by_token(att[g][1], g), LANES, 0, None) for g in range(3)], [],
        [(ATT_OUT, ATT_OUT, 0, F32)] + [(LANES, LANES, 0, F32)] * 3)
    wts = (wt0, wt1, wt2)

    ps, y_ssd = matmul(y, w_sp, "mm_ssd_proj", head=_Head(gated_norm, like=[z], rows=[gn], out=MXU))
    pa, merged = matmul(y_attn, w_ap, "mm_attn_proj",
                        tail=_Tail(lambda r, a, c, d: (r, _sigmoid(c) * a + _sigmoid(d) * r), like=[ps, gs, ga],
                                   outs=[F32, MXU]))
    x1, h2 = matmul(merged, w_o, "mm_out", add=x,
                    tail=_Tail(lambda r, g: (r, _rms_fwd(r, g)), rows=[g2], outs=[F32, MXU]))
    up_g = matmul(h2, w_ug, "mm_up_g")
    up_v = matmul(h2, w_uv, "mm_up_v")
    fw = D_FF // 2
    nfc = D_FF // fw

    def mlp_act(ctx, ug, uv, wg, wv, bg, bv):
        cg = bg + _conv_prev(ug[0], ug[1], wg, ctx.first, FFN_CONV)
        cv = bv + _conv_prev(uv[0], uv[1], wv, ctx.first, FFN_CONV)
        return _silu(cg) * cv, cg, cv

    act, conv_g, conv_v = rw(mlp_act, "mlp_act_fwd", nfc, [(up_g, fw, 0, "prev"), (up_v, fw, 0, "prev")],
                             [(fconv_w, fw, 0), (fconv_w, fw, nfc), (fconv_b, fw, 0), (fconv_b, fw, nfc)],
                             [(D_FF, fw, 0, MXU), (D_FF, fw, 0, F32), (D_FF, fw, 0, F32)], tb_=tbm)
    def loss_tail(out, tv):
        d = out - tv
        g = d * (1.0 / D_MODEL)
        return g, g, jnp.sum(d * d, axis=0, keepdims=True)

    dx2, dx2_m, sq = matmul(act, w_d, "mm_down", add=x1,
                            tail=_Tail(loss_tail, like=[target], outs=[F32, MXU], n_sums=1))

    grads = {}
    dact = matmul(dx2_m, w_d, "mm_d_act", tb=True)
    grads["w_down"] = matmul(act, dx2_m, "mm_dw_down", ta=True, out_dtype=MXU)

    def mlp_bwd(ctx, da, cg, cv, ug, uv, wg, wv):
        (da_c, da_n), (cg_c, cg_n), (cv_c, cv_n) = da, cg, cv
        dup_g_, dwg, dbg = _conv_bwd(da_c * cv_c * _silu_grad(cg_c), da_n * cv_n * _silu_grad(cg_n), ug, wg, ctx,
                                     FFN_CONV)
        dup_v_, dwv, dbv = _conv_bwd(da_c * _silu(cg_c), da_n * _silu(cg_n), uv, wv, ctx, FFN_CONV)
        return dup_g_, dup_v_, dwg, dbg, dwv, dbv

    dup_g, dup_v, dfw_g, dfb_g, dfw_v, dfb_v = rw(
        mlp_bwd, "mlp_bwd", nfc,
        [(dact, fw, 0, "next"), (conv_g, fw, 0, "next"), (conv_v, fw, 0, "next"), (up_g, fw, 0, None),
         (up_v, fw, 0, None)], [(fconv_w, fw, 0), (fconv_w, fw, nfc)],
        [(D_FF, fw, 0, MXU), (D_FF, fw, 0, MXU)], [(FFN_CONV, fw), (1, fw), (FFN_CONV, fw), (1, fw)], tb_=tbm)
    grads["ffn_conv_w"] = jnp.concatenate([dfw_g, dfw_v], axis=1)
    grads["ffn_conv_b"] = jnp.concatenate([dfb_g, dfb_v], axis=1)
    def rms_bwd_fn(dh_, xv, dres, g):
        dxv, dg = _rms_bwd(xv, g, dh_)
        return dres + dxv, dg

    def rms_bwd_fn2(dh_, xv, dres, g):
        dxv, dg = rms_bwd_fn(dh_, xv, dres, g)
        return dxv, dxv, dg

    dh2 = matmul(dup_g, w_ug, "mm_dh2_g", tb=True)
    dx1, dx1_m, grads["norm2_g"] = matmul(
        dup_v, w_uv, "mm_dh2_v", tb=True, add=dh2,
        tail=_Tail(rms_bwd_fn2, like=[x1, dx2], rows=[g2], outs=[F32, MXU], n_sums=1))
    dw_up =[(0, matmul(h2, dup_g, "mm_dw_up_g", ta=True, out_dtype=MXU)),
             (D_FF, matmul(h2, dup_v, "mm_dw_up_v", ta=True, out_dtype=MXU))]
    if w["w_up"].ndim == 3:
        grads["w_up"] = _column_shards(dw_up, w["w_up"].shape[2])
    else:
        grads["w_up"] = jnp.concatenate([p for _, p in dw_up], axis=1)

    def merge_bwd(dm, a, b, c, d):
        sc, sd = _sigmoid(c), _sigmoid(d)
        return dm * sc, dm * sd, dm * a * sc * (1.0 - sc), dm * b * sd * (1.0 - sd)

    dps, dpa, dgs, dga = matmul(dx1_m, w_o, "mm_d_merged", tb=True,
                                tail=_Tail(merge_bwd, like=[ps, pa, gs, ga], outs=[MXU] * 4))
    grads["w_out"] = matmul(merged, dx1_m, "mm_dw_out", ta=True, out_dtype=MXU)

    def gated_norm_bwd(dyn, yv, zv, g):
        sz = _silu(zv)
        yz = yv * sz
        dyz, dgs_ = [], []
        for i in range(0, D_INNER, NORM_GROUP):
            a, b = _rms_bwd(yz[:, i:i + NORM_GROUP], g[:, i:i + NORM_GROUP], dyn[:, i:i + NORM_GROUP])
            dyz.append(a)
            dgs_.append(b)
        dyz = jnp.concatenate(dyz, axis=1)
        return dyz * sz, dyz * yv * _silu_grad(zv), jnp.concatenate(dgs_, axis=1)

    dy, dz, grads["ssd_norm_g"] = matmul(dps, w_sp, "mm_d_y_ssd", tb=True,
                                         tail=_Tail(gated_norm_bwd, like=[y, z], rows=[gn], outs=[F32, MXU], n_sums=1))
    grads["w_ssd_proj"] = matmul(y_ssd, dps, "mm_dw_ssd_proj", ta=True, out_dtype=MXU)
    dy_attn = matmul(dpa, w_ap, "mm_d_y_attn", tb=True)
    grads["w_attn_proj"] = matmul(y_attn, dpa, "mm_dw_attn_proj", ta=True, out_dtype=MXU)

    (rsum,) = rw(lambda ctx, a, b: _reduce_heads(a * b), "attn_rsum", 1,
                 [(dy_attn, ATT_OUT, 0, None), (y_attn, ATT_OUT, 0, None)], [], [(LANES, LANES, 0, F32)])
    dqkv, dgq, dgk = [], 0.0, 0.0
    for g in range(ATT_GROUPS):
        dn = attn_bwd(nq[g], by_residue(dy_attn, g), att[g][1], by_residue(wts[g], g), by_residue(rsum, g), n_seq, seq,
                      ATT_DILATIONS[g], f"attn_bwd{g}")
        d_, a_, b_ = qk_post(qkv[g], dn, gq_t, gk_t, rows, f"qk_post{g}")
        dqkv.append(d_)
        dgq, dgk = dgq + a_, dgk + b_
    per_head = lambda v: jnp.sum(v.reshape(ATT_H, ATT_HD), axis=0, keepdims=True)
    grads["q_norm_g"], grads["k_norm_g"] = per_head(dgq), per_head(dgk)

    if late is None:
        dxact, ddt, dbias, dalog, ddskip = ssd_bwd(xact, dtraw, dt_bias, a_log, d_skip, sin, dy, n_seq, seq)
    else:
        dxact, ddt, dbias, dalog, ddskip, *parts = ssd_bwd(
            xact, dtraw, dt_bias, a_log, d_skip, sin, dy, n_seq, seq,
            comm=direct_exchange([(grads[n] if n == "w_up" else _to_slabs(n, _narrow(n, grads[n])), "scatter")
                                  for n in LATE]))
        grads.update(zip(LATE, parts))
    grads["dt_bias"], grads["a_log"], grads["d_skip"] = dbias[:, :SSD_H], dalog[:, :SSD_H], ddskip[:, :SSD_H]

    def conv_silu_bwd(ctx, dxa, pre, xin, wv):
        return _conv_bwd(dxa[0] * _silu_grad(pre[0]), dxa[1] * _silu_grad(pre[1]), xin, wv, ctx, SSD_CONV)

    dxbc, grads["ssd_conv_w"], grads["ssd_conv_b"] = rw(
        conv_silu_bwd, "ssd_conv_bwd", CONV_DIM // cw,
        [(dxact, cw, 0, "next"), (xpre, cw, 0, "next"), (xbc, cw, 0, None)],
        [(conv_w, cw, 0)], [(CONV_DIM, cw, 0, MXU)], [(SSD_CONV, cw), (1, cw)])

    pieces = [(d_, d_, h, w_, tag) for d_, w_, tag in
              ((dz, w_z, "z"), (ddt, w_dt, "dt"), (dgs, w_gs, "gs"), (dga, w_ga, "ga"))]
    pieces += [(by_token(dqkv[g], g), dqkv[g], h_res[g], w_qkv[g], f"qkv{g}") for g in range(ATT_GROUPS)]
    pieces += [(dxbc, dxbc, h, w_xbc, "xbc")]
    dws = {tag: matmul(h_in, dpart_h, f"mm_dw_{tag}", ta=True, out_dtype=MXU) for _, dpart_h, h_in, _, tag in pieces}
    dw_in = [(splits[0], dws["z"]), (splits[1], dws["xbc"]), (splits[2], dws["dt"][:, :SSD_H])]
    dw_in += [(head_group(t, g)[0], dws[f"qkv{g}"][:, t * ATT_OUT:(t + 1) * ATT_OUT])
              for t in range(3) for g in range(ATT_GROUPS)]
    dw_in += [(splits[6], dws["gs"]), (splits[7], dws["ga"])]
    if w_in.ndim == 3:
        grads["w_in"] = _column_shards(dw_in, w_in.shape[2])
    else:
        grads["w_in"] = jnp.concatenate([p for _, p in dw_in], axis=1)
    dh = None
    for idx, (dpart, _, _, wpart, tag) in enumerate(pieces):
        comm, tail = NO_EXCHANGE, None
        if late is not None and idx == 0:
            slabs = [grads[n] if n == "w_in" else _to_slabs(n, _narrow(n, grads[n])) for n in EARLY]
            comm = sibling_exchange(slabs)
        if idx == len(pieces) - 1:
            tail = _Tail(rms_bwd_fn, like=[x, dx1], rows=[g1], outs=[F32], n_sums=1)
            if late is not None:
                comm = chip_exchange(summed)
        dh = matmul(dpart, wpart, f"mm_dh_{tag}", tb=True, add=dh, comm=comm, tail=tail)
        if late is not None and idx == 0:
            dh, *arrived = dh
            core = lax.axis_index("c")
            own = [lax.dynamic_index_in_dim(s.reshape((N_CHIPS, 2) + s.shape[1:]), core, axis=1, keepdims=False)
                   for s in slabs]
            summed = [pair_add(a_, b_, f"rs_add_{n}") for n, a_, b_ in zip(EARLY, own, arrived)]
    grad_x, grads["norm1_g"], *arrived = dh
    grads.update(zip(EARLY, arrived))
    return sq, grad_x.reshape(n_seq, seq, D_MODEL), grads


EARLY = ("w_in", "ssd_conv_w")
REPLICATED = ("norm1_g", "ssd_conv_b", "dt_bias", "a_log", "d_skip", "ssd_norm_g", "q_norm_g", "k_norm_g",
              "norm2_g", "ffn_conv_b")
WEIGHTS = ("norm1_g", "w_in", "ssd_conv_w", "ssd_conv_b", "dt_bias", "a_log", "d_skip", "ssd_norm_g", "w_ssd_proj",
           "q_norm_g", "k_norm_g", "w_attn_proj", "w_out", "norm2_g", "w_up", "ffn_conv_w", "ffn_conv_b", "w_down")
PACK_ROWS, PACK_COLS = 8, 2048


def _pack(vals):
    flat = jnp.concatenate([vals[n].reshape(-1) for n in REPLICATED])
    return jnp.pad(flat, (0, PACK_ROWS * PACK_COLS - flat.shape[0])).reshape(PACK_ROWS, PACK_COLS)


def _unpack(packed, like):
    flat = packed.reshape(-1)
    out, pos = {}, 0
    for n in REPLICATED:
        size = like[n].size
        out[n] = flat[pos:pos + size].reshape(like[n].shape)
        pos += size
    return out


def step(x, target, w_raw, m_raw, v_raw):
    wsh = {n: a[0] if a.ndim == 3 else a for n, a in w_raw.items()}
    gathered = all_gather([_narrow(n, wsh[n]) for n in EARLY], "ag_weights")
    full = {n: wsh[n] for n in REPLICATED}
    full.update({n: g if n == "w_in" else _from_gathered(n, g) for n, g in zip(EARLY, gathered)})

    sq, grad_x, grads = local_step(x, target, full, late={n: _narrow(n, wsh[n]) for n in LATE})

    (small,) = all_gather([_pack({n: grads[n] for n in REPLICATED})], "ag_small")

    out_g, out_d, out_m, out_v = {}, {}, {}, {}
    for n in EARLY + LATE:
        out_g[n], out_d[n], out_m[n], out_v[n] = adamw(grads[n], w_raw[n], m_raw[n], v_raw[n], f"adamw_{n}")
    pk = adamw(small, _pack(w_raw), _pack(m_raw), _pack(v_raw), "adamw_small")
    for dst, packed in zip((out_g, out_d, out_m, out_v), pk):
        dst.update(_unpack(packed, w_raw))
    loss = lax.psum(0.5 * jnp.sum(sq) / D_MODEL, ("x", "y", "c"))
    return loss, grad_x, out_g, out_d, out_m, out_v


def kernel(x, norm1_g, w_in, ssd_conv_w, ssd_conv_b, dt_bias, a_log, d_skip, ssd_norm_g, w_ssd_proj, q_norm_g, k_norm_g, w_attn_proj, w_out, norm2_g, w_up, ffn_conv_w, ffn_conv_b, w_down, loss_target, m_norm1_g, m_w_in, m_ssd_conv_w, m_ssd_conv_b, m_dt_bias, m_a_log, m_d_skip, m_ssd_norm_g, m_w_ssd_proj, m_q_norm_g, m_k_norm_g, m_w_attn_proj, m_w_out, m_norm2_g, m_w_up, m_ffn_conv_w, m_ffn_conv_b, m_w_down, v_norm1_g, v_w_in, v_ssd_conv_w, v_ssd_conv_b, v_dt_bias, v_a_log, v_d_skip, v_ssd_norm_g, v_w_ssd_proj, v_q_norm_g, v_k_norm_g, v_w_attn_proj, v_w_out, v_norm2_g, v_w_up, v_ffn_conv_w, v_ffn_conv_b, v_w_down):
    ws = (norm1_g, w_in, ssd_conv_w, ssd_conv_b, dt_bias, a_log, d_skip, ssd_norm_g, w_ssd_proj, q_norm_g, k_norm_g,
          w_attn_proj, w_out, norm2_g, w_up, ffn_conv_w, ffn_conv_b, w_down)
    ms = (m_norm1_g, m_w_in, m_ssd_conv_w, m_ssd_conv_b, m_dt_bias, m_a_log, m_d_skip, m_ssd_norm_g, m_w_ssd_proj,
          m_q_norm_g, m_k_norm_g, m_w_attn_proj, m_w_out, m_norm2_g, m_w_up, m_ffn_conv_w, m_ffn_conv_b, m_w_down)
    vs = (v_norm1_g, v_w_in, v_ssd_conv_w, v_ssd_conv_b, v_dt_bias, v_a_log, v_d_skip, v_ssd_norm_g, v_w_ssd_proj,
          v_q_norm_g, v_k_norm_g, v_w_attn_proj, v_w_out, v_norm2_g, v_w_up, v_ffn_conv_w, v_ffn_conv_b, v_w_down)
    loss, grad_x, g, d, m, v = step(x, loss_target, dict(zip(WEIGHTS, ws)), dict(zip(WEIGHTS, ms)), dict(zip(WEIGHTS, vs)))
    ordered = lambda dct: [dct[n] for n in WEIGHTS]
    return (loss, grad_x, *ordered(g), *ordered(d), *ordered(m), *ordered(v))
```

```python
import functools

import jax
import jax.numpy as jnp
from jax import lax
from jax.experimental import pallas as pl
from jax.experimental.pallas import tpu as pltpu

F32 = jnp.float32
BF16 = jnp.bfloat16
MXU = jnp.bfloat16
HIGHEST = lax.Precision.HIGHEST
VMEM_LIMIT_BYTES = 48 * 1024 * 1024
SUBLANES = 8
LANES = 128
N_DEV = 8

D_MODEL = 1024
D_INNER = 2048
SSD_P = 64
SSD_H = 32
SSD_G = 8
SSD_K = SSD_H // SSD_G
SSD_N = 128
SSD_Q = 128
SSD_CONV = 4
CONV_DIM = D_INNER + 2 * SSD_G * SSD_N
NORM_GROUP = D_INNER // SSD_G
ATT_GROUPS = 3
ATT_H = 8
ATT_HD = 64
ATT_BLK = 128
ATT_OUT = ATT_H * ATT_HD
ATT_DILATIONS = (1, 4, 16)
ATT_SCALE = ATT_HD ** -0.5
D_FF = 2816
FFN_CONV = 3
EPS = 1e-6
NEG = -1e30
IN_WIDTHS = (D_INNER, CONV_DIM, SSD_H, 3 * ATT_OUT, 3 * ATT_OUT, 3 * ATT_OUT, D_MODEL, D_MODEL)

ADAM_LR = 0.001
ADAM_B1 = 0.9
ADAM_B2 = 0.999
ADAM_EPS = 1e-08
ADAM_WD = 0.01
ADAM_STEP = 10


def _mm(a, b, dims):
    return lax.dot_general(a.astype(MXU), b.astype(MXU), (dims, ((), ())), preferred_element_type=F32)


def _dot_nn(a, b):
    return _mm(a, b, ((1,), (0,)))


def _dot_nt(a, b):
    return _mm(a, b, ((1,), (1,)))


def _dot_tn(a, b):
    return _mm(a, b, ((0,), (0,)))


def _dot_f32(a, b):
    return lax.dot_general(a, b, (((1,), (0,)), ((), ())), precision=HIGHEST, preferred_element_type=F32)


def _sigmoid(x):
    return 1.0 / (1.0 + jnp.exp(-x))


def _silu(x):
    return x * _sigmoid(x)


def _silu_grad(x):
    s = _sigmoid(x)
    return s * (1.0 + x * (1.0 - s))


def _softplus(x):
    return jnp.maximum(x, 0.0) + jnp.log(1.0 + jnp.exp(-jnp.abs(x)))


def _rms_fwd(x, g):
    r = lax.rsqrt(jnp.mean(x * x, axis=-1, keepdims=True) + EPS)
    return x * r * g


def _rms_bwd(x, g, dy):
    r = lax.rsqrt(jnp.mean(x * x, axis=-1, keepdims=True) + EPS)
    xh = x * r
    dyg = dy * g
    dx = r * (dyg - xh * jnp.mean(dyg * xh, axis=-1, keepdims=True))
    return dx, jnp.sum(dy * xh, axis=0, keepdims=True)


def _onehot_row(h, n=LANES):
    return (lax.broadcasted_iota(jnp.int32, (1, n), 1) == h).astype(F32)


def _onehot_col(h, n=LANES):
    return (lax.broadcasted_iota(jnp.int32, (n, 1), 0) == h).astype(F32)


def _head_expand_matrix():
    r = lax.broadcasted_iota(jnp.int32, (LANES, ATT_OUT), 0)
    c = lax.broadcasted_iota(jnp.int32, (LANES, ATT_OUT), 1)
    return (c // ATT_HD == r).astype(F32)


def _split_bf16(x, parts):
    out = []
    for _ in range(parts - 1):
        hi = x.astype(BF16).astype(F32)
        out.append(hi)
        x = x - hi
    out.append(x)
    return out


def _expand_heads(w):
    e = _head_expand_matrix()
    return sum(_dot_nn(p, e) for p in _split_bf16(w, 2))


def _reduce_heads(x):
    e = _head_expand_matrix()
    return sum(_dot_nt(p, e) for p in _split_bf16(x, 3))


def _shift_prev(cur, halo, s, first):
    if s == 0:
        return cur
    rolled = pltpu.roll(cur, s, 0)
    hr = jnp.where(first, 0.0, pltpu.roll(halo, s, 0))
    rows = lax.broadcasted_iota(jnp.int32, halo.shape, 0)
    head = jnp.where(rows < s, hr, rolled[:SUBLANES])
    if cur.shape[0] == SUBLANES:
        return head
    return jnp.concatenate([head, rolled[SUBLANES:]], axis=0)


def _shift_next(cur, halo, s, last):
    if s == 0:
        return cur
    tb = cur.shape[0]
    rolled = pltpu.roll(cur, tb - s, 0)
    hr = jnp.where(last, 0.0, pltpu.roll(halo, SUBLANES - s, 0))
    rows = lax.broadcasted_iota(jnp.int32, halo.shape, 0)
    tail = jnp.where(rows >= SUBLANES - s, hr, rolled[tb - SUBLANES:])
    return jnp.concatenate([rolled[:tb - SUBLANES], tail], axis=0)


def _conv_prev(x, halo, w, first, taps):
    acc = None
    for i in range(taps):
        term = w[i:i + 1, :] * _shift_prev(x, halo, taps - 1 - i, first)
        acc = term if acc is None else acc + term
    return acc


def _conv_bwd(dpre, dpre_next8, x, w, ctx, taps):
    dx, dws = None, []
    for i in range(taps):
        ahead = _shift_next(dpre, dpre_next8, taps - 1 - i, ctx.last)
        term = w[i:i + 1, :] * ahead
        dx = term if dx is None else dx + term
        dws.append(jnp.sum(ahead * x, axis=0, keepdims=True))
    return dx, jnp.concatenate(dws, axis=0), jnp.sum(dpre, axis=0, keepdims=True)


def _params(sem):
    return pltpu.CompilerParams(dimension_semantics=sem, vmem_limit_bytes=VMEM_LIMIT_BYTES)


N_CHIPS = N_DEV // 2
OTHER_CHIPS = (4, 2, 6)


class _Hosted:
    def __init__(self, arrays, out_shape, sems, ops):
        self.arrays, self.out_shape, self.sems, self.ops = list(arrays), list(out_shape), list(sems), ops
        self.n = len(self.arrays)
        self.specs = [pl.BlockSpec(memory_space=pl.ANY)] * self.n

    def begin(self, in_refs, out_refs, sem_refs, first):
        start, finish = self.ops(in_refs, out_refs, *sem_refs)
        pl.when(first)(start)
        return finish


NO_EXCHANGE = _Hosted((), (), (), None)


def _peer(k):
    x, y, c = lax.axis_index("x"), lax.axis_index("y"), lax.axis_index("c")
    px = 1 - x if k & 4 else x
    py = 1 - y if k & 2 else y
    pc = 1 - c if k & 1 else c
    return (px, py, pc), 4 * px + 2 * py + pc, 2 * px + py


def _remote(src, dst, send_sems, recv_sems, t, k, dev):
    return pltpu.make_async_remote_copy(src_ref=src, dst_ref=dst, send_sem=send_sems.at[t, k], recv_sem=recv_sems.at[t, k],
                                        device_id=dev, device_id_type=pl.DeviceIdType.MESH)


def direct_exchange(items):
    n = len(items)

    def ops(in_refs, out_refs, send_sems, recv_sems, local_sems):
        _, me, _ = _peer(0)
        part = lambda t, pid: in_refs[t] if items[t][1] == "gather" else in_refs[t].at[pid]

        def copy(t, k, arriving):
            dev, pid, _ = _peer(k)
            return _remote(part(t, pid), out_refs[t].at[pid if arriving else me], send_sems, recv_sems, t, k, dev)

        def own(t):
            return pltpu.make_async_copy(part(t, me), out_refs[t].at[me], local_sems.at[t])

        def start():
            for t in range(n):
                own(t).start()
                for k in range(1, N_DEV):
                    copy(t, k, False).start()

        def finish():
            for t in range(n):
                for k in range(1, N_DEV):
                    copy(t, k, True).wait_recv()
            for t in range(n):
                for k in range(1, N_DEV):
                    copy(t, k, False).wait_send()
                own(t).wait()

        return start, finish

    out_shape = [jax.ShapeDtypeStruct((N_DEV,) + tuple(a.shape if m == "gather" else a.shape[1:]), a.dtype)
                 for a, m in items]
    sems = [pltpu.SemaphoreType.DMA((n, N_DEV)), pltpu.SemaphoreType.DMA((n, N_DEV)), pltpu.SemaphoreType.DMA((n,))]
    return _Hosted([a for a, _ in items], out_shape, sems, ops)


def sibling_exchange(arrays):
    n = len(arrays)

    def ops(in_refs, out_refs, send_sems, recv_sems):
        sib, _, _ = _peer(1)
        c = lax.axis_index("c")
        copy = lambda t, q: _remote(in_refs[t].at[2 * q + (1 - c)], out_refs[t].at[q], send_sems, recv_sems, t, q, sib)

        def start():
            for t in range(n):
                for q in range(N_CHIPS):
                    copy(t, q).start()

        def finish():
            for t in range(n):
                for q in range(N_CHIPS):
                    copy(t, q).wait_recv()
            for t in range(n):
                for q in range(N_CHIPS):
                    copy(t, q).wait_send()

        return start, finish

    out_shape = [jax.ShapeDtypeStruct((N_CHIPS,) + a.shape[1:], a.dtype) for a in arrays]
    sems = [pltpu.SemaphoreType.DMA((n, N_CHIPS)), pltpu.SemaphoreType.DMA((n, N_CHIPS))]
    return _Hosted(arrays, out_shape, sems, ops)


def chip_exchange(arrays):
    n = len(arrays)

    def ops(in_refs, out_refs, send_sems, recv_sems, local_sems):
        _, _, mine = _peer(0)

        def copy(t, k, arriving):
            dev, _, q = _peer(k)
            return _remote(in_refs[t].at[q], out_refs[t].at[q if arriving else mine], send_sems, recv_sems, t, k, dev)

        def own(t):
            return pltpu.make_async_copy(in_refs[t].at[mine], out_refs[t].at[mine], local_sems.at[t])

        def start():
            for t in range(n):
                own(t).start()
                for k in OTHER_CHIPS:
                    copy(t, k, False).start()

        def finish():
            for t in range(n):
                for k in OTHER_CHIPS:
                    copy(t, k, True).wait_recv()
            for t in range(n):
                for k in OTHER_CHIPS:
                    copy(t, k, False).wait_send()
                own(t).wait()

        return start, finish

    out_shape = [jax.ShapeDtypeStruct(a.shape, a.dtype) for a in arrays]
    sems = [pltpu.SemaphoreType.DMA((n, N_DEV)), pltpu.SemaphoreType.DMA((n, N_DEV)), pltpu.SemaphoreType.DMA((n,))]
    return _Hosted(arrays, out_shape, sems, ops)


MATMUL_VMEM_BUDGET = 34 * 1024 * 1024


V7X_MXU_FLOPS = 996e12
V7X_HBM_BYTES_PER_S = 3.4e12
GRID_STEP_S = 0.35e-6


def _tile_sizes(dim, cap):
    return [t for t in range(LANES, min(dim, cap) + 1, LANES) if dim % t == 0] or [dim]


def _matmul_tiles(m, n, k, a_bytes, b_bytes, add_bytes, out_bytes, whole_rows=False, whole_k=False):
    best = None
    for tk in ([k] if whole_k else _tile_sizes(k, 8192)):
        nk = k // tk
        for tn in ([n] if whole_rows else _tile_sizes(n, 2048)):
            for tm in _tile_sizes(m, 2048):
                io = tm * tk * a_bytes + tk * tn * b_bytes
                ends = tm * tn * (add_bytes + out_bytes)
                need = 2 * (io + ends) + tm * tn * 4 * (2 if nk > 1 else 1)
                if need > MATMUL_VMEM_BUDGET:
                    continue
                step = max(2.0 * tm * tn * tk / V7X_MXU_FLOPS, (io + ends / nk) / V7X_HBM_BYTES_PER_S)
                if nk > 1:
                    step += tm * tn * 8 / V7X_HBM_BYTES_PER_S
                cost = (m // tm) * (n // tn) * nk * (step + GRID_STEP_S)
                if best is None or cost < best[0]:
                    best = (cost, tm, tn, tk)
    if best is None:
        raise ValueError((m, n, k))
    return best[1:]


class _Tail:
    def __init__(self, fn, like=(), rows=(), outs=(), n_sums=0):
        self.fn, self.like, self.rows, self.outs, self.n_sums = fn, list(like), list(rows), list(outs), n_sums


class _Head:
    def __init__(self, fn, like=(), rows=(), out=None):
        self.fn, self.like, self.rows, self.out = fn, list(like), list(rows), out


def matmul(a, b, name, ta=False, tb=False, add=None, out_dtype=F32, comm=NO_EXCHANGE, tail=None, head=None):
    assert not (ta and tb) and not (ta and head)
    m, k = (a.shape[1], a.shape[0]) if ta else a.shape
    n = b.shape[0] if tb else b.shape[1]
    assert (b.shape[1] if tb else b.shape[0]) == k
    like = ([] if add is None else [add]) + (tail.like if tail else [])
    rows = tail.rows if tail else []
    outs = tail.outs if tail else [out_dtype]
    n_sums = tail.n_sums if tail else 0
    h_like, h_rows = (head.like, head.rows) if head else ([], [])
    a_bytes = a.dtype.itemsize + sum(x.dtype.itemsize for x in h_like) + (jnp.dtype(head.out).itemsize if head else 0)
    tm, tn, tk = _matmul_tiles(m, n, k, a_bytes, b.dtype.itemsize, sum(x.dtype.itemsize for x in like),
                               sum(jnp.dtype(d).itemsize for d in outs),
                               whole_rows=tail is not None or head is not None, whole_k=head is not None)
    nk = k // tk
    grid = (m // tm, n // tn, nk)
    dims = ((0,), (0,)) if ta else (((1,), (1,)) if tb else ((1,), (0,)))
    n_plain = 2 + len(like) + len(rows)
    n_in = n_plain + len(h_like) + len(h_rows)
    n_out = len(outs) + n_sums + (1 if head else 0)
    n_acc = 0 if nk == 1 else 1

    def body(*refs):
        a_ref, b_ref = refs[:2]
        like_refs, row_refs = refs[2:2 + len(like)], refs[2 + len(like):n_plain]
        out_refs = refs[n_in + comm.n:n_in + comm.n + n_out]
        a_val = a_ref[...]
        if head:
            a_val = head.fn(a_val, *[x[...] for x in refs[n_plain:n_in]]).astype(head.out)
            out_refs[n_out - 1][...] = a_val
        ids = [pl.program_id(d) for d in range(3)]
        if comm.n:
            first = functools.reduce(jnp.logical_and, [i == 0 for i in ids])
            last = functools.reduce(jnp.logical_and, [i == g - 1 for i, g in zip(ids, grid)])
            done = comm.begin(refs[n_in:n_in + comm.n], refs[n_in + comm.n + n_out:n_in + 2 * comm.n + n_out],
                              refs[n_in + 2 * comm.n + n_out + n_acc:], first)

        def finish(r):
            if add is not None:
                r = r + like_refs[0][...].astype(F32)
            if tail is None:
                out_refs[0][...] = r.astype(out_dtype)
                return
            vals = tail.fn(r, *[x[...] for x in like_refs[len(like) - len(tail.like):]], *[x[...] for x in row_refs])
            for ref, val in zip(out_refs[:len(outs)], vals):
                ref[...] = val.astype(ref.dtype)
            for ref, val in zip(out_refs[len(outs):], vals[len(outs):]):
                @pl.when(ids[0] == 0)
                def _(ref=ref, val=val):
                    ref[...] = val

                @pl.when(ids[0] != 0)
                def _(ref=ref, val=val):
                    ref[...] += val

        if nk == 1:
            finish(_mm(a_val, b_ref[...], dims))
        else:
            acc = refs[n_in + 2 * comm.n + n_out]

            @pl.when(ids[2] == 0)
            def _():
                acc[...] = jnp.zeros_like(acc)

            acc[...] += _mm(a_val, b_ref[...], dims)

            @pl.when(ids[2] == nk - 1)
            def _():
                finish(acc[...])

        if comm.n:
            pl.when(last)(done)

    a_spec = pl.BlockSpec((tk, tm), lambda i, j, kk: (kk, i)) if ta else pl.BlockSpec((tm, tk), lambda i, j, kk: (i, kk))
    b_spec = pl.BlockSpec((tn, tk), lambda i, j, kk: (j, kk)) if tb else pl.BlockSpec((tk, tn), lambda i, j, kk: (kk, j))
    tile = pl.BlockSpec((tm, tn), lambda i, j, kk: (i, j))
    row = pl.BlockSpec((1, tn), lambda i, j, kk: (0, j))
    sequential = comm.n or n_sums
    res = pl.pallas_call(
        body, name=name,
        grid=grid,
        in_specs=[a_spec, b_spec] + [tile] * len(like) + [row] * len(rows) + [a_spec] * len(h_like)
        + [pl.BlockSpec((1, tk), lambda i, j, kk: (0, kk))] * len(h_rows) + comm.specs,
        out_specs=[tile] * len(outs) + [row] * n_sums + ([a_spec] if head else []) + comm.specs,
        out_shape=[jax.ShapeDtypeStruct((m, n), d) for d in outs] + [jax.ShapeDtypeStruct((1, n), F32)] * n_sums
        + ([jax.ShapeDtypeStruct((m, k), head.out)] if head else []) + comm.out_shape,
        scratch_shapes=([] if nk == 1 else [pltpu.VMEM((tm, tn), F32)]) + comm.sems,
        compiler_params=_params(("arbitrary",) * 3 if sequential else ("parallel", "parallel", "arbitrary")),
    )(a, b, *like, *rows, *h_like, *h_rows, *comm.arrays)
    return res if (comm.n or tail or head) else res[0]


class _Ctx:
    def __init__(self, first, last):
        self.first = first
        self.last = last


def rowwise(fn, name, rows, seq, tb, ncol, ins, params=(), outs=(), accs=()):
    assert rows % tb == 0 and seq % tb == 0 and tb % 16 == 0
    bps = seq // tb
    nrow = rows // tb
    r8 = tb // SUBLANES
    args, in_specs = [], []
    for arr, w, off, halo in ins:
        args.append(arr)
        in_specs.append(pl.BlockSpec((tb, w), lambda j, i, off=off: (i, off + j)))
        if halo in ("prev", "both"):
            args.append(arr)
            in_specs.append(pl.BlockSpec((SUBLANES, w), lambda j, i, off=off: (jnp.maximum(i * r8 - 1, 0), off + j)))
        if halo in ("next", "both"):
            args.append(arr)
            in_specs.append(pl.BlockSpec(
                (SUBLANES, w), lambda j, i, off=off: (jnp.minimum((i + 1) * r8, rows // SUBLANES - 1), off + j)))
    for arr, w, off in params:
        args.append(arr)
        if w is None:
            in_specs.append(pl.BlockSpec(arr.shape, lambda j, i: (0, 0)))
        else:
            in_specs.append(pl.BlockSpec((arr.shape[0], w), lambda j, i, off=off: (0, off + j)))
    out_shape, out_specs = [], []
    for total, w, off, dt in outs:
        out_shape.append(jax.ShapeDtypeStruct((rows, total), dt))
        out_specs.append(pl.BlockSpec((tb, w), lambda j, i, off=off: (i, off + j)))
    for r, w in accs:
        out_shape.append(jax.ShapeDtypeStruct((r, ncol * w), F32))
        out_specs.append(pl.BlockSpec((r, w), lambda j, i: (0, j)))
    n_out, n_acc = len(outs), len(accs)

    def body(*refs):
        i = pl.program_id(1)
        pos = 0
        vals = []
        for _, _, _, halo in ins:
            cur = refs[pos][...]
            pos += 1
            if halo is None:
                vals.append(cur)
            elif halo == "both":
                vals.append((cur, refs[pos][...], refs[pos + 1][...]))
                pos += 2
            else:
                vals.append((cur, refs[pos][...]))
                pos += 1
        for _ in params:
            vals.append(refs[pos][...])
            pos += 1
        ctx = _Ctx(i % bps == 0, i % bps == bps - 1)
        res = fn(ctx, *vals)
        if not isinstance(res, (tuple, list)):
            res = (res,)
        assert len(res) == n_out + n_acc
        for q in range(n_out):
            refs[pos + q][...] = res[q].astype(refs[pos + q].dtype)
        for q in range(n_acc):
            ref, val = refs[pos + n_out + q], res[n_out + q]

            @pl.when(i == 0)
            def _(ref=ref, val=val):
                ref[...] = val

            @pl.when(i != 0)
            def _(ref=ref, val=val):
                ref[...] += val

    res = pl.pallas_call(
        body, name=name,
        grid=(ncol, nrow),
        in_specs=in_specs,
        out_specs=out_specs,
        out_shape=out_shape,
        compiler_params=_params(("parallel", "arbitrary")),
    )(*args)
    return res


GROUP_W = SSD_K * SSD_P


def _tri(lower):
    r = lax.broadcasted_iota(jnp.int32, (SSD_Q, SSD_Q), 0)
    c = lax.broadcasted_iota(jnp.int32, (SSD_Q, SSD_Q), 1)
    return r >= c if lower else r <= c


def _first_head_lanes():
    return lax.broadcasted_iota(jnp.int32, (1, LANES), 1) < SSD_P


def _column(v, j):
    return v[:, j * LANES:(j + 1) * LANES] if v.shape[-1] == GROUP_W else v


def _per_head(vals):
    first = _first_head_lanes()
    return jnp.concatenate([jnp.where(first, _column(vals[2 * j], j), _column(vals[2 * j + 1], j))
                            for j in range(GROUP_W // LANES)], axis=1)


def _per_head_rows(vals):
    return jnp.concatenate([jnp.broadcast_to(v, (SSD_P, 1)) for v in vals], axis=0)


def _own_columns(slab, k):
    keep = _first_head_lanes() if k % 2 == 0 else jnp.logical_not(_first_head_lanes())
    own = jnp.where(keep, _column(slab, k // 2), 0.0)
    return jnp.concatenate([own, jnp.zeros_like(own)] if k < 2 else [jnp.zeros_like(own), own], axis=1)


def _headsum(prod, g):
    out = None
    for k in range(SSD_K):
        keep = _first_head_lanes() if k % 2 == 0 else jnp.logical_not(_first_head_lanes())
        term = jnp.sum(jnp.where(keep, _column(prod, k // 2), 0.0), axis=1, keepdims=True) * _onehot_row(g * SSD_K + k)
        out = term if out is None else out + term
    return out


def ssd_fwd(xact, dtraw, dt_bias, a_log, d_skip, n_seq, seq, comm=NO_EXCHANGE):
    nc = seq // SSD_Q
    rows = n_seq * seq
    nx = comm.n

    def body(*refs):
        xact_ref, dtraw_ref, bias_ref, alog_ref, dskip_ref = refs[:5]
        y_ref, sin_ref = refs[5 + nx:7 + nx]
        state, cs_s, cst_s, dt_s = refs[7 + 2 * nx:11 + 2 * nx]
        b, c = pl.program_id(0), pl.program_id(1)
        if nx:
            finish = comm.begin(refs[5:5 + nx], refs[7 + nx:7 + 2 * nx], refs[11 + 2 * nx:],
                                jnp.logical_and(b == 0, c == 0))

        @pl.when(c == 0)
        def _():
            state[...] = jnp.zeros_like(state)

        sin_ref[0] = state[...]
        dt = _softplus(dtraw_ref[...] + bias_ref[...])
        a = dt * (-jnp.exp(alog_ref[...]))
        cs = _dot_f32(_tri(True).astype(F32), a)
        cs_s[...] = cs
        cst_s[...] = cs.T
        dt_s[...] = dt
        causal = _tri(True)
        def front(g):
            heads = [g * SSD_K + k for k in range(SSD_K)]
            bg = xact_ref[:, pl.ds(D_INNER + g * SSD_N, SSD_N)]
            cg = xact_ref[:, pl.ds(D_INNER + (SSD_G + g) * SSD_N, SSD_N)]
            xg = xact_ref[:, pl.ds(g * GROUP_W, GROUP_W)]
            cols = [cs_s[:, pl.ds(h, 1)] for h in heads]
            lasts = [cs_s[pl.ds(SSD_Q - 1, 1), pl.ds(h, 1)] for h in heads]
            xdg = xg * _per_head([dt_s[:, pl.ds(h, 1)] for h in heads])
            sg = state[g]
            y = (_per_head([jnp.exp(c_) for c_ in cols]) * _dot_nt(cg, sg)
                 + _per_head([dskip_ref[:, pl.ds(h, 1)] for h in heads]) * xg)
            w = _per_head([jnp.exp(l_ - c_) for l_, c_ in zip(lasts, cols)])
            state[g] = _per_head_rows([jnp.exp(l_) for l_ in lasts]) * sg + _dot_tn(w * xdg, bg)
            return heads, cols, _dot_nt(cg, bg), xdg, y

        def back(g, heads, cols, gm, xdg, y):
            mats = [gm * jnp.exp(jnp.where(causal, cols[k] - cst_s[pl.ds(h, 1), :], NEG)) for k, h in enumerate(heads)]
            y4 = _dot_nn(jnp.concatenate(mats, axis=0), xdg)
            y_ref[:, pl.ds(g * GROUP_W, GROUP_W)] = y + _per_head([y4[k * SSD_Q:(k + 1) * SSD_Q] for k in range(SSD_K)])

        ahead = front(0)
        for g in range(SSD_G):
            cur, ahead = ahead, (front(g + 1) if g + 1 < SSD_G else None)
            back(g, *cur)
        if nx:
            pl.when(jnp.logical_and(b == n_seq - 1, c == nc - 1))(finish)

    vec = pl.BlockSpec((1, LANES), lambda b, c: (0, 0))
    return pl.pallas_call(
        body, name="ssd_fwd",
        grid=(n_seq, nc),
        in_specs=[pl.BlockSpec((SSD_Q, CONV_DIM), lambda b, c: (b * nc + c, 0)),
                  pl.BlockSpec((SSD_Q, LANES), lambda b, c: (b * nc + c, 0)), vec, vec, vec] + comm.specs,
        out_specs=[pl.BlockSpec((SSD_Q, D_INNER), lambda b, c: (b * nc + c, 0)),
                   pl.BlockSpec((1, SSD_G, GROUP_W, SSD_N), lambda b, c: (b * nc + c, 0, 0, 0))] + comm.specs,
        out_shape=[jax.ShapeDtypeStruct((rows, D_INNER), F32),
                   jax.ShapeDtypeStruct((n_seq * nc, SSD_G, GROUP_W, SSD_N), F32)] + comm.out_shape,
        scratch_shapes=[pltpu.VMEM((SSD_G, GROUP_W, SSD_N), F32), pltpu.VMEM((SSD_Q, LANES), F32),
                        pltpu.VMEM((LANES, SSD_Q), F32), pltpu.VMEM((SSD_Q, LANES), F32)] + comm.sems,
        compiler_params=_params(("arbitrary", "arbitrary")),
    )(xact, dtraw, dt_bias, a_log, d_skip, *comm.arrays)


def ssd_bwd(xact, dtraw, dt_bias, a_log, d_skip, sin, dy, n_seq, seq, comm=NO_EXCHANGE):
    nc = seq // SSD_Q
    rows = n_seq * seq
    nx = comm.n

    def body(*refs):
        xact_ref, dtraw_ref, bias_ref, alog_ref, dskip_ref, sin_ref, dy_ref = refs[:7]
        dx_ref, ddt_ref, dbias_ref, dalog_ref, ddskip_ref = refs[7 + nx:12 + nx]
        dstate, cs_s, cst_s, dt_s = refs[12 + 2 * nx:16 + 2 * nx]
        b, c = pl.program_id(0), pl.program_id(1)
        if nx:
            finish = comm.begin(refs[7:7 + nx], refs[12 + nx:12 + 2 * nx], refs[16 + 2 * nx:],
                                jnp.logical_and(b == 0, c == 0))

        @pl.when(c == 0)
        def _():
            dstate[...] = jnp.zeros_like(dstate)

        pre = dtraw_ref[...] + bias_ref[...]
        dt = _softplus(pre)
        a_neg = -jnp.exp(alog_ref[...])
        cs = _dot_f32(_tri(True).astype(F32), dt * a_neg)
        cs_s[...] = cs
        cst_s[...] = cs.T
        dt_s[...] = dt
        causal, anti = _tri(True), _tri(False)
        is_last_row = lax.broadcasted_iota(jnp.int32, (SSD_Q, 1), 0) == SSD_Q - 1
        dcs_cf = jnp.zeros((SSD_Q, LANES), F32)
        dcs_rf = jnp.zeros((LANES, SSD_Q), F32)
        ddt_cf = jnp.zeros((SSD_Q, LANES), F32)
        dd_vec = jnp.zeros((1, LANES), F32)
        dlast_vec = jnp.zeros((1, LANES), F32)
        def front(g):
            heads = [g * SSD_K + k for k in range(SSD_K)]
            v = {"heads": heads}
            bg = v["bg"] = xact_ref[:, pl.ds(D_INNER + g * SSD_N, SSD_N)]
            cg = v["cg"] = xact_ref[:, pl.ds(D_INNER + (SSD_G + g) * SSD_N, SSD_N)]
            xg = v["xg"] = xact_ref[:, pl.ds(g * GROUP_W, GROUP_W)]
            dyg = v["dyg"] = dy_ref[:, pl.ds(g * GROUP_W, GROUP_W)]
            cols = [cs_s[:, pl.ds(h, 1)] for h in heads]
            rws = [cst_s[pl.ds(h, 1), :] for h in heads]
            lasts = [cs_s[pl.ds(SSD_Q - 1, 1), pl.ds(h, 1)] for h in heads]
            e_lasts = v["e_lasts"] = [jnp.exp(l_) for l_ in lasts]
            v["dtg"] = _per_head([dt_s[:, pl.ds(h, 1)] for h in heads])
            v["dskg"] = _per_head([dskip_ref[:, pl.ds(h, 1)] for h in heads])
            e_col = _per_head([jnp.exp(c_) for c_ in cols])
            w = v["w"] = _per_head([jnp.exp(l_ - c_) for l_, c_ in zip(lasts, cols)])
            xdg = v["xdg"] = xg * v["dtg"]
            sg = sin_ref[0, g]
            dsn = dstate[g]
            v["gm"] = _dot_nt(cg, bg)
            gmt = _dot_nt(bg, cg)
            v["y_off"] = e_col * _dot_nt(cg, sg)
            d_cs = e_col * dyg
            v["dcg"] = _dot_nn(d_cs, sg)
            dstate[g] = _dot_tn(d_cs, cg) + _per_head_rows(e_lasts) * dsn
            v["dbg"] = _dot_nn(w * xdg, dsn)
            v["dtt"] = _dot_nt(bg, dsn)
            v["dsn_s"] = dsn * sg
            segs = [cols[k] - rws[k] for k in range(SSD_K)]
            v["decays"] = [jnp.exp(jnp.where(causal, s_, NEG)) for s_ in segs]
            v["dm4"] = _dot_nt(jnp.concatenate([_own_columns(dyg, k) for k in range(SSD_K)], axis=0), xdg)
            v["z4"] = _dot_nn(jnp.concatenate([gmt * jnp.exp(jnp.where(anti, -s_, NEG)) for s_ in segs], axis=0), dyg)
            return v

        def back(g, v, sums):
            dcs_cf, dcs_rf, ddt_cf, dd_vec, dlast_vec = sums
            dxd = v["w"] * v["dtt"] + _per_head([v["z4"][k * SSD_Q:(k + 1) * SSD_Q] for k in range(SSD_K)])
            dw = _headsum(v["dtt"] * v["xdg"] * v["w"], g)
            dcs_cf = dcs_cf + _headsum(v["dyg"] * v["y_off"], g) - dw
            dlast_vec = dlast_vec + jnp.sum(dw, axis=0, keepdims=True)
            dgm = jnp.zeros((SSD_Q, SSD_Q), F32)
            for k, h in enumerate(v["heads"]):
                dm = v["dm4"][k * SSD_Q:(k + 1) * SSD_Q]
                dseg = dm * v["gm"] * v["decays"][k]
                dgm = dgm + dm * v["decays"][k]
                oh_r = _onehot_row(h)
                dcs_cf = dcs_cf + jnp.sum(dseg, axis=1, keepdims=True) * oh_r
                dcs_rf = dcs_rf - _onehot_col(h) * jnp.sum(dseg, axis=0, keepdims=True)
                dlast_vec = dlast_vec + (jnp.sum(v["dsn_s"][k * SSD_P:(k + 1) * SSD_P], keepdims=True)
                                         * v["e_lasts"][k] * oh_r)
            dx_ref[:, pl.ds(g * GROUP_W, GROUP_W)] = dxd * v["dtg"] + v["dskg"] * v["dyg"]
            ddt_cf = ddt_cf + _headsum(dxd * v["xg"], g)
            dd_vec = dd_vec + _headsum(jnp.sum(v["dyg"] * v["xg"], axis=0, keepdims=True), g)
            dx_ref[:, pl.ds(D_INNER + g * SSD_N, SSD_N)] = v["dbg"] + _dot_tn(dgm, v["cg"])
            dx_ref[:, pl.ds(D_INNER + (SSD_G + g) * SSD_N, SSD_N)] = v["dcg"] + _dot_nn(dgm, v["bg"])
            return dcs_cf, dcs_rf, ddt_cf, dd_vec, dlast_vec

        sums = (dcs_cf, dcs_rf, ddt_cf, dd_vec, dlast_vec)
        ahead = front(0)
        for g in range(SSD_G):
            cur, ahead = ahead, (front(g + 1) if g + 1 < SSD_G else None)
            sums = back(g, cur, sums)
        dcs_cf, dcs_rf, ddt_cf, dd_vec, dlast_vec = sums
        dcs = dcs_cf + dcs_rf.T + jnp.where(is_last_row, dlast_vec, 0.0)
        da = _dot_f32(_tri(False).astype(F32), dcs)
        ddt = ddt_cf + da * a_neg
        ddtraw = ddt * _sigmoid(pre)
        ddt_ref[...] = ddtraw.astype(ddt_ref.dtype)
        dbias = jnp.sum(ddtraw, axis=0, keepdims=True)
        dalog = jnp.sum(da * dt, axis=0, keepdims=True) * a_neg
        first_step = jnp.logical_and(b == 0, c == 0)

        @pl.when(first_step)
        def _():
            dbias_ref[...] = dbias
            dalog_ref[...] = dalog
            ddskip_ref[...] = dd_vec

        @pl.when(jnp.logical_not(first_step))
        def _():
            dbias_ref[...] += dbias
            dalog_ref[...] += dalog
            ddskip_ref[...] += dd_vec

        if nx:
            pl.when(jnp.logical_and(b == n_seq - 1, c == nc - 1))(finish)

    def rowblk(b, c):
        return b * nc + (nc - 1 - c)

    vec = pl.BlockSpec((1, LANES), lambda b, c: (0, 0))
    return pl.pallas_call(
        body, name="ssd_bwd",
        grid=(n_seq, nc),
        in_specs=[pl.BlockSpec((SSD_Q, CONV_DIM), lambda b, c: (rowblk(b, c), 0)),
                  pl.BlockSpec((SSD_Q, LANES), lambda b, c: (rowblk(b, c), 0)), vec, vec, vec,
                  pl.BlockSpec((1, SSD_G, GROUP_W, SSD_N), lambda b, c: (rowblk(b, c), 0, 0, 0)),
                  pl.BlockSpec((SSD_Q, D_INNER), lambda b, c: (rowblk(b, c), 0))] + comm.specs,
        out_specs=[pl.BlockSpec((SSD_Q, CONV_DIM), lambda b, c: (rowblk(b, c), 0)),
                   pl.BlockSpec((SSD_Q, LANES), lambda b, c: (rowblk(b, c), 0)), vec, vec, vec] + comm.specs,
        out_shape=[jax.ShapeDtypeStruct((rows, CONV_DIM), F32), jax.ShapeDtypeStruct((rows, LANES), BF16),
                   jax.ShapeDtypeStruct((1, LANES), F32), jax.ShapeDtypeStruct((1, LANES), F32),
                   jax.ShapeDtypeStruct((1, LANES), F32)] + comm.out_shape,
        scratch_shapes=[pltpu.VMEM((SSD_G, GROUP_W, SSD_N), F32), pltpu.VMEM((SSD_Q, LANES), F32),
                        pltpu.VMEM((LANES, SSD_Q), F32), pltpu.VMEM((SSD_Q, LANES), F32)] + comm.sems,
        compiler_params=_params(("arbitrary", "arbitrary")),
    )(xact, dtraw, dt_bias, a_log, d_skip, sin, dy, *comm.arrays)


QKV_W = 3 * ATT_OUT
PAIR_W = 2 * ATT_HD
HEAD_PAIRS = ATT_H // 2
PREP_ROWS = 512


def _by_residue(a, n_seq, seq, dil):
    if dil == 1:
        return a
    return a.reshape(n_seq, seq // dil, dil, a.shape[1]).transpose(0, 2, 1, 3).reshape(a.shape)


def _by_token(a, n_seq, seq, dil):
    if dil == 1:
        return a
    return a.reshape(n_seq, dil, seq // dil, a.shape[1]).transpose(0, 2, 1, 3).reshape(a.shape)


def _head_sums(x, fn):
    lo = jnp.logical_not(lax.broadcasted_iota(jnp.int32, (1, 2 * ATT_HD), 1) >= ATT_HD)
    parts = []
    for p in range(ATT_H // 2):
        slab = x[:, p * 2 * ATT_HD:(p + 1) * 2 * ATT_HD]
        s_lo = fn(jnp.sum(jnp.where(lo, slab, 0.0), axis=1, keepdims=True))
        s_hi = fn(jnp.sum(jnp.where(lo, 0.0, slab), axis=1, keepdims=True))
        parts.append(jnp.where(lo, s_lo, s_hi))
    return jnp.concatenate(parts, axis=1)


def _head_rstd(x):
    return _head_sums(x * x, lambda s: lax.rsqrt(s * (1.0 / ATT_HD) + EPS))


def _head_rms_bwd(x, g_t, dy):
    r = _head_rstd(x)
    xh = x * r
    dyg = dy * g_t
    mean = _head_sums(dyg * xh, lambda s: s * (1.0 / ATT_HD))
    return r * (dyg - xh * mean), jnp.sum(dy * xh, axis=0, keepdims=True)


def _lane_hi():
    return lax.broadcasted_iota(jnp.int32, (1, PAIR_W), 1) >= ATT_HD


def _band_mask2(first_valid, query_rows):
    i = lax.broadcasted_iota(jnp.int32, (ATT_BLK, 2 * ATT_BLK), 0)
    j = lax.broadcasted_iota(jnp.int32, (ATT_BLK, 2 * ATT_BLK), 1)
    left = j < ATT_BLK
    right = jnp.logical_not(left)
    if query_rows:
        return jnp.logical_or(jnp.logical_and(jnp.logical_and(left, i <= j), first_valid),
                              jnp.logical_and(right, i >= j - ATT_BLK))
    return jnp.logical_or(jnp.logical_and(left, j >= i),
                          jnp.logical_and(jnp.logical_and(right, j - ATT_BLK <= i), first_valid))


def _only_head(slab, hi):
    keep = _lane_hi() if hi else jnp.logical_not(_lane_hi())
    return jnp.where(keep, slab, jnp.zeros_like(slab))


def attn_fwd(nq, n_seq, seq, dil, name):
    nb = seq // dil // ATT_BLK
    rows = n_seq * seq

    def body(cur_ref, prev_ref, o_ref, lse_ref, s_scr, p_scr):
        n = pl.program_id(1)
        mask = _band_mask2(n > 0, True)
        for h in range(ATT_H):
            sl = pl.ds((h // 2) * PAIR_W, PAIR_W)
            ks = pl.ds(ATT_OUT + (h // 2) * PAIR_W, PAIR_W)
            kcat = jnp.concatenate([prev_ref[:, ks], cur_ref[:, ks]], axis=0)
            s_scr[h] = jnp.where(mask, _dot_nt(_only_head(cur_ref[:, sl], h % 2), kcat), NEG)
        s_all = s_scr[...]
        mx = jnp.max(s_all, axis=2, keepdims=True)
        p_all = jnp.exp(s_all - mx)
        den = jnp.sum(p_all, axis=2, keepdims=True)
        p_scr[...] = p_all.astype(p_scr.dtype)
        inv = 1.0 / den
        lse = mx + jnp.log(den)
        lse_blk = jnp.zeros((ATT_BLK, LANES), F32)
        for h in range(ATT_H):
            lse_blk = lse_blk + lse[h] * _onehot_row(h)
        lse_ref[...] = lse_blk
        for pr in range(HEAD_PAIRS):
            vs = pl.ds(2 * ATT_OUT + pr * PAIR_W, PAIR_W)
            vcat = jnp.concatenate([prev_ref[:, vs], cur_ref[:, vs]], axis=0)
            lo = _dot_nn(p_scr[2 * pr], vcat) * inv[2 * pr]
            hi = _dot_nn(p_scr[2 * pr + 1], vcat) * inv[2 * pr + 1]
            o_ref[:, pl.ds(pr * PAIR_W, PAIR_W)] = jnp.where(_lane_hi(), hi, lo)

    def blk(width, shift):
        if shift:
            return pl.BlockSpec((ATT_BLK, width), lambda s, n: (s * nb + jnp.maximum(n - 1, 0), 0))
        return pl.BlockSpec((ATT_BLK, width), lambda s, n: (s * nb + n, 0))

    return pl.pallas_call(
        body, name=name,
        grid=(n_seq * dil, nb),
        in_specs=[blk(QKV_W, 0), blk(QKV_W, -1)],
        out_specs=[blk(ATT_OUT, 0), blk(LANES, 0)],
        out_shape=[jax.ShapeDtypeStruct((rows, ATT_OUT), F32), jax.ShapeDtypeStruct((rows, LANES), F32)],
        scratch_shapes=[pltpu.VMEM((ATT_H, ATT_BLK, 2 * ATT_BLK), F32), pltpu.VMEM((ATT_H, ATT_BLK, 2 * ATT_BLK), MXU)],
        compiler_params=_params(("parallel", "arbitrary")),
    )(nq, nq)


def attn_bwd(nq, do, lse, wts, rsum, n_seq, seq, dil, name):
    nb = seq // dil // ATT_BLK

    def body(prev_ref, cur_ref, nxt_ref, do_c, do_x, lse_c, lse_x, wt_c, wt_x, rs_c, rs_x, dn_ref):
        n = pl.program_id(1)
        mask_q = _band_mask2(n > 0, True)
        mask_k = _band_mask2(n < nb - 1, False)
        wc, wx = wt_c[...], wt_x[...]
        lse_t = jnp.concatenate([lse_c[...].T, lse_x[...].T], axis=1)
        dl_t = jnp.concatenate([(-wc * rs_c[...]).T, (-wx * rs_x[...]).T], axis=1)
        def operands(pr):
            sl = pl.ds(pr * PAIR_W, PAIR_W)
            ks = pl.ds(ATT_OUT + pr * PAIR_W, PAIR_W)
            vs = pl.ds(2 * ATT_OUT + pr * PAIR_W, PAIR_W)
            he, ho = pl.ds(2 * pr, 1), pl.ds(2 * pr + 1, 1)
            q_c, k_c, v_c = cur_ref[:, sl], cur_ref[:, ks], cur_ref[:, vs]
            dog_c = do_c[:, sl] * jnp.where(_lane_hi(), wt_c[:, ho], wt_c[:, he])
            dog_x = do_x[:, sl] * jnp.where(_lane_hi(), wt_x[:, ho], wt_x[:, he])
            return dict(q_c=q_c, k_c=k_c, v_c=v_c, qcat=jnp.concatenate([q_c, nxt_ref[:, sl]], axis=0),
                        kcat=jnp.concatenate([prev_ref[:, ks], k_c], axis=0),
                        vcat=jnp.concatenate([prev_ref[:, vs], v_c], axis=0),
                        dog=jnp.concatenate([dog_c, dog_x], axis=0).astype(MXU))

        def scores(o, h):
            hi, one = h % 2, pl.ds(h, 1)
            dl_col = -wt_c[:, one] * rs_c[:, one]
            p_q = jnp.exp(jnp.where(mask_q, _dot_nt(_only_head(o["q_c"], hi), o["kcat"]) - lse_c[:, one], NEG))
            ds_q = p_q * (_dot_nt(_only_head(o["dog"][:ATT_BLK], hi), o["vcat"]) + dl_col)
            p_t = jnp.exp(jnp.where(mask_k, _dot_nt(_only_head(o["k_c"], hi), o["qcat"]) - lse_t[h:h + 1, :], NEG))
            ds_t = p_t * (_dot_nt(_only_head(o["v_c"], hi), o["dog"]) + dl_t[h:h + 1, :])
            return ds_q, ds_t, p_t

        ops = [operands(pr) for pr in range(HEAD_PAIRS)]
        ahead = scores(ops[0], 0)
        res = []
        for h in range(ATT_H):
            o = ops[h // 2]
            (ds_q, ds_t, p_t), ahead = ahead, (scores(ops[(h + 1) // 2], h + 1) if h + 1 < ATT_H else None)
            res.append((_dot_nn(ds_q, o["kcat"]), _dot_nn(ds_t, o["qcat"]), _dot_nn(p_t, o["dog"])))
            if h % 2:
                for t, first in enumerate((0, ATT_OUT, 2 * ATT_OUT)):
                    dn_ref[:, pl.ds(first + (h // 2) * PAIR_W, PAIR_W)] = jnp.where(_lane_hi(), res[h][t], res[h - 1][t])

    def at(shift, width):
        if shift < 0:
            return pl.BlockSpec((ATT_BLK, width), lambda s, n: (s * nb + jnp.maximum(n - 1, 0), 0))
        if shift > 0:
            return pl.BlockSpec((ATT_BLK, width), lambda s, n: (s * nb + jnp.minimum(n + 1, nb - 1), 0))
        return pl.BlockSpec((ATT_BLK, width), lambda s, n: (s * nb + n, 0))

    return pl.pallas_call(
        body, name=name,
        grid=(n_seq * dil, nb),
        in_specs=[at(-1, QKV_W), at(0, QKV_W), at(1, QKV_W), at(0, ATT_OUT), at(1, ATT_OUT),
                  at(0, LANES), at(1, LANES), at(0, LANES), at(1, LANES), at(0, LANES), at(1, LANES)],
        out_specs=at(0, QKV_W),
        out_shape=jax.ShapeDtypeStruct((n_seq * seq, QKV_W), F32),
        compiler_params=_params(("parallel", "arbitrary")),
    )(nq, nq, nq, do, do, lse, lse, wts, wts, rsum, rsum)


def qk_post(qkv, dn, gq_t, gk_t, rows, name):
    tb = min(PREP_ROWS, rows)

    def body(x_ref, dn_ref, gq_ref, gk_ref, o_ref, dgq_ref, dgk_ref):
        i = pl.program_id(0)
        qs, ks, vs = pl.ds(0, ATT_OUT), pl.ds(ATT_OUT, ATT_OUT), pl.ds(2 * ATT_OUT, ATT_OUT)
        dq, dgq = _head_rms_bwd(x_ref[:, qs], gq_ref[...], dn_ref[:, qs] * ATT_SCALE)
        dk, dgk = _head_rms_bwd(x_ref[:, ks], gk_ref[...], dn_ref[:, ks])
        o_ref[:, qs] = dq.astype(o_ref.dtype)
        o_ref[:, ks] = dk.astype(o_ref.dtype)
        o_ref[:, vs] = dn_ref[:, vs].astype(o_ref.dtype)

        @pl.when(i == 0)
        def _():
            dgq_ref[...] = dgq
            dgk_ref[...] = dgk

        @pl.when(i != 0)
        def _():
            dgq_ref[...] += dgq
            dgk_ref[...] += dgk

    gspec = pl.BlockSpec((1, ATT_OUT), lambda i: (0, 0))
    blk = pl.BlockSpec((tb, QKV_W), lambda i: (i, 0))
    return pl.pallas_call(
        body, name=name,
        grid=(rows // tb,),
        in_specs=[blk, blk, gspec, gspec],
        out_specs=[blk, gspec, gspec],
        out_shape=[jax.ShapeDtypeStruct((rows, QKV_W), MXU), jax.ShapeDtypeStruct((1, ATT_OUT), F32),
                   jax.ShapeDtypeStruct((1, ATT_OUT), F32)],
        compiler_params=_params(("arbitrary",)),
    )(qkv, dn, gq_t, gk_t)


def all_gather(arrays, name):
    n = len(arrays)

    def body(*refs):
        in_refs, out_refs = refs[:n], refs[n:2 * n]
        send_sems, recv_sems, local_sems = refs[2 * n:]
        _, me, _ = _peer(0)
        sib, _, _ = _peer(1)

        def first(t, k, arriving):
            dev, pid, _ = _peer(k)
            return _remote(in_refs[t], out_refs[t].at[pid if arriving else me], send_sems, recv_sems, t, k, dev)

        def passed(t, k, arriving):
            slot = out_refs[t].at[_peer(k + 1 if arriving else k)[1]]
            return _remote(slot, slot, send_sems, recv_sems, t, k + 1, sib)

        def own(t):
            return pltpu.make_async_copy(in_refs[t], out_refs[t].at[me], local_sems.at[t])

        for t in range(n):
            own(t).start()
            for k in (1,) + OTHER_CHIPS:
                first(t, k, False).start()
        for t in range(n):
            for k in OTHER_CHIPS:
                first(t, k, True).wait_recv()
                passed(t, k, False).start()
        for t in range(n):
            first(t, 1, True).wait_recv()
            for k in OTHER_CHIPS:
                passed(t, k, True).wait_recv()
        for t in range(n):
            for k in (1,) + OTHER_CHIPS:
                first(t, k, False).wait_send()
            for k in OTHER_CHIPS:
                passed(t, k, False).wait_send()
            own(t).wait()

    anyspec = pl.BlockSpec(memory_space=pl.ANY)
    return pl.pallas_call(
        body, name=name,
        in_specs=[anyspec] * n,
        out_specs=[anyspec] * n,
        out_shape=[jax.ShapeDtypeStruct((N_DEV,) + tuple(a.shape), a.dtype) for a in arrays],
        scratch_shapes=[pltpu.SemaphoreType.DMA((n, N_DEV)), pltpu.SemaphoreType.DMA((n, N_DEV)),
                        pltpu.SemaphoreType.DMA((n,))],
    )(*arrays)


def pair_add(a, b, name):
    _, r, c = a.shape
    rb = r if r <= 512 else (128 if c > 1024 else 256)
    assert r % rb == 0

    def body(a_ref, b_ref, o_ref):
        o_ref[...] = (a_ref[...].astype(F32) + b_ref[...].astype(F32)).astype(o_ref.dtype)

    blk = pl.BlockSpec((1, rb, c), lambda q, i: (q, i, 0))
    return pl.pallas_call(
        body, name=name,
        grid=(N_CHIPS, r // rb),
        in_specs=[blk, blk],
        out_specs=blk,
        out_shape=jax.ShapeDtypeStruct(a.shape, a.dtype),
        compiler_params=_params(("parallel", "parallel")),
    )(a, b)


def adamw(parts, w, m, v, name):
    r, c = w.shape[-2:]
    n_parts = parts.shape[0]
    rb = r if r <= 512 else (128 if c > 1024 else 256)
    assert r % rb == 0

    def body(p_ref, w_ref, m_ref, v_ref, g_out, d_out, m_out, v_out):
        g = p_ref[0].astype(F32)
        for i in range(1, n_parts):
            g = g + p_ref[i].astype(F32)
        m_new = ADAM_B1 * m_ref[...] + (1.0 - ADAM_B1) * g
        v_new = ADAM_B2 * v_ref[...] + (1.0 - ADAM_B2) * (g * g)
        m_hat = m_new / (1.0 - ADAM_B1 ** ADAM_STEP)
        v_hat = v_new / (1.0 - ADAM_B2 ** ADAM_STEP)
        g_out[...] = g
        d_out[...] = -ADAM_LR * (m_hat / (jnp.sqrt(v_hat) + ADAM_EPS) + ADAM_WD * w_ref[...])
        m_out[...] = m_new
        v_out[...] = v_new

    if w.ndim == 3:
        blk = pl.BlockSpec((None, rb, c), lambda i: (0, i, 0))
    else:
        blk = pl.BlockSpec((rb, c), lambda i: (i, 0))
    return pl.pallas_call(
        body, name=name,
        grid=(r // rb,),
        in_specs=[pl.BlockSpec((n_parts, rb, c), lambda i: (0, i, 0)), blk, blk, blk],
        out_specs=[blk] * 4,
        out_shape=[jax.ShapeDtypeStruct(w.shape, F32)] * 4,
        compiler_params=_params(("parallel",)),
    )(parts, w, m, v)


def _pad_lanes(vec, n=LANES):
    return jnp.pad(vec, ((0, 0), (0, n - vec.shape[1])))


COL_SHARDED = ("w_in", "ssd_conv_w", "w_attn_proj", "w_up", "ffn_conv_w")
MATRICES = ("w_in", "w_attn_proj", "w_up", "w_ssd_proj", "w_out", "w_down")
LATE = ("w_ssd_proj", "w_attn_proj", "w_out", "w_up", "ffn_conv_w", "w_down")


def _narrow(name, a):
    return a.astype(MXU) if name in MATRICES else a


def _from_gathered(name, g):
    if name in COL_SHARDED:
        return jnp.transpose(g, (1, 0, 2)).reshape(g.shape[1], N_DEV * g.shape[2])
    return g.reshape(N_DEV * g.shape[1], g.shape[2])


def _to_slabs(name, g):
    if name in COL_SHARDED:
        return jnp.transpose(g.reshape(g.shape[0], N_DEV, g.shape[1] // N_DEV), (1, 0, 2))
    return g.reshape(N_DEV, g.shape[0] // N_DEV, g.shape[1])


def _columns(m, a, b):
    if m.ndim == 2:
        return m[:, a:b]
    c = m.shape[2]
    cuts = [m[j][:, max(a - j * c, 0):min(b - j * c, c)] for j in range(a // c, (b - 1) // c + 1)]
    return cuts[0] if len(cuts) == 1 else jnp.concatenate(cuts, axis=1)


def _column_shards(pieces, c):
    shards = []
    for j in range(N_DEV):
        cuts = []
        for start, arr in pieces:
            lo, hi = max(j * c - start, 0), min((j + 1) * c - start, arr.shape[1])
            if lo < hi:
                cuts.append(arr[:, lo:hi])
        shards.append(cuts[0] if len(cuts) == 1 else jnp.concatenate(cuts, axis=1))
    return jnp.stack(shards)


def local_step(x, target, w, late=None):
    n_seq, seq, _ = x.shape
    rows = n_seq * seq
    x = x.reshape(rows, D_MODEL)
    target = target.reshape(rows, D_MODEL)
    mx = lambda a: a.astype(MXU)

    splits = [sum(IN_WIDTHS[:i]) for i in range(len(IN_WIDTHS) + 1)]
    w_in = w["w_in"]
    part = lambda i: _columns(w_in, splits[i], splits[i + 1])
    w_z, w_xbc, w_gs, w_ga = mx(part(0)), mx(part(1)), mx(part(6)), mx(part(7))
    w_dt = mx(_pad_lanes(part(2)))
    head_group = lambda t, g: (splits[3 + t] + g * ATT_OUT, splits[3 + t] + (g + 1) * ATT_OUT)
    w_qkv = [mx(jnp.concatenate([_columns(w_in, *head_group(t, g)) for t in range(3)], axis=1))
             for g in range(ATT_GROUPS)]
    conv_w, conv_b, fconv_b = w["ssd_conv_w"], w["ssd_conv_b"], w["ffn_conv_b"]
    dt_bias, a_log, d_skip = _pad_lanes(w["dt_bias"]), _pad_lanes(w["a_log"]), _pad_lanes(w["d_skip"])
    g1, g2, gn, gq, gk = w["norm1_g"], w["norm2_g"], w["ssd_norm_g"], w["q_norm_g"], w["k_norm_g"]

    tb = min(512, seq)
    tbm = min(256, seq)
    cw = 1024
    rw = lambda fn, name, ncol, ins, params=(), outs=(), accs=(), tb_=tb: rowwise(
        fn, name, rows, seq, tb_, ncol, ins, params, outs, accs)

    z, h = matmul(x, w_z, "mm_z", head=_Head(_rms_fwd, rows=[g1], out=MXU))
    xbc = matmul(h, w_xbc, "mm_xbc")
    dtraw = matmul(h, w_dt, "mm_dt")
    by_residue = lambda a, g: _by_residue(a, n_seq, seq, ATT_DILATIONS[g])
    by_token = lambda a, g: _by_token(a, n_seq, seq, ATT_DILATIONS[g])
    h_res = [by_residue(h, g) for g in range(ATT_GROUPS)]
    gq_t, gk_t = jnp.tile(gq, (1, ATT_H)), jnp.tile(gk, (1, ATT_H))
    qkv_gains = jnp.concatenate([gq_t * ATT_SCALE, gk_t, jnp.ones_like(gk_t)], axis=1)

    def qk_norm(r, gains):
        q, k = r[:, :ATT_OUT], r[:, ATT_OUT:2 * ATT_OUT]
        return r, jnp.concatenate([q * _head_rstd(q), k * _head_rstd(k), r[:, 2 * ATT_OUT:]], axis=1) * gains

    qkv, nq = zip(*[matmul(h_res[g], w_qkv[g], f"mm_qkv{g}", tail=_Tail(qk_norm, rows=[qkv_gains], outs=[F32, MXU]))
                    for g in range(ATT_GROUPS)])
    gs = matmul(h, w_gs, "mm_gs")
    ga = matmul(h, w_ga, "mm_ga")

    def conv_silu(ctx, xh, wv, bv):
        pre = bv + _conv_prev(xh[0], xh[1], wv, ctx.first, SSD_CONV)
        return _silu(pre), pre

    xact, xpre = rw(conv_silu, "ssd_conv_fwd", CONV_DIM // cw, [(xbc, cw, 0, "prev")],
                    [(conv_w, cw, 0), (conv_b, cw, 0)], [(CONV_DIM, cw, 0, F32), (CONV_DIM, cw, 0, F32)])
    if late is None:
        y, sin = ssd_fwd(xact, dtraw, dt_bias, a_log, d_skip, n_seq, seq)
    else:
        y, sin, *gathered = ssd_fwd(xact, dtraw, dt_bias, a_log, d_skip, n_seq, seq,
                                    comm=direct_exchange([(late[n], "gather") for n in LATE]))
        w = {**w, **{n: g if n == "w_up" else _from_gathered(n, g) for n, g in zip(LATE, gathered)}}
    w_sp, w_ap, w_o, w_d = mx(w["w_ssd_proj"]), mx(w["w_attn_proj"]), mx(w["w_out"]), mx(w["w_down"])
    w_ug, w_uv = mx(_columns(w["w_up"], 0, D_FF)), mx(_columns(w["w_up"], D_FF, 2 * D_FF))
    fconv_w = w["ffn_conv_w"]

    def gated_norm(yv, zv, g):
        yz = yv * _silu(zv)
        return jnp.concatenate([_rms_fwd(yz[:, i:i + NORM_GROUP], g[:, i:i + NORM_GROUP])
                                for i in range(0, D_INNER, NORM_GROUP)], axis=1)

    att = [attn_fwd(nq[g], n_seq, seq, ATT_DILATIONS[g], f"attn_fwd{g}") for g in range(ATT_GROUPS)]

    def combine(ctx, o0, o1, o2, l0, l1, l2):
        mxl = jnp.maximum(jnp.maximum(l0, l1), l2)
        e = [jnp.exp(l - mxl) for l in (l0, l1, l2)]
        inv = 1.0 / (e[0] + e[1] + e[2])
        ws = [ei * inv for ei in e]
        out = sum(_expand_heads(wi) * oi for wi, oi in zip(ws, (o0, o1, o2)))
        return (out, *ws)

    y_attn, wt0, wt1, wt2 = rw(
        combine, "attn_combine", 1,
        [(by_token(att[g][0], g), ATT_OUT, 0, None) for g in range(3)]
        + [(by_token(att[g][1], g), LANES, 0, None) for g in range(3)], [],
        [(ATT_OUT, ATT_OUT, 0, F32)] + [(LANES, LANES, 0, F32)] * 3)
    wts = (wt0, wt1, wt2)

    ps, y_ssd = matmul(y, w_sp, "mm_ssd_proj", head=_Head(gated_norm, like=[z], rows=[gn], out=MXU))
    pa, merged = matmul(y_attn, w_ap, "mm_attn_proj",
                        tail=_Tail(lambda r, a, c, d: (r, _sigmoid(c) * a + _sigmoid(d) * r), like=[ps, gs, ga],
                                   outs=[F32, MXU]))
    x1, h2 = matmul(merged, w_o, "mm_out", add=x,
                    tail=_Tail(lambda r, g: (r, _rms_fwd(r, g)), rows=[g2], outs=[F32, MXU]))
    up_g = matmul(h2, w_ug, "mm_up_g")
    up_v = matmul(h2, w_uv, "mm_up_v")
    fw = D_FF // 2
    nfc = D_FF // fw

    def mlp_act(ctx, ug, uv, wg, wv, bg, bv):
        cg = bg + _conv_prev(ug[0], ug[1], wg, ctx.first, FFN_CONV)
        cv = bv + _conv_prev(uv[0], uv[1], wv, ctx.first, FFN_CONV)
        return _silu(cg) * cv, cg, cv

    act, conv_g, conv_v = rw(mlp_act, "mlp_act_fwd", nfc, [(up_g, fw, 0, "prev"), (up_v, fw, 0, "prev")],
                             [(fconv_w, fw, 0), (fconv_w, fw, nfc), (fconv_b, fw, 0), (fconv_b, fw, nfc)],
                             [(D_FF, fw, 0, MXU), (D_FF, fw, 0, F32), (D_FF, fw, 0, F32)], tb_=tbm)
    def loss_tail(out, tv):
        d = out - tv
        g = d * (1.0 / D_MODEL)
        return g, g, jnp.sum(d * d, axis=0, keepdims=True)

    dx2, dx2_m, sq = matmul(act, w_d, "mm_down", add=x1,
                            tail=_Tail(loss_tail, like=[target], outs=[F32, MXU], n_sums=1))

    grads = {}
    dact = matmul(dx2_m, w_d, "mm_d_act", tb=True)
    grads["w_down"] = matmul(act, dx2_m, "mm_dw_down", ta=True, out_dtype=MXU)

    def mlp_bwd(ctx, da, cg, cv, ug, uv, wg, wv):
        (da_c, da_n), (cg_c, cg_n), (cv_c, cv_n) = da, cg, cv
        dup_g_, dwg, dbg = _conv_bwd(da_c * cv_c * _silu_grad(cg_c), da_n * cv_n * _silu_grad(cg_n), ug, wg, ctx,
                                     FFN_CONV)
        dup_v_, dwv, dbv = _conv_bwd(da_c * _silu(cg_c), da_n * _silu(cg_n), uv, wv, ctx, FFN_CONV)
        return dup_g_, dup_v_, dwg, dbg, dwv, dbv

    dup_g, dup_v, dfw_g, dfb_g, dfw_v, dfb_v = rw(
        mlp_bwd, "mlp_bwd", nfc,
        [(dact, fw, 0, "next"), (conv_g, fw, 0, "next"), (conv_v, fw, 0, "next"), (up_g, fw, 0, None),
         (up_v, fw, 0, None)], [(fconv_w, fw, 0), (fconv_w, fw, nfc)],
        [(D_FF, fw, 0, MXU), (D_FF, fw, 0, MXU)], [(FFN_CONV, fw), (1, fw), (FFN_CONV, fw), (1, fw)], tb_=tbm)
    grads["ffn_conv_w"] = jnp.concatenate([dfw_g, dfw_v], axis=1)
    grads["ffn_conv_b"] = jnp.concatenate([dfb_g, dfb_v], axis=1)
    def rms_bwd_fn(dh_, xv, dres, g):
        dxv, dg = _rms_bwd(xv, g, dh_)
        return dres + dxv, dg

    def rms_bwd_fn2(dh_, xv, dres, g):
        dxv, dg = rms_bwd_fn(dh_, xv, dres, g)
        return dxv, dxv, dg

    dh2 = matmul(dup_g, w_ug, "mm_dh2_g", tb=True)
    dx1, dx1_m, grads["norm2_g"] = matmul(
        dup_v, w_uv, "mm_dh2_v", tb=True, add=dh2,
        tail=_Tail(rms_bwd_fn2, like=[x1, dx2], rows=[g2], outs=[F32, MXU], n_sums=1))
    dw_up =[(0, matmul(h2, dup_g, "mm_dw_up_g", ta=True, out_dtype=MXU)),
             (D_FF, matmul(h2, dup_v, "mm_dw_up_v", ta=True, out_dtype=MXU))]
    if w["w_up"].ndim == 3:
        grads["w_up"] = _column_shards(dw_up, w["w_up"].shape[2])
    else:
        grads["w_up"] = jnp.concatenate([p for _, p in dw_up], axis=1)

    def merge_bwd(dm, a, b, c, d):
        sc, sd = _sigmoid(c), _sigmoid(d)
        return dm * sc, dm * sd, dm * a * sc * (1.0 - sc), dm * b * sd * (1.0 - sd)

    dps, dpa, dgs, dga = matmul(dx1_m, w_o, "mm_d_merged", tb=True,
                                tail=_Tail(merge_bwd, like=[ps, pa, gs, ga], outs=[MXU] * 4))
    grads["w_out"] = matmul(merged, dx1_m, "mm_dw_out", ta=True, out_dtype=MXU)

    def gated_norm_bwd(dyn, yv, zv, g):
        sz = _silu(zv)
        yz = yv * sz
        dyz, dgs_ = [], []
        for i in range(0, D_INNER, NORM_GROUP):
            a, b = _rms_bwd(yz[:, i:i + NORM_GROUP], g[:, i:i + NORM_GROUP], dyn[:, i:i + NORM_GROUP])
            dyz.append(a)
            dgs_.append(b)
        dyz = jnp.concatenate(dyz, axis=1)
        return dyz * sz, dyz * yv * _silu_grad(zv), jnp.concatenate(dgs_, axis=1)

    dy, dz, grads["ssd_norm_g"] = matmul(dps, w_sp, "mm_d_y_ssd", tb=True,
                                         tail=_Tail(gated_norm_bwd, like=[y, z], rows=[gn], outs=[F32, MXU], n_sums=1))
    grads["w_ssd_proj"] = matmul(y_ssd, dps, "mm_dw_ssd_proj", ta=True, out_dtype=MXU)
    dy_attn = matmul(dpa, w_ap, "mm_d_y_attn", tb=True)
    grads["w_attn_proj"] = matmul(y_attn, dpa, "mm_dw_attn_proj", ta=True, out_dtype=MXU)

    (rsum,) = rw(lambda ctx, a, b: _reduce_heads(a * b), "attn_rsum", 1,
                 [(dy_attn, ATT_OUT, 0, None), (y_attn, ATT_OUT, 0, None)], [], [(LANES, LANES, 0, F32)])
    dqkv, dgq, dgk = [], 0.0, 0.0
    for g in range(ATT_GROUPS):
        dn = attn_bwd(nq[g], by_residue(dy_attn, g), att[g][1], by_residue(wts[g], g), by_residue(rsum, g), n_seq, seq,
                      ATT_DILATIONS[g], f"attn_bwd{g}")
        d_, a_, b_ = qk_post(qkv[g], dn, gq_t, gk_t, rows, f"qk_post{g}")
        dqkv.append(d_)
        dgq, dgk = dgq + a_, dgk + b_
    per_head = lambda v: jnp.sum(v.reshape(ATT_H, ATT_HD), axis=0, keepdims=True)
    grads["q_norm_g"], grads["k_norm_g"] = per_head(dgq), per_head(dgk)

    if late is None:
        dxact, ddt, dbias, dalog, ddskip = ssd_bwd(xact, dtraw, dt_bias, a_log, d_skip, sin, dy, n_seq, seq)
    else:
        dxact, ddt, dbias, dalog, ddskip, *parts = ssd_bwd(
            xact, dtraw, dt_bias, a_log, d_skip, sin, dy, n_seq, seq,
            comm=direct_exchange([(grads[n] if n == "w_up" else _to_slabs(n, _narrow(n, grads[n])), "scatter")
                                  for n in LATE]))
        grads.update(zip(LATE, parts))
    grads["dt_bias"], grads["a_log"], grads["d_skip"] = dbias[:, :SSD_H], dalog[:, :SSD_H], ddskip[:, :SSD_H]

    def conv_silu_bwd(ctx, dxa, pre, xin, wv):
        return _conv_bwd(dxa[0] * _silu_grad(pre[0]), dxa[1] * _silu_grad(pre[1]), xin, wv, ctx, SSD_CONV)

    dxbc, grads["ssd_conv_w"], grads["ssd_conv_b"] = rw(
        conv_silu_bwd, "ssd_conv_bwd", CONV_DIM // cw,
        [(dxact, cw, 0, "next"), (xpre, cw, 0, "next"), (xbc, cw, 0, None)],
        [(conv_w, cw, 0)], [(CONV_DIM, cw, 0, MXU)], [(SSD_CONV, cw), (1, cw)])

    pieces = [(d_, d_, h, w_, tag) for d_, w_, tag in
              ((dz, w_z, "z"), (ddt, w_dt, "dt"), (dgs, w_gs, "gs"), (dga, w_ga, "ga"))]
    pieces += [(by_token(dqkv[g], g), dqkv[g], h_res[g], w_qkv[g], f"qkv{g}") for g in range(ATT_GROUPS)]
    pieces += [(dxbc, dxbc, h, w_xbc, "xbc")]
    dws = {tag: matmul(h_in, dpart_h, f"mm_dw_{tag}", ta=True, out_dtype=MXU) for _, dpart_h, h_in, _, tag in pieces}
    dw_in = [(splits[0], dws["z"]), (splits[1], dws["xbc"]), (splits[2], dws["dt"][:, :SSD_H])]
    dw_in += [(head_group(t, g)[0], dws[f"qkv{g}"][:, t * ATT_OUT:(t + 1) * ATT_OUT])
              for t in range(3) for g in range(ATT_GROUPS)]
    dw_in += [(splits[6], dws["gs"]), (splits[7], dws["ga"])]
    if w_in.ndim == 3:
        grads["w_in"] = _column_shards(dw_in, w_in.shape[2])
    else:
        grads["w_in"] = jnp.concatenate([p for _, p in dw_in], axis=1)
    dh = None
    for idx, (dpart, _, _, wpart, tag) in enumerate(pieces):
        comm, tail = NO_EXCHANGE, None
        if late is not None and idx == 0:
            slabs = [grads[n] if n == "w_in" else _to_slabs(n, _narrow(n, grads[n])) for n in EARLY]
            comm = sibling_exchange(slabs)
        if idx == len(pieces) - 1:
            tail = _Tail(rms_bwd_fn, like=[x, dx1], rows=[g1], outs=[F32], n_sums=1)
            if late is not None:
                comm = chip_exchange(summed)
        dh = matmul(dpart, wpart, f"mm_dh_{tag}", tb=True, add=dh, comm=comm, tail=tail)
        if late is not None and idx == 0:
            dh, *arrived = dh
            core = lax.axis_index("c")
            own = [lax.dynamic_index_in_dim(s.reshape((N_CHIPS, 2) + s.shape[1:]), core, axis=1, keepdims=False)
                   for s in slabs]
            summed = [pair_add(a_, b_, f"rs_add_{n}") for n, a_, b_ in zip(EARLY, own, arrived)]
    grad_x, grads["norm1_g"], *arrived = dh
    grads.update(zip(EARLY, arrived))
    return sq, grad_x.reshape(n_seq, seq, D_MODEL), grads


EARLY = ("w_in", "ssd_conv_w")
REPLICATED = ("norm1_g", "ssd_conv_b", "dt_bias", "a_log", "d_skip", "ssd_norm_g", "q_norm_g", "k_norm_g",
              "norm2_g", "ffn_conv_b")
WEIGHTS = ("norm1_g", "w_in", "ssd_conv_w", "ssd_conv_b", "dt_bias", "a_log", "d_skip", "ssd_norm_g", "w_ssd_proj",
           "q_norm_g", "k_norm_g", "w_attn_proj", "w_out", "norm2_g", "w_up", "ffn_conv_w", "ffn_conv_b", "w_down")
PACK_ROWS, PACK_COLS = 8, 2048


def _pack(vals):
    flat = jnp.concatenate([vals[n].reshape(-1) for n in REPLICATED])
    return jnp.pad(flat, (0, PACK_ROWS * PACK_COLS - flat.shape[0])).reshape(PACK_ROWS, PACK_COLS)


def _unpack(packed, like):
    flat = packed.reshape(-1)
    out, pos = {}, 0
    for n in REPLICATED:
        size = like[n].size
        out[n] = flat[pos:pos + size].reshape(like[n].shape)
        pos += size
    return out


def step(x, target, w_raw, m_raw, v_raw):
    wsh = {n: a[0] if a.ndim == 3 else a for n, a in w_raw.items()}
    gathered = all_gather([_narrow(n, wsh[n]) for n in EARLY], "ag_weights")
    full = {n: wsh[n] for n in REPLICATED}
    full.update({n: g if n == "w_in" else _from_gathered(n, g) for n, g in zip(EARLY, gathered)})

    sq, grad_x, grads = local_step(x, target, full, late={n: _narrow(n, wsh[n]) for n in LATE})

    (small,) = all_gather([_pack({n: grads[n] for n in REPLICATED})], "ag_small")

    out_g, out_d, out_m, out_v = {}, {}, {}, {}
    for n in EARLY + LATE:
        out_g[n], out_d[n], out_m[n], out_v[n] = adamw(grads[n], w_raw[n], m_raw[n], v_raw[n], f"adamw_{n}")
    pk = adamw(small, _pack(w_raw), _pack(m_raw), _pack(v_raw), "adamw_small")
    for dst, packed in zip((out_g, out_d, out_m, out_v), pk):
        dst.update(_unpack(packed, w_raw))
    loss = lax.psum(0.5 * jnp.sum(sq) / D_MODEL, ("x", "y", "c"))
    return loss, grad_x, out_g, out_d, out_m, out_v


def kernel(x, norm1_g, w_in, ssd_conv_w, ssd_conv_b, dt_bias, a_log, d_skip, ssd_norm_g, w_ssd_proj, q_norm_g, k_norm_g, w_attn_proj, w_out, norm2_g, w_up, ffn_conv_w, ffn_conv_b, w_down, loss_target, m_norm1_g, m_w_in, m_ssd_conv_w, m_ssd_conv_b, m_dt_bias, m_a_log, m_d_skip, m_ssd_norm_g, m_w_ssd_proj, m_q_norm_g, m_k_norm_g, m_w_attn_proj, m_w_out, m_norm2_g, m_w_up, m_ffn_conv_w, m_ffn_conv_b, m_w_down, v_norm1_g, v_w_in, v_ssd_conv_w, v_ssd_conv_b, v_dt_bias, v_a_log, v_d_skip, v_ssd_norm_g, v_w_ssd_proj, v_q_norm_g, v_k_norm_g, v_w_attn_proj, v_w_out, v_norm2_g, v_w_up, v_ffn_conv_w, v_ffn_conv_b, v_w_down):
    ws = (norm1_g, w_in, ssd_conv_w, ssd_conv_b, dt_bias, a_log, d_skip, ssd_norm_g, w_ssd_proj, q_norm_g, k_norm_g,
          w_attn_proj, w_out, norm2_g, w_up, ffn_conv_w, ffn_conv_b, w_down)
    ms = (m_norm1_g, m_w_in, m_ssd_conv_w, m_ssd_conv_b, m_dt_bias, m_a_log, m_d_skip, m_ssd_norm_g, m_w_ssd_proj,
          m_q_norm_g, m_k_norm_g, m_w_attn_proj, m_w_out, m_norm2_g, m_w_up, m_ffn_conv_w, m_ffn_conv_b, m_w_down)
    vs = (v_norm1_g, v_w_in, v_ssd_conv_w, v_ssd_conv_b, v_dt_bias, v_a_log, v_d_skip, v_ssd_norm_g, v_w_ssd_proj,
          v_q_norm_g, v_k_norm_g, v_w_attn_proj, v_w_out, v_norm2_g, v_w_up, v_ffn_conv_w, v_ffn_conv_b, v_w_down)
    loss, grad_x, g, d, m, v = step(x, loss_target, dict(zip(WEIGHTS, ws)), dict(zip(WEIGHTS, ms)), dict(zip(WEIGHTS, vs)))
    ordered = lambda dct: [dct[n] for n in WEIGHTS]
    return (loss, grad_x, *ordered(g), *ordered(d), *ordered(m), *ordered(v))
```
